```python
import math
import jax, jax.numpy as jnp
from jax import lax
import numpy as np

D_MODEL = 1024
BATCH = 8
SEQ = 2048
DEPTH = 1

HEAD_DIM = 64
D_ATTN = D_MODEL // 2
D_RWKV = D_MODEL - D_ATTN
N_Q_HEADS = D_ATTN // HEAD_DIM
N_KV_HEADS = max(1, N_Q_HEADS // 4)
Q_PER_KV = N_Q_HEADS // N_KV_HEADS
D_KV = N_KV_HEADS * HEAD_DIM
WINDOW = 128
BLOCK = 128
N_BUCKETS = 32
MAX_DISTANCE = 128
N_RWKV_HEADS = D_RWKV // HEAD_DIM
LORA_DECAY = 64
LORA_ICLR = 64
LORA_GATE = 128
RWKV_COLS = 3 * D_RWKV + LORA_DECAY + LORA_ICLR + LORA_GATE
RWKV_SPLITS = (D_RWKV, 2 * D_RWKV, 3 * D_RWKV, 3 * D_RWKV + LORA_DECAY, 3 * D_RWKV + LORA_DECAY + LORA_ICLR)
D_IN = D_ATTN + 2 * D_KV + RWKV_COLS
D_FF = 4 * D_MODEL
CONV_WIDTH = 3
NORM_EPS = 1e-6
GN_EPS = 64e-5
NEG_INF = -1e30

kernel_name = "hymba_swa_sink_rwkv7_convffn_sandwich"


def rms_norm(x, g):
    xf = x.astype(jnp.float32)
    y = xf * lax.rsqrt(jnp.mean(xf * xf, axis=-1, keepdims=True) + NORM_EPS) * g.astype(jnp.float32)
    return y.astype(x.dtype)


def t5_bucket(rel):
    n = jnp.maximum(rel, 0)
    max_exact = N_BUCKETS // 2
    large = max_exact + (jnp.log(jnp.maximum(n, 1).astype(jnp.float32) / max_exact)
                         / math.log(MAX_DISTANCE / max_exact) * (N_BUCKETS - max_exact)).astype(jnp.int32)
    large = jnp.minimum(large, N_BUCKETS - 1)
    return jnp.where(n < max_exact, n, large)


def sliding_window_sink_attention(q, k, v, rel_bias, sinks):
    B, S, _ = q.shape
    NB = S // BLOCK
    q = q.reshape(B, NB, BLOCK, N_KV_HEADS, Q_PER_KV, HEAD_DIM)

    def band(t):
        t = t.reshape(B, NB, BLOCK, N_KV_HEADS, HEAD_DIM)
        prev = jnp.concatenate([jnp.zeros_like(t[:, :1]), t[:, :-1]], axis=1)
        return jnp.concatenate([prev, t], axis=2)

    kb, vb = band(k), band(v)
    rel = (jnp.arange(BLOCK)[:, None] + BLOCK) - jnp.arange(2 * BLOCK)[None, :]
    in_window = (rel >= 0) & (rel < WINDOW)
    key_pos = (jnp.arange(NB)[:, None] - 1) * BLOCK + jnp.arange(2 * BLOCK)[None, :]
    mask = in_window[None] & (key_pos >= 0)[:, None, :]
    bias = rel_bias.astype(jnp.float32)[t5_bucket(rel)]
    bias = bias.transpose(2, 0, 1).reshape(N_KV_HEADS, Q_PER_KV, BLOCK, 2 * BLOCK)
    scores = jnp.einsum('bnqhgd,bnkhd->bnhgqk', q, kb).astype(jnp.float32) * (HEAD_DIM ** -0.5) + bias
    scores = jnp.where(mask[None, :, None, None], scores, NEG_INF)
    sink = sinks.astype(jnp.float32).reshape(N_KV_HEADS, Q_PER_KV)[:, :, None, None]
    m = jnp.maximum(scores.max(axis=-1, keepdims=True), sink)
    p = jnp.exp(scores - m)
    probs = p / (p.sum(axis=-1, keepdims=True) + jnp.exp(sink - m))
    out = jnp.einsum('bnhgqk,bnkhd->bnqhgd', probs.astype(v.dtype), vb)
    return out.reshape(B, S, D_ATTN)


def rwkv7_time_mix(p, w0, w_decay_up, a0, w_iclr_up, w_gate_up, k_k, k_a, r_k, ln_x_g, ln_x_b):
    B, S, _ = p.shape
    H, N = N_RWKV_HEADS, HEAD_DIM
    p = p.astype(jnp.float32)
    r, k, v, zw, za, zg = jnp.split(p, RWKV_SPLITS, axis=-1)
    w_log = -jax.nn.softplus(-(w0 + jnp.tanh(zw) @ w_decay_up)) - 0.5
    decay = jnp.exp(-jnp.exp(w_log))
    a = jax.nn.sigmoid(a0 + za @ w_iclr_up)
    g = jax.nn.sigmoid(zg) @ w_gate_up
    kk = (k * k_k).reshape(B, S, H, N)
    kk = kk / jnp.maximum(jnp.sqrt(jnp.sum(kk * kk, axis=-1, keepdims=True)), 1e-12)
    k = k * (1.0 + (a - 1.0) * k_a)

    def heads(t):
        return t.reshape(B, S, H, N)

    def tmaj(t):
        return t.swapaxes(0, 1)

    rh, kh, vh = heads(r), heads(k), heads(v)

    def step(state, inp):
        r_t, w_t, k_t, v_t, kk_t, a_t = inp
        sa = jnp.einsum('bhvk,bhk->bhv', state, -kk_t)
        state = (state * w_t[:, :, None, :]
                 + sa[..., None] * (kk_t * a_t)[:, :, None, :]
                 + v_t[..., None] * k_t[:, :, None, :])
        return state, jnp.einsum('bhvk,bhk->bhv', state, r_t)

    state0 = jnp.zeros((B, H, N, N), jnp.float32)
    _, o = lax.scan(step, state0, (tmaj(rh), tmaj(heads(decay)), tmaj(kh), tmaj(vh), tmaj(kk), tmaj(heads(a))))
    o = o.swapaxes(0, 1)
    mu = jnp.mean(o, axis=-1, keepdims=True)
    var = jnp.mean(jnp.square(o - mu), axis=-1, keepdims=True)
    o = ((o - mu) * lax.rsqrt(var + GN_EPS)).reshape(B, S, D_RWKV) * ln_x_g + ln_x_b
    bonus = jnp.sum(rh * kh * r_k, axis=-1, keepdims=True) * vh
    o = o + bonus.reshape(B, S, D_RWKV)
    return o * g


def conv_gated_ffn(h, w_up, conv_w, conv_b, w_down):
    S = h.shape[1]
    u = h @ w_up
    u_pad = jnp.pad(u, ((0, 0), (CONV_WIDTH - 1, 0), (0, 0)))
    u = conv_b + sum(conv_w[j] * u_pad[:, j:j + S] for j in range(CONV_WIDTH))
    gate, val = jnp.split(u, 2, axis=-1)
    return (jax.nn.gelu(gate, approximate=True) * val) @ w_down


def _fwd_setup_inputs(seed: int = 0) -> dict:
    key = jax.random.key(seed)
    ks = jax.random.split(key, 24)
    L = DEPTH
    nrm = lambda k, shape, s: jax.random.normal(k, shape, jnp.float32) * s
    return {
        "x": jax.random.normal(ks[0], (BATCH, SEQ, D_MODEL), jnp.float32),
        "norm_mix_pre": 1.0 + nrm(ks[1], (L, D_MODEL), 0.02),
        "norm_mix_post": 1.0 + nrm(ks[2], (L, D_MODEL), 0.02),
        "norm_ffn_pre": 1.0 + nrm(ks[3], (L, D_MODEL), 0.02),
        "norm_ffn_post": 1.0 + nrm(ks[4], (L, D_MODEL), 0.02),
        "w_in": nrm(ks[5], (L, D_MODEL, D_IN), D_MODEL ** -0.5),
        "rel_bias": nrm(ks[6], (N_BUCKETS, N_Q_HEADS), 0.5),
        "sinks": nrm(ks[7], (L, N_Q_HEADS), 0.5),
        "rwkv_shift_mix": jax.random.uniform(ks[8], (L, RWKV_COLS), jnp.float32),
        "w0": jax.random.uniform(ks[9], (L, D_RWKV), jnp.float32, -5.0, 0.0),
        "w_decay_up": nrm(ks[10], (L, LORA_DECAY, D_RWKV), 0.5 * LORA_DECAY ** -0.5),
        "a0": nrm(ks[11], (L, D_RWKV), 0.1),
        "w_iclr_up": nrm(ks[12], (L, LORA_ICLR, D_RWKV), 0.5 * LORA_ICLR ** -0.5),
        "w_gate_up": nrm(ks[13], (L, LORA_GATE, D_RWKV), LORA_GATE ** -0.5),
        "k_k": 0.85 + nrm(ks[14], (L, D_RWKV), 0.05),
        "k_a": 1.0 + nrm(ks[15], (L, D_RWKV), 0.05),
        "r_k": nrm(ks[16], (L, N_RWKV_HEADS, HEAD_DIM), 0.1),
        "ln_x_g": 1.0 + nrm(ks[17], (L, D_RWKV), 0.02),
        "ln_x_b": nrm(ks[18], (L, D_RWKV), 0.02),
        "w_out": nrm(ks[19], (L, D_MODEL, D_MODEL), D_MODEL ** -0.5),
        "w_ffn_up": nrm(ks[20], (L, D_MODEL, 2 * D_FF), D_MODEL ** -0.5),
        "conv_w": nrm(ks[21], (L, CONV_WIDTH, 2 * D_FF), CONV_WIDTH ** -0.5),
        "conv_b": nrm(ks[22], (L, 2 * D_FF), 0.02),
        "w_ffn_down": nrm(ks[23], (L, D_FF, D_MODEL), D_FF ** -0.5),
    }


def _fwd_reference(x, norm_mix_pre, norm_mix_post, norm_ffn_pre, norm_ffn_post, w_in, rel_bias, sinks,
              rwkv_shift_mix, w0, w_decay_up, a0, w_iclr_up, w_gate_up, k_k, k_a, r_k, ln_x_g, ln_x_b,
              w_out, w_ffn_up, conv_w, conv_b, w_ffn_down):
    for l in range(DEPTH):
        h = rms_norm(x, norm_mix_pre[l])
        proj = h @ w_in[l]
        q, k, v, p = jnp.split(proj, (D_ATTN, D_ATTN + D_KV, D_ATTN + 2 * D_KV), axis=-1)
        attn = sliding_window_sink_attention(q, k, v, rel_bias, sinks[l])
        p_prev = jnp.concatenate([jnp.zeros_like(p[:, :1]), p[:, :-1]], axis=1)
        p = p + (p_prev - p) * rwkv_shift_mix[l]
        rw = rwkv7_time_mix(p, w0[l], w_decay_up[l], a0[l], w_iclr_up[l], w_gate_up[l],
                            k_k[l], k_a[l], r_k[l], ln_x_g[l], ln_x_b[l])
        mix = jnp.concatenate([attn, rw.astype(x.dtype)], axis=-1) @ w_out[l]
        x = x + rms_norm(mix, norm_mix_post[l])
        f = conv_gated_ffn(rms_norm(x, norm_ffn_pre[l]), w_ffn_up[l], conv_w[l], conv_b[l], w_ffn_down[l])
        x = x + rms_norm(f, norm_ffn_post[l])
    return x


import jax as _jax
import jax.numpy as _jnp

TWIN_FORMAT = 'train_step'
FWD_PARAMS = ['x', 'norm_mix_pre', 'norm_mix_post', 'norm_ffn_pre', 'norm_ffn_post', 'w_in', 'rel_bias', 'sinks', 'rwkv_shift_mix', 'w0', 'w_decay_up', 'a0', 'w_iclr_up', 'w_gate_up', 'k_k', 'k_a', 'r_k', 'ln_x_g', 'ln_x_b', 'w_out', 'w_ffn_up', 'conv_w', 'conv_b', 'w_ffn_down']
TWIN_WEIGHTS = ['norm_mix_pre', 'norm_mix_post', 'norm_ffn_pre', 'norm_ffn_post', 'w_in', 'rel_bias', 'sinks', 'rwkv_shift_mix', 'w0', 'w_decay_up', 'a0', 'w_iclr_up', 'w_gate_up', 'k_k', 'k_a', 'r_k', 'ln_x_g', 'ln_x_b', 'w_out', 'w_ffn_up', 'conv_w', 'conv_b', 'w_ffn_down']
TWIN_DIFF_INPUT = 'x'
TWIN_INPUTS = ['x', 'norm_mix_pre', 'norm_mix_post', 'norm_ffn_pre', 'norm_ffn_post', 'w_in', 'rel_bias', 'sinks', 'rwkv_shift_mix', 'w0', 'w_decay_up', 'a0', 'w_iclr_up', 'w_gate_up', 'k_k', 'k_a', 'r_k', 'ln_x_g', 'ln_x_b', 'w_out', 'w_ffn_up', 'conv_w', 'conv_b', 'w_ffn_down', 'loss_target', 'm_norm_mix_pre', 'm_norm_mix_post', 'm_norm_ffn_pre', 'm_norm_ffn_post', 'm_w_in', 'm_rel_bias', 'm_sinks', 'm_rwkv_shift_mix', 'm_w0', 'm_w_decay_up', 'm_a0', 'm_w_iclr_up', 'm_w_gate_up', 'm_k_k', 'm_k_a', 'm_r_k', 'm_ln_x_g', 'm_ln_x_b', 'm_w_out', 'm_w_ffn_up', 'm_conv_w', 'm_conv_b', 'm_w_ffn_down', 'v_norm_mix_pre', 'v_norm_mix_post', 'v_norm_ffn_pre', 'v_norm_ffn_post', 'v_w_in', 'v_rel_bias', 'v_sinks', 'v_rwkv_shift_mix', 'v_w0', 'v_w_decay_up', 'v_a0', 'v_w_iclr_up', 'v_w_gate_up', 'v_k_k', 'v_k_a', 'v_r_k', 'v_ln_x_g', 'v_ln_x_b', 'v_w_out', 'v_w_ffn_up', 'v_conv_w', 'v_conv_b', 'v_w_ffn_down']
TWIN_OUTPUTS = ['loss', 'grad_x', 'grad_norm_mix_pre', 'grad_norm_mix_post', 'grad_norm_ffn_pre', 'grad_norm_ffn_post', 'grad_w_in', 'grad_rel_bias', 'grad_sinks', 'grad_rwkv_shift_mix', 'grad_w0', 'grad_w_decay_up', 'grad_a0', 'grad_w_iclr_up', 'grad_w_gate_up', 'grad_k_k', 'grad_k_a', 'grad_r_k', 'grad_ln_x_g', 'grad_ln_x_b', 'grad_w_out', 'grad_w_ffn_up', 'grad_conv_w', 'grad_conv_b', 'grad_w_ffn_down', 'delta_norm_mix_pre', 'delta_norm_mix_post', 'delta_norm_ffn_pre', 'delta_norm_ffn_post', 'delta_w_in', 'delta_rel_bias', 'delta_sinks', 'delta_rwkv_shift_mix', 'delta_w0', 'delta_w_decay_up', 'delta_a0', 'delta_w_iclr_up', 'delta_w_gate_up', 'delta_k_k', 'delta_k_a', 'delta_r_k', 'delta_ln_x_g', 'delta_ln_x_b', 'delta_w_out', 'delta_w_ffn_up', 'delta_conv_w', 'delta_conv_b', 'delta_w_ffn_down', 'new_m_norm_mix_pre', 'new_m_norm_mix_post', 'new_m_norm_ffn_pre', 'new_m_norm_ffn_post', 'new_m_w_in', 'new_m_rel_bias', 'new_m_sinks', 'new_m_rwkv_shift_mix', 'new_m_w0', 'new_m_w_decay_up', 'new_m_a0', 'new_m_w_iclr_up', 'new_m_w_gate_up', 'new_m_k_k', 'new_m_k_a', 'new_m_r_k', 'new_m_ln_x_g', 'new_m_ln_x_b', 'new_m_w_out', 'new_m_w_ffn_up', 'new_m_conv_w', 'new_m_conv_b', 'new_m_w_ffn_down', 'new_v_norm_mix_pre', 'new_v_norm_mix_post', 'new_v_norm_ffn_pre', 'new_v_norm_ffn_post', 'new_v_w_in', 'new_v_rel_bias', 'new_v_sinks', 'new_v_rwkv_shift_mix', 'new_v_w0', 'new_v_w_decay_up', 'new_v_a0', 'new_v_w_iclr_up', 'new_v_w_gate_up', 'new_v_k_k', 'new_v_k_a', 'new_v_r_k', 'new_v_ln_x_g', 'new_v_ln_x_b', 'new_v_w_out', 'new_v_w_ffn_up', 'new_v_conv_w', 'new_v_conv_b', 'new_v_w_ffn_down']
TWIN_LEAF_KINDS = {'loss': 'loss', 'grad_x': 'grad_x', 'grad_norm_mix_pre': 'grad_w', 'grad_norm_mix_post': 'grad_w', 'grad_norm_ffn_pre': 'grad_w', 'grad_norm_ffn_post': 'grad_w', 'grad_w_in': 'grad_w', 'grad_rel_bias': 'grad_w', 'grad_sinks': 'grad_w', 'grad_rwkv_shift_mix': 'grad_w', 'grad_w0': 'grad_w', 'grad_w_decay_up': 'grad_w', 'grad_a0': 'grad_w', 'grad_w_iclr_up': 'grad_w', 'grad_w_gate_up': 'grad_w', 'grad_k_k': 'grad_w', 'grad_k_a': 'grad_w', 'grad_r_k': 'grad_w', 'grad_ln_x_g': 'grad_w', 'grad_ln_x_b': 'grad_w', 'grad_w_out': 'grad_w', 'grad_w_ffn_up': 'grad_w', 'grad_conv_w': 'grad_w', 'grad_conv_b': 'grad_w', 'grad_w_ffn_down': 'grad_w', 'delta_norm_mix_pre': 'delta_w', 'delta_norm_mix_post': 'delta_w', 'delta_norm_ffn_pre': 'delta_w', 'delta_norm_ffn_post': 'delta_w', 'delta_w_in': 'delta_w', 'delta_rel_bias': 'delta_w', 'delta_sinks': 'delta_w', 'delta_rwkv_shift_mix': 'delta_w', 'delta_w0': 'delta_w', 'delta_w_decay_up': 'delta_w', 'delta_a0': 'delta_w', 'delta_w_iclr_up': 'delta_w', 'delta_w_gate_up': 'delta_w', 'delta_k_k': 'delta_w', 'delta_k_a': 'delta_w', 'delta_r_k': 'delta_w', 'delta_ln_x_g': 'delta_w', 'delta_ln_x_b': 'delta_w', 'delta_w_out': 'delta_w', 'delta_w_ffn_up': 'delta_w', 'delta_conv_w': 'delta_w', 'delta_conv_b': 'delta_w', 'delta_w_ffn_down': 'delta_w', 'new_m_norm_mix_pre': 'new_m', 'new_m_norm_mix_post': 'new_m', 'new_m_norm_ffn_pre': 'new_m', 'new_m_norm_ffn_post': 'new_m', 'new_m_w_in': 'new_m', 'new_m_rel_bias': 'new_m', 'new_m_sinks': 'new_m', 'new_m_rwkv_shift_mix': 'new_m', 'new_m_w0': 'new_m', 'new_m_w_decay_up': 'new_m', 'new_m_a0': 'new_m', 'new_m_w_iclr_up': 'new_m', 'new_m_w_gate_up': 'new_m', 'new_m_k_k': 'new_m', 'new_m_k_a': 'new_m', 'new_m_r_k': 'new_m', 'new_m_ln_x_g': 'new_m', 'new_m_ln_x_b': 'new_m', 'new_m_w_out': 'new_m', 'new_m_w_ffn_up': 'new_m', 'new_m_conv_w': 'new_m', 'new_m_conv_b': 'new_m', 'new_m_w_ffn_down': 'new_m', 'new_v_norm_mix_pre': 'new_v', 'new_v_norm_mix_post': 'new_v', 'new_v_norm_ffn_pre': 'new_v', 'new_v_norm_ffn_post': 'new_v', 'new_v_w_in': 'new_v', 'new_v_rel_bias': 'new_v', 'new_v_sinks': 'new_v', 'new_v_rwkv_shift_mix': 'new_v', 'new_v_w0': 'new_v', 'new_v_w_decay_up': 'new_v', 'new_v_a0': 'new_v', 'new_v_w_iclr_up': 'new_v', 'new_v_w_gate_up': 'new_v', 'new_v_k_k': 'new_v', 'new_v_k_a': 'new_v', 'new_v_r_k': 'new_v', 'new_v_ln_x_g': 'new_v', 'new_v_ln_x_b': 'new_v', 'new_v_w_out': 'new_v', 'new_v_w_ffn_up': 'new_v', 'new_v_conv_w': 'new_v', 'new_v_conv_b': 'new_v', 'new_v_w_ffn_down': 'new_v'}


def _forward(args):
    return _fwd_reference(*[args[k] for k in FWD_PARAMS])


def _output_shape():
    out = _jax.eval_shape(lambda: _forward(_fwd_setup_inputs(0)))
    return out.shape, out.dtype

N_MICROBATCH = 1
ADAM_LR = 0.001
ADAM_B1 = 0.9
ADAM_B2 = 0.999
ADAM_EPS = 1e-08
ADAM_WD = 0.01
ADAM_STEP = 10
PER_EXAMPLE_BATCH_AXIS = {'x': 0, 'loss_target': 0}
SHARED_INPUTS = []
_WEIGHT_DTYPES = {'norm_mix_pre': _jnp.float32, 'norm_mix_post': _jnp.float32, 'norm_ffn_pre': _jnp.float32, 'norm_ffn_post': _jnp.float32, 'w_in': _jnp.float32, 'rel_bias': _jnp.float32, 'sinks': _jnp.float32, 'rwkv_shift_mix': _jnp.float32, 'w0': _jnp.float32, 'w_decay_up': _jnp.float32, 'a0': _jnp.float32, 'w_iclr_up': _jnp.float32, 'w_gate_up': _jnp.float32, 'k_k': _jnp.float32, 'k_a': _jnp.float32, 'r_k': _jnp.float32, 'ln_x_g': _jnp.float32, 'ln_x_b': _jnp.float32, 'w_out': _jnp.float32, 'w_ffn_up': _jnp.float32, 'conv_w': _jnp.float32, 'conv_b': _jnp.float32, 'w_ffn_down': _jnp.float32}
MOMENT_SCALE = {'norm_mix_pre': 5.754468e-01, 'norm_mix_post': 1.603019e+01, 'norm_ffn_pre': 3.884047e-01, 'norm_ffn_post': 1.599063e+01, 'w_in': 3.691716e-01, 'rel_bias': 2.044896e-01, 'sinks': 1.307057e-01, 'rwkv_shift_mix': 6.995429e-01, 'w0': 2.047346e-01, 'w_decay_up': 2.750808e-02, 'a0': 1.704107e-01, 'w_iclr_up': 1.614514e-01, 'w_gate_up': 5.450035e-01, 'k_k': 3.721396e-01, 'k_a': 4.587555e-01, 'r_k': 1.041058e+00, 'ln_x_g': 6.723486e-01, 'ln_x_b': 5.832385e-01, 'w_out': 3.758165e-01, 'w_ffn_up': 1.292257e-01, 'conv_w': 1.418402e-01, 'conv_b': 2.082904e-01, 'w_ffn_down': 2.933140e-01}


def _to_microbatches(a, axis):
    t = _jnp.moveaxis(a, axis, 0)
    t = t.reshape((N_MICROBATCH, t.shape[0] // N_MICROBATCH) + t.shape[1:])
    return _jnp.moveaxis(t, 1, axis + 1)


def setup_inputs(seed: int = 0) -> dict:
    inp = _fwd_setup_inputs(seed)
    key = _jax.random.fold_in(_jax.random.key(seed), 7919)
    shape, _ = _output_shape()
    out = dict(inp)
    out["loss_target"] = _jax.random.normal(_jax.random.fold_in(key, 0), shape, _jnp.float32)
    for i, name in enumerate(TWIN_WEIGHTS):
        w = inp[name].astype(_jnp.float32)
        if MOMENT_SCALE is None:
            s = _jnp.sqrt(_jnp.mean(_jnp.square(w)) + 1e-30)
        else:
            s = MOMENT_SCALE[name]
        km, kv = _jax.random.split(_jax.random.fold_in(key, i + 1))
        out[name] = w
        out["m_" + name] = s * _jax.random.normal(km, w.shape, _jnp.float32)
        out["v_" + name] = (s * s) * _jax.random.uniform(kv, w.shape, _jnp.float32, 0.5, 1.5)
    if N_MICROBATCH > 1:
        for name, axis in PER_EXAMPLE_BATCH_AXIS.items():
            out[name] = _to_microbatches(out[name], axis)
    return {'x': out['x'], 'norm_mix_pre': out['norm_mix_pre'], 'norm_mix_post': out['norm_mix_post'], 'norm_ffn_pre': out['norm_ffn_pre'], 'norm_ffn_post': out['norm_ffn_post'], 'w_in': out['w_in'], 'rel_bias': out['rel_bias'], 'sinks': out['sinks'], 'rwkv_shift_mix': out['rwkv_shift_mix'], 'w0': out['w0'], 'w_decay_up': out['w_decay_up'], 'a0': out['a0'], 'w_iclr_up': out['w_iclr_up'], 'w_gate_up': out['w_gate_up'], 'k_k': out['k_k'], 'k_a': out['k_a'], 'r_k': out['r_k'], 'ln_x_g': out['ln_x_g'], 'ln_x_b': out['ln_x_b'], 'w_out': out['w_out'], 'w_ffn_up': out['w_ffn_up'], 'conv_w': out['conv_w'], 'conv_b': out['conv_b'], 'w_ffn_down': out['w_ffn_down'], 'loss_target': out['loss_target'], 'm_norm_mix_pre': out['m_norm_mix_pre'], 'm_norm_mix_post': out['m_norm_mix_post'], 'm_norm_ffn_pre': out['m_norm_ffn_pre'], 'm_norm_ffn_post': out['m_norm_ffn_post'], 'm_w_in': out['m_w_in'], 'm_rel_bias': out['m_rel_bias'], 'm_sinks': out['m_sinks'], 'm_rwkv_shift_mix': out['m_rwkv_shift_mix'], 'm_w0': out['m_w0'], 'm_w_decay_up': out['m_w_decay_up'], 'm_a0': out['m_a0'], 'm_w_iclr_up': out['m_w_iclr_up'], 'm_w_gate_up': out['m_w_gate_up'], 'm_k_k': out['m_k_k'], 'm_k_a': out['m_k_a'], 'm_r_k': out['m_r_k'], 'm_ln_x_g': out['m_ln_x_g'], 'm_ln_x_b': out['m_ln_x_b'], 'm_w_out': out['m_w_out'], 'm_w_ffn_up': out['m_w_ffn_up'], 'm_conv_w': out['m_conv_w'], 'm_conv_b': out['m_conv_b'], 'm_w_ffn_down': out['m_w_ffn_down'], 'v_norm_mix_pre': out['v_norm_mix_pre'], 'v_norm_mix_post': out['v_norm_mix_post'], 'v_norm_ffn_pre': out['v_norm_ffn_pre'], 'v_norm_ffn_post': out['v_norm_ffn_post'], 'v_w_in': out['v_w_in'], 'v_rel_bias': out['v_rel_bias'], 'v_sinks': out['v_sinks'], 'v_rwkv_shift_mix': out['v_rwkv_shift_mix'], 'v_w0': out['v_w0'], 'v_w_decay_up': out['v_w_decay_up'], 'v_a0': out['v_a0'], 'v_w_iclr_up': out['v_w_iclr_up'], 'v_w_gate_up': out['v_w_gate_up'], 'v_k_k': out['v_k_k'], 'v_k_a': out['v_k_a'], 'v_r_k': out['v_r_k'], 'v_ln_x_g': out['v_ln_x_g'], 'v_ln_x_b': out['v_ln_x_b'], 'v_w_out': out['v_w_out'], 'v_w_ffn_up': out['v_w_ffn_up'], 'v_conv_w': out['v_conv_w'], 'v_conv_b': out['v_conv_b'], 'v_w_ffn_down': out['v_w_ffn_down']}


def _loss(weights, diff, rest, loss_target):
    with _jax.named_scope("forward"):
        args = {**rest, TWIN_DIFF_INPUT: diff, **{k: w.astype(_WEIGHT_DTYPES[k]) for k, w in weights.items()}}
        y = _forward(args)
    with _jax.named_scope("loss_head"):
        err = _jnp.square(y.astype(_jnp.float32) - loss_target)
        return 0.5 * _jnp.sum(_jnp.mean(err, axis=-1)) if err.ndim else 0.5 * err


def _adamw(w, g, m, v):
    m = ADAM_B1 * m + (1.0 - ADAM_B1) * g
    v = ADAM_B2 * v + (1.0 - ADAM_B2) * _jnp.square(g)
    m_hat = m / (1.0 - ADAM_B1 ** ADAM_STEP)
    v_hat = v / (1.0 - ADAM_B2 ** ADAM_STEP)
    delta = -ADAM_LR * (m_hat / (_jnp.sqrt(v_hat) + ADAM_EPS) + ADAM_WD * w)
    return delta, m, v


def reference(x, norm_mix_pre, norm_mix_post, norm_ffn_pre, norm_ffn_post, w_in, rel_bias, sinks, rwkv_shift_mix, w0, w_decay_up, a0, w_iclr_up, w_gate_up, k_k, k_a, r_k, ln_x_g, ln_x_b, w_out, w_ffn_up, conv_w, conv_b, w_ffn_down, loss_target, m_norm_mix_pre, m_norm_mix_post, m_norm_ffn_pre, m_norm_ffn_post, m_w_in, m_rel_bias, m_sinks, m_rwkv_shift_mix, m_w0, m_w_decay_up, m_a0, m_w_iclr_up, m_w_gate_up, m_k_k, m_k_a, m_r_k, m_ln_x_g, m_ln_x_b, m_w_out, m_w_ffn_up, m_conv_w, m_conv_b, m_w_ffn_down, v_norm_mix_pre, v_norm_mix_post, v_norm_ffn_pre, v_norm_ffn_post, v_w_in, v_rel_bias, v_sinks, v_rwkv_shift_mix, v_w0, v_w_decay_up, v_a0, v_w_iclr_up, v_w_gate_up, v_k_k, v_k_a, v_r_k, v_ln_x_g, v_ln_x_b, v_w_out, v_w_ffn_up, v_conv_w, v_conv_b, v_w_ffn_down):
    given = dict(x=x, norm_mix_pre=norm_mix_pre, norm_mix_post=norm_mix_post, norm_ffn_pre=norm_ffn_pre, norm_ffn_post=norm_ffn_post, w_in=w_in, rel_bias=rel_bias, sinks=sinks, rwkv_shift_mix=rwkv_shift_mix, w0=w0, w_decay_up=w_decay_up, a0=a0, w_iclr_up=w_iclr_up, w_gate_up=w_gate_up, k_k=k_k, k_a=k_a, r_k=r_k, ln_x_g=ln_x_g, ln_x_b=ln_x_b, w_out=w_out, w_ffn_up=w_ffn_up, conv_w=conv_w, conv_b=conv_b, w_ffn_down=w_ffn_down, loss_target=loss_target, m_norm_mix_pre=m_norm_mix_pre, m_norm_mix_post=m_norm_mix_post, m_norm_ffn_pre=m_norm_ffn_pre, m_norm_ffn_post=m_norm_ffn_post, m_w_in=m_w_in, m_rel_bias=m_rel_bias, m_sinks=m_sinks, m_rwkv_shift_mix=m_rwkv_shift_mix, m_w0=m_w0, m_w_decay_up=m_w_decay_up, m_a0=m_a0, m_w_iclr_up=m_w_iclr_up, m_w_gate_up=m_w_gate_up, m_k_k=m_k_k, m_k_a=m_k_a, m_r_k=m_r_k, m_ln_x_g=m_ln_x_g, m_ln_x_b=m_ln_x_b, m_w_out=m_w_out, m_w_ffn_up=m_w_ffn_up, m_conv_w=m_conv_w, m_conv_b=m_conv_b, m_w_ffn_down=m_w_ffn_down, v_norm_mix_pre=v_norm_mix_pre, v_norm_mix_post=v_norm_mix_post, v_norm_ffn_pre=v_norm_ffn_pre, v_norm_ffn_post=v_norm_ffn_post, v_w_in=v_w_in, v_rel_bias=v_rel_bias, v_sinks=v_sinks, v_rwkv_shift_mix=v_rwkv_shift_mix, v_w0=v_w0, v_w_decay_up=v_w_decay_up, v_a0=v_a0, v_w_iclr_up=v_w_iclr_up, v_w_gate_up=v_w_gate_up, v_k_k=v_k_k, v_k_a=v_k_a, v_r_k=v_r_k, v_ln_x_g=v_ln_x_g, v_ln_x_b=v_ln_x_b, v_w_out=v_w_out, v_w_ffn_up=v_w_ffn_up, v_conv_w=v_conv_w, v_conv_b=v_conv_b, v_w_ffn_down=v_w_ffn_down)
    weights = {n: given[n] for n in TWIN_WEIGHTS}
    shared = {n: given[n] for n in SHARED_INPUTS}
    per_example = {n: given[n] for n in ['x']}
    grad_fn = _jax.value_and_grad(_loss, argnums=(0, 1))

    def one_microbatch(ex, loss_target):
        ex = dict(ex)
        diff = ex.pop(TWIN_DIFF_INPUT)
        return grad_fn(weights, diff, {**shared, **ex}, loss_target)

    if N_MICROBATCH == 1:
        loss, (grad_w, grad_x) = one_microbatch(per_example, given["loss_target"])
    else:
        def body(carry, xs):
            loss_sum, grad_sum = carry
            l_k, (gw_k, gx_k) = one_microbatch(xs[0], xs[1])
            with _jax.named_scope("update"):
                return (loss_sum + l_k, _jax.tree.map(_jnp.add, grad_sum, gw_k)), gx_k

        init = (_jnp.zeros((), _jnp.float32), _jax.tree.map(_jnp.zeros_like, weights))
        (loss, grad_w), grad_x = _jax.lax.scan(body, init, (per_example, given["loss_target"]))
    with _jax.named_scope("update"):
        delta_w, new_m, new_v = {}, {}, {}
        for n in TWIN_WEIGHTS:
            delta_w[n], new_m[n], new_v[n] = _adamw(weights[n], grad_w[n], given["m_" + n], given["v_" + n])
    return (loss, grad_x, *[grad_w[n] for n in TWIN_WEIGHTS], *[delta_w[n] for n in TWIN_WEIGHTS],
            *[new_m[n] for n in TWIN_WEIGHTS], *[new_v[n] for n in TWIN_WEIGHTS])
```

```python
import functools
import math

import jax
import jax.numpy as jnp
from jax import lax
from jax.experimental import pallas as pl
from jax.experimental.pallas import tpu as pltpu

F32 = jnp.float32
BF16 = jnp.bfloat16
HI = lax.Precision.HIGHEST

N_DEV = 8
SEQ = 2048
D_MODEL = 1024
HEAD_DIM = 64
D_ATTN = 512
D_KV = 128
D_RWKV = 512
N_HEADS = 8
RWKV_COLS = 1792
D_QKV = D_ATTN + 2 * D_KV
D_IN = D_QKV + RWKV_COLS
D_FF = 4096
BLOCK = 128
N_BLOCKS = SEQ // BLOCK
N_BUCKETS = 32
MAX_DISTANCE = 128
NORM_EPS = 1e-6
GN_EPS = 64e-5
NEG_INF = -1e30
CHUNK = 64
N_CHUNKS = SEQ // CHUNK
TOK_TILE = 256
FF_TILE = 256
FF_TILE_BWD = 128
COL_TILE = 256
LANES = 128
VMEM_LIMIT = 56 * 1024 * 1024

ADAM_LR = 0.001
ADAM_B1 = 0.9
ADAM_B2 = 0.999
ADAM_EPS = 1e-08
ADAM_WD = 0.01
ADAM_STEP = 10

NT = ((1,), (1,))
TN = ((0,), (0,))
NN = ((1,), (0,))


def _sds(shape, dtype=F32):
    return jax.ShapeDtypeStruct(shape, dtype)


def _params(sem=None):
    if sem is None:
        return pltpu.CompilerParams(vmem_limit_bytes=VMEM_LIMIT)
    return pltpu.CompilerParams(dimension_semantics=sem, vmem_limit_bytes=VMEM_LIMIT)


def _dot(a, b, dims):
    return lax.dot_general(a, b, (dims, ((), ())), preferred_element_type=F32)


def dotx(a, b, dims=NN):
    return lax.dot_general(a, b, (dims, ((), ())), precision=HI, preferred_element_type=F32)


@jax.custom_vjp
def mm(a, b):
    return _dot(a.astype(BF16), b.astype(BF16), NN)


def _mm_fwd(a, b):
    return mm(a, b), (a, b)


def _mm_bwd(res, g):
    a, b = res
    gb = g.astype(BF16)
    return _dot(gb, b.astype(BF16), NT).astype(a.dtype), _dot(a.astype(BF16), gb, TN).astype(b.dtype)


mm.defvjp(_mm_fwd, _mm_bwd)


@jax.custom_vjp
def mm_nt(a, b):
    return _dot(a.astype(BF16), b.astype(BF16), NT)


def _mm_nt_fwd(a, b):
    return mm_nt(a, b), (a, b)


def _mm_nt_bwd(res, g):
    a, b = res
    gb = g.astype(BF16)
    return _dot(gb, b.astype(BF16), NN).astype(a.dtype), _dot(gb, a.astype(BF16), TN).astype(b.dtype)


mm_nt.defvjp(_mm_nt_fwd, _mm_nt_bwd)


@jax.custom_vjp
def mmw(a, w, wz):
    return _dot(a.astype(BF16), w, NN)


def _mmw_fwd(a, w, wz):
    return mmw(a, w, wz), (a, w)


def _mmw_bwd(res, g):
    a, w = res
    gb = g.astype(BF16)
    return _dot(gb, w, NT).astype(a.dtype), jnp.zeros_like(w), _dot(a.astype(BF16), gb, TN)


mmw.defvjp(_mmw_fwd, _mmw_bwd)


def _shift_raw(x, n):
    rows = x.shape[0]
    rolled = pltpu.roll(x, n % rows, 0)
    idx = lax.broadcasted_iota(jnp.int32, x.shape, 0)
    keep = idx >= n if n > 0 else idx < rows + n
    return jnp.where(keep, rolled, 0.0)


@functools.partial(jax.custom_vjp, nondiff_argnums=(1,))
def shift_rows(x, n):
    return _shift_raw(x, n)


def _shift_fwd(x, n):
    return _shift_raw(x, n), None


def _shift_bwd(n, _, g):
    return (_shift_raw(g, -n),)


shift_rows.defvjp(_shift_fwd, _shift_bwd)


def _head_matrix(scale):
    a = lax.broadcasted_iota(jnp.int32, (D_RWKV, D_RWKV), 0) // HEAD_DIM
    b = lax.broadcasted_iota(jnp.int32, (D_RWKV, D_RWKV), 1) // HEAD_DIM
    return jnp.where(a == b, scale, 0.0).astype(F32)


def _rms(x, g):
    return x * lax.rsqrt(jnp.mean(x * x, axis=-1, keepdims=True) + NORM_EPS) * g


def _softplus(x):
    return jnp.maximum(x, 0.0) + jnp.log(1.0 + jnp.exp(-jnp.abs(x)))


def _tile_spec(arr, tm):
    return pl.BlockSpec((tm, arr.shape[1]), lambda i: (i, 0))


def _full_spec(arr):
    nd = arr.ndim
    return pl.BlockSpec(arr.shape, lambda i: (0,) * nd)


def tok_fwd(name, fn, tiles, params, zero_shapes, out_widths, out_dtypes, tm=TOK_TILE):
    n_t, n_p = len(tiles), len(params)

    def body(*refs):
        t_vals = [r[...] for r in refs[:n_t]]
        p_vals = [r[...] for r in refs[n_t:n_t + n_p]]
        z_vals = [jnp.zeros(s, F32) for s in zero_shapes]
        outs = fn(*t_vals, *p_vals, *z_vals)
        for r, o in zip(refs[n_t + n_p:], outs):
            r[...] = o.astype(r.dtype)

    rows = tiles[0].shape[0]
    return pl.pallas_call(
        body, grid=(rows // tm,), name=name,
        in_specs=[_tile_spec(t, tm) for t in tiles] + [_full_spec(p) for p in params],
        out_specs=[pl.BlockSpec((tm, w), lambda i: (i, 0)) for w in out_widths],
        out_shape=[_sds((rows, w), dt) for w, dt in zip(out_widths, out_dtypes)],
        compiler_params=_params(("arbitrary",)),
    )(*tiles, *params)


def tok_bwd(name, fn, tiles, params, zero_shapes, cots, diff_params, tm=TOK_TILE):
    n_t, n_p, n_c = len(tiles), len(params), len(cots)
    acc_shapes = [params[i].shape for i in diff_params] + list(zero_shapes)

    def body(*refs):
        t_vals = [r[...].astype(F32) for r in refs[:n_t]]
        p_vals = [r[...] for r in refs[n_t:n_t + n_p]]
        c_vals = [r[...] for r in refs[n_t + n_p:n_t + n_p + n_c]]
        out_refs = refs[n_t + n_p + n_c:]
        z_vals = [jnp.zeros(s, F32) for s in zero_shapes]
        d_vals = [p_vals[i] for i in diff_params]

        def f(t_in, d_in, z_in):
            full = list(p_vals)
            for i, v in zip(diff_params, d_in):
                full[i] = v
            return tuple(fn(*t_in, *full, *z_in))

        _, vjp = jax.vjp(f, t_vals, d_vals, z_vals)
        g_t, g_d, g_z = vjp(tuple(c_vals))
        for r, g in zip(out_refs[:n_t], g_t):
            r[...] = g.astype(r.dtype)
        acc_refs = out_refs[n_t:]

        @pl.when(pl.program_id(0) == 0)
        def _():
            for r in acc_refs:
                r[...] = jnp.zeros_like(r)

        for r, g in zip(acc_refs, list(g_d) + list(g_z)):
            r[...] += g

    rows = tiles[0].shape[0]
    return pl.pallas_call(
        body, grid=(rows // tm,), name=name,
        in_specs=[_tile_spec(t, tm) for t in tiles] + [_full_spec(p) for p in params]
        + [_tile_spec(c, tm) for c in cots],
        out_specs=[_tile_spec(t, tm) for t in tiles]
        + [pl.BlockSpec(s, lambda i, nd=len(s): (0,) * nd) for s in acc_shapes],
        out_shape=[_sds(t.shape) for t in tiles] + [_sds(s) for s in acc_shapes],
        compiler_params=_params(("arbitrary",)),
    )(*tiles, *params, *cots)


def rms_tile(x, g):
    return (_rms(x, g),)


def rwkv_pre_tile(ps, w0, wd_pad, a0, wi_pad, wg, k_k, k_a):
    r = ps[:, 0:D_RWKV]
    k = ps[:, D_RWKV:2 * D_RWKV]
    v = ps[:, 2 * D_RWKV:3 * D_RWKV]
    z2 = ps[:, 3 * D_RWKV:3 * D_RWKV + LANES]
    zg = ps[:, 3 * D_RWKV + LANES:RWKV_COLS]
    w_log = -_softplus(-(w0 + mm(jnp.tanh(z2), wd_pad))) - 0.5
    lw = -jnp.exp(w_log)
    a = jax.nn.sigmoid(a0 + mm(z2, wi_pad))
    g = mm(jax.nn.sigmoid(zg), wg)
    kk = k * k_k
    norm = jnp.sqrt(dotx(kk * kk, _head_matrix(1.0)))
    kk = kk / jnp.maximum(norm, 1e-12)
    k2 = k * (1.0 + (a - 1.0) * k_a)
    return r, lw, k2, v, kk, a, g


def mix_out_tile(o, r, k2, v, g, attn, x, w_out, n_post, ln_g, ln_b, r_k, wz):
    hmean = _head_matrix(1.0 / HEAD_DIM)
    d = o - dotx(o, hmean)
    var = dotx(d * d, hmean)
    on = d * lax.rsqrt(var + GN_EPS) * ln_g + ln_b
    bonus = dotx(r * k2 * r_k, _head_matrix(1.0)) * v
    rw = (on + bonus) * g
    mix = mmw(jnp.concatenate([attn, rw], axis=1), w_out, wz)
    return (x + _rms(mix, n_post),)


def in_proj_fwd(h, w_in, mix_ext):
    def body(h_ref, w_ref, m_ref, proj_ref, ps_ref):
        p = _dot(h_ref[...], w_ref[...], NN)
        proj_ref[...] = p
        ps_ref[...] = p + (_shift_raw(p, 1) - p) * m_ref[...]

    n = D_IN // COL_TILE
    first = D_QKV // COL_TILE
    return pl.pallas_call(
        body, grid=(n,), name="in_proj_fwd",
        in_specs=[pl.BlockSpec((SEQ, D_MODEL), lambda j: (0, 0)),
                  pl.BlockSpec((D_MODEL, COL_TILE), lambda j: (0, j)),
                  pl.BlockSpec((1, COL_TILE), lambda j: (0, j))],
        out_specs=[pl.BlockSpec((SEQ, COL_TILE), lambda j: (0, j)),
                   pl.BlockSpec((SEQ, COL_TILE), lambda j: (0, jnp.maximum(j - first, 0)))],
        out_shape=[_sds((SEQ, D_IN)), _sds((SEQ, RWKV_COLS))],
        compiler_params=_params(("arbitrary",)),
    )(h, w_in, mix_ext)


def in_proj_bwd(h, w_in, mix_ext, proj, dpa):
    def body(h_ref, w_ref, m_ref, p_ref, d_ref, dh_ref, dw_ref, dm_ref):
        d = d_ref[...]
        p = p_ref[...]
        dm_ref[...] = jnp.sum(d * (_shift_raw(p, 1) - p), axis=0, keepdims=True)
        dmix = d * m_ref[...]
        dp = (d - dmix + _shift_raw(dmix, -1)).astype(BF16)
        dw_ref[...] = _dot(h_ref[...], dp, TN)

        @pl.when(pl.program_id(0) == 0)
        def _():
            dh_ref[...] = jnp.zeros_like(dh_ref)

        dh_ref[...] += _dot(dp, w_ref[...], NT)

    n = D_IN // COL_TILE
    col = lambda rows: pl.BlockSpec((rows, COL_TILE), lambda j: (0, j))
    return pl.pallas_call(
        body, grid=(n,), name="in_proj_bwd",
        in_specs=[pl.BlockSpec((SEQ, D_MODEL), lambda j: (0, 0)), col(D_MODEL), col(1), col(SEQ), col(SEQ)],
        out_specs=[pl.BlockSpec((SEQ, D_MODEL), lambda j: (0, 0)), col(D_MODEL), col(1)],
        out_shape=[_sds((SEQ, D_MODEL)), _sds((D_MODEL, D_IN)), _sds((1, D_IN))],
        compiler_params=_params(("arbitrary",)),
    )(h, w_in, mix_ext, proj, dpa)


def _bucket_table():
    rel = (jnp.arange(BLOCK)[:, None] + BLOCK) - jnp.arange(2 * BLOCK)[None, :]
    n = jnp.maximum(rel, 0)
    max_exact = N_BUCKETS // 2
    large = max_exact + (jnp.log(jnp.maximum(n, 1).astype(F32) / max_exact)
                         / math.log(MAX_DISTANCE / max_exact) * (N_BUCKETS - max_exact)).astype(jnp.int32)
    large = jnp.minimum(large, N_BUCKETS - 1)
    return jnp.where(n < max_exact, n, large).astype(jnp.int32)


def _select_matrix(g, o):
    a = lax.broadcasted_iota(jnp.int32, (D_KV, D_KV), 0)
    b = lax.broadcasted_iota(jnp.int32, (D_KV, D_KV), 1)
    return ((a - HEAD_DIM * g == b - o) & (b >= o) & (b < o + HEAD_DIM)).astype(F32)


def _attn_block(q, kp, kc, vp, vc, bias, sinks, block_idx):
    kb = jnp.concatenate([kp, kc], axis=0)
    vb = jnp.concatenate([vp, vc], axis=0)
    row = lax.broadcasted_iota(jnp.int32, (BLOCK, 2 * BLOCK), 0)
    col = lax.broadcasted_iota(jnp.int32, (BLOCK, 2 * BLOCK), 1)
    rel = row + BLOCK - col
    mask = (rel >= 0) & (rel < BLOCK) & (col + (block_idx - 1) * BLOCK >= 0)
    lane8 = lax.broadcasted_iota(jnp.int32, (1, N_HEADS), 1)
    kt, vt = {}, {}
    for g in range(2):
        for o in (0, HEAD_DIM):
            sel = _select_matrix(g, o)
            kt[g, o] = dotx(kb, sel)
            vt[g, o] = dotx(vb, sel)
    outs = []
    for j in range(D_ATTN // LANES):
        qs = q[:, j * LANES:(j + 1) * LANES]
        acc = None
        for half in range(2):
            hq = 2 * j + half
            g, o = hq // 4, half * HEAD_DIM
            s = mm_nt(qs, kt[g, o]) * (HEAD_DIM ** -0.5) + bias[hq]
            s = jnp.where(mask, s, NEG_INF)
            sink = jnp.sum(jnp.where(lane8 == hq, sinks, 0.0), axis=1, keepdims=True)
            m = lax.stop_gradient(jnp.maximum(jnp.max(s, axis=-1, keepdims=True), sink))
            p = jnp.exp(s - m)
            probs = p / (jnp.sum(p, axis=-1, keepdims=True) + jnp.exp(sink - m))
            part = mm(probs, vt[g, o])
            acc = part if acc is None else acc + part
        outs.append(acc)
    return jnp.concatenate(outs, axis=1)


def _build_bias(rb_ref, bucket, bias_ref):
    for hq in range(N_HEADS):
        acc = jnp.zeros((BLOCK, 2 * BLOCK), F32)
        for b in range(N_BUCKETS):
            acc = jnp.where(bucket == b, rb_ref[b, hq], acc)
        bias_ref[hq] = acc


def _attn_in_specs():
    prev = lambda n: jnp.maximum(n - 1, 0)
    return [pl.BlockSpec((BLOCK, D_ATTN), lambda n: (n, 0)),
            pl.BlockSpec((BLOCK, D_KV), lambda n: (prev(n), D_ATTN // D_KV)),
            pl.BlockSpec((BLOCK, D_KV), lambda n: (n, D_ATTN // D_KV)),
            pl.BlockSpec((BLOCK, D_KV), lambda n: (prev(n), D_ATTN // D_KV + 1)),
            pl.BlockSpec((BLOCK, D_KV), lambda n: (n, D_ATTN // D_KV + 1)),
            pl.BlockSpec(memory_space=pltpu.SMEM),
            pl.BlockSpec((BLOCK, 2 * BLOCK), lambda n: (0, 0)),
            pl.BlockSpec((1, N_HEADS), lambda n: (0, 0))]


def attn_fwd(proj, rel_bias, bucket, sinks):
    def body(q_ref, kp_ref, kc_ref, vp_ref, vc_ref, rb_ref, bk_ref, sk_ref, o_ref, bias_ref):
        n = pl.program_id(0)

        @pl.when(n == 0)
        def _():
            _build_bias(rb_ref, bk_ref[...], bias_ref)

        o_ref[...] = _attn_block(q_ref[...], kp_ref[...], kc_ref[...], vp_ref[...], vc_ref[...],
                                 tuple(bias_ref[h] for h in range(N_HEADS)), sk_ref[...], n)

    return pl.pallas_call(
        body, grid=(N_BLOCKS,), name="attn_fwd",
        in_specs=_attn_in_specs(),
        out_specs=pl.BlockSpec((BLOCK, D_ATTN), lambda n: (n, 0)),
        out_shape=_sds((SEQ, D_ATTN)),
        scratch_shapes=[pltpu.VMEM((N_HEADS, BLOCK, 2 * BLOCK), F32)],
        compiler_params=_params(("arbitrary",)),
    )(proj, proj, proj, proj, proj, rel_bias, bucket, sinks)


def attn_bwd(proj, rel_bias, bucket, sinks, d_attn):
    def body(q_ref, kp_ref, kc_ref, vp_ref, vc_ref, rb_ref, bk_ref, sk_ref, do_ref,
             dq_ref, dkc_ref, dkp_ref, dvc_ref, dvp_ref, drb_ref, dsk_ref, bias_ref, dbias_ref):
        n = pl.program_id(0)

        @pl.when(n == 0)
        def _():
            _build_bias(rb_ref, bk_ref[...], bias_ref)
            dbias_ref[...] = jnp.zeros_like(dbias_ref)
            dsk_ref[...] = jnp.zeros_like(dsk_ref)

        f = lambda q, kp, kc, vp, vc, bias, sk: _attn_block(q, kp, kc, vp, vc, bias, sk, n)
        _, vjp = jax.vjp(f, q_ref[...], kp_ref[...], kc_ref[...], vp_ref[...], vc_ref[...],
                         tuple(bias_ref[h] for h in range(N_HEADS)), sk_ref[...])
        dq, dkp, dkc, dvp, dvc, dbias, dsk = vjp(do_ref[...])
        dq_ref[...] = dq
        dkc_ref[...] = dkc
        dkp_ref[...] = dkp
        dvc_ref[...] = dvc
        dvp_ref[...] = dvp
        for h in range(N_HEADS):
            dbias_ref[h] += dbias[h]
        dsk_ref[...] += dsk

        @pl.when(n == N_BLOCKS - 1)
        def _():
            bucket_v = bk_ref[...]
            rowi = lax.broadcasted_iota(jnp.int32, (N_BUCKETS, 2 * BLOCK), 0)
            lane = lax.broadcasted_iota(jnp.int32, (N_BUCKETS, N_HEADS), 1)
            out = jnp.zeros((N_BUCKETS, N_HEADS), F32)
            for hq in range(N_HEADS):
                dbh = dbias_ref[hq]
                rows = jnp.zeros((N_BUCKETS, 2 * BLOCK), F32)
                for b in range(N_BUCKETS):
                    part = jnp.sum(jnp.where(bucket_v == b, dbh, 0.0), axis=0, keepdims=True)
                    rows = jnp.where(rowi == b, part, rows)
                tot = jnp.sum(rows, axis=1, keepdims=True)
                out = jnp.where(lane == hq, tot, out)
            drb_ref[...] = out

    blk = lambda w: pl.BlockSpec((BLOCK, w), lambda n: (n, 0))
    return pl.pallas_call(
        body, grid=(N_BLOCKS,), name="attn_bwd",
        in_specs=_attn_in_specs() + [blk(D_ATTN)],
        out_specs=[blk(D_ATTN), blk(D_KV), blk(D_KV), blk(D_KV), blk(D_KV),
                   pl.BlockSpec((N_BUCKETS, N_HEADS), lambda n: (0, 0)),
                   pl.BlockSpec((1, N_HEADS), lambda n: (0, 0))],
        out_shape=[_sds((SEQ, D_ATTN)), _sds((SEQ, D_KV)), _sds((SEQ, D_KV)), _sds((SEQ, D_KV)),
                   _sds((SEQ, D_KV)), _sds((N_BUCKETS, N_HEADS)), _sds((1, N_HEADS))],
        scratch_shapes=[pltpu.VMEM((N_HEADS, BLOCK, 2 * BLOCK), F32),
                        pltpu.VMEM((N_HEADS, BLOCK, 2 * BLOCK), F32)],
        compiler_params=_params(("arbitrary",)),
    )(proj, proj, proj, proj, proj, rel_bias, bucket, sinks, d_attn)


def _rwkv_chunk(s0, r, lw, k, v, kk, a):
    c, hc = CHUNK, N_HEADS * CHUNK
    ri = lax.broadcasted_iota(jnp.int32, (hc, D_RWKV), 0) // c
    li = lax.broadcasted_iota(jnp.int32, (hc, D_RWKV), 1) // HEAD_DIM
    head_rows = (ri == li).astype(F32)
    t = lax.broadcasted_iota(jnp.int32, (c, hc), 0)
    i = lax.broadcasted_iota(jnp.int32, (c, hc), 1) % c
    strict, incl, eye = i < t, i <= t, (i == t).astype(F32)
    ba = lax.broadcasted_iota(jnp.int32, (hc, hc), 0) // c
    bb = lax.broadcasted_iota(jnp.int32, (hc, hc), 1) // c
    blocks = (ba == bb).astype(F32)
    tt = lax.broadcasted_iota(jnp.int32, (c, c), 0)
    ii = lax.broadcasted_iota(jnp.int32, (c, c), 1)
    tri = (ii <= tt).astype(F32)
    stack = lambda x: jnp.tile(x, (N_HEADS, 1)) * head_rows
    bdiag = lambda xc: jnp.tile(xc, (N_HEADS, 1)) * blocks

    cum = dotx(tri, lw)
    cum_end = jnp.sum(lw, axis=0, keepdims=True)
    beta = kk * a
    al = -kk * jnp.exp(cum - lw)
    p_inv = jnp.exp(-cum)
    be, kb, rb = beta * p_inv, k * p_inv, r * jnp.exp(cum)
    sbe, skb, sv = stack(be), stack(kb), stack(v)
    l_ab = jnp.where(strict, dotx(al, sbe, NT), 0.0)
    l_ak = jnp.where(strict, dotx(al, skb, NT), 0.0)
    l_rb = jnp.where(incl, dotx(rb, sbe, NT), 0.0)
    l_rk = jnp.where(incl, dotx(rb, skb, NT), 0.0)
    inv, pw = eye + l_ab, l_ab
    for _ in range(5):
        pw = dotx(pw, bdiag(pw))
        inv = inv + dotx(inv, bdiag(pw))
    u = dotx(inv, stack(dotx(al, s0, NT) + dotx(l_ak, sv)))
    o = dotx(rb, s0, NT) + dotx(l_rb, stack(u)) + dotx(l_rk, sv)
    to_end = jnp.exp(cum_end - cum)
    s1 = s0 * jnp.exp(cum_end) + blocks * (dotx(u, beta * to_end, TN) + dotx(v, k * to_end, TN))
    return o, s1


def rwkv_scan_fwd(r, lw, k, v, kk, a):
    def body(r_ref, lw_ref, k_ref, v_ref, kk_ref, a_ref, o_ref, st_ref, s_ref):
        @pl.when(pl.program_id(0) == 0)
        def _():
            s_ref[...] = jnp.zeros_like(s_ref)

        s0 = s_ref[...]
        st_ref[0] = s0
        o, s1 = _rwkv_chunk(s0, r_ref[...], lw_ref[...], k_ref[...], v_ref[...], kk_ref[...], a_ref[...])
        o_ref[...] = o
        s_ref[...] = s1

    tb = pl.BlockSpec((CHUNK, D_RWKV), lambda c: (c, 0))
    return pl.pallas_call(
        body, grid=(N_CHUNKS,), name="rwkv_scan_fwd",
        in_specs=[tb] * 6,
        out_specs=[tb, pl.BlockSpec((1, D_RWKV, D_RWKV), lambda c: (c, 0, 0))],
        out_shape=[_sds((SEQ, D_RWKV)), _sds((N_CHUNKS, D_RWKV, D_RWKV))],
        scratch_shapes=[pltpu.VMEM((D_RWKV, D_RWKV), F32)],
        compiler_params=_params(("arbitrary",)),
    )(r, lw, k, v, kk, a)


def rwkv_scan_bwd(r, lw, k, v, kk, a, states, d_o):
    def body(r_ref, lw_ref, k_ref, v_ref, kk_ref, a_ref, st_ref, do_ref,
             dr_ref, dlw_ref, dk_ref, dv_ref, dkk_ref, da_ref, ds_ref):
        @pl.when(pl.program_id(0) == 0)
        def _():
            ds_ref[...] = jnp.zeros_like(ds_ref)

        _, vjp = jax.vjp(_rwkv_chunk, st_ref[0], r_ref[...], lw_ref[...], k_ref[...], v_ref[...],
                         kk_ref[...], a_ref[...])
        g = vjp((do_ref[...], ds_ref[...]))
        ds_ref[...] = g[0]
        for ref, val in zip((dr_ref, dlw_ref, dk_ref, dv_ref, dkk_ref, da_ref), g[1:]):
            ref[...] = val

    last = N_CHUNKS - 1
    tb = pl.BlockSpec((CHUNK, D_RWKV), lambda c: (last - c, 0))
    return pl.pallas_call(
        body, grid=(N_CHUNKS,), name="rwkv_scan_bwd",
        in_specs=[tb] * 6 + [pl.BlockSpec((1, D_RWKV, D_RWKV), lambda c: (last - c, 0, 0)), tb],
        out_specs=[tb] * 6,
        out_shape=[_sds((SEQ, D_RWKV))] * 6,
        scratch_shapes=[pltpu.VMEM((D_RWKV, D_RWKV), F32)],
        compiler_params=_params(("arbitrary",)),
    )(r, lw, k, v, kk, a, states, d_o)


def _ffn_mid(ug, uv, cg, cv, bg, bv):
    conv_g = bg + cg[0] * shift_rows(ug, 2) + cg[1] * shift_rows(ug, 1) + cg[2] * ug
    conv_v = bv + cv[0] * shift_rows(uv, 2) + cv[1] * shift_rows(uv, 1) + cv[2] * uv
    return jax.nn.gelu(conv_g, approximate=True) * conv_v


def _conv_rows(ref):
    return tuple(ref[0, j:j + 1, :] for j in range(3))


def _ffn_specs(tile):
    per = D_MODEL // tile
    half = N_DEV // 2
    w_g = pl.BlockSpec((1, D_MODEL, tile), lambda t: (t // per, 0, t % per))
    w_v = pl.BlockSpec((1, D_MODEL, tile), lambda t: (half + t // per, 0, t % per))
    c_g = pl.BlockSpec((1, 3, tile), lambda t: (t // per, 0, t % per))
    c_v = pl.BlockSpec((1, 3, tile), lambda t: (half + t // per, 0, t % per))
    b_g = pl.BlockSpec((1, tile), lambda t: (0, t))
    b_v = pl.BlockSpec((1, tile), lambda t: (0, D_FF // tile + t))
    w_d = pl.BlockSpec((tile, D_MODEL), lambda t: (t, 0))
    return w_g, w_v, c_g, c_v, b_g, b_v, w_d


def ffn_fwd(h2, w_up, conv_w, conv_b, w_down):
    def body(h_ref, wg_ref, wv_ref, cg_ref, cv_ref, bg_ref, bv_ref, wd_ref, f_ref):
        @pl.when(pl.program_id(0) == 0)
        def _():
            f_ref[...] = jnp.zeros_like(f_ref)

        h = h_ref[...]
        act = _ffn_mid(_dot(h, wg_ref[0], NN), _dot(h, wv_ref[0], NN), _conv_rows(cg_ref), _conv_rows(cv_ref),
                       bg_ref[...], bv_ref[...])
        f_ref[...] += _dot(act.astype(BF16), wd_ref[...], NN)

    full = pl.BlockSpec((SEQ, D_MODEL), lambda t: (0, 0))
    return pl.pallas_call(
        body, grid=(D_FF // FF_TILE,), name="ffn_fwd",
        in_specs=[full, *_ffn_specs(FF_TILE)],
        out_specs=full, out_shape=_sds((SEQ, D_MODEL)),
        compiler_params=_params(("arbitrary",)),
    )(h2, w_up, w_up, conv_w, conv_w, conv_b, conv_b, w_down)


def ffn_bwd(h2, w_up, conv_w, conv_b, w_down, df):
    def body(h_hbm, wg_ref, wv_ref, cg_ref, cv_ref, bg_ref, bv_ref, wd_ref, df_hbm,
             dh_hbm, dwg_ref, dwv_ref, dcg_ref, dcv_ref, dbg_ref, dbv_ref, dwd_ref, h_ref, df_ref, dh_ref, sem):
        @pl.when(pl.program_id(0) == 0)
        def _():
            pltpu.sync_copy(h_hbm, h_ref)
            pltpu.sync_copy(df_hbm, df_ref)

        h, df_b, wg, wv = h_ref[...], df_ref[...], wg_ref[0], wv_ref[0]
        act, vjp = jax.vjp(_ffn_mid, _dot(h, wg, NN), _dot(h, wv, NN), _conv_rows(cg_ref), _conv_rows(cv_ref),
                           bg_ref[...], bv_ref[...])
        dwd_ref[...] = _dot(act.astype(BF16), df_b, TN)
        dug, duv, dcg, dcv, dbg, dbv = vjp(_dot(df_b, wd_ref[...], NT))
        dug, duv = dug.astype(BF16), duv.astype(BF16)
        dwg_ref[0] = _dot(h, dug, TN)
        dwv_ref[0] = _dot(h, duv, TN)

        @pl.when(pl.program_id(0) == 0)
        def _():
            dh_ref[...] = jnp.zeros_like(dh_ref)

        dh_ref[...] += _dot(dug, wg, NT) + _dot(duv, wv, NT)
        for j in range(3):
            dcg_ref[0, j:j + 1, :] = dcg[j]
            dcv_ref[0, j:j + 1, :] = dcv[j]
        dbg_ref[...] = dbg
        dbv_ref[...] = dbv

        @pl.when(pl.program_id(0) == D_FF // FF_TILE_BWD - 1)
        def _():
            cp = pltpu.make_async_copy(dh_ref, dh_hbm, sem)
            cp.start()
            cp.wait()

    hbm = pl.BlockSpec(memory_space=pl.ANY)
    w_g, w_v, c_g, c_v, b_g, b_v, w_d = _ffn_specs(FF_TILE_BWD)
    return pl.pallas_call(
        body, grid=(D_FF // FF_TILE_BWD,), name="ffn_bwd",
        in_specs=[hbm, w_g, w_v, c_g, c_v, b_g, b_v, w_d, hbm],
        out_specs=[hbm, w_g, w_v, c_g, c_v, b_g, b_v, w_d],
        out_shape=[_sds((SEQ, D_MODEL)), _sds((N_DEV, D_MODEL, D_MODEL)), _sds((N_DEV, D_MODEL, D_MODEL)),
                   _sds((N_DEV, 3, D_MODEL)), _sds((N_DEV, 3, D_MODEL)), _sds((1, 2 * D_FF)), _sds((1, 2 * D_FF)),
                   _sds((D_FF, D_MODEL))],
        scratch_shapes=[pltpu.VMEM((SEQ, D_MODEL), BF16), pltpu.VMEM((SEQ, D_MODEL), BF16),
                        pltpu.VMEM((SEQ, D_MODEL), F32), pltpu.SemaphoreType.DMA],
        compiler_params=_params(("arbitrary",)),
    )(h2, w_up, w_up, conv_w, conv_w, conv_b, conv_b, w_down, df)


def loss_head(x1, f, target, n_post):
    def tile_loss(x1_t, f_t, g, tgt):
        err = x1_t + _rms(f_t, g) - tgt
        return 0.5 * jnp.sum(jnp.mean(err * err, axis=-1))

    def body(x_ref, f_ref, t_ref, g_ref, dx_ref, df_ref, dg_ref, loss_ref):
        val, (dx, df, dg) = jax.value_and_grad(tile_loss, argnums=(0, 1, 2))(
            x_ref[...], f_ref[...], g_ref[...], t_ref[...])
        dx_ref[...] = dx
        df_ref[...] = df.astype(BF16)

        @pl.when(pl.program_id(0) == 0)
        def _():
            dg_ref[...] = jnp.zeros_like(dg_ref)
            loss_ref[...] = jnp.zeros_like(loss_ref)

        dg_ref[...] += dg
        loss_ref[...] += jnp.full((1, LANES), val, F32)

    tile = pl.BlockSpec((TOK_TILE, D_MODEL), lambda i: (i, 0))
    vec = pl.BlockSpec((1, D_MODEL), lambda i: (0, 0))
    return pl.pallas_call(
        body, grid=(SEQ // TOK_TILE,), name="loss_head",
        in_specs=[tile, tile, tile, vec],
        out_specs=[tile, tile, vec, pl.BlockSpec((1, LANES), lambda i: (0, 0))],
        out_shape=[_sds((SEQ, D_MODEL)), _sds((SEQ, D_MODEL), BF16), _sds((1, D_MODEL)), _sds((1, LANES))],
        compiler_params=_params(("arbitrary",)),
    )(x1, f, target, n_post)


def _mesh_pos():
    return lax.axis_index("x"), lax.axis_index("y"), lax.axis_index("c")


def _flip(pos, rel):
    x, y, c = pos
    return (1 - x if rel & 4 else x, 1 - y if rel & 2 else y, 1 - c if rel & 1 else c)


def _slot(pos):
    x, y, c = pos
    return 4 * x + 2 * y + c


def cast_bf16(w, rows):
    def body(w_ref, o_ref):
        o_ref[...] = w_ref[...].astype(BF16)

    spec = pl.BlockSpec((rows, w.shape[1]), lambda i: (i, 0))
    return pl.pallas_call(body, grid=(w.shape[0] // rows,), name="cast_bf16_%dx%d" % w.shape,
                          in_specs=[spec], out_specs=spec, out_shape=_sds(w.shape, BF16),
                          compiler_params=_params(("arbitrary",)))(w)


def all_gather(shards):
    n = len(shards)

    def body(*refs):
        srcs, outs = refs[:n], refs[n:2 * n]
        send_sems, recv_sems, local_sems = refs[2 * n:]
        me = _mesh_pos()
        sibling = _flip(me, 1)
        chips = [_flip(me, 2), _flip(me, 4), _flip(me, 6)]

        def copy(a, k, block, to, src=None):
            dst = outs[a].at[_slot(block)]
            return pltpu.make_async_remote_copy(
                src_ref=dst if src is None else src, dst_ref=dst,
                send_sem=send_sems.at[7 * a + k], recv_sem=recv_sems.at[7 * a + k],
                device_id=to, device_id_type=pl.DeviceIdType.MESH)

        mine = [pltpu.make_async_copy(srcs[a], outs[a].at[_slot(me)], local_sems.at[a]) for a in range(n)]
        for cp in mine:
            cp.start()
        first = []
        for a in range(n):
            first.append(copy(a, 0, me, sibling, src=srcs[a]))
            first += [copy(a, 1 + j, me, chip, src=srcs[a]) for j, chip in enumerate(chips)]
        for cp in first:
            cp.start()
        passed = []
        for j, chip in enumerate(chips):
            for a in range(n):
                copy(a, 1 + j, chip, me).wait_recv()
                cp = copy(a, 4 + j, chip, sibling)
                cp.start()
                passed.append(cp)
        for a in range(n):
            copy(a, 0, sibling, me).wait_recv()
            for j, chip in enumerate(chips):
                copy(a, 4 + j, _flip(chip, 1), me).wait_recv()
        for cp in first + passed:
            cp.wait_send()
        for cp in mine:
            cp.wait()

    any_spec = pl.BlockSpec(memory_space=pl.ANY)
    return pl.pallas_call(
        body, name="all_gather_%d" % n,
        in_specs=[any_spec] * n, out_specs=[any_spec] * n,
        out_shape=[_sds((N_DEV,) + s.shape, s.dtype) for s in shards],
        scratch_shapes=[pltpu.SemaphoreType.DMA((7 * n,)), pltpu.SemaphoreType.DMA((7 * n,)),
                        pltpu.SemaphoreType.DMA((n,))],
    )(*shards)


def all_to_all(parts, replicated):
    n = len(parts)

    def body(*refs):
        srcs, outs = refs[:n], refs[n:2 * n]
        send_sems, recv_sems, local_sems = refs[2 * n:]
        me = _mesh_pos()

        def piece(a, pos):
            return srcs[a] if replicated[a] else srcs[a].at[_slot(pos)]

        def copy(a, rel):
            peer = _flip(me, rel)
            return pltpu.make_async_remote_copy(
                src_ref=piece(a, peer), dst_ref=outs[a].at[_slot(me)],
                send_sem=send_sems.at[7 * a + rel - 1], recv_sem=recv_sems.at[7 * a + rel - 1],
                device_id=peer, device_id_type=pl.DeviceIdType.MESH)

        def arrival(a, rel):
            peer = _flip(me, rel)
            return pltpu.make_async_remote_copy(
                src_ref=piece(a, me), dst_ref=outs[a].at[_slot(peer)],
                send_sem=send_sems.at[7 * a + rel - 1], recv_sem=recv_sems.at[7 * a + rel - 1],
                device_id=peer, device_id_type=pl.DeviceIdType.MESH)

        mine = [pltpu.make_async_copy(piece(a, me), outs[a].at[_slot(me)], local_sems.at[a]) for a in range(n)]
        for cp in mine:
            cp.start()
        sends = [copy(a, rel) for rel in range(1, N_DEV) for a in range(n)]
        for cp in sends:
            cp.start()
        for rel in range(1, N_DEV):
            for a in range(n):
                arrival(a, rel).wait_recv()
        for cp in sends:
            cp.wait_send()
        for cp in mine:
            cp.wait()

    any_spec = pl.BlockSpec(memory_space=pl.ANY)
    shapes = [p.shape if r else p.shape[1:] for p, r in zip(parts, replicated)]
    return pl.pallas_call(
        body, name="all_to_all_%d" % n,
        in_specs=[any_spec] * n, out_specs=[any_spec] * n,
        out_shape=[_sds((N_DEV,) + s) for s in shapes],
        scratch_shapes=[pltpu.SemaphoreType.DMA((7 * n,)), pltpu.SemaphoreType.DMA((7 * n,)),
                        pltpu.SemaphoreType.DMA((n,))],
    )(*parts)


def adamw(name, w, parts, m, v, rows):
    c1 = 1.0 - ADAM_B1 ** ADAM_STEP
    c2 = 1.0 - ADAM_B2 ** ADAM_STEP

    def body(w_ref, p_ref, m_ref, v_ref, g_ref, d_ref, nm_ref, nv_ref):
        g = p_ref[0]
        for j in range(1, N_DEV):
            g = g + p_ref[j]
        nm = ADAM_B1 * m_ref[...] + (1.0 - ADAM_B1) * g
        nv = ADAM_B2 * v_ref[...] + (1.0 - ADAM_B2) * (g * g)
        g_ref[...] = g
        nm_ref[...] = nm
        nv_ref[...] = nv
        d_ref[...] = -ADAM_LR * ((nm / c1) / (jnp.sqrt(nv / c2) + ADAM_EPS) + ADAM_WD * w_ref[...])

    cols = w.shape[1]
    spec = pl.BlockSpec((rows, cols), lambda i: (i, 0))
    return pl.pallas_call(
        body, grid=(w.shape[0] // rows,), name=name,
        in_specs=[spec, pl.BlockSpec((N_DEV, rows, cols), lambda i: (0, i, 0)), spec, spec],
        out_specs=[spec] * 4, out_shape=[_sds(w.shape)] * 4,
        compiler_params=_params(("arbitrary",)),
    )(w, parts, m, v)


def _rows128(a):
    flat = a.reshape(-1)
    pad = (-flat.shape[0]) % LANES
    if pad:
        flat = jnp.concatenate([flat, jnp.zeros((pad,), flat.dtype)])
    return flat.reshape(-1, LANES)


def _pack(arrays):
    return jnp.concatenate([_rows128(a) for a in arrays], axis=0)


def _unpack(packed, like):
    out, row = [], 0
    for a in like:
        n = math.prod(a.shape)
        rows = -(-n // LANES)
        out.append(packed[row:row + rows].reshape(-1)[:n].reshape(a.shape))
        row += rows
    return out


def _to_slots(full, per):
    return full.reshape(full.shape[0], N_DEV, per).transpose(1, 0, 2)


def _from_slots(slots):
    return slots.transpose(1, 0, 2).reshape(slots.shape[1], -1)


def kernel(x, norm_mix_pre, norm_mix_post, norm_ffn_pre, norm_ffn_post, w_in, rel_bias, sinks, rwkv_shift_mix, w0, w_decay_up, a0, w_iclr_up, w_gate_up, k_k, k_a, r_k, ln_x_g, ln_x_b, w_out, w_ffn_up, conv_w, conv_b, w_ffn_down, loss_target, m_norm_mix_pre, m_norm_mix_post, m_norm_ffn_pre, m_norm_ffn_post, m_w_in, m_rel_bias, m_sinks, m_rwkv_shift_mix, m_w0, m_w_decay_up, m_a0, m_w_iclr_up, m_w_gate_up, m_k_k, m_k_a, m_r_k, m_ln_x_g, m_ln_x_b, m_w_out, m_w_ffn_up, m_conv_w, m_conv_b, m_w_ffn_down, v_norm_mix_pre, v_norm_mix_post, v_norm_ffn_pre, v_norm_ffn_post, v_w_in, v_rel_bias, v_sinks, v_rwkv_shift_mix, v_w0, v_w_decay_up, v_a0, v_w_iclr_up, v_w_gate_up, v_k_k, v_k_a, v_r_k, v_ln_x_g, v_ln_x_b, v_w_out, v_w_ffn_up, v_conv_w, v_conv_b, v_w_ffn_down):
    x2 = x[0]
    target = loss_target[0]

    g_in, g_out, g_up, g_down, g_decay, g_iclr, g_gate, g_conv = all_gather([
        cast_bf16(w_in[0], 256), cast_bf16(w_out[0], 128), cast_bf16(w_ffn_up[0], 256),
        cast_bf16(w_ffn_down[0], 256), w_decay_up[0], w_iclr_up[0], w_gate_up[0], conv_w[0]])
    w_in_b = _from_slots(g_in)
    w_out_b = g_out.reshape(D_MODEL, D_MODEL)
    w_down_b = g_down.reshape(D_FF, D_MODEL)
    lora = jnp.zeros((HEAD_DIM, D_RWKV), F32)
    wd_pad = jnp.concatenate([_from_slots(g_decay), lora], axis=0)
    wi_pad = jnp.concatenate([lora, _from_slots(g_iclr)], axis=0)
    wg_full = _from_slots(g_gate)
    mix_ext = jnp.concatenate([jnp.zeros((1, D_QKV), F32), rwkv_shift_mix], axis=1)
    r_k_row = r_k.reshape(1, D_RWKV)
    bucket = _bucket_table()

    (h1,) = tok_fwd("rms_mix_pre", rms_tile, [x2], [norm_mix_pre], [], [D_MODEL], [BF16])
    proj, ps = in_proj_fwd(h1, w_in_b, mix_ext)
    attn = attn_fwd(proj, rel_bias, bucket, sinks)
    pre_params = [w0, wd_pad, a0, wi_pad, wg_full, k_k, k_a]
    r_, lw_, k2_, v_, kk_, a_, gate_ = tok_fwd("rwkv_pre", rwkv_pre_tile, [ps], pre_params, [],
                                               [D_RWKV] * 7, [F32] * 7)
    o_, states = rwkv_scan_fwd(r_, lw_, k2_, v_, kk_, a_)
    mix_tiles = [o_, r_, k2_, v_, gate_, attn, x2]
    mix_params = [w_out_b, norm_mix_post, ln_x_g, ln_x_b, r_k_row]
    (x1,) = tok_fwd("mix_out", mix_out_tile, mix_tiles, mix_params, [(D_MODEL, D_MODEL)], [D_MODEL], [F32])
    (h2,) = tok_fwd("rms_ffn_pre", rms_tile, [x1], [norm_ffn_pre], [], [D_MODEL], [BF16])
    f = ffn_fwd(h2, g_up, g_conv, conv_b, w_down_b)
    dy, df, d_n_ffn_post, loss_row = loss_head(x1, f, target, norm_ffn_post)
    loss = lax.psum(loss_row[0, 0], ("x", "y", "c"))

    dh2, d_up_g, d_up_v, d_cw_g, d_cw_v, d_cb_g, d_cb_v, d_down = ffn_bwd(h2, g_up, g_conv, conv_b, w_down_b, df)
    half = N_DEV // 2
    d_up = jnp.concatenate([d_up_g[:half], d_up_v[half:]], axis=0)
    d_cw = jnp.concatenate([d_cw_g[:half], d_cw_v[half:]], axis=0)
    d_cb = jnp.concatenate([d_cb_g[:, :D_FF], d_cb_v[:, D_FF:]], axis=1)
    dx1_ffn, d_n_ffn_pre = tok_bwd("rms_ffn_pre_bwd", rms_tile, [x1], [norm_ffn_pre], [], [dh2], [0])
    dx1 = dy + dx1_ffn
    (d_o, d_r1, d_k1, d_v1, d_gate, d_attn, dx_res, d_n_mix_post, d_ln_g, d_ln_b, d_r_k,
     d_w_out) = tok_bwd("mix_out_bwd", mix_out_tile, mix_tiles, mix_params, [(D_MODEL, D_MODEL)], [dx1],
                        [1, 2, 3, 4])
    d_r2, d_lw, d_k2, d_v2, d_kk, d_a = rwkv_scan_bwd(r_, lw_, k2_, v_, kk_, a_, states, d_o)
    pre_cots = [d_r1 + d_r2, d_lw, d_k1 + d_k2, d_v1 + d_v2, d_kk, d_a, d_gate]
    (d_ps, d_w0, d_wd_pad, d_a0, d_wi_pad, d_wg, d_k_k, d_k_a) = tok_bwd(
        "rwkv_pre_bwd", rwkv_pre_tile, [ps], pre_params, [], pre_cots, [0, 1, 2, 3, 4, 5, 6])
    dq, dkc, dkp, dvc, dvp, d_rel_bias, d_sinks = attn_bwd(proj, rel_bias, bucket, sinks, d_attn)
    zero_blk = jnp.zeros((BLOCK, D_KV), F32)
    dk = dkc + jnp.concatenate([dkp[BLOCK:], zero_blk], axis=0)
    dv = dvc + jnp.concatenate([dvp[BLOCK:], zero_blk], axis=0)
    dpa = jnp.concatenate([dq, dk, dv, d_ps], axis=1)
    dh1, d_w_in, d_mix_ext = in_proj_bwd(h1, w_in_b, mix_ext, proj, dpa)
    dx_pre, d_n_mix_pre = tok_bwd("rms_mix_pre_bwd", rms_tile, [x2], [norm_mix_pre], [], [dh1], [0])
    grad_x = (dx_res + dx_pre)[None]

    small_rep = [d_n_mix_pre, d_n_mix_post, d_n_ffn_pre, d_n_ffn_post, d_rel_bias, d_sinks,
                 d_mix_ext[:, D_QKV:], d_w0, d_a0, d_k_k, d_k_a, d_r_k.reshape(r_k.shape), d_ln_g, d_ln_b, d_cb]
    rep_w = [norm_mix_pre, norm_mix_post, norm_ffn_pre, norm_ffn_post, rel_bias, sinks, rwkv_shift_mix,
             w0, a0, k_k, k_a, r_k, ln_x_g, ln_x_b, conv_b]
    rep_m = [m_norm_mix_pre, m_norm_mix_post, m_norm_ffn_pre, m_norm_ffn_post, m_rel_bias, m_sinks,
             m_rwkv_shift_mix, m_w0, m_a0, m_k_k, m_k_a, m_r_k, m_ln_x_g, m_ln_x_b, m_conv_b]
    rep_v = [v_norm_mix_pre, v_norm_mix_post, v_norm_ffn_pre, v_norm_ffn_post, v_rel_bias, v_sinks,
             v_rwkv_shift_mix, v_w0, v_a0, v_k_k, v_k_a, v_r_k, v_ln_x_g, v_ln_x_b, v_conv_b]
    sh_w = [w_decay_up, w_iclr_up, w_gate_up, conv_w]
    sh_m = [m_w_decay_up, m_w_iclr_up, m_w_gate_up, m_conv_w]
    sh_v = [v_w_decay_up, v_w_iclr_up, v_w_gate_up, v_conv_w]
    sh_parts = [_to_slots(d_wd_pad[:HEAD_DIM], HEAD_DIM), _to_slots(d_wi_pad[HEAD_DIM:], HEAD_DIM),
                _to_slots(d_wg, HEAD_DIM), d_cw]
    small_sh = jnp.stack([_pack([p[j] for p in sh_parts]) for j in range(N_DEV)])
    got_in, got_out, got_up, got_down, got_sh, got_rep = all_to_all(
        [_to_slots(d_w_in, D_IN // N_DEV), d_w_out.reshape(N_DEV, D_MODEL // N_DEV, D_MODEL), d_up,
         d_down.reshape(N_DEV, D_FF // N_DEV, D_MODEL), small_sh, _pack(small_rep)],
        [False, False, False, False, False, True])

    big = [adamw("adamw_w_in", w_in[0], got_in, m_w_in[0], v_w_in[0], 256),
           adamw("adamw_w_out", w_out[0], got_out, m_w_out[0], v_w_out[0], 128),
           adamw("adamw_w_ffn_up", w_ffn_up[0], got_up, m_w_ffn_up[0], v_w_ffn_up[0], 128),
           adamw("adamw_w_ffn_down", w_ffn_down[0], got_down, m_w_ffn_down[0], v_w_ffn_down[0], 128)]
    small_w = rep_w + sh_w
    n_small = jnp.concatenate([got_rep, got_sh], axis=1)
    small = adamw("adamw_small", _pack(small_w), n_small, _pack(rep_m + sh_m), _pack(rep_v + sh_v),
                  n_small.shape[1])
    small = [_unpack(p, small_w) for p in small]

    names = ["norm_mix_pre", "norm_mix_post", "norm_ffn_pre", "norm_ffn_post", "w_in", "rel_bias", "sinks",
             "rwkv_shift_mix", "w0", "w_decay_up", "a0", "w_iclr_up", "w_gate_up", "k_k", "k_a", "r_k",
             "ln_x_g", "ln_x_b", "w_out", "w_ffn_up", "conv_w", "conv_b", "w_ffn_down"]
    small_names = ["norm_mix_pre", "norm_mix_post", "norm_ffn_pre", "norm_ffn_post", "rel_bias", "sinks",
                   "rwkv_shift_mix", "w0", "a0", "k_k", "k_a", "r_k", "ln_x_g", "ln_x_b", "conv_b",
                   "w_decay_up", "w_iclr_up", "w_gate_up", "conv_w"]
    big_names = {"w_in": 0, "w_out": 1, "w_ffn_up": 2, "w_ffn_down": 3}
    outs = []
    for kind in range(4):
        for nm in names:
            if nm in big_names:
                outs.append(big[big_names[nm]][kind][None])
            else:
                outs.append(small[kind][small_names.index(nm)])
    return (loss, grad_x, *outs)
```

```python
import functools
import math

import jax
import jax.numpy as jnp
from jax import lax
from jax.experimental import pallas as pl
from jax.experimental.pallas import tpu as pltpu

F32 = jnp.float32
BF16 = jnp.bfloat16

N_DEV = 8
SEQ = 2048
D_MODEL = 1024
HEAD_DIM = 64
D_ATTN = 512
D_KV = 128
D_RWKV = 512
N_HEADS = 8
RWKV_COLS = 1792
D_QKV = D_ATTN + 2 * D_KV
D_IN = D_QKV + RWKV_COLS
D_FF = 4096
BLOCK = 128
N_BLOCKS = SEQ // BLOCK
N_BUCKETS = 32
MAX_DISTANCE = 128
NORM_EPS = 1e-6
GN_EPS = 64e-5
NEG_INF = -1e30
CHUNK = 64
N_CHUNKS = SEQ // CHUNK
TOK_TILE = 256
FF_TILE = 256
FF_TILE_BWD = 128
COL_TILE = 256
LANES = 128
VMEM_LIMIT = 56 * 1024 * 1024

ADAM_LR = 0.001
ADAM_B1 = 0.9
ADAM_B2 = 0.999
ADAM_EPS = 1e-08
ADAM_WD = 0.01
ADAM_STEP = 10

NT = ((1,), (1,))
TN = ((0,), (0,))
NN = ((1,), (0,))


def _sds(shape, dtype=F32):
    return jax.ShapeDtypeStruct(shape, dtype)


def _params(sem=None):
    if sem is None:
        return pltpu.CompilerParams(vmem_limit_bytes=VMEM_LIMIT)
    return pltpu.CompilerParams(dimension_semantics=sem, vmem_limit_bytes=VMEM_LIMIT)


def _dot(a, b, dims):
    return lax.dot_general(a, b, (dims, ((), ())), preferred_element_type=F32)


def _split2(x):
    hi = x.astype(BF16)
    return hi, (x - hi.astype(F32)).astype(BF16)


def _dot3_raw(a, b, dims):
    ah, al = _split2(a)
    bh, bl = _split2(b)
    return _dot(ah, bh, dims) + (_dot(al, bh, dims) + _dot(ah, bl, dims))


@functools.partial(jax.custom_vjp, nondiff_argnums=(2,))
def dot3(a, b, dims):
    return _dot3_raw(a, b, dims)


def _dot3_fwd(a, b, dims):
    return _dot3_raw(a, b, dims), (a, b)


def _dot3_bwd(dims, res, g):
    a, b = res
    if dims == NN:
        return dot3(g, b, NT), dot3(a, g, TN)
    if dims == NT:
        return dot3(g, b, NN), dot3(g, a, TN)
    return dot3(b, g, NT), dot3(a, g, NN)


dot3.defvjp(_dot3_fwd, _dot3_bwd)


@jax.custom_vjp
def mm(a, b):
    return _dot(a.astype(BF16), b.astype(BF16), NN)


def _mm_fwd(a, b):
    return mm(a, b), (a, b)


def _mm_bwd(res, g):
    a, b = res
    gb = g.astype(BF16)
    return _dot(gb, b.astype(BF16), NT).astype(a.dtype), _dot(a.astype(BF16), gb, TN).astype(b.dtype)


mm.defvjp(_mm_fwd, _mm_bwd)


@jax.custom_vjp
def mm_nt(a, b):
    return _dot(a.astype(BF16), b.astype(BF16), NT)


def _mm_nt_fwd(a, b):
    return mm_nt(a, b), (a, b)


def _mm_nt_bwd(res, g):
    a, b = res
    gb = g.astype(BF16)
    return _dot(gb, b.astype(BF16), NN).astype(a.dtype), _dot(gb, a.astype(BF16), TN).astype(b.dtype)


mm_nt.defvjp(_mm_nt_fwd, _mm_nt_bwd)


@jax.custom_vjp
def mmw(a, w, wz):
    return _dot(a.astype(BF16), w, NN)


def _mmw_fwd(a, w, wz):
    return mmw(a, w, wz), (a, w)


def _mmw_bwd(res, g):
    a, w = res
    gb = g.astype(BF16)
    return _dot(gb, w, NT).astype(a.dtype), jnp.zeros_like(w), _dot(a.astype(BF16), gb, TN)


mmw.defvjp(_mmw_fwd, _mmw_bwd)


def _shift_raw(x, n):
    rows = x.shape[0]
    rolled = pltpu.roll(x, n % rows, 0)
    idx = lax.broadcasted_iota(jnp.int32, x.shape, 0)
    keep = idx >= n if n > 0 else idx < rows + n
    return jnp.where(keep, rolled, 0.0)


@functools.partial(jax.custom_vjp, nondiff_argnums=(1,))
def shift_rows(x, n):
    return _shift_raw(x, n)


def _shift_fwd(x, n):
    return _shift_raw(x, n), None


def _shift_bwd(n, _, g):
    return (_shift_raw(g, -n),)


shift_rows.defvjp(_shift_fwd, _shift_bwd)


def _head_matrix(scale):
    a = lax.broadcasted_iota(jnp.int32, (D_RWKV, D_RWKV), 0) // HEAD_DIM
    b = lax.broadcasted_iota(jnp.int32, (D_RWKV, D_RWKV), 1) // HEAD_DIM
    return jnp.where(a == b, scale, 0.0).astype(F32)


def _rms(x, g):
    return x * lax.rsqrt(jnp.mean(x * x, axis=-1, keepdims=True) + NORM_EPS) * g


def _softplus(x):
    return jnp.maximum(x, 0.0) + jnp.log(1.0 + jnp.exp(-jnp.abs(x)))


def _tile_spec(arr, tm):
    return pl.BlockSpec((tm, arr.shape[1]), lambda i: (i, 0))


def _full_spec(arr):
    nd = arr.ndim
    return pl.BlockSpec(arr.shape, lambda i: (0,) * nd)


def tok_fwd(name, fn, tiles, params, zero_shapes, out_widths, out_dtypes, tm=TOK_TILE):
    n_t, n_p = len(tiles), len(params)

    def body(*refs):
        t_vals = [r[...] for r in refs[:n_t]]
        p_vals = [r[...] for r in refs[n_t:n_t + n_p]]
        z_vals = [jnp.zeros(s, F32) for s in zero_shapes]
        outs = fn(*t_vals, *p_vals, *z_vals)
        for r, o in zip(refs[n_t + n_p:], outs):
            r[...] = o.astype(r.dtype)

    rows = tiles[0].shape[0]
    return pl.pallas_call(
        body, grid=(rows // tm,), name=name,
        in_specs=[_tile_spec(t, tm) for t in tiles] + [_full_spec(p) for p in params],
        out_specs=[pl.BlockSpec((tm, w), lambda i: (i, 0)) for w in out_widths],
        out_shape=[_sds((rows, w), dt) for w, dt in zip(out_widths, out_dtypes)],
        compiler_params=_params(("arbitrary",)),
    )(*tiles, *params)


def tok_bwd(name, fn, tiles, params, zero_shapes, cots, diff_params, tm=TOK_TILE):
    n_t, n_p, n_c = len(tiles), len(params), len(cots)
    acc_shapes = [params[i].shape for i in diff_params] + list(zero_shapes)

    def body(*refs):
        t_vals = [r[...].astype(F32) for r in refs[:n_t]]
        p_vals = [r[...] for r in refs[n_t:n_t + n_p]]
        c_vals = [r[...] for r in refs[n_t + n_p:n_t + n_p + n_c]]
        out_refs = refs[n_t + n_p + n_c:]
        z_vals = [jnp.zeros(s, F32) for s in zero_shapes]
        d_vals = [p_vals[i] for i in diff_params]

        def f(t_in, d_in, z_in):
            full = list(p_vals)
            for i, v in zip(diff_params, d_in):
                full[i] = v
            return tuple(fn(*t_in, *full, *z_in))

        _, vjp = jax.vjp(f, t_vals, d_vals, z_vals)
        g_t, g_d, g_z = vjp(tuple(c_vals))
        for r, g in zip(out_refs[:n_t], g_t):
            r[...] = g.astype(r.dtype)
        acc_refs = out_refs[n_t:]

        @pl.when(pl.program_id(0) == 0)
        def _():
            for r in acc_refs:
                r[...] = jnp.zeros_like(r)

        for r, g in zip(acc_refs, list(g_d) + list(g_z)):
            r[...] += g

    rows = tiles[0].shape[0]
    return pl.pallas_call(
        body, grid=(rows // tm,), name=name,
        in_specs=[_tile_spec(t, tm) for t in tiles] + [_full_spec(p) for p in params]
        + [_tile_spec(c, tm) for c in cots],
        out_specs=[_tile_spec(t, tm) for t in tiles]
        + [pl.BlockSpec(s, lambda i, nd=len(s): (0,) * nd) for s in acc_shapes],
        out_shape=[_sds(t.shape) for t in tiles] + [_sds(s) for s in acc_shapes],
        compiler_params=_params(("arbitrary",)),
    )(*tiles, *params, *cots)


def rms_tile(x, g):
    return (_rms(x, g),)


def rwkv_pre_tile(ps, w0, wd_pad, a0, wi_pad, wg, k_k, k_a):
    r = ps[:, 0:D_RWKV]
    k = ps[:, D_RWKV:2 * D_RWKV]
    v = ps[:, 2 * D_RWKV:3 * D_RWKV]
    z2 = ps[:, 3 * D_RWKV:3 * D_RWKV + LANES]
    zg = ps[:, 3 * D_RWKV + LANES:RWKV_COLS]
    w_log = -_softplus(-(w0 + mm(jnp.tanh(z2), wd_pad))) - 0.5
    lw = -jnp.exp(w_log)
    a = jax.nn.sigmoid(a0 + mm(z2, wi_pad))
    g = mm(jax.nn.sigmoid(zg), wg)
    kk = k * k_k
    norm = jnp.sqrt(dot3(kk * kk, _head_matrix(1.0), NN))
    kk = kk / jnp.maximum(norm, 1e-12)
    k2 = k * (1.0 + (a - 1.0) * k_a)
    return r, lw, k2, v, kk, a, g


def mix_out_tile(o, r, k2, v, g, attn, x, w_out, n_post, ln_g, ln_b, r_k, wz):
    hmean = _head_matrix(1.0 / HEAD_DIM)
    d = o - dot3(o, hmean, NN)
    var = dot3(d * d, hmean, NN)
    on = d * lax.rsqrt(var + GN_EPS) * ln_g + ln_b
    bonus = dot3(r * k2 * r_k, _head_matrix(1.0), NN) * v
    rw = (on + bonus) * g
    mix = mmw(jnp.concatenate([attn, rw], axis=1), w_out, wz)
    return (x + _rms(mix, n_post),)


def in_proj_fwd(h, w_in, mix_ext):
    def body(h_ref, w_ref, m_ref, proj_ref, ps_ref):
        p = _dot(h_ref[...], w_ref[...], NN)
        proj_ref[...] = p
        ps_ref[...] = p + (_shift_raw(p, 1) - p) * m_ref[...]

    n = D_IN // COL_TILE
    first = D_QKV // COL_TILE
    return pl.pallas_call(
        body, grid=(n,), name="in_proj_fwd",
        in_specs=[pl.BlockSpec((SEQ, D_MODEL), lambda j: (0, 0)),
                  pl.BlockSpec((D_MODEL, COL_TILE), lambda j: (0, j)),
                  pl.BlockSpec((1, COL_TILE), lambda j: (0, j))],
        out_specs=[pl.BlockSpec((SEQ, COL_TILE), lambda j: (0, j)),
                   pl.BlockSpec((SEQ, COL_TILE), lambda j: (0, jnp.maximum(j - first, 0)))],
        out_shape=[_sds((SEQ, D_IN)), _sds((SEQ, RWKV_COLS))],
        compiler_params=_params(("arbitrary",)),
    )(h, w_in, mix_ext)


def in_proj_bwd(h, w_in, mix_ext, proj, dpa):
    def body(h_ref, w_ref, m_ref, p_ref, d_ref, dh_ref, dw_ref, dm_ref):
        d = d_ref[...]
        p = p_ref[...]
        dm_ref[...] = jnp.sum(d * (_shift_raw(p, 1) - p), axis=0, keepdims=True)
        dmix = d * m_ref[...]
        dp = (d - dmix + _shift_raw(dmix, -1)).astype(BF16)
        dw_ref[...] = _dot(h_ref[...], dp, TN)

        @pl.when(pl.program_id(0) == 0)
        def _():
            dh_ref[...] = jnp.zeros_like(dh_ref)

        dh_ref[...] += _dot(dp, w_ref[...], NT)

    n = D_IN // COL_TILE
    col = lambda rows: pl.BlockSpec((rows, COL_TILE), lambda j: (0, j))
    return pl.pallas_call(
        body, grid=(n,), name="in_proj_bwd",
        in_specs=[pl.BlockSpec((SEQ, D_MODEL), lambda j: (0, 0)), col(D_MODEL), col(1), col(SEQ), col(SEQ)],
        out_specs=[pl.BlockSpec((SEQ, D_MODEL), lambda j: (0, 0)), col(D_MODEL), col(1)],
        out_shape=[_sds((SEQ, D_MODEL)), _sds((D_MODEL, D_IN)), _sds((1, D_IN))],
        compiler_params=_params(("arbitrary",)),
    )(h, w_in, mix_ext, proj, dpa)


def _bucket_table():
    rel = (jnp.arange(BLOCK)[:, None] + BLOCK) - jnp.arange(2 * BLOCK)[None, :]
    n = jnp.maximum(rel, 0)
    max_exact = N_BUCKETS // 2
    large = max_exact + (jnp.log(jnp.maximum(n, 1).astype(F32) / max_exact)
                         / math.log(MAX_DISTANCE / max_exact) * (N_BUCKETS - max_exact)).astype(jnp.int32)
    large = jnp.minimum(large, N_BUCKETS - 1)
    return jnp.where(n < max_exact, n, large).astype(jnp.int32)


def _select_matrix(g, o):
    a = lax.broadcasted_iota(jnp.int32, (D_KV, D_KV), 0)
    b = lax.broadcasted_iota(jnp.int32, (D_KV, D_KV), 1)
    return ((a - HEAD_DIM * g == b - o) & (b >= o) & (b < o + HEAD_DIM)).astype(F32)


def _attn_block(q, kp, kc, vp, vc, bias, sinks, block_idx):
    kb = jnp.concatenate([kp, kc], axis=0)
    vb = jnp.concatenate([vp, vc], axis=0)
    row = lax.broadcasted_iota(jnp.int32, (BLOCK, 2 * BLOCK), 0)
    col = lax.broadcasted_iota(jnp.int32, (BLOCK, 2 * BLOCK), 1)
    rel = row + BLOCK - col
    mask = (rel >= 0) & (rel < BLOCK) & (col + (block_idx - 1) * BLOCK >= 0)
    lane8 = lax.broadcasted_iota(jnp.int32, (1, N_HEADS), 1)
    kt, vt = {}, {}
    for g in range(2):
        for o in (0, HEAD_DIM):
            sel = _select_matrix(g, o)
            kt[g, o] = mm(kb, sel)
            vt[g, o] = mm(vb, sel)
    outs = []
    for j in range(D_ATTN // LANES):
        qs = q[:, j * LANES:(j + 1) * LANES]
        acc = None
        for half in range(2):
            hq = 2 * j + half
            g, o = hq // 4, half * HEAD_DIM
            s = mm_nt(qs, kt[g, o]) * (HEAD_DIM ** -0.5) + bias[hq]
            s = jnp.where(mask, s, NEG_INF)
            sink = jnp.sum(jnp.where(lane8 == hq, sinks, 0.0), axis=1, keepdims=True)
            m = lax.stop_gradient(jnp.maximum(jnp.max(s, axis=-1, keepdims=True), sink))
            p = jnp.exp(s - m)
            probs = p / (jnp.sum(p, axis=-1, keepdims=True) + jnp.exp(sink - m))
            part = mm(probs, vt[g, o])
            acc = part if acc is None else acc + part
        outs.append(acc)
    return jnp.concatenate(outs, axis=1)


def _build_bias(rb_ref, bucket, bias_ref):
    for hq in range(N_HEADS):
        acc = jnp.zeros((BLOCK, 2 * BLOCK), F32)
        for b in range(N_BUCKETS):
            acc = jnp.where(bucket == b, rb_ref[b, hq], acc)
        bias_ref[hq] = acc


def _attn_in_specs():
    prev = lambda n: jnp.maximum(n - 1, 0)
    return [pl.BlockSpec((BLOCK, D_ATTN), lambda n: (n, 0)),
            pl.BlockSpec((BLOCK, D_KV), lambda n: (prev(n), D_ATTN // D_KV)),
            pl.BlockSpec((BLOCK, D_KV), lambda n: (n, D_ATTN // D_KV)),
            pl.BlockSpec((BLOCK, D_KV), lambda n: (prev(n), D_ATTN // D_KV + 1)),
            pl.BlockSpec((BLOCK, D_KV), lambda n: (n, D_ATTN // D_KV + 1)),
            pl.BlockSpec(memory_space=pltpu.SMEM),
            pl.BlockSpec((BLOCK, 2 * BLOCK), lambda n: (0, 0)),
            pl.BlockSpec((1, N_HEADS), lambda n: (0, 0))]


def attn_fwd(proj, rel_bias, bucket, sinks):
    def body(q_ref, kp_ref, kc_ref, vp_ref, vc_ref, rb_ref, bk_ref, sk_ref, o_ref, bias_ref):
        n = pl.program_id(0)

        @pl.when(n == 0)
        def _():
            _build_bias(rb_ref, bk_ref[...], bias_ref)

        o_ref[...] = _attn_block(q_ref[...], kp_ref[...], kc_ref[...], vp_ref[...], vc_ref[...],
                                 tuple(bias_ref[h] for h in range(N_HEADS)), sk_ref[...], n)

    return pl.pallas_call(
        body, grid=(N_BLOCKS,), name="attn_fwd",
        in_specs=_attn_in_specs(),
        out_specs=pl.BlockSpec((BLOCK, D_ATTN), lambda n: (n, 0)),
        out_shape=_sds((SEQ, D_ATTN)),
        scratch_shapes=[pltpu.VMEM((N_HEADS, BLOCK, 2 * BLOCK), F32)],
        compiler_params=_params(("arbitrary",)),
    )(proj, proj, proj, proj, proj, rel_bias, bucket, sinks)


def attn_bwd(proj, rel_bias, bucket, sinks, d_attn):
    def body(q_ref, kp_ref, kc_ref, vp_ref, vc_ref, rb_ref, bk_ref, sk_ref, do_ref,
             dq_ref, dkc_ref, dkp_ref, dvc_ref, dvp_ref, drb_ref, dsk_ref, bias_ref, dbias_ref):
        n = pl.program_id(0)

        @pl.when(n == 0)
        def _():
            _build_bias(rb_ref, bk_ref[...], bias_ref)
            dbias_ref[...] = jnp.zeros_like(dbias_ref)
            dsk_ref[...] = jnp.zeros_like(dsk_ref)

        f = lambda q, kp, kc, vp, vc, bias, sk: _attn_block(q, kp, kc, vp, vc, bias, sk, n)
        _, vjp = jax.vjp(f, q_ref[...], kp_ref[...], kc_ref[...], vp_ref[...], vc_ref[...],
                         tuple(bias_ref[h] for h in range(N_HEADS)), sk_ref[...])
        dq, dkp, dkc, dvp, dvc, dbias, dsk = vjp(do_ref[...])
        dq_ref[...] = dq
        dkc_ref[...] = dkc
        dkp_ref[...] = dkp
        dvc_ref[...] = dvc
        dvp_ref[...] = dvp
        for h in range(N_HEADS):
            dbias_ref[h] += dbias[h]
        dsk_ref[...] += dsk

        @pl.when(n == N_BLOCKS - 1)
        def _():
            bucket_v = bk_ref[...]
            rowi = lax.broadcasted_iota(jnp.int32, (N_BUCKETS, 2 * BLOCK), 0)
            lane = lax.broadcasted_iota(jnp.int32, (N_BUCKETS, N_HEADS), 1)
            out = jnp.zeros((N_BUCKETS, N_HEADS), F32)
            for hq in range(N_HEADS):
                dbh = dbias_ref[hq]
                rows = jnp.zeros((N_BUCKETS, 2 * BLOCK), F32)
                for b in range(N_BUCKETS):
                    part = jnp.sum(jnp.where(bucket_v == b, dbh, 0.0), axis=0, keepdims=True)
                    rows = jnp.where(rowi == b, part, rows)
                tot = jnp.sum(rows, axis=1, keepdims=True)
                out = jnp.where(lane == hq, tot, out)
            drb_ref[...] = out

    blk = lambda w: pl.BlockSpec((BLOCK, w), lambda n: (n, 0))
    return pl.pallas_call(
        body, grid=(N_BLOCKS,), name="attn_bwd",
        in_specs=_attn_in_specs() + [blk(D_ATTN)],
        out_specs=[blk(D_ATTN), blk(D_KV), blk(D_KV), blk(D_KV), blk(D_KV),
                   pl.BlockSpec((N_BUCKETS, N_HEADS), lambda n: (0, 0)),
                   pl.BlockSpec((1, N_HEADS), lambda n: (0, 0))],
        out_shape=[_sds((SEQ, D_ATTN)), _sds((SEQ, D_KV)), _sds((SEQ, D_KV)), _sds((SEQ, D_KV)),
                   _sds((SEQ, D_KV)), _sds((N_BUCKETS, N_HEADS)), _sds((1, N_HEADS))],
        scratch_shapes=[pltpu.VMEM((N_HEADS, BLOCK, 2 * BLOCK), F32),
                        pltpu.VMEM((N_HEADS, BLOCK, 2 * BLOCK), F32)],
        compiler_params=_params(("arbitrary",)),
    )(proj, proj, proj, proj, proj, rel_bias, bucket, sinks, d_attn)


def _chunk_masks():
    c, hc = CHUNK, N_HEADS * CHUNK
    ri = lax.broadcasted_iota(jnp.int32, (hc, D_RWKV), 0) // c
    li = lax.broadcasted_iota(jnp.int32, (hc, D_RWKV), 1) // HEAD_DIM
    ba = lax.broadcasted_iota(jnp.int32, (hc, hc), 0) // c
    bb = lax.broadcasted_iota(jnp.int32, (hc, hc), 1) // c
    return (ri == li).astype(F32), (ba == bb).astype(F32)


def _bdiag(xc, blocks):
    return jnp.tile(xc, (N_HEADS, 1)) * blocks


def _neumann(l):
    c = CHUNK
    _, blocks = _chunk_masks()
    t = lax.broadcasted_iota(jnp.int32, l.shape, 0)
    i = lax.broadcasted_iota(jnp.int32, l.shape, 1) % c
    inv = (i == t).astype(F32) + l
    pw = dot3(l, _bdiag(l, blocks), NN)
    for _ in range(4):
        both = dot3(jnp.concatenate([inv, pw], axis=0), _bdiag(pw, blocks), NN)
        inv = inv + both[:c]
        pw = both[c:]
    return inv + dot3(inv, _bdiag(pw, blocks), NN)


@jax.custom_vjp
def neumann_inv(l):
    return _neumann(l)


def _neumann_fwd(l):
    inv = _neumann(l)
    return inv, inv


def _neumann_bwd(inv, g):
    c = CHUNK
    _, blocks = _chunk_masks()
    bd_t = _bdiag(inv, blocks).T
    inv_t = bd_t[0:c]
    for h in range(1, N_HEADS):
        inv_t = inv_t + bd_t[h * c:(h + 1) * c]
    return (dot3(dot3(inv_t, _bdiag(g, blocks), NN), bd_t, NN),)


neumann_inv.defvjp(_neumann_fwd, _neumann_bwd)


def _cumsum_raw(x, dims):
    c = x.shape[0]
    tt = lax.broadcasted_iota(jnp.int32, (c, c), 0)
    ii = lax.broadcasted_iota(jnp.int32, (c, c), 1)
    tri = (ii <= tt).astype(BF16)
    hi = x.astype(BF16)
    rest = x - hi.astype(F32)
    mid = rest.astype(BF16)
    lo = (rest - mid.astype(F32)).astype(BF16)
    return _dot(tri, hi, dims) + (_dot(tri, mid, dims) + _dot(tri, lo, dims))


@jax.custom_vjp
def cumsum_rows(x):
    return _cumsum_raw(x, NN)


def _cumsum_fwd(x):
    return _cumsum_raw(x, NN), None


def _cumsum_bwd(_, g):
    return (_cumsum_raw(g, TN),)


cumsum_rows.defvjp(_cumsum_fwd, _cumsum_bwd)


def _rwkv_chunk(s0, r, lw, k, v, kk, a):
    c, hc = CHUNK, N_HEADS * CHUNK
    head_rows, blocks = _chunk_masks()
    t = lax.broadcasted_iota(jnp.int32, (c, hc), 0)
    i = lax.broadcasted_iota(jnp.int32, (c, hc), 1) % c
    strict, incl = i < t, i <= t
    stack = lambda x: jnp.tile(x, (N_HEADS, 1)) * head_rows

    cum = cumsum_rows(lw)
    cum_end = jnp.sum(lw, axis=0, keepdims=True)
    beta = kk * a
    al = -kk * jnp.exp(cum - lw)
    p_inv = jnp.exp(-cum)
    be, kb, rb = beta * p_inv, k * p_inv, r * jnp.exp(cum)
    ar = jnp.concatenate([al, rb], axis=0)
    sv = stack(v)
    l_all = dot3(ar, jnp.concatenate([stack(be), stack(kb)], axis=0), NT)
    l_ab = jnp.where(strict, l_all[:c, :hc], 0.0)
    l_ak = jnp.where(strict, l_all[:c, hc:], 0.0)
    l_rb = jnp.where(incl, l_all[c:, :hc], 0.0)
    l_rk = jnp.where(incl, l_all[c:, hc:], 0.0)
    inv = neumann_inv(l_ab)
    from_s0 = dot3(ar, s0, NT)
    from_v = dot3(jnp.concatenate([l_ak, l_rk], axis=0), sv, NN)
    u = dot3(inv, stack(from_s0[:c] + from_v[:c]), NN)
    o = from_s0[c:] + from_v[c:] + dot3(l_rb, stack(u), NN)
    to_end = jnp.exp(cum_end - cum)
    s1 = s0 * jnp.exp(cum_end) + blocks * dot3(
        jnp.concatenate([u, v], axis=0), jnp.concatenate([beta * to_end, k * to_end], axis=0), TN)
    return o, s1


def rwkv_scan_fwd(r, lw, k, v, kk, a):
    def body(r_ref, lw_ref, k_ref, v_ref, kk_ref, a_ref, o_ref, st_ref, s_ref):
        @pl.when(pl.program_id(0) == 0)
        def _():
            s_ref[...] = jnp.zeros_like(s_ref)

        s0 = s_ref[...]
        st_ref[0] = s0
        o, s1 = _rwkv_chunk(s0, r_ref[...], lw_ref[...], k_ref[...], v_ref[...], kk_ref[...], a_ref[...])
        o_ref[...] = o
        s_ref[...] = s1

    tb = pl.BlockSpec((CHUNK, D_RWKV), lambda c: (c, 0))
    return pl.pallas_call(
        body, grid=(N_CHUNKS,), name="rwkv_scan_fwd",
        in_specs=[tb] * 6,
        out_specs=[tb, pl.BlockSpec((1, D_RWKV, D_RWKV), lambda c: (c, 0, 0))],
        out_shape=[_sds((SEQ, D_RWKV)), _sds((N_CHUNKS, D_RWKV, D_RWKV))],
        scratch_shapes=[pltpu.VMEM((D_RWKV, D_RWKV), F32)],
        compiler_params=_params(("arbitrary",)),
    )(r, lw, k, v, kk, a)


def rwkv_scan_bwd(r, lw, k, v, kk, a, states, d_o):
    def body(r_ref, lw_ref, k_ref, v_ref, kk_ref, a_ref, st_ref, do_ref,
             dr_ref, dlw_ref, dk_ref, dv_ref, dkk_ref, da_ref, ds_ref):
        @pl.when(pl.program_id(0) == 0)
        def _():
            ds_ref[...] = jnp.zeros_like(ds_ref)

        _, vjp = jax.vjp(_rwkv_chunk, st_ref[0], r_ref[...], lw_ref[...], k_ref[...], v_ref[...],
                         kk_ref[...], a_ref[...])
        g = vjp((do_ref[...], ds_ref[...]))
        ds_ref[...] = g[0]
        for ref, val in zip((dr_ref, dlw_ref, dk_ref, dv_ref, dkk_ref, da_ref), g[1:]):
            ref[...] = val

    last = N_CHUNKS - 1
    tb = pl.BlockSpec((CHUNK, D_RWKV), lambda c: (last - c, 0))
    return pl.pallas_call(
        body, grid=(N_CHUNKS,), name="rwkv_scan_bwd",
        in_specs=[tb] * 6 + [pl.BlockSpec((1, D_RWKV, D_RWKV), lambda c: (last - c, 0, 0)), tb],
        out_specs=[tb] * 6,
        out_shape=[_sds((SEQ, D_RWKV))] * 6,
        scratch_shapes=[pltpu.VMEM((D_RWKV, D_RWKV), F32)],
        compiler_params=_params(("arbitrary",)),
    )(r, lw, k, v, kk, a, states, d_o)


def _ffn_mid(ug, uv, cg, cv, bg, bv):
    conv_g = bg + cg[0] * shift_rows(ug, 2) + cg[1] * shift_rows(ug, 1) + cg[2] * ug
    conv_v = bv + cv[0] * shift_rows(uv, 2) + cv[1] * shift_rows(uv, 1) + cv[2] * uv
    return jax.nn.gelu(conv_g, approximate=True) * conv_v


def _conv_rows(ref):
    return tuple(ref[0, j:j + 1, :] for j in range(3))


def _ffn_specs(tile):
    per = D_MODEL // tile
    half = N_DEV // 2
    w_g = pl.BlockSpec((1, D_MODEL, tile), lambda t: (t // per, 0, t % per))
    w_v = pl.BlockSpec((1, D_MODEL, tile), lambda t: (half + t // per, 0, t % per))
    c_g = pl.BlockSpec((1, 3, tile), lambda t: (t // per, 0, t % per))
    c_v = pl.BlockSpec((1, 3, tile), lambda t: (half + t // per, 0, t % per))
    b_g = pl.BlockSpec((1, tile), lambda t: (0, t))
    b_v = pl.BlockSpec((1, tile), lambda t: (0, D_FF // tile + t))
    w_d = pl.BlockSpec((tile, D_MODEL), lambda t: (t, 0))
    return w_g, w_v, c_g, c_v, b_g, b_v, w_d


def ffn_fwd(h2, w_up, conv_w, conv_b, w_down):
    def body(h_ref, wg_ref, wv_ref, cg_ref, cv_ref, bg_ref, bv_ref, wd_ref, f_ref):
        @pl.when(pl.program_id(0) == 0)
        def _():
            f_ref[...] = jnp.zeros_like(f_ref)

        h = h_ref[...]
        act = _ffn_mid(_dot(h, wg_ref[0], NN), _dot(h, wv_ref[0], NN), _conv_rows(cg_ref), _conv_rows(cv_ref),
                       bg_ref[...], bv_ref[...])
        f_ref[...] += _dot(act.astype(BF16), wd_ref[...], NN)

    full = pl.BlockSpec((SEQ, D_MODEL), lambda t: (0, 0))
    return pl.pallas_call(
        body, grid=(D_FF // FF_TILE,), name="ffn_fwd",
        in_specs=[full, *_ffn_specs(FF_TILE)],
        out_specs=full, out_shape=_sds((SEQ, D_MODEL)),
        compiler_params=_params(("arbitrary",)),
    )(h2, w_up, w_up, conv_w, conv_w, conv_b, conv_b, w_down)


def ffn_bwd(h2, w_up, conv_w, conv_b, w_down, df):
    def body(h_hbm, wg_ref, wv_ref, cg_ref, cv_ref, bg_ref, bv_ref, wd_ref, df_hbm,
             dh_hbm, dwg_ref, dwv_ref, dcg_ref, dcv_ref, dbg_ref, dbv_ref, dwd_ref, h_ref, df_ref, dh_ref, sem):
        @pl.when(pl.program_id(0) == 0)
        def _():
            pltpu.sync_copy(h_hbm, h_ref)
            pltpu.sync_copy(df_hbm, df_ref)

        h, df_b, wg, wv = h_ref[...], df_ref[...], wg_ref[0], wv_ref[0]
        act, vjp = jax.vjp(_ffn_mid, _dot(h, wg, NN), _dot(h, wv, NN), _conv_rows(cg_ref), _conv_rows(cv_ref),
                           bg_ref[...], bv_ref[...])
        dwd_ref[...] = _dot(act.astype(BF16), df_b, TN)
        dug, duv, dcg, dcv, dbg, dbv = vjp(_dot(df_b, wd_ref[...], NT))
        dug, duv = dug.astype(BF16), duv.astype(BF16)
        dwg_ref[0] = _dot(h, dug, TN)
        dwv_ref[0] = _dot(h, duv, TN)

        @pl.when(pl.program_id(0) == 0)
        def _():
            dh_ref[...] = jnp.zeros_like(dh_ref)

        dh_ref[...] += _dot(dug, wg, NT) + _dot(duv, wv, NT)
        for j in range(3):
            dcg_ref[0, j:j + 1, :] = dcg[j]
            dcv_ref[0, j:j + 1, :] = dcv[j]
        dbg_ref[...] = dbg
        dbv_ref[...] = dbv

        @pl.when(pl.program_id(0) == D_FF // FF_TILE_BWD - 1)
        def _():
            cp = pltpu.make_async_copy(dh_ref, dh_hbm, sem)
            cp.start()
            cp.wait()

    hbm = pl.BlockSpec(memory_space=pl.ANY)
    w_g, w_v, c_g, c_v, b_g, b_v, w_d = _ffn_specs(FF_TILE_BWD)
    return pl.pallas_call(
        body, grid=(D_FF // FF_TILE_BWD,), name="ffn_bwd",
        in_specs=[hbm, w_g, w_v, c_g, c_v, b_g, b_v, w_d, hbm],
        out_specs=[hbm, w_g, w_v, c_g, c_v, b_g, b_v, w_d],
        out_shape=[_sds((SEQ, D_MODEL)), _sds((N_DEV, D_MODEL, D_MODEL)), _sds((N_DEV, D_MODEL, D_MODEL)),
                   _sds((N_DEV, 3, D_MODEL)), _sds((N_DEV, 3, D_MODEL)), _sds((1, 2 * D_FF)), _sds((1, 2 * D_FF)),
                   _sds((D_FF, D_MODEL))],
        scratch_shapes=[pltpu.VMEM((SEQ, D_MODEL), BF16), pltpu.VMEM((SEQ, D_MODEL), BF16),
                        pltpu.VMEM((SEQ, D_MODEL), F32), pltpu.SemaphoreType.DMA],
        compiler_params=_params(("arbitrary",)),
    )(h2, w_up, w_up, conv_w, conv_w, conv_b, conv_b, w_down, df)


def loss_head(x1, f, target, n_post):
    def tile_loss(x1_t, f_t, g, tgt):
        err = x1_t + _rms(f_t, g) - tgt
        return 0.5 * jnp.sum(jnp.mean(err * err, axis=-1))

    def body(x_ref, f_ref, t_ref, g_ref, dx_ref, df_ref, dg_ref, loss_ref):
        val, (dx, df, dg) = jax.value_and_grad(tile_loss, argnums=(0, 1, 2))(
            x_ref[...], f_ref[...], g_ref[...], t_ref[...])
        dx_ref[...] = dx
        df_ref[...] = df.astype(BF16)

        @pl.when(pl.program_id(0) == 0)
        def _():
            dg_ref[...] = jnp.zeros_like(dg_ref)
            loss_ref[...] = jnp.zeros_like(loss_ref)

        dg_ref[...] += dg
        loss_ref[...] += jnp.full((1, LANES), val, F32)

    tile = pl.BlockSpec((TOK_TILE, D_MODEL), lambda i: (i, 0))
    vec = pl.BlockSpec((1, D_MODEL), lambda i: (0, 0))
    return pl.pallas_call(
        body, grid=(SEQ // TOK_TILE,), name="loss_head",
        in_specs=[tile, tile, tile, vec],
        out_specs=[tile, tile, vec, pl.BlockSpec((1, LANES), lambda i: (0, 0))],
        out_shape=[_sds((SEQ, D_MODEL)), _sds((SEQ, D_MODEL), BF16), _sds((1, D_MODEL)), _sds((1, LANES))],
        compiler_params=_params(("arbitrary",)),
    )(x1, f, target, n_post)


def _mesh_pos():
    return lax.axis_index("x"), lax.axis_index("y"), lax.axis_index("c")


def _flip(pos, rel):
    x, y, c = pos
    return (1 - x if rel & 4 else x, 1 - y if rel & 2 else y, 1 - c if rel & 1 else c)


def _slot(pos):
    x, y, c = pos
    return 4 * x + 2 * y + c


def cast_bf16(w, rows):
    def body(w_ref, o_ref):
        o_ref[...] = w_ref[...].astype(BF16)

    spec = pl.BlockSpec((rows, w.shape[1]), lambda i: (i, 0))
    return pl.pallas_call(body, grid=(w.shape[0] // rows,), name="cast_bf16_%dx%d" % w.shape,
                          in_specs=[spec], out_specs=spec, out_shape=_sds(w.shape, BF16),
                          compiler_params=_params(("arbitrary",)))(w)


def all_gather(shards):
    n = len(shards)

    def body(*refs):
        srcs, outs = refs[:n], refs[n:2 * n]
        send_sems, recv_sems, local_sems = refs[2 * n:]
        me = _mesh_pos()
        sibling = _flip(me, 1)
        chips = [_flip(me, 2), _flip(me, 4), _flip(me, 6)]

        def copy(a, k, block, to, src=None):
            dst = outs[a].at[_slot(block)]
            return pltpu.make_async_remote_copy(
                src_ref=dst if src is None else src, dst_ref=dst,
                send_sem=send_sems.at[7 * a + k], recv_sem=recv_sems.at[7 * a + k],
                device_id=to, device_id_type=pl.DeviceIdType.MESH)

        mine = [pltpu.make_async_copy(srcs[a], outs[a].at[_slot(me)], local_sems.at[a]) for a in range(n)]
        for cp in mine:
            cp.start()
        first = []
        for a in range(n):
            first.append(copy(a, 0, me, sibling, src=srcs[a]))
            first += [copy(a, 1 + j, me, chip, src=srcs[a]) for j, chip in enumerate(chips)]
        for cp in first:
            cp.start()
        passed = []
        for j, chip in enumerate(chips):
            for a in range(n):
                copy(a, 1 + j, chip, me).wait_recv()
                cp = copy(a, 4 + j, chip, sibling)
                cp.start()
                passed.append(cp)
        for a in range(n):
            copy(a, 0, sibling, me).wait_recv()
            for j, chip in enumerate(chips):
                copy(a, 4 + j, _flip(chip, 1), me).wait_recv()
        for cp in first + passed:
            cp.wait_send()
        for cp in mine:
            cp.wait()

    any_spec = pl.BlockSpec(memory_space=pl.ANY)
    return pl.pallas_call(
        body, name="all_gather_%d" % n,
        in_specs=[any_spec] * n, out_specs=[any_spec] * n,
        out_shape=[_sds((N_DEV,) + s.shape, s.dtype) for s in shards],
        scratch_shapes=[pltpu.SemaphoreType.DMA((7 * n,)), pltpu.SemaphoreType.DMA((7 * n,)),
                        pltpu.SemaphoreType.DMA((n,))],
    )(*shards)


def exchange(name, parts, replicated, rels, members, index, member_axis):
    n, nr = len(parts), len(rels)
    pick_index = (slice(None),) * member_axis + (0,)

    def body(*refs):
        srcs, outs = refs[:n], refs[n:2 * n]
        send_sems, recv_sems, local_sems = refs[2 * n:]
        me = _mesh_pos()

        def piece(a, pos):
            return srcs[a] if replicated[a] else srcs[a].at[(slice(None),) * member_axis + (index(pos),)]

        def copy(a, j, src_pos, dst_pos):
            return pltpu.make_async_remote_copy(
                src_ref=piece(a, src_pos), dst_ref=outs[a].at[index(dst_pos)],
                send_sem=send_sems.at[nr * a + j], recv_sem=recv_sems.at[nr * a + j],
                device_id=_flip(me, rels[j]), device_id_type=pl.DeviceIdType.MESH)

        mine = [pltpu.make_async_copy(piece(a, me), outs[a].at[index(me)], local_sems.at[a]) for a in range(n)]
        for cp in mine:
            cp.start()
        sends = [copy(a, j, _flip(me, rels[j]), me) for j in range(nr) for a in range(n)]
        for cp in sends:
            cp.start()
        for j in range(nr):
            for a in range(n):
                copy(a, j, me, _flip(me, rels[j])).wait_recv()
        for cp in sends:
            cp.wait_send()
        for cp in mine:
            cp.wait()

    any_spec = pl.BlockSpec(memory_space=pl.ANY)
    shapes = [p.shape if r else jax.eval_shape(lambda t: t[pick_index], p).shape for p, r in zip(parts, replicated)]
    return pl.pallas_call(
        body, name=name,
        in_specs=[any_spec] * n, out_specs=[any_spec] * n,
        out_shape=[_sds((members,) + s, p.dtype) for s, p in zip(shapes, parts)],
        scratch_shapes=[pltpu.SemaphoreType.DMA((nr * n,)), pltpu.SemaphoreType.DMA((nr * n,)),
                        pltpu.SemaphoreType.DMA((n,))],
    )(*parts)


def pair_exchange(parts, replicated):
    return exchange("pair_exchange", parts, replicated, [1], 2, lambda pos: pos[2], 1)


def chip_exchange(parts, replicated):
    return exchange("chip_exchange", parts, replicated, [2, 4, 6], 4, lambda pos: 2 * pos[0] + pos[1], 0)


def add_pair(name, both, out_dtype, rows):
    def body(b_ref, o_ref):
        o_ref[...] = (b_ref[0] + b_ref[1]).astype(o_ref.dtype)

    _, n, r, c = both.shape
    return pl.pallas_call(
        body, grid=(n, r // rows), name=name,
        in_specs=[pl.BlockSpec((2, 1, rows, c), lambda i, j: (0, i, j, 0))],
        out_specs=pl.BlockSpec((1, rows, c), lambda i, j: (i, j, 0)),
        out_shape=_sds((n, r, c), out_dtype),
        compiler_params=_params(("arbitrary", "arbitrary")),
    )(both)


def adamw(name, w, parts, m, v, rows):
    c1 = 1.0 - ADAM_B1 ** ADAM_STEP
    c2 = 1.0 - ADAM_B2 ** ADAM_STEP
    n_parts = parts.shape[0]

    def body(w_ref, p_ref, m_ref, v_ref, g_ref, d_ref, nm_ref, nv_ref):
        g = p_ref[0].astype(F32)
        for j in range(1, n_parts):
            g = g + p_ref[j].astype(F32)
        nm = ADAM_B1 * m_ref[...] + (1.0 - ADAM_B1) * g
        nv = ADAM_B2 * v_ref[...] + (1.0 - ADAM_B2) * (g * g)
        g_ref[...] = g
        nm_ref[...] = nm
        nv_ref[...] = nv
        d_ref[...] = -ADAM_LR * ((nm / c1) / (jnp.sqrt(nv / c2) + ADAM_EPS) + ADAM_WD * w_ref[...])

    cols = w.shape[1]
    spec = pl.BlockSpec((rows, cols), lambda i: (i, 0))
    return pl.pallas_call(
        body, grid=(w.shape[0] // rows,), name=name,
        in_specs=[spec, pl.BlockSpec((n_parts, rows, cols), lambda i: (0, i, 0)), spec, spec],
        out_specs=[spec] * 4, out_shape=[_sds(w.shape)] * 4,
        compiler_params=_params(("arbitrary",)),
    )(w, parts, m, v)


def _rows128(a):
    flat = a.reshape(-1)
    pad = (-flat.shape[0]) % LANES
    if pad:
        flat = jnp.concatenate([flat, jnp.zeros((pad,), flat.dtype)])
    return flat.reshape(-1, LANES)


def _pack(arrays):
    rows = [_rows128(a) for a in arrays]
    pad = (-sum(r.shape[0] for r in rows)) % 8
    return jnp.concatenate(rows + [jnp.zeros((pad, LANES), rows[0].dtype)] * (pad > 0), axis=0)


def _unpack(packed, like):
    out, row = [], 0
    for a in like:
        n = math.prod(a.shape)
        rows = -(-n // LANES)
        out.append(packed[row:row + rows].reshape(-1)[:n].reshape(a.shape))
        row += rows
    return out


def _to_slots(full, per):
    return full.reshape(full.shape[0], N_DEV, per).transpose(1, 0, 2)


def _from_slots(slots):
    return slots.transpose(1, 0, 2).reshape(slots.shape[1], -1)


def kernel(x, norm_mix_pre, norm_mix_post, norm_ffn_pre, norm_ffn_post, w_in, rel_bias, sinks, rwkv_shift_mix, w0, w_decay_up, a0, w_iclr_up, w_gate_up, k_k, k_a, r_k, ln_x_g, ln_x_b, w_out, w_ffn_up, conv_w, conv_b, w_ffn_down, loss_target, m_norm_mix_pre, m_norm_mix_post, m_norm_ffn_pre, m_norm_ffn_post, m_w_in, m_rel_bias, m_sinks, m_rwkv_shift_mix, m_w0, m_w_decay_up, m_a0, m_w_iclr_up, m_w_gate_up, m_k_k, m_k_a, m_r_k, m_ln_x_g, m_ln_x_b, m_w_out, m_w_ffn_up, m_conv_w, m_conv_b, m_w_ffn_down, v_norm_mix_pre, v_norm_mix_post, v_norm_ffn_pre, v_norm_ffn_post, v_w_in, v_rel_bias, v_sinks, v_rwkv_shift_mix, v_w0, v_w_decay_up, v_a0, v_w_iclr_up, v_w_gate_up, v_k_k, v_k_a, v_r_k, v_ln_x_g, v_ln_x_b, v_w_out, v_w_ffn_up, v_conv_w, v_conv_b, v_w_ffn_down):
    x2 = x[0]
    target = loss_target[0]

    g_in, g_out, g_up, g_down, g_decay, g_iclr, g_gate, g_conv = all_gather([
        cast_bf16(w_in[0], 256), cast_bf16(w_out[0], 128), cast_bf16(w_ffn_up[0], 256),
        cast_bf16(w_ffn_down[0], 256), w_decay_up[0], w_iclr_up[0], w_gate_up[0], conv_w[0]])
    w_in_b = _from_slots(g_in)
    w_out_b = g_out.reshape(D_MODEL, D_MODEL)
    w_down_b = g_down.reshape(D_FF, D_MODEL)
    lora = jnp.zeros((HEAD_DIM, D_RWKV), F32)
    wd_pad = jnp.concatenate([_from_slots(g_decay), lora], axis=0)
    wi_pad = jnp.concatenate([lora, _from_slots(g_iclr)], axis=0)
    wg_full = _from_slots(g_gate)
    mix_ext = jnp.concatenate([jnp.zeros((1, D_QKV), F32), rwkv_shift_mix], axis=1)
    r_k_row = r_k.reshape(1, D_RWKV)
    bucket = _bucket_table()

    (h1,) = tok_fwd("rms_mix_pre", rms_tile, [x2], [norm_mix_pre], [], [D_MODEL], [BF16])
    proj, ps = in_proj_fwd(h1, w_in_b, mix_ext)
    attn = attn_fwd(proj, rel_bias, bucket, sinks)
    pre_params = [w0, wd_pad, a0, wi_pad, wg_full, k_k, k_a]
    r_, lw_, k2_, v_, kk_, a_, gate_ = tok_fwd("rwkv_pre", rwkv_pre_tile, [ps], pre_params, [],
                                               [D_RWKV] * 7, [F32] * 7)
    o_, states = rwkv_scan_fwd(r_, lw_, k2_, v_, kk_, a_)
    mix_tiles = [o_, r_, k2_, v_, gate_, attn, x2]
    mix_params = [w_out_b, norm_mix_post, ln_x_g, ln_x_b, r_k_row]
    (x1,) = tok_fwd("mix_out", mix_out_tile, mix_tiles, mix_params, [(D_MODEL, D_MODEL)], [D_MODEL], [F32])
    (h2,) = tok_fwd("rms_ffn_pre", rms_tile, [x1], [norm_ffn_pre], [], [D_MODEL], [BF16])
    f = ffn_fwd(h2, g_up, g_conv, conv_b, w_down_b)
    dy, df, d_n_ffn_post, loss_row = loss_head(x1, f, target, norm_ffn_post)
    loss = lax.psum(loss_row[0, 0], ("x", "y", "c"))

    dh2, d_up_g, d_up_v, d_cw_g, d_cw_v, d_cb_g, d_cb_v, d_down = ffn_bwd(h2, g_up, g_conv, conv_b, w_down_b, df)
    half = N_DEV // 2
    d_up = jnp.concatenate([d_up_g[:half], d_up_v[half:]], axis=0)
    d_cw = jnp.concatenate([d_cw_g[:half], d_cw_v[half:]], axis=0)
    d_cb = jnp.concatenate([d_cb_g[:, :D_FF], d_cb_v[:, D_FF:]], axis=1)
    dx1_ffn, d_n_ffn_pre = tok_bwd("rms_ffn_pre_bwd", rms_tile, [x1], [norm_ffn_pre], [], [dh2], [0])
    dx1 = dy + dx1_ffn
    (d_o, d_r1, d_k1, d_v1, d_gate, d_attn, dx_res, d_n_mix_post, d_ln_g, d_ln_b, d_r_k,
     d_w_out) = tok_bwd("mix_out_bwd", mix_out_tile, mix_tiles, mix_params, [(D_MODEL, D_MODEL)], [dx1],
                        [1, 2, 3, 4])
    d_r2, d_lw, d_k2, d_v2, d_kk, d_a = rwkv_scan_bwd(r_, lw_, k2_, v_, kk_, a_, states, d_o)
    pre_cots = [d_r1 + d_r2, d_lw, d_k1 + d_k2, d_v1 + d_v2, d_kk, d_a, d_gate]
    (d_ps, d_w0, d_wd_pad, d_a0, d_wi_pad, d_wg, d_k_k, d_k_a) = tok_bwd(
        "rwkv_pre_bwd", rwkv_pre_tile, [ps], pre_params, [], pre_cots, [0, 1, 2, 3, 4, 5, 6])
    dq, dkc, dkp, dvc, dvp, d_rel_bias, d_sinks = attn_bwd(proj, rel_bias, bucket, sinks, d_attn)
    zero_blk = jnp.zeros((BLOCK, D_KV), F32)
    dk = dkc + jnp.concatenate([dkp[BLOCK:], zero_blk], axis=0)
    dv = dvc + jnp.concatenate([dvp[BLOCK:], zero_blk], axis=0)
    dpa = jnp.concatenate([dq, dk, dv, d_ps], axis=1)
    dh1, d_w_in, d_mix_ext = in_proj_bwd(h1, w_in_b, mix_ext, proj, dpa)
    dx_pre, d_n_mix_pre = tok_bwd("rms_mix_pre_bwd", rms_tile, [x2], [norm_mix_pre], [], [dh1], [0])
    grad_x = (dx_res + dx_pre)[None]

    small_rep = [d_n_mix_pre, d_n_mix_post, d_n_ffn_pre, d_n_ffn_post, d_rel_bias, d_sinks,
                 d_mix_ext[:, D_QKV:], d_w0, d_a0, d_k_k, d_k_a, d_r_k.reshape(r_k.shape), d_ln_g, d_ln_b, d_cb]
    rep_w = [norm_mix_pre, norm_mix_post, norm_ffn_pre, norm_ffn_post, rel_bias, sinks, rwkv_shift_mix,
             w0, a0, k_k, k_a, r_k, ln_x_g, ln_x_b, conv_b]
    rep_m = [m_norm_mix_pre, m_norm_mix_post, m_norm_ffn_pre, m_norm_ffn_post, m_rel_bias, m_sinks,
             m_rwkv_shift_mix, m_w0, m_a0, m_k_k, m_k_a, m_r_k, m_ln_x_g, m_ln_x_b, m_conv_b]
    rep_v = [v_norm_mix_pre, v_norm_mix_post, v_norm_ffn_pre, v_norm_ffn_post, v_rel_bias, v_sinks,
             v_rwkv_shift_mix, v_w0, v_a0, v_k_k, v_k_a, v_r_k, v_ln_x_g, v_ln_x_b, v_conv_b]
    sh_w = [w_decay_up, w_iclr_up, w_gate_up, conv_w]
    sh_m = [m_w_decay_up, m_w_iclr_up, m_w_gate_up, m_conv_w]
    sh_v = [v_w_decay_up, v_w_iclr_up, v_w_gate_up, v_conv_w]
    sh_parts = [_to_slots(d_wd_pad[:HEAD_DIM], HEAD_DIM), _to_slots(d_wi_pad[HEAD_DIM:], HEAD_DIM),
                _to_slots(d_wg, HEAD_DIM), d_cw]
    small_sh = jnp.stack([_pack([p[j] for p in sh_parts]) for j in range(N_DEV)])
    by_pair = lambda slots: slots.reshape((N_DEV // 2, 2) + slots.shape[1:])
    swapped = pair_exchange(
        [by_pair(_to_slots(d_w_in, D_IN // N_DEV)), by_pair(d_w_out.reshape(N_DEV, D_MODEL // N_DEV, D_MODEL)),
         by_pair(d_up), by_pair(d_down.reshape(N_DEV, D_FF // N_DEV, D_MODEL)), by_pair(small_sh),
         _pack(small_rep)[None]],
        [False, False, False, False, False, True])
    chip_sums = [add_pair("pair_add_w_in", swapped[0], BF16, 512), add_pair("pair_add_w_out", swapped[1], BF16, 128),
                 add_pair("pair_add_w_ffn_up", swapped[2], BF16, 256),
                 add_pair("pair_add_w_ffn_down", swapped[3], BF16, 256),
                 add_pair("pair_add_small", swapped[4], F32, small_sh.shape[1]),
                 add_pair("pair_add_replicated", swapped[5], F32, swapped[5].shape[2])[0]]
    got_in, got_out, got_up, got_down, got_sh, got_rep = chip_exchange(
        chip_sums, [False, False, False, False, False, True])

    big = [adamw("adamw_w_in", w_in[0], got_in, m_w_in[0], v_w_in[0], 256),
           adamw("adamw_w_out", w_out[0], got_out, m_w_out[0], v_w_out[0], 128),
           adamw("adamw_w_ffn_up", w_ffn_up[0], got_up, m_w_ffn_up[0], v_w_ffn_up[0], 128),
           adamw("adamw_w_ffn_down", w_ffn_down[0], got_down, m_w_ffn_down[0], v_w_ffn_down[0], 128)]
    pack2 = lambda rep, sh: jnp.concatenate([_pack(rep), _pack(sh)], axis=0)
    n_small = jnp.concatenate([got_rep, got_sh], axis=1)
    small = adamw("adamw_small", pack2(rep_w, sh_w), n_small, pack2(rep_m, sh_m), pack2(rep_v, sh_v),
                  n_small.shape[1])
    n_rep = got_rep.shape[1]
    small = [_unpack(p[:n_rep], rep_w) + _unpack(p[n_rep:], sh_w) for p in small]

    names = ["norm_mix_pre", "norm_mix_post", "norm_ffn_pre", "norm_ffn_post", "w_in", "rel_bias", "sinks",
             "rwkv_shift_mix", "w0", "w_decay_up", "a0", "w_iclr_up", "w_gate_up", "k_k", "k_a", "r_k",
             "ln_x_g", "ln_x_b", "w_out", "w_ffn_up", "conv_w", "conv_b", "w_ffn_down"]
    small_names = ["norm_mix_pre", "norm_mix_post", "norm_ffn_pre", "norm_ffn_post", "rel_bias", "sinks",
                   "rwkv_shift_mix", "w0", "a0", "k_k", "k_a", "r_k", "ln_x_g", "ln_x_b", "conv_b",
                   "w_decay_up", "w_iclr_up", "w_gate_up", "conv_w"]
    big_names = {"w_in": 0, "w_out": 1, "w_ffn_up": 2, "w_ffn_down": 3}
    outs = []
    for kind in range(4):
        for nm in names:
            if nm in big_names:
                outs.append(big[big_names[nm]][kind][None])
            else:
                outs.append(small[kind][small_names.index(nm)])
    return (loss, grad_x, *outs)
```

```python
import functools
import math

import jax
import jax.numpy as jnp
from jax import lax
from jax.experimental import pallas as pl
from jax.experimental.pallas import tpu as pltpu

F32 = jnp.float32
BF16 = jnp.bfloat16

N_DEV = 8
SEQ = 2048
D_MODEL = 1024
HEAD_DIM = 64
D_ATTN = 512
D_KV = 128
D_RWKV = 512
N_HEADS = 8
RWKV_COLS = 1792
D_QKV = D_ATTN + 2 * D_KV
D_IN = D_QKV + RWKV_COLS
D_FF = 4096
BLOCK = 128
N_BLOCKS = SEQ // BLOCK
N_BUCKETS = 32
MAX_DISTANCE = 128
NORM_EPS = 1e-6
GN_EPS = 64e-5
NEG_INF = -1e30
CHUNK = 64
N_CHUNKS = SEQ // CHUNK
TOK_TILE = 256
FF_TILE = 256
FF_TILE_BWD = 128
COL_TILE = 256
LANES = 128
VMEM_LIMIT = 56 * 1024 * 1024

ADAM_LR = 0.001
ADAM_B1 = 0.9
ADAM_B2 = 0.999
ADAM_EPS = 1e-08
ADAM_WD = 0.01
ADAM_STEP = 10

NT = ((1,), (1,))
TN = ((0,), (0,))
NN = ((1,), (0,))


def _sds(shape, dtype=F32):
    return jax.ShapeDtypeStruct(shape, dtype)


def _params(sem=None):
    if sem is None:
        return pltpu.CompilerParams(vmem_limit_bytes=VMEM_LIMIT)
    return pltpu.CompilerParams(dimension_semantics=sem, vmem_limit_bytes=VMEM_LIMIT)


def _dot(a, b, dims):
    return lax.dot_general(a, b, (dims, ((), ())), preferred_element_type=F32)


def _split2(x):
    hi = x.astype(BF16)
    return hi, (x - hi.astype(F32)).astype(BF16)


def _dot3_raw(a, b, dims):
    ah, al = _split2(a)
    bh, bl = _split2(b)
    return _dot(ah, bh, dims) + (_dot(al, bh, dims) + _dot(ah, bl, dims))


@functools.partial(jax.custom_vjp, nondiff_argnums=(2,))
def dot3(a, b, dims):
    return _dot3_raw(a, b, dims)


def _dot3_fwd(a, b, dims):
    return _dot3_raw(a, b, dims), (a, b)


def _dot3_bwd(dims, res, g):
    a, b = res
    if dims == NN:
        return dot3(g, b, NT), dot3(a, g, TN)
    if dims == NT:
        return dot3(g, b, NN), dot3(g, a, TN)
    return dot3(b, g, NT), dot3(a, g, NN)


dot3.defvjp(_dot3_fwd, _dot3_bwd)


@jax.custom_vjp
def mm(a, b):
    return _dot(a.astype(BF16), b.astype(BF16), NN)


def _mm_fwd(a, b):
    return mm(a, b), (a, b)


def _mm_bwd(res, g):
    a, b = res
    gb = g.astype(BF16)
    return _dot(gb, b.astype(BF16), NT).astype(a.dtype), _dot(a.astype(BF16), gb, TN).astype(b.dtype)


mm.defvjp(_mm_fwd, _mm_bwd)


@jax.custom_vjp
def mm_nt(a, b):
    return _dot(a.astype(BF16), b.astype(BF16), NT)


def _mm_nt_fwd(a, b):
    return mm_nt(a, b), (a, b)


def _mm_nt_bwd(res, g):
    a, b = res
    gb = g.astype(BF16)
    return _dot(gb, b.astype(BF16), NN).astype(a.dtype), _dot(gb, a.astype(BF16), TN).astype(b.dtype)


mm_nt.defvjp(_mm_nt_fwd, _mm_nt_bwd)


@jax.custom_vjp
def mmw(a, w, wz):
    return _dot(a.astype(BF16), w, NN)


def _mmw_fwd(a, w, wz):
    return mmw(a, w, wz), (a, w)


def _mmw_bwd(res, g):
    a, w = res
    gb = g.astype(BF16)
    return _dot(gb, w, NT).astype(a.dtype), jnp.zeros_like(w), _dot(a.astype(BF16), gb, TN)


mmw.defvjp(_mmw_fwd, _mmw_bwd)


def _shift_raw(x, n):
    rows = x.shape[0]
    rolled = pltpu.roll(x, n % rows, 0)
    idx = lax.broadcasted_iota(jnp.int32, x.shape, 0)
    keep = idx >= n if n > 0 else idx < rows + n
    return jnp.where(keep, rolled, 0.0)


@functools.partial(jax.custom_vjp, nondiff_argnums=(1,))
def shift_rows(x, n):
    return _shift_raw(x, n)


def _shift_fwd(x, n):
    return _shift_raw(x, n), None


def _shift_bwd(n, _, g):
    return (_shift_raw(g, -n),)


shift_rows.defvjp(_shift_fwd, _shift_bwd)


def _head_matrix(scale):
    a = lax.broadcasted_iota(jnp.int32, (D_RWKV, D_RWKV), 0) // HEAD_DIM
    b = lax.broadcasted_iota(jnp.int32, (D_RWKV, D_RWKV), 1) // HEAD_DIM
    return jnp.where(a == b, scale, 0.0).astype(F32)


def _rms(x, g):
    return x * lax.rsqrt(jnp.mean(x * x, axis=-1, keepdims=True) + NORM_EPS) * g


def _softplus(x):
    return jnp.maximum(x, 0.0) + jnp.log(1.0 + jnp.exp(-jnp.abs(x)))


def _tile_spec(arr, tm):
    return pl.BlockSpec((tm, arr.shape[1]), lambda i: (i, 0))


def _full_spec(arr):
    nd = arr.ndim
    return pl.BlockSpec(arr.shape, lambda i: (0,) * nd)


def tok_fwd(name, fn, tiles, params, zero_shapes, out_widths, out_dtypes, tm=TOK_TILE):
    n_t, n_p = len(tiles), len(params)

    def body(*refs):
        t_vals = [r[...] for r in refs[:n_t]]
        p_vals = [r[...] for r in refs[n_t:n_t + n_p]]
        z_vals = [jnp.zeros(s, F32) for s in zero_shapes]
        outs = fn(*t_vals, *p_vals, *z_vals)
        for r, o in zip(refs[n_t + n_p:], outs):
            r[...] = o.astype(r.dtype)

    rows = tiles[0].shape[0]
    return pl.pallas_call(
        body, grid=(rows // tm,), name=name,
        in_specs=[_tile_spec(t, tm) for t in tiles] + [_full_spec(p) for p in params],
        out_specs=[pl.BlockSpec((tm, w), lambda i: (i, 0)) for w in out_widths],
        out_shape=[_sds((rows, w), dt) for w, dt in zip(out_widths, out_dtypes)],
        compiler_params=_params(("arbitrary",)),
    )(*tiles, *params)


def tok_bwd(name, fn, tiles, params, zero_shapes, cots, diff_params, tm=TOK_TILE):
    n_t, n_p, n_c = len(tiles), len(params), len(cots)
    acc_shapes = [params[i].shape for i in diff_params] + list(zero_shapes)

    def body(*refs):
        t_vals = [r[...].astype(F32) for r in refs[:n_t]]
        p_vals = [r[...] for r in refs[n_t:n_t + n_p]]
        c_vals = [r[...] for r in refs[n_t + n_p:n_t + n_p + n_c]]
        out_refs = refs[n_t + n_p + n_c:]
        z_vals = [jnp.zeros(s, F32) for s in zero_shapes]
        d_vals = [p_vals[i] for i in diff_params]

        def f(t_in, d_in, z_in):
            full = list(p_vals)
            for i, v in zip(diff_params, d_in):
                full[i] = v
            return tuple(fn(*t_in, *full, *z_in))

        _, vjp = jax.vjp(f, t_vals, d_vals, z_vals)
        g_t, g_d, g_z = vjp(tuple(c_vals))
        for r, g in zip(out_refs[:n_t], g_t):
            r[...] = g.astype(r.dtype)
        acc_refs = out_refs[n_t:]

        @pl.when(pl.program_id(0) == 0)
        def _():
            for r in acc_refs:
                r[...] = jnp.zeros_like(r)

        for r, g in zip(acc_refs, list(g_d) + list(g_z)):
            r[...] += g

    rows = tiles[0].shape[0]
    return pl.pallas_call(
        body, grid=(rows // tm,), name=name,
        in_specs=[_tile_spec(t, tm) for t in tiles] + [_full_spec(p) for p in params]
        + [_tile_spec(c, tm) for c in cots],
        out_specs=[_tile_spec(t, tm) for t in tiles]
        + [pl.BlockSpec(s, lambda i, nd=len(s): (0,) * nd) for s in acc_shapes],
        out_shape=[_sds(t.shape) for t in tiles] + [_sds(s) for s in acc_shapes],
        compiler_params=_params(("arbitrary",)),
    )(*tiles, *params, *cots)


def rms_tile(x, g):
    return (_rms(x, g),)


def rwkv_pre_tile(ps, w0, wd_pad, a0, wi_pad, wg, k_k, k_a):
    r = ps[:, 0:D_RWKV]
    k = ps[:, D_RWKV:2 * D_RWKV]
    v = ps[:, 2 * D_RWKV:3 * D_RWKV]
    z2 = ps[:, 3 * D_RWKV:3 * D_RWKV + LANES]
    zg = ps[:, 3 * D_RWKV + LANES:RWKV_COLS]
    w_log = -_softplus(-(w0 + mm(jnp.tanh(z2), wd_pad))) - 0.5
    lw = -jnp.exp(w_log)
    a = jax.nn.sigmoid(a0 + mm(z2, wi_pad))
    g = mm(jax.nn.sigmoid(zg), wg)
    kk = k * k_k
    norm = jnp.sqrt(dot3(kk * kk, _head_matrix(1.0), NN))
    kk = kk / jnp.maximum(norm, 1e-12)
    k2 = k * (1.0 + (a - 1.0) * k_a)
    return r, lw, k2, v, kk, a, g


def mix_out_tile(o, r, k2, v, g, attn, x, w_out, n_post, ln_g, ln_b, r_k, wz):
    hmean = _head_matrix(1.0 / HEAD_DIM)
    d = o - dot3(o, hmean, NN)
    var = dot3(d * d, hmean, NN)
    on = d * lax.rsqrt(var + GN_EPS) * ln_g + ln_b
    bonus = dot3(r * k2 * r_k, _head_matrix(1.0), NN) * v
    rw = (on + bonus) * g
    mix = mmw(jnp.concatenate([attn, rw], axis=1), w_out, wz)
    return (x + _rms(mix, n_post),)


def in_proj_fwd(h, w_in, mix_ext):
    def body(h_ref, w_ref, m_ref, proj_ref, ps_ref):
        p = _dot(h_ref[...], w_ref[...], NN)
        proj_ref[...] = p
        ps_ref[...] = p + (_shift_raw(p, 1) - p) * m_ref[...]

    n = D_IN // COL_TILE
    first = D_QKV // COL_TILE
    return pl.pallas_call(
        body, grid=(n,), name="in_proj_fwd",
        in_specs=[pl.BlockSpec((SEQ, D_MODEL), lambda j: (0, 0)),
                  pl.BlockSpec((D_MODEL, COL_TILE), lambda j: (0, j)),
                  pl.BlockSpec((1, COL_TILE), lambda j: (0, j))],
        out_specs=[pl.BlockSpec((SEQ, COL_TILE), lambda j: (0, j)),
                   pl.BlockSpec((SEQ, COL_TILE), lambda j: (0, jnp.maximum(j - first, 0)))],
        out_shape=[_sds((SEQ, D_IN)), _sds((SEQ, RWKV_COLS))],
        compiler_params=_params(("arbitrary",)),
    )(h, w_in, mix_ext)


def in_proj_bwd(h, w_in, mix_ext, proj, dpa):
    def body(h_ref, w_ref, m_ref, p_ref, d_ref, dh_ref, dw_ref, dm_ref):
        d = d_ref[...]
        p = p_ref[...]
        dm_ref[...] = jnp.sum(d * (_shift_raw(p, 1) - p), axis=0, keepdims=True)
        dmix = d * m_ref[...]
        dp = (d - dmix + _shift_raw(dmix, -1)).astype(BF16)
        dw_ref[...] = _dot(h_ref[...], dp, TN)

        @pl.when(pl.program_id(0) == 0)
        def _():
            dh_ref[...] = jnp.zeros_like(dh_ref)

        dh_ref[...] += _dot(dp, w_ref[...], NT)

    n = D_IN // COL_TILE
    col = lambda rows: pl.BlockSpec((rows, COL_TILE), lambda j: (0, j))
    return pl.pallas_call(
        body, grid=(n,), name="in_proj_bwd",
        in_specs=[pl.BlockSpec((SEQ, D_MODEL), lambda j: (0, 0)), col(D_MODEL), col(1), col(SEQ), col(SEQ)],
        out_specs=[pl.BlockSpec((SEQ, D_MODEL), lambda j: (0, 0)), col(D_MODEL), col(1)],
        out_shape=[_sds((SEQ, D_MODEL)), _sds((D_MODEL, D_IN)), _sds((1, D_IN))],
        compiler_params=_params(("arbitrary",)),
    )(h, w_in, mix_ext, proj, dpa)


def _bucket_table():
    rel = (jnp.arange(BLOCK)[:, None] + BLOCK) - jnp.arange(2 * BLOCK)[None, :]
    n = jnp.maximum(rel, 0)
    max_exact = N_BUCKETS // 2
    large = max_exact + (jnp.log(jnp.maximum(n, 1).astype(F32) / max_exact)
                         / math.log(MAX_DISTANCE / max_exact) * (N_BUCKETS - max_exact)).astype(jnp.int32)
    large = jnp.minimum(large, N_BUCKETS - 1)
    return jnp.where(n < max_exact, n, large).astype(jnp.int32)


def _select_matrix(g, o):
    a = lax.broadcasted_iota(jnp.int32, (D_KV, D_KV), 0)
    b = lax.broadcasted_iota(jnp.int32, (D_KV, D_KV), 1)
    return ((a - HEAD_DIM * g == b - o) & (b >= o) & (b < o + HEAD_DIM)).astype(F32)


def _attn_block(q, kp, kc, vp, vc, bias, sinks, block_idx):
    kb = jnp.concatenate([kp, kc], axis=0)
    vb = jnp.concatenate([vp, vc], axis=0)
    row = lax.broadcasted_iota(jnp.int32, (BLOCK, 2 * BLOCK), 0)
    col = lax.broadcasted_iota(jnp.int32, (BLOCK, 2 * BLOCK), 1)
    rel = row + BLOCK - col
    mask = (rel >= 0) & (rel < BLOCK) & (col + (block_idx - 1) * BLOCK >= 0)
    lane8 = lax.broadcasted_iota(jnp.int32, (1, N_HEADS), 1)
    kt, vt = {}, {}
    for g in range(2):
        for o in (0, HEAD_DIM):
            sel = _select_matrix(g, o)
            kt[g, o] = mm(kb, sel)
            vt[g, o] = mm(vb, sel)
    outs = []
    for j in range(D_ATTN // LANES):
        qs = q[:, j * LANES:(j + 1) * LANES]
        acc = None
        for half in range(2):
            hq = 2 * j + half
            g, o = hq // 4, half * HEAD_DIM
            s = mm_nt(qs, kt[g, o]) * (HEAD_DIM ** -0.5) + bias[hq]
            s = jnp.where(mask, s, NEG_INF)
            sink = jnp.sum(jnp.where(lane8 == hq, sinks, 0.0), axis=1, keepdims=True)
            m = lax.stop_gradient(jnp.maximum(jnp.max(s, axis=-1, keepdims=True), sink))
            p = jnp.exp(s - m)
            probs = p / (jnp.sum(p, axis=-1, keepdims=True) + jnp.exp(sink - m))
            part = mm(probs, vt[g, o])
            acc = part if acc is None else acc + part
        outs.append(acc)
    return jnp.concatenate(outs, axis=1)


def _build_bias(rb_ref, bucket, bias_ref):
    for hq in range(N_HEADS):
        acc = jnp.zeros((BLOCK, 2 * BLOCK), F32)
        for b in range(N_BUCKETS):
            acc = jnp.where(bucket == b, rb_ref[b, hq], acc)
        bias_ref[hq] = acc


def _attn_in_specs():
    prev = lambda n: jnp.maximum(n - 1, 0)
    return [pl.BlockSpec((BLOCK, D_ATTN), lambda n: (n, 0)),
            pl.BlockSpec((BLOCK, D_KV), lambda n: (prev(n), D_ATTN // D_KV)),
            pl.BlockSpec((BLOCK, D_KV), lambda n: (n, D_ATTN // D_KV)),
            pl.BlockSpec((BLOCK, D_KV), lambda n: (prev(n), D_ATTN // D_KV + 1)),
            pl.BlockSpec((BLOCK, D_KV), lambda n: (n, D_ATTN // D_KV + 1)),
            pl.BlockSpec(memory_space=pltpu.SMEM),
            pl.BlockSpec((BLOCK, 2 * BLOCK), lambda n: (0, 0)),
            pl.BlockSpec((1, N_HEADS), lambda n: (0, 0))]


def attn_fwd(proj, rel_bias, bucket, sinks):
    def body(q_ref, kp_ref, kc_ref, vp_ref, vc_ref, rb_ref, bk_ref, sk_ref, o_ref, bias_ref):
        n = pl.program_id(0)

        @pl.when(n == 0)
        def _():
            _build_bias(rb_ref, bk_ref[...], bias_ref)

        o_ref[...] = _attn_block(q_ref[...], kp_ref[...], kc_ref[...], vp_ref[...], vc_ref[...],
                                 tuple(bias_ref[h] for h in range(N_HEADS)), sk_ref[...], n)

    return pl.pallas_call(
        body, grid=(N_BLOCKS,), name="attn_fwd",
        in_specs=_attn_in_specs(),
        out_specs=pl.BlockSpec((BLOCK, D_ATTN), lambda n: (n, 0)),
        out_shape=_sds((SEQ, D_ATTN)),
        scratch_shapes=[pltpu.VMEM((N_HEADS, BLOCK, 2 * BLOCK), F32)],
        compiler_params=_params(("arbitrary",)),
    )(proj, proj, proj, proj, proj, rel_bias, bucket, sinks)


def attn_bwd(proj, rel_bias, bucket, sinks, d_attn):
    def body(q_ref, kp_ref, kc_ref, vp_ref, vc_ref, rb_ref, bk_ref, sk_ref, do_ref,
             dq_ref, dkc_ref, dkp_ref, dvc_ref, dvp_ref, drb_ref, dsk_ref, bias_ref, dbias_ref):
        n = pl.program_id(0)

        @pl.when(n == 0)
        def _():
            _build_bias(rb_ref, bk_ref[...], bias_ref)
            dbias_ref[...] = jnp.zeros_like(dbias_ref)
            dsk_ref[...] = jnp.zeros_like(dsk_ref)

        f = lambda q, kp, kc, vp, vc, bias, sk: _attn_block(q, kp, kc, vp, vc, bias, sk, n)
        _, vjp = jax.vjp(f, q_ref[...], kp_ref[...], kc_ref[...], vp_ref[...], vc_ref[...],
                         tuple(bias_ref[h] for h in range(N_HEADS)), sk_ref[...])
        dq, dkp, dkc, dvp, dvc, dbias, dsk = vjp(do_ref[...])
        dq_ref[...] = dq
        dkc_ref[...] = dkc
        dkp_ref[...] = dkp
        dvc_ref[...] = dvc
        dvp_ref[...] = dvp
        for h in range(N_HEADS):
            dbias_ref[h] += dbias[h]
        dsk_ref[...] += dsk

        @pl.when(n == N_BLOCKS - 1)
        def _():
            bucket_v = bk_ref[...]
            rowi = lax.broadcasted_iota(jnp.int32, (N_BUCKETS, 2 * BLOCK), 0)
            lane = lax.broadcasted_iota(jnp.int32, (N_BUCKETS, N_HEADS), 1)
            out = jnp.zeros((N_BUCKETS, N_HEADS), F32)
            for hq in range(N_HEADS):
                dbh = dbias_ref[hq]
                rows = jnp.zeros((N_BUCKETS, 2 * BLOCK), F32)
                for b in range(N_BUCKETS):
                    part = jnp.sum(jnp.where(bucket_v == b, dbh, 0.0), axis=0, keepdims=True)
                    rows = jnp.where(rowi == b, part, rows)
                tot = jnp.sum(rows, axis=1, keepdims=True)
                out = jnp.where(lane == hq, tot, out)
            drb_ref[...] = out

    blk = lambda w: pl.BlockSpec((BLOCK, w), lambda n: (n, 0))
    return pl.pallas_call(
        body, grid=(N_BLOCKS,), name="attn_bwd",
        in_specs=_attn_in_specs() + [blk(D_ATTN)],
        out_specs=[blk(D_ATTN), blk(D_KV), blk(D_KV), blk(D_KV), blk(D_KV),
                   pl.BlockSpec((N_BUCKETS, N_HEADS), lambda n: (0, 0)),
                   pl.BlockSpec((1, N_HEADS), lambda n: (0, 0))],
        out_shape=[_sds((SEQ, D_ATTN)), _sds((SEQ, D_KV)), _sds((SEQ, D_KV)), _sds((SEQ, D_KV)),
                   _sds((SEQ, D_KV)), _sds((N_BUCKETS, N_HEADS)), _sds((1, N_HEADS))],
        scratch_shapes=[pltpu.VMEM((N_HEADS, BLOCK, 2 * BLOCK), F32),
                        pltpu.VMEM((N_HEADS, BLOCK, 2 * BLOCK), F32)],
        compiler_params=_params(("arbitrary",)),
    )(proj, proj, proj, proj, proj, rel_bias, bucket, sinks, d_attn)


def _chunk_masks():
    c, hc = CHUNK, N_HEADS * CHUNK
    ri = lax.broadcasted_iota(jnp.int32, (hc, D_RWKV), 0) // c
    li = lax.broadcasted_iota(jnp.int32, (hc, D_RWKV), 1) // HEAD_DIM
    ba = lax.broadcasted_iota(jnp.int32, (hc, hc), 0) // c
    bb = lax.broadcasted_iota(jnp.int32, (hc, hc), 1) // c
    return (ri == li).astype(F32), (ba == bb).astype(F32)


def _bdiag(xc, blocks):
    return jnp.tile(xc, (N_HEADS, 1)) * blocks


def _neumann(l):
    c = CHUNK
    _, blocks = _chunk_masks()
    t = lax.broadcasted_iota(jnp.int32, l.shape, 0)
    i = lax.broadcasted_iota(jnp.int32, l.shape, 1) % c
    inv = (i == t).astype(F32) + l
    pw = dot3(l, _bdiag(l, blocks), NN)
    for _ in range(4):
        both = dot3(jnp.concatenate([inv, pw], axis=0), _bdiag(pw, blocks), NN)
        inv = inv + both[:c]
        pw = both[c:]
    return inv + dot3(inv, _bdiag(pw, blocks), NN)


@jax.custom_vjp
def neumann_inv(l):
    return _neumann(l)


def _neumann_fwd(l):
    inv = _neumann(l)
    return inv, inv


def _neumann_bwd(inv, g):
    c = CHUNK
    _, blocks = _chunk_masks()
    bd_t = _bdiag(inv, blocks).T
    inv_t = bd_t[0:c]
    for h in range(1, N_HEADS):
        inv_t = inv_t + bd_t[h * c:(h + 1) * c]
    return (dot3(dot3(inv_t, _bdiag(g, blocks), NN), bd_t, NN),)


neumann_inv.defvjp(_neumann_fwd, _neumann_bwd)


def _cumsum_raw(x, dims):
    c = x.shape[0]
    tt = lax.broadcasted_iota(jnp.int32, (c, c), 0)
    ii = lax.broadcasted_iota(jnp.int32, (c, c), 1)
    tri = (ii <= tt).astype(BF16)
    hi = x.astype(BF16)
    rest = x - hi.astype(F32)
    mid = rest.astype(BF16)
    lo = (rest - mid.astype(F32)).astype(BF16)
    return _dot(tri, hi, dims) + (_dot(tri, mid, dims) + _dot(tri, lo, dims))


@jax.custom_vjp
def cumsum_rows(x):
    return _cumsum_raw(x, NN)


def _cumsum_fwd(x):
    return _cumsum_raw(x, NN), None


def _cumsum_bwd(_, g):
    return (_cumsum_raw(g, TN),)


cumsum_rows.defvjp(_cumsum_fwd, _cumsum_bwd)


def _rwkv_chunk(s0, r, lw, k, v, kk, a):
    c, hc = CHUNK, N_HEADS * CHUNK
    head_rows, blocks = _chunk_masks()
    t = lax.broadcasted_iota(jnp.int32, (c, hc), 0)
    i = lax.broadcasted_iota(jnp.int32, (c, hc), 1) % c
    strict, incl = i < t, i <= t
    stack = lambda x: jnp.tile(x, (N_HEADS, 1)) * head_rows

    cum = cumsum_rows(lw)
    cum_end = jnp.sum(lw, axis=0, keepdims=True)
    beta = kk * a
    al = -kk * jnp.exp(cum - lw)
    p_inv = jnp.exp(-cum)
    be, kb, rb = beta * p_inv, k * p_inv, r * jnp.exp(cum)
    ar = jnp.concatenate([al, rb], axis=0)
    sv = stack(v)
    l_all = dot3(ar, jnp.concatenate([stack(be), stack(kb)], axis=0), NT)
    l_ab = jnp.where(strict, l_all[:c, :hc], 0.0)
    l_ak = jnp.where(strict, l_all[:c, hc:], 0.0)
    l_rb = jnp.where(incl, l_all[c:, :hc], 0.0)
    l_rk = jnp.where(incl, l_all[c:, hc:], 0.0)
    inv = neumann_inv(l_ab)
    from_s0 = dot3(ar, s0, NT)
    from_v = dot3(jnp.concatenate([l_ak, l_rk], axis=0), sv, NN)
    u = dot3(inv, stack(from_s0[:c] + from_v[:c]), NN)
    o = from_s0[c:] + from_v[c:] + dot3(l_rb, stack(u), NN)
    to_end = jnp.exp(cum_end - cum)
    s1 = s0 * jnp.exp(cum_end) + blocks * dot3(
        jnp.concatenate([u, v], axis=0), jnp.concatenate([beta * to_end, k * to_end], axis=0), TN)
    return o, s1


def call_with_comm(plan, middle_step, body, grid, name, in_specs, out_specs, out_shape, scratch_shapes, operands):
    n_in, n_out, n_scr = len(in_specs), len(out_specs), len(scratch_shapes)
    p_in, p_out = len(plan.ins), len(plan.out_shape)

    def fused(*refs):
        refs = list(refs)
        ins, refs = refs[:n_in], refs[n_in:]
        p_ins, refs = refs[:p_in], refs[p_in:]
        outs, refs = refs[:n_out], refs[n_out:]
        p_outs, refs = refs[:p_out], refs[p_out:]
        scr, p_sems = refs[:n_scr], refs[n_scr:]
        start, middle, finish = plan.stages(p_ins, p_outs, p_sems)
        step = pl.program_id(0)
        pl.when(step == 0)(start)
        body(*ins, *outs, *scr)
        pl.when(step == middle_step)(middle)
        pl.when(step == grid[0] - 1)(finish)

    any_spec = pl.BlockSpec(memory_space=pl.ANY)
    res = pl.pallas_call(
        fused, grid=grid, name=name,
        in_specs=list(in_specs) + [any_spec] * p_in, out_specs=list(out_specs) + [any_spec] * p_out,
        out_shape=list(out_shape) + list(plan.out_shape), scratch_shapes=list(scratch_shapes) + list(plan.scratch),
        compiler_params=_params(("arbitrary",)),
    )(*operands, *plan.ins)
    return res[:n_out], res[n_out:]


def rwkv_scan_fwd(r, lw, k, v, kk, a, plan):
    def body(r_ref, lw_ref, k_ref, v_ref, kk_ref, a_ref, o_ref, st_ref, s_ref):
        @pl.when(pl.program_id(0) == 0)
        def _():
            s_ref[...] = jnp.zeros_like(s_ref)

        s0 = s_ref[...]
        st_ref[0] = s0
        o, s1 = _rwkv_chunk(s0, r_ref[...], lw_ref[...], k_ref[...], v_ref[...], kk_ref[...], a_ref[...])
        o_ref[...] = o
        s_ref[...] = s1

    tb = pl.BlockSpec((CHUNK, D_RWKV), lambda c: (c, 0))
    return call_with_comm(
        plan, 3 * N_CHUNKS // 4, body, (N_CHUNKS,), "rwkv_scan_fwd",
        [tb] * 6, [tb, pl.BlockSpec((1, D_RWKV, D_RWKV), lambda c: (c, 0, 0))],
        [_sds((SEQ, D_RWKV)), _sds((N_CHUNKS, D_RWKV, D_RWKV))],
        [pltpu.VMEM((D_RWKV, D_RWKV), F32)], (r, lw, k, v, kk, a))


def rwkv_scan_bwd(r, lw, k, v, kk, a, states, d_o, plan):
    def body(r_ref, lw_ref, k_ref, v_ref, kk_ref, a_ref, st_ref, do_ref,
             dr_ref, dlw_ref, dk_ref, dv_ref, dkk_ref, da_ref, ds_ref):
        @pl.when(pl.program_id(0) == 0)
        def _():
            ds_ref[...] = jnp.zeros_like(ds_ref)

        _, vjp = jax.vjp(_rwkv_chunk, st_ref[0], r_ref[...], lw_ref[...], k_ref[...], v_ref[...],
                         kk_ref[...], a_ref[...])
        g = vjp((do_ref[...], ds_ref[...]))
        ds_ref[...] = g[0]
        for ref, val in zip((dr_ref, dlw_ref, dk_ref, dv_ref, dkk_ref, da_ref), g[1:]):
            ref[...] = val

    last = N_CHUNKS - 1
    tb = pl.BlockSpec((CHUNK, D_RWKV), lambda c: (last - c, 0))
    return call_with_comm(
        plan, N_CHUNKS // 4, body, (N_CHUNKS,), "rwkv_scan_bwd",
        [tb] * 6 + [pl.BlockSpec((1, D_RWKV, D_RWKV), lambda c: (last - c, 0, 0)), tb], [tb] * 6,
        [_sds((SEQ, D_RWKV))] * 6, [pltpu.VMEM((D_RWKV, D_RWKV), F32)], (r, lw, k, v, kk, a, states, d_o))


def _ffn_mid(ug, uv, cg, cv, bg, bv):
    conv_g = bg + cg[0] * shift_rows(ug, 2) + cg[1] * shift_rows(ug, 1) + cg[2] * ug
    conv_v = bv + cv[0] * shift_rows(uv, 2) + cv[1] * shift_rows(uv, 1) + cv[2] * uv
    return jax.nn.gelu(conv_g, approximate=True) * conv_v


def _conv_rows(ref):
    return tuple(ref[0, j:j + 1, :] for j in range(3))


def _ffn_specs(tile):
    per = D_MODEL // tile
    half = N_DEV // 2
    w_g = pl.BlockSpec((1, D_MODEL, tile), lambda t: (t // per, 0, t % per))
    w_v = pl.BlockSpec((1, D_MODEL, tile), lambda t: (half + t // per, 0, t % per))
    c_g = pl.BlockSpec((1, 3, tile), lambda t: (t // per, 0, t % per))
    c_v = pl.BlockSpec((1, 3, tile), lambda t: (half + t // per, 0, t % per))
    b_g = pl.BlockSpec((1, tile), lambda t: (0, t))
    b_v = pl.BlockSpec((1, tile), lambda t: (0, D_FF // tile + t))
    w_d = pl.BlockSpec((tile, D_MODEL), lambda t: (t, 0))
    return w_g, w_v, c_g, c_v, b_g, b_v, w_d


def ffn_fwd(h2, w_up, conv_w, conv_b, w_down):
    def body(h_ref, wg_ref, wv_ref, cg_ref, cv_ref, bg_ref, bv_ref, wd_ref, f_ref):
        @pl.when(pl.program_id(0) == 0)
        def _():
            f_ref[...] = jnp.zeros_like(f_ref)

        h = h_ref[...]
        act = _ffn_mid(_dot(h, wg_ref[0], NN), _dot(h, wv_ref[0], NN), _conv_rows(cg_ref), _conv_rows(cv_ref),
                       bg_ref[...], bv_ref[...])
        f_ref[...] += _dot(act.astype(BF16), wd_ref[...], NN)

    full = pl.BlockSpec((SEQ, D_MODEL), lambda t: (0, 0))
    return pl.pallas_call(
        body, grid=(D_FF // FF_TILE,), name="ffn_fwd",
        in_specs=[full, *_ffn_specs(FF_TILE)],
        out_specs=full, out_shape=_sds((SEQ, D_MODEL)),
        compiler_params=_params(("arbitrary",)),
    )(h2, w_up, w_up, conv_w, conv_w, conv_b, conv_b, w_down)


def ffn_bwd(h2, w_up, conv_w, conv_b, w_down, df):
    tile = FF_TILE_BWD
    per = D_MODEL // tile

    def body(h_hbm, wg_ref, wv_ref, cg_ref, cv_ref, bg_ref, bv_ref, wd_ref, df_hbm,
             dh_hbm, dup_hbm, dcg_ref, dcv_ref, dbg_ref, dbv_ref, dwd_ref,
             h_ref, df_ref, dh_ref, dwg_ref, dwv_ref, sem, up_sems):
        t = pl.program_id(0)

        @pl.when(t == 0)
        def _():
            pltpu.sync_copy(h_hbm, h_ref)
            pltpu.sync_copy(df_hbm, df_ref)

        h, df_b, wg, wv = h_ref[...], df_ref[...], wg_ref[0], wv_ref[0]
        act, vjp = jax.vjp(_ffn_mid, _dot(h, wg, NN), _dot(h, wv, NN), _conv_rows(cg_ref), _conv_rows(cv_ref),
                           bg_ref[...], bv_ref[...])
        dwd_ref[...] = _dot(act.astype(BF16), df_b, TN)
        dug, duv, dcg, dcv, dbg, dbv = vjp(_dot(df_b, wd_ref[...], NT))
        dug, duv = dug.astype(BF16), duv.astype(BF16)
        cols = pl.ds(pl.multiple_of((t % per) * tile, tile), tile)
        to_gate = pltpu.make_async_copy(dwg_ref, dup_hbm.at[t // per, :, cols], up_sems.at[0])
        to_value = pltpu.make_async_copy(dwv_ref, dup_hbm.at[N_DEV // 2 + t // per, :, cols], up_sems.at[1])
        dwg_ref[...] = _dot(h, dug, TN)
        to_gate.start()
        dwv_ref[...] = _dot(h, duv, TN)
        to_value.start()

        @pl.when(pl.program_id(0) == 0)
        def _():
            dh_ref[...] = jnp.zeros_like(dh_ref)

        dh_ref[...] += _dot(dug, wg, NT) + _dot(duv, wv, NT)
        for j in range(3):
            dcg_ref[0, j:j + 1, :] = dcg[j]
            dcv_ref[0, j:j + 1, :] = dcv[j]
        dbg_ref[...] = dbg
        dbv_ref[...] = dbv
        to_gate.wait()
        to_value.wait()

        @pl.when(t == D_FF // tile - 1)
        def _():
            cp = pltpu.make_async_copy(dh_ref, dh_hbm, sem)
            cp.start()
            cp.wait()

    hbm = pl.BlockSpec(memory_space=pl.ANY)
    w_g, w_v, c_g, c_v, b_g, b_v, w_d = _ffn_specs(tile)
    return pl.pallas_call(
        body, grid=(D_FF // tile,), name="ffn_bwd",
        in_specs=[hbm, w_g, w_v, c_g, c_v, b_g, b_v, w_d, hbm],
        out_specs=[hbm, hbm, c_g, c_v, b_g, b_v, w_d],
        out_shape=[_sds((SEQ, D_MODEL)), _sds((N_DEV, D_MODEL, D_MODEL)),
                   _sds((N_DEV, 3, D_MODEL)), _sds((N_DEV, 3, D_MODEL)), _sds((1, 2 * D_FF)), _sds((1, 2 * D_FF)),
                   _sds((D_FF, D_MODEL))],
        scratch_shapes=[pltpu.VMEM((SEQ, D_MODEL), BF16), pltpu.VMEM((SEQ, D_MODEL), BF16),
                        pltpu.VMEM((SEQ, D_MODEL), F32), pltpu.VMEM((D_MODEL, tile), F32),
                        pltpu.VMEM((D_MODEL, tile), F32), pltpu.SemaphoreType.DMA, pltpu.SemaphoreType.DMA((2,))],
        compiler_params=_params(("arbitrary",)),
    )(h2, w_up, w_up, conv_w, conv_w, conv_b, conv_b, w_down, df)


def loss_head(x1, f, target, n_post):
    def tile_loss(x1_t, f_t, g, tgt):
        err = x1_t + _rms(f_t, g) - tgt
        return 0.5 * jnp.sum(jnp.mean(err * err, axis=-1))

    def body(x_ref, f_ref, t_ref, g_ref, dx_ref, df_ref, dg_ref, loss_ref):
        val, (dx, df, dg) = jax.value_and_grad(tile_loss, argnums=(0, 1, 2))(
            x_ref[...], f_ref[...], g_ref[...], t_ref[...])
        dx_ref[...] = dx
        df_ref[...] = df.astype(BF16)

        @pl.when(pl.program_id(0) == 0)
        def _():
            dg_ref[...] = jnp.zeros_like(dg_ref)
            loss_ref[...] = jnp.zeros_like(loss_ref)

        dg_ref[...] += dg
        loss_ref[...] += jnp.full((1, LANES), val, F32)

    tile = pl.BlockSpec((TOK_TILE, D_MODEL), lambda i: (i, 0))
    vec = pl.BlockSpec((1, D_MODEL), lambda i: (0, 0))
    return pl.pallas_call(
        body, grid=(SEQ // TOK_TILE,), name="loss_head",
        in_specs=[tile, tile, tile, vec],
        out_specs=[tile, tile, vec, pl.BlockSpec((1, LANES), lambda i: (0, 0))],
        out_shape=[_sds((SEQ, D_MODEL)), _sds((SEQ, D_MODEL), BF16), _sds((1, D_MODEL)), _sds((1, LANES))],
        compiler_params=_params(("arbitrary",)),
    )(x1, f, target, n_post)


def _mesh_pos():
    return lax.axis_index("x"), lax.axis_index("y"), lax.axis_index("c")


def _flip(pos, rel):
    x, y, c = pos
    return (1 - x if rel & 4 else x, 1 - y if rel & 2 else y, 1 - c if rel & 1 else c)


def _slot(pos):
    x, y, c = pos
    return 4 * x + 2 * y + c


def cast_bf16(w, rows):
    def body(w_ref, o_ref):
        o_ref[...] = w_ref[...].astype(BF16)

    spec = pl.BlockSpec((rows, w.shape[1]), lambda i: (i, 0))
    return pl.pallas_call(body, grid=(w.shape[0] // rows,), name="cast_bf16_%dx%d" % w.shape,
                          in_specs=[spec], out_specs=spec, out_shape=_sds(w.shape, BF16),
                          compiler_params=_params(("arbitrary",)))(w)


class CommPlan:
    def __init__(self, ins, out_shape, scratch, stages):
        self.ins, self.out_shape, self.scratch, self.stages = ins, out_shape, scratch, stages


def run_comm(name, plan):
    n_in, n_out = len(plan.ins), len(plan.out_shape)

    def body(*refs):
        for stage in plan.stages(refs[:n_in], refs[n_in:n_in + n_out], refs[n_in + n_out:]):
            stage()

    any_spec = pl.BlockSpec(memory_space=pl.ANY)
    return pl.pallas_call(
        body, name=name, in_specs=[any_spec] * len(plan.ins), out_specs=[any_spec] * len(plan.out_shape),
        out_shape=plan.out_shape, scratch_shapes=plan.scratch)(*plan.ins)


def gather_plan(shards):
    n = len(shards)

    def stages(srcs, outs, sems):
        send_sems, recv_sems, local_sems = sems

        def places():
            me = _mesh_pos()
            return me, _flip(me, 1), [_flip(me, 2), _flip(me, 4), _flip(me, 6)]

        def copy(a, k, block, to, src=None):
            dst = outs[a].at[_slot(block)]
            return pltpu.make_async_remote_copy(
                src_ref=dst if src is None else src, dst_ref=dst,
                send_sem=send_sems.at[7 * a + k], recv_sem=recv_sems.at[7 * a + k],
                device_id=to, device_id_type=pl.DeviceIdType.MESH)

        def local(a, me):
            return pltpu.make_async_copy(srcs[a], outs[a].at[_slot(me)], local_sems.at[a])

        def own(a, me, sibling, chips):
            return [copy(a, 0, me, sibling, src=srcs[a])] + [
                copy(a, 1 + j, me, chip, src=srcs[a]) for j, chip in enumerate(chips)]

        def start():
            me, sibling, chips = places()
            for a in range(n):
                local(a, me).start()
                for cp in own(a, me, sibling, chips):
                    cp.start()

        def forward():
            me, sibling, chips = places()
            for j, chip in enumerate(chips):
                for a in range(n):
                    copy(a, 1 + j, chip, me).wait_recv()
                    copy(a, 4 + j, chip, sibling).start()

        def finish():
            me, sibling, chips = places()
            for a in range(n):
                copy(a, 0, sibling, me).wait_recv()
                for j, chip in enumerate(chips):
                    copy(a, 4 + j, _flip(chip, 1), me).wait_recv()
            for a in range(n):
                for cp in own(a, me, sibling, chips):
                    cp.wait_send()
                for j, chip in enumerate(chips):
                    copy(a, 4 + j, chip, sibling).wait_send()
                local(a, me).wait()

        return start, forward, finish

    return CommPlan(list(shards), [_sds((N_DEV,) + s.shape, s.dtype) for s in shards],
                    [pltpu.SemaphoreType.DMA((7 * n,)), pltpu.SemaphoreType.DMA((7 * n,)),
                     pltpu.SemaphoreType.DMA((n,))], stages)


def exchange_plan(parts, replicated, rels, members, index, member_axis):
    n, nr = len(parts), len(rels)
    pick_index = (slice(None),) * member_axis + (0,)
    subs = [1 if (r or member_axis == 0) else p.shape[0] for p, r in zip(parts, replicated)]
    first = [sum(subs[:a]) for a in range(n)]
    total = sum(subs)

    def stages(srcs, outs, sems):
        send_sems, recv_sems, local_sems = sems

        def src(a, s, pos):
            if replicated[a]:
                return srcs[a]
            return srcs[a].at[index(pos)] if member_axis == 0 else srcs[a].at[s, index(pos)]

        def dst(a, s, pos):
            block = outs[a].at[index(pos)]
            return block if (replicated[a] or member_axis == 0) else block.at[s]

        def copy(a, s, j, me, src_pos, dst_pos):
            sem = nr * (first[a] + s) + j
            return pltpu.make_async_remote_copy(
                src_ref=src(a, s, src_pos), dst_ref=dst(a, s, dst_pos),
                send_sem=send_sems.at[sem], recv_sem=recv_sems.at[sem],
                device_id=_flip(me, rels[j]), device_id_type=pl.DeviceIdType.MESH)

        pieces = [(a, s) for a in range(n) for s in range(subs[a])]

        def local(a, s, me):
            return pltpu.make_async_copy(src(a, s, me), dst(a, s, me), local_sems.at[first[a] + s])

        def sends(me):
            return [copy(a, s, j, me, _flip(me, rels[j]), me) for j in range(nr) for a, s in pieces]

        def start():
            me = _mesh_pos()
            for cp in sends(me) + [local(a, s, me) for a, s in pieces]:
                cp.start()

        def middle():
            pass

        def finish():
            me = _mesh_pos()
            for j in range(nr):
                for a, s in pieces:
                    copy(a, s, j, me, me, _flip(me, rels[j])).wait_recv()
            for cp in sends(me):
                cp.wait_send()
            for a, s in pieces:
                local(a, s, me).wait()

        return start, middle, finish

    shapes = [p.shape if r else jax.eval_shape(lambda t: t[pick_index], p).shape for p, r in zip(parts, replicated)]
    return CommPlan(list(parts), [_sds((members,) + s, p.dtype) for s, p in zip(shapes, parts)],
                    [pltpu.SemaphoreType.DMA((nr * total,)), pltpu.SemaphoreType.DMA((nr * total,)),
                     pltpu.SemaphoreType.DMA((total,))], stages)


def pair_plan(parts, replicated):
    return exchange_plan(parts, replicated, [1], 2, lambda pos: pos[2], 1)


def chip_plan(parts, replicated):
    return exchange_plan(parts, replicated, [2, 4, 6], 4, lambda pos: 2 * pos[0] + pos[1], 0)


def add_pair(name, both, out_dtype, rows):
    def body(b_ref, o_ref):
        o_ref[...] = (b_ref[0] + b_ref[1]).astype(o_ref.dtype)

    _, n, r, c = both.shape
    return pl.pallas_call(
        body, grid=(n, r // rows), name=name,
        in_specs=[pl.BlockSpec((2, 1, rows, c), lambda i, j: (0, i, j, 0))],
        out_specs=pl.BlockSpec((1, rows, c), lambda i, j: (i, j, 0)),
        out_shape=_sds((n, r, c), out_dtype),
        compiler_params=_params(("arbitrary", "arbitrary")),
    )(both)


def adamw(name, w, parts, m, v, rows):
    c1 = 1.0 - ADAM_B1 ** ADAM_STEP
    c2 = 1.0 - ADAM_B2 ** ADAM_STEP
    n_parts = parts.shape[0]

    def body(w_ref, p_ref, m_ref, v_ref, g_ref, d_ref, nm_ref, nv_ref):
        g = p_ref[0].astype(F32)
        for j in range(1, n_parts):
            g = g + p_ref[j].astype(F32)
        nm = ADAM_B1 * m_ref[...] + (1.0 - ADAM_B1) * g
        nv = ADAM_B2 * v_ref[...] + (1.0 - ADAM_B2) * (g * g)
        g_ref[...] = g
        nm_ref[...] = nm
        nv_ref[...] = nv
        d_ref[...] = -ADAM_LR * ((nm / c1) / (jnp.sqrt(nv / c2) + ADAM_EPS) + ADAM_WD * w_ref[...])

    cols = w.shape[1]
    spec = pl.BlockSpec((rows, cols), lambda i: (i, 0))
    return pl.pallas_call(
        body, grid=(w.shape[0] // rows,), name=name,
        in_specs=[spec, pl.BlockSpec((n_parts, rows, cols), lambda i: (0, i, 0)), spec, spec],
        out_specs=[spec] * 4, out_shape=[_sds(w.shape)] * 4,
        compiler_params=_params(("arbitrary",)),
    )(w, parts, m, v)


def _rows128(a):
    flat = a.reshape(-1)
    pad = (-flat.shape[0]) % LANES
    if pad:
        flat = jnp.concatenate([flat, jnp.zeros((pad,), flat.dtype)])
    return flat.reshape(-1, LANES)


def _pack(arrays):
    rows = [_rows128(a) for a in arrays]
    pad = (-sum(r.shape[0] for r in rows)) % 8
    return jnp.concatenate(rows + [jnp.zeros((pad, LANES), rows[0].dtype)] * (pad > 0), axis=0)


def _unpack(packed, like):
    out, row = [], 0
    for a in like:
        n = math.prod(a.shape)
        rows = -(-n // LANES)
        out.append(packed[row:row + rows].reshape(-1)[:n].reshape(a.shape))
        row += rows
    return out


def _to_slots(full, per):
    return full.reshape(full.shape[0], N_DEV, per).transpose(1, 0, 2)


def _from_slots(slots):
    return slots.transpose(1, 0, 2).reshape(slots.shape[1], -1)


def kernel(x, norm_mix_pre, norm_mix_post, norm_ffn_pre, norm_ffn_post, w_in, rel_bias, sinks, rwkv_shift_mix, w0, w_decay_up, a0, w_iclr_up, w_gate_up, k_k, k_a, r_k, ln_x_g, ln_x_b, w_out, w_ffn_up, conv_w, conv_b, w_ffn_down, loss_target, m_norm_mix_pre, m_norm_mix_post, m_norm_ffn_pre, m_norm_ffn_post, m_w_in, m_rel_bias, m_sinks, m_rwkv_shift_mix, m_w0, m_w_decay_up, m_a0, m_w_iclr_up, m_w_gate_up, m_k_k, m_k_a, m_r_k, m_ln_x_g, m_ln_x_b, m_w_out, m_w_ffn_up, m_conv_w, m_conv_b, m_w_ffn_down, v_norm_mix_pre, v_norm_mix_post, v_norm_ffn_pre, v_norm_ffn_post, v_w_in, v_rel_bias, v_sinks, v_rwkv_shift_mix, v_w0, v_w_decay_up, v_a0, v_w_iclr_up, v_w_gate_up, v_k_k, v_k_a, v_r_k, v_ln_x_g, v_ln_x_b, v_w_out, v_w_ffn_up, v_conv_w, v_conv_b, v_w_ffn_down):
    x2 = x[0]
    target = loss_target[0]

    g_in, g_out, g_decay, g_iclr, g_gate, g_conv = run_comm("all_gather_mixer", gather_plan([
        cast_bf16(w_in[0], 256), cast_bf16(w_out[0], 128), w_decay_up[0], w_iclr_up[0], w_gate_up[0], conv_w[0]]))
    ffn_gather = gather_plan([cast_bf16(w_ffn_up[0], 256), cast_bf16(w_ffn_down[0], 256)])
    w_in_b = _from_slots(g_in)
    w_out_b = g_out.reshape(D_MODEL, D_MODEL)
    lora = jnp.zeros((HEAD_DIM, D_RWKV), F32)
    wd_pad = jnp.concatenate([_from_slots(g_decay), lora], axis=0)
    wi_pad = jnp.concatenate([lora, _from_slots(g_iclr)], axis=0)
    wg_full = _from_slots(g_gate)
    mix_ext = jnp.concatenate([jnp.zeros((1, D_QKV), F32), rwkv_shift_mix], axis=1)
    r_k_row = r_k.reshape(1, D_RWKV)
    bucket = _bucket_table()

    (h1,) = tok_fwd("rms_mix_pre", rms_tile, [x2], [norm_mix_pre], [], [D_MODEL], [BF16])
    proj, ps = in_proj_fwd(h1, w_in_b, mix_ext)
    attn = attn_fwd(proj, rel_bias, bucket, sinks)
    pre_params = [w0, wd_pad, a0, wi_pad, wg_full, k_k, k_a]
    r_, lw_, k2_, v_, kk_, a_, gate_ = tok_fwd("rwkv_pre", rwkv_pre_tile, [ps], pre_params, [],
                                               [D_RWKV] * 7, [F32] * 7)
    (o_, states), (g_up, g_down) = rwkv_scan_fwd(r_, lw_, k2_, v_, kk_, a_, ffn_gather)
    w_down_b = g_down.reshape(D_FF, D_MODEL)
    mix_tiles = [o_, r_, k2_, v_, gate_, attn, x2]
    mix_params = [w_out_b, norm_mix_post, ln_x_g, ln_x_b, r_k_row]
    (x1,) = tok_fwd("mix_out", mix_out_tile, mix_tiles, mix_params, [(D_MODEL, D_MODEL)], [D_MODEL], [F32])
    (h2,) = tok_fwd("rms_ffn_pre", rms_tile, [x1], [norm_ffn_pre], [], [D_MODEL], [BF16])
    f = ffn_fwd(h2, g_up, g_conv, conv_b, w_down_b)
    dy, df, d_n_ffn_post, loss_row = loss_head(x1, f, target, norm_ffn_post)
    loss = lax.psum(loss_row[0, 0], ("x", "y", "c"))

    dh2, d_up, d_cw_g, d_cw_v, d_cb_g, d_cb_v, d_down = ffn_bwd(h2, g_up, g_conv, conv_b, w_down_b, df)
    half = N_DEV // 2
    d_cw = jnp.concatenate([d_cw_g[:half], d_cw_v[half:]], axis=0)
    by_pair = lambda slots: slots.reshape((N_DEV // 2, 2) + slots.shape[1:])
    ffn_swapped = run_comm("pair_exchange_ffn", pair_plan(
        [by_pair(d_up), by_pair(d_down.reshape(N_DEV, D_FF // N_DEV, D_MODEL))], [False, False]))
    ffn_exchange = chip_plan([add_pair("pair_add_w_ffn_up", ffn_swapped[0], BF16, 256),
                              add_pair("pair_add_w_ffn_down", ffn_swapped[1], BF16, 256)], [False, False])
    d_cb = jnp.concatenate([d_cb_g[:, :D_FF], d_cb_v[:, D_FF:]], axis=1)
    dx1_ffn, d_n_ffn_pre = tok_bwd("rms_ffn_pre_bwd", rms_tile, [x1], [norm_ffn_pre], [], [dh2], [0])
    dx1 = dy + dx1_ffn
    (d_o, d_r1, d_k1, d_v1, d_gate, d_attn, dx_res, d_n_mix_post, d_ln_g, d_ln_b, d_r_k,
     d_w_out) = tok_bwd("mix_out_bwd", mix_out_tile, mix_tiles, mix_params, [(D_MODEL, D_MODEL)], [dx1],
                        [1, 2, 3, 4])
    (d_r2, d_lw, d_k2, d_v2, d_kk, d_a), (got_up, got_down) = rwkv_scan_bwd(
        r_, lw_, k2_, v_, kk_, a_, states, d_o, ffn_exchange)
    pre_cots = [d_r1 + d_r2, d_lw, d_k1 + d_k2, d_v1 + d_v2, d_kk, d_a, d_gate]
    (d_ps, d_w0, d_wd_pad, d_a0, d_wi_pad, d_wg, d_k_k, d_k_a) = tok_bwd(
        "rwkv_pre_bwd", rwkv_pre_tile, [ps], pre_params, [], pre_cots, [0, 1, 2, 3, 4, 5, 6])
    dq, dkc, dkp, dvc, dvp, d_rel_bias, d_sinks = attn_bwd(proj, rel_bias, bucket, sinks, d_attn)
    zero_blk = jnp.zeros((BLOCK, D_KV), F32)
    dk = dkc + jnp.concatenate([dkp[BLOCK:], zero_blk], axis=0)
    dv = dvc + jnp.concatenate([dvp[BLOCK:], zero_blk], axis=0)
    dpa = jnp.concatenate([dq, dk, dv, d_ps], axis=1)
    dh1, d_w_in, d_mix_ext = in_proj_bwd(h1, w_in_b, mix_ext, proj, dpa)
    dx_pre, d_n_mix_pre = tok_bwd("rms_mix_pre_bwd", rms_tile, [x2], [norm_mix_pre], [], [dh1], [0])
    grad_x = (dx_res + dx_pre)[None]

    small_rep = [d_n_mix_pre, d_n_mix_post, d_n_ffn_pre, d_n_ffn_post, d_rel_bias, d_sinks,
                 d_mix_ext[:, D_QKV:], d_w0, d_a0, d_k_k, d_k_a, d_r_k.reshape(r_k.shape), d_ln_g, d_ln_b, d_cb]
    rep_w = [norm_mix_pre, norm_mix_post, norm_ffn_pre, norm_ffn_post, rel_bias, sinks, rwkv_shift_mix,
             w0, a0, k_k, k_a, r_k, ln_x_g, ln_x_b, conv_b]
    rep_m = [m_norm_mix_pre, m_norm_mix_post, m_norm_ffn_pre, m_norm_ffn_post, m_rel_bias, m_sinks,
             m_rwkv_shift_mix, m_w0, m_a0, m_k_k, m_k_a, m_r_k, m_ln_x_g, m_ln_x_b, m_conv_b]
    rep_v = [v_norm_mix_pre, v_norm_mix_post, v_norm_ffn_pre, v_norm_ffn_post, v_rel_bias, v_sinks,
             v_rwkv_shift_mix, v_w0, v_a0, v_k_k, v_k_a, v_r_k, v_ln_x_g, v_ln_x_b, v_conv_b]
    sh_w = [w_decay_up, w_iclr_up, w_gate_up, conv_w]
    sh_m = [m_w_decay_up, m_w_iclr_up, m_w_gate_up, m_conv_w]
    sh_v = [v_w_decay_up, v_w_iclr_up, v_w_gate_up, v_conv_w]
    sh_parts = [_to_slots(d_wd_pad[:HEAD_DIM], HEAD_DIM), _to_slots(d_wi_pad[HEAD_DIM:], HEAD_DIM),
                _to_slots(d_wg, HEAD_DIM), d_cw]
    small_sh = jnp.stack([_pack([p[j] for p in sh_parts]) for j in range(N_DEV)])
    swapped = run_comm("pair_exchange_mixer", pair_plan(
        [by_pair(_to_slots(d_w_in, D_IN // N_DEV)), by_pair(d_w_out.reshape(N_DEV, D_MODEL // N_DEV, D_MODEL)),
         by_pair(small_sh), _pack(small_rep)[None]], [False, False, False, True]))
    chip_sums = [add_pair("pair_add_w_in", swapped[0], BF16, 512), add_pair("pair_add_w_out", swapped[1], BF16, 128),
                 add_pair("pair_add_small", swapped[2], F32, small_sh.shape[1]),
                 add_pair("pair_add_replicated", swapped[3], F32, swapped[3].shape[2])[0]]
    got_in, got_out, got_sh, got_rep = run_comm("chip_exchange_mixer", chip_plan(
        chip_sums, [False, False, False, True]))

    big = [adamw("adamw_w_in", w_in[0], got_in, m_w_in[0], v_w_in[0], 256),
           adamw("adamw_w_out", w_out[0], got_out, m_w_out[0], v_w_out[0], 128),
           adamw("adamw_w_ffn_up", w_ffn_up[0], got_up, m_w_ffn_up[0], v_w_ffn_up[0], 128),
           adamw("adamw_w_ffn_down", w_ffn_down[0], got_down, m_w_ffn_down[0], v_w_ffn_down[0], 128)]
    pack2 = lambda rep, sh: jnp.concatenate([_pack(rep), _pack(sh)], axis=0)
    n_small = jnp.concatenate([got_rep, got_sh], axis=1)
    small = adamw("adamw_small", pack2(rep_w, sh_w), n_small, pack2(rep_m, sh_m), pack2(rep_v, sh_v),
                  n_small.shape[1])
    n_rep = got_rep.shape[1]
    small = [_unpack(p[:n_rep], rep_w) + _unpack(p[n_rep:], sh_w) for p in small]

    names = ["norm_mix_pre", "norm_mix_post", "norm_ffn_pre", "norm_ffn_post", "w_in", "rel_bias", "sinks",
             "rwkv_shift_mix", "w0", "w_decay_up", "a0", "w_iclr_up", "w_gate_up", "k_k", "k_a", "r_k",
             "ln_x_g", "ln_x_b", "w_out", "w_ffn_up", "conv_w", "conv_b", "w_ffn_down"]
    small_names = ["norm_mix_pre", "norm_mix_post", "norm_ffn_pre", "norm_ffn_post", "rel_bias", "sinks",
                   "rwkv_shift_mix", "w0", "a0", "k_k", "k_a", "r_k", "ln_x_g", "ln_x_b", "conv_b",
                   "w_decay_up", "w_iclr_up", "w_gate_up", "conv_w"]
    big_names = {"w_in": 0, "w_out": 1, "w_ffn_up": 2, "w_ffn_down": 3}
    outs = []
    for kind in range(4):
        for nm in names:
            if nm in big_names:
                outs.append(big[big_names[nm]][kind][None])
            else:
                outs.append(small[kind][small_names.index(nm)])
    return (loss, grad_x, *outs)
```

```python
import functools
import math

import jax
import jax.numpy as jnp
from jax import lax
from jax.experimental import pallas as pl
from jax.experimental.pallas import tpu as pltpu

F32 = jnp.float32
BF16 = jnp.bfloat16

N_DEV = 8
SEQ = 2048
D_MODEL = 1024
HEAD_DIM = 64
D_ATTN = 512
D_KV = 128
D_RWKV = 512
N_HEADS = 8
RWKV_COLS = 1792
D_QKV = D_ATTN + 2 * D_KV
D_IN = D_QKV + RWKV_COLS
D_FF = 4096
BLOCK = 128
N_BLOCKS = SEQ // BLOCK
N_BUCKETS = 32
MAX_DISTANCE = 128
NORM_EPS = 1e-6
GN_EPS = 64e-5
NEG_INF = -1e30
CHUNK = 64
N_CHUNKS = SEQ // CHUNK
TOK_TILE = 256
FF_TILE = 256
FF_TILE_BWD = 128
COL_TILE = 256
LANES = 128
VMEM_LIMIT = 56 * 1024 * 1024

ADAM_LR = 0.001
ADAM_B1 = 0.9
ADAM_B2 = 0.999
ADAM_EPS = 1e-08
ADAM_WD = 0.01
ADAM_STEP = 10

NT = ((1,), (1,))
TN = ((0,), (0,))
NN = ((1,), (0,))


def _sds(shape, dtype=F32):
    return jax.ShapeDtypeStruct(shape, dtype)


def _params(sem=None):
    if sem is None:
        return pltpu.CompilerParams(vmem_limit_bytes=VMEM_LIMIT)
    return pltpu.CompilerParams(dimension_semantics=sem, vmem_limit_bytes=VMEM_LIMIT)


def _dot(a, b, dims):
    return lax.dot_general(a, b, (dims, ((), ())), preferred_element_type=F32)


def _split2(x):
    hi = x.astype(BF16)
    return hi, (x - hi.astype(F32)).astype(BF16)


def _dot3_raw(a, b, dims):
    ah, al = _split2(a)
    bh, bl = _split2(b)
    return _dot(ah, bh, dims) + (_dot(al, bh, dims) + _dot(ah, bl, dims))


@functools.partial(jax.custom_vjp, nondiff_argnums=(2,))
def dot3(a, b, dims):
    return _dot3_raw(a, b, dims)


def _dot3_fwd(a, b, dims):
    return _dot3_raw(a, b, dims), (a, b)


def _dot3_bwd(dims, res, g):
    a, b = res
    if dims == NN:
        return dot3(g, b, NT), dot3(a, g, TN)
    if dims == NT:
        return dot3(g, b, NN), dot3(g, a, TN)
    return dot3(b, g, NT), dot3(a, g, NN)


dot3.defvjp(_dot3_fwd, _dot3_bwd)


@jax.custom_vjp
def mm(a, b):
    return _dot(a.astype(BF16), b.astype(BF16), NN)


def _mm_fwd(a, b):
    return mm(a, b), (a, b)


def _mm_bwd(res, g):
    a, b = res
    gb = g.astype(BF16)
    return _dot(gb, b.astype(BF16), NT).astype(a.dtype), _dot(a.astype(BF16), gb, TN).astype(b.dtype)


mm.defvjp(_mm_fwd, _mm_bwd)


@jax.custom_vjp
def mm_nt(a, b):
    return _dot(a.astype(BF16), b.astype(BF16), NT)


def _mm_nt_fwd(a, b):
    return mm_nt(a, b), (a, b)


def _mm_nt_bwd(res, g):
    a, b = res
    gb = g.astype(BF16)
    return _dot(gb, b.astype(BF16), NN).astype(a.dtype), _dot(gb, a.astype(BF16), TN).astype(b.dtype)


mm_nt.defvjp(_mm_nt_fwd, _mm_nt_bwd)


@jax.custom_vjp
def mmw(a, w, wz):
    return _dot(a.astype(BF16), w, NN)


def _mmw_fwd(a, w, wz):
    return mmw(a, w, wz), (a, w)


def _mmw_bwd(res, g):
    a, w = res
    gb = g.astype(BF16)
    return _dot(gb, w, NT).astype(a.dtype), jnp.zeros_like(w), _dot(a.astype(BF16), gb, TN)


mmw.defvjp(_mmw_fwd, _mmw_bwd)


def _shift_raw(x, n):
    rows = x.shape[0]
    rolled = pltpu.roll(x, n % rows, 0)
    idx = lax.broadcasted_iota(jnp.int32, x.shape, 0)
    keep = idx >= n if n > 0 else idx < rows + n
    return jnp.where(keep, rolled, 0.0)


@functools.partial(jax.custom_vjp, nondiff_argnums=(1,))
def shift_rows(x, n):
    return _shift_raw(x, n)


def _shift_fwd(x, n):
    return _shift_raw(x, n), None


def _shift_bwd(n, _, g):
    return (_shift_raw(g, -n),)


shift_rows.defvjp(_shift_fwd, _shift_bwd)


def _head_matrix(scale):
    a = lax.broadcasted_iota(jnp.int32, (D_RWKV, D_RWKV), 0) // HEAD_DIM
    b = lax.broadcasted_iota(jnp.int32, (D_RWKV, D_RWKV), 1) // HEAD_DIM
    return jnp.where(a == b, scale, 0.0).astype(F32)


def _rms(x, g):
    return x * lax.rsqrt(jnp.mean(x * x, axis=-1, keepdims=True) + NORM_EPS) * g


def _softplus(x):
    return jnp.maximum(x, 0.0) + jnp.log(1.0 + jnp.exp(-jnp.abs(x)))


def _tile_spec(arr, tm):
    return pl.BlockSpec((tm, arr.shape[1]), lambda i: (i, 0))


def _full_spec(arr):
    nd = arr.ndim
    return pl.BlockSpec(arr.shape, lambda i: (0,) * nd)


def tok_fwd(name, fn, tiles, params, zero_shapes, out_widths, out_dtypes, tm=TOK_TILE):
    n_t, n_p = len(tiles), len(params)

    def body(*refs):
        t_vals = [r[...] for r in refs[:n_t]]
        p_vals = [r[...] for r in refs[n_t:n_t + n_p]]
        z_vals = [jnp.zeros(s, F32) for s in zero_shapes]
        outs = fn(*t_vals, *p_vals, *z_vals)
        for r, o in zip(refs[n_t + n_p:], outs):
            r[...] = o.astype(r.dtype)

    rows = tiles[0].shape[0]
    return pl.pallas_call(
        body, grid=(rows // tm,), name=name,
        in_specs=[_tile_spec(t, tm) for t in tiles] + [_full_spec(p) for p in params],
        out_specs=[pl.BlockSpec((tm, w), lambda i: (i, 0)) for w in out_widths],
        out_shape=[_sds((rows, w), dt) for w, dt in zip(out_widths, out_dtypes)],
        compiler_params=_params(("arbitrary",)),
    )(*tiles, *params)


def tok_bwd(name, fn, tiles, params, zero_shapes, cots, diff_params, tm=TOK_TILE):
    n_t, n_p, n_c = len(tiles), len(params), len(cots)
    acc_shapes = [params[i].shape for i in diff_params] + list(zero_shapes)

    def body(*refs):
        t_vals = [r[...].astype(F32) for r in refs[:n_t]]
        p_vals = [r[...] for r in refs[n_t:n_t + n_p]]
        c_vals = [r[...] for r in refs[n_t + n_p:n_t + n_p + n_c]]
        out_refs = refs[n_t + n_p + n_c:]
        z_vals = [jnp.zeros(s, F32) for s in zero_shapes]
        d_vals = [p_vals[i] for i in diff_params]

        def f(t_in, d_in, z_in):
            full = list(p_vals)
            for i, v in zip(diff_params, d_in):
                full[i] = v
            return tuple(fn(*t_in, *full, *z_in))

        _, vjp = jax.vjp(f, t_vals, d_vals, z_vals)
        g_t, g_d, g_z = vjp(tuple(c_vals))
        for r, g in zip(out_refs[:n_t], g_t):
            r[...] = g.astype(r.dtype)
        acc_refs = out_refs[n_t:]

        @pl.when(pl.program_id(0) == 0)
        def _():
            for r in acc_refs:
                r[...] = jnp.zeros_like(r)

        for r, g in zip(acc_refs, list(g_d) + list(g_z)):
            r[...] += g

    rows = tiles[0].shape[0]
    return pl.pallas_call(
        body, grid=(rows // tm,), name=name,
        in_specs=[_tile_spec(t, tm) for t in tiles] + [_full_spec(p) for p in params]
        + [_tile_spec(c, tm) for c in cots],
        out_specs=[_tile_spec(t, tm) for t in tiles]
        + [pl.BlockSpec(s, lambda i, nd=len(s): (0,) * nd) for s in acc_shapes],
        out_shape=[_sds(t.shape) for t in tiles] + [_sds(s) for s in acc_shapes],
        compiler_params=_params(("arbitrary",)),
    )(*tiles, *params, *cots)


def rms_tile(x, g):
    return (_rms(x, g),)


def rwkv_pre_tile(ps, w0, wd_pad, a0, wi_pad, wg, k_k, k_a):
    r = ps[:, 0:D_RWKV]
    k = ps[:, D_RWKV:2 * D_RWKV]
    v = ps[:, 2 * D_RWKV:3 * D_RWKV]
    z2 = ps[:, 3 * D_RWKV:3 * D_RWKV + LANES]
    zg = ps[:, 3 * D_RWKV + LANES:RWKV_COLS]
    w_log = -_softplus(-(w0 + mm(jnp.tanh(z2), wd_pad))) - 0.5
    lw = -jnp.exp(w_log)
    a = jax.nn.sigmoid(a0 + mm(z2, wi_pad))
    g = mm(jax.nn.sigmoid(zg), wg)
    kk = k * k_k
    norm = jnp.sqrt(dot3(kk * kk, _head_matrix(1.0), NN))
    kk = kk / jnp.maximum(norm, 1e-12)
    k2 = k * (1.0 + (a - 1.0) * k_a)
    return r, lw, k2, v, kk, a, g


def mix_out_tile(o, r, k2, v, g, attn, x, w_out, n_post, ln_g, ln_b, r_k, wz):
    hmean = _head_matrix(1.0 / HEAD_DIM)
    d = o - dot3(o, hmean, NN)
    var = dot3(d * d, hmean, NN)
    on = d * lax.rsqrt(var + GN_EPS) * ln_g + ln_b
    bonus = dot3(r * k2 * r_k, _head_matrix(1.0), NN) * v
    rw = (on + bonus) * g
    mix = mmw(jnp.concatenate([attn, rw], axis=1), w_out, wz)
    return (x + _rms(mix, n_post),)


def in_proj_fwd(h, w_in, mix_ext):
    def body(h_ref, w_ref, m_ref, proj_ref, ps_ref):
        p = _dot(h_ref[...], w_ref[...], NN)
        proj_ref[...] = p
        ps_ref[...] = p + (_shift_raw(p, 1) - p) * m_ref[...]

    n = D_IN // COL_TILE
    first = D_QKV // COL_TILE
    return pl.pallas_call(
        body, grid=(n,), name="in_proj_fwd",
        in_specs=[pl.BlockSpec((SEQ, D_MODEL), lambda j: (0, 0)),
                  pl.BlockSpec((D_MODEL, COL_TILE), lambda j: (0, j)),
                  pl.BlockSpec((1, COL_TILE), lambda j: (0, j))],
        out_specs=[pl.BlockSpec((SEQ, COL_TILE), lambda j: (0, j)),
                   pl.BlockSpec((SEQ, COL_TILE), lambda j: (0, jnp.maximum(j - first, 0)))],
        out_shape=[_sds((SEQ, D_IN)), _sds((SEQ, RWKV_COLS))],
        compiler_params=_params(("arbitrary",)),
    )(h, w_in, mix_ext)


def in_proj_bwd(h, w_in, mix_ext, proj, dpa):
    def body(h_ref, w_ref, m_ref, p_ref, d_ref, dh_ref, dw_ref, dm_ref):
        d = d_ref[...]
        p = p_ref[...]
        dm_ref[...] = jnp.sum(d * (_shift_raw(p, 1) - p), axis=0, keepdims=True)
        dmix = d * m_ref[...]
        dp = (d - dmix + _shift_raw(dmix, -1)).astype(BF16)
        dw_ref[...] = _dot(h_ref[...], dp, TN)

        @pl.when(pl.program_id(0) == 0)
        def _():
            dh_ref[...] = jnp.zeros_like(dh_ref)

        dh_ref[...] += _dot(dp, w_ref[...], NT)

    n = D_IN // COL_TILE
    col = lambda rows: pl.BlockSpec((rows, COL_TILE), lambda j: (0, j))
    return pl.pallas_call(
        body, grid=(n,), name="in_proj_bwd",
        in_specs=[pl.BlockSpec((SEQ, D_MODEL), lambda j: (0, 0)), col(D_MODEL), col(1), col(SEQ), col(SEQ)],
        out_specs=[pl.BlockSpec((SEQ, D_MODEL), lambda j: (0, 0)), col(D_MODEL), col(1)],
        out_shape=[_sds((SEQ, D_MODEL)), _sds((D_MODEL, D_IN)), _sds((1, D_IN))],
        compiler_params=_params(("arbitrary",)),
    )(h, w_in, mix_ext, proj, dpa)


def _bucket_table():
    rel = (jnp.arange(BLOCK)[:, None] + BLOCK) - jnp.arange(2 * BLOCK)[None, :]
    n = jnp.maximum(rel, 0)
    max_exact = N_BUCKETS // 2
    large = max_exact + (jnp.log(jnp.maximum(n, 1).astype(F32) / max_exact)
                         / math.log(MAX_DISTANCE / max_exact) * (N_BUCKETS - max_exact)).astype(jnp.int32)
    large = jnp.minimum(large, N_BUCKETS - 1)
    return jnp.where(n < max_exact, n, large).astype(jnp.int32)


def _select_matrix(g, o):
    a = lax.broadcasted_iota(jnp.int32, (D_KV, D_KV), 0)
    b = lax.broadcasted_iota(jnp.int32, (D_KV, D_KV), 1)
    return ((a - HEAD_DIM * g == b - o) & (b >= o) & (b < o + HEAD_DIM)).astype(F32)


def _attn_block(q, kp, kc, vp, vc, bias, sinks, block_idx):
    kb = jnp.concatenate([kp, kc], axis=0)
    vb = jnp.concatenate([vp, vc], axis=0)
    row = lax.broadcasted_iota(jnp.int32, (BLOCK, 2 * BLOCK), 0)
    col = lax.broadcasted_iota(jnp.int32, (BLOCK, 2 * BLOCK), 1)
    rel = row + BLOCK - col
    mask = (rel >= 0) & (rel < BLOCK) & (col + (block_idx - 1) * BLOCK >= 0)
    lane8 = lax.broadcasted_iota(jnp.int32, (1, N_HEADS), 1)
    kt, vt = {}, {}
    for g in range(2):
        for o in (0, HEAD_DIM):
            sel = _select_matrix(g, o)
            kt[g, o] = mm(kb, sel)
            vt[g, o] = mm(vb, sel)
    outs = []
    for j in range(D_ATTN // LANES):
        qs = q[:, j * LANES:(j + 1) * LANES]
        acc = None
        for half in range(2):
            hq = 2 * j + half
            g, o = hq // 4, half * HEAD_DIM
            s = mm_nt(qs, kt[g, o]) * (HEAD_DIM ** -0.5) + bias[hq]
            s = jnp.where(mask, s, NEG_INF)
            sink = jnp.sum(jnp.where(lane8 == hq, sinks, 0.0), axis=1, keepdims=True)
            m = lax.stop_gradient(jnp.maximum(jnp.max(s, axis=-1, keepdims=True), sink))
            p = jnp.exp(s - m)
            probs = p / (jnp.sum(p, axis=-1, keepdims=True) + jnp.exp(sink - m))
            part = mm(probs, vt[g, o])
            acc = part if acc is None else acc + part
        outs.append(acc)
    return jnp.concatenate(outs, axis=1)


def _build_bias(rb_ref, bucket, bias_ref):
    for hq in range(N_HEADS):
        acc = jnp.zeros((BLOCK, 2 * BLOCK), F32)
        for b in range(N_BUCKETS):
            acc = jnp.where(bucket == b, rb_ref[b, hq], acc)
        bias_ref[hq] = acc


def _attn_in_specs():
    prev = lambda n: jnp.maximum(n - 1, 0)
    return [pl.BlockSpec((BLOCK, D_ATTN), lambda n: (n, 0)),
            pl.BlockSpec((BLOCK, D_KV), lambda n: (prev(n), D_ATTN // D_KV)),
            pl.BlockSpec((BLOCK, D_KV), lambda n: (n, D_ATTN // D_KV)),
            pl.BlockSpec((BLOCK, D_KV), lambda n: (prev(n), D_ATTN // D_KV + 1)),
            pl.BlockSpec((BLOCK, D_KV), lambda n: (n, D_ATTN // D_KV + 1)),
            pl.BlockSpec(memory_space=pltpu.SMEM),
            pl.BlockSpec((BLOCK, 2 * BLOCK), lambda n: (0, 0)),
            pl.BlockSpec((1, N_HEADS), lambda n: (0, 0))]


def attn_fwd(proj, rel_bias, bucket, sinks):
    def body(q_ref, kp_ref, kc_ref, vp_ref, vc_ref, rb_ref, bk_ref, sk_ref, o_ref, bias_ref):
        n = pl.program_id(0)

        @pl.when(n == 0)
        def _():
            _build_bias(rb_ref, bk_ref[...], bias_ref)

        o_ref[...] = _attn_block(q_ref[...], kp_ref[...], kc_ref[...], vp_ref[...], vc_ref[...],
                                 tuple(bias_ref[h] for h in range(N_HEADS)), sk_ref[...], n)

    return pl.pallas_call(
        body, grid=(N_BLOCKS,), name="attn_fwd",
        in_specs=_attn_in_specs(),
        out_specs=pl.BlockSpec((BLOCK, D_ATTN), lambda n: (n, 0)),
        out_shape=_sds((SEQ, D_ATTN)),
        scratch_shapes=[pltpu.VMEM((N_HEADS, BLOCK, 2 * BLOCK), F32)],
        compiler_params=_params(("arbitrary",)),
    )(proj, proj, proj, proj, proj, rel_bias, bucket, sinks)


def attn_bwd(proj, rel_bias, bucket, sinks, d_attn):
    def body(q_ref, kp_ref, kc_ref, vp_ref, vc_ref, rb_ref, bk_ref, sk_ref, do_ref,
             dq_ref, dkc_ref, dkp_ref, dvc_ref, dvp_ref, drb_ref, dsk_ref, bias_ref, dbias_ref):
        n = pl.program_id(0)

        @pl.when(n == 0)
        def _():
            _build_bias(rb_ref, bk_ref[...], bias_ref)
            dbias_ref[...] = jnp.zeros_like(dbias_ref)
            dsk_ref[...] = jnp.zeros_like(dsk_ref)

        f = lambda q, kp, kc, vp, vc, bias, sk: _attn_block(q, kp, kc, vp, vc, bias, sk, n)
        _, vjp = jax.vjp(f, q_ref[...], kp_ref[...], kc_ref[...], vp_ref[...], vc_ref[...],
                         tuple(bias_ref[h] for h in range(N_HEADS)), sk_ref[...])
        dq, dkp, dkc, dvp, dvc, dbias, dsk = vjp(do_ref[...])
        dq_ref[...] = dq
        dkc_ref[...] = dkc
        dkp_ref[...] = dkp
        dvc_ref[...] = dvc
        dvp_ref[...] = dvp
        for h in range(N_HEADS):
            dbias_ref[h] += dbias[h]
        dsk_ref[...] += dsk

        @pl.when(n == N_BLOCKS - 1)
        def _():
            bucket_v = bk_ref[...]
            rowi = lax.broadcasted_iota(jnp.int32, (N_BUCKETS, 2 * BLOCK), 0)
            lane = lax.broadcasted_iota(jnp.int32, (N_BUCKETS, N_HEADS), 1)
            out = jnp.zeros((N_BUCKETS, N_HEADS), F32)
            for hq in range(N_HEADS):
                dbh = dbias_ref[hq]
                rows = jnp.zeros((N_BUCKETS, 2 * BLOCK), F32)
                for b in range(N_BUCKETS):
                    part = jnp.sum(jnp.where(bucket_v == b, dbh, 0.0), axis=0, keepdims=True)
                    rows = jnp.where(rowi == b, part, rows)
                tot = jnp.sum(rows, axis=1, keepdims=True)
                out = jnp.where(lane == hq, tot, out)
            drb_ref[...] = out

    blk = lambda w: pl.BlockSpec((BLOCK, w), lambda n: (n, 0))
    return pl.pallas_call(
        body, grid=(N_BLOCKS,), name="attn_bwd",
        in_specs=_attn_in_specs() + [blk(D_ATTN)],
        out_specs=[blk(D_ATTN), blk(D_KV), blk(D_KV), blk(D_KV), blk(D_KV),
                   pl.BlockSpec((N_BUCKETS, N_HEADS), lambda n: (0, 0)),
                   pl.BlockSpec((1, N_HEADS), lambda n: (0, 0))],
        out_shape=[_sds((SEQ, D_ATTN)), _sds((SEQ, D_KV)), _sds((SEQ, D_KV)), _sds((SEQ, D_KV)),
                   _sds((SEQ, D_KV)), _sds((N_BUCKETS, N_HEADS)), _sds((1, N_HEADS))],
        scratch_shapes=[pltpu.VMEM((N_HEADS, BLOCK, 2 * BLOCK), F32),
                        pltpu.VMEM((N_HEADS, BLOCK, 2 * BLOCK), F32)],
        compiler_params=_params(("arbitrary",)),
    )(proj, proj, proj, proj, proj, rel_bias, bucket, sinks, d_attn)


def _chunk_masks():
    c, hc = CHUNK, N_HEADS * CHUNK
    ri = lax.broadcasted_iota(jnp.int32, (hc, D_RWKV), 0) // c
    li = lax.broadcasted_iota(jnp.int32, (hc, D_RWKV), 1) // HEAD_DIM
    ba = lax.broadcasted_iota(jnp.int32, (hc, hc), 0) // c
    bb = lax.broadcasted_iota(jnp.int32, (hc, hc), 1) // c
    return (ri == li).astype(F32), (ba == bb).astype(F32)


def _bdiag(xc, blocks):
    return jnp.tile(xc, (N_HEADS, 1)) * blocks


def _neumann(l):
    c = CHUNK
    _, blocks = _chunk_masks()
    t = lax.broadcasted_iota(jnp.int32, l.shape, 0)
    i = lax.broadcasted_iota(jnp.int32, l.shape, 1) % c
    inv = (i == t).astype(F32) + l
    pw = dot3(l, _bdiag(l, blocks), NN)
    for _ in range(4):
        both = dot3(jnp.concatenate([inv, pw], axis=0), _bdiag(pw, blocks), NN)
        inv = inv + both[:c]
        pw = both[c:]
    return inv + dot3(inv, _bdiag(pw, blocks), NN)


@jax.custom_vjp
def neumann_inv(l):
    return _neumann(l)


def _neumann_fwd(l):
    inv = _neumann(l)
    return inv, inv


def _neumann_bwd(inv, g):
    c = CHUNK
    _, blocks = _chunk_masks()
    bd_t = _bdiag(inv, blocks).T
    inv_t = bd_t[0:c]
    for h in range(1, N_HEADS):
        inv_t = inv_t + bd_t[h * c:(h + 1) * c]
    return (dot3(dot3(inv_t, _bdiag(g, blocks), NN), bd_t, NN),)


neumann_inv.defvjp(_neumann_fwd, _neumann_bwd)


def _cumsum_raw(x, dims):
    c = x.shape[0]
    tt = lax.broadcasted_iota(jnp.int32, (c, c), 0)
    ii = lax.broadcasted_iota(jnp.int32, (c, c), 1)
    tri = (ii <= tt).astype(BF16)
    hi = x.astype(BF16)
    rest = x - hi.astype(F32)
    mid = rest.astype(BF16)
    lo = (rest - mid.astype(F32)).astype(BF16)
    return _dot(tri, hi, dims) + (_dot(tri, mid, dims) + _dot(tri, lo, dims))


@jax.custom_vjp
def cumsum_rows(x):
    return _cumsum_raw(x, NN)


def _cumsum_fwd(x):
    return _cumsum_raw(x, NN), None


def _cumsum_bwd(_, g):
    return (_cumsum_raw(g, TN),)


cumsum_rows.defvjp(_cumsum_fwd, _cumsum_bwd)


def _rwkv_chunk(s0, r, lw, k, v, kk, a):
    c, hc = CHUNK, N_HEADS * CHUNK
    head_rows, blocks = _chunk_masks()
    t = lax.broadcasted_iota(jnp.int32, (c, hc), 0)
    i = lax.broadcasted_iota(jnp.int32, (c, hc), 1) % c
    strict, incl = i < t, i <= t
    stack = lambda x: jnp.tile(x, (N_HEADS, 1)) * head_rows

    cum = cumsum_rows(lw)
    cum_end = jnp.sum(lw, axis=0, keepdims=True)
    beta = kk * a
    al = -kk * jnp.exp(cum - lw)
    p_inv = jnp.exp(-cum)
    be, kb, rb = beta * p_inv, k * p_inv, r * jnp.exp(cum)
    ar = jnp.concatenate([al, rb], axis=0)
    sv = stack(v)
    l_all = dot3(ar, jnp.concatenate([stack(be), stack(kb)], axis=0), NT)
    l_ab = jnp.where(strict, l_all[:c, :hc], 0.0)
    l_ak = jnp.where(strict, l_all[:c, hc:], 0.0)
    l_rb = jnp.where(incl, l_all[c:, :hc], 0.0)
    l_rk = jnp.where(incl, l_all[c:, hc:], 0.0)
    inv = neumann_inv(l_ab)
    from_s0 = dot3(ar, s0, NT)
    from_v = dot3(jnp.concatenate([l_ak, l_rk], axis=0), sv, NN)
    u = dot3(inv, stack(from_s0[:c] + from_v[:c]), NN)
    o = from_s0[c:] + from_v[c:] + dot3(l_rb, stack(u), NN)
    to_end = jnp.exp(cum_end - cum)
    s1 = s0 * jnp.exp(cum_end) + blocks * dot3(
        jnp.concatenate([u, v], axis=0), jnp.concatenate([beta * to_end, k * to_end], axis=0), TN)
    return o, s1


def call_with_comm(plan, middle_step, body, grid, name, in_specs, out_specs, out_shape, scratch_shapes, operands):
    n_in, n_out, n_scr = len(in_specs), len(out_specs), len(scratch_shapes)
    p_in, p_out = len(plan.ins), len(plan.out_shape)

    def fused(*refs):
        refs = list(refs)
        ins, refs = refs[:n_in], refs[n_in:]
        p_ins, refs = refs[:p_in], refs[p_in:]
        outs, refs = refs[:n_out], refs[n_out:]
        p_outs, refs = refs[:p_out], refs[p_out:]
        scr, p_sems = refs[:n_scr], refs[n_scr:]
        start, middle, finish = plan.stages(p_ins, p_outs, p_sems)
        step = pl.program_id(0)
        pl.when(step == 0)(start)
        body(*ins, *outs, *scr)
        pl.when(step == middle_step)(middle)
        pl.when(step == grid[0] - 1)(finish)

    any_spec = pl.BlockSpec(memory_space=pl.ANY)
    res = pl.pallas_call(
        fused, grid=grid, name=name,
        in_specs=list(in_specs) + [any_spec] * p_in, out_specs=list(out_specs) + [any_spec] * p_out,
        out_shape=list(out_shape) + list(plan.out_shape), scratch_shapes=list(scratch_shapes) + list(plan.scratch),
        compiler_params=_params(("arbitrary",)),
    )(*operands, *plan.ins)
    return res[:n_out], res[n_out:]


def rwkv_scan_fwd(r, lw, k, v, kk, a, plan):
    def body(r_ref, lw_ref, k_ref, v_ref, kk_ref, a_ref, o_ref, st_ref, s_ref):
        @pl.when(pl.program_id(0) == 0)
        def _():
            s_ref[...] = jnp.zeros_like(s_ref)

        s0 = s_ref[...]
        st_ref[0] = s0
        o, s1 = _rwkv_chunk(s0, r_ref[...], lw_ref[...], k_ref[...], v_ref[...], kk_ref[...], a_ref[...])
        o_ref[...] = o
        s_ref[...] = s1

    tb = pl.BlockSpec((CHUNK, D_RWKV), lambda c: (c, 0))
    return call_with_comm(
        plan, 3 * N_CHUNKS // 4, body, (N_CHUNKS,), "rwkv_scan_fwd",
        [tb] * 6, [tb, pl.BlockSpec((1, D_RWKV, D_RWKV), lambda c: (c, 0, 0))],
        [_sds((SEQ, D_RWKV)), _sds((N_CHUNKS, D_RWKV, D_RWKV))],
        [pltpu.VMEM((D_RWKV, D_RWKV), F32)], (r, lw, k, v, kk, a))


def rwkv_scan_bwd(r, lw, k, v, kk, a, states, d_o, plan):
    def body(r_ref, lw_ref, k_ref, v_ref, kk_ref, a_ref, st_ref, do_ref,
             dr_ref, dlw_ref, dk_ref, dv_ref, dkk_ref, da_ref, ds_ref):
        @pl.when(pl.program_id(0) == 0)
        def _():
            ds_ref[...] = jnp.zeros_like(ds_ref)

        _, vjp = jax.vjp(_rwkv_chunk, st_ref[0], r_ref[...], lw_ref[...], k_ref[...], v_ref[...],
                         kk_ref[...], a_ref[...])
        g = vjp((do_ref[...], ds_ref[...]))
        ds_ref[...] = g[0]
        for ref, val in zip((dr_ref, dlw_ref, dk_ref, dv_ref, dkk_ref, da_ref), g[1:]):
            ref[...] = val

    last = N_CHUNKS - 1
    tb = pl.BlockSpec((CHUNK, D_RWKV), lambda c: (last - c, 0))
    return call_with_comm(
        plan, N_CHUNKS // 4, body, (N_CHUNKS,), "rwkv_scan_bwd",
        [tb] * 6 + [pl.BlockSpec((1, D_RWKV, D_RWKV), lambda c: (last - c, 0, 0)), tb], [tb] * 6,
        [_sds((SEQ, D_RWKV))] * 6, [pltpu.VMEM((D_RWKV, D_RWKV), F32)], (r, lw, k, v, kk, a, states, d_o))


def _ffn_mid(ug, uv, cg, cv, bg, bv):
    conv_g = bg + cg[0] * shift_rows(ug, 2) + cg[1] * shift_rows(ug, 1) + cg[2] * ug
    conv_v = bv + cv[0] * shift_rows(uv, 2) + cv[1] * shift_rows(uv, 1) + cv[2] * uv
    return jax.nn.gelu(conv_g, approximate=True) * conv_v


def _conv_rows(ref):
    return tuple(ref[0, j:j + 1, :] for j in range(3))


def _ffn_specs(tile):
    per = D_MODEL // tile
    half = N_DEV // 2
    w_g = pl.BlockSpec((1, D_MODEL, tile), lambda t: (t // per, 0, t % per))
    w_v = pl.BlockSpec((1, D_MODEL, tile), lambda t: (half + t // per, 0, t % per))
    c_g = pl.BlockSpec((1, 3, tile), lambda t: (t // per, 0, t % per))
    c_v = pl.BlockSpec((1, 3, tile), lambda t: (half + t // per, 0, t % per))
    b_g = pl.BlockSpec((1, tile), lambda t: (0, t))
    b_v = pl.BlockSpec((1, tile), lambda t: (0, D_FF // tile + t))
    w_d = pl.BlockSpec((tile, D_MODEL), lambda t: (t, 0))
    return w_g, w_v, c_g, c_v, b_g, b_v, w_d


def ffn_fwd(h2, w_up, conv_w, conv_b, w_down):
    def body(h_ref, wg_ref, wv_ref, cg_ref, cv_ref, bg_ref, bv_ref, wd_ref, f_ref):
        @pl.when(pl.program_id(0) == 0)
        def _():
            f_ref[...] = jnp.zeros_like(f_ref)

        h = h_ref[...]
        act = _ffn_mid(_dot(h, wg_ref[0], NN), _dot(h, wv_ref[0], NN), _conv_rows(cg_ref), _conv_rows(cv_ref),
                       bg_ref[...], bv_ref[...])
        f_ref[...] += _dot(act.astype(BF16), wd_ref[...], NN)

    full = pl.BlockSpec((SEQ, D_MODEL), lambda t: (0, 0))
    return pl.pallas_call(
        body, grid=(D_FF // FF_TILE,), name="ffn_fwd",
        in_specs=[full, *_ffn_specs(FF_TILE)],
        out_specs=full, out_shape=_sds((SEQ, D_MODEL)),
        compiler_params=_params(("arbitrary",)),
    )(h2, w_up, w_up, conv_w, conv_w, conv_b, conv_b, w_down)


def ffn_bwd(h2, w_up, conv_w, conv_b, w_down, df):
    tile = FF_TILE_BWD
    per = D_MODEL // tile

    def body(h_hbm, wg_ref, wv_ref, cg_ref, cv_ref, bg_ref, bv_ref, wd_ref, df_hbm,
             dh_hbm, dup_hbm, dcg_ref, dcv_ref, dbg_ref, dbv_ref, dwd_ref,
             h_ref, df_ref, dh_ref, dwg_ref, dwv_ref, sem, up_sems):
        t = pl.program_id(0)

        @pl.when(t == 0)
        def _():
            pltpu.sync_copy(h_hbm, h_ref)
            pltpu.sync_copy(df_hbm, df_ref)

        h, df_b, wg, wv = h_ref[...], df_ref[...], wg_ref[0], wv_ref[0]
        act, vjp = jax.vjp(_ffn_mid, _dot(h, wg, NN), _dot(h, wv, NN), _conv_rows(cg_ref), _conv_rows(cv_ref),
                           bg_ref[...], bv_ref[...])
        dwd_ref[...] = _dot(act.astype(BF16), df_b, TN)
        dug, duv, dcg, dcv, dbg, dbv = vjp(_dot(df_b, wd_ref[...], NT))
        dug, duv = dug.astype(BF16), duv.astype(BF16)
        cols = pl.ds(pl.multiple_of((t % per) * tile, tile), tile)
        to_gate = pltpu.make_async_copy(dwg_ref, dup_hbm.at[t // per, :, cols], up_sems.at[0])
        to_value = pltpu.make_async_copy(dwv_ref, dup_hbm.at[N_DEV // 2 + t // per, :, cols], up_sems.at[1])
        dwg_ref[...] = _dot(h, dug, TN)
        to_gate.start()
        dwv_ref[...] = _dot(h, duv, TN)
        to_value.start()

        @pl.when(pl.program_id(0) == 0)
        def _():
            dh_ref[...] = jnp.zeros_like(dh_ref)

        dh_ref[...] += _dot(dug, wg, NT) + _dot(duv, wv, NT)
        for j in range(3):
            dcg_ref[0, j:j + 1, :] = dcg[j]
            dcv_ref[0, j:j + 1, :] = dcv[j]
        dbg_ref[...] = dbg
        dbv_ref[...] = dbv
        to_gate.wait()
        to_value.wait()

        @pl.when(t == D_FF // tile - 1)
        def _():
            cp = pltpu.make_async_copy(dh_ref, dh_hbm, sem)
            cp.start()
            cp.wait()

    hbm = pl.BlockSpec(memory_space=pl.ANY)
    w_g, w_v, c_g, c_v, b_g, b_v, w_d = _ffn_specs(tile)
    return pl.pallas_call(
        body, grid=(D_FF // tile,), name="ffn_bwd",
        in_specs=[hbm, w_g, w_v, c_g, c_v, b_g, b_v, w_d, hbm],
        out_specs=[hbm, hbm, c_g, c_v, b_g, b_v, w_d],
        out_shape=[_sds((SEQ, D_MODEL)), _sds((N_DEV, D_MODEL, D_MODEL)),
                   _sds((N_DEV, 3, D_MODEL)), _sds((N_DEV, 3, D_MODEL)), _sds((1, 2 * D_FF)), _sds((1, 2 * D_FF)),
                   _sds((D_FF, D_MODEL))],
        scratch_shapes=[pltpu.VMEM((SEQ, D_MODEL), BF16), pltpu.VMEM((SEQ, D_MODEL), BF16),
                        pltpu.VMEM((SEQ, D_MODEL), F32), pltpu.VMEM((D_MODEL, tile), F32),
                        pltpu.VMEM((D_MODEL, tile), F32), pltpu.SemaphoreType.DMA, pltpu.SemaphoreType.DMA((2,))],
        compiler_params=_params(("arbitrary",)),
    )(h2, w_up, w_up, conv_w, conv_w, conv_b, conv_b, w_down, df)


def loss_head(x1, f, target, n_post):
    def tile_loss(x1_t, f_t, g, tgt):
        err = x1_t + _rms(f_t, g) - tgt
        return 0.5 * jnp.sum(jnp.mean(err * err, axis=-1))

    def body(x_ref, f_ref, t_ref, g_ref, dx_ref, df_ref, dg_ref, loss_ref):
        val, (dx, df, dg) = jax.value_and_grad(tile_loss, argnums=(0, 1, 2))(
            x_ref[...], f_ref[...], g_ref[...], t_ref[...])
        dx_ref[...] = dx
        df_ref[...] = df.astype(BF16)

        @pl.when(pl.program_id(0) == 0)
        def _():
            dg_ref[...] = jnp.zeros_like(dg_ref)
            loss_ref[...] = jnp.zeros_like(loss_ref)

        dg_ref[...] += dg
        loss_ref[...] += jnp.full((1, LANES), val, F32)

    tile = pl.BlockSpec((TOK_TILE, D_MODEL), lambda i: (i, 0))
    vec = pl.BlockSpec((1, D_MODEL), lambda i: (0, 0))
    return pl.pallas_call(
        body, grid=(SEQ // TOK_TILE,), name="loss_head",
        in_specs=[tile, tile, tile, vec],
        out_specs=[tile, tile, vec, pl.BlockSpec((1, LANES), lambda i: (0, 0))],
        out_shape=[_sds((SEQ, D_MODEL)), _sds((SEQ, D_MODEL), BF16), _sds((1, D_MODEL)), _sds((1, LANES))],
        compiler_params=_params(("arbitrary",)),
    )(x1, f, target, n_post)


def _mesh_pos():
    return lax.axis_index("x"), lax.axis_index("y"), lax.axis_index("c")


def _flip(pos, rel):
    x, y, c = pos
    return (1 - x if rel & 4 else x, 1 - y if rel & 2 else y, 1 - c if rel & 1 else c)


def _slot(pos):
    x, y, c = pos
    return 4 * x + 2 * y + c


def cast_bf16(w, rows):
    def body(w_ref, o_ref):
        o_ref[...] = w_ref[...].astype(BF16)

    spec = pl.BlockSpec((rows, w.shape[1]), lambda i: (i, 0))
    return pl.pallas_call(body, grid=(w.shape[0] // rows,), name="cast_bf16_%dx%d" % w.shape,
                          in_specs=[spec], out_specs=spec, out_shape=_sds(w.shape, BF16),
                          compiler_params=_params(("arbitrary",)))(w)


class CommPlan:
    def __init__(self, ins, out_shape, scratch, stages):
        self.ins, self.out_shape, self.scratch, self.stages = ins, out_shape, scratch, stages


def run_comm(name, plan):
    n_in, n_out = len(plan.ins), len(plan.out_shape)

    def body(*refs):
        for stage in plan.stages(refs[:n_in], refs[n_in:n_in + n_out], refs[n_in + n_out:]):
            stage()

    any_spec = pl.BlockSpec(memory_space=pl.ANY)
    return pl.pallas_call(
        body, name=name, in_specs=[any_spec] * len(plan.ins), out_specs=[any_spec] * len(plan.out_shape),
        out_shape=plan.out_shape, scratch_shapes=plan.scratch)(*plan.ins)


def gather_plan(shards):
    n = len(shards)

    def stages(srcs, outs, sems):
        send_sems, recv_sems, local_sems = sems

        def places():
            me = _mesh_pos()
            return me, _flip(me, 1), [_flip(me, 2), _flip(me, 4), _flip(me, 6)]

        def copy(a, k, block, to, src=None):
            dst = outs[a].at[_slot(block)]
            return pltpu.make_async_remote_copy(
                src_ref=dst if src is None else src, dst_ref=dst,
                send_sem=send_sems.at[7 * a + k], recv_sem=recv_sems.at[7 * a + k],
                device_id=to, device_id_type=pl.DeviceIdType.MESH)

        def local(a, me):
            return pltpu.make_async_copy(srcs[a], outs[a].at[_slot(me)], local_sems.at[a])

        def own(a, me, sibling, chips):
            return [copy(a, 0, me, sibling, src=srcs[a])] + [
                copy(a, 1 + j, me, chip, src=srcs[a]) for j, chip in enumerate(chips)]

        def start():
            me, sibling, chips = places()
            for a in range(n):
                local(a, me).start()
                for cp in own(a, me, sibling, chips):
                    cp.start()

        def forward():
            me, sibling, chips = places()
            for j, chip in enumerate(chips):
                for a in range(n):
                    copy(a, 1 + j, chip, me).wait_recv()
                    copy(a, 4 + j, chip, sibling).start()

        def finish():
            me, sibling, chips = places()
            for a in range(n):
                copy(a, 0, sibling, me).wait_recv()
                for j, chip in enumerate(chips):
                    copy(a, 4 + j, _flip(chip, 1), me).wait_recv()
            for a in range(n):
                for cp in own(a, me, sibling, chips):
                    cp.wait_send()
                for j, chip in enumerate(chips):
                    copy(a, 4 + j, chip, sibling).wait_send()
                local(a, me).wait()

        return start, forward, finish

    return CommPlan(list(shards), [_sds((N_DEV,) + s.shape, s.dtype) for s in shards],
                    [pltpu.SemaphoreType.DMA((7 * n,)), pltpu.SemaphoreType.DMA((7 * n,)),
                     pltpu.SemaphoreType.DMA((n,))], stages)


def exchange_plan(parts, replicated, rels, members, index, member_axis, own_copy):
    n, nr = len(parts), len(rels)
    pick_index = (slice(None),) * member_axis + (0,)
    subs = [1 if (r or member_axis == 0) else p.shape[0] for p, r in zip(parts, replicated)]
    first = [sum(subs[:a]) for a in range(n)]
    total = sum(subs)

    def stages(srcs, outs, sems):
        send_sems, recv_sems, local_sems = sems

        def src(a, s, pos):
            if replicated[a]:
                return srcs[a]
            return srcs[a].at[index(pos)] if member_axis == 0 else srcs[a].at[s, index(pos)]

        def dst(a, s, pos):
            block = outs[a].at[index(pos)]
            return block if (replicated[a] or member_axis == 0) else block.at[s]

        def copy(a, s, j, me, src_pos, dst_pos):
            sem = nr * (first[a] + s) + j
            return pltpu.make_async_remote_copy(
                src_ref=src(a, s, src_pos), dst_ref=dst(a, s, dst_pos),
                send_sem=send_sems.at[sem], recv_sem=recv_sems.at[sem],
                device_id=_flip(me, rels[j]), device_id_type=pl.DeviceIdType.MESH)

        pieces = [(a, s) for a in range(n) for s in range(subs[a])]

        def local(a, s, me):
            return pltpu.make_async_copy(src(a, s, me), dst(a, s, me), local_sems.at[first[a] + s])

        def sends(me):
            return [copy(a, s, j, me, _flip(me, rels[j]), me) for j in range(nr) for a, s in pieces]

        own = pieces if own_copy else []

        def start():
            me = _mesh_pos()
            for cp in sends(me) + [local(a, s, me) for a, s in own]:
                cp.start()

        def middle():
            pass

        def finish():
            me = _mesh_pos()
            for j in range(nr):
                for a, s in pieces:
                    copy(a, s, j, me, me, _flip(me, rels[j])).wait_recv()
            for cp in sends(me):
                cp.wait_send()
            for a, s in own:
                local(a, s, me).wait()

        return start, middle, finish

    shapes = [p.shape if r else jax.eval_shape(lambda t: t[pick_index], p).shape for p, r in zip(parts, replicated)]
    return CommPlan(list(parts), [_sds((members,) + s, p.dtype) for s, p in zip(shapes, parts)],
                    [pltpu.SemaphoreType.DMA((nr * total,)), pltpu.SemaphoreType.DMA((nr * total,)),
                     pltpu.SemaphoreType.DMA((total,))], stages)


def pair_plan(parts, replicated):
    return exchange_plan(parts, replicated, [1], 2, lambda pos: pos[2], 1, False)


def chip_plan(parts, replicated):
    return exchange_plan(parts, replicated, [2, 4, 6], 4, lambda pos: 2 * pos[0] + pos[1], 0, True)


def add_pair(name, mine, swapped, out_dtype, rows):
    def body(m_ref, s_ref, o_ref):
        own = m_ref[0, 0] if mine.ndim == 4 else m_ref[0]
        o_ref[0] = (own + s_ref[0, 0]).astype(o_ref.dtype)

    _, n, r, c = swapped.shape
    core = lambda: lax.axis_index("c")
    if mine.ndim == 4:
        mine_spec = pl.BlockSpec((1, 1, rows, c), lambda i, j: (i, core(), j, 0))
    else:
        mine_spec = pl.BlockSpec((1, rows, c), lambda i, j: (i, j, 0))
    return pl.pallas_call(
        body, grid=(n, r // rows), name=name,
        in_specs=[mine_spec, pl.BlockSpec((1, 1, rows, c), lambda i, j: (1 - core(), i, j, 0))],
        out_specs=pl.BlockSpec((1, rows, c), lambda i, j: (i, j, 0)),
        out_shape=_sds((n, r, c), out_dtype),
        compiler_params=_params(("arbitrary", "arbitrary")),
    )(mine, swapped)


def adamw(name, w, parts, m, v, rows):
    c1 = 1.0 - ADAM_B1 ** ADAM_STEP
    c2 = 1.0 - ADAM_B2 ** ADAM_STEP
    n_parts = parts.shape[0]

    def body(w_ref, p_ref, m_ref, v_ref, g_ref, d_ref, nm_ref, nv_ref):
        g = p_ref[0].astype(F32)
        for j in range(1, n_parts):
            g = g + p_ref[j].astype(F32)
        nm = ADAM_B1 * m_ref[...] + (1.0 - ADAM_B1) * g
        nv = ADAM_B2 * v_ref[...] + (1.0 - ADAM_B2) * (g * g)
        g_ref[...] = g
        nm_ref[...] = nm
        nv_ref[...] = nv
        d_ref[...] = -ADAM_LR * ((nm / c1) / (jnp.sqrt(nv / c2) + ADAM_EPS) + ADAM_WD * w_ref[...])

    cols = w.shape[1]
    spec = pl.BlockSpec((rows, cols), lambda i: (i, 0))
    return pl.pallas_call(
        body, grid=(w.shape[0] // rows,), name=name,
        in_specs=[spec, pl.BlockSpec((n_parts, rows, cols), lambda i: (0, i, 0)), spec, spec],
        out_specs=[spec] * 4, out_shape=[_sds(w.shape)] * 4,
        compiler_params=_params(("arbitrary",)),
    )(w, parts, m, v)


def _rows128(a):
    flat = a.reshape(-1)
    pad = (-flat.shape[0]) % LANES
    if pad:
        flat = jnp.concatenate([flat, jnp.zeros((pad,), flat.dtype)])
    return flat.reshape(-1, LANES)


def _pack(arrays):
    rows = [_rows128(a) for a in arrays]
    pad = (-sum(r.shape[0] for r in rows)) % 8
    return jnp.concatenate(rows + [jnp.zeros((pad, LANES), rows[0].dtype)] * (pad > 0), axis=0)


def _unpack(packed, like):
    out, row = [], 0
    for a in like:
        n = math.prod(a.shape)
        rows = -(-n // LANES)
        out.append(packed[row:row + rows].reshape(-1)[:n].reshape(a.shape))
        row += rows
    return out


def _to_slots(full, per):
    return full.reshape(full.shape[0], N_DEV, per).transpose(1, 0, 2)


def _from_slots(slots):
    return slots.transpose(1, 0, 2).reshape(slots.shape[1], -1)


def kernel(x, norm_mix_pre, norm_mix_post, norm_ffn_pre, norm_ffn_post, w_in, rel_bias, sinks, rwkv_shift_mix, w0, w_decay_up, a0, w_iclr_up, w_gate_up, k_k, k_a, r_k, ln_x_g, ln_x_b, w_out, w_ffn_up, conv_w, conv_b, w_ffn_down, loss_target, m_norm_mix_pre, m_norm_mix_post, m_norm_ffn_pre, m_norm_ffn_post, m_w_in, m_rel_bias, m_sinks, m_rwkv_shift_mix, m_w0, m_w_decay_up, m_a0, m_w_iclr_up, m_w_gate_up, m_k_k, m_k_a, m_r_k, m_ln_x_g, m_ln_x_b, m_w_out, m_w_ffn_up, m_conv_w, m_conv_b, m_w_ffn_down, v_norm_mix_pre, v_norm_mix_post, v_norm_ffn_pre, v_norm_ffn_post, v_w_in, v_rel_bias, v_sinks, v_rwkv_shift_mix, v_w0, v_w_decay_up, v_a0, v_w_iclr_up, v_w_gate_up, v_k_k, v_k_a, v_r_k, v_ln_x_g, v_ln_x_b, v_w_out, v_w_ffn_up, v_conv_w, v_conv_b, v_w_ffn_down):
    x2 = x[0]
    target = loss_target[0]

    g_in, g_out, g_decay, g_iclr, g_gate, g_conv = run_comm("all_gather_mixer", gather_plan([
        cast_bf16(w_in[0], 256), cast_bf16(w_out[0], 128), w_decay_up[0], w_iclr_up[0], w_gate_up[0], conv_w[0]]))
    ffn_gather = gather_plan([cast_bf16(w_ffn_up[0], 256), cast_bf16(w_ffn_down[0], 256)])
    w_in_b = _from_slots(g_in)
    w_out_b = g_out.reshape(D_MODEL, D_MODEL)
    lora = jnp.zeros((HEAD_DIM, D_RWKV), F32)
    wd_pad = jnp.concatenate([_from_slots(g_decay), lora], axis=0)
    wi_pad = jnp.concatenate([lora, _from_slots(g_iclr)], axis=0)
    wg_full = _from_slots(g_gate)
    mix_ext = jnp.concatenate([jnp.zeros((1, D_QKV), F32), rwkv_shift_mix], axis=1)
    r_k_row = r_k.reshape(1, D_RWKV)
    bucket = _bucket_table()

    (h1,) = tok_fwd("rms_mix_pre", rms_tile, [x2], [norm_mix_pre], [], [D_MODEL], [BF16])
    proj, ps = in_proj_fwd(h1, w_in_b, mix_ext)
    attn = attn_fwd(proj, rel_bias, bucket, sinks)
    pre_params = [w0, wd_pad, a0, wi_pad, wg_full, k_k, k_a]
    r_, lw_, k2_, v_, kk_, a_, gate_ = tok_fwd("rwkv_pre", rwkv_pre_tile, [ps], pre_params, [],
                                               [D_RWKV] * 7, [F32] * 7)
    (o_, states), (g_up, g_down) = rwkv_scan_fwd(r_, lw_, k2_, v_, kk_, a_, ffn_gather)
    w_down_b = g_down.reshape(D_FF, D_MODEL)
    mix_tiles = [o_, r_, k2_, v_, gate_, attn, x2]
    mix_params = [w_out_b, norm_mix_post, ln_x_g, ln_x_b, r_k_row]
    (x1,) = tok_fwd("mix_out", mix_out_tile, mix_tiles, mix_params, [(D_MODEL, D_MODEL)], [D_MODEL], [F32])
    (h2,) = tok_fwd("rms_ffn_pre", rms_tile, [x1], [norm_ffn_pre], [], [D_MODEL], [BF16])
    f = ffn_fwd(h2, g_up, g_conv, conv_b, w_down_b)
    dy, df, d_n_ffn_post, loss_row = loss_head(x1, f, target, norm_ffn_post)
    loss = lax.psum(loss_row[0, 0], ("x", "y", "c"))

    dh2, d_up, d_cw_g, d_cw_v, d_cb_g, d_cb_v, d_down = ffn_bwd(h2, g_up, g_conv, conv_b, w_down_b, df)
    half = N_DEV // 2
    d_cw = jnp.concatenate([d_cw_g[:half], d_cw_v[half:]], axis=0)
    by_pair = lambda slots: slots.reshape((N_DEV // 2, 2) + slots.shape[1:])
    ffn_mine = [by_pair(d_up), by_pair(d_down.reshape(N_DEV, D_FF // N_DEV, D_MODEL))]
    ffn_swapped = run_comm("pair_exchange_ffn", pair_plan(ffn_mine, [False, False]))
    ffn_exchange = chip_plan([add_pair("pair_add_w_ffn_up", ffn_mine[0], ffn_swapped[0], BF16, 256),
                              add_pair("pair_add_w_ffn_down", ffn_mine[1], ffn_swapped[1], BF16, 256)],
                             [False, False])
    d_cb = jnp.concatenate([d_cb_g[:, :D_FF], d_cb_v[:, D_FF:]], axis=1)
    dx1_ffn, d_n_ffn_pre = tok_bwd("rms_ffn_pre_bwd", rms_tile, [x1], [norm_ffn_pre], [], [dh2], [0])
    dx1 = dy + dx1_ffn
    (d_o, d_r1, d_k1, d_v1, d_gate, d_attn, dx_res, d_n_mix_post, d_ln_g, d_ln_b, d_r_k,
     d_w_out) = tok_bwd("mix_out_bwd", mix_out_tile, mix_tiles, mix_params, [(D_MODEL, D_MODEL)], [dx1],
                        [1, 2, 3, 4])
    (d_r2, d_lw, d_k2, d_v2, d_kk, d_a), (got_up, got_down) = rwkv_scan_bwd(
        r_, lw_, k2_, v_, kk_, a_, states, d_o, ffn_exchange)
    pre_cots = [d_r1 + d_r2, d_lw, d_k1 + d_k2, d_v1 + d_v2, d_kk, d_a, d_gate]
    (d_ps, d_w0, d_wd_pad, d_a0, d_wi_pad, d_wg, d_k_k, d_k_a) = tok_bwd(
        "rwkv_pre_bwd", rwkv_pre_tile, [ps], pre_params, [], pre_cots, [0, 1, 2, 3, 4, 5, 6])
    dq, dkc, dkp, dvc, dvp, d_rel_bias, d_sinks = attn_bwd(proj, rel_bias, bucket, sinks, d_attn)
    zero_blk = jnp.zeros((BLOCK, D_KV), F32)
    dk = dkc + jnp.concatenate([dkp[BLOCK:], zero_blk], axis=0)
    dv = dvc + jnp.concatenate([dvp[BLOCK:], zero_blk], axis=0)
    dpa = jnp.concatenate([dq, dk, dv, d_ps], axis=1)
    dh1, d_w_in, d_mix_ext = in_proj_bwd(h1, w_in_b, mix_ext, proj, dpa)
    dx_pre, d_n_mix_pre = tok_bwd("rms_mix_pre_bwd", rms_tile, [x2], [norm_mix_pre], [], [dh1], [0])
    grad_x = (dx_res + dx_pre)[None]

    small_rep = [d_n_mix_pre, d_n_mix_post, d_n_ffn_pre, d_n_ffn_post, d_rel_bias, d_sinks,
                 d_mix_ext[:, D_QKV:], d_w0, d_a0, d_k_k, d_k_a, d_r_k.reshape(r_k.shape), d_ln_g, d_ln_b, d_cb]
    rep_w = [norm_mix_pre, norm_mix_post, norm_ffn_pre, norm_ffn_post, rel_bias, sinks, rwkv_shift_mix,
             w0, a0, k_k, k_a, r_k, ln_x_g, ln_x_b, conv_b]
    rep_m = [m_norm_mix_pre, m_norm_mix_post, m_norm_ffn_pre, m_norm_ffn_post, m_rel_bias, m_sinks,
             m_rwkv_shift_mix, m_w0, m_a0, m_k_k, m_k_a, m_r_k, m_ln_x_g, m_ln_x_b, m_conv_b]
    rep_v = [v_norm_mix_pre, v_norm_mix_post, v_norm_ffn_pre, v_norm_ffn_post, v_rel_bias, v_sinks,
             v_rwkv_shift_mix, v_w0, v_a0, v_k_k, v_k_a, v_r_k, v_ln_x_g, v_ln_x_b, v_conv_b]
    sh_w = [w_decay_up, w_iclr_up, w_gate_up, conv_w]
    sh_m = [m_w_decay_up, m_w_iclr_up, m_w_gate_up, m_conv_w]
    sh_v = [v_w_decay_up, v_w_iclr_up, v_w_gate_up, v_conv_w]
    sh_parts = [_to_slots(d_wd_pad[:HEAD_DIM], HEAD_DIM), _to_slots(d_wi_pad[HEAD_DIM:], HEAD_DIM),
                _to_slots(d_wg, HEAD_DIM), d_cw]
    small_sh = jnp.stack([_pack([p[j] for p in sh_parts]) for j in range(N_DEV)])
    mine = [by_pair(_to_slots(d_w_in, D_IN // N_DEV)), by_pair(d_w_out.reshape(N_DEV, D_MODEL // N_DEV, D_MODEL)),
            by_pair(small_sh), _pack(small_rep)[None]]
    swapped = run_comm("pair_exchange_mixer", pair_plan(mine, [False, False, False, True]))
    chip_sums = [add_pair("pair_add_w_in", mine[0], swapped[0], BF16, 512),
                 add_pair("pair_add_w_out", mine[1], swapped[1], BF16, 128),
                 add_pair("pair_add_small", mine[2], swapped[2], F32, small_sh.shape[1]),
                 add_pair("pair_add_replicated", mine[3], swapped[3], F32, mine[3].shape[1])[0]]
    got_in, got_out, got_sh, got_rep = run_comm("chip_exchange_mixer", chip_plan(
        chip_sums, [False, False, False, True]))

    big = [adamw("adamw_w_in", w_in[0], got_in, m_w_in[0], v_w_in[0], 256),
           adamw("adamw_w_out", w_out[0], got_out, m_w_out[0], v_w_out[0], 128),
           adamw("adamw_w_ffn_up", w_ffn_up[0], got_up, m_w_ffn_up[0], v_w_ffn_up[0], 128),
           adamw("adamw_w_ffn_down", w_ffn_down[0], got_down, m_w_ffn_down[0], v_w_ffn_down[0], 128)]
    pack2 = lambda rep, sh: jnp.concatenate([_pack(rep), _pack(sh)], axis=0)
    n_small = jnp.concatenate([got_rep, got_sh], axis=1)
    small = adamw("adamw_small", pack2(rep_w, sh_w), n_small, pack2(rep_m, sh_m), pack2(rep_v, sh_v),
                  n_small.shape[1])
    n_rep = got_rep.shape[1]
    small = [_unpack(p[:n_rep], rep_w) + _unpack(p[n_rep:], sh_w) for p in small]

    names = ["norm_mix_pre", "norm_mix_post", "norm_ffn_pre", "norm_ffn_post", "w_in", "rel_bias", "sinks",
             "rwkv_shift_mix", "w0", "w_decay_up", "a0", "w_iclr_up", "w_gate_up", "k_k", "k_a", "r_k",
             "ln_x_g", "ln_x_b", "w_out", "w_ffn_up", "conv_w", "conv_b", "w_ffn_down"]
    small_names = ["norm_mix_pre", "norm_mix_post", "norm_ffn_pre", "norm_ffn_post", "rel_bias", "sinks",
                   "rwkv_shift_mix", "w0", "a0", "k_k", "k_a", "r_k", "ln_x_g", "ln_x_b", "conv_b",
                   "w_decay_up", "w_iclr_up", "w_gate_up", "conv_w"]
    big_names = {"w_in": 0, "w_out": 1, "w_ffn_up": 2, "w_ffn_down": 3}
    outs = []
    for kind in range(4):
        for nm in names:
            if nm in big_names:
                outs.append(big[big_names[nm]][kind][None])
            else:
                outs.append(small[kind][small_names.index(nm)])
    return (loss, grad_x, *outs)
```

```python
import functools
import math

import jax
import jax.numpy as jnp
from jax import lax
from jax.experimental import pallas as pl
from jax.experimental.pallas import tpu as pltpu

F32 = jnp.float32
BF16 = jnp.bfloat16

N_DEV = 8
SEQ = 2048
D_MODEL = 1024
HEAD_DIM = 64
D_ATTN = 512
D_KV = 128
D_RWKV = 512
N_HEADS = 8
RWKV_COLS = 1792
D_QKV = D_ATTN + 2 * D_KV
D_IN = D_QKV + RWKV_COLS
D_FF = 4096
BLOCK = 128
N_BLOCKS = SEQ // BLOCK
N_BUCKETS = 32
MAX_DISTANCE = 128
NORM_EPS = 1e-6
GN_EPS = 64e-5
NEG_INF = -1e30
CHUNK = 64
N_CHUNKS = SEQ // CHUNK
SCAN_GROUPS = 2
SCAN_WIDTH = D_RWKV // SCAN_GROUPS
TOK_TILE = 256
FF_TILE = 256
FF_TILE_BWD = 128
FF_ROW_CHUNK = 256
FF_HALO = 8
COL_TILE = 256
LANES = 128
VMEM_LIMIT = 56 * 1024 * 1024

ADAM_LR = 0.001
ADAM_B1 = 0.9
ADAM_B2 = 0.999
ADAM_EPS = 1e-08
ADAM_WD = 0.01
ADAM_STEP = 10

NT = ((1,), (1,))
TN = ((0,), (0,))
NN = ((1,), (0,))


def _sds(shape, dtype=F32):
    return jax.ShapeDtypeStruct(shape, dtype)


def _params(sem=None):
    if sem is None:
        return pltpu.CompilerParams(vmem_limit_bytes=VMEM_LIMIT)
    return pltpu.CompilerParams(dimension_semantics=sem, vmem_limit_bytes=VMEM_LIMIT)


def _dot(a, b, dims):
    return lax.dot_general(a, b, (dims, ((), ())), preferred_element_type=F32)


def _split2(x):
    hi = x.astype(BF16)
    return hi, (x - hi.astype(F32)).astype(BF16)


def _dot3_raw(a, b, dims):
    ah, al = _split2(a)
    bh, bl = _split2(b)
    return _dot(ah, bh, dims) + (_dot(al, bh, dims) + _dot(ah, bl, dims))


@functools.partial(jax.custom_vjp, nondiff_argnums=(2,))
def dot3(a, b, dims):
    return _dot3_raw(a, b, dims)


def _dot3_fwd(a, b, dims):
    return _dot3_raw(a, b, dims), (a, b)


def _dot3_bwd(dims, res, g):
    a, b = res
    if dims == NN:
        return dot3(g, b, NT), dot3(a, g, TN)
    if dims == NT:
        return dot3(g, b, NN), dot3(g, a, TN)
    return dot3(b, g, NT), dot3(a, g, NN)


dot3.defvjp(_dot3_fwd, _dot3_bwd)


@jax.custom_vjp
def mm(a, b):
    return _dot(a.astype(BF16), b.astype(BF16), NN)


def _mm_fwd(a, b):
    return mm(a, b), (a, b)


def _mm_bwd(res, g):
    a, b = res
    gb = g.astype(BF16)
    return _dot(gb, b.astype(BF16), NT).astype(a.dtype), _dot(a.astype(BF16), gb, TN).astype(b.dtype)


mm.defvjp(_mm_fwd, _mm_bwd)


@jax.custom_vjp
def mm_nt(a, b):
    return _dot(a.astype(BF16), b.astype(BF16), NT)


def _mm_nt_fwd(a, b):
    return mm_nt(a, b), (a, b)


def _mm_nt_bwd(res, g):
    a, b = res
    gb = g.astype(BF16)
    return _dot(gb, b.astype(BF16), NN).astype(a.dtype), _dot(gb, a.astype(BF16), TN).astype(b.dtype)


mm_nt.defvjp(_mm_nt_fwd, _mm_nt_bwd)


@jax.custom_vjp
def mmw(a, w, wz):
    return _dot(a.astype(BF16), w, NN)


def _mmw_fwd(a, w, wz):
    return mmw(a, w, wz), (a, w)


def _mmw_bwd(res, g):
    a, w = res
    gb = g.astype(BF16)
    return _dot(gb, w, NT).astype(a.dtype), jnp.zeros_like(w), _dot(a.astype(BF16), gb, TN)


mmw.defvjp(_mmw_fwd, _mmw_bwd)


def _shift_raw(x, n):
    rows = x.shape[0]
    rolled = pltpu.roll(x, n % rows, 0)
    idx = lax.broadcasted_iota(jnp.int32, x.shape, 0)
    keep = idx >= n if n > 0 else idx < rows + n
    return jnp.where(keep, rolled, 0.0)


@functools.partial(jax.custom_vjp, nondiff_argnums=(1,))
def shift_rows(x, n):
    return _shift_raw(x, n)


def _shift_fwd(x, n):
    return _shift_raw(x, n), None


def _shift_bwd(n, _, g):
    return (_shift_raw(g, -n),)


shift_rows.defvjp(_shift_fwd, _shift_bwd)


def _head_matrix(scale):
    a = lax.broadcasted_iota(jnp.int32, (D_RWKV, D_RWKV), 0) // HEAD_DIM
    b = lax.broadcasted_iota(jnp.int32, (D_RWKV, D_RWKV), 1) // HEAD_DIM
    return jnp.where(a == b, scale, 0.0).astype(F32)


def _rms(x, g):
    return x * lax.rsqrt(jnp.mean(x * x, axis=-1, keepdims=True) + NORM_EPS) * g


def _softplus(x):
    return jnp.maximum(x, 0.0) + jnp.log(1.0 + jnp.exp(-jnp.abs(x)))


def _tile_spec(arr, tm):
    return pl.BlockSpec((tm, arr.shape[1]), lambda i: (i, 0))


def _full_spec(arr):
    nd = arr.ndim
    return pl.BlockSpec(arr.shape, lambda i: (0,) * nd)


def tok_fwd(name, fn, tiles, params, zero_shapes, out_widths, out_dtypes, tm=TOK_TILE):
    n_t, n_p = len(tiles), len(params)

    def body(*refs):
        t_vals = [r[...] for r in refs[:n_t]]
        p_vals = [r[...] for r in refs[n_t:n_t + n_p]]
        z_vals = [jnp.zeros(s, F32) for s in zero_shapes]
        outs = fn(*t_vals, *p_vals, *z_vals)
        for r, o in zip(refs[n_t + n_p:], outs):
            r[...] = o.astype(r.dtype)

    rows = tiles[0].shape[0]
    return pl.pallas_call(
        body, grid=(rows // tm,), name=name,
        in_specs=[_tile_spec(t, tm) for t in tiles] + [_full_spec(p) for p in params],
        out_specs=[pl.BlockSpec((tm, w), lambda i: (i, 0)) for w in out_widths],
        out_shape=[_sds((rows, w), dt) for w, dt in zip(out_widths, out_dtypes)],
        compiler_params=_params(("arbitrary",)),
    )(*tiles, *params)


def tok_bwd(name, fn, tiles, params, zero_shapes, cots, diff_params, tm=TOK_TILE):
    n_t, n_p, n_c = len(tiles), len(params), len(cots)
    acc_shapes = [params[i].shape for i in diff_params] + list(zero_shapes)

    def body(*refs):
        t_vals = [r[...].astype(F32) for r in refs[:n_t]]
        p_vals = [r[...] for r in refs[n_t:n_t + n_p]]
        c_vals = [r[...] for r in refs[n_t + n_p:n_t + n_p + n_c]]
        out_refs = refs[n_t + n_p + n_c:]
        z_vals = [jnp.zeros(s, F32) for s in zero_shapes]
        d_vals = [p_vals[i] for i in diff_params]

        def f(t_in, d_in, z_in):
            full = list(p_vals)
            for i, v in zip(diff_params, d_in):
                full[i] = v
            return tuple(fn(*t_in, *full, *z_in))

        _, vjp = jax.vjp(f, t_vals, d_vals, z_vals)
        g_t, g_d, g_z = vjp(tuple(c_vals))
        for r, g in zip(out_refs[:n_t], g_t):
            r[...] = g.astype(r.dtype)
        acc_refs = out_refs[n_t:]

        @pl.when(pl.program_id(0) == 0)
        def _():
            for r in acc_refs:
                r[...] = jnp.zeros_like(r)

        for r, g in zip(acc_refs, list(g_d) + list(g_z)):
            r[...] += g

    rows = tiles[0].shape[0]
    return pl.pallas_call(
        body, grid=(rows // tm,), name=name,
        in_specs=[_tile_spec(t, tm) for t in tiles] + [_full_spec(p) for p in params]
        + [_tile_spec(c, tm) for c in cots],
        out_specs=[_tile_spec(t, tm) for t in tiles]
        + [pl.BlockSpec(s, lambda i, nd=len(s): (0,) * nd) for s in acc_shapes],
        out_shape=[_sds(t.shape) for t in tiles] + [_sds(s) for s in acc_shapes],
        compiler_params=_params(("arbitrary",)),
    )(*tiles, *params, *cots)


def rms_tile(x, g):
    return (_rms(x, g),)


def rwkv_pre_tile(ps, w0, wd_pad, a0, wi_pad, wg, k_k, k_a):
    r = ps[:, 0:D_RWKV]
    k = ps[:, D_RWKV:2 * D_RWKV]
    v = ps[:, 2 * D_RWKV:3 * D_RWKV]
    z2 = ps[:, 3 * D_RWKV:3 * D_RWKV + LANES]
    zg = ps[:, 3 * D_RWKV + LANES:RWKV_COLS]
    w_log = -_softplus(-(w0 + mm(jnp.tanh(z2), wd_pad))) - 0.5
    lw = -jnp.exp(w_log)
    a = jax.nn.sigmoid(a0 + mm(z2, wi_pad))
    g = mm(jax.nn.sigmoid(zg), wg)
    kk = k * k_k
    norm = jnp.sqrt(dot3(kk * kk, _head_matrix(1.0), NN))
    kk = kk / jnp.maximum(norm, 1e-12)
    k2 = k * (1.0 + (a - 1.0) * k_a)
    return r, lw, k2, v, kk, a, g


def mix_out_tile(o, r, k2, v, g, attn, x, w_out, n_post, ln_g, ln_b, r_k, wz):
    hmean = _head_matrix(1.0 / HEAD_DIM)
    d = o - dot3(o, hmean, NN)
    var = dot3(d * d, hmean, NN)
    on = d * lax.rsqrt(var + GN_EPS) * ln_g + ln_b
    bonus = dot3(r * k2 * r_k, _head_matrix(1.0), NN) * v
    rw = (on + bonus) * g
    mix = mmw(jnp.concatenate([attn, rw], axis=1), w_out, wz)
    return (x + _rms(mix, n_post),)


def in_proj_fwd(h, w_in, mix_ext):
    def body(h_ref, w_ref, m_ref, proj_ref, ps_ref):
        p = _dot(h_ref[...], w_ref[...], NN)
        proj_ref[...] = p
        ps_ref[...] = p + (_shift_raw(p, 1) - p) * m_ref[...]

    n = D_IN // COL_TILE
    first = D_QKV // COL_TILE
    return pl.pallas_call(
        body, grid=(n,), name="in_proj_fwd",
        in_specs=[pl.BlockSpec((SEQ, D_MODEL), lambda j: (0, 0)),
                  pl.BlockSpec((D_MODEL, COL_TILE), lambda j: (0, j)),
                  pl.BlockSpec((1, COL_TILE), lambda j: (0, j))],
        out_specs=[pl.BlockSpec((SEQ, COL_TILE), lambda j: (0, j)),
                   pl.BlockSpec((SEQ, COL_TILE), lambda j: (0, jnp.maximum(j - first, 0)))],
        out_shape=[_sds((SEQ, D_IN)), _sds((SEQ, RWKV_COLS))],
        compiler_params=_params(("arbitrary",)),
    )(h, w_in, mix_ext)


def in_proj_bwd(h, w_in, mix_ext, proj, dpa):
    def body(h_ref, w_ref, m_ref, p_ref, d_ref, dh_ref, dw_ref, dm_ref):
        d = d_ref[...]
        p = p_ref[...]
        dm_ref[...] = jnp.sum(d * (_shift_raw(p, 1) - p), axis=0, keepdims=True)
        dmix = d * m_ref[...]
        dp = (d - dmix + _shift_raw(dmix, -1)).astype(BF16)
        dw_ref[...] = _dot(h_ref[...], dp, TN)

        @pl.when(pl.program_id(0) == 0)
        def _():
            dh_ref[...] = jnp.zeros_like(dh_ref)

        dh_ref[...] += _dot(dp, w_ref[...], NT)

    n = D_IN // COL_TILE
    col = lambda rows: pl.BlockSpec((rows, COL_TILE), lambda j: (0, j))
    return pl.pallas_call(
        body, grid=(n,), name="in_proj_bwd",
        in_specs=[pl.BlockSpec((SEQ, D_MODEL), lambda j: (0, 0)), col(D_MODEL), col(1), col(SEQ), col(SEQ)],
        out_specs=[pl.BlockSpec((SEQ, D_MODEL), lambda j: (0, 0)), col(D_MODEL), col(1)],
        out_shape=[_sds((SEQ, D_MODEL)), _sds((D_MODEL, D_IN)), _sds((1, D_IN))],
        compiler_params=_params(("arbitrary",)),
    )(h, w_in, mix_ext, proj, dpa)


def _bucket_table():
    rel = (jnp.arange(BLOCK)[:, None] + BLOCK) - jnp.arange(2 * BLOCK)[None, :]
    n = jnp.maximum(rel, 0)
    max_exact = N_BUCKETS // 2
    large = max_exact + (jnp.log(jnp.maximum(n, 1).astype(F32) / max_exact)
                         / math.log(MAX_DISTANCE / max_exact) * (N_BUCKETS - max_exact)).astype(jnp.int32)
    large = jnp.minimum(large, N_BUCKETS - 1)
    return jnp.where(n < max_exact, n, large).astype(jnp.int32)


def _select_matrix(g, o):
    a = lax.broadcasted_iota(jnp.int32, (D_KV, D_KV), 0)
    b = lax.broadcasted_iota(jnp.int32, (D_KV, D_KV), 1)
    return ((a - HEAD_DIM * g == b - o) & (b >= o) & (b < o + HEAD_DIM)).astype(F32)


def _attn_block(q, kp, kc, vp, vc, bias, sinks, block_idx):
    kb = jnp.concatenate([kp, kc], axis=0)
    vb = jnp.concatenate([vp, vc], axis=0)
    row = lax.broadcasted_iota(jnp.int32, (BLOCK, 2 * BLOCK), 0)
    col = lax.broadcasted_iota(jnp.int32, (BLOCK, 2 * BLOCK), 1)
    rel = row + BLOCK - col
    mask = (rel >= 0) & (rel < BLOCK) & (col + (block_idx - 1) * BLOCK >= 0)
    lane8 = lax.broadcasted_iota(jnp.int32, (1, N_HEADS), 1)
    kt, vt = {}, {}
    for g in range(2):
        for o in (0, HEAD_DIM):
            sel = _select_matrix(g, o)
            kt[g, o] = mm(kb, sel)
            vt[g, o] = mm(vb, sel)
    outs = []
    for j in range(D_ATTN // LANES):
        qs = q[:, j * LANES:(j + 1) * LANES]
        acc = None
        for half in range(2):
            hq = 2 * j + half
            g, o = hq // 4, half * HEAD_DIM
            s = mm_nt(qs, kt[g, o]) * (HEAD_DIM ** -0.5) + bias[hq]
            s = jnp.where(mask, s, NEG_INF)
            sink = jnp.sum(jnp.where(lane8 == hq, sinks, 0.0), axis=1, keepdims=True)
            m = lax.stop_gradient(jnp.maximum(jnp.max(s, axis=-1, keepdims=True), sink))
            p = jnp.exp(s - m)
            probs = p / (jnp.sum(p, axis=-1, keepdims=True) + jnp.exp(sink - m))
            part = mm(probs, vt[g, o])
            acc = part if acc is None else acc + part
        outs.append(acc)
    return jnp.concatenate(outs, axis=1)


def _build_bias(rb_ref, bucket, bias_ref):
    for hq in range(N_HEADS):
        acc = jnp.zeros((BLOCK, 2 * BLOCK), F32)
        for b in range(N_BUCKETS):
            acc = jnp.where(bucket == b, rb_ref[b, hq], acc)
        bias_ref[hq] = acc


def _attn_in_specs():
    prev = lambda n: jnp.maximum(n - 1, 0)
    return [pl.BlockSpec((BLOCK, D_ATTN), lambda n: (n, 0)),
            pl.BlockSpec((BLOCK, D_KV), lambda n: (prev(n), D_ATTN // D_KV)),
            pl.BlockSpec((BLOCK, D_KV), lambda n: (n, D_ATTN // D_KV)),
            pl.BlockSpec((BLOCK, D_KV), lambda n: (prev(n), D_ATTN // D_KV + 1)),
            pl.BlockSpec((BLOCK, D_KV), lambda n: (n, D_ATTN // D_KV + 1)),
            pl.BlockSpec(memory_space=pltpu.SMEM),
            pl.BlockSpec((BLOCK, 2 * BLOCK), lambda n: (0, 0)),
            pl.BlockSpec((1, N_HEADS), lambda n: (0, 0))]


def attn_fwd(proj, rel_bias, bucket, sinks):
    def body(q_ref, kp_ref, kc_ref, vp_ref, vc_ref, rb_ref, bk_ref, sk_ref, o_ref, bias_ref):
        n = pl.program_id(0)

        @pl.when(n == 0)
        def _():
            _build_bias(rb_ref, bk_ref[...], bias_ref)

        o_ref[...] = _attn_block(q_ref[...], kp_ref[...], kc_ref[...], vp_ref[...], vc_ref[...],
                                 tuple(bias_ref[h] for h in range(N_HEADS)), sk_ref[...], n)

    return pl.pallas_call(
        body, grid=(N_BLOCKS,), name="attn_fwd",
        in_specs=_attn_in_specs(),
        out_specs=pl.BlockSpec((BLOCK, D_ATTN), lambda n: (n, 0)),
        out_shape=_sds((SEQ, D_ATTN)),
        scratch_shapes=[pltpu.VMEM((N_HEADS, BLOCK, 2 * BLOCK), F32)],
        compiler_params=_params(("arbitrary",)),
    )(proj, proj, proj, proj, proj, rel_bias, bucket, sinks)


def attn_bwd(proj, rel_bias, bucket, sinks, d_attn):
    def body(q_ref, kp_ref, kc_ref, vp_ref, vc_ref, rb_ref, bk_ref, sk_ref, do_ref,
             dq_ref, dkc_ref, dkp_ref, dvc_ref, dvp_ref, drb_ref, dsk_ref, bias_ref, dbias_ref):
        n = pl.program_id(0)

        @pl.when(n == 0)
        def _():
            _build_bias(rb_ref, bk_ref[...], bias_ref)
            dbias_ref[...] = jnp.zeros_like(dbias_ref)
            dsk_ref[...] = jnp.zeros_like(dsk_ref)

        f = lambda q, kp, kc, vp, vc, bias, sk: _attn_block(q, kp, kc, vp, vc, bias, sk, n)
        _, vjp = jax.vjp(f, q_ref[...], kp_ref[...], kc_ref[...], vp_ref[...], vc_ref[...],
                         tuple(bias_ref[h] for h in range(N_HEADS)), sk_ref[...])
        dq, dkp, dkc, dvp, dvc, dbias, dsk = vjp(do_ref[...])
        dq_ref[...] = dq
        dkc_ref[...] = dkc
        dkp_ref[...] = dkp
        dvc_ref[...] = dvc
        dvp_ref[...] = dvp
        for h in range(N_HEADS):
            dbias_ref[h] += dbias[h]
        dsk_ref[...] += dsk

        @pl.when(n == N_BLOCKS - 1)
        def _():
            bucket_v = bk_ref[...]
            rowi = lax.broadcasted_iota(jnp.int32, (N_BUCKETS, 2 * BLOCK), 0)
            lane = lax.broadcasted_iota(jnp.int32, (N_BUCKETS, N_HEADS), 1)
            out = jnp.zeros((N_BUCKETS, N_HEADS), F32)
            for hq in range(N_HEADS):
                dbh = dbias_ref[hq]
                rows = jnp.zeros((N_BUCKETS, 2 * BLOCK), F32)
                for b in range(N_BUCKETS):
                    part = jnp.sum(jnp.where(bucket_v == b, dbh, 0.0), axis=0, keepdims=True)
                    rows = jnp.where(rowi == b, part, rows)
                tot = jnp.sum(rows, axis=1, keepdims=True)
                out = jnp.where(lane == hq, tot, out)
            drb_ref[...] = out

    blk = lambda w: pl.BlockSpec((BLOCK, w), lambda n: (n, 0))
    return pl.pallas_call(
        body, grid=(N_BLOCKS,), name="attn_bwd",
        in_specs=_attn_in_specs() + [blk(D_ATTN)],
        out_specs=[blk(D_ATTN), blk(D_KV), blk(D_KV), blk(D_KV), blk(D_KV),
                   pl.BlockSpec((N_BUCKETS, N_HEADS), lambda n: (0, 0)),
                   pl.BlockSpec((1, N_HEADS), lambda n: (0, 0))],
        out_shape=[_sds((SEQ, D_ATTN)), _sds((SEQ, D_KV)), _sds((SEQ, D_KV)), _sds((SEQ, D_KV)),
                   _sds((SEQ, D_KV)), _sds((N_BUCKETS, N_HEADS)), _sds((1, N_HEADS))],
        scratch_shapes=[pltpu.VMEM((N_HEADS, BLOCK, 2 * BLOCK), F32),
                        pltpu.VMEM((N_HEADS, BLOCK, 2 * BLOCK), F32)],
        compiler_params=_params(("arbitrary",)),
    )(proj, proj, proj, proj, proj, rel_bias, bucket, sinks, d_attn)


def _chunk_masks(heads):
    c, hc = CHUNK, heads * CHUNK
    ri = lax.broadcasted_iota(jnp.int32, (hc, heads * HEAD_DIM), 0) // c
    li = lax.broadcasted_iota(jnp.int32, (hc, heads * HEAD_DIM), 1) // HEAD_DIM
    ba = lax.broadcasted_iota(jnp.int32, (hc, hc), 0) // c
    bb = lax.broadcasted_iota(jnp.int32, (hc, hc), 1) // c
    return (ri == li).astype(F32), (ba == bb).astype(F32)


def _bdiag(xc, blocks):
    return jnp.tile(xc, (xc.shape[1] // CHUNK, 1)) * blocks


def _neumann(l):
    c = CHUNK
    _, blocks = _chunk_masks(l.shape[1] // c)
    t = lax.broadcasted_iota(jnp.int32, l.shape, 0)
    i = lax.broadcasted_iota(jnp.int32, l.shape, 1) % c
    inv = (i == t).astype(F32) + l
    pw = dot3(l, _bdiag(l, blocks), NN)
    for _ in range(4):
        both = dot3(jnp.concatenate([inv, pw], axis=0), _bdiag(pw, blocks), NN)
        inv = inv + both[:c]
        pw = both[c:]
    return inv + dot3(inv, _bdiag(pw, blocks), NN)


@jax.custom_vjp
def neumann_inv(l):
    return _neumann(l)


def _neumann_fwd(l):
    inv = _neumann(l)
    return inv, inv


def _neumann_bwd(inv, g):
    c = CHUNK
    heads = inv.shape[1] // c
    _, blocks = _chunk_masks(heads)
    bd_t = _bdiag(inv, blocks).T
    inv_t = bd_t[0:c]
    for h in range(1, heads):
        inv_t = inv_t + bd_t[h * c:(h + 1) * c]
    return (dot3(dot3(inv_t, _bdiag(g, blocks), NN), bd_t, NN),)


neumann_inv.defvjp(_neumann_fwd, _neumann_bwd)


def _cumsum_raw(x, dims):
    c = x.shape[0]
    tt = lax.broadcasted_iota(jnp.int32, (c, c), 0)
    ii = lax.broadcasted_iota(jnp.int32, (c, c), 1)
    tri = (ii <= tt).astype(BF16)
    hi = x.astype(BF16)
    rest = x - hi.astype(F32)
    mid = rest.astype(BF16)
    lo = (rest - mid.astype(F32)).astype(BF16)
    return _dot(tri, hi, dims) + (_dot(tri, mid, dims) + _dot(tri, lo, dims))


@jax.custom_vjp
def cumsum_rows(x):
    return _cumsum_raw(x, NN)


def _cumsum_fwd(x):
    return _cumsum_raw(x, NN), None


def _cumsum_bwd(_, g):
    return (_cumsum_raw(g, TN),)


cumsum_rows.defvjp(_cumsum_fwd, _cumsum_bwd)


def _rwkv_chunk(s0, r, lw, k, v, kk, a):
    heads = r.shape[1] // HEAD_DIM
    c, hc = CHUNK, heads * CHUNK
    head_rows, blocks = _chunk_masks(heads)
    t = lax.broadcasted_iota(jnp.int32, (c, hc), 0)
    i = lax.broadcasted_iota(jnp.int32, (c, hc), 1) % c
    strict, incl = i < t, i <= t
    stack = lambda x: jnp.tile(x, (heads, 1)) * head_rows

    cum = cumsum_rows(lw)
    cum_end = jnp.sum(lw, axis=0, keepdims=True)
    beta = kk * a
    al = -kk * jnp.exp(cum - lw)
    p_inv = jnp.exp(-cum)
    be, kb, rb = beta * p_inv, k * p_inv, r * jnp.exp(cum)
    ar = jnp.concatenate([al, rb], axis=0)
    sv = stack(v)
    l_all = dot3(ar, jnp.concatenate([stack(be), stack(kb)], axis=0), NT)
    l_ab = jnp.where(strict, l_all[:c, :hc], 0.0)
    l_ak = jnp.where(strict, l_all[:c, hc:], 0.0)
    l_rb = jnp.where(incl, l_all[c:, :hc], 0.0)
    l_rk = jnp.where(incl, l_all[c:, hc:], 0.0)
    inv = neumann_inv(l_ab)
    from_s0 = dot3(ar, s0, NT)
    from_v = dot3(jnp.concatenate([l_ak, l_rk], axis=0), sv, NN)
    u = dot3(inv, stack(from_s0[:c] + from_v[:c]), NN)
    o = from_s0[c:] + from_v[c:] + dot3(l_rb, stack(u), NN)
    to_end = jnp.exp(cum_end - cum)
    s1 = s0 * jnp.exp(cum_end) + blocks * dot3(
        jnp.concatenate([u, v], axis=0), jnp.concatenate([beta * to_end, k * to_end], axis=0), TN)
    return o, s1


def call_with_comm(plan, middle_step, body, grid, name, in_specs, out_specs, out_shape, scratch_shapes, operands):
    n_in, n_out, n_scr = len(in_specs), len(out_specs), len(scratch_shapes)
    p_in, p_out = len(plan.ins), len(plan.out_shape)

    def fused(*refs):
        refs = list(refs)
        ins, refs = refs[:n_in], refs[n_in:]
        p_ins, refs = refs[:p_in], refs[p_in:]
        outs, refs = refs[:n_out], refs[n_out:]
        p_outs, refs = refs[:p_out], refs[p_out:]
        scr, p_sems = refs[:n_scr], refs[n_scr:]
        start, middle, finish = plan.stages(p_ins, p_outs, p_sems)
        step = pl.program_id(0)
        pl.when(step == 0)(start)
        body(*ins, *outs, *scr)
        pl.when(step == middle_step)(middle)
        pl.when(step == grid[0] - 1)(finish)

    any_spec = pl.BlockSpec(memory_space=pl.ANY)
    res = pl.pallas_call(
        fused, grid=grid, name=name,
        in_specs=list(in_specs) + [any_spec] * p_in, out_specs=list(out_specs) + [any_spec] * p_out,
        out_shape=list(out_shape) + list(plan.out_shape), scratch_shapes=list(scratch_shapes) + list(plan.scratch),
        compiler_params=_params(("arbitrary",)),
    )(*operands, *plan.ins)
    return res[:n_out], res[n_out:]


def rwkv_scan_fwd(r, lw, k, v, kk, a, plan):
    def body(r_ref, lw_ref, k_ref, v_ref, kk_ref, a_ref, o_ref, st_ref, s_ref):
        @pl.when(pl.program_id(0) == 0)
        def _():
            s_ref[...] = jnp.zeros_like(s_ref)

        for g in range(SCAN_GROUPS):
            lanes = slice(g * SCAN_WIDTH, (g + 1) * SCAN_WIDTH)
            s0 = s_ref[g]
            st_ref[0, g] = s0
            o, s1 = _rwkv_chunk(s0, *(ref[:, lanes] for ref in (r_ref, lw_ref, k_ref, v_ref, kk_ref, a_ref)))
            o_ref[:, lanes] = o
            s_ref[g] = s1

    tb = pl.BlockSpec((CHUNK, D_RWKV), lambda c: (c, 0))
    state = (SCAN_GROUPS, SCAN_WIDTH, SCAN_WIDTH)
    return call_with_comm(
        plan, 3 * N_CHUNKS // 4, body, (N_CHUNKS,), "rwkv_scan_fwd",
        [tb] * 6, [tb, pl.BlockSpec((1,) + state, lambda c: (c, 0, 0, 0))],
        [_sds((SEQ, D_RWKV)), _sds((N_CHUNKS,) + state)], [pltpu.VMEM(state, F32)], (r, lw, k, v, kk, a))


def rwkv_scan_bwd(r, lw, k, v, kk, a, states, d_o, plan):
    def body(r_ref, lw_ref, k_ref, v_ref, kk_ref, a_ref, st_ref, do_ref,
             dr_ref, dlw_ref, dk_ref, dv_ref, dkk_ref, da_ref, ds_ref):
        @pl.when(pl.program_id(0) == 0)
        def _():
            ds_ref[...] = jnp.zeros_like(ds_ref)

        for g in range(SCAN_GROUPS):
            lanes = slice(g * SCAN_WIDTH, (g + 1) * SCAN_WIDTH)
            _, vjp = jax.vjp(_rwkv_chunk, st_ref[0, g],
                             *(ref[:, lanes] for ref in (r_ref, lw_ref, k_ref, v_ref, kk_ref, a_ref)))
            grads = vjp((do_ref[:, lanes], ds_ref[g]))
            ds_ref[g] = grads[0]
            for ref, val in zip((dr_ref, dlw_ref, dk_ref, dv_ref, dkk_ref, da_ref), grads[1:]):
                ref[:, lanes] = val

    last = N_CHUNKS - 1
    tb = pl.BlockSpec((CHUNK, D_RWKV), lambda c: (last - c, 0))
    state = (SCAN_GROUPS, SCAN_WIDTH, SCAN_WIDTH)
    return call_with_comm(
        plan, N_CHUNKS // 4, body, (N_CHUNKS,), "rwkv_scan_bwd",
        [tb] * 6 + [pl.BlockSpec((1,) + state, lambda c: (last - c, 0, 0, 0)), tb], [tb] * 6,
        [_sds((SEQ, D_RWKV))] * 6, [pltpu.VMEM(state, F32)], (r, lw, k, v, kk, a, states, d_o))


def _ffn_mid(ug, uv, cg, cv, bg, bv):
    conv_g = bg + cg[0] * shift_rows(ug, 2) + cg[1] * shift_rows(ug, 1) + cg[2] * ug
    conv_v = bv + cv[0] * shift_rows(uv, 2) + cv[1] * shift_rows(uv, 1) + cv[2] * uv
    return jax.nn.gelu(conv_g, approximate=True) * conv_v


def _conv_rows(ref):
    return tuple(ref[0, j:j + 1, :] for j in range(3))


def _ffn_specs(tile):
    per = D_MODEL // tile
    half = N_DEV // 2
    w_g = pl.BlockSpec((1, D_MODEL, tile), lambda t: (t // per, 0, t % per))
    w_v = pl.BlockSpec((1, D_MODEL, tile), lambda t: (half + t // per, 0, t % per))
    c_g = pl.BlockSpec((1, 3, tile), lambda t: (t // per, 0, t % per))
    c_v = pl.BlockSpec((1, 3, tile), lambda t: (half + t // per, 0, t % per))
    b_g = pl.BlockSpec((1, tile), lambda t: (0, t))
    b_v = pl.BlockSpec((1, tile), lambda t: (0, D_FF // tile + t))
    w_d = pl.BlockSpec((tile, D_MODEL), lambda t: (t, 0))
    return w_g, w_v, c_g, c_v, b_g, b_v, w_d


def ffn_fwd(h2, w_up, conv_w, conv_b, w_down):
    def body(h_ref, wg_ref, wv_ref, cg_ref, cv_ref, bg_ref, bv_ref, wd_ref, f_ref):
        @pl.when(pl.program_id(0) == 0)
        def _():
            f_ref[...] = jnp.zeros_like(f_ref)

        h = h_ref[...]
        act = _ffn_mid(_dot(h, wg_ref[0], NN), _dot(h, wv_ref[0], NN), _conv_rows(cg_ref), _conv_rows(cv_ref),
                       bg_ref[...], bv_ref[...])
        f_ref[...] += _dot(act.astype(BF16), wd_ref[...], NN)

    full = pl.BlockSpec((SEQ, D_MODEL), lambda t: (0, 0))
    return pl.pallas_call(
        body, grid=(D_FF // FF_TILE,), name="ffn_fwd",
        in_specs=[full, *_ffn_specs(FF_TILE)],
        out_specs=full, out_shape=_sds((SEQ, D_MODEL)),
        compiler_params=_params(("arbitrary",)),
    )(h2, w_up, w_up, conv_w, conv_w, conv_b, conv_b, w_down)


def ffn_bwd(h2, w_up, conv_w, conv_b, w_down, df):
    tile = FF_TILE_BWD
    per = D_MODEL // tile

    def body(h_hbm, wg_ref, wv_ref, cg_ref, cv_ref, bg_ref, bv_ref, wd_ref, df_hbm,
             dh_hbm, dup_hbm, dcg_ref, dcv_ref, dbg_ref, dbv_ref, dwd_ref,
             h_ref, df_ref, dh_ref, dwg_ref, dwv_ref, sem, up_sems):
        t = pl.program_id(0)

        @pl.when(t == 0)
        def _():
            pltpu.sync_copy(h_hbm, h_ref)
            pltpu.sync_copy(df_hbm, df_ref)

        h, df_b, wg, wv = h_ref[...], df_ref[...], wg_ref[0], wv_ref[0]
        act, vjp = jax.vjp(_ffn_mid, _dot(h, wg, NN), _dot(h, wv, NN), _conv_rows(cg_ref), _conv_rows(cv_ref),
                           bg_ref[...], bv_ref[...])
        dwd_ref[...] = _dot(act.astype(BF16), df_b, TN)
        dug, duv, dcg, dcv, dbg, dbv = vjp(_dot(df_b, wd_ref[...], NT))
        dug, duv = dug.astype(BF16), duv.astype(BF16)
        cols = pl.ds(pl.multiple_of((t % per) * tile, tile), tile)
        to_gate = pltpu.make_async_copy(dwg_ref, dup_hbm.at[t // per, :, cols], up_sems.at[0])
        to_value = pltpu.make_async_copy(dwv_ref, dup_hbm.at[N_DEV // 2 + t // per, :, cols], up_sems.at[1])
        dwg_ref[...] = _dot(h, dug, TN)
        to_gate.start()
        dwv_ref[...] = _dot(h, duv, TN)
        to_value.start()

        @pl.when(pl.program_id(0) == 0)
        def _():
            dh_ref[...] = jnp.zeros_like(dh_ref)

        dh_ref[...] += _dot(dug, wg, NT) + _dot(duv, wv, NT)
        for j in range(3):
            dcg_ref[0, j:j + 1, :] = dcg[j]
            dcv_ref[0, j:j + 1, :] = dcv[j]
        dbg_ref[...] = dbg
        dbv_ref[...] = dbv
        to_gate.wait()
        to_value.wait()

        @pl.when(t == D_FF // tile - 1)
        def _():
            cp = pltpu.make_async_copy(dh_ref, dh_hbm, sem)
            cp.start()
            cp.wait()

    hbm = pl.BlockSpec(memory_space=pl.ANY)
    w_g, w_v, c_g, c_v, b_g, b_v, w_d = _ffn_specs(tile)
    return pl.pallas_call(
        body, grid=(D_FF // tile,), name="ffn_bwd",
        in_specs=[hbm, w_g, w_v, c_g, c_v, b_g, b_v, w_d, hbm],
        out_specs=[hbm, hbm, c_g, c_v, b_g, b_v, w_d],
        out_shape=[_sds((SEQ, D_MODEL)), _sds((N_DEV, D_MODEL, D_MODEL)),
                   _sds((N_DEV, 3, D_MODEL)), _sds((N_DEV, 3, D_MODEL)), _sds((1, 2 * D_FF)), _sds((1, 2 * D_FF)),
                   _sds((D_FF, D_MODEL))],
        scratch_shapes=[pltpu.VMEM((SEQ, D_MODEL), BF16), pltpu.VMEM((SEQ, D_MODEL), BF16),
                        pltpu.VMEM((SEQ, D_MODEL), F32), pltpu.VMEM((D_MODEL, tile), F32),
                        pltpu.VMEM((D_MODEL, tile), F32), pltpu.SemaphoreType.DMA, pltpu.SemaphoreType.DMA((2,))],
        compiler_params=_params(("arbitrary",)),
    )(h2, w_up, w_up, conv_w, conv_w, conv_b, conv_b, w_down, df)


def ffn_bwd_mid(h2, w_up, conv_w, conv_b, w_down, df):
    tile, rows, halo = FF_TILE, FF_ROW_CHUNK, FF_HALO
    ext = rows + 2 * halo

    def body(h_hbm, wg_ref, wv_ref, cg_ref, cv_ref, bg_ref, bv_ref, wd_ref, df_hbm,
             dug_ref, duv_ref, dcg_ref, dcv_ref, dbg_ref, dbv_ref, dwd_ref,
             h_ref, df_ref, ug_ref, uv_ref, da_ref, act_ref):
        @pl.when(pl.program_id(0) == 0)
        def _():
            pltpu.sync_copy(h_hbm, h_ref)
            pltpu.sync_copy(df_hbm, df_ref)
            for ref in (ug_ref, uv_ref, da_ref):
                ref[0:halo, :] = jnp.zeros((halo, tile), F32)
                ref[halo + SEQ:, :] = jnp.zeros((halo, tile), F32)

        h, df_b = h_ref[...], df_ref[...]
        ug_ref[halo:halo + SEQ, :] = _dot(h, wg_ref[0], NN)
        uv_ref[halo:halo + SEQ, :] = _dot(h, wv_ref[0], NN)
        da_ref[halo:halo + SEQ, :] = _dot(df_b, wd_ref[...], NT)
        cg, cv, bg, bv = _conv_rows(cg_ref), _conv_rows(cv_ref), bg_ref[...], bv_ref[...]
        down = lambda x, n: pltpu.roll(x, n, 0)
        up = lambda x, n: pltpu.roll(x, ext - n, 0)
        mid = slice(halo, halo + rows)

        def chunk(i, sums):
            r0 = pl.multiple_of(i * rows, rows)
            window = pl.ds(r0, ext)
            ug, uv, da = ug_ref[window, :], uv_ref[window, :], da_ref[window, :]
            ug1, ug2, uv1, uv2 = down(ug, 1), down(ug, 2), down(uv, 1), down(uv, 2)
            conv_g = bg + cg[0] * ug2 + cg[1] * ug1 + cg[2] * ug
            conv_v = bv + cv[0] * uv2 + cv[1] * uv1 + cv[2] * uv
            act, vjp = jax.vjp(lambda a, b: jax.nn.gelu(a, approximate=True) * b, conv_g, conv_v)
            dcg, dcv = vjp(da)
            dug = cg[2] * dcg + cg[1] * up(dcg, 1) + cg[0] * up(dcg, 2)
            duv = cv[2] * dcv + cv[1] * up(dcv, 1) + cv[0] * up(dcv, 2)
            out = pl.ds(r0, rows)
            act_ref[out, :] = act[mid].astype(BF16)
            dug_ref[out, :] = dug[mid].astype(BF16)
            duv_ref[out, :] = duv[mid].astype(BF16)
            col = lambda x: jnp.sum(x[mid], axis=0, keepdims=True)
            new = (col(dcg * ug2), col(dcg * ug1), col(dcg * ug), col(dcv * uv2), col(dcv * uv1), col(dcv * uv),
                   col(dcg), col(dcv))
            return tuple(s + n for s, n in zip(sums, new))

        zero = jnp.zeros((1, tile), F32)
        sums = lax.fori_loop(0, SEQ // rows, chunk, (zero,) * 8)
        for j in range(3):
            dcg_ref[0, j:j + 1, :] = sums[j]
            dcv_ref[0, j:j + 1, :] = sums[3 + j]
        dbg_ref[...] = sums[6]
        dbv_ref[...] = sums[7]
        dwd_ref[...] = _dot(act_ref[...], df_b, TN)

    hbm = pl.BlockSpec(memory_space=pl.ANY)
    w_g, w_v, c_g, c_v, b_g, b_v, w_d = _ffn_specs(tile)
    col = pl.BlockSpec((SEQ, tile), lambda t: (0, t))
    padded = pltpu.VMEM((SEQ + 2 * halo, tile), F32)
    return pl.pallas_call(
        body, grid=(D_FF // tile,), name="ffn_bwd_mid",
        in_specs=[hbm, w_g, w_v, c_g, c_v, b_g, b_v, w_d, hbm],
        out_specs=[col, col, c_g, c_v, b_g, b_v, w_d],
        out_shape=[_sds((SEQ, D_FF), BF16), _sds((SEQ, D_FF), BF16), _sds((N_DEV, 3, D_MODEL)),
                   _sds((N_DEV, 3, D_MODEL)), _sds((1, 2 * D_FF)), _sds((1, 2 * D_FF)), _sds((D_FF, D_MODEL))],
        scratch_shapes=[pltpu.VMEM((SEQ, D_MODEL), BF16), pltpu.VMEM((SEQ, D_MODEL), BF16), padded, padded, padded,
                        pltpu.VMEM((SEQ, tile), BF16)],
        compiler_params=_params(("arbitrary",)),
    )(h2, w_up, w_up, conv_w, conv_w, conv_b, conv_b, w_down, df)


def ffn_bwd_up(h2, w_up, dug, duv):
    tile = FF_TILE
    per = D_MODEL // tile

    def body(h_hbm, wg_ref, wv_ref, dug_ref, duv_ref, dh_hbm, dup_hbm, h_ref, dh_ref, dwg_ref, dwv_ref, sem, up_sems):
        t = pl.program_id(0)

        @pl.when(t == 0)
        def _():
            pltpu.sync_copy(h_hbm, h_ref)
            dh_ref[...] = jnp.zeros_like(dh_ref)

        h, dug_b, duv_b = h_ref[...], dug_ref[...], duv_ref[...]
        cols = pl.ds(pl.multiple_of((t % per) * tile, tile), tile)
        to_gate = pltpu.make_async_copy(dwg_ref, dup_hbm.at[t // per, :, cols], up_sems.at[0])
        to_value = pltpu.make_async_copy(dwv_ref, dup_hbm.at[N_DEV // 2 + t // per, :, cols], up_sems.at[1])
        dwg_ref[...] = _dot(h, dug_b, TN)
        to_gate.start()
        dwv_ref[...] = _dot(h, duv_b, TN)
        to_value.start()
        dh_ref[...] += _dot(jnp.concatenate([dug_b, duv_b], axis=1),
                            jnp.concatenate([wg_ref[0], wv_ref[0]], axis=1), NT)
        to_gate.wait()
        to_value.wait()

        @pl.when(t == D_FF // tile - 1)
        def _():
            cp = pltpu.make_async_copy(dh_ref, dh_hbm, sem)
            cp.start()
            cp.wait()

    hbm = pl.BlockSpec(memory_space=pl.ANY)
    w_g, w_v = _ffn_specs(tile)[:2]
    col = pl.BlockSpec((SEQ, tile), lambda t: (0, t))
    return pl.pallas_call(
        body, grid=(D_FF // tile,), name="ffn_bwd_up",
        in_specs=[hbm, w_g, w_v, col, col], out_specs=[hbm, hbm],
        out_shape=[_sds((SEQ, D_MODEL)), _sds((N_DEV, D_MODEL, D_MODEL))],
        scratch_shapes=[pltpu.VMEM((SEQ, D_MODEL), BF16), pltpu.VMEM((SEQ, D_MODEL), F32),
                        pltpu.VMEM((D_MODEL, tile), F32), pltpu.VMEM((D_MODEL, tile), F32),
                        pltpu.SemaphoreType.DMA, pltpu.SemaphoreType.DMA((2,))],
        compiler_params=_params(("arbitrary",)),
    )(h2, w_up, w_up, dug, duv)


def loss_head(x1, f, target, n_post):
    def tile_loss(x1_t, f_t, g, tgt):
        err = x1_t + _rms(f_t, g) - tgt
        return 0.5 * jnp.sum(jnp.mean(err * err, axis=-1))

    def body(x_ref, f_ref, t_ref, g_ref, dx_ref, df_ref, dg_ref, loss_ref):
        val, (dx, df, dg) = jax.value_and_grad(tile_loss, argnums=(0, 1, 2))(
            x_ref[...], f_ref[...], g_ref[...], t_ref[...])
        dx_ref[...] = dx
        df_ref[...] = df.astype(BF16)

        @pl.when(pl.program_id(0) == 0)
        def _():
            dg_ref[...] = jnp.zeros_like(dg_ref)
            loss_ref[...] = jnp.zeros_like(loss_ref)

        dg_ref[...] += dg
        loss_ref[...] += jnp.full((1, LANES), val, F32)

    tile = pl.BlockSpec((TOK_TILE, D_MODEL), lambda i: (i, 0))
    vec = pl.BlockSpec((1, D_MODEL), lambda i: (0, 0))
    return pl.pallas_call(
        body, grid=(SEQ // TOK_TILE,), name="loss_head",
        in_specs=[tile, tile, tile, vec],
        out_specs=[tile, tile, vec, pl.BlockSpec((1, LANES), lambda i: (0, 0))],
        out_shape=[_sds((SEQ, D_MODEL)), _sds((SEQ, D_MODEL), BF16), _sds((1, D_MODEL)), _sds((1, LANES))],
        compiler_params=_params(("arbitrary",)),
    )(x1, f, target, n_post)


def _mesh_pos():
    return lax.axis_index("x"), lax.axis_index("y"), lax.axis_index("c")


def _flip(pos, rel):
    x, y, c = pos
    return (1 - x if rel & 4 else x, 1 - y if rel & 2 else y, 1 - c if rel & 1 else c)


def _slot(pos):
    x, y, c = pos
    return 4 * x + 2 * y + c


def cast_bf16(w, rows):
    def body(w_ref, o_ref):
        o_ref[...] = w_ref[...].astype(BF16)

    spec = pl.BlockSpec((rows, w.shape[1]), lambda i: (i, 0))
    return pl.pallas_call(body, grid=(w.shape[0] // rows,), name="cast_bf16_%dx%d" % w.shape,
                          in_specs=[spec], out_specs=spec, out_shape=_sds(w.shape, BF16),
                          compiler_params=_params(("arbitrary",)))(w)


class CommPlan:
    def __init__(self, ins, out_shape, scratch, stages):
        self.ins, self.out_shape, self.scratch, self.stages = ins, out_shape, scratch, stages


def run_comm(name, plan):
    n_in, n_out = len(plan.ins), len(plan.out_shape)

    def body(*refs):
        for stage in plan.stages(refs[:n_in], refs[n_in:n_in + n_out], refs[n_in + n_out:]):
            stage()

    any_spec = pl.BlockSpec(memory_space=pl.ANY)
    return pl.pallas_call(
        body, name=name, in_specs=[any_spec] * len(plan.ins), out_specs=[any_spec] * len(plan.out_shape),
        out_shape=plan.out_shape, scratch_shapes=plan.scratch)(*plan.ins)


def gather_plan(shards):
    n = len(shards)

    def stages(srcs, outs, sems):
        send_sems, recv_sems, local_sems = sems

        def places():
            me = _mesh_pos()
            return me, _flip(me, 1), [_flip(me, 2), _flip(me, 4), _flip(me, 6)]

        def copy(a, k, block, to, src=None):
            dst = outs[a].at[_slot(block)]
            return pltpu.make_async_remote_copy(
                src_ref=dst if src is None else src, dst_ref=dst,
                send_sem=send_sems.at[7 * a + k], recv_sem=recv_sems.at[7 * a + k],
                device_id=to, device_id_type=pl.DeviceIdType.MESH)

        def local(a, me):
            return pltpu.make_async_copy(srcs[a], outs[a].at[_slot(me)], local_sems.at[a])

        def own(a, me, sibling, chips):
            return [copy(a, 0, me, sibling, src=srcs[a])] + [
                copy(a, 1 + j, me, chip, src=srcs[a]) for j, chip in enumerate(chips)]

        def start():
            me, sibling, chips = places()
            for a in range(n):
                local(a, me).start()
                for cp in own(a, me, sibling, chips):
                    cp.start()

        def forward():
            me, sibling, chips = places()
            for j, chip in enumerate(chips):
                for a in range(n):
                    copy(a, 1 + j, chip, me).wait_recv()
                    copy(a, 4 + j, chip, sibling).start()

        def finish():
            me, sibling, chips = places()
            for a in range(n):
                copy(a, 0, sibling, me).wait_recv()
                for j, chip in enumerate(chips):
                    copy(a, 4 + j, _flip(chip, 1), me).wait_recv()
            for a in range(n):
                for cp in own(a, me, sibling, chips):
                    cp.wait_send()
                for j, chip in enumerate(chips):
                    copy(a, 4 + j, chip, sibling).wait_send()
                local(a, me).wait()

        return start, forward, finish

    return CommPlan(list(shards), [_sds((N_DEV,) + s.shape, s.dtype) for s in shards],
                    [pltpu.SemaphoreType.DMA((7 * n,)), pltpu.SemaphoreType.DMA((7 * n,)),
                     pltpu.SemaphoreType.DMA((n,))], stages)


def exchange_plan(parts, replicated, rels, members, index, member_axis, own_copy):
    n, nr = len(parts), len(rels)
    pick_index = (slice(None),) * member_axis + (0,)
    subs = [1 if (r or member_axis == 0) else p.shape[0] for p, r in zip(parts, replicated)]
    first = [sum(subs[:a]) for a in range(n)]
    total = sum(subs)

    def stages(srcs, outs, sems):
        send_sems, recv_sems, local_sems = sems

        def src(a, s, pos):
            if replicated[a]:
                return srcs[a]
            return srcs[a].at[index(pos)] if member_axis == 0 else srcs[a].at[s, index(pos)]

        def dst(a, s, pos):
            block = outs[a].at[index(pos)]
            return block if (replicated[a] or member_axis == 0) else block.at[s]

        def copy(a, s, j, me, src_pos, dst_pos):
            sem = nr * (first[a] + s) + j
            return pltpu.make_async_remote_copy(
                src_ref=src(a, s, src_pos), dst_ref=dst(a, s, dst_pos),
                send_sem=send_sems.at[sem], recv_sem=recv_sems.at[sem],
                device_id=_flip(me, rels[j]), device_id_type=pl.DeviceIdType.MESH)

        pieces = [(a, s) for a in range(n) for s in range(subs[a])]

        def local(a, s, me):
            return pltpu.make_async_copy(src(a, s, me), dst(a, s, me), local_sems.at[first[a] + s])

        def sends(me):
            return [copy(a, s, j, me, _flip(me, rels[j]), me) for j in range(nr) for a, s in pieces]

        own = pieces if own_copy else []

        def start():
            me = _mesh_pos()
            for cp in sends(me) + [local(a, s, me) for a, s in own]:
                cp.start()

        def middle():
            pass

        def finish():
            me = _mesh_pos()
            for j in range(nr):
                for a, s in pieces:
                    copy(a, s, j, me, me, _flip(me, rels[j])).wait_recv()
            for cp in sends(me):
                cp.wait_send()
            for a, s in own:
                local(a, s, me).wait()

        return start, middle, finish

    shapes = [p.shape if r else jax.eval_shape(lambda t: t[pick_index], p).shape for p, r in zip(parts, replicated)]
    return CommPlan(list(parts), [_sds((members,) + s, p.dtype) for s, p in zip(shapes, parts)],
                    [pltpu.SemaphoreType.DMA((nr * total,)), pltpu.SemaphoreType.DMA((nr * total,)),
                     pltpu.SemaphoreType.DMA((total,))], stages)


def pair_plan(parts, replicated):
    return exchange_plan(parts, replicated, [1], 2, lambda pos: pos[2], 1, False)


def chip_plan(parts, replicated):
    return exchange_plan(parts, replicated, [2, 4, 6], 4, lambda pos: 2 * pos[0] + pos[1], 0, True)


def add_pair(name, mine, swapped, out_dtype, rows):
    def body(m_ref, s_ref, o_ref):
        own = m_ref[0, 0] if mine.ndim == 4 else m_ref[0]
        o_ref[0] = (own + s_ref[0, 0]).astype(o_ref.dtype)

    _, n, r, c = swapped.shape
    core = lambda: lax.axis_index("c")
    if mine.ndim == 4:
        mine_spec = pl.BlockSpec((1, 1, rows, c), lambda i, j: (i, core(), j, 0))
    else:
        mine_spec = pl.BlockSpec((1, rows, c), lambda i, j: (i, j, 0))
    return pl.pallas_call(
        body, grid=(n, r // rows), name=name,
        in_specs=[mine_spec, pl.BlockSpec((1, 1, rows, c), lambda i, j: (1 - core(), i, j, 0))],
        out_specs=pl.BlockSpec((1, rows, c), lambda i, j: (i, j, 0)),
        out_shape=_sds((n, r, c), out_dtype),
        compiler_params=_params(("arbitrary", "arbitrary")),
    )(mine, swapped)


def adamw(name, w, parts, m, v, rows):
    c1 = 1.0 - ADAM_B1 ** ADAM_STEP
    c2 = 1.0 - ADAM_B2 ** ADAM_STEP
    n_parts = parts.shape[0]

    def body(w_ref, p_ref, m_ref, v_ref, g_ref, d_ref, nm_ref, nv_ref):
        g = p_ref[0].astype(F32)
        for j in range(1, n_parts):
            g = g + p_ref[j].astype(F32)
        nm = ADAM_B1 * m_ref[...] + (1.0 - ADAM_B1) * g
        nv = ADAM_B2 * v_ref[...] + (1.0 - ADAM_B2) * (g * g)
        g_ref[...] = g
        nm_ref[...] = nm
        nv_ref[...] = nv
        d_ref[...] = -ADAM_LR * ((nm / c1) / (jnp.sqrt(nv / c2) + ADAM_EPS) + ADAM_WD * w_ref[...])

    cols = w.shape[1]
    spec = pl.BlockSpec((rows, cols), lambda i: (i, 0))
    return pl.pallas_call(
        body, grid=(w.shape[0] // rows,), name=name,
        in_specs=[spec, pl.BlockSpec((n_parts, rows, cols), lambda i: (0, i, 0)), spec, spec],
        out_specs=[spec] * 4, out_shape=[_sds(w.shape)] * 4,
        compiler_params=_params(("arbitrary",)),
    )(w, parts, m, v)


def _rows128(a):
    flat = a.reshape(-1)
    pad = (-flat.shape[0]) % LANES
    if pad:
        flat = jnp.concatenate([flat, jnp.zeros((pad,), flat.dtype)])
    return flat.reshape(-1, LANES)


def _pack(arrays):
    rows = [_rows128(a) for a in arrays]
    pad = (-sum(r.shape[0] for r in rows)) % 8
    return jnp.concatenate(rows + [jnp.zeros((pad, LANES), rows[0].dtype)] * (pad > 0), axis=0)


def _unpack(packed, like):
    out, row = [], 0
    for a in like:
        n = math.prod(a.shape)
        rows = -(-n // LANES)
        out.append(packed[row:row + rows].reshape(-1)[:n].reshape(a.shape))
        row += rows
    return out


def _to_slots(full, per):
    return full.reshape(full.shape[0], N_DEV, per).transpose(1, 0, 2)


def _from_slots(slots):
    return slots.transpose(1, 0, 2).reshape(slots.shape[1], -1)


def kernel(x, norm_mix_pre, norm_mix_post, norm_ffn_pre, norm_ffn_post, w_in, rel_bias, sinks, rwkv_shift_mix, w0, w_decay_up, a0, w_iclr_up, w_gate_up, k_k, k_a, r_k, ln_x_g, ln_x_b, w_out, w_ffn_up, conv_w, conv_b, w_ffn_down, loss_target, m_norm_mix_pre, m_norm_mix_post, m_norm_ffn_pre, m_norm_ffn_post, m_w_in, m_rel_bias, m_sinks, m_rwkv_shift_mix, m_w0, m_w_decay_up, m_a0, m_w_iclr_up, m_w_gate_up, m_k_k, m_k_a, m_r_k, m_ln_x_g, m_ln_x_b, m_w_out, m_w_ffn_up, m_conv_w, m_conv_b, m_w_ffn_down, v_norm_mix_pre, v_norm_mix_post, v_norm_ffn_pre, v_norm_ffn_post, v_w_in, v_rel_bias, v_sinks, v_rwkv_shift_mix, v_w0, v_w_decay_up, v_a0, v_w_iclr_up, v_w_gate_up, v_k_k, v_k_a, v_r_k, v_ln_x_g, v_ln_x_b, v_w_out, v_w_ffn_up, v_conv_w, v_conv_b, v_w_ffn_down):
    x2 = x[0]
    target = loss_target[0]

    g_in, g_out, g_decay, g_iclr, g_gate, g_conv = run_comm("all_gather_mixer", gather_plan([
        cast_bf16(w_in[0], 256), cast_bf16(w_out[0], 128), w_decay_up[0], w_iclr_up[0], w_gate_up[0], conv_w[0]]))
    ffn_gather = gather_plan([cast_bf16(w_ffn_up[0], 256), cast_bf16(w_ffn_down[0], 256)])
    w_in_b = _from_slots(g_in)
    w_out_b = g_out.reshape(D_MODEL, D_MODEL)
    lora = jnp.zeros((HEAD_DIM, D_RWKV), F32)
    wd_pad = jnp.concatenate([_from_slots(g_decay), lora], axis=0)
    wi_pad = jnp.concatenate([lora, _from_slots(g_iclr)], axis=0)
    wg_full = _from_slots(g_gate)
    mix_ext = jnp.concatenate([jnp.zeros((1, D_QKV), F32), rwkv_shift_mix], axis=1)
    r_k_row = r_k.reshape(1, D_RWKV)
    bucket = _bucket_table()

    (h1,) = tok_fwd("rms_mix_pre", rms_tile, [x2], [norm_mix_pre], [], [D_MODEL], [BF16])
    proj, ps = in_proj_fwd(h1, w_in_b, mix_ext)
    attn = attn_fwd(proj, rel_bias, bucket, sinks)
    pre_params = [w0, wd_pad, a0, wi_pad, wg_full, k_k, k_a]
    r_, lw_, k2_, v_, kk_, a_, gate_ = tok_fwd("rwkv_pre", rwkv_pre_tile, [ps], pre_params, [],
                                               [D_RWKV] * 7, [F32] * 7)
    (o_, states), (g_up, g_down) = rwkv_scan_fwd(r_, lw_, k2_, v_, kk_, a_, ffn_gather)
    w_down_b = g_down.reshape(D_FF, D_MODEL)
    mix_tiles = [o_, r_, k2_, v_, gate_, attn, x2]
    mix_params = [w_out_b, norm_mix_post, ln_x_g, ln_x_b, r_k_row]
    (x1,) = tok_fwd("mix_out", mix_out_tile, mix_tiles, mix_params, [(D_MODEL, D_MODEL)], [D_MODEL], [F32])
    (h2,) = tok_fwd("rms_ffn_pre", rms_tile, [x1], [norm_ffn_pre], [], [D_MODEL], [BF16])
    f = ffn_fwd(h2, g_up, g_conv, conv_b, w_down_b)
    dy, df, d_n_ffn_post, loss_row = loss_head(x1, f, target, norm_ffn_post)
    loss = lax.psum(loss_row[0, 0], ("x", "y", "c"))

    d_ug, d_uv, d_cw_g, d_cw_v, d_cb_g, d_cb_v, d_down = ffn_bwd_mid(h2, g_up, g_conv, conv_b, w_down_b, df)
    dh2, d_up = ffn_bwd_up(h2, g_up, d_ug, d_uv)
    half = N_DEV // 2
    d_cw = jnp.concatenate([d_cw_g[:half], d_cw_v[half:]], axis=0)
    by_pair = lambda slots: slots.reshape((N_DEV // 2, 2) + slots.shape[1:])
    ffn_mine = [by_pair(d_up), by_pair(d_down.reshape(N_DEV, D_FF // N_DEV, D_MODEL))]
    ffn_swapped = run_comm("pair_exchange_ffn", pair_plan(ffn_mine, [False, False]))
    ffn_exchange = chip_plan([add_pair("pair_add_w_ffn_up", ffn_mine[0], ffn_swapped[0], BF16, 256),
                              add_pair("pair_add_w_ffn_down", ffn_mine[1], ffn_swapped[1], BF16, 256)],
                             [False, False])
    d_cb = jnp.concatenate([d_cb_g[:, :D_FF], d_cb_v[:, D_FF:]], axis=1)
    dx1_ffn, d_n_ffn_pre = tok_bwd("rms_ffn_pre_bwd", rms_tile, [x1], [norm_ffn_pre], [], [dh2], [0])
    dx1 = dy + dx1_ffn
    (d_o, d_r1, d_k1, d_v1, d_gate, d_attn, dx_res, d_n_mix_post, d_ln_g, d_ln_b, d_r_k,
     d_w_out) = tok_bwd("mix_out_bwd", mix_out_tile, mix_tiles, mix_params, [(D_MODEL, D_MODEL)], [dx1],
                        [1, 2, 3, 4])
    (d_r2, d_lw, d_k2, d_v2, d_kk, d_a), (got_up, got_down) = rwkv_scan_bwd(
        r_, lw_, k2_, v_, kk_, a_, states, d_o, ffn_exchange)
    pre_cots = [d_r1 + d_r2, d_lw, d_k1 + d_k2, d_v1 + d_v2, d_kk, d_a, d_gate]
    (d_ps, d_w0, d_wd_pad, d_a0, d_wi_pad, d_wg, d_k_k, d_k_a) = tok_bwd(
        "rwkv_pre_bwd", rwkv_pre_tile, [ps], pre_params, [], pre_cots, [0, 1, 2, 3, 4, 5, 6])
    dq, dkc, dkp, dvc, dvp, d_rel_bias, d_sinks = attn_bwd(proj, rel_bias, bucket, sinks, d_attn)
    zero_blk = jnp.zeros((BLOCK, D_KV), F32)
    dk = dkc + jnp.concatenate([dkp[BLOCK:], zero_blk], axis=0)
    dv = dvc + jnp.concatenate([dvp[BLOCK:], zero_blk], axis=0)
    dpa = jnp.concatenate([dq, dk, dv, d_ps], axis=1)
    dh1, d_w_in, d_mix_ext = in_proj_bwd(h1, w_in_b, mix_ext, proj, dpa)
    dx_pre, d_n_mix_pre = tok_bwd("rms_mix_pre_bwd", rms_tile, [x2], [norm_mix_pre], [], [dh1], [0])
    grad_x = (dx_res + dx_pre)[None]

    small_rep = [d_n_mix_pre, d_n_mix_post, d_n_ffn_pre, d_n_ffn_post, d_rel_bias, d_sinks,
                 d_mix_ext[:, D_QKV:], d_w0, d_a0, d_k_k, d_k_a, d_r_k.reshape(r_k.shape), d_ln_g, d_ln_b, d_cb]
    rep_w = [norm_mix_pre, norm_mix_post, norm_ffn_pre, norm_ffn_post, rel_bias, sinks, rwkv_shift_mix,
             w0, a0, k_k, k_a, r_k, ln_x_g, ln_x_b, conv_b]
    rep_m = [m_norm_mix_pre, m_norm_mix_post, m_norm_ffn_pre, m_norm_ffn_post, m_rel_bias, m_sinks,
             m_rwkv_shift_mix, m_w0, m_a0, m_k_k, m_k_a, m_r_k, m_ln_x_g, m_ln_x_b, m_conv_b]
    rep_v = [v_norm_mix_pre, v_norm_mix_post, v_norm_ffn_pre, v_norm_ffn_post, v_rel_bias, v_sinks,
             v_rwkv_shift_mix, v_w0, v_a0, v_k_k, v_k_a, v_r_k, v_ln_x_g, v_ln_x_b, v_conv_b]
    sh_w = [w_decay_up, w_iclr_up, w_gate_up, conv_w]
    sh_m = [m_w_decay_up, m_w_iclr_up, m_w_gate_up, m_conv_w]
    sh_v = [v_w_decay_up, v_w_iclr_up, v_w_gate_up, v_conv_w]
    sh_parts = [_to_slots(d_wd_pad[:HEAD_DIM], HEAD_DIM), _to_slots(d_wi_pad[HEAD_DIM:], HEAD_DIM),
                _to_slots(d_wg, HEAD_DIM), d_cw]
    small_sh = jnp.stack([_pack([p[j] for p in sh_parts]) for j in range(N_DEV)])
    mine = [by_pair(_to_slots(d_w_in, D_IN // N_DEV)), by_pair(d_w_out.reshape(N_DEV, D_MODEL // N_DEV, D_MODEL)),
            by_pair(small_sh), _pack(small_rep)[None]]
    swapped = run_comm("pair_exchange_mixer", pair_plan(mine, [False, False, False, True]))
    chip_sums = [add_pair("pair_add_w_in", mine[0], swapped[0], BF16, 512),
                 add_pair("pair_add_w_out", mine[1], swapped[1], BF16, 128),
                 add_pair("pair_add_small", mine[2], swapped[2], F32, small_sh.shape[1]),
                 add_pair("pair_add_replicated", mine[3], swapped[3], F32, mine[3].shape[1])[0]]
    got_in, got_out, got_sh, got_rep = run_comm("chip_exchange_mixer", chip_plan(
        chip_sums, [False, False, False, True]))

    big = [adamw("adamw_w_in", w_in[0], got_in, m_w_in[0], v_w_in[0], 256),
           adamw("adamw_w_out", w_out[0], got_out, m_w_out[0], v_w_out[0], 128),
           adamw("adamw_w_ffn_up", w_ffn_up[0], got_up, m_w_ffn_up[0], v_w_ffn_up[0], 128),
           adamw("adamw_w_ffn_down", w_ffn_down[0], got_down, m_w_ffn_down[0], v_w_ffn_down[0], 128)]
    pack2 = lambda rep, sh: jnp.concatenate([_pack(rep), _pack(sh)], axis=0)
    n_small = jnp.concatenate([got_rep, got_sh], axis=1)
    small = adamw("adamw_small", pack2(rep_w, sh_w), n_small, pack2(rep_m, sh_m), pack2(rep_v, sh_v),
                  n_small.shape[1])
    n_rep = got_rep.shape[1]
    small = [_unpack(p[:n_rep], rep_w) + _unpack(p[n_rep:], sh_w) for p in small]

    names = ["norm_mix_pre", "norm_mix_post", "norm_ffn_pre", "norm_ffn_post", "w_in", "rel_bias", "sinks",
             "rwkv_shift_mix", "w0", "w_decay_up", "a0", "w_iclr_up", "w_gate_up", "k_k", "k_a", "r_k",
             "ln_x_g", "ln_x_b", "w_out", "w_ffn_up", "conv_w", "conv_b", "w_ffn_down"]
    small_names = ["norm_mix_pre", "norm_mix_post", "norm_ffn_pre", "norm_ffn_post", "rel_bias", "sinks",
                   "rwkv_shift_mix", "w0", "a0", "k_k", "k_a", "r_k", "ln_x_g", "ln_x_b", "conv_b",
                   "w_decay_up", "w_iclr_up", "w_gate_up", "conv_w"]
    big_names = {"w_in": 0, "w_out": 1, "w_ffn_up": 2, "w_ffn_down": 3}
    outs = []
    for kind in range(4):
        for nm in names:
            if nm in big_names:
                outs.append(big[big_names[nm]][kind][None])
            else:
                outs.append(small[kind][small_names.index(nm)])
    return (loss, grad_x, *outs)
```

```python
import functools
import math

import jax
import jax.numpy as jnp
from jax import lax
from jax.experimental import pallas as pl
from jax.experimental.pallas import tpu as pltpu

F32 = jnp.float32
BF16 = jnp.bfloat16

N_DEV = 8
SEQ = 2048
D_MODEL = 1024
HEAD_DIM = 64
D_ATTN = 512
D_KV = 128
D_RWKV = 512
N_HEADS = 8
RWKV_COLS = 1792
D_QKV = D_ATTN + 2 * D_KV
D_IN = D_QKV + RWKV_COLS
D_FF = 4096
BLOCK = 128
N_BLOCKS = SEQ // BLOCK
N_BUCKETS = 32
MAX_DISTANCE = 128
NORM_EPS = 1e-6
GN_EPS = 64e-5
NEG_INF = -1e30
CHUNK = 64
N_CHUNKS = SEQ // CHUNK
SCAN_GROUPS = 2
SCAN_WIDTH = D_RWKV // SCAN_GROUPS
TOK_TILE = 256
FF_TILE = 256
FF_TILE_BWD = 128
FF_ROW_CHUNK = 256
FF_HALO = 8
COL_TILE = 256
LANES = 128
VMEM_LIMIT = 56 * 1024 * 1024

ADAM_LR = 0.001
ADAM_B1 = 0.9
ADAM_B2 = 0.999
ADAM_EPS = 1e-08
ADAM_WD = 0.01
ADAM_STEP = 10

NT = ((1,), (1,))
TN = ((0,), (0,))
NN = ((1,), (0,))


def _sds(shape, dtype=F32):
    return jax.ShapeDtypeStruct(shape, dtype)


def _params(sem=None):
    if sem is None:
        return pltpu.CompilerParams(vmem_limit_bytes=VMEM_LIMIT)
    return pltpu.CompilerParams(dimension_semantics=sem, vmem_limit_bytes=VMEM_LIMIT)


def _dot(a, b, dims):
    return lax.dot_general(a, b, (dims, ((), ())), preferred_element_type=F32)


def _split2(x):
    hi = x.astype(BF16)
    return hi, (x - hi.astype(F32)).astype(BF16)


def _dot3_raw(a, b, dims):
    ah, al = _split2(a)
    bh, bl = _split2(b)
    return _dot(ah, bh, dims) + (_dot(al, bh, dims) + _dot(ah, bl, dims))


@functools.partial(jax.custom_vjp, nondiff_argnums=(2,))
def dot3(a, b, dims):
    return _dot3_raw(a, b, dims)


def _dot3_fwd(a, b, dims):
    return _dot3_raw(a, b, dims), (a, b)


def _dot3_bwd(dims, res, g):
    a, b = res
    if dims == NN:
        return dot3(g, b, NT), dot3(a, g, TN)
    if dims == NT:
        return dot3(g, b, NN), dot3(g, a, TN)
    return dot3(b, g, NT), dot3(a, g, NN)


dot3.defvjp(_dot3_fwd, _dot3_bwd)


@jax.custom_vjp
def mm(a, b):
    return _dot(a.astype(BF16), b.astype(BF16), NN)


def _mm_fwd(a, b):
    return mm(a, b), (a, b)


def _mm_bwd(res, g):
    a, b = res
    gb = g.astype(BF16)
    return _dot(gb, b.astype(BF16), NT).astype(a.dtype), _dot(a.astype(BF16), gb, TN).astype(b.dtype)


mm.defvjp(_mm_fwd, _mm_bwd)


@jax.custom_vjp
def mm_nt(a, b):
    return _dot(a.astype(BF16), b.astype(BF16), NT)


def _mm_nt_fwd(a, b):
    return mm_nt(a, b), (a, b)


def _mm_nt_bwd(res, g):
    a, b = res
    gb = g.astype(BF16)
    return _dot(gb, b.astype(BF16), NN).astype(a.dtype), _dot(gb, a.astype(BF16), TN).astype(b.dtype)


mm_nt.defvjp(_mm_nt_fwd, _mm_nt_bwd)


@jax.custom_vjp
def mmw(a, w, wz):
    return _dot(a.astype(BF16), w, NN)


def _mmw_fwd(a, w, wz):
    return mmw(a, w, wz), (a, w)


def _mmw_bwd(res, g):
    a, w = res
    gb = g.astype(BF16)
    return _dot(gb, w, NT).astype(a.dtype), jnp.zeros_like(w), _dot(a.astype(BF16), gb, TN)


mmw.defvjp(_mmw_fwd, _mmw_bwd)


def _shift_raw(x, n):
    rows = x.shape[0]
    rolled = pltpu.roll(x, n % rows, 0)
    idx = lax.broadcasted_iota(jnp.int32, x.shape, 0)
    keep = idx >= n if n > 0 else idx < rows + n
    return jnp.where(keep, rolled, 0.0)


@functools.partial(jax.custom_vjp, nondiff_argnums=(1,))
def shift_rows(x, n):
    return _shift_raw(x, n)


def _shift_fwd(x, n):
    return _shift_raw(x, n), None


def _shift_bwd(n, _, g):
    return (_shift_raw(g, -n),)


shift_rows.defvjp(_shift_fwd, _shift_bwd)


def _head_matrix(scale):
    a = lax.broadcasted_iota(jnp.int32, (D_RWKV, D_RWKV), 0) // HEAD_DIM
    b = lax.broadcasted_iota(jnp.int32, (D_RWKV, D_RWKV), 1) // HEAD_DIM
    return jnp.where(a == b, scale, 0.0).astype(F32)


def _rms(x, g):
    return x * lax.rsqrt(jnp.mean(x * x, axis=-1, keepdims=True) + NORM_EPS) * g


def _softplus(x):
    return jnp.maximum(x, 0.0) + jnp.log(1.0 + jnp.exp(-jnp.abs(x)))


def _tile_spec(arr, tm):
    return pl.BlockSpec((tm, arr.shape[1]), lambda i: (i, 0))


def _full_spec(arr):
    nd = arr.ndim
    return pl.BlockSpec(arr.shape, lambda i: (0,) * nd)


def tok_fwd(name, fn, tiles, params, zero_shapes, out_widths, out_dtypes, tm=TOK_TILE):
    n_t, n_p = len(tiles), len(params)

    def body(*refs):
        t_vals = [r[...] for r in refs[:n_t]]
        p_vals = [r[...] for r in refs[n_t:n_t + n_p]]
        z_vals = [jnp.zeros(s, F32) for s in zero_shapes]
        outs = fn(*t_vals, *p_vals, *z_vals)
        for r, o in zip(refs[n_t + n_p:], outs):
            r[...] = o.astype(r.dtype)

    rows = tiles[0].shape[0]
    return pl.pallas_call(
        body, grid=(rows // tm,), name=name,
        in_specs=[_tile_spec(t, tm) for t in tiles] + [_full_spec(p) for p in params],
        out_specs=[pl.BlockSpec((tm, w), lambda i: (i, 0)) for w in out_widths],
        out_shape=[_sds((rows, w), dt) for w, dt in zip(out_widths, out_dtypes)],
        compiler_params=_params(("arbitrary",)),
    )(*tiles, *params)


def tok_bwd(name, fn, tiles, params, zero_shapes, cots, diff_params, residuals=(), tm=TOK_TILE):
    cot_parts = [c if isinstance(c, tuple) else (c,) for c in cots]
    flat_cots = [a for part in cot_parts for a in part]
    residuals = dict(residuals)
    extra = [residuals[i] for i in sorted(residuals)]
    n_t, n_p, n_c, n_r = len(tiles), len(params), len(flat_cots), len(extra)
    acc_shapes = [params[i].shape for i in diff_params] + list(zero_shapes)

    def body(*refs):
        t_vals = [r[...].astype(F32) for r in refs[:n_t]]
        p_vals = [r[...] for r in refs[n_t:n_t + n_p]]
        flat = iter(r[...] for r in refs[n_t + n_p:n_t + n_p + n_c])
        c_vals = [functools.reduce(jnp.add, [next(flat) for _ in part]) for part in cot_parts]
        r_vals = dict(zip(sorted(residuals), (r[...] for r in refs[n_t + n_p + n_c:n_t + n_p + n_c + n_r])))
        out_refs = refs[n_t + n_p + n_c + n_r:]
        z_vals = [jnp.zeros(s, F32) for s in zero_shapes]
        d_vals = [p_vals[i] for i in diff_params]

        def f(t_in, d_in, z_in):
            full = list(p_vals)
            for i, v in zip(diff_params, d_in):
                full[i] = v
            return tuple(fn(*t_in, *full, *z_in))

        _, vjp = jax.vjp(f, t_vals, d_vals, z_vals)
        g_t, g_d, g_z = vjp(tuple(c_vals))
        for i, (r, g) in enumerate(zip(out_refs[:n_t], g_t)):
            r[...] = (g + r_vals[i] if i in r_vals else g).astype(r.dtype)
        acc_refs = out_refs[n_t:]

        @pl.when(pl.program_id(0) == 0)
        def _():
            for r in acc_refs:
                r[...] = jnp.zeros_like(r)

        for r, g in zip(acc_refs, list(g_d) + list(g_z)):
            r[...] += g

    rows = tiles[0].shape[0]
    return pl.pallas_call(
        body, grid=(rows // tm,), name=name,
        in_specs=[_tile_spec(t, tm) for t in tiles] + [_full_spec(p) for p in params]
        + [_tile_spec(c, tm) for c in flat_cots + extra],
        out_specs=[_tile_spec(t, tm) for t in tiles]
        + [pl.BlockSpec(s, lambda i, nd=len(s): (0,) * nd) for s in acc_shapes],
        out_shape=[_sds(t.shape) for t in tiles] + [_sds(s) for s in acc_shapes],
        compiler_params=_params(("arbitrary",)),
    )(*tiles, *params, *flat_cots, *extra)


def rms_tile(x, g):
    return (_rms(x, g),)


def rwkv_pre_tile(ps, w0, wd_pad, a0, wi_pad, wg, k_k, k_a):
    r = ps[:, 0:D_RWKV]
    k = ps[:, D_RWKV:2 * D_RWKV]
    v = ps[:, 2 * D_RWKV:3 * D_RWKV]
    z2 = ps[:, 3 * D_RWKV:3 * D_RWKV + LANES]
    zg = ps[:, 3 * D_RWKV + LANES:RWKV_COLS]
    w_log = -_softplus(-(w0 + mm(jnp.tanh(z2), wd_pad))) - 0.5
    lw = -jnp.exp(w_log)
    a = jax.nn.sigmoid(a0 + mm(z2, wi_pad))
    g = mm(jax.nn.sigmoid(zg), wg)
    kk = k * k_k
    norm = jnp.sqrt(dot3(kk * kk, _head_matrix(1.0), NN))
    kk = kk / jnp.maximum(norm, 1e-12)
    k2 = k * (1.0 + (a - 1.0) * k_a)
    return r, lw, k2, v, kk, a, g


def mix_out_tile(o, r, k2, v, g, attn, x, w_out, n_post, ln_g, ln_b, r_k, wz):
    hmean = _head_matrix(1.0 / HEAD_DIM)
    d = o - dot3(o, hmean, NN)
    var = dot3(d * d, hmean, NN)
    on = d * lax.rsqrt(var + GN_EPS) * ln_g + ln_b
    bonus = dot3(r * k2 * r_k, _head_matrix(1.0), NN) * v
    rw = (on + bonus) * g
    mix = mmw(jnp.concatenate([attn, rw], axis=1), w_out, wz)
    return (x + _rms(mix, n_post),)


def in_proj_fwd(h, w_in, mix_ext):
    def body(h_ref, w_ref, m_ref, proj_ref, ps_ref):
        p = _dot(h_ref[...], w_ref[...], NN)
        proj_ref[...] = p
        ps_ref[...] = p + (_shift_raw(p, 1) - p) * m_ref[...]

    n = D_IN // COL_TILE
    first = D_QKV // COL_TILE
    return pl.pallas_call(
        body, grid=(n,), name="in_proj_fwd",
        in_specs=[pl.BlockSpec((SEQ, D_MODEL), lambda j: (0, 0)),
                  pl.BlockSpec((D_MODEL, COL_TILE), lambda j: (0, j)),
                  pl.BlockSpec((1, COL_TILE), lambda j: (0, j))],
        out_specs=[pl.BlockSpec((SEQ, COL_TILE), lambda j: (0, j)),
                   pl.BlockSpec((SEQ, COL_TILE), lambda j: (0, jnp.maximum(j - first, 0)))],
        out_shape=[_sds((SEQ, D_IN)), _sds((SEQ, RWKV_COLS))],
        compiler_params=_params(("arbitrary",)),
    )(h, w_in, mix_ext)


def in_proj_bwd(h, w_in, mix_ext, proj, dpa):
    def body(h_ref, w_ref, m_ref, p_ref, d_ref, dh_ref, dw_ref, dm_ref):
        d = d_ref[...]
        p = p_ref[...]
        dm_ref[...] = jnp.sum(d * (_shift_raw(p, 1) - p), axis=0, keepdims=True)
        dmix = d * m_ref[...]
        dp = (d - dmix + _shift_raw(dmix, -1)).astype(BF16)
        dw_ref[...] = _dot(h_ref[...], dp, TN)

        @pl.when(pl.program_id(0) == 0)
        def _():
            dh_ref[...] = jnp.zeros_like(dh_ref)

        dh_ref[...] += _dot(dp, w_ref[...], NT)

    n = D_IN // COL_TILE
    col = lambda rows: pl.BlockSpec((rows, COL_TILE), lambda j: (0, j))
    return pl.pallas_call(
        body, grid=(n,), name="in_proj_bwd",
        in_specs=[pl.BlockSpec((SEQ, D_MODEL), lambda j: (0, 0)), col(D_MODEL), col(1), col(SEQ), col(SEQ)],
        out_specs=[pl.BlockSpec((SEQ, D_MODEL), lambda j: (0, 0)), col(D_MODEL), col(1)],
        out_shape=[_sds((SEQ, D_MODEL)), _sds((D_MODEL, D_IN)), _sds((1, D_IN))],
        compiler_params=_params(("arbitrary",)),
    )(h, w_in, mix_ext, proj, dpa)


def _bucket_table():
    rel = (jnp.arange(BLOCK)[:, None] + BLOCK) - jnp.arange(2 * BLOCK)[None, :]
    n = jnp.maximum(rel, 0)
    max_exact = N_BUCKETS // 2
    large = max_exact + (jnp.log(jnp.maximum(n, 1).astype(F32) / max_exact)
                         / math.log(MAX_DISTANCE / max_exact) * (N_BUCKETS - max_exact)).astype(jnp.int32)
    large = jnp.minimum(large, N_BUCKETS - 1)
    return jnp.where(n < max_exact, n, large).astype(jnp.int32)


def _select_matrix(g, o):
    a = lax.broadcasted_iota(jnp.int32, (D_KV, D_KV), 0)
    b = lax.broadcasted_iota(jnp.int32, (D_KV, D_KV), 1)
    return ((a - HEAD_DIM * g == b - o) & (b >= o) & (b < o + HEAD_DIM)).astype(F32)


def _attn_block(q, kp, kc, vp, vc, bias, sinks, block_idx):
    kb = jnp.concatenate([kp, kc], axis=0)
    vb = jnp.concatenate([vp, vc], axis=0)
    row = lax.broadcasted_iota(jnp.int32, (BLOCK, 2 * BLOCK), 0)
    col = lax.broadcasted_iota(jnp.int32, (BLOCK, 2 * BLOCK), 1)
    rel = row + BLOCK - col
    mask = (rel >= 0) & (rel < BLOCK) & (col + (block_idx - 1) * BLOCK >= 0)
    lane8 = lax.broadcasted_iota(jnp.int32, (1, N_HEADS), 1)
    kt, vt = {}, {}
    for g in range(2):
        for o in (0, HEAD_DIM):
            sel = _select_matrix(g, o)
            kt[g, o] = mm(kb, sel)
            vt[g, o] = mm(vb, sel)
    outs = []
    for j in range(D_ATTN // LANES):
        qs = q[:, j * LANES:(j + 1) * LANES]
        acc = None
        for half in range(2):
            hq = 2 * j + half
            g, o = hq // 4, half * HEAD_DIM
            s = mm_nt(qs, kt[g, o]) * (HEAD_DIM ** -0.5) + bias[hq]
            s = jnp.where(mask, s, NEG_INF)
            sink = jnp.sum(jnp.where(lane8 == hq, sinks, 0.0), axis=1, keepdims=True)
            m = lax.stop_gradient(jnp.maximum(jnp.max(s, axis=-1, keepdims=True), sink))
            p = jnp.exp(s - m)
            probs = p / (jnp.sum(p, axis=-1, keepdims=True) + jnp.exp(sink - m))
            part = mm(probs, vt[g, o])
            acc = part if acc is None else acc + part
        outs.append(acc)
    return jnp.concatenate(outs, axis=1)


def _build_bias(rb_ref, bucket, bias_ref):
    for hq in range(N_HEADS):
        acc = jnp.zeros((BLOCK, 2 * BLOCK), F32)
        for b in range(N_BUCKETS):
            acc = jnp.where(bucket == b, rb_ref[b, hq], acc)
        bias_ref[hq] = acc


def _attn_in_specs():
    prev = lambda n: jnp.maximum(n - 1, 0)
    return [pl.BlockSpec((BLOCK, D_ATTN), lambda n: (n, 0)),
            pl.BlockSpec((BLOCK, D_KV), lambda n: (prev(n), D_ATTN // D_KV)),
            pl.BlockSpec((BLOCK, D_KV), lambda n: (n, D_ATTN // D_KV)),
            pl.BlockSpec((BLOCK, D_KV), lambda n: (prev(n), D_ATTN // D_KV + 1)),
            pl.BlockSpec((BLOCK, D_KV), lambda n: (n, D_ATTN // D_KV + 1)),
            pl.BlockSpec(memory_space=pltpu.SMEM),
            pl.BlockSpec((BLOCK, 2 * BLOCK), lambda n: (0, 0)),
            pl.BlockSpec((1, N_HEADS), lambda n: (0, 0))]


def attn_fwd(proj, rel_bias, bucket, sinks):
    def body(q_ref, kp_ref, kc_ref, vp_ref, vc_ref, rb_ref, bk_ref, sk_ref, o_ref, bias_ref):
        n = pl.program_id(0)

        @pl.when(n == 0)
        def _():
            _build_bias(rb_ref, bk_ref[...], bias_ref)

        o_ref[...] = _attn_block(q_ref[...], kp_ref[...], kc_ref[...], vp_ref[...], vc_ref[...],
                                 tuple(bias_ref[h] for h in range(N_HEADS)), sk_ref[...], n)

    return pl.pallas_call(
        body, grid=(N_BLOCKS,), name="attn_fwd",
        in_specs=_attn_in_specs(),
        out_specs=pl.BlockSpec((BLOCK, D_ATTN), lambda n: (n, 0)),
        out_shape=_sds((SEQ, D_ATTN)),
        scratch_shapes=[pltpu.VMEM((N_HEADS, BLOCK, 2 * BLOCK), F32)],
        compiler_params=_params(("arbitrary",)),
    )(proj, proj, proj, proj, proj, rel_bias, bucket, sinks)


def attn_bwd(proj, rel_bias, bucket, sinks, d_attn):
    def body(q_ref, kp_ref, kc_ref, vp_ref, vc_ref, rb_ref, bk_ref, sk_ref, do_ref,
             dq_ref, dkc_ref, dkp_ref, dvc_ref, dvp_ref, drb_ref, dsk_ref, bias_ref, dbias_ref):
        n = pl.program_id(0)

        @pl.when(n == 0)
        def _():
            _build_bias(rb_ref, bk_ref[...], bias_ref)
            dbias_ref[...] = jnp.zeros_like(dbias_ref)
            dsk_ref[...] = jnp.zeros_like(dsk_ref)

        f = lambda q, kp, kc, vp, vc, bias, sk: _attn_block(q, kp, kc, vp, vc, bias, sk, n)
        _, vjp = jax.vjp(f, q_ref[...], kp_ref[...], kc_ref[...], vp_ref[...], vc_ref[...],
                         tuple(bias_ref[h] for h in range(N_HEADS)), sk_ref[...])
        dq, dkp, dkc, dvp, dvc, dbias, dsk = vjp(do_ref[...])
        dq_ref[...] = dq
        dkc_ref[...] = dkc
        dkp_ref[...] = dkp
        dvc_ref[...] = dvc
        dvp_ref[...] = dvp
        for h in range(N_HEADS):
            dbias_ref[h] += dbias[h]
        dsk_ref[...] += dsk

        @pl.when(n == N_BLOCKS - 1)
        def _():
            bucket_v = bk_ref[...]
            rowi = lax.broadcasted_iota(jnp.int32, (N_BUCKETS, 2 * BLOCK), 0)
            lane = lax.broadcasted_iota(jnp.int32, (N_BUCKETS, N_HEADS), 1)
            out = jnp.zeros((N_BUCKETS, N_HEADS), F32)
            for hq in range(N_HEADS):
                dbh = dbias_ref[hq]
                rows = jnp.zeros((N_BUCKETS, 2 * BLOCK), F32)
                for b in range(N_BUCKETS):
                    part = jnp.sum(jnp.where(bucket_v == b, dbh, 0.0), axis=0, keepdims=True)
                    rows = jnp.where(rowi == b, part, rows)
                tot = jnp.sum(rows, axis=1, keepdims=True)
                out = jnp.where(lane == hq, tot, out)
            drb_ref[...] = out

    blk = lambda w: pl.BlockSpec((BLOCK, w), lambda n: (n, 0))
    return pl.pallas_call(
        body, grid=(N_BLOCKS,), name="attn_bwd",
        in_specs=_attn_in_specs() + [blk(D_ATTN)],
        out_specs=[blk(D_ATTN), blk(D_KV), blk(D_KV), blk(D_KV), blk(D_KV),
                   pl.BlockSpec((N_BUCKETS, N_HEADS), lambda n: (0, 0)),
                   pl.BlockSpec((1, N_HEADS), lambda n: (0, 0))],
        out_shape=[_sds((SEQ, D_ATTN)), _sds((SEQ, D_KV)), _sds((SEQ, D_KV)), _sds((SEQ, D_KV)),
                   _sds((SEQ, D_KV)), _sds((N_BUCKETS, N_HEADS)), _sds((1, N_HEADS))],
        scratch_shapes=[pltpu.VMEM((N_HEADS, BLOCK, 2 * BLOCK), F32),
                        pltpu.VMEM((N_HEADS, BLOCK, 2 * BLOCK), F32)],
        compiler_params=_params(("arbitrary",)),
    )(proj, proj, proj, proj, proj, rel_bias, bucket, sinks, d_attn)


def _chunk_masks(heads):
    c, hc = CHUNK, heads * CHUNK
    ri = lax.broadcasted_iota(jnp.int32, (hc, heads * HEAD_DIM), 0) // c
    li = lax.broadcasted_iota(jnp.int32, (hc, heads * HEAD_DIM), 1) // HEAD_DIM
    ba = lax.broadcasted_iota(jnp.int32, (hc, hc), 0) // c
    bb = lax.broadcasted_iota(jnp.int32, (hc, hc), 1) // c
    return (ri == li).astype(F32), (ba == bb).astype(F32)


def _bdiag(xc, blocks):
    return jnp.tile(xc, (xc.shape[1] // CHUNK, 1)) * blocks


def _neumann(l):
    c = CHUNK
    _, blocks = _chunk_masks(l.shape[1] // c)
    t = lax.broadcasted_iota(jnp.int32, l.shape, 0)
    i = lax.broadcasted_iota(jnp.int32, l.shape, 1) % c
    inv = (i == t).astype(F32) + l
    pw = dot3(l, _bdiag(l, blocks), NN)
    for _ in range(4):
        both = dot3(jnp.concatenate([inv, pw], axis=0), _bdiag(pw, blocks), NN)
        inv = inv + both[:c]
        pw = both[c:]
    return inv + dot3(inv, _bdiag(pw, blocks), NN)


@jax.custom_vjp
def neumann_inv(l):
    return _neumann(l)


def _neumann_fwd(l):
    inv = _neumann(l)
    return inv, inv


def _neumann_bwd(inv, g):
    c = CHUNK
    heads = inv.shape[1] // c
    _, blocks = _chunk_masks(heads)
    bd_t = _bdiag(inv, blocks).T
    inv_t = bd_t[0:c]
    for h in range(1, heads):
        inv_t = inv_t + bd_t[h * c:(h + 1) * c]
    return (dot3(dot3(inv_t, _bdiag(g, blocks), NN), bd_t, NN),)


neumann_inv.defvjp(_neumann_fwd, _neumann_bwd)


def _cumsum_raw(x, dims):
    c = x.shape[0]
    tt = lax.broadcasted_iota(jnp.int32, (c, c), 0)
    ii = lax.broadcasted_iota(jnp.int32, (c, c), 1)
    tri = (ii <= tt).astype(BF16)
    hi = x.astype(BF16)
    rest = x - hi.astype(F32)
    mid = rest.astype(BF16)
    lo = (rest - mid.astype(F32)).astype(BF16)
    return _dot(tri, hi, dims) + (_dot(tri, mid, dims) + _dot(tri, lo, dims))


@jax.custom_vjp
def cumsum_rows(x):
    return _cumsum_raw(x, NN)


def _cumsum_fwd(x):
    return _cumsum_raw(x, NN), None


def _cumsum_bwd(_, g):
    return (_cumsum_raw(g, TN),)


cumsum_rows.defvjp(_cumsum_fwd, _cumsum_bwd)


def _rwkv_chunk(s0, r, lw, k, v, kk, a):
    heads = r.shape[1] // HEAD_DIM
    c, hc = CHUNK, heads * CHUNK
    head_rows, blocks = _chunk_masks(heads)
    t = lax.broadcasted_iota(jnp.int32, (c, hc), 0)
    i = lax.broadcasted_iota(jnp.int32, (c, hc), 1) % c
    strict, incl = i < t, i <= t
    stack = lambda x: jnp.tile(x, (heads, 1)) * head_rows

    cum = cumsum_rows(lw)
    cum_end = jnp.sum(lw, axis=0, keepdims=True)
    beta = kk * a
    al = -kk * jnp.exp(cum - lw)
    p_inv = jnp.exp(-cum)
    be, kb, rb = beta * p_inv, k * p_inv, r * jnp.exp(cum)
    ar = jnp.concatenate([al, rb], axis=0)
    sv = stack(v)
    l_all = dot3(ar, jnp.concatenate([stack(be), stack(kb)], axis=0), NT)
    l_ab = jnp.where(strict, l_all[:c, :hc], 0.0)
    l_ak = jnp.where(strict, l_all[:c, hc:], 0.0)
    l_rb = jnp.where(incl, l_all[c:, :hc], 0.0)
    l_rk = jnp.where(incl, l_all[c:, hc:], 0.0)
    inv = neumann_inv(l_ab)
    from_s0 = dot3(ar, s0, NT)
    from_v = dot3(jnp.concatenate([l_ak, l_rk], axis=0), sv, NN)
    u = dot3(inv, stack(from_s0[:c] + from_v[:c]), NN)
    o = from_s0[c:] + from_v[c:] + dot3(l_rb, stack(u), NN)
    to_end = jnp.exp(cum_end - cum)
    s1 = s0 * jnp.exp(cum_end) + blocks * dot3(
        jnp.concatenate([u, v], axis=0), jnp.concatenate([beta * to_end, k * to_end], axis=0), TN)
    return o, s1


def call_with_comm(plan, middle_step, body, grid, name, in_specs, out_specs, out_shape, scratch_shapes, operands):
    n_in, n_out, n_scr = len(in_specs), len(out_specs), len(scratch_shapes)
    p_in, p_out = len(plan.ins), len(plan.out_shape)

    def fused(*refs):
        refs = list(refs)
        ins, refs = refs[:n_in], refs[n_in:]
        p_ins, refs = refs[:p_in], refs[p_in:]
        outs, refs = refs[:n_out], refs[n_out:]
        p_outs, refs = refs[:p_out], refs[p_out:]
        scr, p_sems = refs[:n_scr], refs[n_scr:]
        start, middle, finish = plan.stages(p_ins, p_outs, p_sems)
        step = pl.program_id(0)
        pl.when(step == 0)(start)
        body(*ins, *outs, *scr)
        pl.when(step == middle_step)(middle)
        pl.when(step == grid[0] - 1)(finish)

    any_spec = pl.BlockSpec(memory_space=pl.ANY)
    res = pl.pallas_call(
        fused, grid=grid, name=name,
        in_specs=list(in_specs) + [any_spec] * p_in, out_specs=list(out_specs) + [any_spec] * p_out,
        out_shape=list(out_shape) + list(plan.out_shape), scratch_shapes=list(scratch_shapes) + list(plan.scratch),
        compiler_params=_params(("arbitrary",)),
    )(*operands, *plan.ins)
    return res[:n_out], res[n_out:]


def rwkv_scan_fwd(r, lw, k, v, kk, a, plan):
    def body(r_ref, lw_ref, k_ref, v_ref, kk_ref, a_ref, o_ref, st_ref, s_ref):
        @pl.when(pl.program_id(0) == 0)
        def _():
            s_ref[...] = jnp.zeros_like(s_ref)

        for g in range(SCAN_GROUPS):
            lanes = slice(g * SCAN_WIDTH, (g + 1) * SCAN_WIDTH)
            s0 = s_ref[g]
            st_ref[0, g] = s0
            o, s1 = _rwkv_chunk(s0, *(ref[:, lanes] for ref in (r_ref, lw_ref, k_ref, v_ref, kk_ref, a_ref)))
            o_ref[:, lanes] = o
            s_ref[g] = s1

    tb = pl.BlockSpec((CHUNK, D_RWKV), lambda c: (c, 0))
    state = (SCAN_GROUPS, SCAN_WIDTH, SCAN_WIDTH)
    return call_with_comm(
        plan, 3 * N_CHUNKS // 4, body, (N_CHUNKS,), "rwkv_scan_fwd",
        [tb] * 6, [tb, pl.BlockSpec((1,) + state, lambda c: (c, 0, 0, 0))],
        [_sds((SEQ, D_RWKV)), _sds((N_CHUNKS,) + state)], [pltpu.VMEM(state, F32)], (r, lw, k, v, kk, a))


def rwkv_scan_bwd(r, lw, k, v, kk, a, states, d_o, plan):
    def body(r_ref, lw_ref, k_ref, v_ref, kk_ref, a_ref, st_ref, do_ref,
             dr_ref, dlw_ref, dk_ref, dv_ref, dkk_ref, da_ref, ds_ref):
        @pl.when(pl.program_id(0) == 0)
        def _():
            ds_ref[...] = jnp.zeros_like(ds_ref)

        for g in range(SCAN_GROUPS):
            lanes = slice(g * SCAN_WIDTH, (g + 1) * SCAN_WIDTH)
            _, vjp = jax.vjp(_rwkv_chunk, st_ref[0, g],
                             *(ref[:, lanes] for ref in (r_ref, lw_ref, k_ref, v_ref, kk_ref, a_ref)))
            grads = vjp((do_ref[:, lanes], ds_ref[g]))
            ds_ref[g] = grads[0]
            for ref, val in zip((dr_ref, dlw_ref, dk_ref, dv_ref, dkk_ref, da_ref), grads[1:]):
                ref[:, lanes] = val

    last = N_CHUNKS - 1
    tb = pl.BlockSpec((CHUNK, D_RWKV), lambda c: (last - c, 0))
    state = (SCAN_GROUPS, SCAN_WIDTH, SCAN_WIDTH)
    return call_with_comm(
        plan, N_CHUNKS // 4, body, (N_CHUNKS,), "rwkv_scan_bwd",
        [tb] * 6 + [pl.BlockSpec((1,) + state, lambda c: (last - c, 0, 0, 0)), tb], [tb] * 6,
        [_sds((SEQ, D_RWKV))] * 6, [pltpu.VMEM(state, F32)], (r, lw, k, v, kk, a, states, d_o))


def _ffn_mid(ug, uv, cg, cv, bg, bv):
    conv_g = bg + cg[0] * shift_rows(ug, 2) + cg[1] * shift_rows(ug, 1) + cg[2] * ug
    conv_v = bv + cv[0] * shift_rows(uv, 2) + cv[1] * shift_rows(uv, 1) + cv[2] * uv
    return jax.nn.gelu(conv_g, approximate=True) * conv_v


def _conv_rows(ref):
    return tuple(ref[0, j:j + 1, :] for j in range(3))


def _ffn_specs(tile):
    per = D_MODEL // tile
    half = N_DEV // 2
    w_g = pl.BlockSpec((1, D_MODEL, tile), lambda t: (t // per, 0, t % per))
    w_v = pl.BlockSpec((1, D_MODEL, tile), lambda t: (half + t // per, 0, t % per))
    c_g = pl.BlockSpec((1, 3, tile), lambda t: (t // per, 0, t % per))
    c_v = pl.BlockSpec((1, 3, tile), lambda t: (half + t // per, 0, t % per))
    b_g = pl.BlockSpec((1, tile), lambda t: (0, t))
    b_v = pl.BlockSpec((1, tile), lambda t: (0, D_FF // tile + t))
    w_d = pl.BlockSpec((tile, D_MODEL), lambda t: (t, 0))
    return w_g, w_v, c_g, c_v, b_g, b_v, w_d


def ffn_fwd(h2, w_up, conv_w, conv_b, w_down):
    def body(h_ref, wg_ref, wv_ref, cg_ref, cv_ref, bg_ref, bv_ref, wd_ref, f_ref):
        @pl.when(pl.program_id(0) == 0)
        def _():
            f_ref[...] = jnp.zeros_like(f_ref)

        h = h_ref[...]
        act = _ffn_mid(_dot(h, wg_ref[0], NN), _dot(h, wv_ref[0], NN), _conv_rows(cg_ref), _conv_rows(cv_ref),
                       bg_ref[...], bv_ref[...])
        f_ref[...] += _dot(act.astype(BF16), wd_ref[...], NN)

    full = pl.BlockSpec((SEQ, D_MODEL), lambda t: (0, 0))
    return pl.pallas_call(
        body, grid=(D_FF // FF_TILE,), name="ffn_fwd",
        in_specs=[full, *_ffn_specs(FF_TILE)],
        out_specs=full, out_shape=_sds((SEQ, D_MODEL)),
        compiler_params=_params(("arbitrary",)),
    )(h2, w_up, w_up, conv_w, conv_w, conv_b, conv_b, w_down)


def ffn_bwd(h2, w_up, conv_w, conv_b, w_down, df):
    tile = FF_TILE_BWD
    per = D_MODEL // tile

    def body(h_hbm, wg_ref, wv_ref, cg_ref, cv_ref, bg_ref, bv_ref, wd_ref, df_hbm,
             dh_hbm, dup_hbm, dcg_ref, dcv_ref, dbg_ref, dbv_ref, dwd_ref,
             h_ref, df_ref, dh_ref, dwg_ref, dwv_ref, sem, up_sems):
        t = pl.program_id(0)

        @pl.when(t == 0)
        def _():
            pltpu.sync_copy(h_hbm, h_ref)
            pltpu.sync_copy(df_hbm, df_ref)

        h, df_b, wg, wv = h_ref[...], df_ref[...], wg_ref[0], wv_ref[0]
        act, vjp = jax.vjp(_ffn_mid, _dot(h, wg, NN), _dot(h, wv, NN), _conv_rows(cg_ref), _conv_rows(cv_ref),
                           bg_ref[...], bv_ref[...])
        dwd_ref[...] = _dot(act.astype(BF16), df_b, TN)
        dug, duv, dcg, dcv, dbg, dbv = vjp(_dot(df_b, wd_ref[...], NT))
        dug, duv = dug.astype(BF16), duv.astype(BF16)
        cols = pl.ds(pl.multiple_of((t % per) * tile, tile), tile)
        to_gate = pltpu.make_async_copy(dwg_ref, dup_hbm.at[t // per, :, cols], up_sems.at[0])
        to_value = pltpu.make_async_copy(dwv_ref, dup_hbm.at[N_DEV // 2 + t // per, :, cols], up_sems.at[1])
        dwg_ref[...] = _dot(h, dug, TN)
        to_gate.start()
        dwv_ref[...] = _dot(h, duv, TN)
        to_value.start()

        @pl.when(pl.program_id(0) == 0)
        def _():
            dh_ref[...] = jnp.zeros_like(dh_ref)

        dh_ref[...] += _dot(dug, wg, NT) + _dot(duv, wv, NT)
        for j in range(3):
            dcg_ref[0, j:j + 1, :] = dcg[j]
            dcv_ref[0, j:j + 1, :] = dcv[j]
        dbg_ref[...] = dbg
        dbv_ref[...] = dbv
        to_gate.wait()
        to_value.wait()

        @pl.when(t == D_FF // tile - 1)
        def _():
            cp = pltpu.make_async_copy(dh_ref, dh_hbm, sem)
            cp.start()
            cp.wait()

    hbm = pl.BlockSpec(memory_space=pl.ANY)
    w_g, w_v, c_g, c_v, b_g, b_v, w_d = _ffn_specs(tile)
    return pl.pallas_call(
        body, grid=(D_FF // tile,), name="ffn_bwd",
        in_specs=[hbm, w_g, w_v, c_g, c_v, b_g, b_v, w_d, hbm],
        out_specs=[hbm, hbm, c_g, c_v, b_g, b_v, w_d],
        out_shape=[_sds((SEQ, D_MODEL)), _sds((N_DEV, D_MODEL, D_MODEL)),
                   _sds((N_DEV, 3, D_MODEL)), _sds((N_DEV, 3, D_MODEL)), _sds((1, 2 * D_FF)), _sds((1, 2 * D_FF)),
                   _sds((D_FF, D_MODEL))],
        scratch_shapes=[pltpu.VMEM((SEQ, D_MODEL), BF16), pltpu.VMEM((SEQ, D_MODEL), BF16),
                        pltpu.VMEM((SEQ, D_MODEL), F32), pltpu.VMEM((D_MODEL, tile), F32),
                        pltpu.VMEM((D_MODEL, tile), F32), pltpu.SemaphoreType.DMA, pltpu.SemaphoreType.DMA((2,))],
        compiler_params=_params(("arbitrary",)),
    )(h2, w_up, w_up, conv_w, conv_w, conv_b, conv_b, w_down, df)


def ffn_bwd_mid(h2, w_up, conv_w, conv_b, w_down, df):
    tile, rows, halo = FF_TILE, FF_ROW_CHUNK, FF_HALO
    ext = rows + 2 * halo

    def body(h_hbm, wg_ref, wv_ref, cg_ref, cv_ref, bg_ref, bv_ref, wd_ref, df_hbm,
             dug_ref, duv_ref, dcg_ref, dcv_ref, dbg_ref, dbv_ref, dwd_ref,
             h_ref, df_ref, ug_ref, uv_ref, da_ref, act_ref):
        @pl.when(pl.program_id(0) == 0)
        def _():
            pltpu.sync_copy(h_hbm, h_ref)
            pltpu.sync_copy(df_hbm, df_ref)
            for ref in (ug_ref, uv_ref, da_ref):
                ref[0:halo, :] = jnp.zeros((halo, tile), F32)
                ref[halo + SEQ:, :] = jnp.zeros((halo, tile), F32)

        h, df_b = h_ref[...], df_ref[...]
        ug_ref[halo:halo + SEQ, :] = _dot(h, wg_ref[0], NN)
        uv_ref[halo:halo + SEQ, :] = _dot(h, wv_ref[0], NN)
        da_ref[halo:halo + SEQ, :] = _dot(df_b, wd_ref[...], NT)
        cg, cv, bg, bv = _conv_rows(cg_ref), _conv_rows(cv_ref), bg_ref[...], bv_ref[...]
        down = lambda x, n: pltpu.roll(x, n, 0)
        up = lambda x, n: pltpu.roll(x, ext - n, 0)
        mid = slice(halo, halo + rows)

        def chunk(i, sums):
            r0 = pl.multiple_of(i * rows, rows)
            window = pl.ds(r0, ext)
            ug, uv, da = ug_ref[window, :], uv_ref[window, :], da_ref[window, :]
            ug1, ug2, uv1, uv2 = down(ug, 1), down(ug, 2), down(uv, 1), down(uv, 2)
            conv_g = bg + cg[0] * ug2 + cg[1] * ug1 + cg[2] * ug
            conv_v = bv + cv[0] * uv2 + cv[1] * uv1 + cv[2] * uv
            act, vjp = jax.vjp(lambda a, b: jax.nn.gelu(a, approximate=True) * b, conv_g, conv_v)
            dcg, dcv = vjp(da)
            dug = cg[2] * dcg + cg[1] * up(dcg, 1) + cg[0] * up(dcg, 2)
            duv = cv[2] * dcv + cv[1] * up(dcv, 1) + cv[0] * up(dcv, 2)
            out = pl.ds(r0, rows)
            act_ref[out, :] = act[mid].astype(BF16)
            dug_ref[out, :] = dug[mid].astype(BF16)
            duv_ref[out, :] = duv[mid].astype(BF16)
            col = lambda x: jnp.sum(x[mid], axis=0, keepdims=True)
            new = (col(dcg * ug2), col(dcg * ug1), col(dcg * ug), col(dcv * uv2), col(dcv * uv1), col(dcv * uv),
                   col(dcg), col(dcv))
            return tuple(s + n for s, n in zip(sums, new))

        zero = jnp.zeros((1, tile), F32)
        sums = lax.fori_loop(0, SEQ // rows, chunk, (zero,) * 8)
        for j in range(3):
            dcg_ref[0, j:j + 1, :] = sums[j]
            dcv_ref[0, j:j + 1, :] = sums[3 + j]
        dbg_ref[...] = sums[6]
        dbv_ref[...] = sums[7]
        dwd_ref[...] = _dot(act_ref[...], df_b, TN)

    hbm = pl.BlockSpec(memory_space=pl.ANY)
    w_g, w_v, c_g, c_v, b_g, b_v, w_d = _ffn_specs(tile)
    col = pl.BlockSpec((SEQ, tile), lambda t: (0, t))
    padded = pltpu.VMEM((SEQ + 2 * halo, tile), F32)
    return pl.pallas_call(
        body, grid=(D_FF // tile,), name="ffn_bwd_mid",
        in_specs=[hbm, w_g, w_v, c_g, c_v, b_g, b_v, w_d, hbm],
        out_specs=[col, col, c_g, c_v, b_g, b_v, w_d],
        out_shape=[_sds((SEQ, D_FF), BF16), _sds((SEQ, D_FF), BF16), _sds((N_DEV, 3, D_MODEL)),
                   _sds((N_DEV, 3, D_MODEL)), _sds((1, 2 * D_FF)), _sds((1, 2 * D_FF)), _sds((D_FF, D_MODEL))],
        scratch_shapes=[pltpu.VMEM((SEQ, D_MODEL), BF16), pltpu.VMEM((SEQ, D_MODEL), BF16), padded, padded, padded,
                        pltpu.VMEM((SEQ, tile), BF16)],
        compiler_params=_params(("arbitrary",)),
    )(h2, w_up, w_up, conv_w, conv_w, conv_b, conv_b, w_down, df)


def ffn_bwd_up(h2, w_up, dug, duv):
    tile = FF_TILE
    per = D_MODEL // tile

    def body(h_hbm, wg_ref, wv_ref, dug_ref, duv_ref, dh_hbm, dup_hbm, h_ref, dh_ref, dwg_ref, dwv_ref, sem, up_sems):
        t = pl.program_id(0)

        @pl.when(t == 0)
        def _():
            pltpu.sync_copy(h_hbm, h_ref)
            dh_ref[...] = jnp.zeros_like(dh_ref)

        h, dug_b, duv_b = h_ref[...], dug_ref[...], duv_ref[...]
        cols = pl.ds(pl.multiple_of((t % per) * tile, tile), tile)
        to_gate = pltpu.make_async_copy(dwg_ref, dup_hbm.at[t // per, :, cols], up_sems.at[0])
        to_value = pltpu.make_async_copy(dwv_ref, dup_hbm.at[N_DEV // 2 + t // per, :, cols], up_sems.at[1])
        dwg_ref[...] = _dot(h, dug_b, TN)
        to_gate.start()
        dwv_ref[...] = _dot(h, duv_b, TN)
        to_value.start()
        dh_ref[...] += _dot(jnp.concatenate([dug_b, duv_b], axis=1),
                            jnp.concatenate([wg_ref[0], wv_ref[0]], axis=1), NT)
        to_gate.wait()
        to_value.wait()

        @pl.when(t == D_FF // tile - 1)
        def _():
            cp = pltpu.make_async_copy(dh_ref, dh_hbm, sem)
            cp.start()
            cp.wait()

    hbm = pl.BlockSpec(memory_space=pl.ANY)
    w_g, w_v = _ffn_specs(tile)[:2]
    col = pl.BlockSpec((SEQ, tile), lambda t: (0, t))
    return pl.pallas_call(
        body, grid=(D_FF // tile,), name="ffn_bwd_up",
        in_specs=[hbm, w_g, w_v, col, col], out_specs=[hbm, hbm],
        out_shape=[_sds((SEQ, D_MODEL)), _sds((N_DEV, D_MODEL, D_MODEL))],
        scratch_shapes=[pltpu.VMEM((SEQ, D_MODEL), BF16), pltpu.VMEM((SEQ, D_MODEL), F32),
                        pltpu.VMEM((D_MODEL, tile), F32), pltpu.VMEM((D_MODEL, tile), F32),
                        pltpu.SemaphoreType.DMA, pltpu.SemaphoreType.DMA((2,))],
        compiler_params=_params(("arbitrary",)),
    )(h2, w_up, w_up, dug, duv)


def loss_head(x1, f, target, n_post):
    def tile_loss(x1_t, f_t, g, tgt):
        err = x1_t + _rms(f_t, g) - tgt
        return 0.5 * jnp.sum(jnp.mean(err * err, axis=-1))

    def body(x_ref, f_ref, t_ref, g_ref, dx_ref, df_ref, dg_ref, loss_ref):
        val, (dx, df, dg) = jax.value_and_grad(tile_loss, argnums=(0, 1, 2))(
            x_ref[...], f_ref[...], g_ref[...], t_ref[...])
        dx_ref[...] = dx
        df_ref[...] = df.astype(BF16)

        @pl.when(pl.program_id(0) == 0)
        def _():
            dg_ref[...] = jnp.zeros_like(dg_ref)
            loss_ref[...] = jnp.zeros_like(loss_ref)

        dg_ref[...] += dg
        loss_ref[...] += jnp.full((1, LANES), val, F32)

    tile = pl.BlockSpec((TOK_TILE, D_MODEL), lambda i: (i, 0))
    vec = pl.BlockSpec((1, D_MODEL), lambda i: (0, 0))
    return pl.pallas_call(
        body, grid=(SEQ // TOK_TILE,), name="loss_head",
        in_specs=[tile, tile, tile, vec],
        out_specs=[tile, tile, vec, pl.BlockSpec((1, LANES), lambda i: (0, 0))],
        out_shape=[_sds((SEQ, D_MODEL)), _sds((SEQ, D_MODEL), BF16), _sds((1, D_MODEL)), _sds((1, LANES))],
        compiler_params=_params(("arbitrary",)),
    )(x1, f, target, n_post)


def _mesh_pos():
    return lax.axis_index("x"), lax.axis_index("y"), lax.axis_index("c")


def _flip(pos, rel):
    x, y, c = pos
    return (1 - x if rel & 4 else x, 1 - y if rel & 2 else y, 1 - c if rel & 1 else c)


def _slot(pos):
    x, y, c = pos
    return 4 * x + 2 * y + c


def cast_bf16(w, rows):
    def body(w_ref, o_ref):
        o_ref[...] = w_ref[...].astype(BF16)

    spec = pl.BlockSpec((rows, w.shape[1]), lambda i: (i, 0))
    return pl.pallas_call(body, grid=(w.shape[0] // rows,), name="cast_bf16_%dx%d" % w.shape,
                          in_specs=[spec], out_specs=spec, out_shape=_sds(w.shape, BF16),
                          compiler_params=_params(("arbitrary",)))(w)


class CommPlan:
    def __init__(self, ins, out_shape, scratch, stages):
        self.ins, self.out_shape, self.scratch, self.stages = ins, out_shape, scratch, stages


def run_comm(name, plan):
    n_in, n_out = len(plan.ins), len(plan.out_shape)

    def body(*refs):
        for stage in plan.stages(refs[:n_in], refs[n_in:n_in + n_out], refs[n_in + n_out:]):
            stage()

    any_spec = pl.BlockSpec(memory_space=pl.ANY)
    return pl.pallas_call(
        body, name=name, in_specs=[any_spec] * len(plan.ins), out_specs=[any_spec] * len(plan.out_shape),
        out_shape=plan.out_shape, scratch_shapes=plan.scratch)(*plan.ins)


def gather_plan(shards):
    n = len(shards)

    def stages(srcs, outs, sems):
        send_sems, recv_sems, local_sems = sems

        def places():
            me = _mesh_pos()
            return me, _flip(me, 1), [_flip(me, 2), _flip(me, 4), _flip(me, 6)]

        def copy(a, k, block, to, src=None):
            dst = outs[a].at[_slot(block)]
            return pltpu.make_async_remote_copy(
                src_ref=dst if src is None else src, dst_ref=dst,
                send_sem=send_sems.at[7 * a + k], recv_sem=recv_sems.at[7 * a + k],
                device_id=to, device_id_type=pl.DeviceIdType.MESH)

        def local(a, me):
            return pltpu.make_async_copy(srcs[a], outs[a].at[_slot(me)], local_sems.at[a])

        def own(a, me, sibling, chips):
            return [copy(a, 0, me, sibling, src=srcs[a])] + [
                copy(a, 1 + j, me, chip, src=srcs[a]) for j, chip in enumerate(chips)]

        def start():
            me, sibling, chips = places()
            for a in range(n):
                local(a, me).start()
                for cp in own(a, me, sibling, chips):
                    cp.start()

        def forward():
            me, sibling, chips = places()
            for j, chip in enumerate(chips):
                for a in range(n):
                    copy(a, 1 + j, chip, me).wait_recv()
                    copy(a, 4 + j, chip, sibling).start()

        def finish():
            me, sibling, chips = places()
            for a in range(n):
                copy(a, 0, sibling, me).wait_recv()
                for j, chip in enumerate(chips):
                    copy(a, 4 + j, _flip(chip, 1), me).wait_recv()
            for a in range(n):
                for cp in own(a, me, sibling, chips):
                    cp.wait_send()
                for j, chip in enumerate(chips):
                    copy(a, 4 + j, chip, sibling).wait_send()
                local(a, me).wait()

        return start, forward, finish

    return CommPlan(list(shards), [_sds((N_DEV,) + s.shape, s.dtype) for s in shards],
                    [pltpu.SemaphoreType.DMA((7 * n,)), pltpu.SemaphoreType.DMA((7 * n,)),
                     pltpu.SemaphoreType.DMA((n,))], stages)


def exchange_plan(parts, replicated, rels, members, index, member_axis, own_copy):
    n, nr = len(parts), len(rels)
    pick_index = (slice(None),) * member_axis + (0,)
    subs = [1 if (r or member_axis == 0) else p.shape[0] for p, r in zip(parts, replicated)]
    first = [sum(subs[:a]) for a in range(n)]
    total = sum(subs)

    def stages(srcs, outs, sems):
        send_sems, recv_sems, local_sems = sems

        def src(a, s, pos):
            if replicated[a]:
                return srcs[a]
            return srcs[a].at[index(pos)] if member_axis == 0 else srcs[a].at[s, index(pos)]

        def dst(a, s, pos):
            block = outs[a].at[index(pos)]
            return block if (replicated[a] or member_axis == 0) else block.at[s]

        def copy(a, s, j, me, src_pos, dst_pos):
            sem = nr * (first[a] + s) + j
            return pltpu.make_async_remote_copy(
                src_ref=src(a, s, src_pos), dst_ref=dst(a, s, dst_pos),
                send_sem=send_sems.at[sem], recv_sem=recv_sems.at[sem],
                device_id=_flip(me, rels[j]), device_id_type=pl.DeviceIdType.MESH)

        pieces = [(a, s) for a in range(n) for s in range(subs[a])]

        def local(a, s, me):
            return pltpu.make_async_copy(src(a, s, me), dst(a, s, me), local_sems.at[first[a] + s])

        def sends(me):
            return [copy(a, s, j, me, _flip(me, rels[j]), me) for j in range(nr) for a, s in pieces]

        own = pieces if own_copy else []

        def start():
            me = _mesh_pos()
            for cp in sends(me) + [local(a, s, me) for a, s in own]:
                cp.start()

        def middle():
            pass

        def finish():
            me = _mesh_pos()
            for j in range(nr):
                for a, s in pieces:
                    copy(a, s, j, me, me, _flip(me, rels[j])).wait_recv()
            for cp in sends(me):
                cp.wait_send()
            for a, s in own:
                local(a, s, me).wait()

        return start, middle, finish

    shapes = [p.shape if r else jax.eval_shape(lambda t: t[pick_index], p).shape for p, r in zip(parts, replicated)]
    return CommPlan(list(parts), [_sds((members,) + s, p.dtype) for s, p in zip(shapes, parts)],
                    [pltpu.SemaphoreType.DMA((nr * total,)), pltpu.SemaphoreType.DMA((nr * total,)),
                     pltpu.SemaphoreType.DMA((total,))], stages)


def pair_plan(parts, replicated):
    return exchange_plan(parts, replicated, [1], 2, lambda pos: pos[2], 1, False)


def chip_plan(parts, replicated):
    return exchange_plan(parts, replicated, [2, 4, 6], 4, lambda pos: 2 * pos[0] + pos[1], 0, True)


def add_pair(name, mine, swapped, out_dtype, rows):
    def body(m_ref, s_ref, o_ref):
        own = m_ref[0, 0] if mine.ndim == 4 else m_ref[0]
        o_ref[0] = (own + s_ref[0, 0]).astype(o_ref.dtype)

    _, n, r, c = swapped.shape
    core = lambda: lax.axis_index("c")
    if mine.ndim == 4:
        mine_spec = pl.BlockSpec((1, 1, rows, c), lambda i, j: (i, core(), j, 0))
    else:
        mine_spec = pl.BlockSpec((1, rows, c), lambda i, j: (i, j, 0))
    return pl.pallas_call(
        body, grid=(n, r // rows), name=name,
        in_specs=[mine_spec, pl.BlockSpec((1, 1, rows, c), lambda i, j: (1 - core(), i, j, 0))],
        out_specs=pl.BlockSpec((1, rows, c), lambda i, j: (i, j, 0)),
        out_shape=_sds((n, r, c), out_dtype),
        compiler_params=_params(("arbitrary", "arbitrary")),
    )(mine, swapped)


def add_pair_small(mines, swappeds):
    n = len(mines)
    halves = [m.ndim == s.ndim for m, s in zip(mines, swappeds)]

    def body(*refs):
        c = lax.axis_index("c")
        for i in range(n):
            m_ref, s_ref, o_ref = refs[i], refs[n + i], refs[2 * n + i]
            o_ref[...] = (m_ref[:, c] if halves[i] else m_ref[...]) + s_ref[1 - c]

    vmem = pl.BlockSpec(memory_space=pltpu.VMEM)
    return pl.pallas_call(
        body, name="pair_add_small", in_specs=[vmem] * (2 * n), out_specs=[vmem] * n,
        out_shape=[_sds(s.shape[1:]) for s in swappeds], compiler_params=_params(),
    )(*mines, *swappeds)


def _adamw_math(w, g, m, v):
    nm = ADAM_B1 * m + (1.0 - ADAM_B1) * g
    nv = ADAM_B2 * v + (1.0 - ADAM_B2) * (g * g)
    m_hat = nm / (1.0 - ADAM_B1 ** ADAM_STEP)
    v_hat = nv / (1.0 - ADAM_B2 ** ADAM_STEP)
    return -ADAM_LR * (m_hat / (jnp.sqrt(v_hat) + ADAM_EPS) + ADAM_WD * w), nm, nv


def adamw_small(ws, parts, ms, vs):
    n = len(ws)

    def body(*refs):
        for i in range(n):
            w_ref, p_ref, m_ref, v_ref = (refs[k * n + i] for k in range(4))
            g = p_ref[0]
            for j in range(1, p_ref.shape[0]):
                g = g + p_ref[j]
            delta, nm, nv = _adamw_math(w_ref[...], g, m_ref[...], v_ref[...])
            for k, val in enumerate((g, delta, nm, nv)):
                refs[(4 + k) * n + i][...] = val

    vmem = pl.BlockSpec(memory_space=pltpu.VMEM)
    outs = pl.pallas_call(
        body, name="adamw_small", in_specs=[vmem] * (4 * n), out_specs=[vmem] * (4 * n),
        out_shape=[_sds(w.shape) for w in ws] * 4, compiler_params=_params(),
    )(*ws, *parts, *ms, *vs)
    return [outs[k * n:(k + 1) * n] for k in range(4)]

def adamw(name, w, parts, m, v, rows):
    n_parts = parts.shape[0]

    def body(w_ref, p_ref, m_ref, v_ref, g_ref, d_ref, nm_ref, nv_ref):
        g = p_ref[0].astype(F32)
        for j in range(1, n_parts):
            g = g + p_ref[j].astype(F32)
        g_ref[...] = g
        d_ref[...], nm_ref[...], nv_ref[...] = _adamw_math(w_ref[...], g, m_ref[...], v_ref[...])

    cols = w.shape[1]
    spec = pl.BlockSpec((rows, cols), lambda i: (i, 0))
    return pl.pallas_call(
        body, grid=(w.shape[0] // rows,), name=name,
        in_specs=[spec, pl.BlockSpec((n_parts, rows, cols), lambda i: (0, i, 0)), spec, spec],
        out_specs=[spec] * 4, out_shape=[_sds(w.shape)] * 4,
        compiler_params=_params(("arbitrary",)),
    )(w, parts, m, v)


def _rows128(a):
    flat = a.reshape(-1)
    pad = (-flat.shape[0]) % LANES
    if pad:
        flat = jnp.concatenate([flat, jnp.zeros((pad,), flat.dtype)])
    return flat.reshape(-1, LANES)


def _pack(arrays):
    rows = [_rows128(a) for a in arrays]
    pad = (-sum(r.shape[0] for r in rows)) % 8
    return jnp.concatenate(rows + [jnp.zeros((pad, LANES), rows[0].dtype)] * (pad > 0), axis=0)


def _unpack(packed, like):
    out, row = [], 0
    for a in like:
        n = math.prod(a.shape)
        rows = -(-n // LANES)
        out.append(packed[row:row + rows].reshape(-1)[:n].reshape(a.shape))
        row += rows
    return out


def _to_slots(full, per):
    return full.reshape(full.shape[0], N_DEV, per).transpose(1, 0, 2)


def _from_slots(slots):
    return slots.transpose(1, 0, 2).reshape(slots.shape[1], -1)


def kernel(x, norm_mix_pre, norm_mix_post, norm_ffn_pre, norm_ffn_post, w_in, rel_bias, sinks, rwkv_shift_mix, w0, w_decay_up, a0, w_iclr_up, w_gate_up, k_k, k_a, r_k, ln_x_g, ln_x_b, w_out, w_ffn_up, conv_w, conv_b, w_ffn_down, loss_target, m_norm_mix_pre, m_norm_mix_post, m_norm_ffn_pre, m_norm_ffn_post, m_w_in, m_rel_bias, m_sinks, m_rwkv_shift_mix, m_w0, m_w_decay_up, m_a0, m_w_iclr_up, m_w_gate_up, m_k_k, m_k_a, m_r_k, m_ln_x_g, m_ln_x_b, m_w_out, m_w_ffn_up, m_conv_w, m_conv_b, m_w_ffn_down, v_norm_mix_pre, v_norm_mix_post, v_norm_ffn_pre, v_norm_ffn_post, v_w_in, v_rel_bias, v_sinks, v_rwkv_shift_mix, v_w0, v_w_decay_up, v_a0, v_w_iclr_up, v_w_gate_up, v_k_k, v_k_a, v_r_k, v_ln_x_g, v_ln_x_b, v_w_out, v_w_ffn_up, v_conv_w, v_conv_b, v_w_ffn_down):
    x2 = x[0]
    target = loss_target[0]

    g_in, g_out, g_decay, g_iclr, g_gate, g_conv = run_comm("all_gather_mixer", gather_plan([
        cast_bf16(w_in[0], 256), cast_bf16(w_out[0], 128), w_decay_up[0], w_iclr_up[0], w_gate_up[0], conv_w[0]]))
    ffn_gather = gather_plan([cast_bf16(w_ffn_up[0], 256), cast_bf16(w_ffn_down[0], 256)])
    w_in_b = _from_slots(g_in)
    w_out_b = g_out.reshape(D_MODEL, D_MODEL)
    lora = jnp.zeros((HEAD_DIM, D_RWKV), F32)
    wd_pad = jnp.concatenate([_from_slots(g_decay), lora], axis=0)
    wi_pad = jnp.concatenate([lora, _from_slots(g_iclr)], axis=0)
    wg_full = _from_slots(g_gate)
    mix_ext = jnp.concatenate([jnp.zeros((1, D_QKV), F32), rwkv_shift_mix], axis=1)
    r_k_row = r_k.reshape(1, D_RWKV)
    bucket = _bucket_table()

    (h1,) = tok_fwd("rms_mix_pre", rms_tile, [x2], [norm_mix_pre], [], [D_MODEL], [BF16])
    proj, ps = in_proj_fwd(h1, w_in_b, mix_ext)
    attn = attn_fwd(proj, rel_bias, bucket, sinks)
    pre_params = [w0, wd_pad, a0, wi_pad, wg_full, k_k, k_a]
    r_, lw_, k2_, v_, kk_, a_, gate_ = tok_fwd("rwkv_pre", rwkv_pre_tile, [ps], pre_params, [],
                                               [D_RWKV] * 7, [F32] * 7)
    (o_, states), (g_up, g_down) = rwkv_scan_fwd(r_, lw_, k2_, v_, kk_, a_, ffn_gather)
    w_down_b = g_down.reshape(D_FF, D_MODEL)
    mix_tiles = [o_, r_, k2_, v_, gate_, attn, x2]
    mix_params = [w_out_b, norm_mix_post, ln_x_g, ln_x_b, r_k_row]
    (x1,) = tok_fwd("mix_out", mix_out_tile, mix_tiles, mix_params, [(D_MODEL, D_MODEL)], [D_MODEL], [F32])
    (h2,) = tok_fwd("rms_ffn_pre", rms_tile, [x1], [norm_ffn_pre], [], [D_MODEL], [BF16])
    f = ffn_fwd(h2, g_up, g_conv, conv_b, w_down_b)
    dy, df, d_n_ffn_post, loss_row = loss_head(x1, f, target, norm_ffn_post)
    loss = lax.psum(loss_row[0, 0], ("x", "y", "c"))

    d_ug, d_uv, d_cw_g, d_cw_v, d_cb_g, d_cb_v, d_down = ffn_bwd_mid(h2, g_up, g_conv, conv_b, w_down_b, df)
    dh2, d_up = ffn_bwd_up(h2, g_up, d_ug, d_uv)
    half = N_DEV // 2
    d_cw = jnp.concatenate([d_cw_g[:half], d_cw_v[half:]], axis=0)
    by_pair = lambda slots: slots.reshape((N_DEV // 2, 2) + slots.shape[1:])
    ffn_mine = [by_pair(d_up), by_pair(d_down.reshape(N_DEV, D_FF // N_DEV, D_MODEL))]
    ffn_swapped = run_comm("pair_exchange_ffn", pair_plan(ffn_mine, [False, False]))
    ffn_exchange = chip_plan([add_pair("pair_add_w_ffn_up", ffn_mine[0], ffn_swapped[0], BF16, 256),
                              add_pair("pair_add_w_ffn_down", ffn_mine[1], ffn_swapped[1], BF16, 256)],
                             [False, False])
    d_cb = jnp.concatenate([d_cb_g[:, :D_FF], d_cb_v[:, D_FF:]], axis=1)
    dx1_ffn, d_n_ffn_pre = tok_bwd("rms_ffn_pre_bwd", rms_tile, [x1], [norm_ffn_pre], [], [dh2], [0])
    (d_o, d_r1, d_k1, d_v1, d_gate, d_attn, dx_res, d_n_mix_post, d_ln_g, d_ln_b, d_r_k,
     d_w_out) = tok_bwd("mix_out_bwd", mix_out_tile, mix_tiles, mix_params, [(D_MODEL, D_MODEL)], [(dy, dx1_ffn)],
                        [1, 2, 3, 4])
    (d_r2, d_lw, d_k2, d_v2, d_kk, d_a), (got_up, got_down) = rwkv_scan_bwd(
        r_, lw_, k2_, v_, kk_, a_, states, d_o, ffn_exchange)
    pre_cots = [(d_r1, d_r2), d_lw, (d_k1, d_k2), (d_v1, d_v2), d_kk, d_a, d_gate]
    (d_ps, d_w0, d_wd_pad, d_a0, d_wi_pad, d_wg, d_k_k, d_k_a) = tok_bwd(
        "rwkv_pre_bwd", rwkv_pre_tile, [ps], pre_params, [], pre_cots, [0, 1, 2, 3, 4, 5, 6])
    dq, dkc, dkp, dvc, dvp, d_rel_bias, d_sinks = attn_bwd(proj, rel_bias, bucket, sinks, d_attn)
    zero_blk = jnp.zeros((BLOCK, D_KV), F32)
    dk = dkc + jnp.concatenate([dkp[BLOCK:], zero_blk], axis=0)
    dv = dvc + jnp.concatenate([dvp[BLOCK:], zero_blk], axis=0)
    dpa = jnp.concatenate([dq, dk, dv, d_ps], axis=1)
    dh1, d_w_in, d_mix_ext = in_proj_bwd(h1, w_in_b, mix_ext, proj, dpa)
    grad_x2, d_n_mix_pre = tok_bwd("rms_mix_pre_bwd", rms_tile, [x2], [norm_mix_pre], [], [dh1], [0], {0: dx_res})
    grad_x = grad_x2[None]

    small_rep = [d_n_mix_pre, d_n_mix_post, d_n_ffn_pre, d_n_ffn_post, d_rel_bias, d_sinks,
                 d_mix_ext[:, D_QKV:], d_w0, d_a0, d_k_k, d_k_a, d_r_k.reshape(r_k.shape), d_ln_g, d_ln_b, d_cb]
    rep_w = [norm_mix_pre, norm_mix_post, norm_ffn_pre, norm_ffn_post, rel_bias, sinks, rwkv_shift_mix,
             w0, a0, k_k, k_a, r_k, ln_x_g, ln_x_b, conv_b]
    rep_m = [m_norm_mix_pre, m_norm_mix_post, m_norm_ffn_pre, m_norm_ffn_post, m_rel_bias, m_sinks,
             m_rwkv_shift_mix, m_w0, m_a0, m_k_k, m_k_a, m_r_k, m_ln_x_g, m_ln_x_b, m_conv_b]
    rep_v = [v_norm_mix_pre, v_norm_mix_post, v_norm_ffn_pre, v_norm_ffn_post, v_rel_bias, v_sinks,
             v_rwkv_shift_mix, v_w0, v_a0, v_k_k, v_k_a, v_r_k, v_ln_x_g, v_ln_x_b, v_conv_b]
    sh_w = [w_decay_up, w_iclr_up, w_gate_up, conv_w]
    sh_m = [m_w_decay_up, m_w_iclr_up, m_w_gate_up, m_conv_w]
    sh_v = [v_w_decay_up, v_w_iclr_up, v_w_gate_up, v_conv_w]
    sh_parts = [_to_slots(d_wd_pad[:HEAD_DIM], HEAD_DIM), _to_slots(d_wi_pad[HEAD_DIM:], HEAD_DIM),
                _to_slots(d_wg, HEAD_DIM), d_cw]
    n_rep, n_sh = len(small_rep), len(sh_parts)
    mine = [by_pair(_to_slots(d_w_in, D_IN // N_DEV)), by_pair(d_w_out.reshape(N_DEV, D_MODEL // N_DEV, D_MODEL)),
            *small_rep, *(by_pair(p) for p in sh_parts)]
    is_rep = [False, False] + [True] * n_rep + [False] * n_sh
    swapped = run_comm("pair_exchange_mixer", pair_plan(mine, is_rep))
    chip_sums = [add_pair("pair_add_w_in", mine[0], swapped[0], BF16, 512),
                 add_pair("pair_add_w_out", mine[1], swapped[1], BF16, 128),
                 *add_pair_small(mine[2:], swapped[2:])]
    got = run_comm("chip_exchange_mixer", chip_plan(chip_sums, is_rep))

    big = [adamw("adamw_w_in", w_in[0], got[0], m_w_in[0], v_w_in[0], 256),
           adamw("adamw_w_out", w_out[0], got[1], m_w_out[0], v_w_out[0], 128),
           adamw("adamw_w_ffn_up", w_ffn_up[0], got_up, m_w_ffn_up[0], v_w_ffn_up[0], 128),
           adamw("adamw_w_ffn_down", w_ffn_down[0], got_down, m_w_ffn_down[0], v_w_ffn_down[0], 128)]
    small_w = rep_w + sh_w
    as_grad = lambda arrays: [a.reshape(g.shape[1:]) for a, g in zip(arrays, got[2:])]
    small = adamw_small(as_grad(small_w), got[2:], as_grad(rep_m + sh_m), as_grad(rep_v + sh_v))
    small = [[a.reshape(w.shape) for a, w in zip(kind, small_w)] for kind in small]

    names = ["norm_mix_pre", "norm_mix_post", "norm_ffn_pre", "norm_ffn_post", "w_in", "rel_bias", "sinks",
             "rwkv_shift_mix", "w0", "w_decay_up", "a0", "w_iclr_up", "w_gate_up", "k_k", "k_a", "r_k",
             "ln_x_g", "ln_x_b", "w_out", "w_ffn_up", "conv_w", "conv_b", "w_ffn_down"]
    small_names = ["norm_mix_pre", "norm_mix_post", "norm_ffn_pre", "norm_ffn_post", "rel_bias", "sinks",
                   "rwkv_shift_mix", "w0", "a0", "k_k", "k_a", "r_k", "ln_x_g", "ln_x_b", "conv_b",
                   "w_decay_up", "w_iclr_up", "w_gate_up", "conv_w"]
    big_names = {"w_in": 0, "w_out": 1, "w_ffn_up": 2, "w_ffn_down": 3}
    outs = []
    for kind in range(4):
        for nm in names:
            if nm in big_names:
                outs.append(big[big_names[nm]][kind][None])
            else:
                outs.append(small[kind][small_names.index(nm)])
    return (loss, grad_x, *outs)
```

```python
import functools
import math

import jax
import jax.numpy as jnp
from jax import lax
from jax.experimental import pallas as pl
from jax.experimental.pallas import tpu as pltpu

F32 = jnp.float32
BF16 = jnp.bfloat16

N_DEV = 8
SEQ = 2048
D_MODEL = 1024
HEAD_DIM = 64
D_ATTN = 512
D_KV = 128
D_RWKV = 512
N_HEADS = 8
RWKV_COLS = 1792
D_QKV = D_ATTN + 2 * D_KV
D_IN = D_QKV + RWKV_COLS
D_FF = 4096
BLOCK = 128
N_BLOCKS = SEQ // BLOCK
N_BUCKETS = 32
MAX_DISTANCE = 128
NORM_EPS = 1e-6
GN_EPS = 64e-5
NEG_INF = -1e30
CHUNK = 64
N_CHUNKS = SEQ // CHUNK
SCAN_GROUPS = 4
SCAN_WIDTH = D_RWKV // SCAN_GROUPS
TOK_TILE = 256
FF_TILE = 256
FF_TILE_BWD = 128
FF_ROW_CHUNK = 256
FF_HALO = 8
COL_TILE = 256
LANES = 128
VMEM_LIMIT = 56 * 1024 * 1024

ADAM_LR = 0.001
ADAM_B1 = 0.9
ADAM_B2 = 0.999
ADAM_EPS = 1e-08
ADAM_WD = 0.01
ADAM_STEP = 10

NT = ((1,), (1,))
TN = ((0,), (0,))
NN = ((1,), (0,))


def _sds(shape, dtype=F32):
    return jax.ShapeDtypeStruct(shape, dtype)


def _params(sem=None):
    if sem is None:
        return pltpu.CompilerParams(vmem_limit_bytes=VMEM_LIMIT)
    return pltpu.CompilerParams(dimension_semantics=sem, vmem_limit_bytes=VMEM_LIMIT)


def _dot(a, b, dims):
    return lax.dot_general(a, b, (dims, ((), ())), preferred_element_type=F32)


def _split2(x):
    hi = x.astype(BF16)
    return hi, (x - hi.astype(F32)).astype(BF16)


def _dot3_raw(a, b, dims):
    ah, al = _split2(a)
    bh, bl = _split2(b)
    return _dot(ah, bh, dims) + (_dot(al, bh, dims) + _dot(ah, bl, dims))


@functools.partial(jax.custom_vjp, nondiff_argnums=(2,))
def dot3(a, b, dims):
    return _dot3_raw(a, b, dims)


def _dot3_fwd(a, b, dims):
    return _dot3_raw(a, b, dims), (a, b)


def _dot3_bwd(dims, res, g):
    a, b = res
    if dims == NN:
        return dot3(g, b, NT), dot3(a, g, TN)
    if dims == NT:
        return dot3(g, b, NN), dot3(g, a, TN)
    return dot3(b, g, NT), dot3(a, g, NN)


dot3.defvjp(_dot3_fwd, _dot3_bwd)


@jax.custom_vjp
def mm(a, b):
    return _dot(a.astype(BF16), b.astype(BF16), NN)


def _mm_fwd(a, b):
    return mm(a, b), (a, b)


def _mm_bwd(res, g):
    a, b = res
    gb = g.astype(BF16)
    return _dot(gb, b.astype(BF16), NT).astype(a.dtype), _dot(a.astype(BF16), gb, TN).astype(b.dtype)


mm.defvjp(_mm_fwd, _mm_bwd)


@jax.custom_vjp
def mm_nt(a, b):
    return _dot(a.astype(BF16), b.astype(BF16), NT)


def _mm_nt_fwd(a, b):
    return mm_nt(a, b), (a, b)


def _mm_nt_bwd(res, g):
    a, b = res
    gb = g.astype(BF16)
    return _dot(gb, b.astype(BF16), NN).astype(a.dtype), _dot(gb, a.astype(BF16), TN).astype(b.dtype)


mm_nt.defvjp(_mm_nt_fwd, _mm_nt_bwd)


@jax.custom_vjp
def mmw(a, w, wz):
    return _dot(a.astype(BF16), w, NN)


def _mmw_fwd(a, w, wz):
    return mmw(a, w, wz), (a, w)


def _mmw_bwd(res, g):
    a, w = res
    gb = g.astype(BF16)
    return _dot(gb, w, NT).astype(a.dtype), jnp.zeros_like(w), _dot(a.astype(BF16), gb, TN)


mmw.defvjp(_mmw_fwd, _mmw_bwd)


def _shift_raw(x, n):
    rows = x.shape[0]
    rolled = pltpu.roll(x, n % rows, 0)
    idx = lax.broadcasted_iota(jnp.int32, x.shape, 0)
    keep = idx >= n if n > 0 else idx < rows + n
    return jnp.where(keep, rolled, 0.0)


@functools.partial(jax.custom_vjp, nondiff_argnums=(1,))
def shift_rows(x, n):
    return _shift_raw(x, n)


def _shift_fwd(x, n):
    return _shift_raw(x, n), None


def _shift_bwd(n, _, g):
    return (_shift_raw(g, -n),)


shift_rows.defvjp(_shift_fwd, _shift_bwd)


def _head_matrix(scale):
    a = lax.broadcasted_iota(jnp.int32, (D_RWKV, D_RWKV), 0) // HEAD_DIM
    b = lax.broadcasted_iota(jnp.int32, (D_RWKV, D_RWKV), 1) // HEAD_DIM
    return jnp.where(a == b, scale, 0.0).astype(F32)


def _rms(x, g):
    return x * lax.rsqrt(jnp.mean(x * x, axis=-1, keepdims=True) + NORM_EPS) * g


def _softplus(x):
    return jnp.maximum(x, 0.0) + jnp.log(1.0 + jnp.exp(-jnp.abs(x)))


def _tile_spec(arr, tm):
    return pl.BlockSpec((tm, arr.shape[1]), lambda i: (i, 0))


def _full_spec(arr):
    nd = arr.ndim
    return pl.BlockSpec(arr.shape, lambda i: (0,) * nd)


def tok_fwd(name, fn, tiles, params, zero_shapes, out_widths, out_dtypes, tm=TOK_TILE):
    n_t, n_p = len(tiles), len(params)

    def body(*refs):
        t_vals = [r[...] for r in refs[:n_t]]
        p_vals = [r[...] for r in refs[n_t:n_t + n_p]]
        z_vals = [jnp.zeros(s, F32) for s in zero_shapes]
        outs = fn(*t_vals, *p_vals, *z_vals)
        for r, o in zip(refs[n_t + n_p:], outs):
            r[...] = o.astype(r.dtype)

    rows = tiles[0].shape[0]
    return pl.pallas_call(
        body, grid=(rows // tm,), name=name,
        in_specs=[_tile_spec(t, tm) for t in tiles] + [_full_spec(p) for p in params],
        out_specs=[pl.BlockSpec((tm, w), lambda i: (i, 0)) for w in out_widths],
        out_shape=[_sds((rows, w), dt) for w, dt in zip(out_widths, out_dtypes)],
        compiler_params=_params(("arbitrary",)),
    )(*tiles, *params)


def tok_bwd(name, fn, tiles, params, zero_shapes, cots, diff_params, residuals=(), tm=TOK_TILE):
    cot_parts = [c if isinstance(c, tuple) else (c,) for c in cots]
    flat_cots = [a for part in cot_parts for a in part]
    residuals = dict(residuals)
    extra = [residuals[i] for i in sorted(residuals)]
    n_t, n_p, n_c, n_r = len(tiles), len(params), len(flat_cots), len(extra)
    acc_shapes = [params[i].shape for i in diff_params] + list(zero_shapes)

    def body(*refs):
        t_vals = [r[...].astype(F32) for r in refs[:n_t]]
        p_vals = [r[...] for r in refs[n_t:n_t + n_p]]
        flat = iter(r[...] for r in refs[n_t + n_p:n_t + n_p + n_c])
        c_vals = [functools.reduce(jnp.add, [next(flat) for _ in part]) for part in cot_parts]
        r_vals = dict(zip(sorted(residuals), (r[...] for r in refs[n_t + n_p + n_c:n_t + n_p + n_c + n_r])))
        out_refs = refs[n_t + n_p + n_c + n_r:]
        z_vals = [jnp.zeros(s, F32) for s in zero_shapes]
        d_vals = [p_vals[i] for i in diff_params]

        def f(t_in, d_in, z_in):
            full = list(p_vals)
            for i, v in zip(diff_params, d_in):
                full[i] = v
            return tuple(fn(*t_in, *full, *z_in))

        _, vjp = jax.vjp(f, t_vals, d_vals, z_vals)
        g_t, g_d, g_z = vjp(tuple(c_vals))
        for i, (r, g) in enumerate(zip(out_refs[:n_t], g_t)):
            r[...] = (g + r_vals[i] if i in r_vals else g).astype(r.dtype)
        acc_refs = out_refs[n_t:]

        @pl.when(pl.program_id(0) == 0)
        def _():
            for r in acc_refs:
                r[...] = jnp.zeros_like(r)

        for r, g in zip(acc_refs, list(g_d) + list(g_z)):
            r[...] += g

    rows = tiles[0].shape[0]
    return pl.pallas_call(
        body, grid=(rows // tm,), name=name,
        in_specs=[_tile_spec(t, tm) for t in tiles] + [_full_spec(p) for p in params]
        + [_tile_spec(c, tm) for c in flat_cots + extra],
        out_specs=[_tile_spec(t, tm) for t in tiles]
        + [pl.BlockSpec(s, lambda i, nd=len(s): (0,) * nd) for s in acc_shapes],
        out_shape=[_sds(t.shape) for t in tiles] + [_sds(s) for s in acc_shapes],
        compiler_params=_params(("arbitrary",)),
    )(*tiles, *params, *flat_cots, *extra)


def rms_tile(x, g):
    return (_rms(x, g),)


def rwkv_pre_tile(ps, w0, wd_pad, a0, wi_pad, wg, k_k, k_a):
    r = ps[:, 0:D_RWKV]
    k = ps[:, D_RWKV:2 * D_RWKV]
    v = ps[:, 2 * D_RWKV:3 * D_RWKV]
    z2 = ps[:, 3 * D_RWKV:3 * D_RWKV + LANES]
    zg = ps[:, 3 * D_RWKV + LANES:RWKV_COLS]
    w_log = -_softplus(-(w0 + mm(jnp.tanh(z2), wd_pad))) - 0.5
    lw = -jnp.exp(w_log)
    a = jax.nn.sigmoid(a0 + mm(z2, wi_pad))
    g = mm(jax.nn.sigmoid(zg), wg)
    kk = k * k_k
    norm = jnp.sqrt(dot3(kk * kk, _head_matrix(1.0), NN))
    kk = kk / jnp.maximum(norm, 1e-12)
    k2 = k * (1.0 + (a - 1.0) * k_a)
    return r, lw, k2, v, kk, a, g


def mix_out_tile(o, r, k2, v, g, attn, x, w_out, n_post, ln_g, ln_b, r_k, wz):
    hmean = _head_matrix(1.0 / HEAD_DIM)
    d = o - dot3(o, hmean, NN)
    var = dot3(d * d, hmean, NN)
    on = d * lax.rsqrt(var + GN_EPS) * ln_g + ln_b
    bonus = dot3(r * k2 * r_k, _head_matrix(1.0), NN) * v
    rw = (on + bonus) * g
    mix = mmw(jnp.concatenate([attn, rw], axis=1), w_out, wz)
    return (x + _rms(mix, n_post),)


def in_proj_fwd(h, w_in, mix_ext):
    def body(h_ref, w_ref, m_ref, proj_ref, ps_ref):
        p = _dot(h_ref[...], w_ref[...], NN)
        proj_ref[...] = p
        ps_ref[...] = p + (_shift_raw(p, 1) - p) * m_ref[...]

    n = D_IN // COL_TILE
    first = D_QKV // COL_TILE
    return pl.pallas_call(
        body, grid=(n,), name="in_proj_fwd",
        in_specs=[pl.BlockSpec((SEQ, D_MODEL), lambda j: (0, 0)),
                  pl.BlockSpec((D_MODEL, COL_TILE), lambda j: (0, j)),
                  pl.BlockSpec((1, COL_TILE), lambda j: (0, j))],
        out_specs=[pl.BlockSpec((SEQ, COL_TILE), lambda j: (0, j)),
                   pl.BlockSpec((SEQ, COL_TILE), lambda j: (0, jnp.maximum(j - first, 0)))],
        out_shape=[_sds((SEQ, D_IN)), _sds((SEQ, RWKV_COLS))],
        compiler_params=_params(("arbitrary",)),
    )(h, w_in, mix_ext)


def in_proj_bwd(h, w_in, mix_ext, proj, dpa):
    def body(h_ref, w_ref, m_ref, p_ref, d_ref, dh_ref, dw_ref, dm_ref):
        d = d_ref[...]
        p = p_ref[...]
        dm_ref[...] = jnp.sum(d * (_shift_raw(p, 1) - p), axis=0, keepdims=True)
        dmix = d * m_ref[...]
        dp = (d - dmix + _shift_raw(dmix, -1)).astype(BF16)
        dw_ref[...] = _dot(h_ref[...], dp, TN)

        @pl.when(pl.program_id(0) == 0)
        def _():
            dh_ref[...] = jnp.zeros_like(dh_ref)

        dh_ref[...] += _dot(dp, w_ref[...], NT)

    n = D_IN // COL_TILE
    col = lambda rows: pl.BlockSpec((rows, COL_TILE), lambda j: (0, j))
    return pl.pallas_call(
        body, grid=(n,), name="in_proj_bwd",
        in_specs=[pl.BlockSpec((SEQ, D_MODEL), lambda j: (0, 0)), col(D_MODEL), col(1), col(SEQ), col(SEQ)],
        out_specs=[pl.BlockSpec((SEQ, D_MODEL), lambda j: (0, 0)), col(D_MODEL), col(1)],
        out_shape=[_sds((SEQ, D_MODEL)), _sds((D_MODEL, D_IN)), _sds((1, D_IN))],
        compiler_params=_params(("arbitrary",)),
    )(h, w_in, mix_ext, proj, dpa)


def _bucket_table():
    rel = (jnp.arange(BLOCK)[:, None] + BLOCK) - jnp.arange(2 * BLOCK)[None, :]
    n = jnp.maximum(rel, 0)
    max_exact = N_BUCKETS // 2
    large = max_exact + (jnp.log(jnp.maximum(n, 1).astype(F32) / max_exact)
                         / math.log(MAX_DISTANCE / max_exact) * (N_BUCKETS - max_exact)).astype(jnp.int32)
    large = jnp.minimum(large, N_BUCKETS - 1)
    return jnp.where(n < max_exact, n, large).astype(jnp.int32)


def _select_matrix(g, o):
    a = lax.broadcasted_iota(jnp.int32, (D_KV, D_KV), 0)
    b = lax.broadcasted_iota(jnp.int32, (D_KV, D_KV), 1)
    return ((a - HEAD_DIM * g == b - o) & (b >= o) & (b < o + HEAD_DIM)).astype(F32)


def _attn_block(q, kp, kc, vp, vc, bias, sinks, block_idx):
    kb = jnp.concatenate([kp, kc], axis=0)
    vb = jnp.concatenate([vp, vc], axis=0)
    row = lax.broadcasted_iota(jnp.int32, (BLOCK, 2 * BLOCK), 0)
    col = lax.broadcasted_iota(jnp.int32, (BLOCK, 2 * BLOCK), 1)
    rel = row + BLOCK - col
    mask = (rel >= 0) & (rel < BLOCK) & (col + (block_idx - 1) * BLOCK >= 0)
    lane8 = lax.broadcasted_iota(jnp.int32, (1, N_HEADS), 1)
    kt, vt = {}, {}
    for g in range(2):
        for o in (0, HEAD_DIM):
            sel = _select_matrix(g, o)
            kt[g, o] = mm(kb, sel)
            vt[g, o] = mm(vb, sel)
    outs = []
    for j in range(D_ATTN // LANES):
        qs = q[:, j * LANES:(j + 1) * LANES]
        acc = None
        for half in range(2):
            hq = 2 * j + half
            g, o = hq // 4, half * HEAD_DIM
            s = mm_nt(qs, kt[g, o]) * (HEAD_DIM ** -0.5) + bias[hq]
            s = jnp.where(mask, s, NEG_INF)
            sink = jnp.sum(jnp.where(lane8 == hq, sinks, 0.0), axis=1, keepdims=True)
            m = lax.stop_gradient(jnp.maximum(jnp.max(s, axis=-1, keepdims=True), sink))
            p = jnp.exp(s - m)
            probs = p / (jnp.sum(p, axis=-1, keepdims=True) + jnp.exp(sink - m))
            part = mm(probs, vt[g, o])
            acc = part if acc is None else acc + part
        outs.append(acc)
    return jnp.concatenate(outs, axis=1)


def _build_bias(rb_ref, bucket, bias_ref):
    for hq in range(N_HEADS):
        acc = jnp.zeros((BLOCK, 2 * BLOCK), F32)
        for b in range(N_BUCKETS):
            acc = jnp.where(bucket == b, rb_ref[b, hq], acc)
        bias_ref[hq] = acc


def _attn_in_specs():
    prev = lambda n: jnp.maximum(n - 1, 0)
    return [pl.BlockSpec((BLOCK, D_ATTN), lambda n: (n, 0)),
            pl.BlockSpec((BLOCK, D_KV), lambda n: (prev(n), D_ATTN // D_KV)),
            pl.BlockSpec((BLOCK, D_KV), lambda n: (n, D_ATTN // D_KV)),
            pl.BlockSpec((BLOCK, D_KV), lambda n: (prev(n), D_ATTN // D_KV + 1)),
            pl.BlockSpec((BLOCK, D_KV), lambda n: (n, D_ATTN // D_KV + 1)),
            pl.BlockSpec(memory_space=pltpu.SMEM),
            pl.BlockSpec((BLOCK, 2 * BLOCK), lambda n: (0, 0)),
            pl.BlockSpec((1, N_HEADS), lambda n: (0, 0))]


def attn_fwd(proj, rel_bias, bucket, sinks):
    def body(q_ref, kp_ref, kc_ref, vp_ref, vc_ref, rb_ref, bk_ref, sk_ref, o_ref, bias_ref):
        n = pl.program_id(0)

        @pl.when(n == 0)
        def _():
            _build_bias(rb_ref, bk_ref[...], bias_ref)

        o_ref[...] = _attn_block(q_ref[...], kp_ref[...], kc_ref[...], vp_ref[...], vc_ref[...],
                                 tuple(bias_ref[h] for h in range(N_HEADS)), sk_ref[...], n)

    return pl.pallas_call(
        body, grid=(N_BLOCKS,), name="attn_fwd",
        in_specs=_attn_in_specs(),
        out_specs=pl.BlockSpec((BLOCK, D_ATTN), lambda n: (n, 0)),
        out_shape=_sds((SEQ, D_ATTN)),
        scratch_shapes=[pltpu.VMEM((N_HEADS, BLOCK, 2 * BLOCK), F32)],
        compiler_params=_params(("arbitrary",)),
    )(proj, proj, proj, proj, proj, rel_bias, bucket, sinks)


def attn_bwd(proj, rel_bias, bucket, sinks, d_attn):
    def body(q_ref, kp_ref, kc_ref, vp_ref, vc_ref, rb_ref, bk_ref, sk_ref, do_ref,
             dq_ref, dkc_ref, dkp_ref, dvc_ref, dvp_ref, drb_ref, dsk_ref, bias_ref, dbias_ref):
        n = pl.program_id(0)

        @pl.when(n == 0)
        def _():
            _build_bias(rb_ref, bk_ref[...], bias_ref)
            dbias_ref[...] = jnp.zeros_like(dbias_ref)
            dsk_ref[...] = jnp.zeros_like(dsk_ref)

        f = lambda q, kp, kc, vp, vc, bias, sk: _attn_block(q, kp, kc, vp, vc, bias, sk, n)
        _, vjp = jax.vjp(f, q_ref[...], kp_ref[...], kc_ref[...], vp_ref[...], vc_ref[...],
                         tuple(bias_ref[h] for h in range(N_HEADS)), sk_ref[...])
        dq, dkp, dkc, dvp, dvc, dbias, dsk = vjp(do_ref[...])
        dq_ref[...] = dq
        dkc_ref[...] = dkc
        dkp_ref[...] = dkp
        dvc_ref[...] = dvc
        dvp_ref[...] = dvp
        for h in range(N_HEADS):
            dbias_ref[h] += dbias[h]
        dsk_ref[...] += dsk

        @pl.when(n == N_BLOCKS - 1)
        def _():
            bucket_v = bk_ref[...]
            rowi = lax.broadcasted_iota(jnp.int32, (N_BUCKETS, 2 * BLOCK), 0)
            lane = lax.broadcasted_iota(jnp.int32, (N_BUCKETS, N_HEADS), 1)
            out = jnp.zeros((N_BUCKETS, N_HEADS), F32)
            for hq in range(N_HEADS):
                dbh = dbias_ref[hq]
                rows = jnp.zeros((N_BUCKETS, 2 * BLOCK), F32)
                for b in range(N_BUCKETS):
                    part = jnp.sum(jnp.where(bucket_v == b, dbh, 0.0), axis=0, keepdims=True)
                    rows = jnp.where(rowi == b, part, rows)
                tot = jnp.sum(rows, axis=1, keepdims=True)
                out = jnp.where(lane == hq, tot, out)
            drb_ref[...] = out

    blk = lambda w: pl.BlockSpec((BLOCK, w), lambda n: (n, 0))
    return pl.pallas_call(
        body, grid=(N_BLOCKS,), name="attn_bwd",
        in_specs=_attn_in_specs() + [blk(D_ATTN)],
        out_specs=[blk(D_ATTN), blk(D_KV), blk(D_KV), blk(D_KV), blk(D_KV),
                   pl.BlockSpec((N_BUCKETS, N_HEADS), lambda n: (0, 0)),
                   pl.BlockSpec((1, N_HEADS), lambda n: (0, 0))],
        out_shape=[_sds((SEQ, D_ATTN)), _sds((SEQ, D_KV)), _sds((SEQ, D_KV)), _sds((SEQ, D_KV)),
                   _sds((SEQ, D_KV)), _sds((N_BUCKETS, N_HEADS)), _sds((1, N_HEADS))],
        scratch_shapes=[pltpu.VMEM((N_HEADS, BLOCK, 2 * BLOCK), F32),
                        pltpu.VMEM((N_HEADS, BLOCK, 2 * BLOCK), F32)],
        compiler_params=_params(("arbitrary",)),
    )(proj, proj, proj, proj, proj, rel_bias, bucket, sinks, d_attn)


def _stack(x):
    groups = x.shape[1] // HEAD_DIM
    lane = lax.broadcasted_iota(jnp.int32, x.shape, 1) // HEAD_DIM
    return jnp.concatenate([jnp.where(lane == i, x, 0.0) for i in range(groups)], axis=0)


def _neumann(l):
    c = CHUNK
    t = lax.broadcasted_iota(jnp.int32, l.shape, 0)
    i = lax.broadcasted_iota(jnp.int32, l.shape, 1) % c
    inv = (i == t).astype(F32) + l
    pw = dot3(l, _stack(l), NN)
    for _ in range(4):
        both = dot3(jnp.concatenate([inv, pw], axis=0), _stack(pw), NN)
        inv = inv + both[:c]
        pw = both[c:]
    return inv + dot3(inv, _stack(pw), NN)


@jax.custom_vjp
def neumann_inv(l):
    return _neumann(l)


def _neumann_fwd(l):
    inv = _neumann(l)
    return inv, inv


def _neumann_bwd(inv, g):
    c = CHUNK
    bd_t = _stack(inv).T
    inv_t = bd_t[0:c]
    for h in range(1, inv.shape[1] // c):
        inv_t = inv_t + bd_t[h * c:(h + 1) * c]
    return (dot3(dot3(inv_t, _stack(g), NN), bd_t, NN),)


neumann_inv.defvjp(_neumann_fwd, _neumann_bwd)


def _cumsum_raw(x, dims):
    c = x.shape[0]
    tt = lax.broadcasted_iota(jnp.int32, (c, c), 0)
    ii = lax.broadcasted_iota(jnp.int32, (c, c), 1)
    tri = (ii <= tt).astype(BF16)
    hi = x.astype(BF16)
    rest = x - hi.astype(F32)
    mid = rest.astype(BF16)
    lo = (rest - mid.astype(F32)).astype(BF16)
    return _dot(tri, hi, dims) + (_dot(tri, mid, dims) + _dot(tri, lo, dims))


@jax.custom_vjp
def cumsum_rows(x):
    return _cumsum_raw(x, NN)


def _cumsum_fwd(x):
    return _cumsum_raw(x, NN), None


def _cumsum_bwd(_, g):
    return (_cumsum_raw(g, TN),)


cumsum_rows.defvjp(_cumsum_fwd, _cumsum_bwd)


def _rwkv_chunk(s0, r, lw, k, v, kk, a):
    heads = r.shape[1] // HEAD_DIM
    c, hc = CHUNK, heads * CHUNK
    t = lax.broadcasted_iota(jnp.int32, (c, hc), 0)
    i = lax.broadcasted_iota(jnp.int32, (c, hc), 1) % c
    strict, incl = i < t, i <= t
    stack = _stack
    ba = lax.broadcasted_iota(jnp.int32, (hc, hc), 0) // c
    bb = lax.broadcasted_iota(jnp.int32, (hc, hc), 1) // c
    blocks = (ba == bb).astype(F32)

    cum = cumsum_rows(lw)
    cum_end = jnp.sum(lw, axis=0, keepdims=True)
    beta = kk * a
    al = -kk * jnp.exp(cum - lw)
    p_inv = jnp.exp(-cum)
    be, kb, rb = beta * p_inv, k * p_inv, r * jnp.exp(cum)
    ar = jnp.concatenate([al, rb], axis=0)
    sv = stack(v)
    l_all = dot3(ar, jnp.concatenate([stack(be), stack(kb)], axis=0), NT)
    l_ab = jnp.where(strict, l_all[:c, :hc], 0.0)
    l_ak = jnp.where(strict, l_all[:c, hc:], 0.0)
    l_rb = jnp.where(incl, l_all[c:, :hc], 0.0)
    l_rk = jnp.where(incl, l_all[c:, hc:], 0.0)
    inv = neumann_inv(l_ab)
    from_s0 = dot3(ar, s0, NT)
    from_v = dot3(jnp.concatenate([l_ak, l_rk], axis=0), sv, NN)
    u = dot3(inv, stack(from_s0[:c] + from_v[:c]), NN)
    o = from_s0[c:] + from_v[c:] + dot3(l_rb, stack(u), NN)
    to_end = jnp.exp(cum_end - cum)
    s1 = s0 * jnp.exp(cum_end) + blocks * dot3(
        jnp.concatenate([u, v], axis=0), jnp.concatenate([beta * to_end, k * to_end], axis=0), TN)
    return o, s1


def call_with_comm(plan, middle_step, body, grid, name, in_specs, out_specs, out_shape, scratch_shapes, operands):
    n_in, n_out, n_scr = len(in_specs), len(out_specs), len(scratch_shapes)
    p_in, p_out = len(plan.ins), len(plan.out_shape)

    def fused(*refs):
        refs = list(refs)
        ins, refs = refs[:n_in], refs[n_in:]
        p_ins, refs = refs[:p_in], refs[p_in:]
        outs, refs = refs[:n_out], refs[n_out:]
        p_outs, refs = refs[:p_out], refs[p_out:]
        scr, p_sems = refs[:n_scr], refs[n_scr:]
        start, middle, finish = plan.stages(p_ins, p_outs, p_sems)
        step = pl.program_id(0)
        pl.when(step == 0)(start)
        body(*ins, *outs, *scr)
        pl.when(step == middle_step)(middle)
        pl.when(step == grid[0] - 1)(finish)

    any_spec = pl.BlockSpec(memory_space=pl.ANY)
    res = pl.pallas_call(
        fused, grid=grid, name=name,
        in_specs=list(in_specs) + [any_spec] * p_in, out_specs=list(out_specs) + [any_spec] * p_out,
        out_shape=list(out_shape) + list(plan.out_shape), scratch_shapes=list(scratch_shapes) + list(plan.scratch),
        compiler_params=_params(("arbitrary",)),
    )(*operands, *plan.ins)
    return res[:n_out], res[n_out:]


def _by_group(ref):
    return jnp.stack([ref[:, g * SCAN_WIDTH:(g + 1) * SCAN_WIDTH] for g in range(SCAN_GROUPS)])


def _store_groups(ref, val):
    for g in range(SCAN_GROUPS):
        ref[:, g * SCAN_WIDTH:(g + 1) * SCAN_WIDTH] = val[g]


def rwkv_scan_fwd(r, lw, k, v, kk, a, plan):
    def body(r_ref, lw_ref, k_ref, v_ref, kk_ref, a_ref, o_ref, st_ref, s_ref):
        @pl.when(pl.program_id(0) == 0)
        def _():
            s_ref[...] = jnp.zeros_like(s_ref)

        s0 = s_ref[...]
        st_ref[0] = s0
        o, s1 = jax.vmap(_rwkv_chunk)(s0, *(_by_group(ref) for ref in (r_ref, lw_ref, k_ref, v_ref, kk_ref, a_ref)))
        _store_groups(o_ref, o)
        s_ref[...] = s1

    tb = pl.BlockSpec((CHUNK, D_RWKV), lambda c: (c, 0))
    state = (SCAN_GROUPS, SCAN_WIDTH, SCAN_WIDTH)
    return call_with_comm(
        plan, 3 * N_CHUNKS // 4, body, (N_CHUNKS,), "rwkv_scan_fwd",
        [tb] * 6, [tb, pl.BlockSpec((1,) + state, lambda c: (c, 0, 0, 0))],
        [_sds((SEQ, D_RWKV)), _sds((N_CHUNKS,) + state)], [pltpu.VMEM(state, F32)], (r, lw, k, v, kk, a))


def rwkv_scan_bwd(r, lw, k, v, kk, a, states, d_o, plan):
    def body(r_ref, lw_ref, k_ref, v_ref, kk_ref, a_ref, st_ref, do_ref,
             dr_ref, dlw_ref, dk_ref, dv_ref, dkk_ref, da_ref, ds_ref):
        @pl.when(pl.program_id(0) == 0)
        def _():
            ds_ref[...] = jnp.zeros_like(ds_ref)

        _, vjp = jax.vjp(jax.vmap(_rwkv_chunk), st_ref[0],
                         *(_by_group(ref) for ref in (r_ref, lw_ref, k_ref, v_ref, kk_ref, a_ref)))
        grads = vjp((_by_group(do_ref), ds_ref[...]))
        ds_ref[...] = grads[0]
        for ref, val in zip((dr_ref, dlw_ref, dk_ref, dv_ref, dkk_ref, da_ref), grads[1:]):
            _store_groups(ref, val)

    last = N_CHUNKS - 1
    tb = pl.BlockSpec((CHUNK, D_RWKV), lambda c: (last - c, 0))
    state = (SCAN_GROUPS, SCAN_WIDTH, SCAN_WIDTH)
    return call_with_comm(
        plan, N_CHUNKS // 4, body, (N_CHUNKS,), "rwkv_scan_bwd",
        [tb] * 6 + [pl.BlockSpec((1,) + state, lambda c: (last - c, 0, 0, 0)), tb], [tb] * 6,
        [_sds((SEQ, D_RWKV))] * 6, [pltpu.VMEM(state, F32)], (r, lw, k, v, kk, a, states, d_o))


def _ffn_mid(ug, uv, cg, cv, bg, bv):
    conv_g = bg + cg[0] * shift_rows(ug, 2) + cg[1] * shift_rows(ug, 1) + cg[2] * ug
    conv_v = bv + cv[0] * shift_rows(uv, 2) + cv[1] * shift_rows(uv, 1) + cv[2] * uv
    return jax.nn.gelu(conv_g, approximate=True) * conv_v


def _conv_rows(ref):
    return tuple(ref[0, j:j + 1, :] for j in range(3))


def _ffn_specs(tile):
    per = D_MODEL // tile
    half = N_DEV // 2
    w_g = pl.BlockSpec((1, D_MODEL, tile), lambda t: (t // per, 0, t % per))
    w_v = pl.BlockSpec((1, D_MODEL, tile), lambda t: (half + t // per, 0, t % per))
    c_g = pl.BlockSpec((1, 3, tile), lambda t: (t // per, 0, t % per))
    c_v = pl.BlockSpec((1, 3, tile), lambda t: (half + t // per, 0, t % per))
    b_g = pl.BlockSpec((1, tile), lambda t: (0, t))
    b_v = pl.BlockSpec((1, tile), lambda t: (0, D_FF // tile + t))
    w_d = pl.BlockSpec((tile, D_MODEL), lambda t: (t, 0))
    return w_g, w_v, c_g, c_v, b_g, b_v, w_d


def ffn_fwd(h2, w_up, conv_w, conv_b, w_down):
    def body(h_ref, wg_ref, wv_ref, cg_ref, cv_ref, bg_ref, bv_ref, wd_ref, f_ref):
        @pl.when(pl.program_id(0) == 0)
        def _():
            f_ref[...] = jnp.zeros_like(f_ref)

        h = h_ref[...]
        act = _ffn_mid(_dot(h, wg_ref[0], NN), _dot(h, wv_ref[0], NN), _conv_rows(cg_ref), _conv_rows(cv_ref),
                       bg_ref[...], bv_ref[...])
        f_ref[...] += _dot(act.astype(BF16), wd_ref[...], NN)

    full = pl.BlockSpec((SEQ, D_MODEL), lambda t: (0, 0))
    return pl.pallas_call(
        body, grid=(D_FF // FF_TILE,), name="ffn_fwd",
        in_specs=[full, *_ffn_specs(FF_TILE)],
        out_specs=full, out_shape=_sds((SEQ, D_MODEL)),
        compiler_params=_params(("arbitrary",)),
    )(h2, w_up, w_up, conv_w, conv_w, conv_b, conv_b, w_down)


def ffn_bwd(h2, w_up, conv_w, conv_b, w_down, df):
    tile = FF_TILE_BWD
    per = D_MODEL // tile

    def body(h_hbm, wg_ref, wv_ref, cg_ref, cv_ref, bg_ref, bv_ref, wd_ref, df_hbm,
             dh_hbm, dup_hbm, dcg_ref, dcv_ref, dbg_ref, dbv_ref, dwd_ref,
             h_ref, df_ref, dh_ref, dwg_ref, dwv_ref, sem, up_sems):
        t = pl.program_id(0)

        @pl.when(t == 0)
        def _():
            pltpu.sync_copy(h_hbm, h_ref)
            pltpu.sync_copy(df_hbm, df_ref)

        h, df_b, wg, wv = h_ref[...], df_ref[...], wg_ref[0], wv_ref[0]
        act, vjp = jax.vjp(_ffn_mid, _dot(h, wg, NN), _dot(h, wv, NN), _conv_rows(cg_ref), _conv_rows(cv_ref),
                           bg_ref[...], bv_ref[...])
        dwd_ref[...] = _dot(act.astype(BF16), df_b, TN)
        dug, duv, dcg, dcv, dbg, dbv = vjp(_dot(df_b, wd_ref[...], NT))
        dug, duv = dug.astype(BF16), duv.astype(BF16)
        cols = pl.ds(pl.multiple_of((t % per) * tile, tile), tile)
        to_gate = pltpu.make_async_copy(dwg_ref, dup_hbm.at[t // per, :, cols], up_sems.at[0])
        to_value = pltpu.make_async_copy(dwv_ref, dup_hbm.at[N_DEV // 2 + t // per, :, cols], up_sems.at[1])
        dwg_ref[...] = _dot(h, dug, TN)
        to_gate.start()
        dwv_ref[...] = _dot(h, duv, TN)
        to_value.start()

        @pl.when(pl.program_id(0) == 0)
        def _():
            dh_ref[...] = jnp.zeros_like(dh_ref)

        dh_ref[...] += _dot(dug, wg, NT) + _dot(duv, wv, NT)
        for j in range(3):
            dcg_ref[0, j:j + 1, :] = dcg[j]
            dcv_ref[0, j:j + 1, :] = dcv[j]
        dbg_ref[...] = dbg
        dbv_ref[...] = dbv
        to_gate.wait()
        to_value.wait()

        @pl.when(t == D_FF // tile - 1)
        def _():
            cp = pltpu.make_async_copy(dh_ref, dh_hbm, sem)
            cp.start()
            cp.wait()

    hbm = pl.BlockSpec(memory_space=pl.ANY)
    w_g, w_v, c_g, c_v, b_g, b_v, w_d = _ffn_specs(tile)
    return pl.pallas_call(
        body, grid=(D_FF // tile,), name="ffn_bwd",
        in_specs=[hbm, w_g, w_v, c_g, c_v, b_g, b_v, w_d, hbm],
        out_specs=[hbm, hbm, c_g, c_v, b_g, b_v, w_d],
        out_shape=[_sds((SEQ, D_MODEL)), _sds((N_DEV, D_MODEL, D_MODEL)),
                   _sds((N_DEV, 3, D_MODEL)), _sds((N_DEV, 3, D_MODEL)), _sds((1, 2 * D_FF)), _sds((1, 2 * D_FF)),
                   _sds((D_FF, D_MODEL))],
        scratch_shapes=[pltpu.VMEM((SEQ, D_MODEL), BF16), pltpu.VMEM((SEQ, D_MODEL), BF16),
                        pltpu.VMEM((SEQ, D_MODEL), F32), pltpu.VMEM((D_MODEL, tile), F32),
                        pltpu.VMEM((D_MODEL, tile), F32), pltpu.SemaphoreType.DMA, pltpu.SemaphoreType.DMA((2,))],
        compiler_params=_params(("arbitrary",)),
    )(h2, w_up, w_up, conv_w, conv_w, conv_b, conv_b, w_down, df)


def ffn_bwd_mid(h2, w_up, conv_w, conv_b, w_down, df):
    tile, rows, halo = FF_TILE, FF_ROW_CHUNK, FF_HALO
    ext = rows + 2 * halo

    def body(h_hbm, wg_ref, wv_ref, cg_ref, cv_ref, bg_ref, bv_ref, wd_ref, df_hbm,
             dug_ref, duv_ref, dcg_ref, dcv_ref, dbg_ref, dbv_ref, dwd_ref,
             h_ref, df_ref, ug_ref, uv_ref, da_ref, act_ref):
        @pl.when(pl.program_id(0) == 0)
        def _():
            pltpu.sync_copy(h_hbm, h_ref)
            pltpu.sync_copy(df_hbm, df_ref)
            for ref in (ug_ref, uv_ref, da_ref):
                ref[0:halo, :] = jnp.zeros((halo, tile), F32)
                ref[halo + SEQ:, :] = jnp.zeros((halo, tile), F32)

        h, df_b = h_ref[...], df_ref[...]
        ug_ref[halo:halo + SEQ, :] = _dot(h, wg_ref[0], NN)
        uv_ref[halo:halo + SEQ, :] = _dot(h, wv_ref[0], NN)
        da_ref[halo:halo + SEQ, :] = _dot(df_b, wd_ref[...], NT)
        cg, cv, bg, bv = _conv_rows(cg_ref), _conv_rows(cv_ref), bg_ref[...], bv_ref[...]
        down = lambda x, n: pltpu.roll(x, n, 0)
        up = lambda x, n: pltpu.roll(x, ext - n, 0)
        mid = slice(halo, halo + rows)

        def chunk(i, sums):
            r0 = pl.multiple_of(i * rows, rows)
            window = pl.ds(r0, ext)
            ug, uv, da = ug_ref[window, :], uv_ref[window, :], da_ref[window, :]
            ug1, ug2, uv1, uv2 = down(ug, 1), down(ug, 2), down(uv, 1), down(uv, 2)
            conv_g = bg + cg[0] * ug2 + cg[1] * ug1 + cg[2] * ug
            conv_v = bv + cv[0] * uv2 + cv[1] * uv1 + cv[2] * uv
            act, vjp = jax.vjp(lambda a, b: jax.nn.gelu(a, approximate=True) * b, conv_g, conv_v)
            dcg, dcv = vjp(da)
            dug = cg[2] * dcg + cg[1] * up(dcg, 1) + cg[0] * up(dcg, 2)
            duv = cv[2] * dcv + cv[1] * up(dcv, 1) + cv[0] * up(dcv, 2)
            out = pl.ds(r0, rows)
            act_ref[out, :] = act[mid].astype(BF16)
            dug_ref[out, :] = dug[mid].astype(BF16)
            duv_ref[out, :] = duv[mid].astype(BF16)
            col = lambda x: jnp.sum(x[mid], axis=0, keepdims=True)
            new = (col(dcg * ug2), col(dcg * ug1), col(dcg * ug), col(dcv * uv2), col(dcv * uv1), col(dcv * uv),
                   col(dcg), col(dcv))
            return tuple(s + n for s, n in zip(sums, new))

        zero = jnp.zeros((1, tile), F32)
        sums = lax.fori_loop(0, SEQ // rows, chunk, (zero,) * 8)
        for j in range(3):
            dcg_ref[0, j:j + 1, :] = sums[j]
            dcv_ref[0, j:j + 1, :] = sums[3 + j]
        dbg_ref[...] = sums[6]
        dbv_ref[...] = sums[7]
        dwd_ref[...] = _dot(act_ref[...], df_b, TN)

    hbm = pl.BlockSpec(memory_space=pl.ANY)
    w_g, w_v, c_g, c_v, b_g, b_v, w_d = _ffn_specs(tile)
    col = pl.BlockSpec((SEQ, tile), lambda t: (0, t))
    padded = pltpu.VMEM((SEQ + 2 * halo, tile), F32)
    return pl.pallas_call(
        body, grid=(D_FF // tile,), name="ffn_bwd_mid",
        in_specs=[hbm, w_g, w_v, c_g, c_v, b_g, b_v, w_d, hbm],
        out_specs=[col, col, c_g, c_v, b_g, b_v, w_d],
        out_shape=[_sds((SEQ, D_FF), BF16), _sds((SEQ, D_FF), BF16), _sds((N_DEV, 3, D_MODEL)),
                   _sds((N_DEV, 3, D_MODEL)), _sds((1, 2 * D_FF)), _sds((1, 2 * D_FF)), _sds((D_FF, D_MODEL))],
        scratch_shapes=[pltpu.VMEM((SEQ, D_MODEL), BF16), pltpu.VMEM((SEQ, D_MODEL), BF16), padded, padded, padded,
                        pltpu.VMEM((SEQ, tile), BF16)],
        compiler_params=_params(("arbitrary",)),
    )(h2, w_up, w_up, conv_w, conv_w, conv_b, conv_b, w_down, df)


def ffn_bwd_up(h2, w_up, dug, duv):
    tile = FF_TILE
    per = D_MODEL // tile

    def body(h_hbm, wg_ref, wv_ref, dug_ref, duv_ref, dh_hbm, dup_hbm, h_ref, dh_ref, dwg_ref, dwv_ref, sem, up_sems):
        t = pl.program_id(0)

        @pl.when(t == 0)
        def _():
            pltpu.sync_copy(h_hbm, h_ref)
            dh_ref[...] = jnp.zeros_like(dh_ref)

        h, dug_b, duv_b = h_ref[...], dug_ref[...], duv_ref[...]
        cols = pl.ds(pl.multiple_of((t % per) * tile, tile), tile)
        to_gate = pltpu.make_async_copy(dwg_ref, dup_hbm.at[t // per, :, cols], up_sems.at[0])
        to_value = pltpu.make_async_copy(dwv_ref, dup_hbm.at[N_DEV // 2 + t // per, :, cols], up_sems.at[1])
        dwg_ref[...] = _dot(h, dug_b, TN)
        to_gate.start()
        dwv_ref[...] = _dot(h, duv_b, TN)
        to_value.start()
        dh_ref[...] += _dot(jnp.concatenate([dug_b, duv_b], axis=1),
                            jnp.concatenate([wg_ref[0], wv_ref[0]], axis=1), NT)
        to_gate.wait()
        to_value.wait()

        @pl.when(t == D_FF // tile - 1)
        def _():
            cp = pltpu.make_async_copy(dh_ref, dh_hbm, sem)
            cp.start()
            cp.wait()

    hbm = pl.BlockSpec(memory_space=pl.ANY)
    w_g, w_v = _ffn_specs(tile)[:2]
    col = pl.BlockSpec((SEQ, tile), lambda t: (0, t))
    return pl.pallas_call(
        body, grid=(D_FF // tile,), name="ffn_bwd_up",
        in_specs=[hbm, w_g, w_v, col, col], out_specs=[hbm, hbm],
        out_shape=[_sds((SEQ, D_MODEL)), _sds((N_DEV, D_MODEL, D_MODEL))],
        scratch_shapes=[pltpu.VMEM((SEQ, D_MODEL), BF16), pltpu.VMEM((SEQ, D_MODEL), F32),
                        pltpu.VMEM((D_MODEL, tile), F32), pltpu.VMEM((D_MODEL, tile), F32),
                        pltpu.SemaphoreType.DMA, pltpu.SemaphoreType.DMA((2,))],
        compiler_params=_params(("arbitrary",)),
    )(h2, w_up, w_up, dug, duv)


def loss_head(x1, f, target, n_post):
    def tile_loss(x1_t, f_t, g, tgt):
        err = x1_t + _rms(f_t, g) - tgt
        return 0.5 * jnp.sum(jnp.mean(err * err, axis=-1))

    def body(x_ref, f_ref, t_ref, g_ref, dx_ref, df_ref, dg_ref, loss_ref):
        val, (dx, df, dg) = jax.value_and_grad(tile_loss, argnums=(0, 1, 2))(
            x_ref[...], f_ref[...], g_ref[...], t_ref[...])
        dx_ref[...] = dx
        df_ref[...] = df.astype(BF16)

        @pl.when(pl.program_id(0) == 0)
        def _():
            dg_ref[...] = jnp.zeros_like(dg_ref)
            loss_ref[...] = jnp.zeros_like(loss_ref)

        dg_ref[...] += dg
        loss_ref[...] += jnp.full((1, LANES), val, F32)

    tile = pl.BlockSpec((TOK_TILE, D_MODEL), lambda i: (i, 0))
    vec = pl.BlockSpec((1, D_MODEL), lambda i: (0, 0))
    return pl.pallas_call(
        body, grid=(SEQ // TOK_TILE,), name="loss_head",
        in_specs=[tile, tile, tile, vec],
        out_specs=[tile, tile, vec, pl.BlockSpec((1, LANES), lambda i: (0, 0))],
        out_shape=[_sds((SEQ, D_MODEL)), _sds((SEQ, D_MODEL), BF16), _sds((1, D_MODEL)), _sds((1, LANES))],
        compiler_params=_params(("arbitrary",)),
    )(x1, f, target, n_post)


def _mesh_pos():
    return lax.axis_index("x"), lax.axis_index("y"), lax.axis_index("c")


def _flip(pos, rel):
    x, y, c = pos
    return (1 - x if rel & 4 else x, 1 - y if rel & 2 else y, 1 - c if rel & 1 else c)


def _slot(pos):
    x, y, c = pos
    return 4 * x + 2 * y + c


def cast_bf16(w, rows):
    def body(w_ref, o_ref):
        o_ref[...] = w_ref[...].astype(BF16)

    spec = pl.BlockSpec((rows, w.shape[1]), lambda i: (i, 0))
    return pl.pallas_call(body, grid=(w.shape[0] // rows,), name="cast_bf16_%dx%d" % w.shape,
                          in_specs=[spec], out_specs=spec, out_shape=_sds(w.shape, BF16),
                          compiler_params=_params(("arbitrary",)))(w)


class CommPlan:
    def __init__(self, ins, out_shape, scratch, stages):
        self.ins, self.out_shape, self.scratch, self.stages = ins, out_shape, scratch, stages


def run_comm(name, plan):
    n_in, n_out = len(plan.ins), len(plan.out_shape)

    def body(*refs):
        for stage in plan.stages(refs[:n_in], refs[n_in:n_in + n_out], refs[n_in + n_out:]):
            stage()

    any_spec = pl.BlockSpec(memory_space=pl.ANY)
    return pl.pallas_call(
        body, name=name, in_specs=[any_spec] * len(plan.ins), out_specs=[any_spec] * len(plan.out_shape),
        out_shape=plan.out_shape, scratch_shapes=plan.scratch)(*plan.ins)


def gather_plan(shards):
    n = len(shards)

    def stages(srcs, outs, sems):
        send_sems, recv_sems, local_sems = sems

        def places():
            me = _mesh_pos()
            return me, _flip(me, 1), [_flip(me, 2), _flip(me, 4), _flip(me, 6)]

        def copy(a, k, block, to, src=None):
            dst = outs[a].at[_slot(block)]
            return pltpu.make_async_remote_copy(
                src_ref=dst if src is None else src, dst_ref=dst,
                send_sem=send_sems.at[7 * a + k], recv_sem=recv_sems.at[7 * a + k],
                device_id=to, device_id_type=pl.DeviceIdType.MESH)

        def local(a, me):
            return pltpu.make_async_copy(srcs[a], outs[a].at[_slot(me)], local_sems.at[a])

        def own(a, me, sibling, chips):
            return [copy(a, 0, me, sibling, src=srcs[a])] + [
                copy(a, 1 + j, me, chip, src=srcs[a]) for j, chip in enumerate(chips)]

        def start():
            me, sibling, chips = places()
            for a in range(n):
                local(a, me).start()
                for cp in own(a, me, sibling, chips):
                    cp.start()

        def forward():
            me, sibling, chips = places()
            for j, chip in enumerate(chips):
                for a in range(n):
                    copy(a, 1 + j, chip, me).wait_recv()
                    copy(a, 4 + j, chip, sibling).start()

        def finish():
            me, sibling, chips = places()
            for a in range(n):
                copy(a, 0, sibling, me).wait_recv()
                for j, chip in enumerate(chips):
                    copy(a, 4 + j, _flip(chip, 1), me).wait_recv()
            for a in range(n):
                for cp in own(a, me, sibling, chips):
                    cp.wait_send()
                for j, chip in enumerate(chips):
                    copy(a, 4 + j, chip, sibling).wait_send()
                local(a, me).wait()

        return start, forward, finish

    return CommPlan(list(shards), [_sds((N_DEV,) + s.shape, s.dtype) for s in shards],
                    [pltpu.SemaphoreType.DMA((7 * n,)), pltpu.SemaphoreType.DMA((7 * n,)),
                     pltpu.SemaphoreType.DMA((n,))], stages)


def exchange_plan(parts, replicated, rels, members, index, member_axis, own_copy):
    n, nr = len(parts), len(rels)
    pick_index = (slice(None),) * member_axis + (0,)
    subs = [1 if (r or member_axis == 0) else p.shape[0] for p, r in zip(parts, replicated)]
    first = [sum(subs[:a]) for a in range(n)]
    total = sum(subs)

    def stages(srcs, outs, sems):
        send_sems, recv_sems, local_sems = sems

        def src(a, s, pos):
            if replicated[a]:
                return srcs[a]
            return srcs[a].at[index(pos)] if member_axis == 0 else srcs[a].at[s, index(pos)]

        def dst(a, s, pos):
            block = outs[a].at[index(pos)]
            return block if (replicated[a] or member_axis == 0) else block.at[s]

        def copy(a, s, j, me, src_pos, dst_pos):
            sem = nr * (first[a] + s) + j
            return pltpu.make_async_remote_copy(
                src_ref=src(a, s, src_pos), dst_ref=dst(a, s, dst_pos),
                send_sem=send_sems.at[sem], recv_sem=recv_sems.at[sem],
                device_id=_flip(me, rels[j]), device_id_type=pl.DeviceIdType.MESH)

        pieces = [(a, s) for a in range(n) for s in range(subs[a])]

        def local(a, s, me):
            return pltpu.make_async_copy(src(a, s, me), dst(a, s, me), local_sems.at[first[a] + s])

        def sends(me):
            return [copy(a, s, j, me, _flip(me, rels[j]), me) for j in range(nr) for a, s in pieces]

        own = pieces if own_copy else []

        def start():
            me = _mesh_pos()
            for cp in sends(me) + [local(a, s, me) for a, s in own]:
                cp.start()

        def middle():
            pass

        def finish():
            me = _mesh_pos()
            for j in range(nr):
                for a, s in pieces:
                    copy(a, s, j, me, me, _flip(me, rels[j])).wait_recv()
            for cp in sends(me):
                cp.wait_send()
            for a, s in own:
                local(a, s, me).wait()

        return start, middle, finish

    shapes = [p.shape if r else jax.eval_shape(lambda t: t[pick_index], p).shape for p, r in zip(parts, replicated)]
    return CommPlan(list(parts), [_sds((members,) + s, p.dtype) for s, p in zip(shapes, parts)],
                    [pltpu.SemaphoreType.DMA((nr * total,)), pltpu.SemaphoreType.DMA((nr * total,)),
                     pltpu.SemaphoreType.DMA((total,))], stages)


def pair_plan(parts, replicated):
    return exchange_plan(parts, replicated, [1], 2, lambda pos: pos[2], 1, False)


def chip_plan(parts, replicated):
    return exchange_plan(parts, replicated, [2, 4, 6], 4, lambda pos: 2 * pos[0] + pos[1], 0, True)


def add_pair(name, mine, swapped, out_dtype, rows):
    def body(m_ref, s_ref, o_ref):
        own = m_ref[0, 0] if mine.ndim == 4 else m_ref[0]
        o_ref[0] = (own + s_ref[0, 0]).astype(o_ref.dtype)

    _, n, r, c = swapped.shape
    core = lambda: lax.axis_index("c")
    if mine.ndim == 4:
        mine_spec = pl.BlockSpec((1, 1, rows, c), lambda i, j: (i, core(), j, 0))
    else:
        mine_spec = pl.BlockSpec((1, rows, c), lambda i, j: (i, j, 0))
    return pl.pallas_call(
        body, grid=(n, r // rows), name=name,
        in_specs=[mine_spec, pl.BlockSpec((1, 1, rows, c), lambda i, j: (1 - core(), i, j, 0))],
        out_specs=pl.BlockSpec((1, rows, c), lambda i, j: (i, j, 0)),
        out_shape=_sds((n, r, c), out_dtype),
        compiler_params=_params(("arbitrary", "arbitrary")),
    )(mine, swapped)


def add_pair_small(mines, swappeds):
    n = len(mines)
    halves = [m.ndim == s.ndim for m, s in zip(mines, swappeds)]

    def body(*refs):
        c = lax.axis_index("c")
        for i in range(n):
            m_ref, s_ref, o_ref = refs[i], refs[n + i], refs[2 * n + i]
            o_ref[...] = (m_ref[:, c] if halves[i] else m_ref[...]) + s_ref[1 - c]

    vmem = pl.BlockSpec(memory_space=pltpu.VMEM)
    return pl.pallas_call(
        body, name="pair_add_small", in_specs=[vmem] * (2 * n), out_specs=[vmem] * n,
        out_shape=[_sds(s.shape[1:]) for s in swappeds], compiler_params=_params(),
    )(*mines, *swappeds)


def _adamw_math(w, g, m, v):
    nm = ADAM_B1 * m + (1.0 - ADAM_B1) * g
    nv = ADAM_B2 * v + (1.0 - ADAM_B2) * (g * g)
    m_hat = nm / (1.0 - ADAM_B1 ** ADAM_STEP)
    v_hat = nv / (1.0 - ADAM_B2 ** ADAM_STEP)
    return -ADAM_LR * (m_hat / (jnp.sqrt(v_hat) + ADAM_EPS) + ADAM_WD * w), nm, nv


def adamw_small(ws, parts, ms, vs):
    n = len(ws)

    def body(*refs):
        for i in range(n):
            w_ref, p_ref, m_ref, v_ref = (refs[k * n + i] for k in range(4))
            g = p_ref[0]
            for j in range(1, p_ref.shape[0]):
                g = g + p_ref[j]
            delta, nm, nv = _adamw_math(w_ref[...], g, m_ref[...], v_ref[...])
            for k, val in enumerate((g, delta, nm, nv)):
                refs[(4 + k) * n + i][...] = val

    vmem = pl.BlockSpec(memory_space=pltpu.VMEM)
    outs = pl.pallas_call(
        body, name="adamw_small", in_specs=[vmem] * (4 * n), out_specs=[vmem] * (4 * n),
        out_shape=[_sds(w.shape) for w in ws] * 4, compiler_params=_params(),
    )(*ws, *parts, *ms, *vs)
    return [outs[k * n:(k + 1) * n] for k in range(4)]

def adamw(name, w, parts, m, v, rows):
    n_parts = parts.shape[0]

    def body(w_ref, p_ref, m_ref, v_ref, g_ref, d_ref, nm_ref, nv_ref):
        g = p_ref[0].astype(F32)
        for j in range(1, n_parts):
            g = g + p_ref[j].astype(F32)
        g_ref[...] = g
        d_ref[...], nm_ref[...], nv_ref[...] = _adamw_math(w_ref[...], g, m_ref[...], v_ref[...])

    cols = w.shape[1]
    spec = pl.BlockSpec((rows, cols), lambda i: (i, 0))
    return pl.pallas_call(
        body, grid=(w.shape[0] // rows,), name=name,
        in_specs=[spec, pl.BlockSpec((n_parts, rows, cols), lambda i: (0, i, 0)), spec, spec],
        out_specs=[spec] * 4, out_shape=[_sds(w.shape)] * 4,
        compiler_params=_params(("arbitrary",)),
    )(w, parts, m, v)


def _rows128(a):
    flat = a.reshape(-1)
    pad = (-flat.shape[0]) % LANES
    if pad:
        flat = jnp.concatenate([flat, jnp.zeros((pad,), flat.dtype)])
    return flat.reshape(-1, LANES)


def _pack(arrays):
    rows = [_rows128(a) for a in arrays]
    pad = (-sum(r.shape[0] for r in rows)) % 8
    return jnp.concatenate(rows + [jnp.zeros((pad, LANES), rows[0].dtype)] * (pad > 0), axis=0)


def _unpack(packed, like):
    out, row = [], 0
    for a in like:
        n = math.prod(a.shape)
        rows = -(-n // LANES)
        out.append(packed[row:row + rows].reshape(-1)[:n].reshape(a.shape))
        row += rows
    return out


def _to_slots(full, per):
    return full.reshape(full.shape[0], N_DEV, per).transpose(1, 0, 2)


def _from_slots(slots):
    return slots.transpose(1, 0, 2).reshape(slots.shape[1], -1)


def kernel(x, norm_mix_pre, norm_mix_post, norm_ffn_pre, norm_ffn_post, w_in, rel_bias, sinks, rwkv_shift_mix, w0, w_decay_up, a0, w_iclr_up, w_gate_up, k_k, k_a, r_k, ln_x_g, ln_x_b, w_out, w_ffn_up, conv_w, conv_b, w_ffn_down, loss_target, m_norm_mix_pre, m_norm_mix_post, m_norm_ffn_pre, m_norm_ffn_post, m_w_in, m_rel_bias, m_sinks, m_rwkv_shift_mix, m_w0, m_w_decay_up, m_a0, m_w_iclr_up, m_w_gate_up, m_k_k, m_k_a, m_r_k, m_ln_x_g, m_ln_x_b, m_w_out, m_w_ffn_up, m_conv_w, m_conv_b, m_w_ffn_down, v_norm_mix_pre, v_norm_mix_post, v_norm_ffn_pre, v_norm_ffn_post, v_w_in, v_rel_bias, v_sinks, v_rwkv_shift_mix, v_w0, v_w_decay_up, v_a0, v_w_iclr_up, v_w_gate_up, v_k_k, v_k_a, v_r_k, v_ln_x_g, v_ln_x_b, v_w_out, v_w_ffn_up, v_conv_w, v_conv_b, v_w_ffn_down):
    x2 = x[0]
    target = loss_target[0]

    g_in, g_out, g_decay, g_iclr, g_gate, g_conv = run_comm("all_gather_mixer", gather_plan([
        cast_bf16(w_in[0], 256), cast_bf16(w_out[0], 128), w_decay_up[0], w_iclr_up[0], w_gate_up[0], conv_w[0]]))
    ffn_gather = gather_plan([cast_bf16(w_ffn_up[0], 256), cast_bf16(w_ffn_down[0], 256)])
    w_in_b = _from_slots(g_in)
    w_out_b = g_out.reshape(D_MODEL, D_MODEL)
    lora = jnp.zeros((HEAD_DIM, D_RWKV), F32)
    wd_pad = jnp.concatenate([_from_slots(g_decay), lora], axis=0)
    wi_pad = jnp.concatenate([lora, _from_slots(g_iclr)], axis=0)
    wg_full = _from_slots(g_gate)
    mix_ext = jnp.concatenate([jnp.zeros((1, D_QKV), F32), rwkv_shift_mix], axis=1)
    r_k_row = r_k.reshape(1, D_RWKV)
    bucket = _bucket_table()

    (h1,) = tok_fwd("rms_mix_pre", rms_tile, [x2], [norm_mix_pre], [], [D_MODEL], [BF16])
    proj, ps = in_proj_fwd(h1, w_in_b, mix_ext)
    attn = attn_fwd(proj, rel_bias, bucket, sinks)
    pre_params = [w0, wd_pad, a0, wi_pad, wg_full, k_k, k_a]
    r_, lw_, k2_, v_, kk_, a_, gate_ = tok_fwd("rwkv_pre", rwkv_pre_tile, [ps], pre_params, [],
                                               [D_RWKV] * 7, [F32] * 7)
    (o_, states), (g_up, g_down) = rwkv_scan_fwd(r_, lw_, k2_, v_, kk_, a_, ffn_gather)
    w_down_b = g_down.reshape(D_FF, D_MODEL)
    mix_tiles = [o_, r_, k2_, v_, gate_, attn, x2]
    mix_params = [w_out_b, norm_mix_post, ln_x_g, ln_x_b, r_k_row]
    (x1,) = tok_fwd("mix_out", mix_out_tile, mix_tiles, mix_params, [(D_MODEL, D_MODEL)], [D_MODEL], [F32])
    (h2,) = tok_fwd("rms_ffn_pre", rms_tile, [x1], [norm_ffn_pre], [], [D_MODEL], [BF16])
    f = ffn_fwd(h2, g_up, g_conv, conv_b, w_down_b)
    dy, df, d_n_ffn_post, loss_row = loss_head(x1, f, target, norm_ffn_post)
    loss = lax.psum(loss_row[0, 0], ("x", "y", "c"))

    d_ug, d_uv, d_cw_g, d_cw_v, d_cb_g, d_cb_v, d_down = ffn_bwd_mid(h2, g_up, g_conv, conv_b, w_down_b, df)
    dh2, d_up = ffn_bwd_up(h2, g_up, d_ug, d_uv)
    half = N_DEV // 2
    d_cw = jnp.concatenate([d_cw_g[:half], d_cw_v[half:]], axis=0)
    by_pair = lambda slots: slots.reshape((N_DEV // 2, 2) + slots.shape[1:])
    ffn_mine = [by_pair(d_up), by_pair(d_down.reshape(N_DEV, D_FF // N_DEV, D_MODEL))]
    ffn_swapped = run_comm("pair_exchange_ffn", pair_plan(ffn_mine, [False, False]))
    ffn_exchange = chip_plan([add_pair("pair_add_w_ffn_up", ffn_mine[0], ffn_swapped[0], BF16, 256),
                              add_pair("pair_add_w_ffn_down", ffn_mine[1], ffn_swapped[1], BF16, 256)],
                             [False, False])
    d_cb = jnp.concatenate([d_cb_g[:, :D_FF], d_cb_v[:, D_FF:]], axis=1)
    dx1_ffn, d_n_ffn_pre = tok_bwd("rms_ffn_pre_bwd", rms_tile, [x1], [norm_ffn_pre], [], [dh2], [0])
    (d_o, d_r1, d_k1, d_v1, d_gate, d_attn, dx_res, d_n_mix_post, d_ln_g, d_ln_b, d_r_k,
     d_w_out) = tok_bwd("mix_out_bwd", mix_out_tile, mix_tiles, mix_params, [(D_MODEL, D_MODEL)], [(dy, dx1_ffn)],
                        [1, 2, 3, 4])
    (d_r2, d_lw, d_k2, d_v2, d_kk, d_a), (got_up, got_down) = rwkv_scan_bwd(
        r_, lw_, k2_, v_, kk_, a_, states, d_o, ffn_exchange)
    pre_cots = [(d_r1, d_r2), d_lw, (d_k1, d_k2), (d_v1, d_v2), d_kk, d_a, d_gate]
    (d_ps, d_w0, d_wd_pad, d_a0, d_wi_pad, d_wg, d_k_k, d_k_a) = tok_bwd(
        "rwkv_pre_bwd", rwkv_pre_tile, [ps], pre_params, [], pre_cots, [0, 1, 2, 3, 4, 5, 6])
    dq, dkc, dkp, dvc, dvp, d_rel_bias, d_sinks = attn_bwd(proj, rel_bias, bucket, sinks, d_attn)
    zero_blk = jnp.zeros((BLOCK, D_KV), F32)
    dk = dkc + jnp.concatenate([dkp[BLOCK:], zero_blk], axis=0)
    dv = dvc + jnp.concatenate([dvp[BLOCK:], zero_blk], axis=0)
    dpa = jnp.concatenate([dq, dk, dv, d_ps], axis=1)
    dh1, d_w_in, d_mix_ext = in_proj_bwd(h1, w_in_b, mix_ext, proj, dpa)
    grad_x2, d_n_mix_pre = tok_bwd("rms_mix_pre_bwd", rms_tile, [x2], [norm_mix_pre], [], [dh1], [0], {0: dx_res})
    grad_x = grad_x2[None]

    small_rep = [d_n_mix_pre, d_n_mix_post, d_n_ffn_pre, d_n_ffn_post, d_rel_bias, d_sinks,
                 d_mix_ext[:, D_QKV:], d_w0, d_a0, d_k_k, d_k_a, d_r_k.reshape(r_k.shape), d_ln_g, d_ln_b, d_cb]
    rep_w = [norm_mix_pre, norm_mix_post, norm_ffn_pre, norm_ffn_post, rel_bias, sinks, rwkv_shift_mix,
             w0, a0, k_k, k_a, r_k, ln_x_g, ln_x_b, conv_b]
    rep_m = [m_norm_mix_pre, m_norm_mix_post, m_norm_ffn_pre, m_norm_ffn_post, m_rel_bias, m_sinks,
             m_rwkv_shift_mix, m_w0, m_a0, m_k_k, m_k_a, m_r_k, m_ln_x_g, m_ln_x_b, m_conv_b]
    rep_v = [v_norm_mix_pre, v_norm_mix_post, v_norm_ffn_pre, v_norm_ffn_post, v_rel_bias, v_sinks,
             v_rwkv_shift_mix, v_w0, v_a0, v_k_k, v_k_a, v_r_k, v_ln_x_g, v_ln_x_b, v_conv_b]
    sh_w = [w_decay_up, w_iclr_up, w_gate_up, conv_w]
    sh_m = [m_w_decay_up, m_w_iclr_up, m_w_gate_up, m_conv_w]
    sh_v = [v_w_decay_up, v_w_iclr_up, v_w_gate_up, v_conv_w]
    sh_parts = [_to_slots(d_wd_pad[:HEAD_DIM], HEAD_DIM), _to_slots(d_wi_pad[HEAD_DIM:], HEAD_DIM),
                _to_slots(d_wg, HEAD_DIM), d_cw]
    n_rep, n_sh = len(small_rep), len(sh_parts)
    mine = [by_pair(_to_slots(d_w_in, D_IN // N_DEV)), by_pair(d_w_out.reshape(N_DEV, D_MODEL // N_DEV, D_MODEL)),
            *small_rep, *(by_pair(p) for p in sh_parts)]
    is_rep = [False, False] + [True] * n_rep + [False] * n_sh
    swapped = run_comm("pair_exchange_mixer", pair_plan(mine, is_rep))
    chip_sums = [add_pair("pair_add_w_in", mine[0], swapped[0], BF16, 512),
                 add_pair("pair_add_w_out", mine[1], swapped[1], BF16, 128),
                 *add_pair_small(mine[2:], swapped[2:])]
    got = run_comm("chip_exchange_mixer", chip_plan(chip_sums, is_rep))

    big = [adamw("adamw_w_in", w_in[0], got[0], m_w_in[0], v_w_in[0], 256),
           adamw("adamw_w_out", w_out[0], got[1], m_w_out[0], v_w_out[0], 128),
           adamw("adamw_w_ffn_up", w_ffn_up[0], got_up, m_w_ffn_up[0], v_w_ffn_up[0], 128),
           adamw("adamw_w_ffn_down", w_ffn_down[0], got_down, m_w_ffn_down[0], v_w_ffn_down[0], 128)]
    small_w = rep_w + sh_w
    as_grad = lambda arrays: [a.reshape(g.shape[1:]) for a, g in zip(arrays, got[2:])]
    small = adamw_small(as_grad(small_w), got[2:], as_grad(rep_m + sh_m), as_grad(rep_v + sh_v))
    small = [[a.reshape(w.shape) for a, w in zip(kind, small_w)] for kind in small]

    names = ["norm_mix_pre", "norm_mix_post", "norm_ffn_pre", "norm_ffn_post", "w_in", "rel_bias", "sinks",
             "rwkv_shift_mix", "w0", "w_decay_up", "a0", "w_iclr_up", "w_gate_up", "k_k", "k_a", "r_k",
             "ln_x_g", "ln_x_b", "w_out", "w_ffn_up", "conv_w", "conv_b", "w_ffn_down"]
    small_names = ["norm_mix_pre", "norm_mix_post", "norm_ffn_pre", "norm_ffn_post", "rel_bias", "sinks",
                   "rwkv_shift_mix", "w0", "a0", "k_k", "k_a", "r_k", "ln_x_g", "ln_x_b", "conv_b",
                   "w_decay_up", "w_iclr_up", "w_gate_up", "conv_w"]
    big_names = {"w_in": 0, "w_out": 1, "w_ffn_up": 2, "w_ffn_down": 3}
    outs = []
    for kind in range(4):
        for nm in names:
            if nm in big_names:
                outs.append(big[big_names[nm]][kind][None])
            else:
                outs.append(small[kind][small_names.index(nm)])
    return (loss, grad_x, *outs)
```

```python
import functools
import math

import jax
import jax.numpy as jnp
from jax import lax
from jax.experimental import pallas as pl
from jax.experimental.pallas import tpu as pltpu

F32 = jnp.float32
BF16 = jnp.bfloat16

N_DEV = 8
SEQ = 2048
D_MODEL = 1024
HEAD_DIM = 64
D_ATTN = 512
D_KV = 128
D_RWKV = 512
N_HEADS = 8
RWKV_COLS = 1792
D_QKV = D_ATTN + 2 * D_KV
D_IN = D_QKV + RWKV_COLS
D_FF = 4096
BLOCK = 128
N_BLOCKS = SEQ // BLOCK
N_BUCKETS = 32
MAX_DISTANCE = 128
NORM_EPS = 1e-6
GN_EPS = 64e-5
NEG_INF = -1e30
CHUNK = 64
N_CHUNKS = SEQ // CHUNK
SCAN_GROUPS = 4
SCAN_WIDTH = D_RWKV // SCAN_GROUPS
TOK_TILE = 256
FF_TILE = 256
FF_TILE_BWD = 128
FF_ROW_CHUNK = 256
FF_HALO = 8
COL_TILE = 256
LANES = 128
VMEM_LIMIT = 56 * 1024 * 1024

ADAM_LR = 0.001
ADAM_B1 = 0.9
ADAM_B2 = 0.999
ADAM_EPS = 1e-08
ADAM_WD = 0.01
ADAM_STEP = 10

NT = ((1,), (1,))
TN = ((0,), (0,))
NN = ((1,), (0,))


def _sds(shape, dtype=F32):
    return jax.ShapeDtypeStruct(shape, dtype)


def _params(sem=None):
    if sem is None:
        return pltpu.CompilerParams(vmem_limit_bytes=VMEM_LIMIT)
    return pltpu.CompilerParams(dimension_semantics=sem, vmem_limit_bytes=VMEM_LIMIT)


def _dot(a, b, dims):
    return lax.dot_general(a, b, (dims, ((), ())), preferred_element_type=F32)


def _split2(x):
    hi = x.astype(BF16)
    return hi, (x - hi.astype(F32)).astype(BF16)


def _dot3_raw(a, b, dims):
    ah, al = _split2(a)
    bh, bl = _split2(b)
    return _dot(ah, bh, dims) + (_dot(al, bh, dims) + _dot(ah, bl, dims))


@functools.partial(jax.custom_vjp, nondiff_argnums=(2,))
def dot3(a, b, dims):
    return _dot3_raw(a, b, dims)


def _dot3_fwd(a, b, dims):
    return _dot3_raw(a, b, dims), (a, b)


def _dot3_bwd(dims, res, g):
    a, b = res
    if dims == NN:
        return dot3(g, b, NT), dot3(a, g, TN)
    if dims == NT:
        return dot3(g, b, NN), dot3(g, a, TN)
    return dot3(b, g, NT), dot3(a, g, NN)


dot3.defvjp(_dot3_fwd, _dot3_bwd)


@jax.custom_vjp
def mm(a, b):
    return _dot(a.astype(BF16), b.astype(BF16), NN)


def _mm_fwd(a, b):
    return mm(a, b), (a, b)


def _mm_bwd(res, g):
    a, b = res
    gb = g.astype(BF16)
    return _dot(gb, b.astype(BF16), NT).astype(a.dtype), _dot(a.astype(BF16), gb, TN).astype(b.dtype)


mm.defvjp(_mm_fwd, _mm_bwd)


@jax.custom_vjp
def mm_nt(a, b):
    return _dot(a.astype(BF16), b.astype(BF16), NT)


def _mm_nt_fwd(a, b):
    return mm_nt(a, b), (a, b)


def _mm_nt_bwd(res, g):
    a, b = res
    gb = g.astype(BF16)
    return _dot(gb, b.astype(BF16), NN).astype(a.dtype), _dot(gb, a.astype(BF16), TN).astype(b.dtype)


mm_nt.defvjp(_mm_nt_fwd, _mm_nt_bwd)


@jax.custom_vjp
def mmw(a, w, wz):
    return _dot(a.astype(BF16), w, NN)


def _mmw_fwd(a, w, wz):
    return mmw(a, w, wz), (a, w)


def _mmw_bwd(res, g):
    a, w = res
    gb = g.astype(BF16)
    return _dot(gb, w, NT).astype(a.dtype), jnp.zeros_like(w), _dot(a.astype(BF16), gb, TN)


mmw.defvjp(_mmw_fwd, _mmw_bwd)


def _shift_raw(x, n):
    rows = x.shape[0]
    rolled = pltpu.roll(x, n % rows, 0)
    idx = lax.broadcasted_iota(jnp.int32, x.shape, 0)
    keep = idx >= n if n > 0 else idx < rows + n
    return jnp.where(keep, rolled, 0.0)


@functools.partial(jax.custom_vjp, nondiff_argnums=(1,))
def shift_rows(x, n):
    return _shift_raw(x, n)


def _shift_fwd(x, n):
    return _shift_raw(x, n), None


def _shift_bwd(n, _, g):
    return (_shift_raw(g, -n),)


shift_rows.defvjp(_shift_fwd, _shift_bwd)


def _head_matrix(scale):
    a = lax.broadcasted_iota(jnp.int32, (D_RWKV, D_RWKV), 0) // HEAD_DIM
    b = lax.broadcasted_iota(jnp.int32, (D_RWKV, D_RWKV), 1) // HEAD_DIM
    return jnp.where(a == b, scale, 0.0).astype(F32)


def _rms(x, g):
    return x * lax.rsqrt(jnp.mean(x * x, axis=-1, keepdims=True) + NORM_EPS) * g


def _softplus(x):
    return jnp.maximum(x, 0.0) + jnp.log(1.0 + jnp.exp(-jnp.abs(x)))


def _tile_spec(arr, tm):
    return pl.BlockSpec((tm, arr.shape[1]), lambda i: (i, 0))


def _full_spec(arr):
    nd = arr.ndim
    return pl.BlockSpec(arr.shape, lambda i: (0,) * nd)


def tok_fwd(name, fn, tiles, params, zero_shapes, out_widths, out_dtypes, tm=TOK_TILE):
    n_t, n_p = len(tiles), len(params)

    def body(*refs):
        t_vals = [r[...] for r in refs[:n_t]]
        p_vals = [r[...] for r in refs[n_t:n_t + n_p]]
        z_vals = [jnp.zeros(s, F32) for s in zero_shapes]
        outs = fn(*t_vals, *p_vals, *z_vals)
        for r, o in zip(refs[n_t + n_p:], outs):
            r[...] = o.astype(r.dtype)

    rows = tiles[0].shape[0]
    return pl.pallas_call(
        body, grid=(rows // tm,), name=name,
        in_specs=[_tile_spec(t, tm) for t in tiles] + [_full_spec(p) for p in params],
        out_specs=[pl.BlockSpec((tm, w), lambda i: (i, 0)) for w in out_widths],
        out_shape=[_sds((rows, w), dt) for w, dt in zip(out_widths, out_dtypes)],
        compiler_params=_params(("arbitrary",)),
    )(*tiles, *params)


def tok_bwd(name, fn, tiles, params, zero_shapes, cots, diff_params, residuals=(), tm=TOK_TILE):
    cot_parts = [c if isinstance(c, tuple) else (c,) for c in cots]
    flat_cots = [a for part in cot_parts for a in part]
    residuals = dict(residuals)
    extra = [residuals[i] for i in sorted(residuals)]
    n_t, n_p, n_c, n_r = len(tiles), len(params), len(flat_cots), len(extra)
    acc_shapes = [params[i].shape for i in diff_params] + list(zero_shapes)

    def body(*refs):
        t_vals = [r[...].astype(F32) for r in refs[:n_t]]
        p_vals = [r[...] for r in refs[n_t:n_t + n_p]]
        flat = iter(r[...] for r in refs[n_t + n_p:n_t + n_p + n_c])
        c_vals = [functools.reduce(jnp.add, [next(flat) for _ in part]) for part in cot_parts]
        r_vals = dict(zip(sorted(residuals), (r[...] for r in refs[n_t + n_p + n_c:n_t + n_p + n_c + n_r])))
        out_refs = refs[n_t + n_p + n_c + n_r:]
        z_vals = [jnp.zeros(s, F32) for s in zero_shapes]
        d_vals = [p_vals[i] for i in diff_params]

        def f(t_in, d_in, z_in):
            full = list(p_vals)
            for i, v in zip(diff_params, d_in):
                full[i] = v
            return tuple(fn(*t_in, *full, *z_in))

        _, vjp = jax.vjp(f, t_vals, d_vals, z_vals)
        g_t, g_d, g_z = vjp(tuple(c_vals))
        for i, (r, g) in enumerate(zip(out_refs[:n_t], g_t)):
            r[...] = (g + r_vals[i] if i in r_vals else g).astype(r.dtype)
        acc_refs = out_refs[n_t:]

        @pl.when(pl.program_id(0) == 0)
        def _():
            for r in acc_refs:
                r[...] = jnp.zeros_like(r)

        for r, g in zip(acc_refs, list(g_d) + list(g_z)):
            r[...] += g

    rows = tiles[0].shape[0]
    return pl.pallas_call(
        body, grid=(rows // tm,), name=name,
        in_specs=[_tile_spec(t, tm) for t in tiles] + [_full_spec(p) for p in params]
        + [_tile_spec(c, tm) for c in flat_cots + extra],
        out_specs=[_tile_spec(t, tm) for t in tiles]
        + [pl.BlockSpec(s, lambda i, nd=len(s): (0,) * nd) for s in acc_shapes],
        out_shape=[_sds(t.shape) for t in tiles] + [_sds(s) for s in acc_shapes],
        compiler_params=_params(("arbitrary",)),
    )(*tiles, *params, *flat_cots, *extra)


def rms_tile(x, g):
    return (_rms(x, g),)


def rwkv_pre_tile(ps, w0, wd_pad, a0, wi_pad, wg, k_k, k_a):
    r = ps[:, 0:D_RWKV]
    k = ps[:, D_RWKV:2 * D_RWKV]
    v = ps[:, 2 * D_RWKV:3 * D_RWKV]
    z2 = ps[:, 3 * D_RWKV:3 * D_RWKV + LANES]
    zg = ps[:, 3 * D_RWKV + LANES:RWKV_COLS]
    w_log = -_softplus(-(w0 + mm(jnp.tanh(z2), wd_pad))) - 0.5
    lw = -jnp.exp(w_log)
    a = jax.nn.sigmoid(a0 + mm(z2, wi_pad))
    g = mm(jax.nn.sigmoid(zg), wg)
    kk = k * k_k
    norm = jnp.sqrt(dot3(kk * kk, _head_matrix(1.0), NN))
    kk = kk / jnp.maximum(norm, 1e-12)
    k2 = k * (1.0 + (a - 1.0) * k_a)
    return r, lw, k2, v, kk, a, g


def mix_out_tile(o, r, k2, v, g, attn, x, w_out, n_post, ln_g, ln_b, r_k, n_ffn_pre, wz):
    hmean = _head_matrix(1.0 / HEAD_DIM)
    d = o - dot3(o, hmean, NN)
    var = dot3(d * d, hmean, NN)
    on = d * lax.rsqrt(var + GN_EPS) * ln_g + ln_b
    bonus = dot3(r * k2 * r_k, _head_matrix(1.0), NN) * v
    rw = (on + bonus) * g
    mix = mmw(jnp.concatenate([attn, rw], axis=1), w_out, wz)
    x1 = x + _rms(mix, n_post)
    return x1, _rms(x1, n_ffn_pre)


def in_proj_fwd(h, w_in, mix_ext):
    def body(h_ref, w_ref, m_ref, proj_ref, ps_ref):
        p = _dot(h_ref[...], w_ref[...], NN)
        proj_ref[...] = p
        ps_ref[...] = p + (_shift_raw(p, 1) - p) * m_ref[...]

    n = D_IN // COL_TILE
    first = D_QKV // COL_TILE
    return pl.pallas_call(
        body, grid=(n,), name="in_proj_fwd",
        in_specs=[pl.BlockSpec((SEQ, D_MODEL), lambda j: (0, 0)),
                  pl.BlockSpec((D_MODEL, COL_TILE), lambda j: (0, j)),
                  pl.BlockSpec((1, COL_TILE), lambda j: (0, j))],
        out_specs=[pl.BlockSpec((SEQ, COL_TILE), lambda j: (0, j)),
                   pl.BlockSpec((SEQ, COL_TILE), lambda j: (0, jnp.maximum(j - first, 0)))],
        out_shape=[_sds((SEQ, D_IN)), _sds((SEQ, RWKV_COLS))],
        compiler_params=_params(("arbitrary",)),
    )(h, w_in, mix_ext)


def in_proj_bwd(h, w_in, mix_ext, proj, dpa):
    def body(h_ref, w_ref, m_ref, p_ref, d_ref, dh_ref, dw_ref, dm_ref):
        d = d_ref[...]
        p = p_ref[...]
        dm_ref[...] = jnp.sum(d * (_shift_raw(p, 1) - p), axis=0, keepdims=True)
        dmix = d * m_ref[...]
        dp = (d - dmix + _shift_raw(dmix, -1)).astype(BF16)
        dw_ref[...] = _dot(h_ref[...], dp, TN)

        @pl.when(pl.program_id(0) == 0)
        def _():
            dh_ref[...] = jnp.zeros_like(dh_ref)

        dh_ref[...] += _dot(dp, w_ref[...], NT)

    n = D_IN // COL_TILE
    col = lambda rows: pl.BlockSpec((rows, COL_TILE), lambda j: (0, j))
    return pl.pallas_call(
        body, grid=(n,), name="in_proj_bwd",
        in_specs=[pl.BlockSpec((SEQ, D_MODEL), lambda j: (0, 0)), col(D_MODEL), col(1), col(SEQ), col(SEQ)],
        out_specs=[pl.BlockSpec((SEQ, D_MODEL), lambda j: (0, 0)), col(D_MODEL), col(1)],
        out_shape=[_sds((SEQ, D_MODEL)), _sds((D_MODEL, D_IN)), _sds((1, D_IN))],
        compiler_params=_params(("arbitrary",)),
    )(h, w_in, mix_ext, proj, dpa)


def _bucket_table():
    rel = (jnp.arange(BLOCK)[:, None] + BLOCK) - jnp.arange(2 * BLOCK)[None, :]
    n = jnp.maximum(rel, 0)
    max_exact = N_BUCKETS // 2
    large = max_exact + (jnp.log(jnp.maximum(n, 1).astype(F32) / max_exact)
                         / math.log(MAX_DISTANCE / max_exact) * (N_BUCKETS - max_exact)).astype(jnp.int32)
    large = jnp.minimum(large, N_BUCKETS - 1)
    return jnp.where(n < max_exact, n, large).astype(jnp.int32)


def _select_matrix(g, o):
    a = lax.broadcasted_iota(jnp.int32, (D_KV, D_KV), 0)
    b = lax.broadcasted_iota(jnp.int32, (D_KV, D_KV), 1)
    return ((a - HEAD_DIM * g == b - o) & (b >= o) & (b < o + HEAD_DIM)).astype(F32)


def _attn_block(q, kp, kc, vp, vc, bias, sinks, block_idx):
    kb = jnp.concatenate([kp, kc], axis=0)
    vb = jnp.concatenate([vp, vc], axis=0)
    row = lax.broadcasted_iota(jnp.int32, (BLOCK, 2 * BLOCK), 0)
    col = lax.broadcasted_iota(jnp.int32, (BLOCK, 2 * BLOCK), 1)
    rel = row + BLOCK - col
    mask = (rel >= 0) & (rel < BLOCK) & (col + (block_idx - 1) * BLOCK >= 0)
    lane8 = lax.broadcasted_iota(jnp.int32, (1, N_HEADS), 1)
    kt, vt = {}, {}
    for g in range(2):
        for o in (0, HEAD_DIM):
            sel = _select_matrix(g, o)
            kt[g, o] = mm(kb, sel)
            vt[g, o] = mm(vb, sel)
    outs = []
    for j in range(D_ATTN // LANES):
        qs = q[:, j * LANES:(j + 1) * LANES]
        acc = None
        for half in range(2):
            hq = 2 * j + half
            g, o = hq // 4, half * HEAD_DIM
            s = mm_nt(qs, kt[g, o]) * (HEAD_DIM ** -0.5) + bias[hq]
            s = jnp.where(mask, s, NEG_INF)
            sink = jnp.sum(jnp.where(lane8 == hq, sinks, 0.0), axis=1, keepdims=True)
            m = lax.stop_gradient(jnp.maximum(jnp.max(s, axis=-1, keepdims=True), sink))
            p = jnp.exp(s - m)
            probs = p / (jnp.sum(p, axis=-1, keepdims=True) + jnp.exp(sink - m))
            part = mm(probs, vt[g, o])
            acc = part if acc is None else acc + part
        outs.append(acc)
    return jnp.concatenate(outs, axis=1)


def _build_bias(rb_ref, bucket, bias_ref):
    for hq in range(N_HEADS):
        acc = jnp.zeros((BLOCK, 2 * BLOCK), F32)
        for b in range(N_BUCKETS):
            acc = jnp.where(bucket == b, rb_ref[b, hq], acc)
        bias_ref[hq] = acc


def _attn_in_specs():
    prev = lambda n: jnp.maximum(n - 1, 0)
    return [pl.BlockSpec((BLOCK, D_ATTN), lambda n: (n, 0)),
            pl.BlockSpec((BLOCK, D_KV), lambda n: (prev(n), D_ATTN // D_KV)),
            pl.BlockSpec((BLOCK, D_KV), lambda n: (n, D_ATTN // D_KV)),
            pl.BlockSpec((BLOCK, D_KV), lambda n: (prev(n), D_ATTN // D_KV + 1)),
            pl.BlockSpec((BLOCK, D_KV), lambda n: (n, D_ATTN // D_KV + 1)),
            pl.BlockSpec(memory_space=pltpu.SMEM),
            pl.BlockSpec((BLOCK, 2 * BLOCK), lambda n: (0, 0)),
            pl.BlockSpec((1, N_HEADS), lambda n: (0, 0))]


def attn_fwd(proj, rel_bias, bucket, sinks, plan):
    def body(q_ref, kp_ref, kc_ref, vp_ref, vc_ref, rb_ref, bk_ref, sk_ref, o_ref, bias_ref):
        n = pl.program_id(0)

        @pl.when(n == 0)
        def _():
            _build_bias(rb_ref, bk_ref[...], bias_ref)

        o_ref[...] = _attn_block(q_ref[...], kp_ref[...], kc_ref[...], vp_ref[...], vc_ref[...],
                                 tuple(bias_ref[h] for h in range(N_HEADS)), sk_ref[...], n)

    (attn,), gathered = call_with_comm(
        plan, 3 * N_BLOCKS // 4, body, (N_BLOCKS,), "attn_fwd", _attn_in_specs(),
        [pl.BlockSpec((BLOCK, D_ATTN), lambda n: (n, 0))], [_sds((SEQ, D_ATTN))],
        [pltpu.VMEM((N_HEADS, BLOCK, 2 * BLOCK), F32)], (proj, proj, proj, proj, proj, rel_bias, bucket, sinks))
    return attn, gathered


def attn_bwd(proj, rel_bias, bucket, sinks, d_attn):
    def body(q_ref, kp_ref, kc_ref, vp_ref, vc_ref, rb_ref, bk_ref, sk_ref, do_ref,
             dq_ref, dkc_ref, dkp_ref, dvc_ref, dvp_ref, drb_ref, dsk_ref, bias_ref, dbias_ref):
        n = pl.program_id(0)

        @pl.when(n == 0)
        def _():
            _build_bias(rb_ref, bk_ref[...], bias_ref)
            dbias_ref[...] = jnp.zeros_like(dbias_ref)
            dsk_ref[...] = jnp.zeros_like(dsk_ref)

        f = lambda q, kp, kc, vp, vc, bias, sk: _attn_block(q, kp, kc, vp, vc, bias, sk, n)
        _, vjp = jax.vjp(f, q_ref[...], kp_ref[...], kc_ref[...], vp_ref[...], vc_ref[...],
                         tuple(bias_ref[h] for h in range(N_HEADS)), sk_ref[...])
        dq, dkp, dkc, dvp, dvc, dbias, dsk = vjp(do_ref[...])
        dq_ref[...] = dq
        dkc_ref[...] = dkc
        dkp_ref[...] = dkp
        dvc_ref[...] = dvc
        dvp_ref[...] = dvp
        for h in range(N_HEADS):
            dbias_ref[h] += dbias[h]
        dsk_ref[...] += dsk

        @pl.when(n == N_BLOCKS - 1)
        def _():
            bucket_v = bk_ref[...]
            rowi = lax.broadcasted_iota(jnp.int32, (N_BUCKETS, 2 * BLOCK), 0)
            lane = lax.broadcasted_iota(jnp.int32, (N_BUCKETS, N_HEADS), 1)
            out = jnp.zeros((N_BUCKETS, N_HEADS), F32)
            for hq in range(N_HEADS):
                dbh = dbias_ref[hq]
                rows = jnp.zeros((N_BUCKETS, 2 * BLOCK), F32)
                for b in range(N_BUCKETS):
                    part = jnp.sum(jnp.where(bucket_v == b, dbh, 0.0), axis=0, keepdims=True)
                    rows = jnp.where(rowi == b, part, rows)
                tot = jnp.sum(rows, axis=1, keepdims=True)
                out = jnp.where(lane == hq, tot, out)
            drb_ref[...] = out

    blk = lambda w: pl.BlockSpec((BLOCK, w), lambda n: (n, 0))
    return pl.pallas_call(
        body, grid=(N_BLOCKS,), name="attn_bwd",
        in_specs=_attn_in_specs() + [blk(D_ATTN)],
        out_specs=[blk(D_ATTN), blk(D_KV), blk(D_KV), blk(D_KV), blk(D_KV),
                   pl.BlockSpec((N_BUCKETS, N_HEADS), lambda n: (0, 0)),
                   pl.BlockSpec((1, N_HEADS), lambda n: (0, 0))],
        out_shape=[_sds((SEQ, D_ATTN)), _sds((SEQ, D_KV)), _sds((SEQ, D_KV)), _sds((SEQ, D_KV)),
                   _sds((SEQ, D_KV)), _sds((N_BUCKETS, N_HEADS)), _sds((1, N_HEADS))],
        scratch_shapes=[pltpu.VMEM((N_HEADS, BLOCK, 2 * BLOCK), F32),
                        pltpu.VMEM((N_HEADS, BLOCK, 2 * BLOCK), F32)],
        compiler_params=_params(("arbitrary",)),
    )(proj, proj, proj, proj, proj, rel_bias, bucket, sinks, d_attn)


def _stack(x, size):
    groups = x.shape[1] // size
    lane = lax.broadcasted_iota(jnp.int32, x.shape, 1) // size
    return jnp.concatenate([jnp.where(lane == i, x, 0.0) for i in range(groups)], axis=0)


def _neumann(l):
    c = CHUNK
    t = lax.broadcasted_iota(jnp.int32, l.shape, 0)
    i = lax.broadcasted_iota(jnp.int32, l.shape, 1) % c
    inv = (i == t).astype(F32) + l
    pw = dot3(l, _stack(l, c), NN)
    for _ in range(int(math.log2(c)) - 2):
        both = dot3(jnp.concatenate([inv, pw], axis=0), _stack(pw, c), NN)
        inv = inv + both[:c]
        pw = both[c:]
    return inv + dot3(inv, _stack(pw, c), NN)


@jax.custom_vjp
def neumann_inv(l):
    return _neumann(l)


def _neumann_fwd(l):
    inv = _neumann(l)
    return inv, inv


def _neumann_bwd(inv, g):
    c = CHUNK
    bd_t = _stack(inv, c).T
    inv_t = bd_t[0:c]
    for h in range(1, inv.shape[1] // c):
        inv_t = inv_t + bd_t[h * c:(h + 1) * c]
    return (dot3(dot3(inv_t, _stack(g, c), NN), bd_t, NN),)


neumann_inv.defvjp(_neumann_fwd, _neumann_bwd)


def _cumsum_raw(x, dims):
    c = x.shape[0]
    tt = lax.broadcasted_iota(jnp.int32, (c, c), 0)
    ii = lax.broadcasted_iota(jnp.int32, (c, c), 1)
    tri = (ii <= tt).astype(BF16)
    hi = x.astype(BF16)
    rest = x - hi.astype(F32)
    mid = rest.astype(BF16)
    lo = (rest - mid.astype(F32)).astype(BF16)
    return _dot(tri, hi, dims) + (_dot(tri, mid, dims) + _dot(tri, lo, dims))


@jax.custom_vjp
def cumsum_rows(x):
    return _cumsum_raw(x, NN)


def _cumsum_fwd(x):
    return _cumsum_raw(x, NN), None


def _cumsum_bwd(_, g):
    return (_cumsum_raw(g, TN),)


cumsum_rows.defvjp(_cumsum_fwd, _cumsum_bwd)


def _rwkv_chunk(s0, r, lw, k, v, kk, a):
    heads = r.shape[1] // HEAD_DIM
    c, hc = CHUNK, heads * CHUNK
    t = lax.broadcasted_iota(jnp.int32, (c, hc), 0)
    i = lax.broadcasted_iota(jnp.int32, (c, hc), 1) % c
    strict, incl = i < t, i <= t
    stack = lambda x: _stack(x, HEAD_DIM)
    ba = lax.broadcasted_iota(jnp.int32, s0.shape, 0) // HEAD_DIM
    bb = lax.broadcasted_iota(jnp.int32, s0.shape, 1) // HEAD_DIM
    blocks = (ba == bb).astype(F32)

    cum = cumsum_rows(lw)
    cum_end = jnp.sum(lw, axis=0, keepdims=True)
    beta = kk * a
    al = -kk * jnp.exp(cum - lw)
    p_inv = jnp.exp(-cum)
    be, kb, rb = beta * p_inv, k * p_inv, r * jnp.exp(cum)
    ar = jnp.concatenate([al, rb], axis=0)
    sv = stack(v)
    l_all = dot3(ar, jnp.concatenate([stack(be), stack(kb)], axis=0), NT)
    l_ab = jnp.where(strict, l_all[:c, :hc], 0.0)
    l_ak = jnp.where(strict, l_all[:c, hc:], 0.0)
    l_rb = jnp.where(incl, l_all[c:, :hc], 0.0)
    l_rk = jnp.where(incl, l_all[c:, hc:], 0.0)
    inv = neumann_inv(l_ab)
    from_s0 = dot3(ar, s0, NT)
    from_v = dot3(jnp.concatenate([l_ak, l_rk], axis=0), sv, NN)
    u = dot3(inv, stack(from_s0[:c] + from_v[:c]), NN)
    o = from_s0[c:] + from_v[c:] + dot3(l_rb, stack(u), NN)
    to_end = jnp.exp(cum_end - cum)
    s1 = s0 * jnp.exp(cum_end) + blocks * dot3(
        jnp.concatenate([u, v], axis=0), jnp.concatenate([beta * to_end, k * to_end], axis=0), TN)
    return o, s1


def call_with_comm(plan, middle_step, body, grid, name, in_specs, out_specs, out_shape, scratch_shapes, operands):
    n_in, n_out, n_scr = len(in_specs), len(out_specs), len(scratch_shapes)
    p_in, p_out = len(plan.ins), len(plan.out_shape)

    def fused(*refs):
        refs = list(refs)
        ins, refs = refs[:n_in], refs[n_in:]
        p_ins, refs = refs[:p_in], refs[p_in:]
        outs, refs = refs[:n_out], refs[n_out:]
        p_outs, refs = refs[:p_out], refs[p_out:]
        scr, p_sems = refs[:n_scr], refs[n_scr:]
        start, middle, finish = plan.stages(p_ins, p_outs, p_sems)
        step = pl.program_id(0)
        pl.when(step == 0)(start)
        body(*ins, *outs, *scr)
        pl.when(step == middle_step)(middle)
        pl.when(step == grid[0] - 1)(finish)

    any_spec = pl.BlockSpec(memory_space=pl.ANY)
    res = pl.pallas_call(
        fused, grid=grid, name=name,
        in_specs=list(in_specs) + [any_spec] * p_in, out_specs=list(out_specs) + [any_spec] * p_out,
        out_shape=list(out_shape) + list(plan.out_shape), scratch_shapes=list(scratch_shapes) + list(plan.scratch),
        compiler_params=_params(("arbitrary",)),
    )(*operands, *plan.ins)
    return res[:n_out], res[n_out:]


def _by_group(ref):
    return jnp.stack([ref[:, g * SCAN_WIDTH:(g + 1) * SCAN_WIDTH] for g in range(SCAN_GROUPS)])


def _store_groups(ref, val):
    for g in range(SCAN_GROUPS):
        ref[:, g * SCAN_WIDTH:(g + 1) * SCAN_WIDTH] = val[g]


def rwkv_scan_fwd(r, lw, k, v, kk, a, plan):
    def body(r_ref, lw_ref, k_ref, v_ref, kk_ref, a_ref, o_ref, st_ref, s_ref):
        @pl.when(pl.program_id(0) == 0)
        def _():
            s_ref[...] = jnp.zeros_like(s_ref)

        s0 = s_ref[...]
        st_ref[0] = s0
        o, s1 = jax.vmap(_rwkv_chunk)(s0, *(_by_group(ref) for ref in (r_ref, lw_ref, k_ref, v_ref, kk_ref, a_ref)))
        _store_groups(o_ref, o)
        s_ref[...] = s1

    tb = pl.BlockSpec((CHUNK, D_RWKV), lambda c: (c, 0))
    state = (SCAN_GROUPS, SCAN_WIDTH, SCAN_WIDTH)
    return call_with_comm(
        plan, 3 * N_CHUNKS // 4, body, (N_CHUNKS,), "rwkv_scan_fwd",
        [tb] * 6, [tb, pl.BlockSpec((1,) + state, lambda c: (c, 0, 0, 0))],
        [_sds((SEQ, D_RWKV)), _sds((N_CHUNKS,) + state)], [pltpu.VMEM(state, F32)], (r, lw, k, v, kk, a))


def rwkv_scan_bwd(r, lw, k, v, kk, a, states, d_o, plan):
    def body(r_ref, lw_ref, k_ref, v_ref, kk_ref, a_ref, st_ref, do_ref,
             dr_ref, dlw_ref, dk_ref, dv_ref, dkk_ref, da_ref, ds_ref):
        @pl.when(pl.program_id(0) == 0)
        def _():
            ds_ref[...] = jnp.zeros_like(ds_ref)

        _, vjp = jax.vjp(jax.vmap(_rwkv_chunk), st_ref[0],
                         *(_by_group(ref) for ref in (r_ref, lw_ref, k_ref, v_ref, kk_ref, a_ref)))
        grads = vjp((_by_group(do_ref), ds_ref[...]))
        ds_ref[...] = grads[0]
        for ref, val in zip((dr_ref, dlw_ref, dk_ref, dv_ref, dkk_ref, da_ref), grads[1:]):
            _store_groups(ref, val)

    last = N_CHUNKS - 1
    tb = pl.BlockSpec((CHUNK, D_RWKV), lambda c: (last - c, 0))
    state = (SCAN_GROUPS, SCAN_WIDTH, SCAN_WIDTH)
    return call_with_comm(
        plan, N_CHUNKS // 4, body, (N_CHUNKS,), "rwkv_scan_bwd",
        [tb] * 6 + [pl.BlockSpec((1,) + state, lambda c: (last - c, 0, 0, 0)), tb], [tb] * 6,
        [_sds((SEQ, D_RWKV))] * 6, [pltpu.VMEM(state, F32)], (r, lw, k, v, kk, a, states, d_o))


def _ffn_mid(ug, uv, cg, cv, bg, bv):
    conv_g = bg + cg[0] * shift_rows(ug, 2) + cg[1] * shift_rows(ug, 1) + cg[2] * ug
    conv_v = bv + cv[0] * shift_rows(uv, 2) + cv[1] * shift_rows(uv, 1) + cv[2] * uv
    return jax.nn.gelu(conv_g, approximate=True) * conv_v


def _conv_rows(ref):
    return tuple(ref[0, j:j + 1, :] for j in range(3))


def _ffn_specs(tile):
    per = D_MODEL // tile
    half = N_DEV // 2
    w_g = pl.BlockSpec((1, D_MODEL, tile), lambda t: (t // per, 0, t % per))
    w_v = pl.BlockSpec((1, D_MODEL, tile), lambda t: (half + t // per, 0, t % per))
    c_g = pl.BlockSpec((1, 3, tile), lambda t: (t // per, 0, t % per))
    c_v = pl.BlockSpec((1, 3, tile), lambda t: (half + t // per, 0, t % per))
    b_g = pl.BlockSpec((1, tile), lambda t: (0, t))
    b_v = pl.BlockSpec((1, tile), lambda t: (0, D_FF // tile + t))
    w_d = pl.BlockSpec((tile, D_MODEL), lambda t: (t, 0))
    return w_g, w_v, c_g, c_v, b_g, b_v, w_d


def ffn_fwd(h2, w_up, conv_w, conv_b, w_down):
    def body(h_ref, wg_ref, wv_ref, cg_ref, cv_ref, bg_ref, bv_ref, wd_ref, f_ref):
        @pl.when(pl.program_id(0) == 0)
        def _():
            f_ref[...] = jnp.zeros_like(f_ref)

        h = h_ref[...]
        act = _ffn_mid(_dot(h, wg_ref[0], NN), _dot(h, wv_ref[0], NN), _conv_rows(cg_ref), _conv_rows(cv_ref),
                       bg_ref[...], bv_ref[...])
        f_ref[...] += _dot(act.astype(BF16), wd_ref[...], NN)

    full = pl.BlockSpec((SEQ, D_MODEL), lambda t: (0, 0))
    return pl.pallas_call(
        body, grid=(D_FF // FF_TILE,), name="ffn_fwd",
        in_specs=[full, *_ffn_specs(FF_TILE)],
        out_specs=full, out_shape=_sds((SEQ, D_MODEL)),
        compiler_params=_params(("arbitrary",)),
    )(h2, w_up, w_up, conv_w, conv_w, conv_b, conv_b, w_down)


def ffn_bwd(h2, w_up, conv_w, conv_b, w_down, df):
    tile = FF_TILE_BWD
    per = D_MODEL // tile

    def body(h_hbm, wg_ref, wv_ref, cg_ref, cv_ref, bg_ref, bv_ref, wd_ref, df_hbm,
             dh_hbm, dup_hbm, dcg_ref, dcv_ref, dbg_ref, dbv_ref, dwd_ref,
             h_ref, df_ref, dh_ref, dwg_ref, dwv_ref, sem, up_sems):
        t = pl.program_id(0)

        @pl.when(t == 0)
        def _():
            pltpu.sync_copy(h_hbm, h_ref)
            pltpu.sync_copy(df_hbm, df_ref)

        h, df_b, wg, wv = h_ref[...], df_ref[...], wg_ref[0], wv_ref[0]
        act, vjp = jax.vjp(_ffn_mid, _dot(h, wg, NN), _dot(h, wv, NN), _conv_rows(cg_ref), _conv_rows(cv_ref),
                           bg_ref[...], bv_ref[...])
        dwd_ref[...] = _dot(act.astype(BF16), df_b, TN)
        dug, duv, dcg, dcv, dbg, dbv = vjp(_dot(df_b, wd_ref[...], NT))
        dug, duv = dug.astype(BF16), duv.astype(BF16)
        cols = pl.ds(pl.multiple_of((t % per) * tile, tile), tile)
        to_gate = pltpu.make_async_copy(dwg_ref, dup_hbm.at[t // per, :, cols], up_sems.at[0])
        to_value = pltpu.make_async_copy(dwv_ref, dup_hbm.at[N_DEV // 2 + t // per, :, cols], up_sems.at[1])
        dwg_ref[...] = _dot(h, dug, TN)
        to_gate.start()
        dwv_ref[...] = _dot(h, duv, TN)
        to_value.start()

        @pl.when(pl.program_id(0) == 0)
        def _():
            dh_ref[...] = jnp.zeros_like(dh_ref)

        dh_ref[...] += _dot(dug, wg, NT) + _dot(duv, wv, NT)
        for j in range(3):
            dcg_ref[0, j:j + 1, :] = dcg[j]
            dcv_ref[0, j:j + 1, :] = dcv[j]
        dbg_ref[...] = dbg
        dbv_ref[...] = dbv
        to_gate.wait()
        to_value.wait()

        @pl.when(t == D_FF // tile - 1)
        def _():
            cp = pltpu.make_async_copy(dh_ref, dh_hbm, sem)
            cp.start()
            cp.wait()

    hbm = pl.BlockSpec(memory_space=pl.ANY)
    w_g, w_v, c_g, c_v, b_g, b_v, w_d = _ffn_specs(tile)
    return pl.pallas_call(
        body, grid=(D_FF // tile,), name="ffn_bwd",
        in_specs=[hbm, w_g, w_v, c_g, c_v, b_g, b_v, w_d, hbm],
        out_specs=[hbm, hbm, c_g, c_v, b_g, b_v, w_d],
        out_shape=[_sds((SEQ, D_MODEL)), _sds((N_DEV, D_MODEL, D_MODEL)),
                   _sds((N_DEV, 3, D_MODEL)), _sds((N_DEV, 3, D_MODEL)), _sds((1, 2 * D_FF)), _sds((1, 2 * D_FF)),
                   _sds((D_FF, D_MODEL))],
        scratch_shapes=[pltpu.VMEM((SEQ, D_MODEL), BF16), pltpu.VMEM((SEQ, D_MODEL), BF16),
                        pltpu.VMEM((SEQ, D_MODEL), F32), pltpu.VMEM((D_MODEL, tile), F32),
                        pltpu.VMEM((D_MODEL, tile), F32), pltpu.SemaphoreType.DMA, pltpu.SemaphoreType.DMA((2,))],
        compiler_params=_params(("arbitrary",)),
    )(h2, w_up, w_up, conv_w, conv_w, conv_b, conv_b, w_down, df)


def ffn_bwd_mid(h2, w_up, conv_w, conv_b, w_down, df):
    tile, rows, halo = FF_TILE, FF_ROW_CHUNK, FF_HALO
    ext = rows + 2 * halo

    def body(h_hbm, wg_ref, wv_ref, cg_ref, cv_ref, bg_ref, bv_ref, wd_ref, df_hbm,
             dug_ref, duv_ref, dcg_ref, dcv_ref, dbg_ref, dbv_ref, dwd_ref,
             h_ref, df_ref, ug_ref, uv_ref, da_ref, act_ref):
        @pl.when(pl.program_id(0) == 0)
        def _():
            pltpu.sync_copy(h_hbm, h_ref)
            pltpu.sync_copy(df_hbm, df_ref)
            for ref in (ug_ref, uv_ref, da_ref):
                ref[0:halo, :] = jnp.zeros((halo, tile), F32)
                ref[halo + SEQ:, :] = jnp.zeros((halo, tile), F32)

        h, df_b = h_ref[...], df_ref[...]
        ug_ref[halo:halo + SEQ, :] = _dot(h, wg_ref[0], NN)
        uv_ref[halo:halo + SEQ, :] = _dot(h, wv_ref[0], NN)
        da_ref[halo:halo + SEQ, :] = _dot(df_b, wd_ref[...], NT)
        cg, cv, bg, bv = _conv_rows(cg_ref), _conv_rows(cv_ref), bg_ref[...], bv_ref[...]
        down = lambda x, n: pltpu.roll(x, n, 0)
        up = lambda x, n: pltpu.roll(x, ext - n, 0)
        mid = slice(halo, halo + rows)

        def chunk(i, sums):
            r0 = pl.multiple_of(i * rows, rows)
            window = pl.ds(r0, ext)
            ug, uv, da = ug_ref[window, :], uv_ref[window, :], da_ref[window, :]
            ug1, ug2, uv1, uv2 = down(ug, 1), down(ug, 2), down(uv, 1), down(uv, 2)
            conv_g = bg + cg[0] * ug2 + cg[1] * ug1 + cg[2] * ug
            conv_v = bv + cv[0] * uv2 + cv[1] * uv1 + cv[2] * uv
            act, vjp = jax.vjp(lambda a, b: jax.nn.gelu(a, approximate=True) * b, conv_g, conv_v)
            dcg, dcv = vjp(da)
            dug = cg[2] * dcg + cg[1] * up(dcg, 1) + cg[0] * up(dcg, 2)
            duv = cv[2] * dcv + cv[1] * up(dcv, 1) + cv[0] * up(dcv, 2)
            out = pl.ds(r0, rows)
            act_ref[out, :] = act[mid].astype(BF16)
            dug_ref[out, :] = dug[mid].astype(BF16)
            duv_ref[out, :] = duv[mid].astype(BF16)
            col = lambda x: jnp.sum(x[mid], axis=0, keepdims=True)
            new = (col(dcg * ug2), col(dcg * ug1), col(dcg * ug), col(dcv * uv2), col(dcv * uv1), col(dcv * uv),
                   col(dcg), col(dcv))
            return tuple(s + n for s, n in zip(sums, new))

        zero = jnp.zeros((1, tile), F32)
        sums = lax.fori_loop(0, SEQ // rows, chunk, (zero,) * 8)
        for j in range(3):
            dcg_ref[0, j:j + 1, :] = sums[j]
            dcv_ref[0, j:j + 1, :] = sums[3 + j]
        dbg_ref[...] = sums[6]
        dbv_ref[...] = sums[7]
        dwd_ref[...] = _dot(act_ref[...], df_b, TN)

    hbm = pl.BlockSpec(memory_space=pl.ANY)
    w_g, w_v, c_g, c_v, b_g, b_v, w_d = _ffn_specs(tile)
    col = pl.BlockSpec((SEQ, tile), lambda t: (0, t))
    padded = pltpu.VMEM((SEQ + 2 * halo, tile), F32)
    return pl.pallas_call(
        body, grid=(D_FF // tile,), name="ffn_bwd_mid",
        in_specs=[hbm, w_g, w_v, c_g, c_v, b_g, b_v, w_d, hbm],
        out_specs=[col, col, c_g, c_v, b_g, b_v, w_d],
        out_shape=[_sds((SEQ, D_FF), BF16), _sds((SEQ, D_FF), BF16), _sds((N_DEV, 3, D_MODEL)),
                   _sds((N_DEV, 3, D_MODEL)), _sds((1, 2 * D_FF)), _sds((1, 2 * D_FF)), _sds((D_FF, D_MODEL))],
        scratch_shapes=[pltpu.VMEM((SEQ, D_MODEL), BF16), pltpu.VMEM((SEQ, D_MODEL), BF16), padded, padded, padded,
                        pltpu.VMEM((SEQ, tile), BF16)],
        compiler_params=_params(("arbitrary",)),
    )(h2, w_up, w_up, conv_w, conv_w, conv_b, conv_b, w_down, df)


def ffn_bwd_up(h2, w_up, dug, duv):
    tile = FF_TILE
    per = D_MODEL // tile

    def body(h_hbm, wg_ref, wv_ref, dug_ref, duv_ref, dh_hbm, dup_hbm, h_ref, dh_ref, dwg_ref, dwv_ref, sem, up_sems):
        t = pl.program_id(0)

        @pl.when(t == 0)
        def _():
            pltpu.sync_copy(h_hbm, h_ref)
            dh_ref[...] = jnp.zeros_like(dh_ref)

        h, dug_b, duv_b = h_ref[...], dug_ref[...], duv_ref[...]
        cols = pl.ds(pl.multiple_of((t % per) * tile, tile), tile)
        to_gate = pltpu.make_async_copy(dwg_ref, dup_hbm.at[t // per, :, cols], up_sems.at[0])
        to_value = pltpu.make_async_copy(dwv_ref, dup_hbm.at[N_DEV // 2 + t // per, :, cols], up_sems.at[1])
        dwg_ref[...] = _dot(h, dug_b, TN)
        to_gate.start()
        dwv_ref[...] = _dot(h, duv_b, TN)
        to_value.start()
        dh_ref[...] += _dot(jnp.concatenate([dug_b, duv_b], axis=1),
                            jnp.concatenate([wg_ref[0], wv_ref[0]], axis=1), NT)
        to_gate.wait()
        to_value.wait()

        @pl.when(t == D_FF // tile - 1)
        def _():
            cp = pltpu.make_async_copy(dh_ref, dh_hbm, sem)
            cp.start()
            cp.wait()

    hbm = pl.BlockSpec(memory_space=pl.ANY)
    w_g, w_v = _ffn_specs(tile)[:2]
    col = pl.BlockSpec((SEQ, tile), lambda t: (0, t))
    return pl.pallas_call(
        body, grid=(D_FF // tile,), name="ffn_bwd_up",
        in_specs=[hbm, w_g, w_v, col, col], out_specs=[hbm, hbm],
        out_shape=[_sds((SEQ, D_MODEL)), _sds((N_DEV, D_MODEL, D_MODEL))],
        scratch_shapes=[pltpu.VMEM((SEQ, D_MODEL), BF16), pltpu.VMEM((SEQ, D_MODEL), F32),
                        pltpu.VMEM((D_MODEL, tile), F32), pltpu.VMEM((D_MODEL, tile), F32),
                        pltpu.SemaphoreType.DMA, pltpu.SemaphoreType.DMA((2,))],
        compiler_params=_params(("arbitrary",)),
    )(h2, w_up, w_up, dug, duv)


def loss_head(x1, f, target, n_post):
    def tile_loss(x1_t, f_t, g, tgt):
        err = x1_t + _rms(f_t, g) - tgt
        return 0.5 * jnp.sum(jnp.mean(err * err, axis=-1))

    def body(x_ref, f_ref, t_ref, g_ref, dx_ref, df_ref, dg_ref, loss_ref):
        val, (dx, df, dg) = jax.value_and_grad(tile_loss, argnums=(0, 1, 2))(
            x_ref[...], f_ref[...], g_ref[...], t_ref[...])
        dx_ref[...] = dx
        df_ref[...] = df.astype(BF16)

        @pl.when(pl.program_id(0) == 0)
        def _():
            dg_ref[...] = jnp.zeros_like(dg_ref)
            loss_ref[...] = jnp.zeros_like(loss_ref)

        dg_ref[...] += dg
        loss_ref[...] += jnp.full((1, LANES), val, F32)

    tile = pl.BlockSpec((TOK_TILE, D_MODEL), lambda i: (i, 0))
    vec = pl.BlockSpec((1, D_MODEL), lambda i: (0, 0))
    return pl.pallas_call(
        body, grid=(SEQ // TOK_TILE,), name="loss_head",
        in_specs=[tile, tile, tile, vec],
        out_specs=[tile, tile, vec, pl.BlockSpec((1, LANES), lambda i: (0, 0))],
        out_shape=[_sds((SEQ, D_MODEL)), _sds((SEQ, D_MODEL), BF16), _sds((1, D_MODEL)), _sds((1, LANES))],
        compiler_params=_params(("arbitrary",)),
    )(x1, f, target, n_post)


def _mesh_pos():
    return lax.axis_index("x"), lax.axis_index("y"), lax.axis_index("c")


def _flip(pos, rel):
    x, y, c = pos
    return (1 - x if rel & 4 else x, 1 - y if rel & 2 else y, 1 - c if rel & 1 else c)


def _slot(pos):
    x, y, c = pos
    return 4 * x + 2 * y + c


def cast_bf16(w, rows):
    def body(w_ref, o_ref):
        o_ref[...] = w_ref[...].astype(BF16)

    spec = pl.BlockSpec((rows, w.shape[1]), lambda i: (i, 0))
    return pl.pallas_call(body, grid=(w.shape[0] // rows,), name="cast_bf16_%dx%d" % w.shape,
                          in_specs=[spec], out_specs=spec, out_shape=_sds(w.shape, BF16),
                          compiler_params=_params(("arbitrary",)))(w)


class CommPlan:
    def __init__(self, ins, out_shape, scratch, stages):
        self.ins, self.out_shape, self.scratch, self.stages = ins, out_shape, scratch, stages


def run_comm(name, plan):
    n_in, n_out = len(plan.ins), len(plan.out_shape)

    def body(*refs):
        for stage in plan.stages(refs[:n_in], refs[n_in:n_in + n_out], refs[n_in + n_out:]):
            stage()

    any_spec = pl.BlockSpec(memory_space=pl.ANY)
    return pl.pallas_call(
        body, name=name, in_specs=[any_spec] * len(plan.ins), out_specs=[any_spec] * len(plan.out_shape),
        out_shape=plan.out_shape, scratch_shapes=plan.scratch)(*plan.ins)


def gather_plan(shards):
    n = len(shards)

    def stages(srcs, outs, sems):
        send_sems, recv_sems, local_sems = sems

        def places():
            me = _mesh_pos()
            return me, _flip(me, 1), [_flip(me, 2), _flip(me, 4), _flip(me, 6)]

        def copy(a, k, block, to, src=None):
            dst = outs[a].at[_slot(block)]
            return pltpu.make_async_remote_copy(
                src_ref=dst if src is None else src, dst_ref=dst,
                send_sem=send_sems.at[7 * a + k], recv_sem=recv_sems.at[7 * a + k],
                device_id=to, device_id_type=pl.DeviceIdType.MESH)

        def local(a, me):
            return pltpu.make_async_copy(srcs[a], outs[a].at[_slot(me)], local_sems.at[a])

        def own(a, me, sibling, chips):
            return [copy(a, 0, me, sibling, src=srcs[a])] + [
                copy(a, 1 + j, me, chip, src=srcs[a]) for j, chip in enumerate(chips)]

        def start():
            me, sibling, chips = places()
            for a in range(n):
                local(a, me).start()
                for cp in own(a, me, sibling, chips):
                    cp.start()

        def forward():
            me, sibling, chips = places()
            for j, chip in enumerate(chips):
                for a in range(n):
                    copy(a, 1 + j, chip, me).wait_recv()
                    copy(a, 4 + j, chip, sibling).start()

        def finish():
            me, sibling, chips = places()
            for a in range(n):
                copy(a, 0, sibling, me).wait_recv()
                for j, chip in enumerate(chips):
                    copy(a, 4 + j, _flip(chip, 1), me).wait_recv()
            for a in range(n):
                for cp in own(a, me, sibling, chips):
                    cp.wait_send()
                for j, chip in enumerate(chips):
                    copy(a, 4 + j, chip, sibling).wait_send()
                local(a, me).wait()

        return start, forward, finish

    return CommPlan(list(shards), [_sds((N_DEV,) + s.shape, s.dtype) for s in shards],
                    [pltpu.SemaphoreType.DMA((7 * n,)), pltpu.SemaphoreType.DMA((7 * n,)),
                     pltpu.SemaphoreType.DMA((n,))], stages)


def exchange_plan(parts, replicated, rels, members, index, member_axis, own_copy):
    n, nr = len(parts), len(rels)
    pick_index = (slice(None),) * member_axis + (0,)
    subs = [1 if (r or member_axis == 0) else p.shape[0] for p, r in zip(parts, replicated)]
    first = [sum(subs[:a]) for a in range(n)]
    total = sum(subs)

    def stages(srcs, outs, sems):
        send_sems, recv_sems, local_sems = sems

        def src(a, s, pos):
            if replicated[a]:
                return srcs[a]
            return srcs[a].at[index(pos)] if member_axis == 0 else srcs[a].at[s, index(pos)]

        def dst(a, s, pos):
            block = outs[a].at[index(pos)]
            return block if (replicated[a] or member_axis == 0) else block.at[s]

        def copy(a, s, j, me, src_pos, dst_pos):
            sem = nr * (first[a] + s) + j
            return pltpu.make_async_remote_copy(
                src_ref=src(a, s, src_pos), dst_ref=dst(a, s, dst_pos),
                send_sem=send_sems.at[sem], recv_sem=recv_sems.at[sem],
                device_id=_flip(me, rels[j]), device_id_type=pl.DeviceIdType.MESH)

        pieces = [(a, s) for a in range(n) for s in range(subs[a])]

        def local(a, s, me):
            return pltpu.make_async_copy(src(a, s, me), dst(a, s, me), local_sems.at[first[a] + s])

        def sends(me):
            return [copy(a, s, j, me, _flip(me, rels[j]), me) for j in range(nr) for a, s in pieces]

        own = pieces if own_copy else []

        def start():
            me = _mesh_pos()
            for cp in sends(me) + [local(a, s, me) for a, s in own]:
                cp.start()

        def middle():
            pass

        def finish():
            me = _mesh_pos()
            for j in range(nr):
                for a, s in pieces:
                    copy(a, s, j, me, me, _flip(me, rels[j])).wait_recv()
            for cp in sends(me):
                cp.wait_send()
            for a, s in own:
                local(a, s, me).wait()

        return start, middle, finish

    shapes = [p.shape if r else jax.eval_shape(lambda t: t[pick_index], p).shape for p, r in zip(parts, replicated)]
    return CommPlan(list(parts), [_sds((members,) + s, p.dtype) for s, p in zip(shapes, parts)],
                    [pltpu.SemaphoreType.DMA((nr * total,)), pltpu.SemaphoreType.DMA((nr * total,)),
                     pltpu.SemaphoreType.DMA((total,))], stages)


def pair_plan(parts, replicated):
    return exchange_plan(parts, replicated, [1], 2, lambda pos: pos[2], 1, False)


def chip_plan(parts, replicated):
    return exchange_plan(parts, replicated, [2, 4, 6], 4, lambda pos: 2 * pos[0] + pos[1], 0, True)


def add_pair(name, mine, swapped, out_dtype, rows):
    def body(m_ref, s_ref, o_ref):
        own = m_ref[0, 0] if mine.ndim == 4 else m_ref[0]
        o_ref[0] = (own + s_ref[0, 0]).astype(o_ref.dtype)

    _, n, r, c = swapped.shape
    core = lambda: lax.axis_index("c")
    if mine.ndim == 4:
        mine_spec = pl.BlockSpec((1, 1, rows, c), lambda i, j: (i, core(), j, 0))
    else:
        mine_spec = pl.BlockSpec((1, rows, c), lambda i, j: (i, j, 0))
    return pl.pallas_call(
        body, grid=(n, r // rows), name=name,
        in_specs=[mine_spec, pl.BlockSpec((1, 1, rows, c), lambda i, j: (1 - core(), i, j, 0))],
        out_specs=pl.BlockSpec((1, rows, c), lambda i, j: (i, j, 0)),
        out_shape=_sds((n, r, c), out_dtype),
        compiler_params=_params(("arbitrary", "arbitrary")),
    )(mine, swapped)


def add_pair_small(mines, swappeds):
    n = len(mines)
    halves = [m.ndim == s.ndim for m, s in zip(mines, swappeds)]

    def body(*refs):
        c = lax.axis_index("c")
        for i in range(n):
            m_ref, s_ref, o_ref = refs[i], refs[n + i], refs[2 * n + i]
            o_ref[...] = (m_ref[:, c] if halves[i] else m_ref[...]) + s_ref[1 - c]

    vmem = pl.BlockSpec(memory_space=pltpu.VMEM)
    return pl.pallas_call(
        body, name="pair_add_small", in_specs=[vmem] * (2 * n), out_specs=[vmem] * n,
        out_shape=[_sds(s.shape[1:]) for s in swappeds], compiler_params=_params(),
    )(*mines, *swappeds)


def _adamw_math(w, g, m, v):
    nm = ADAM_B1 * m + (1.0 - ADAM_B1) * g
    nv = ADAM_B2 * v + (1.0 - ADAM_B2) * (g * g)
    m_hat = nm / (1.0 - ADAM_B1 ** ADAM_STEP)
    v_hat = nv / (1.0 - ADAM_B2 ** ADAM_STEP)
    return -ADAM_LR * (m_hat / (jnp.sqrt(v_hat) + ADAM_EPS) + ADAM_WD * w), nm, nv


def adamw_small(ws, parts, ms, vs):
    n = len(ws)

    def body(*refs):
        for i in range(n):
            w_ref, p_ref, m_ref, v_ref = (refs[k * n + i] for k in range(4))
            g = p_ref[0]
            for j in range(1, p_ref.shape[0]):
                g = g + p_ref[j]
            delta, nm, nv = _adamw_math(w_ref[...], g, m_ref[...], v_ref[...])
            for k, val in enumerate((g, delta, nm, nv)):
                refs[(4 + k) * n + i][...] = val

    vmem = pl.BlockSpec(memory_space=pltpu.VMEM)
    outs = pl.pallas_call(
        body, name="adamw_small", in_specs=[vmem] * (4 * n), out_specs=[vmem] * (4 * n),
        out_shape=[_sds(w.shape) for w in ws] * 4, compiler_params=_params(),
    )(*ws, *parts, *ms, *vs)
    return [outs[k * n:(k + 1) * n] for k in range(4)]

def adamw(name, w, parts, m, v, rows, plan=None):
    n_parts = parts.shape[0]

    def body(w_ref, p_ref, m_ref, v_ref, g_ref, d_ref, nm_ref, nv_ref):
        g = p_ref[0].astype(F32)
        for j in range(1, n_parts):
            g = g + p_ref[j].astype(F32)
        g_ref[...] = g
        d_ref[...], nm_ref[...], nv_ref[...] = _adamw_math(w_ref[...], g, m_ref[...], v_ref[...])

    cols = w.shape[1]
    spec = pl.BlockSpec((rows, cols), lambda i: (i, 0))
    grid = (w.shape[0] // rows,)
    in_specs = [spec, pl.BlockSpec((n_parts, rows, cols), lambda i: (0, i, 0)), spec, spec]
    if plan is not None:
        return call_with_comm(plan, 0, body, grid, name, in_specs, [spec] * 4, [_sds(w.shape)] * 4, [],
                              (w, parts, m, v))
    return pl.pallas_call(
        body, grid=grid, name=name, in_specs=in_specs, out_specs=[spec] * 4, out_shape=[_sds(w.shape)] * 4,
        compiler_params=_params(("arbitrary",)),
    )(w, parts, m, v)


def _rows128(a):
    flat = a.reshape(-1)
    pad = (-flat.shape[0]) % LANES
    if pad:
        flat = jnp.concatenate([flat, jnp.zeros((pad,), flat.dtype)])
    return flat.reshape(-1, LANES)


def _pack(arrays):
    rows = [_rows128(a) for a in arrays]
    pad = (-sum(r.shape[0] for r in rows)) % 8
    return jnp.concatenate(rows + [jnp.zeros((pad, LANES), rows[0].dtype)] * (pad > 0), axis=0)


def _unpack(packed, like):
    out, row = [], 0
    for a in like:
        n = math.prod(a.shape)
        rows = -(-n // LANES)
        out.append(packed[row:row + rows].reshape(-1)[:n].reshape(a.shape))
        row += rows
    return out


def _to_slots(full, per):
    return full.reshape(full.shape[0], N_DEV, per).transpose(1, 0, 2)


def _from_slots(slots):
    return slots.transpose(1, 0, 2).reshape(slots.shape[1], -1)


def kernel(x, norm_mix_pre, norm_mix_post, norm_ffn_pre, norm_ffn_post, w_in, rel_bias, sinks, rwkv_shift_mix, w0, w_decay_up, a0, w_iclr_up, w_gate_up, k_k, k_a, r_k, ln_x_g, ln_x_b, w_out, w_ffn_up, conv_w, conv_b, w_ffn_down, loss_target, m_norm_mix_pre, m_norm_mix_post, m_norm_ffn_pre, m_norm_ffn_post, m_w_in, m_rel_bias, m_sinks, m_rwkv_shift_mix, m_w0, m_w_decay_up, m_a0, m_w_iclr_up, m_w_gate_up, m_k_k, m_k_a, m_r_k, m_ln_x_g, m_ln_x_b, m_w_out, m_w_ffn_up, m_conv_w, m_conv_b, m_w_ffn_down, v_norm_mix_pre, v_norm_mix_post, v_norm_ffn_pre, v_norm_ffn_post, v_w_in, v_rel_bias, v_sinks, v_rwkv_shift_mix, v_w0, v_w_decay_up, v_a0, v_w_iclr_up, v_w_gate_up, v_k_k, v_k_a, v_r_k, v_ln_x_g, v_ln_x_b, v_w_out, v_w_ffn_up, v_conv_w, v_conv_b, v_w_ffn_down):
    x2 = x[0]
    target = loss_target[0]

    g_in, g_out, g_decay, g_iclr, g_gate, g_conv = run_comm("all_gather_mixer", gather_plan([
        cast_bf16(w_in[0], 256), cast_bf16(w_out[0], 128), w_decay_up[0], w_iclr_up[0], w_gate_up[0], conv_w[0]]))
    up_gather = gather_plan([cast_bf16(w_ffn_up[0], 256)])
    down_gather = gather_plan([cast_bf16(w_ffn_down[0], 256)])
    w_in_b = _from_slots(g_in)
    w_out_b = g_out.reshape(D_MODEL, D_MODEL)
    lora = jnp.zeros((HEAD_DIM, D_RWKV), F32)
    wd_pad = jnp.concatenate([_from_slots(g_decay), lora], axis=0)
    wi_pad = jnp.concatenate([lora, _from_slots(g_iclr)], axis=0)
    wg_full = _from_slots(g_gate)
    mix_ext = jnp.concatenate([jnp.zeros((1, D_QKV), F32), rwkv_shift_mix], axis=1)
    r_k_row = r_k.reshape(1, D_RWKV)
    bucket = _bucket_table()

    (h1,) = tok_fwd("rms_mix_pre", rms_tile, [x2], [norm_mix_pre], [], [D_MODEL], [BF16])
    proj, ps = in_proj_fwd(h1, w_in_b, mix_ext)
    attn, (g_down,) = attn_fwd(proj, rel_bias, bucket, sinks, down_gather)
    pre_params = [w0, wd_pad, a0, wi_pad, wg_full, k_k, k_a]
    r_, lw_, k2_, v_, kk_, a_, gate_ = tok_fwd("rwkv_pre", rwkv_pre_tile, [ps], pre_params, [],
                                               [D_RWKV] * 7, [F32] * 7)
    (o_, states), (g_up,) = rwkv_scan_fwd(r_, lw_, k2_, v_, kk_, a_, up_gather)
    w_down_b = g_down.reshape(D_FF, D_MODEL)
    mix_tiles = [o_, r_, k2_, v_, gate_, attn, x2]
    mix_params = [w_out_b, norm_mix_post, ln_x_g, ln_x_b, r_k_row, norm_ffn_pre]
    x1, h2 = tok_fwd("mix_out", mix_out_tile, mix_tiles, mix_params, [(D_MODEL, D_MODEL)], [D_MODEL, D_MODEL],
                     [F32, BF16])
    f = ffn_fwd(h2, g_up, g_conv, conv_b, w_down_b)
    dy, df, d_n_ffn_post, loss_row = loss_head(x1, f, target, norm_ffn_post)
    loss = lax.psum(loss_row[0, 0], ("x", "y", "c"))

    d_ug, d_uv, d_cw_g, d_cw_v, d_cb_g, d_cb_v, d_down = ffn_bwd_mid(h2, g_up, g_conv, conv_b, w_down_b, df)
    dh2, d_up = ffn_bwd_up(h2, g_up, d_ug, d_uv)
    half = N_DEV // 2
    d_cw = jnp.concatenate([d_cw_g[:half], d_cw_v[half:]], axis=0)
    by_pair = lambda slots: slots.reshape((N_DEV // 2, 2) + slots.shape[1:])
    ffn_mine = [by_pair(d_up), by_pair(d_down.reshape(N_DEV, D_FF // N_DEV, D_MODEL))]
    ffn_swapped = run_comm("pair_exchange_ffn", pair_plan(ffn_mine, [False, False]))
    ffn_exchange = chip_plan([add_pair("pair_add_w_ffn_up", ffn_mine[0], ffn_swapped[0], BF16, 256),
                              add_pair("pair_add_w_ffn_down", ffn_mine[1], ffn_swapped[1], BF16, 256)],
                             [False, False])
    d_cb = jnp.concatenate([d_cb_g[:, :D_FF], d_cb_v[:, D_FF:]], axis=1)
    (d_o, d_r1, d_k1, d_v1, d_gate, d_attn, dx_res, d_n_mix_post, d_ln_g, d_ln_b, d_r_k, d_n_ffn_pre,
     d_w_out) = tok_bwd("mix_out_bwd", mix_out_tile, mix_tiles, mix_params, [(D_MODEL, D_MODEL)], [dy, dh2],
                        [1, 2, 3, 4, 5])
    (d_r2, d_lw, d_k2, d_v2, d_kk, d_a), (got_up, got_down) = rwkv_scan_bwd(
        r_, lw_, k2_, v_, kk_, a_, states, d_o, ffn_exchange)
    pre_cots = [(d_r1, d_r2), d_lw, (d_k1, d_k2), (d_v1, d_v2), d_kk, d_a, d_gate]
    (d_ps, d_w0, d_wd_pad, d_a0, d_wi_pad, d_wg, d_k_k, d_k_a) = tok_bwd(
        "rwkv_pre_bwd", rwkv_pre_tile, [ps], pre_params, [], pre_cots, [0, 1, 2, 3, 4, 5, 6])
    dq, dkc, dkp, dvc, dvp, d_rel_bias, d_sinks = attn_bwd(proj, rel_bias, bucket, sinks, d_attn)
    zero_blk = jnp.zeros((BLOCK, D_KV), F32)
    dk = dkc + jnp.concatenate([dkp[BLOCK:], zero_blk], axis=0)
    dv = dvc + jnp.concatenate([dvp[BLOCK:], zero_blk], axis=0)
    dpa = jnp.concatenate([dq, dk, dv, d_ps], axis=1)
    dh1, d_w_in, d_mix_ext = in_proj_bwd(h1, w_in_b, mix_ext, proj, dpa)
    grad_x2, d_n_mix_pre = tok_bwd("rms_mix_pre_bwd", rms_tile, [x2], [norm_mix_pre], [], [dh1], [0], {0: dx_res})
    grad_x = grad_x2[None]

    small_rep = [d_n_mix_pre, d_n_mix_post, d_n_ffn_pre, d_n_ffn_post, d_rel_bias, d_sinks,
                 d_mix_ext[:, D_QKV:], d_w0, d_a0, d_k_k, d_k_a, d_r_k.reshape(r_k.shape), d_ln_g, d_ln_b, d_cb]
    rep_w = [norm_mix_pre, norm_mix_post, norm_ffn_pre, norm_ffn_post, rel_bias, sinks, rwkv_shift_mix,
             w0, a0, k_k, k_a, r_k, ln_x_g, ln_x_b, conv_b]
    rep_m = [m_norm_mix_pre, m_norm_mix_post, m_norm_ffn_pre, m_norm_ffn_post, m_rel_bias, m_sinks,
             m_rwkv_shift_mix, m_w0, m_a0, m_k_k, m_k_a, m_r_k, m_ln_x_g, m_ln_x_b, m_conv_b]
    rep_v = [v_norm_mix_pre, v_norm_mix_post, v_norm_ffn_pre, v_norm_ffn_post, v_rel_bias, v_sinks,
             v_rwkv_shift_mix, v_w0, v_a0, v_k_k, v_k_a, v_r_k, v_ln_x_g, v_ln_x_b, v_conv_b]
    sh_w = [w_decay_up, w_iclr_up, w_gate_up, conv_w]
    sh_m = [m_w_decay_up, m_w_iclr_up, m_w_gate_up, m_conv_w]
    sh_v = [v_w_decay_up, v_w_iclr_up, v_w_gate_up, v_conv_w]
    sh_parts = [_to_slots(d_wd_pad[:HEAD_DIM], HEAD_DIM), _to_slots(d_wi_pad[HEAD_DIM:], HEAD_DIM),
                _to_slots(d_wg, HEAD_DIM), d_cw]
    n_rep, n_sh = len(small_rep), len(sh_parts)
    mine = [by_pair(_to_slots(d_w_in, D_IN // N_DEV)), by_pair(d_w_out.reshape(N_DEV, D_MODEL // N_DEV, D_MODEL)),
            *small_rep, *(by_pair(p) for p in sh_parts)]
    is_rep = [False, False] + [True] * n_rep + [False] * n_sh
    adam_down, swapped = adamw("adamw_w_ffn_down", w_ffn_down[0], got_down, m_w_ffn_down[0], v_w_ffn_down[0], 128,
                               pair_plan(mine, is_rep))
    chip_sums = [add_pair("pair_add_w_in", mine[0], swapped[0], BF16, 512),
                 add_pair("pair_add_w_out", mine[1], swapped[1], BF16, 128),
                 *add_pair_small(mine[2:], swapped[2:])]
    adam_up, got = adamw("adamw_w_ffn_up", w_ffn_up[0], got_up, m_w_ffn_up[0], v_w_ffn_up[0], 128,
                         chip_plan(chip_sums, is_rep))

    big = [adamw("adamw_w_in", w_in[0], got[0], m_w_in[0], v_w_in[0], 256),
           adamw("adamw_w_out", w_out[0], got[1], m_w_out[0], v_w_out[0], 128), adam_up, adam_down]
    small_w = rep_w + sh_w
    as_grad = lambda arrays: [a.reshape(g.shape[1:]) for a, g in zip(arrays, got[2:])]
    small = adamw_small(as_grad(small_w), got[2:], as_grad(rep_m + sh_m), as_grad(rep_v + sh_v))
    small = [[a.reshape(w.shape) for a, w in zip(kind, small_w)] for kind in small]

    names = ["norm_mix_pre", "norm_mix_post", "norm_ffn_pre", "norm_ffn_post", "w_in", "rel_bias", "sinks",
             "rwkv_shift_mix", "w0", "w_decay_up", "a0", "w_iclr_up", "w_gate_up", "k_k", "k_a", "r_k",
             "ln_x_g", "ln_x_b", "w_out", "w_ffn_up", "conv_w", "conv_b", "w_ffn_down"]
    small_names = ["norm_mix_pre", "norm_mix_post", "norm_ffn_pre", "norm_ffn_post", "rel_bias", "sinks",
                   "rwkv_shift_mix", "w0", "a0", "k_k", "k_a", "r_k", "ln_x_g", "ln_x_b", "conv_b",
                   "w_decay_up", "w_iclr_up", "w_gate_up", "conv_w"]
    big_names = {"w_in": 0, "w_out": 1, "w_ffn_up": 2, "w_ffn_down": 3}
    outs = []
    for kind in range(4):
        for nm in names:
            if nm in big_names:
                outs.append(big[big_names[nm]][kind][None])
            else:
                outs.append(small[kind][small_names.index(nm)])
    return (loss, grad_x, *outs)
```

```python
import functools
import math

import jax
import jax.numpy as jnp
from jax import lax
from jax.experimental import pallas as pl
from jax.experimental.pallas import tpu as pltpu

F32 = jnp.float32
BF16 = jnp.bfloat16

N_DEV = 8
SEQ = 2048
D_MODEL = 1024
HEAD_DIM = 64
D_ATTN = 512
D_KV = 128
D_RWKV = 512
N_HEADS = 8
RWKV_COLS = 1792
D_QKV = D_ATTN + 2 * D_KV
D_IN = D_QKV + RWKV_COLS
D_FF = 4096
BLOCK = 128
N_BLOCKS = SEQ // BLOCK
N_BUCKETS = 32
MAX_DISTANCE = 128
NORM_EPS = 1e-6
GN_EPS = 64e-5
NEG_INF = -1e30
CHUNK = 64
N_CHUNKS = SEQ // CHUNK
SCAN_GROUPS = 4
SCAN_WIDTH = D_RWKV // SCAN_GROUPS
TOK_TILE = 256
FF_TILE = 256
FF_ROW_CHUNK = 256
FF_HALO = 8
COL_TILE = 256
LANES = 128
VMEM_LIMIT = 56 * 1024 * 1024

ADAM_LR = 0.001
ADAM_B1 = 0.9
ADAM_B2 = 0.999
ADAM_EPS = 1e-08
ADAM_WD = 0.01
ADAM_STEP = 10

NT = ((1,), (1,))
TN = ((0,), (0,))
NN = ((1,), (0,))


def _sds(shape, dtype=F32):
    return jax.ShapeDtypeStruct(shape, dtype)


def _params(sem=None):
    if sem is None:
        return pltpu.CompilerParams(vmem_limit_bytes=VMEM_LIMIT)
    return pltpu.CompilerParams(dimension_semantics=sem, vmem_limit_bytes=VMEM_LIMIT)


def _dot(a, b, dims):
    return lax.dot_general(a, b, (dims, ((), ())), preferred_element_type=F32)


def _split2(x):
    hi = x.astype(BF16)
    return hi, (x - hi.astype(F32)).astype(BF16)


def _dot3_raw(a, b, dims):
    ah, al = _split2(a)
    bh, bl = _split2(b)
    return _dot(ah, bh, dims) + (_dot(al, bh, dims) + _dot(ah, bl, dims))


@functools.partial(jax.custom_vjp, nondiff_argnums=(2,))
def dot3(a, b, dims):
    return _dot3_raw(a, b, dims)


def _dot3_fwd(a, b, dims):
    return _dot3_raw(a, b, dims), (a, b)


def _dot3_bwd(dims, res, g):
    a, b = res
    if dims == NN:
        return dot3(g, b, NT), dot3(a, g, TN)
    if dims == NT:
        return dot3(g, b, NN), dot3(g, a, TN)
    return dot3(b, g, NT), dot3(a, g, NN)


dot3.defvjp(_dot3_fwd, _dot3_bwd)


@jax.custom_vjp
def mm(a, b):
    return _dot(a.astype(BF16), b.astype(BF16), NN)


def _mm_fwd(a, b):
    return mm(a, b), (a, b)


def _mm_bwd(res, g):
    a, b = res
    gb = g.astype(BF16)
    return _dot(gb, b.astype(BF16), NT).astype(a.dtype), _dot(a.astype(BF16), gb, TN).astype(b.dtype)


mm.defvjp(_mm_fwd, _mm_bwd)


@jax.custom_vjp
def mm_nt(a, b):
    return _dot(a.astype(BF16), b.astype(BF16), NT)


def _mm_nt_fwd(a, b):
    return mm_nt(a, b), (a, b)


def _mm_nt_bwd(res, g):
    a, b = res
    gb = g.astype(BF16)
    return _dot(gb, b.astype(BF16), NN).astype(a.dtype), _dot(gb, a.astype(BF16), TN).astype(b.dtype)


mm_nt.defvjp(_mm_nt_fwd, _mm_nt_bwd)


@jax.custom_vjp
def mmw(a, w, wz):
    return _dot(a.astype(BF16), w, NN)


def _mmw_fwd(a, w, wz):
    return mmw(a, w, wz), (a, w)


def _mmw_bwd(res, g):
    a, w = res
    gb = g.astype(BF16)
    return _dot(gb, w, NT).astype(a.dtype), jnp.zeros_like(w), _dot(a.astype(BF16), gb, TN)


mmw.defvjp(_mmw_fwd, _mmw_bwd)


def _shift_raw(x, n):
    rows = x.shape[0]
    rolled = pltpu.roll(x, n % rows, 0)
    idx = lax.broadcasted_iota(jnp.int32, x.shape, 0)
    keep = idx >= n if n > 0 else idx < rows + n
    return jnp.where(keep, rolled, 0.0)


@functools.partial(jax.custom_vjp, nondiff_argnums=(1,))
def shift_rows(x, n):
    return _shift_raw(x, n)


def _shift_fwd(x, n):
    return _shift_raw(x, n), None


def _shift_bwd(n, _, g):
    return (_shift_raw(g, -n),)


shift_rows.defvjp(_shift_fwd, _shift_bwd)


def _head_matrix(scale):
    a = lax.broadcasted_iota(jnp.int32, (D_RWKV, D_RWKV), 0) // HEAD_DIM
    b = lax.broadcasted_iota(jnp.int32, (D_RWKV, D_RWKV), 1) // HEAD_DIM
    return jnp.where(a == b, scale, 0.0).astype(F32)


def _rms(x, g):
    return x * lax.rsqrt(jnp.mean(x * x, axis=-1, keepdims=True) + NORM_EPS) * g


def _softplus(x):
    return jnp.maximum(x, 0.0) + jnp.log(1.0 + jnp.exp(-jnp.abs(x)))


def _tile_spec(arr, tm):
    return pl.BlockSpec((tm, arr.shape[1]), lambda i: (i, 0))


def _full_spec(arr):
    nd = arr.ndim
    return pl.BlockSpec(arr.shape, lambda i: (0,) * nd)


def tok_fwd(name, fn, tiles, params, zero_shapes, out_widths, out_dtypes, tm=TOK_TILE):
    n_t, n_p = len(tiles), len(params)

    def body(*refs):
        t_vals = [r[...] for r in refs[:n_t]]
        p_vals = [r[...] for r in refs[n_t:n_t + n_p]]
        z_vals = [jnp.zeros(s, F32) for s in zero_shapes]
        outs = fn(*t_vals, *p_vals, *z_vals)
        for r, o in zip(refs[n_t + n_p:], outs):
            r[...] = o.astype(r.dtype)

    rows = tiles[0].shape[0]
    return pl.pallas_call(
        body, grid=(rows // tm,), name=name,
        in_specs=[_tile_spec(t, tm) for t in tiles] + [_full_spec(p) for p in params],
        out_specs=[pl.BlockSpec((tm, w), lambda i: (i, 0)) for w in out_widths],
        out_shape=[_sds((rows, w), dt) for w, dt in zip(out_widths, out_dtypes)],
        compiler_params=_params(("arbitrary",)),
    )(*tiles, *params)


def tok_bwd(name, fn, tiles, params, zero_shapes, cots, diff_params, residuals=(), tm=TOK_TILE):
    cot_parts = [c if isinstance(c, tuple) else (c,) for c in cots]
    flat_cots = [a for part in cot_parts for a in part]
    residuals = dict(residuals)
    extra = [residuals[i] for i in sorted(residuals)]
    n_t, n_p, n_c, n_r = len(tiles), len(params), len(flat_cots), len(extra)
    acc_shapes = [params[i].shape for i in diff_params] + list(zero_shapes)

    def body(*refs):
        t_vals = [r[...].astype(F32) for r in refs[:n_t]]
        p_vals = [r[...] for r in refs[n_t:n_t + n_p]]
        flat = iter(r[...] for r in refs[n_t + n_p:n_t + n_p + n_c])
        c_vals = [functools.reduce(jnp.add, [next(flat) for _ in part]) for part in cot_parts]
        r_vals = dict(zip(sorted(residuals), (r[...] for r in refs[n_t + n_p + n_c:n_t + n_p + n_c + n_r])))
        out_refs = refs[n_t + n_p + n_c + n_r:]
        z_vals = [jnp.zeros(s, F32) for s in zero_shapes]
        d_vals = [p_vals[i] for i in diff_params]

        def f(t_in, d_in, z_in):
            full = list(p_vals)
            for i, v in zip(diff_params, d_in):
                full[i] = v
            return tuple(fn(*t_in, *full, *z_in))

        _, vjp = jax.vjp(f, t_vals, d_vals, z_vals)
        g_t, g_d, g_z = vjp(tuple(c_vals))
        for i, (r, g) in enumerate(zip(out_refs[:n_t], g_t)):
            r[...] = (g + r_vals[i] if i in r_vals else g).astype(r.dtype)
        acc_refs = out_refs[n_t:]

        @pl.when(pl.program_id(0) == 0)
        def _():
            for r in acc_refs:
                r[...] = jnp.zeros_like(r)

        for r, g in zip(acc_refs, list(g_d) + list(g_z)):
            r[...] += g

    rows = tiles[0].shape[0]
    return pl.pallas_call(
        body, grid=(rows // tm,), name=name,
        in_specs=[_tile_spec(t, tm) for t in tiles] + [_full_spec(p) for p in params]
        + [_tile_spec(c, tm) for c in flat_cots + extra],
        out_specs=[_tile_spec(t, tm) for t in tiles]
        + [pl.BlockSpec(s, lambda i, nd=len(s): (0,) * nd) for s in acc_shapes],
        out_shape=[_sds(t.shape) for t in tiles] + [_sds(s) for s in acc_shapes],
        compiler_params=_params(("arbitrary",)),
    )(*tiles, *params, *flat_cots, *extra)


def rms_tile(x, g):
    return (_rms(x, g),)


def rwkv_pre_tile(ps, w0, wd_pad, a0, wi_pad, wg, k_k, k_a):
    r = ps[:, 0:D_RWKV]
    k = ps[:, D_RWKV:2 * D_RWKV]
    v = ps[:, 2 * D_RWKV:3 * D_RWKV]
    z2 = ps[:, 3 * D_RWKV:3 * D_RWKV + LANES]
    zg = ps[:, 3 * D_RWKV + LANES:RWKV_COLS]
    w_log = -_softplus(-(w0 + mm(jnp.tanh(z2), wd_pad))) - 0.5
    lw = -jnp.exp(w_log)
    a = jax.nn.sigmoid(a0 + mm(z2, wi_pad))
    g = mm(jax.nn.sigmoid(zg), wg)
    kk = k * k_k
    norm = jnp.sqrt(dot3(kk * kk, _head_matrix(1.0), NN))
    kk = kk / jnp.maximum(norm, 1e-12)
    k2 = k * (1.0 + (a - 1.0) * k_a)
    return r, lw, k2, v, kk, a, g


def mix_out_tile(o, r, k2, v, g, attn, x, w_out, n_post, ln_g, ln_b, r_k, n_ffn_pre, wz):
    hmean = _head_matrix(1.0 / HEAD_DIM)
    d = o - dot3(o, hmean, NN)
    var = dot3(d * d, hmean, NN)
    on = d * lax.rsqrt(var + GN_EPS) * ln_g + ln_b
    bonus = dot3(r * k2 * r_k, _head_matrix(1.0), NN) * v
    rw = (on + bonus) * g
    mix = mmw(jnp.concatenate([attn, rw], axis=1), w_out, wz)
    x1 = x + _rms(mix, n_post)
    return x1, _rms(x1, n_ffn_pre)


def in_proj_fwd(h, w_in, mix_ext):
    def body(h_ref, w_ref, m_ref, proj_ref, ps_ref):
        p = _dot(h_ref[...], w_ref[...], NN)
        proj_ref[...] = p
        ps_ref[...] = p + (_shift_raw(p, 1) - p) * m_ref[...]

    n = D_IN // COL_TILE
    first = D_QKV // COL_TILE
    return pl.pallas_call(
        body, grid=(n,), name="in_proj_fwd",
        in_specs=[pl.BlockSpec((SEQ, D_MODEL), lambda j: (0, 0)),
                  pl.BlockSpec((D_MODEL, COL_TILE), lambda j: (0, j)),
                  pl.BlockSpec((1, COL_TILE), lambda j: (0, j))],
        out_specs=[pl.BlockSpec((SEQ, COL_TILE), lambda j: (0, j)),
                   pl.BlockSpec((SEQ, COL_TILE), lambda j: (0, jnp.maximum(j - first, 0)))],
        out_shape=[_sds((SEQ, D_IN)), _sds((SEQ, RWKV_COLS))],
        compiler_params=_params(("arbitrary",)),
    )(h, w_in, mix_ext)


def in_proj_bwd(h, w_in, mix_ext, proj, dpa):
    def body(h_ref, w_ref, m_ref, p_ref, d_ref, dh_ref, dw_ref, dm_ref):
        d = d_ref[...]
        p = p_ref[...]
        dm_ref[...] = jnp.sum(d * (_shift_raw(p, 1) - p), axis=0, keepdims=True)
        dmix = d * m_ref[...]
        dp = (d - dmix + _shift_raw(dmix, -1)).astype(BF16)
        dw_ref[...] = _dot(h_ref[...], dp, TN)

        @pl.when(pl.program_id(0) == 0)
        def _():
            dh_ref[...] = jnp.zeros_like(dh_ref)

        dh_ref[...] += _dot(dp, w_ref[...], NT)

    n = D_IN // COL_TILE
    col = lambda rows: pl.BlockSpec((rows, COL_TILE), lambda j: (0, j))
    return pl.pallas_call(
        body, grid=(n,), name="in_proj_bwd",
        in_specs=[pl.BlockSpec((SEQ, D_MODEL), lambda j: (0, 0)), col(D_MODEL), col(1), col(SEQ), col(SEQ)],
        out_specs=[pl.BlockSpec((SEQ, D_MODEL), lambda j: (0, 0)), col(D_MODEL), col(1)],
        out_shape=[_sds((SEQ, D_MODEL)), _sds((D_MODEL, D_IN)), _sds((1, D_IN))],
        compiler_params=_params(("arbitrary",)),
    )(h, w_in, mix_ext, proj, dpa)


def _bucket_table():
    rel = (jnp.arange(BLOCK)[:, None] + BLOCK) - jnp.arange(2 * BLOCK)[None, :]
    n = jnp.maximum(rel, 0)
    max_exact = N_BUCKETS // 2
    large = max_exact + (jnp.log(jnp.maximum(n, 1).astype(F32) / max_exact)
                         / math.log(MAX_DISTANCE / max_exact) * (N_BUCKETS - max_exact)).astype(jnp.int32)
    large = jnp.minimum(large, N_BUCKETS - 1)
    return jnp.where(n < max_exact, n, large).astype(jnp.int32)


def _select_matrix(g, o):
    a = lax.broadcasted_iota(jnp.int32, (D_KV, D_KV), 0)
    b = lax.broadcasted_iota(jnp.int32, (D_KV, D_KV), 1)
    return ((a - HEAD_DIM * g == b - o) & (b >= o) & (b < o + HEAD_DIM)).astype(F32)


def _attn_block(q, kp, kc, vp, vc, bias, sinks, block_idx):
    kb = jnp.concatenate([kp, kc], axis=0)
    vb = jnp.concatenate([vp, vc], axis=0)
    row = lax.broadcasted_iota(jnp.int32, (BLOCK, 2 * BLOCK), 0)
    col = lax.broadcasted_iota(jnp.int32, (BLOCK, 2 * BLOCK), 1)
    rel = row + BLOCK - col
    mask = (rel >= 0) & (rel < BLOCK) & (col + (block_idx - 1) * BLOCK >= 0)
    lane8 = lax.broadcasted_iota(jnp.int32, (1, N_HEADS), 1)
    kt, vt = {}, {}
    for g in range(2):
        for o in (0, HEAD_DIM):
            sel = _select_matrix(g, o)
            kt[g, o] = mm(kb, sel)
            vt[g, o] = mm(vb, sel)
    outs = []
    for j in range(D_ATTN // LANES):
        qs = q[:, j * LANES:(j + 1) * LANES]
        acc = None
        for half in range(2):
            hq = 2 * j + half
            g, o = hq // 4, half * HEAD_DIM
            s = mm_nt(qs, kt[g, o]) * (HEAD_DIM ** -0.5) + bias[hq]
            s = jnp.where(mask, s, NEG_INF)
            sink = jnp.sum(jnp.where(lane8 == hq, sinks, 0.0), axis=1, keepdims=True)
            m = lax.stop_gradient(jnp.maximum(jnp.max(s, axis=-1, keepdims=True), sink))
            p = jnp.exp(s - m)
            probs = p / (jnp.sum(p, axis=-1, keepdims=True) + jnp.exp(sink - m))
            part = mm(probs, vt[g, o])
            acc = part if acc is None else acc + part
        outs.append(acc)
    return jnp.concatenate(outs, axis=1)


def _build_bias(rb_ref, bucket, bias_ref):
    for hq in range(N_HEADS):
        acc = jnp.zeros((BLOCK, 2 * BLOCK), F32)
        for b in range(N_BUCKETS):
            acc = jnp.where(bucket == b, rb_ref[b, hq], acc)
        bias_ref[hq] = acc


def _attn_in_specs():
    prev = lambda n: jnp.maximum(n - 1, 0)
    return [pl.BlockSpec((BLOCK, D_ATTN), lambda n: (n, 0)),
            pl.BlockSpec((BLOCK, D_KV), lambda n: (prev(n), D_ATTN // D_KV)),
            pl.BlockSpec((BLOCK, D_KV), lambda n: (n, D_ATTN // D_KV)),
            pl.BlockSpec((BLOCK, D_KV), lambda n: (prev(n), D_ATTN // D_KV + 1)),
            pl.BlockSpec((BLOCK, D_KV), lambda n: (n, D_ATTN // D_KV + 1)),
            pl.BlockSpec(memory_space=pltpu.SMEM),
            pl.BlockSpec((BLOCK, 2 * BLOCK), lambda n: (0, 0)),
            pl.BlockSpec((1, N_HEADS), lambda n: (0, 0))]


def attn_fwd(proj, rel_bias, bucket, sinks, plan):
    def body(q_ref, kp_ref, kc_ref, vp_ref, vc_ref, rb_ref, bk_ref, sk_ref, o_ref, bias_ref):
        n = pl.program_id(0)

        @pl.when(n == 0)
        def _():
            _build_bias(rb_ref, bk_ref[...], bias_ref)

        o_ref[...] = _attn_block(q_ref[...], kp_ref[...], kc_ref[...], vp_ref[...], vc_ref[...],
                                 tuple(bias_ref[h] for h in range(N_HEADS)), sk_ref[...], n)

    (attn,), gathered = call_with_comm(
        plan, 3 * N_BLOCKS // 4, body, (N_BLOCKS,), "attn_fwd", _attn_in_specs(),
        [pl.BlockSpec((BLOCK, D_ATTN), lambda n: (n, 0))], [_sds((SEQ, D_ATTN))],
        [pltpu.VMEM((N_HEADS, BLOCK, 2 * BLOCK), F32)], (proj, proj, proj, proj, proj, rel_bias, bucket, sinks))
    return attn, gathered


def attn_bwd(proj, rel_bias, bucket, sinks, d_attn):
    def body(q_ref, kp_ref, kc_ref, vp_ref, vc_ref, rb_ref, bk_ref, sk_ref, do_ref,
             dq_ref, dkc_ref, dkp_ref, dvc_ref, dvp_ref, drb_ref, dsk_ref, bias_ref, dbias_ref):
        n = pl.program_id(0)

        @pl.when(n == 0)
        def _():
            _build_bias(rb_ref, bk_ref[...], bias_ref)
            dbias_ref[...] = jnp.zeros_like(dbias_ref)
            dsk_ref[...] = jnp.zeros_like(dsk_ref)

        f = lambda q, kp, kc, vp, vc, bias, sk: _attn_block(q, kp, kc, vp, vc, bias, sk, n)
        _, vjp = jax.vjp(f, q_ref[...], kp_ref[...], kc_ref[...], vp_ref[...], vc_ref[...],
                         tuple(bias_ref[h] for h in range(N_HEADS)), sk_ref[...])
        dq, dkp, dkc, dvp, dvc, dbias, dsk = vjp(do_ref[...])
        dq_ref[...] = dq
        dkc_ref[...] = dkc
        dkp_ref[...] = dkp
        dvc_ref[...] = dvc
        dvp_ref[...] = dvp
        for h in range(N_HEADS):
            dbias_ref[h] += dbias[h]
        dsk_ref[...] += dsk

        @pl.when(n == N_BLOCKS - 1)
        def _():
            bucket_v = bk_ref[...]
            rowi = lax.broadcasted_iota(jnp.int32, (N_BUCKETS, 2 * BLOCK), 0)
            lane = lax.broadcasted_iota(jnp.int32, (N_BUCKETS, N_HEADS), 1)
            out = jnp.zeros((N_BUCKETS, N_HEADS), F32)
            for hq in range(N_HEADS):
                dbh = dbias_ref[hq]
                rows = jnp.zeros((N_BUCKETS, 2 * BLOCK), F32)
                for b in range(N_BUCKETS):
                    part = jnp.sum(jnp.where(bucket_v == b, dbh, 0.0), axis=0, keepdims=True)
                    rows = jnp.where(rowi == b, part, rows)
                tot = jnp.sum(rows, axis=1, keepdims=True)
                out = jnp.where(lane == hq, tot, out)
            drb_ref[...] = out

    blk = lambda w: pl.BlockSpec((BLOCK, w), lambda n: (n, 0))
    return pl.pallas_call(
        body, grid=(N_BLOCKS,), name="attn_bwd",
        in_specs=_attn_in_specs() + [blk(D_ATTN)],
        out_specs=[blk(D_ATTN), blk(D_KV), blk(D_KV), blk(D_KV), blk(D_KV),
                   pl.BlockSpec((N_BUCKETS, N_HEADS), lambda n: (0, 0)),
                   pl.BlockSpec((1, N_HEADS), lambda n: (0, 0))],
        out_shape=[_sds((SEQ, D_ATTN)), _sds((SEQ, D_KV)), _sds((SEQ, D_KV)), _sds((SEQ, D_KV)),
                   _sds((SEQ, D_KV)), _sds((N_BUCKETS, N_HEADS)), _sds((1, N_HEADS))],
        scratch_shapes=[pltpu.VMEM((N_HEADS, BLOCK, 2 * BLOCK), F32),
                        pltpu.VMEM((N_HEADS, BLOCK, 2 * BLOCK), F32)],
        compiler_params=_params(("arbitrary",)),
    )(proj, proj, proj, proj, proj, rel_bias, bucket, sinks, d_attn)


def _stack(x, size):
    groups = x.shape[1] // size
    lane = lax.broadcasted_iota(jnp.int32, x.shape, 1) // size
    return jnp.concatenate([jnp.where(lane == i, x, 0.0) for i in range(groups)], axis=0)


def _neumann(l):
    c = CHUNK
    t = lax.broadcasted_iota(jnp.int32, l.shape, 0)
    i = lax.broadcasted_iota(jnp.int32, l.shape, 1) % c
    inv = (i == t).astype(F32) + l
    pw = dot3(l, _stack(l, c), NN)
    for _ in range(int(math.log2(c)) - 2):
        both = dot3(jnp.concatenate([inv, pw], axis=0), _stack(pw, c), NN)
        inv = inv + both[:c]
        pw = both[c:]
    return inv + dot3(inv, _stack(pw, c), NN)


@jax.custom_vjp
def neumann_inv(l):
    return _neumann(l)


def _neumann_fwd(l):
    inv = _neumann(l)
    return inv, inv


def _neumann_bwd(inv, g):
    c = CHUNK
    bd_t = _stack(inv, c).T
    inv_t = bd_t[0:c]
    for h in range(1, inv.shape[1] // c):
        inv_t = inv_t + bd_t[h * c:(h + 1) * c]
    return (dot3(dot3(inv_t, _stack(g, c), NN), bd_t, NN),)


neumann_inv.defvjp(_neumann_fwd, _neumann_bwd)


def _cumsum_raw(x, dims):
    c = x.shape[0]
    tt = lax.broadcasted_iota(jnp.int32, (c, c), 0)
    ii = lax.broadcasted_iota(jnp.int32, (c, c), 1)
    tri = (ii <= tt).astype(BF16)
    hi = x.astype(BF16)
    rest = x - hi.astype(F32)
    mid = rest.astype(BF16)
    lo = (rest - mid.astype(F32)).astype(BF16)
    return _dot(tri, hi, dims) + (_dot(tri, mid, dims) + _dot(tri, lo, dims))


@jax.custom_vjp
def cumsum_rows(x):
    return _cumsum_raw(x, NN)


def _cumsum_fwd(x):
    return _cumsum_raw(x, NN), None


def _cumsum_bwd(_, g):
    return (_cumsum_raw(g, TN),)


cumsum_rows.defvjp(_cumsum_fwd, _cumsum_bwd)


def _rwkv_chunk(s0, r, lw, k, v, kk, a):
    heads = r.shape[1] // HEAD_DIM
    c, hc = CHUNK, heads * CHUNK
    t = lax.broadcasted_iota(jnp.int32, (c, hc), 0)
    i = lax.broadcasted_iota(jnp.int32, (c, hc), 1) % c
    strict, incl = i < t, i <= t
    stack = lambda x: _stack(x, HEAD_DIM)
    ba = lax.broadcasted_iota(jnp.int32, s0.shape, 0) // HEAD_DIM
    bb = lax.broadcasted_iota(jnp.int32, s0.shape, 1) // HEAD_DIM
    blocks = (ba == bb).astype(F32)

    cum = cumsum_rows(lw)
    cum_end = jnp.sum(lw, axis=0, keepdims=True)
    beta = kk * a
    al = -kk * jnp.exp(cum - lw)
    p_inv = jnp.exp(-cum)
    be, kb, rb = beta * p_inv, k * p_inv, r * jnp.exp(cum)
    ar = jnp.concatenate([al, rb], axis=0)
    sv = stack(v)
    l_all = dot3(ar, jnp.concatenate([stack(be), stack(kb)], axis=0), NT)
    l_ab = jnp.where(strict, l_all[:c, :hc], 0.0)
    l_ak = jnp.where(strict, l_all[:c, hc:], 0.0)
    l_rb = jnp.where(incl, l_all[c:, :hc], 0.0)
    l_rk = jnp.where(incl, l_all[c:, hc:], 0.0)
    inv = neumann_inv(l_ab)
    from_s0 = dot3(ar, s0, NT)
    from_v = dot3(jnp.concatenate([l_ak, l_rk], axis=0), sv, NN)
    u = dot3(inv, stack(from_s0[:c] + from_v[:c]), NN)
    o = from_s0[c:] + from_v[c:] + dot3(l_rb, stack(u), NN)
    to_end = jnp.exp(cum_end - cum)
    s1 = s0 * jnp.exp(cum_end) + blocks * dot3(
        jnp.concatenate([u, v], axis=0), jnp.concatenate([beta * to_end, k * to_end], axis=0), TN)
    return o, s1


def call_with_comm(plan, middle_step, body, grid, name, in_specs, out_specs, out_shape, scratch_shapes, operands):
    n_in, n_out, n_scr = len(in_specs), len(out_specs), len(scratch_shapes)
    p_in, p_out = len(plan.ins), len(plan.out_shape)

    def fused(*refs):
        refs = list(refs)
        ins, refs = refs[:n_in], refs[n_in:]
        p_ins, refs = refs[:p_in], refs[p_in:]
        outs, refs = refs[:n_out], refs[n_out:]
        p_outs, refs = refs[:p_out], refs[p_out:]
        scr, p_sems = refs[:n_scr], refs[n_scr:]
        start, middle, finish = plan.stages(p_ins, p_outs, p_sems)
        step = pl.program_id(0)
        pl.when(step == 0)(start)
        body(*ins, *outs, *scr)
        pl.when(step == middle_step)(middle)
        pl.when(step == grid[0] - 1)(finish)

    any_spec = pl.BlockSpec(memory_space=pl.ANY)
    res = pl.pallas_call(
        fused, grid=grid, name=name,
        in_specs=list(in_specs) + [any_spec] * p_in, out_specs=list(out_specs) + [any_spec] * p_out,
        out_shape=list(out_shape) + list(plan.out_shape), scratch_shapes=list(scratch_shapes) + list(plan.scratch),
        compiler_params=_params(("arbitrary",)),
    )(*operands, *plan.ins)
    return res[:n_out], res[n_out:]


def _by_group(ref):
    return jnp.stack([ref[:, g * SCAN_WIDTH:(g + 1) * SCAN_WIDTH] for g in range(SCAN_GROUPS)])


def _store_groups(ref, val):
    for g in range(SCAN_GROUPS):
        ref[:, g * SCAN_WIDTH:(g + 1) * SCAN_WIDTH] = val[g]


def rwkv_scan_fwd(r, lw, k, v, kk, a, plan):
    def body(r_ref, lw_ref, k_ref, v_ref, kk_ref, a_ref, o_ref, st_ref, s_ref):
        @pl.when(pl.program_id(0) == 0)
        def _():
            s_ref[...] = jnp.zeros_like(s_ref)

        s0 = s_ref[...]
        st_ref[0] = s0
        o, s1 = jax.vmap(_rwkv_chunk)(s0, *(_by_group(ref) for ref in (r_ref, lw_ref, k_ref, v_ref, kk_ref, a_ref)))
        _store_groups(o_ref, o)
        s_ref[...] = s1

    tb = pl.BlockSpec((CHUNK, D_RWKV), lambda c: (c, 0))
    state = (SCAN_GROUPS, SCAN_WIDTH, SCAN_WIDTH)
    return call_with_comm(
        plan, 3 * N_CHUNKS // 4, body, (N_CHUNKS,), "rwkv_scan_fwd",
        [tb] * 6, [tb, pl.BlockSpec((1,) + state, lambda c: (c, 0, 0, 0))],
        [_sds((SEQ, D_RWKV)), _sds((N_CHUNKS,) + state)], [pltpu.VMEM(state, F32)], (r, lw, k, v, kk, a))


def rwkv_scan_bwd(r, lw, k, v, kk, a, states, d_o, plan):
    def body(r_ref, lw_ref, k_ref, v_ref, kk_ref, a_ref, st_ref, do_ref,
             dr_ref, dlw_ref, dk_ref, dv_ref, dkk_ref, da_ref, ds_ref):
        @pl.when(pl.program_id(0) == 0)
        def _():
            ds_ref[...] = jnp.zeros_like(ds_ref)

        _, vjp = jax.vjp(jax.vmap(_rwkv_chunk), st_ref[0],
                         *(_by_group(ref) for ref in (r_ref, lw_ref, k_ref, v_ref, kk_ref, a_ref)))
        grads = vjp((_by_group(do_ref), ds_ref[...]))
        ds_ref[...] = grads[0]
        for ref, val in zip((dr_ref, dlw_ref, dk_ref, dv_ref, dkk_ref, da_ref), grads[1:]):
            _store_groups(ref, val)

    last = N_CHUNKS - 1
    tb = pl.BlockSpec((CHUNK, D_RWKV), lambda c: (last - c, 0))
    state = (SCAN_GROUPS, SCAN_WIDTH, SCAN_WIDTH)
    return call_with_comm(
        plan, N_CHUNKS // 4, body, (N_CHUNKS,), "rwkv_scan_bwd",
        [tb] * 6 + [pl.BlockSpec((1,) + state, lambda c: (last - c, 0, 0, 0)), tb], [tb] * 6,
        [_sds((SEQ, D_RWKV))] * 6, [pltpu.VMEM(state, F32)], (r, lw, k, v, kk, a, states, d_o))


def _ffn_mid(ug, uv, cg, cv, bg, bv):
    conv_g = bg + cg[0] * shift_rows(ug, 2) + cg[1] * shift_rows(ug, 1) + cg[2] * ug
    conv_v = bv + cv[0] * shift_rows(uv, 2) + cv[1] * shift_rows(uv, 1) + cv[2] * uv
    return jax.nn.gelu(conv_g, approximate=True) * conv_v


def _conv_rows(ref):
    return tuple(ref[0, j:j + 1, :] for j in range(3))


def _ffn_specs(tile):
    per = D_MODEL // tile
    half = N_DEV // 2
    w_g = pl.BlockSpec((1, D_MODEL, tile), lambda t: (t // per, 0, t % per))
    w_v = pl.BlockSpec((1, D_MODEL, tile), lambda t: (half + t // per, 0, t % per))
    c_g = pl.BlockSpec((1, 3, tile), lambda t: (t // per, 0, t % per))
    c_v = pl.BlockSpec((1, 3, tile), lambda t: (half + t // per, 0, t % per))
    b_g = pl.BlockSpec((1, tile), lambda t: (0, t))
    b_v = pl.BlockSpec((1, tile), lambda t: (0, D_FF // tile + t))
    w_d = pl.BlockSpec((tile, D_MODEL), lambda t: (t, 0))
    return w_g, w_v, c_g, c_v, b_g, b_v, w_d


def ffn_fwd(h2, w_up, conv_w, conv_b, w_down):
    def body(h_ref, wg_ref, wv_ref, cg_ref, cv_ref, bg_ref, bv_ref, wd_ref, f_ref):
        @pl.when(pl.program_id(0) == 0)
        def _():
            f_ref[...] = jnp.zeros_like(f_ref)

        h = h_ref[...]
        act = _ffn_mid(_dot(h, wg_ref[0], NN), _dot(h, wv_ref[0], NN), _conv_rows(cg_ref), _conv_rows(cv_ref),
                       bg_ref[...], bv_ref[...])
        f_ref[...] += _dot(act.astype(BF16), wd_ref[...], NN)

    full = pl.BlockSpec((SEQ, D_MODEL), lambda t: (0, 0))
    return pl.pallas_call(
        body, grid=(D_FF // FF_TILE,), name="ffn_fwd",
        in_specs=[full, *_ffn_specs(FF_TILE)],
        out_specs=full, out_shape=_sds((SEQ, D_MODEL)),
        compiler_params=_params(("arbitrary",)),
    )(h2, w_up, w_up, conv_w, conv_w, conv_b, conv_b, w_down)


def ffn_bwd_mid(h2, w_up, conv_w, conv_b, w_down, df):
    tile, rows, halo = FF_TILE, FF_ROW_CHUNK, FF_HALO
    ext = rows + 2 * halo

    def body(h_hbm, wg_ref, wv_ref, cg_ref, cv_ref, bg_ref, bv_ref, wd_ref, df_hbm,
             dug_ref, duv_ref, dcg_ref, dcv_ref, dbg_ref, dbv_ref, dwd_ref,
             h_ref, df_ref, ug_ref, uv_ref, da_ref, act_ref):
        @pl.when(pl.program_id(0) == 0)
        def _():
            pltpu.sync_copy(h_hbm, h_ref)
            pltpu.sync_copy(df_hbm, df_ref)
            for ref in (ug_ref, uv_ref, da_ref):
                ref[0:halo, :] = jnp.zeros((halo, tile), F32)
                ref[halo + SEQ:, :] = jnp.zeros((halo, tile), F32)

        h, df_b = h_ref[...], df_ref[...]
        ug_ref[halo:halo + SEQ, :] = _dot(h, wg_ref[0], NN)
        uv_ref[halo:halo + SEQ, :] = _dot(h, wv_ref[0], NN)
        da_ref[halo:halo + SEQ, :] = _dot(df_b, wd_ref[...], NT)
        cg, cv, bg, bv = _conv_rows(cg_ref), _conv_rows(cv_ref), bg_ref[...], bv_ref[...]
        down = lambda x, n: pltpu.roll(x, n, 0)
        up = lambda x, n: pltpu.roll(x, ext - n, 0)
        mid = slice(halo, halo + rows)

        def chunk(i, sums):
            r0 = pl.multiple_of(i * rows, rows)
            window = pl.ds(r0, ext)
            ug, uv, da = ug_ref[window, :], uv_ref[window, :], da_ref[window, :]
            ug1, ug2, uv1, uv2 = down(ug, 1), down(ug, 2), down(uv, 1), down(uv, 2)
            conv_g = bg + cg[0] * ug2 + cg[1] * ug1 + cg[2] * ug
            conv_v = bv + cv[0] * uv2 + cv[1] * uv1 + cv[2] * uv
            act, vjp = jax.vjp(lambda a, b: jax.nn.gelu(a, approximate=True) * b, conv_g, conv_v)
            dcg, dcv = vjp(da)
            dug = cg[2] * dcg + cg[1] * up(dcg, 1) + cg[0] * up(dcg, 2)
            duv = cv[2] * dcv + cv[1] * up(dcv, 1) + cv[0] * up(dcv, 2)
            out = pl.ds(r0, rows)
            act_ref[out, :] = act[mid].astype(BF16)
            dug_ref[out, :] = dug[mid].astype(BF16)
            duv_ref[out, :] = duv[mid].astype(BF16)
            col = lambda x: jnp.sum(x[mid], axis=0, keepdims=True)
            new = (col(dcg * ug2), col(dcg * ug1), col(dcg * ug), col(dcv * uv2), col(dcv * uv1), col(dcv * uv),
                   col(dcg), col(dcv))
            return tuple(s + n for s, n in zip(sums, new))

        zero = jnp.zeros((1, tile), F32)
        sums = lax.fori_loop(0, SEQ // rows, chunk, (zero,) * 8)
        for j in range(3):
            dcg_ref[0, j:j + 1, :] = sums[j]
            dcv_ref[0, j:j + 1, :] = sums[3 + j]
        dbg_ref[...] = sums[6]
        dbv_ref[...] = sums[7]
        dwd_ref[...] = _dot(act_ref[...], df_b, TN).astype(BF16)

    hbm = pl.BlockSpec(memory_space=pl.ANY)
    w_g, w_v, c_g, c_v, b_g, b_v, w_d = _ffn_specs(tile)
    col = pl.BlockSpec((SEQ, tile), lambda t: (0, t))
    padded = pltpu.VMEM((SEQ + 2 * halo, tile), F32)
    return pl.pallas_call(
        body, grid=(D_FF // tile,), name="ffn_bwd_mid",
        in_specs=[hbm, w_g, w_v, c_g, c_v, b_g, b_v, w_d, hbm],
        out_specs=[col, col, c_g, c_v, b_g, b_v, w_d],
        out_shape=[_sds((SEQ, D_FF), BF16), _sds((SEQ, D_FF), BF16), _sds((N_DEV, 3, D_MODEL)),
                   _sds((N_DEV, 3, D_MODEL)), _sds((1, 2 * D_FF)), _sds((1, 2 * D_FF)), _sds((D_FF, D_MODEL), BF16)],
        scratch_shapes=[pltpu.VMEM((SEQ, D_MODEL), BF16), pltpu.VMEM((SEQ, D_MODEL), BF16), padded, padded, padded,
                        pltpu.VMEM((SEQ, tile), BF16)],
        compiler_params=_params(("arbitrary",)),
    )(h2, w_up, w_up, conv_w, conv_w, conv_b, conv_b, w_down, df)


def ffn_bwd_up(h2, w_up, dug, duv):
    tile = FF_TILE
    per = D_MODEL // tile

    def body(h_hbm, wg_ref, wv_ref, dug_ref, duv_ref, dh_hbm, dup_hbm, h_ref, dh_ref, dwg_ref, dwv_ref, sem, up_sems):
        t = pl.program_id(0)

        @pl.when(t == 0)
        def _():
            pltpu.sync_copy(h_hbm, h_ref)
            dh_ref[...] = jnp.zeros_like(dh_ref)

        h, dug_b, duv_b = h_ref[...], dug_ref[...], duv_ref[...]
        cols = pl.ds(pl.multiple_of((t % per) * tile, tile), tile)
        to_gate = pltpu.make_async_copy(dwg_ref, dup_hbm.at[t // per, :, cols], up_sems.at[0])
        to_value = pltpu.make_async_copy(dwv_ref, dup_hbm.at[N_DEV // 2 + t // per, :, cols], up_sems.at[1])
        dwg_ref[...] = _dot(h, dug_b, TN).astype(BF16)
        to_gate.start()
        dwv_ref[...] = _dot(h, duv_b, TN).astype(BF16)
        to_value.start()
        dh_ref[...] += _dot(jnp.concatenate([dug_b, duv_b], axis=1),
                            jnp.concatenate([wg_ref[0], wv_ref[0]], axis=1), NT)
        to_gate.wait()
        to_value.wait()

        @pl.when(t == D_FF // tile - 1)
        def _():
            cp = pltpu.make_async_copy(dh_ref, dh_hbm, sem)
            cp.start()
            cp.wait()

    hbm = pl.BlockSpec(memory_space=pl.ANY)
    w_g, w_v = _ffn_specs(tile)[:2]
    col = pl.BlockSpec((SEQ, tile), lambda t: (0, t))
    return pl.pallas_call(
        body, grid=(D_FF // tile,), name="ffn_bwd_up",
        in_specs=[hbm, w_g, w_v, col, col], out_specs=[hbm, hbm],
        out_shape=[_sds((SEQ, D_MODEL)), _sds((N_DEV, D_MODEL, D_MODEL), BF16)],
        scratch_shapes=[pltpu.VMEM((SEQ, D_MODEL), BF16), pltpu.VMEM((SEQ, D_MODEL), F32),
                        pltpu.VMEM((D_MODEL, tile), BF16), pltpu.VMEM((D_MODEL, tile), BF16),
                        pltpu.SemaphoreType.DMA, pltpu.SemaphoreType.DMA((2,))],
        compiler_params=_params(("arbitrary",)),
    )(h2, w_up, w_up, dug, duv)


def loss_head(x1, f, target, n_post):
    def tile_loss(x1_t, f_t, g, tgt):
        err = x1_t + _rms(f_t, g) - tgt
        return 0.5 * jnp.sum(jnp.mean(err * err, axis=-1))

    def body(x_ref, f_ref, t_ref, g_ref, dx_ref, df_ref, dg_ref, loss_ref):
        val, (dx, df, dg) = jax.value_and_grad(tile_loss, argnums=(0, 1, 2))(
            x_ref[...], f_ref[...], g_ref[...], t_ref[...])
        dx_ref[...] = dx
        df_ref[...] = df.astype(BF16)

        @pl.when(pl.program_id(0) == 0)
        def _():
            dg_ref[...] = jnp.zeros_like(dg_ref)
            loss_ref[...] = jnp.zeros_like(loss_ref)

        dg_ref[...] += dg
        loss_ref[...] += jnp.full((1, LANES), val, F32)

    tile = pl.BlockSpec((TOK_TILE, D_MODEL), lambda i: (i, 0))
    vec = pl.BlockSpec((1, D_MODEL), lambda i: (0, 0))
    return pl.pallas_call(
        body, grid=(SEQ // TOK_TILE,), name="loss_head",
        in_specs=[tile, tile, tile, vec],
        out_specs=[tile, tile, vec, pl.BlockSpec((1, LANES), lambda i: (0, 0))],
        out_shape=[_sds((SEQ, D_MODEL)), _sds((SEQ, D_MODEL), BF16), _sds((1, D_MODEL)), _sds((1, LANES))],
        compiler_params=_params(("arbitrary",)),
    )(x1, f, target, n_post)


def _mesh_pos():
    return lax.axis_index("x"), lax.axis_index("y"), lax.axis_index("c")


def _flip(pos, rel):
    x, y, c = pos
    return (1 - x if rel & 4 else x, 1 - y if rel & 2 else y, 1 - c if rel & 1 else c)


def _slot(pos):
    x, y, c = pos
    return 4 * x + 2 * y + c


def cast_bf16(w, rows):
    def body(w_ref, o_ref):
        o_ref[...] = w_ref[...].astype(BF16)

    spec = pl.BlockSpec((rows, w.shape[1]), lambda i: (i, 0))
    return pl.pallas_call(body, grid=(w.shape[0] // rows,), name="cast_bf16_%dx%d" % w.shape,
                          in_specs=[spec], out_specs=spec, out_shape=_sds(w.shape, BF16),
                          compiler_params=_params(("arbitrary",)))(w)


class CommPlan:
    def __init__(self, ins, out_shape, scratch, stages):
        self.ins, self.out_shape, self.scratch, self.stages = ins, out_shape, scratch, stages


def run_comm(name, plan):
    n_in, n_out = len(plan.ins), len(plan.out_shape)

    def body(*refs):
        for stage in plan.stages(refs[:n_in], refs[n_in:n_in + n_out], refs[n_in + n_out:]):
            stage()

    any_spec = pl.BlockSpec(memory_space=pl.ANY)
    return pl.pallas_call(
        body, name=name, in_specs=[any_spec] * len(plan.ins), out_specs=[any_spec] * len(plan.out_shape),
        out_shape=plan.out_shape, scratch_shapes=plan.scratch)(*plan.ins)


def gather_plan(shards):
    n = len(shards)

    def stages(srcs, outs, sems):
        send_sems, recv_sems, local_sems = sems

        def places():
            me = _mesh_pos()
            return me, _flip(me, 1), [_flip(me, 2), _flip(me, 4), _flip(me, 6)]

        def copy(a, k, block, to, src=None):
            dst = outs[a].at[_slot(block)]
            return pltpu.make_async_remote_copy(
                src_ref=dst if src is None else src, dst_ref=dst,
                send_sem=send_sems.at[7 * a + k], recv_sem=recv_sems.at[7 * a + k],
                device_id=to, device_id_type=pl.DeviceIdType.MESH)

        def local(a, me):
            return pltpu.make_async_copy(srcs[a], outs[a].at[_slot(me)], local_sems.at[a])

        def own(a, me, sibling, chips):
            return [copy(a, 0, me, sibling, src=srcs[a])] + [
                copy(a, 1 + j, me, chip, src=srcs[a]) for j, chip in enumerate(chips)]

        def start():
            me, sibling, chips = places()
            for a in range(n):
                local(a, me).start()
                for cp in own(a, me, sibling, chips):
                    cp.start()

        def forward():
            me, sibling, chips = places()
            for j, chip in enumerate(chips):
                for a in range(n):
                    copy(a, 1 + j, chip, me).wait_recv()
                    copy(a, 4 + j, chip, sibling).start()

        def finish():
            me, sibling, chips = places()
            for a in range(n):
                copy(a, 0, sibling, me).wait_recv()
                for j, chip in enumerate(chips):
                    copy(a, 4 + j, _flip(chip, 1), me).wait_recv()
            for a in range(n):
                for cp in own(a, me, sibling, chips):
                    cp.wait_send()
                for j, chip in enumerate(chips):
                    copy(a, 4 + j, chip, sibling).wait_send()
                local(a, me).wait()

        return start, forward, finish

    return CommPlan(list(shards), [_sds((N_DEV,) + s.shape, s.dtype) for s in shards],
                    [pltpu.SemaphoreType.DMA((7 * n,)), pltpu.SemaphoreType.DMA((7 * n,)),
                     pltpu.SemaphoreType.DMA((n,))], stages)


def exchange_plan(parts, replicated, rels, members, index, member_axis, own_copy):
    n, nr = len(parts), len(rels)
    pick_index = (slice(None),) * member_axis + (0,)
    subs = [1 if (r or member_axis == 0) else p.shape[0] for p, r in zip(parts, replicated)]
    first = [sum(subs[:a]) for a in range(n)]
    total = sum(subs)

    def stages(srcs, outs, sems):
        send_sems, recv_sems, local_sems = sems

        def src(a, s, pos):
            if replicated[a]:
                return srcs[a]
            return srcs[a].at[index(pos)] if member_axis == 0 else srcs[a].at[s, index(pos)]

        def dst(a, s, pos):
            block = outs[a].at[index(pos)]
            return block if (replicated[a] or member_axis == 0) else block.at[s]

        def copy(a, s, j, me, src_pos, dst_pos):
            sem = nr * (first[a] + s) + j
            return pltpu.make_async_remote_copy(
                src_ref=src(a, s, src_pos), dst_ref=dst(a, s, dst_pos),
                send_sem=send_sems.at[sem], recv_sem=recv_sems.at[sem],
                device_id=_flip(me, rels[j]), device_id_type=pl.DeviceIdType.MESH)

        pieces = [(a, s) for a in range(n) for s in range(subs[a])]

        def local(a, s, me):
            return pltpu.make_async_copy(src(a, s, me), dst(a, s, me), local_sems.at[first[a] + s])

        def sends(me):
            return [copy(a, s, j, me, _flip(me, rels[j]), me) for j in range(nr) for a, s in pieces]

        own = pieces if own_copy else []

        def start():
            me = _mesh_pos()
            for cp in sends(me) + [local(a, s, me) for a, s in own]:
                cp.start()

        def middle():
            pass

        def finish():
            me = _mesh_pos()
            for j in range(nr):
                for a, s in pieces:
                    copy(a, s, j, me, me, _flip(me, rels[j])).wait_recv()
            for cp in sends(me):
                cp.wait_send()
            for a, s in own:
                local(a, s, me).wait()

        return start, middle, finish

    shapes = [p.shape if r else jax.eval_shape(lambda t: t[pick_index], p).shape for p, r in zip(parts, replicated)]
    return CommPlan(list(parts), [_sds((members,) + s, p.dtype) for s, p in zip(shapes, parts)],
                    [pltpu.SemaphoreType.DMA((nr * total,)), pltpu.SemaphoreType.DMA((nr * total,)),
                     pltpu.SemaphoreType.DMA((total,))], stages)


def pair_plan(parts, replicated):
    return exchange_plan(parts, replicated, [1], 2, lambda pos: pos[2], 1, False)


def chip_plan(parts, replicated):
    return exchange_plan(parts, replicated, [2, 4, 6], 4, lambda pos: 2 * pos[0] + pos[1], 0, True)


def add_pair(name, mine, swapped, out_dtype, rows):
    def body(m_ref, s_ref, o_ref):
        own = m_ref[0, 0] if mine.ndim == 4 else m_ref[0]
        o_ref[0] = (own.astype(F32) + s_ref[0, 0].astype(F32)).astype(o_ref.dtype)

    _, n, r, c = swapped.shape
    core = lambda: lax.axis_index("c")
    if mine.ndim == 4:
        mine_spec = pl.BlockSpec((1, 1, rows, c), lambda i, j: (i, core(), j, 0))
    else:
        mine_spec = pl.BlockSpec((1, rows, c), lambda i, j: (i, j, 0))
    return pl.pallas_call(
        body, grid=(n, r // rows), name=name,
        in_specs=[mine_spec, pl.BlockSpec((1, 1, rows, c), lambda i, j: (1 - core(), i, j, 0))],
        out_specs=pl.BlockSpec((1, rows, c), lambda i, j: (i, j, 0)),
        out_shape=_sds((n, r, c), out_dtype),
        compiler_params=_params(("arbitrary", "arbitrary")),
    )(mine, swapped)


def add_pair_small(mines, swappeds):
    n = len(mines)
    halves = [m.ndim == s.ndim for m, s in zip(mines, swappeds)]

    def body(*refs):
        c = lax.axis_index("c")
        for i in range(n):
            m_ref, s_ref, o_ref = refs[i], refs[n + i], refs[2 * n + i]
            o_ref[...] = (m_ref[:, c] if halves[i] else m_ref[...]) + s_ref[1 - c]

    vmem = pl.BlockSpec(memory_space=pltpu.VMEM)
    return pl.pallas_call(
        body, name="pair_add_small", in_specs=[vmem] * (2 * n), out_specs=[vmem] * n,
        out_shape=[_sds(s.shape[1:]) for s in swappeds], compiler_params=_params(),
    )(*mines, *swappeds)


def _adamw_math(w, g, m, v):
    nm = ADAM_B1 * m + (1.0 - ADAM_B1) * g
    nv = ADAM_B2 * v + (1.0 - ADAM_B2) * (g * g)
    m_hat = nm / (1.0 - ADAM_B1 ** ADAM_STEP)
    v_hat = nv / (1.0 - ADAM_B2 ** ADAM_STEP)
    return -ADAM_LR * (m_hat / (jnp.sqrt(v_hat) + ADAM_EPS) + ADAM_WD * w), nm, nv


def adamw_small(ws, parts, ms, vs):
    n = len(ws)

    def body(*refs):
        for i in range(n):
            w_ref, p_ref, m_ref, v_ref = (refs[k * n + i] for k in range(4))
            g = p_ref[0]
            for j in range(1, p_ref.shape[0]):
                g = g + p_ref[j]
            delta, nm, nv = _adamw_math(w_ref[...], g, m_ref[...], v_ref[...])
            for k, val in enumerate((g, delta, nm, nv)):
                refs[(4 + k) * n + i][...] = val

    vmem = pl.BlockSpec(memory_space=pltpu.VMEM)
    outs = pl.pallas_call(
        body, name="adamw_small", in_specs=[vmem] * (4 * n), out_specs=[vmem] * (4 * n),
        out_shape=[_sds(w.shape) for w in ws] * 4, compiler_params=_params(),
    )(*ws, *parts, *ms, *vs)
    return [outs[k * n:(k + 1) * n] for k in range(4)]

def adamw(name, w, parts, m, v, rows, plan=None):
    n_parts = parts.shape[0]

    def body(w_ref, p_ref, m_ref, v_ref, g_ref, d_ref, nm_ref, nv_ref):
        g = p_ref[0].astype(F32)
        for j in range(1, n_parts):
            g = g + p_ref[j].astype(F32)
        g_ref[...] = g
        d_ref[...], nm_ref[...], nv_ref[...] = _adamw_math(w_ref[...], g, m_ref[...], v_ref[...])

    cols = w.shape[1]
    spec = pl.BlockSpec((rows, cols), lambda i: (i, 0))
    grid = (w.shape[0] // rows,)
    in_specs = [spec, pl.BlockSpec((n_parts, rows, cols), lambda i: (0, i, 0)), spec, spec]
    if plan is not None:
        return call_with_comm(plan, 0, body, grid, name, in_specs, [spec] * 4, [_sds(w.shape)] * 4, [],
                              (w, parts, m, v))
    return pl.pallas_call(
        body, grid=grid, name=name, in_specs=in_specs, out_specs=[spec] * 4, out_shape=[_sds(w.shape)] * 4,
        compiler_params=_params(("arbitrary",)),
    )(w, parts, m, v)


def _to_slots(full, per):
    return full.reshape(full.shape[0], N_DEV, per).transpose(1, 0, 2)


def _from_slots(slots):
    return slots.transpose(1, 0, 2).reshape(slots.shape[1], -1)


def kernel(x, norm_mix_pre, norm_mix_post, norm_ffn_pre, norm_ffn_post, w_in, rel_bias, sinks, rwkv_shift_mix, w0, w_decay_up, a0, w_iclr_up, w_gate_up, k_k, k_a, r_k, ln_x_g, ln_x_b, w_out, w_ffn_up, conv_w, conv_b, w_ffn_down, loss_target, m_norm_mix_pre, m_norm_mix_post, m_norm_ffn_pre, m_norm_ffn_post, m_w_in, m_rel_bias, m_sinks, m_rwkv_shift_mix, m_w0, m_w_decay_up, m_a0, m_w_iclr_up, m_w_gate_up, m_k_k, m_k_a, m_r_k, m_ln_x_g, m_ln_x_b, m_w_out, m_w_ffn_up, m_conv_w, m_conv_b, m_w_ffn_down, v_norm_mix_pre, v_norm_mix_post, v_norm_ffn_pre, v_norm_ffn_post, v_w_in, v_rel_bias, v_sinks, v_rwkv_shift_mix, v_w0, v_w_decay_up, v_a0, v_w_iclr_up, v_w_gate_up, v_k_k, v_k_a, v_r_k, v_ln_x_g, v_ln_x_b, v_w_out, v_w_ffn_up, v_conv_w, v_conv_b, v_w_ffn_down):
    x2 = x[0]
    target = loss_target[0]

    g_in, g_out, g_decay, g_iclr, g_gate, g_conv = run_comm("all_gather_mixer", gather_plan([
        cast_bf16(w_in[0], 256), cast_bf16(w_out[0], 128), w_decay_up[0], w_iclr_up[0], w_gate_up[0], conv_w[0]]))
    up_gather = gather_plan([cast_bf16(w_ffn_up[0], 256)])
    down_gather = gather_plan([cast_bf16(w_ffn_down[0], 256)])
    w_in_b = _from_slots(g_in)
    w_out_b = g_out.reshape(D_MODEL, D_MODEL)
    lora = jnp.zeros((HEAD_DIM, D_RWKV), F32)
    wd_pad = jnp.concatenate([_from_slots(g_decay), lora], axis=0)
    wi_pad = jnp.concatenate([lora, _from_slots(g_iclr)], axis=0)
    wg_full = _from_slots(g_gate)
    mix_ext = jnp.concatenate([jnp.zeros((1, D_QKV), F32), rwkv_shift_mix], axis=1)
    r_k_row = r_k.reshape(1, D_RWKV)
    bucket = _bucket_table()

    (h1,) = tok_fwd("rms_mix_pre", rms_tile, [x2], [norm_mix_pre], [], [D_MODEL], [BF16])
    proj, ps = in_proj_fwd(h1, w_in_b, mix_ext)
    attn, (g_down,) = attn_fwd(proj, rel_bias, bucket, sinks, down_gather)
    pre_params = [w0, wd_pad, a0, wi_pad, wg_full, k_k, k_a]
    r_, lw_, k2_, v_, kk_, a_, gate_ = tok_fwd("rwkv_pre", rwkv_pre_tile, [ps], pre_params, [],
                                               [D_RWKV] * 7, [F32] * 7)
    (o_, states), (g_up,) = rwkv_scan_fwd(r_, lw_, k2_, v_, kk_, a_, up_gather)
    w_down_b = g_down.reshape(D_FF, D_MODEL)
    mix_tiles = [o_, r_, k2_, v_, gate_, attn, x2]
    mix_params = [w_out_b, norm_mix_post, ln_x_g, ln_x_b, r_k_row, norm_ffn_pre]
    x1, h2 = tok_fwd("mix_out", mix_out_tile, mix_tiles, mix_params, [(D_MODEL, D_MODEL)], [D_MODEL, D_MODEL],
                     [F32, BF16])
    f = ffn_fwd(h2, g_up, g_conv, conv_b, w_down_b)
    dy, df, d_n_ffn_post, loss_row = loss_head(x1, f, target, norm_ffn_post)

    d_ug, d_uv, d_cw_g, d_cw_v, d_cb_g, d_cb_v, d_down = ffn_bwd_mid(h2, g_up, g_conv, conv_b, w_down_b, df)
    dh2, d_up = ffn_bwd_up(h2, g_up, d_ug, d_uv)
    half = N_DEV // 2
    d_cw = jnp.concatenate([d_cw_g[:half], d_cw_v[half:]], axis=0)
    by_pair = lambda slots: slots.reshape((N_DEV // 2, 2) + slots.shape[1:])
    ffn_mine = [by_pair(d_up), by_pair(d_down.reshape(N_DEV, D_FF // N_DEV, D_MODEL))]
    ffn_swapped = run_comm("pair_exchange_ffn", pair_plan(ffn_mine, [False, False]))
    ffn_exchange = chip_plan([add_pair("pair_add_w_ffn_up", ffn_mine[0], ffn_swapped[0], BF16, 256),
                              add_pair("pair_add_w_ffn_down", ffn_mine[1], ffn_swapped[1], BF16, 256)],
                             [False, False])
    d_cb = jnp.concatenate([d_cb_g[:, :D_FF], d_cb_v[:, D_FF:]], axis=1)
    (d_o, d_r1, d_k1, d_v1, d_gate, d_attn, dx_res, d_n_mix_post, d_ln_g, d_ln_b, d_r_k, d_n_ffn_pre,
     d_w_out) = tok_bwd("mix_out_bwd", mix_out_tile, mix_tiles, mix_params, [(D_MODEL, D_MODEL)], [dy, dh2],
                        [1, 2, 3, 4, 5])
    (d_r2, d_lw, d_k2, d_v2, d_kk, d_a), (got_up, got_down) = rwkv_scan_bwd(
        r_, lw_, k2_, v_, kk_, a_, states, d_o, ffn_exchange)
    pre_cots = [(d_r1, d_r2), d_lw, (d_k1, d_k2), (d_v1, d_v2), d_kk, d_a, d_gate]
    (d_ps, d_w0, d_wd_pad, d_a0, d_wi_pad, d_wg, d_k_k, d_k_a) = tok_bwd(
        "rwkv_pre_bwd", rwkv_pre_tile, [ps], pre_params, [], pre_cots, [0, 1, 2, 3, 4, 5, 6])
    dq, dkc, dkp, dvc, dvp, d_rel_bias, d_sinks = attn_bwd(proj, rel_bias, bucket, sinks, d_attn)
    zero_blk = jnp.zeros((BLOCK, D_KV), F32)
    dk = dkc + jnp.concatenate([dkp[BLOCK:], zero_blk], axis=0)
    dv = dvc + jnp.concatenate([dvp[BLOCK:], zero_blk], axis=0)
    dpa = jnp.concatenate([dq, dk, dv, d_ps], axis=1)
    dh1, d_w_in, d_mix_ext = in_proj_bwd(h1, w_in_b, mix_ext, proj, dpa)
    grad_x2, d_n_mix_pre = tok_bwd("rms_mix_pre_bwd", rms_tile, [x2], [norm_mix_pre], [], [dh1], [0], {0: dx_res})
    grad_x = grad_x2[None]

    small_rep = [d_n_mix_pre, d_n_mix_post, d_n_ffn_pre, d_n_ffn_post, d_rel_bias, d_sinks,
                 d_mix_ext[:, D_QKV:], d_w0, d_a0, d_k_k, d_k_a, d_r_k.reshape(r_k.shape), d_ln_g, d_ln_b, d_cb]
    rep_w = [norm_mix_pre, norm_mix_post, norm_ffn_pre, norm_ffn_post, rel_bias, sinks, rwkv_shift_mix,
             w0, a0, k_k, k_a, r_k, ln_x_g, ln_x_b, conv_b]
    rep_m = [m_norm_mix_pre, m_norm_mix_post, m_norm_ffn_pre, m_norm_ffn_post, m_rel_bias, m_sinks,
             m_rwkv_shift_mix, m_w0, m_a0, m_k_k, m_k_a, m_r_k, m_ln_x_g, m_ln_x_b, m_conv_b]
    rep_v = [v_norm_mix_pre, v_norm_mix_post, v_norm_ffn_pre, v_norm_ffn_post, v_rel_bias, v_sinks,
             v_rwkv_shift_mix, v_w0, v_a0, v_k_k, v_k_a, v_r_k, v_ln_x_g, v_ln_x_b, v_conv_b]
    sh_w = [w_decay_up, w_iclr_up, w_gate_up, conv_w]
    sh_m = [m_w_decay_up, m_w_iclr_up, m_w_gate_up, m_conv_w]
    sh_v = [v_w_decay_up, v_w_iclr_up, v_w_gate_up, v_conv_w]
    sh_parts = [_to_slots(d_wd_pad[:HEAD_DIM], HEAD_DIM), _to_slots(d_wi_pad[HEAD_DIM:], HEAD_DIM),
                _to_slots(d_wg, HEAD_DIM), d_cw]
    n_rep, n_sh = len(small_rep), len(sh_parts)
    mine = [by_pair(_to_slots(d_w_in, D_IN // N_DEV)), by_pair(d_w_out.reshape(N_DEV, D_MODEL // N_DEV, D_MODEL)),
            *small_rep, *(by_pair(p) for p in sh_parts), loss_row]
    is_rep = [False, False] + [True] * n_rep + [False] * n_sh + [True]
    adam_down, swapped = adamw("adamw_w_ffn_down", w_ffn_down[0], got_down, m_w_ffn_down[0], v_w_ffn_down[0], 128,
                               pair_plan(mine, is_rep))
    chip_sums = [add_pair("pair_add_w_in", mine[0], swapped[0], BF16, 512),
                 add_pair("pair_add_w_out", mine[1], swapped[1], BF16, 128),
                 *add_pair_small(mine[2:], swapped[2:])]
    adam_up, got = adamw("adamw_w_ffn_up", w_ffn_up[0], got_up, m_w_ffn_up[0], v_w_ffn_up[0], 128,
                         chip_plan(chip_sums, is_rep))

    big = [adamw("adamw_w_in", w_in[0], got[0], m_w_in[0], v_w_in[0], 256),
           adamw("adamw_w_out", w_out[0], got[1], m_w_out[0], v_w_out[0], 128), adam_up, adam_down]
    loss = functools.reduce(jnp.add, [got[-1][q, 0, 0] for q in range(N_DEV // 2)])
    small_w, small_g = rep_w + sh_w, got[2:-1]
    as_grad = lambda arrays: [a.reshape(g.shape[1:]) for a, g in zip(arrays, small_g)]
    small = adamw_small(as_grad(small_w), small_g, as_grad(rep_m + sh_m), as_grad(rep_v + sh_v))
    small = [[a.reshape(w.shape) for a, w in zip(kind, small_w)] for kind in small]

    names = ["norm_mix_pre", "norm_mix_post", "norm_ffn_pre", "norm_ffn_post", "w_in", "rel_bias", "sinks",
             "rwkv_shift_mix", "w0", "w_decay_up", "a0", "w_iclr_up", "w_gate_up", "k_k", "k_a", "r_k",
             "ln_x_g", "ln_x_b", "w_out", "w_ffn_up", "conv_w", "conv_b", "w_ffn_down"]
    small_names = ["norm_mix_pre", "norm_mix_post", "norm_ffn_pre", "norm_ffn_post", "rel_bias", "sinks",
                   "rwkv_shift_mix", "w0", "a0", "k_k", "k_a", "r_k", "ln_x_g", "ln_x_b", "conv_b",
                   "w_decay_up", "w_iclr_up", "w_gate_up", "conv_w"]
    big_names = {"w_in": 0, "w_out": 1, "w_ffn_up": 2, "w_ffn_down": 3}
    outs = []
    for kind in range(4):
        for nm in names:
            if nm in big_names:
                outs.append(big[big_names[nm]][kind][None])
            else:
                outs.append(small[kind][small_names.index(nm)])
    return (loss, grad_x, *outs)
```

```python
import functools
import math

import jax
import jax.numpy as jnp
from jax import lax
from jax.experimental import pallas as pl
from jax.experimental.pallas import tpu as pltpu

F32 = jnp.float32
BF16 = jnp.bfloat16

N_DEV = 8
SEQ = 2048
D_MODEL = 1024
HEAD_DIM = 64
D_ATTN = 512
D_KV = 128
D_RWKV = 512
N_HEADS = 8
RWKV_COLS = 1792
D_QKV = D_ATTN + 2 * D_KV
D_IN = D_QKV + RWKV_COLS
D_FF = 4096
BLOCK = 128
N_BLOCKS = SEQ // BLOCK
N_BUCKETS = 32
MAX_DISTANCE = 128
NORM_EPS = 1e-6
GN_EPS = 64e-5
NEG_INF = -1e30
CHUNK = 64
N_CHUNKS = SEQ // CHUNK
SCAN_GROUPS = 4
SCAN_WIDTH = D_RWKV // SCAN_GROUPS
TOK_TILE = 256
FF_TILE = 256
FF_ROW_CHUNK = 256
FF_HALO = 8
COL_TILE = 256
LANES = 128
VMEM_LIMIT = 56 * 1024 * 1024

ADAM_LR = 0.001
ADAM_B1 = 0.9
ADAM_B2 = 0.999
ADAM_EPS = 1e-08
ADAM_WD = 0.01
ADAM_STEP = 10

NT = ((1,), (1,))
TN = ((0,), (0,))
NN = ((1,), (0,))


def _sds(shape, dtype=F32):
    return jax.ShapeDtypeStruct(shape, dtype)


def _params(sem=None):
    if sem is None:
        return pltpu.CompilerParams(vmem_limit_bytes=VMEM_LIMIT)
    return pltpu.CompilerParams(dimension_semantics=sem, vmem_limit_bytes=VMEM_LIMIT)


def _dot(a, b, dims):
    return lax.dot_general(a, b, (dims, ((), ())), preferred_element_type=F32)


def _split2(x):
    hi = x.astype(BF16)
    return hi, (x - hi.astype(F32)).astype(BF16)


def _dot3_raw(a, b, dims):
    ah, al = _split2(a)
    bh, bl = _split2(b)
    return _dot(ah, bh, dims) + (_dot(al, bh, dims) + _dot(ah, bl, dims))


@functools.partial(jax.custom_vjp, nondiff_argnums=(2,))
def dot3(a, b, dims):
    return _dot3_raw(a, b, dims)


def _dot3_fwd(a, b, dims):
    return _dot3_raw(a, b, dims), (a, b)


def _dot3_bwd(dims, res, g):
    a, b = res
    if dims == NN:
        return dot3(g, b, NT), dot3(a, g, TN)
    if dims == NT:
        return dot3(g, b, NN), dot3(g, a, TN)
    return dot3(b, g, NT), dot3(a, g, NN)


dot3.defvjp(_dot3_fwd, _dot3_bwd)


@functools.partial(jax.custom_vjp, nondiff_argnums=(2,))
def dot1(a, b, dims):
    return _dot(a.astype(BF16), b.astype(BF16), dims)


def _dot1_fwd(a, b, dims):
    return dot1(a, b, dims), (a, b)


def _dot1_bwd(dims, res, g):
    a, b = res
    if dims == NN:
        return dot1(g, b, NT), dot1(a, g, TN)
    if dims == NT:
        return dot1(g, b, NN), dot1(g, a, TN)
    return dot1(b, g, NT), dot1(a, g, NN)


dot1.defvjp(_dot1_fwd, _dot1_bwd)


@jax.custom_vjp
def mm(a, b):
    return _dot(a.astype(BF16), b.astype(BF16), NN)


def _mm_fwd(a, b):
    return mm(a, b), (a, b)


def _mm_bwd(res, g):
    a, b = res
    gb = g.astype(BF16)
    return _dot(gb, b.astype(BF16), NT).astype(a.dtype), _dot(a.astype(BF16), gb, TN).astype(b.dtype)


mm.defvjp(_mm_fwd, _mm_bwd)


@jax.custom_vjp
def mm_nt(a, b):
    return _dot(a.astype(BF16), b.astype(BF16), NT)


def _mm_nt_fwd(a, b):
    return mm_nt(a, b), (a, b)


def _mm_nt_bwd(res, g):
    a, b = res
    gb = g.astype(BF16)
    return _dot(gb, b.astype(BF16), NN).astype(a.dtype), _dot(gb, a.astype(BF16), TN).astype(b.dtype)


mm_nt.defvjp(_mm_nt_fwd, _mm_nt_bwd)


@jax.custom_vjp
def mmw(a, w, wz):
    return _dot(a.astype(BF16), w, NN)


def _mmw_fwd(a, w, wz):
    return mmw(a, w, wz), (a, w)


def _mmw_bwd(res, g):
    a, w = res
    gb = g.astype(BF16)
    return _dot(gb, w, NT).astype(a.dtype), jnp.zeros_like(w), _dot(a.astype(BF16), gb, TN)


mmw.defvjp(_mmw_fwd, _mmw_bwd)


def _shift_raw(x, n):
    rows = x.shape[0]
    rolled = pltpu.roll(x, n % rows, 0)
    idx = lax.broadcasted_iota(jnp.int32, x.shape, 0)
    keep = idx >= n if n > 0 else idx < rows + n
    return jnp.where(keep, rolled, 0.0)


@functools.partial(jax.custom_vjp, nondiff_argnums=(1,))
def shift_rows(x, n):
    return _shift_raw(x, n)


def _shift_fwd(x, n):
    return _shift_raw(x, n), None


def _shift_bwd(n, _, g):
    return (_shift_raw(g, -n),)


shift_rows.defvjp(_shift_fwd, _shift_bwd)


def _head_matrix(scale):
    a = lax.broadcasted_iota(jnp.int32, (D_RWKV, D_RWKV), 0) // HEAD_DIM
    b = lax.broadcasted_iota(jnp.int32, (D_RWKV, D_RWKV), 1) // HEAD_DIM
    return jnp.where(a == b, scale, 0.0).astype(F32)


def _rms(x, g):
    return x * lax.rsqrt(jnp.mean(x * x, axis=-1, keepdims=True) + NORM_EPS) * g


def _softplus(x):
    return jnp.maximum(x, 0.0) + jnp.log(1.0 + jnp.exp(-jnp.abs(x)))


def _tile_spec(arr, tm):
    return pl.BlockSpec((tm, arr.shape[1]), lambda i: (i, 0))


def _full_spec(arr):
    nd = arr.ndim
    return pl.BlockSpec(arr.shape, lambda i: (0,) * nd)


def tok_fwd(name, fn, tiles, params, zero_shapes, out_widths, out_dtypes, tm=TOK_TILE):
    n_t, n_p = len(tiles), len(params)

    def body(*refs):
        t_vals = [r[...] for r in refs[:n_t]]
        p_vals = [r[...] for r in refs[n_t:n_t + n_p]]
        z_vals = [jnp.zeros(s, F32) for s in zero_shapes]
        outs = fn(*t_vals, *p_vals, *z_vals)
        for r, o in zip(refs[n_t + n_p:], outs):
            r[...] = o.astype(r.dtype)

    rows = tiles[0].shape[0]
    return pl.pallas_call(
        body, grid=(rows // tm,), name=name,
        in_specs=[_tile_spec(t, tm) for t in tiles] + [_full_spec(p) for p in params],
        out_specs=[pl.BlockSpec((tm, w), lambda i: (i, 0)) for w in out_widths],
        out_shape=[_sds((rows, w), dt) for w, dt in zip(out_widths, out_dtypes)],
        compiler_params=_params(("arbitrary",)),
    )(*tiles, *params)


def tok_bwd(name, fn, tiles, params, zero_shapes, cots, diff_params, residuals=(), tm=TOK_TILE):
    cot_parts = [c if isinstance(c, tuple) else (c,) for c in cots]
    flat_cots = [a for part in cot_parts for a in part]
    residuals = dict(residuals)
    extra = [residuals[i] for i in sorted(residuals)]
    n_t, n_p, n_c, n_r = len(tiles), len(params), len(flat_cots), len(extra)
    acc_shapes = [params[i].shape for i in diff_params] + list(zero_shapes)

    def body(*refs):
        t_vals = [r[...].astype(F32) for r in refs[:n_t]]
        p_vals = [r[...] for r in refs[n_t:n_t + n_p]]
        flat = iter(r[...] for r in refs[n_t + n_p:n_t + n_p + n_c])
        c_vals = [functools.reduce(jnp.add, [next(flat) for _ in part]) for part in cot_parts]
        r_vals = dict(zip(sorted(residuals), (r[...] for r in refs[n_t + n_p + n_c:n_t + n_p + n_c + n_r])))
        out_refs = refs[n_t + n_p + n_c + n_r:]
        z_vals = [jnp.zeros(s, F32) for s in zero_shapes]
        d_vals = [p_vals[i] for i in diff_params]

        def f(t_in, d_in, z_in):
            full = list(p_vals)
            for i, v in zip(diff_params, d_in):
                full[i] = v
            return tuple(fn(*t_in, *full, *z_in))

        _, vjp = jax.vjp(f, t_vals, d_vals, z_vals)
        g_t, g_d, g_z = vjp(tuple(c_vals))
        for i, (r, g) in enumerate(zip(out_refs[:n_t], g_t)):
            r[...] = (g + r_vals[i] if i in r_vals else g).astype(r.dtype)
        acc_refs = out_refs[n_t:]

        @pl.when(pl.program_id(0) == 0)
        def _():
            for r in acc_refs:
                r[...] = jnp.zeros_like(r)

        for r, g in zip(acc_refs, list(g_d) + list(g_z)):
            r[...] += g

    rows = tiles[0].shape[0]
    return pl.pallas_call(
        body, grid=(rows // tm,), name=name,
        in_specs=[_tile_spec(t, tm) for t in tiles] + [_full_spec(p) for p in params]
        + [_tile_spec(c, tm) for c in flat_cots + extra],
        out_specs=[_tile_spec(t, tm) for t in tiles]
        + [pl.BlockSpec(s, lambda i, nd=len(s): (0,) * nd) for s in acc_shapes],
        out_shape=[_sds(t.shape) for t in tiles] + [_sds(s) for s in acc_shapes],
        compiler_params=_params(("arbitrary",)),
    )(*tiles, *params, *flat_cots, *extra)


def rms_tile(x, g):
    return (_rms(x, g),)


def rwkv_pre_tile(ps, w0, wd_pad, a0, wi_pad, wg, k_k, k_a):
    r = ps[:, 0:D_RWKV]
    k = ps[:, D_RWKV:2 * D_RWKV]
    v = ps[:, 2 * D_RWKV:3 * D_RWKV]
    z2 = ps[:, 3 * D_RWKV:3 * D_RWKV + LANES]
    zg = ps[:, 3 * D_RWKV + LANES:RWKV_COLS]
    w_log = -_softplus(-(w0 + mm(jnp.tanh(z2), wd_pad))) - 0.5
    lw = -jnp.exp(w_log)
    a = jax.nn.sigmoid(a0 + mm(z2, wi_pad))
    g = mm(jax.nn.sigmoid(zg), wg)
    kk = k * k_k
    norm = jnp.sqrt(dot3(kk * kk, _head_matrix(1.0), NN))
    kk = kk / jnp.maximum(norm, 1e-12)
    k2 = k * (1.0 + (a - 1.0) * k_a)
    return r, lw, k2, v, kk, a, g


def mix_out_tile(o, r, k2, v, g, attn, x, w_out, n_post, ln_g, ln_b, r_k, n_ffn_pre, wz):
    hmean = _head_matrix(1.0 / HEAD_DIM)
    d = o - dot3(o, hmean, NN)
    var = dot3(d * d, hmean, NN)
    on = d * lax.rsqrt(var + GN_EPS) * ln_g + ln_b
    bonus = dot3(r * k2 * r_k, _head_matrix(1.0), NN) * v
    rw = (on + bonus) * g
    mix = mmw(jnp.concatenate([attn, rw], axis=1), w_out, wz)
    x1 = x + _rms(mix, n_post)
    return x1, _rms(x1, n_ffn_pre)


def in_proj_fwd(h, w_in, mix_ext):
    def body(h_ref, w_ref, m_ref, proj_ref, ps_ref):
        p = _dot(h_ref[...], w_ref[...], NN)
        proj_ref[...] = p
        ps_ref[...] = p + (_shift_raw(p, 1) - p) * m_ref[...]

    n = D_IN // COL_TILE
    first = D_QKV // COL_TILE
    return pl.pallas_call(
        body, grid=(n,), name="in_proj_fwd",
        in_specs=[pl.BlockSpec((SEQ, D_MODEL), lambda j: (0, 0)),
                  pl.BlockSpec((D_MODEL, COL_TILE), lambda j: (0, j)),
                  pl.BlockSpec((1, COL_TILE), lambda j: (0, j))],
        out_specs=[pl.BlockSpec((SEQ, COL_TILE), lambda j: (0, j)),
                   pl.BlockSpec((SEQ, COL_TILE), lambda j: (0, jnp.maximum(j - first, 0)))],
        out_shape=[_sds((SEQ, D_IN)), _sds((SEQ, RWKV_COLS))],
        compiler_params=_params(("arbitrary",)),
    )(h, w_in, mix_ext)


def in_proj_bwd(h, w_in, mix_ext, proj, dpa):
    def body(h_ref, w_ref, m_ref, p_ref, d_ref, dh_ref, dw_ref, dm_ref):
        d = d_ref[...]
        p = p_ref[...]
        dm_ref[...] = jnp.sum(d * (_shift_raw(p, 1) - p), axis=0, keepdims=True)
        dmix = d * m_ref[...]
        dp = (d - dmix + _shift_raw(dmix, -1)).astype(BF16)
        dw_ref[...] = _dot(h_ref[...], dp, TN)

        @pl.when(pl.program_id(0) == 0)
        def _():
            dh_ref[...] = jnp.zeros_like(dh_ref)

        dh_ref[...] += _dot(dp, w_ref[...], NT)

    n = D_IN // COL_TILE
    col = lambda rows: pl.BlockSpec((rows, COL_TILE), lambda j: (0, j))
    return pl.pallas_call(
        body, grid=(n,), name="in_proj_bwd",
        in_specs=[pl.BlockSpec((SEQ, D_MODEL), lambda j: (0, 0)), col(D_MODEL), col(1), col(SEQ), col(SEQ)],
        out_specs=[pl.BlockSpec((SEQ, D_MODEL), lambda j: (0, 0)), col(D_MODEL), col(1)],
        out_shape=[_sds((SEQ, D_MODEL)), _sds((D_MODEL, D_IN)), _sds((1, D_IN))],
        compiler_params=_params(("arbitrary",)),
    )(h, w_in, mix_ext, proj, dpa)


def _bucket_table():
    rel = (jnp.arange(BLOCK)[:, None] + BLOCK) - jnp.arange(2 * BLOCK)[None, :]
    n = jnp.maximum(rel, 0)
    max_exact = N_BUCKETS // 2
    large = max_exact + (jnp.log(jnp.maximum(n, 1).astype(F32) / max_exact)
                         / math.log(MAX_DISTANCE / max_exact) * (N_BUCKETS - max_exact)).astype(jnp.int32)
    large = jnp.minimum(large, N_BUCKETS - 1)
    return jnp.where(n < max_exact, n, large).astype(jnp.int32)


def _select_matrix(g, o):
    a = lax.broadcasted_iota(jnp.int32, (D_KV, D_KV), 0)
    b = lax.broadcasted_iota(jnp.int32, (D_KV, D_KV), 1)
    return ((a - HEAD_DIM * g == b - o) & (b >= o) & (b < o + HEAD_DIM)).astype(F32)


def _attn_block(q, kp, kc, vp, vc, bias, sinks, block_idx):
    kb = jnp.concatenate([kp, kc], axis=0)
    vb = jnp.concatenate([vp, vc], axis=0)
    row = lax.broadcasted_iota(jnp.int32, (BLOCK, 2 * BLOCK), 0)
    col = lax.broadcasted_iota(jnp.int32, (BLOCK, 2 * BLOCK), 1)
    rel = row + BLOCK - col
    mask = (rel >= 0) & (rel < BLOCK) & (col + (block_idx - 1) * BLOCK >= 0)
    lane8 = lax.broadcasted_iota(jnp.int32, (1, N_HEADS), 1)
    kt, vt = {}, {}
    for g in range(2):
        for o in (0, HEAD_DIM):
            sel = _select_matrix(g, o)
            kt[g, o] = mm(kb, sel)
            vt[g, o] = mm(vb, sel)
    outs = []
    for j in range(D_ATTN // LANES):
        qs = q[:, j * LANES:(j + 1) * LANES]
        acc = None
        for half in range(2):
            hq = 2 * j + half
            g, o = hq // 4, half * HEAD_DIM
            s = mm_nt(qs, kt[g, o]) * (HEAD_DIM ** -0.5) + bias[hq]
            s = jnp.where(mask, s, NEG_INF)
            sink = jnp.sum(jnp.where(lane8 == hq, sinks, 0.0), axis=1, keepdims=True)
            m = lax.stop_gradient(jnp.maximum(jnp.max(s, axis=-1, keepdims=True), sink))
            p = jnp.exp(s - m)
            probs = p / (jnp.sum(p, axis=-1, keepdims=True) + jnp.exp(sink - m))
            part = mm(probs, vt[g, o])
            acc = part if acc is None else acc + part
        outs.append(acc)
    return jnp.concatenate(outs, axis=1)


def _build_bias(rb_ref, bucket, bias_ref):
    for hq in range(N_HEADS):
        acc = jnp.zeros((BLOCK, 2 * BLOCK), F32)
        for b in range(N_BUCKETS):
            acc = jnp.where(bucket == b, rb_ref[b, hq], acc)
        bias_ref[hq] = acc


def _attn_in_specs():
    prev = lambda n: jnp.maximum(n - 1, 0)
    return [pl.BlockSpec((BLOCK, D_ATTN), lambda n: (n, 0)),
            pl.BlockSpec((BLOCK, D_KV), lambda n: (prev(n), D_ATTN // D_KV)),
            pl.BlockSpec((BLOCK, D_KV), lambda n: (n, D_ATTN // D_KV)),
            pl.BlockSpec((BLOCK, D_KV), lambda n: (prev(n), D_ATTN // D_KV + 1)),
            pl.BlockSpec((BLOCK, D_KV), lambda n: (n, D_ATTN // D_KV + 1)),
            pl.BlockSpec(memory_space=pltpu.SMEM),
            pl.BlockSpec((BLOCK, 2 * BLOCK), lambda n: (0, 0)),
            pl.BlockSpec((1, N_HEADS), lambda n: (0, 0))]


def attn_fwd(proj, rel_bias, bucket, sinks, plan):
    def body(q_ref, kp_ref, kc_ref, vp_ref, vc_ref, rb_ref, bk_ref, sk_ref, o_ref, bias_ref):
        n = pl.program_id(0)

        @pl.when(n == 0)
        def _():
            _build_bias(rb_ref, bk_ref[...], bias_ref)

        o_ref[...] = _attn_block(q_ref[...], kp_ref[...], kc_ref[...], vp_ref[...], vc_ref[...],
                                 tuple(bias_ref[h] for h in range(N_HEADS)), sk_ref[...], n)

    (attn,), gathered = call_with_comm(
        plan, 3 * N_BLOCKS // 4, body, (N_BLOCKS,), "attn_fwd", _attn_in_specs(),
        [pl.BlockSpec((BLOCK, D_ATTN), lambda n: (n, 0))], [_sds((SEQ, D_ATTN))],
        [pltpu.VMEM((N_HEADS, BLOCK, 2 * BLOCK), F32)], (proj, proj, proj, proj, proj, rel_bias, bucket, sinks))
    return attn, gathered


def attn_bwd(proj, rel_bias, bucket, sinks, d_attn):
    def body(q_ref, kp_ref, kc_ref, vp_ref, vc_ref, rb_ref, bk_ref, sk_ref, do_ref,
             dq_ref, dkc_ref, dkp_ref, dvc_ref, dvp_ref, drb_ref, dsk_ref, bias_ref, dbias_ref):
        n = pl.program_id(0)

        @pl.when(n == 0)
        def _():
            _build_bias(rb_ref, bk_ref[...], bias_ref)
            dbias_ref[...] = jnp.zeros_like(dbias_ref)
            dsk_ref[...] = jnp.zeros_like(dsk_ref)

        f = lambda q, kp, kc, vp, vc, bias, sk: _attn_block(q, kp, kc, vp, vc, bias, sk, n)
        _, vjp = jax.vjp(f, q_ref[...], kp_ref[...], kc_ref[...], vp_ref[...], vc_ref[...],
                         tuple(bias_ref[h] for h in range(N_HEADS)), sk_ref[...])
        dq, dkp, dkc, dvp, dvc, dbias, dsk = vjp(do_ref[...])
        dq_ref[...] = dq
        dkc_ref[...] = dkc
        dkp_ref[...] = dkp
        dvc_ref[...] = dvc
        dvp_ref[...] = dvp
        for h in range(N_HEADS):
            dbias_ref[h] += dbias[h]
        dsk_ref[...] += dsk

        @pl.when(n == N_BLOCKS - 1)
        def _():
            bucket_v = bk_ref[...]
            rowi = lax.broadcasted_iota(jnp.int32, (N_BUCKETS, 2 * BLOCK), 0)
            lane = lax.broadcasted_iota(jnp.int32, (N_BUCKETS, N_HEADS), 1)
            out = jnp.zeros((N_BUCKETS, N_HEADS), F32)
            for hq in range(N_HEADS):
                dbh = dbias_ref[hq]
                rows = jnp.zeros((N_BUCKETS, 2 * BLOCK), F32)
                for b in range(N_BUCKETS):
                    part = jnp.sum(jnp.where(bucket_v == b, dbh, 0.0), axis=0, keepdims=True)
                    rows = jnp.where(rowi == b, part, rows)
                tot = jnp.sum(rows, axis=1, keepdims=True)
                out = jnp.where(lane == hq, tot, out)
            drb_ref[...] = out

    blk = lambda w: pl.BlockSpec((BLOCK, w), lambda n: (n, 0))
    return pl.pallas_call(
        body, grid=(N_BLOCKS,), name="attn_bwd",
        in_specs=_attn_in_specs() + [blk(D_ATTN)],
        out_specs=[blk(D_ATTN), blk(D_KV), blk(D_KV), blk(D_KV), blk(D_KV),
                   pl.BlockSpec((N_BUCKETS, N_HEADS), lambda n: (0, 0)),
                   pl.BlockSpec((1, N_HEADS), lambda n: (0, 0))],
        out_shape=[_sds((SEQ, D_ATTN)), _sds((SEQ, D_KV)), _sds((SEQ, D_KV)), _sds((SEQ, D_KV)),
                   _sds((SEQ, D_KV)), _sds((N_BUCKETS, N_HEADS)), _sds((1, N_HEADS))],
        scratch_shapes=[pltpu.VMEM((N_HEADS, BLOCK, 2 * BLOCK), F32),
                        pltpu.VMEM((N_HEADS, BLOCK, 2 * BLOCK), F32)],
        compiler_params=_params(("arbitrary",)),
    )(proj, proj, proj, proj, proj, rel_bias, bucket, sinks, d_attn)


def _stack(x, size):
    groups = x.shape[1] // size
    lane = lax.broadcasted_iota(jnp.int32, x.shape, 1) // size
    return jnp.concatenate([jnp.where(lane == i, x, 0.0) for i in range(groups)], axis=0)


def _neumann(l):
    c = CHUNK
    t = lax.broadcasted_iota(jnp.int32, l.shape, 0)
    i = lax.broadcasted_iota(jnp.int32, l.shape, 1) % c
    inv = (i == t).astype(F32) + l
    pw = dot1(l, _stack(l, c), NN)
    for _ in range(int(math.log2(c)) - 2):
        both = dot1(jnp.concatenate([inv, pw], axis=0), _stack(pw, c), NN)
        inv = inv + both[:c]
        pw = both[c:]
    return inv + dot1(inv, _stack(pw, c), NN)


@jax.custom_vjp
def neumann_inv(l):
    return _neumann(l)


def _neumann_fwd(l):
    inv = _neumann(l)
    return inv, inv


def _neumann_bwd(inv, g):
    c = CHUNK
    bd_t = _stack(inv, c).T
    inv_t = bd_t[0:c]
    for h in range(1, inv.shape[1] // c):
        inv_t = inv_t + bd_t[h * c:(h + 1) * c]
    return (dot1(dot1(inv_t, _stack(g, c), NN), bd_t, NN),)


neumann_inv.defvjp(_neumann_fwd, _neumann_bwd)


def _cumsum_raw(x, dims):
    c = x.shape[0]
    tt = lax.broadcasted_iota(jnp.int32, (c, c), 0)
    ii = lax.broadcasted_iota(jnp.int32, (c, c), 1)
    tri = (ii <= tt).astype(BF16)
    hi = x.astype(BF16)
    rest = x - hi.astype(F32)
    mid = rest.astype(BF16)
    lo = (rest - mid.astype(F32)).astype(BF16)
    return _dot(tri, hi, dims) + (_dot(tri, mid, dims) + _dot(tri, lo, dims))


@jax.custom_vjp
def cumsum_rows(x):
    return _cumsum_raw(x, NN)


def _cumsum_fwd(x):
    return _cumsum_raw(x, NN), None


def _cumsum_bwd(_, g):
    return (_cumsum_raw(g, TN),)


cumsum_rows.defvjp(_cumsum_fwd, _cumsum_bwd)


def _rwkv_chunk(s0, r, lw, k, v, kk, a):
    heads = r.shape[1] // HEAD_DIM
    c, hc = CHUNK, heads * CHUNK
    t = lax.broadcasted_iota(jnp.int32, (c, hc), 0)
    i = lax.broadcasted_iota(jnp.int32, (c, hc), 1) % c
    strict, incl = i < t, i <= t
    stack = lambda x: _stack(x, HEAD_DIM)
    ba = lax.broadcasted_iota(jnp.int32, s0.shape, 0) // HEAD_DIM
    bb = lax.broadcasted_iota(jnp.int32, s0.shape, 1) // HEAD_DIM
    blocks = (ba == bb).astype(F32)

    cum = cumsum_rows(lw)
    cum_end = jnp.sum(lw, axis=0, keepdims=True)
    beta = kk * a
    al = -kk * jnp.exp(cum - lw)
    p_inv = jnp.exp(-cum)
    be, kb, rb = beta * p_inv, k * p_inv, r * jnp.exp(cum)
    ar = jnp.concatenate([al, rb], axis=0)
    sv = stack(v)
    l_all = dot1(ar, jnp.concatenate([stack(be), stack(kb)], axis=0), NT)
    l_ab = jnp.where(strict, l_all[:c, :hc], 0.0)
    l_ak = jnp.where(strict, l_all[:c, hc:], 0.0)
    l_rb = jnp.where(incl, l_all[c:, :hc], 0.0)
    l_rk = jnp.where(incl, l_all[c:, hc:], 0.0)
    inv = neumann_inv(l_ab)
    from_s0 = dot1(ar, s0, NT)
    from_v = dot1(jnp.concatenate([l_ak, l_rk], axis=0), sv, NN)
    u = dot1(inv, stack(from_s0[:c] + from_v[:c]), NN)
    o = from_s0[c:] + from_v[c:] + dot1(l_rb, stack(u), NN)
    to_end = jnp.exp(cum_end - cum)
    s1 = s0 * jnp.exp(cum_end) + blocks * dot1(
        jnp.concatenate([u, v], axis=0), jnp.concatenate([beta * to_end, k * to_end], axis=0), TN)
    return o, s1


def call_with_comm(plan, middle_step, body, grid, name, in_specs, out_specs, out_shape, scratch_shapes, operands):
    n_in, n_out, n_scr = len(in_specs), len(out_specs), len(scratch_shapes)
    p_in, p_out = len(plan.ins), len(plan.out_shape)

    def fused(*refs):
        refs = list(refs)
        ins, refs = refs[:n_in], refs[n_in:]
        p_ins, refs = refs[:p_in], refs[p_in:]
        outs, refs = refs[:n_out], refs[n_out:]
        p_outs, refs = refs[:p_out], refs[p_out:]
        scr, p_sems = refs[:n_scr], refs[n_scr:]
        start, middle, finish = plan.stages(p_ins, p_outs, p_sems)
        step = pl.program_id(0)
        pl.when(step == 0)(start)
        body(*ins, *outs, *scr)
        pl.when(step == middle_step)(middle)
        pl.when(step == grid[0] - 1)(finish)

    any_spec = pl.BlockSpec(memory_space=pl.ANY)
    res = pl.pallas_call(
        fused, grid=grid, name=name,
        in_specs=list(in_specs) + [any_spec] * p_in, out_specs=list(out_specs) + [any_spec] * p_out,
        out_shape=list(out_shape) + list(plan.out_shape), scratch_shapes=list(scratch_shapes) + list(plan.scratch),
        compiler_params=_params(("arbitrary",)),
    )(*operands, *plan.ins)
    return res[:n_out], res[n_out:]


def _by_group(ref):
    return jnp.stack([ref[:, g * SCAN_WIDTH:(g + 1) * SCAN_WIDTH] for g in range(SCAN_GROUPS)])


def _store_groups(ref, val):
    for g in range(SCAN_GROUPS):
        ref[:, g * SCAN_WIDTH:(g + 1) * SCAN_WIDTH] = val[g]


def rwkv_scan_fwd(r, lw, k, v, kk, a, plan):
    def body(r_ref, lw_ref, k_ref, v_ref, kk_ref, a_ref, o_ref, st_ref, s_ref):
        @pl.when(pl.program_id(0) == 0)
        def _():
            s_ref[...] = jnp.zeros_like(s_ref)

        s0 = s_ref[...]
        st_ref[0] = s0
        o, s1 = jax.vmap(_rwkv_chunk)(s0, *(_by_group(ref) for ref in (r_ref, lw_ref, k_ref, v_ref, kk_ref, a_ref)))
        _store_groups(o_ref, o)
        s_ref[...] = s1

    tb = pl.BlockSpec((CHUNK, D_RWKV), lambda c: (c, 0))
    state = (SCAN_GROUPS, SCAN_WIDTH, SCAN_WIDTH)
    return call_with_comm(
        plan, 3 * N_CHUNKS // 4, body, (N_CHUNKS,), "rwkv_scan_fwd",
        [tb] * 6, [tb, pl.BlockSpec((1,) + state, lambda c: (c, 0, 0, 0))],
        [_sds((SEQ, D_RWKV)), _sds((N_CHUNKS,) + state)], [pltpu.VMEM(state, F32)], (r, lw, k, v, kk, a))


def rwkv_scan_bwd(r, lw, k, v, kk, a, states, d_o, plan):
    def body(r_ref, lw_ref, k_ref, v_ref, kk_ref, a_ref, st_ref, do_ref,
             dr_ref, dlw_ref, dk_ref, dv_ref, dkk_ref, da_ref, ds_ref):
        @pl.when(pl.program_id(0) == 0)
        def _():
            ds_ref[...] = jnp.zeros_like(ds_ref)

        _, vjp = jax.vjp(jax.vmap(_rwkv_chunk), st_ref[0],
                         *(_by_group(ref) for ref in (r_ref, lw_ref, k_ref, v_ref, kk_ref, a_ref)))
        grads = vjp((_by_group(do_ref), ds_ref[...]))
        ds_ref[...] = grads[0]
        for ref, val in zip((dr_ref, dlw_ref, dk_ref, dv_ref, dkk_ref, da_ref), grads[1:]):
            _store_groups(ref, val)

    last = N_CHUNKS - 1
    tb = pl.BlockSpec((CHUNK, D_RWKV), lambda c: (last - c, 0))
    state = (SCAN_GROUPS, SCAN_WIDTH, SCAN_WIDTH)
    return call_with_comm(
        plan, N_CHUNKS // 4, body, (N_CHUNKS,), "rwkv_scan_bwd",
        [tb] * 6 + [pl.BlockSpec((1,) + state, lambda c: (last - c, 0, 0, 0)), tb], [tb] * 6,
        [_sds((SEQ, D_RWKV))] * 6, [pltpu.VMEM(state, F32)], (r, lw, k, v, kk, a, states, d_o))


def _ffn_mid(ug, uv, cg, cv, bg, bv):
    conv_g = bg + cg[0] * shift_rows(ug, 2) + cg[1] * shift_rows(ug, 1) + cg[2] * ug
    conv_v = bv + cv[0] * shift_rows(uv, 2) + cv[1] * shift_rows(uv, 1) + cv[2] * uv
    return jax.nn.gelu(conv_g, approximate=True) * conv_v


def _conv_rows(ref):
    return tuple(ref[0, j:j + 1, :] for j in range(3))


def _ffn_specs(tile):
    per = D_MODEL // tile
    half = N_DEV // 2
    w_g = pl.BlockSpec((1, D_MODEL, tile), lambda t: (t // per, 0, t % per))
    w_v = pl.BlockSpec((1, D_MODEL, tile), lambda t: (half + t // per, 0, t % per))
    c_g = pl.BlockSpec((1, 3, tile), lambda t: (t // per, 0, t % per))
    c_v = pl.BlockSpec((1, 3, tile), lambda t: (half + t // per, 0, t % per))
    b_g = pl.BlockSpec((1, tile), lambda t: (0, t))
    b_v = pl.BlockSpec((1, tile), lambda t: (0, D_FF // tile + t))
    w_d = pl.BlockSpec((tile, D_MODEL), lambda t: (t, 0))
    return w_g, w_v, c_g, c_v, b_g, b_v, w_d


def ffn_fwd(h2, w_up, conv_w, conv_b, w_down):
    def body(h_ref, wg_ref, wv_ref, cg_ref, cv_ref, bg_ref, bv_ref, wd_ref, f_ref):
        @pl.when(pl.program_id(0) == 0)
        def _():
            f_ref[...] = jnp.zeros_like(f_ref)

        h = h_ref[...]
        act = _ffn_mid(_dot(h, wg_ref[0], NN), _dot(h, wv_ref[0], NN), _conv_rows(cg_ref), _conv_rows(cv_ref),
                       bg_ref[...], bv_ref[...])
        f_ref[...] += _dot(act.astype(BF16), wd_ref[...], NN)

    full = pl.BlockSpec((SEQ, D_MODEL), lambda t: (0, 0))
    return pl.pallas_call(
        body, grid=(D_FF // FF_TILE,), name="ffn_fwd",
        in_specs=[full, *_ffn_specs(FF_TILE)],
        out_specs=full, out_shape=_sds((SEQ, D_MODEL)),
        compiler_params=_params(("arbitrary",)),
    )(h2, w_up, w_up, conv_w, conv_w, conv_b, conv_b, w_down)


def ffn_bwd_mid(h2, w_up, conv_w, conv_b, w_down, df):
    tile, rows, halo = FF_TILE, FF_ROW_CHUNK, FF_HALO
    ext = rows + 2 * halo

    def body(h_hbm, wg_ref, wv_ref, cg_ref, cv_ref, bg_ref, bv_ref, wd_ref, df_hbm,
             dug_ref, duv_ref, dcg_ref, dcv_ref, dbg_ref, dbv_ref, dwd_ref,
             h_ref, df_ref, ug_ref, uv_ref, da_ref, act_ref):
        @pl.when(pl.program_id(0) == 0)
        def _():
            pltpu.sync_copy(h_hbm, h_ref)
            pltpu.sync_copy(df_hbm, df_ref)
            for ref in (ug_ref, uv_ref, da_ref):
                ref[0:halo, :] = jnp.zeros((halo, tile), F32)
                ref[halo + SEQ:, :] = jnp.zeros((halo, tile), F32)

        h, df_b = h_ref[...], df_ref[...]
        ug_ref[halo:halo + SEQ, :] = _dot(h, wg_ref[0], NN)
        uv_ref[halo:halo + SEQ, :] = _dot(h, wv_ref[0], NN)
        da_ref[halo:halo + SEQ, :] = _dot(df_b, wd_ref[...], NT)
        cg, cv, bg, bv = _conv_rows(cg_ref), _conv_rows(cv_ref), bg_ref[...], bv_ref[...]
        down = lambda x, n: pltpu.roll(x, n, 0)
        up = lambda x, n: pltpu.roll(x, ext - n, 0)
        mid = slice(halo, halo + rows)

        def chunk(i, sums):
            r0 = pl.multiple_of(i * rows, rows)
            window = pl.ds(r0, ext)
            ug, uv, da = ug_ref[window, :], uv_ref[window, :], da_ref[window, :]
            ug1, ug2, uv1, uv2 = down(ug, 1), down(ug, 2), down(uv, 1), down(uv, 2)
            conv_g = bg + cg[0] * ug2 + cg[1] * ug1 + cg[2] * ug
            conv_v = bv + cv[0] * uv2 + cv[1] * uv1 + cv[2] * uv
            act, vjp = jax.vjp(lambda a, b: jax.nn.gelu(a, approximate=True) * b, conv_g, conv_v)
            dcg, dcv = vjp(da)
            dug = cg[2] * dcg + cg[1] * up(dcg, 1) + cg[0] * up(dcg, 2)
            duv = cv[2] * dcv + cv[1] * up(dcv, 1) + cv[0] * up(dcv, 2)
            out = pl.ds(r0, rows)
            act_ref[out, :] = act[mid].astype(BF16)
            dug_ref[out, :] = dug[mid].astype(BF16)
            duv_ref[out, :] = duv[mid].astype(BF16)
            col = lambda x: jnp.sum(x[mid], axis=0, keepdims=True)
            new = (col(dcg * ug2), col(dcg * ug1), col(dcg * ug), col(dcv * uv2), col(dcv * uv1), col(dcv * uv),
                   col(dcg), col(dcv))
            return tuple(s + n for s, n in zip(sums, new))

        zero = jnp.zeros((1, tile), F32)
        sums = lax.fori_loop(0, SEQ // rows, chunk, (zero,) * 8)
        for j in range(3):
            dcg_ref[0, j:j + 1, :] = sums[j]
            dcv_ref[0, j:j + 1, :] = sums[3 + j]
        dbg_ref[...] = sums[6]
        dbv_ref[...] = sums[7]
        dwd_ref[...] = _dot(act_ref[...], df_b, TN).astype(BF16)

    hbm = pl.BlockSpec(memory_space=pl.ANY)
    w_g, w_v, c_g, c_v, b_g, b_v, w_d = _ffn_specs(tile)
    col = pl.BlockSpec((SEQ, tile), lambda t: (0, t))
    padded = pltpu.VMEM((SEQ + 2 * halo, tile), F32)
    return pl.pallas_call(
        body, grid=(D_FF // tile,), name="ffn_bwd_mid",
        in_specs=[hbm, w_g, w_v, c_g, c_v, b_g, b_v, w_d, hbm],
        out_specs=[col, col, c_g, c_v, b_g, b_v, w_d],
        out_shape=[_sds((SEQ, D_FF), BF16), _sds((SEQ, D_FF), BF16), _sds((N_DEV, 3, D_MODEL)),
                   _sds((N_DEV, 3, D_MODEL)), _sds((1, 2 * D_FF)), _sds((1, 2 * D_FF)), _sds((D_FF, D_MODEL), BF16)],
        scratch_shapes=[pltpu.VMEM((SEQ, D_MODEL), BF16), pltpu.VMEM((SEQ, D_MODEL), BF16), padded, padded, padded,
                        pltpu.VMEM((SEQ, tile), BF16)],
        compiler_params=_params(("arbitrary",)),
    )(h2, w_up, w_up, conv_w, conv_w, conv_b, conv_b, w_down, df)


def ffn_bwd_up(h2, w_up, dug, duv):
    tile = FF_TILE
    per = D_MODEL // tile

    def body(h_hbm, wg_ref, wv_ref, dug_ref, duv_ref, dh_hbm, dup_hbm, h_ref, dh_ref, dwg_ref, dwv_ref, sem, up_sems):
        t = pl.program_id(0)

        @pl.when(t == 0)
        def _():
            pltpu.sync_copy(h_hbm, h_ref)
            dh_ref[...] = jnp.zeros_like(dh_ref)

        h, dug_b, duv_b = h_ref[...], dug_ref[...], duv_ref[...]
        cols = pl.ds(pl.multiple_of((t % per) * tile, tile), tile)
        to_gate = pltpu.make_async_copy(dwg_ref, dup_hbm.at[t // per, :, cols], up_sems.at[0])
        to_value = pltpu.make_async_copy(dwv_ref, dup_hbm.at[N_DEV // 2 + t // per, :, cols], up_sems.at[1])
        dwg_ref[...] = _dot(h, dug_b, TN).astype(BF16)
        to_gate.start()
        dwv_ref[...] = _dot(h, duv_b, TN).astype(BF16)
        to_value.start()
        dh_ref[...] += _dot(jnp.concatenate([dug_b, duv_b], axis=1),
                            jnp.concatenate([wg_ref[0], wv_ref[0]], axis=1), NT)
        to_gate.wait()
        to_value.wait()

        @pl.when(t == D_FF // tile - 1)
        def _():
            cp = pltpu.make_async_copy(dh_ref, dh_hbm, sem)
            cp.start()
            cp.wait()

    hbm = pl.BlockSpec(memory_space=pl.ANY)
    w_g, w_v = _ffn_specs(tile)[:2]
    col = pl.BlockSpec((SEQ, tile), lambda t: (0, t))
    return pl.pallas_call(
        body, grid=(D_FF // tile,), name="ffn_bwd_up",
        in_specs=[hbm, w_g, w_v, col, col], out_specs=[hbm, hbm],
        out_shape=[_sds((SEQ, D_MODEL)), _sds((N_DEV, D_MODEL, D_MODEL), BF16)],
        scratch_shapes=[pltpu.VMEM((SEQ, D_MODEL), BF16), pltpu.VMEM((SEQ, D_MODEL), F32),
                        pltpu.VMEM((D_MODEL, tile), BF16), pltpu.VMEM((D_MODEL, tile), BF16),
                        pltpu.SemaphoreType.DMA, pltpu.SemaphoreType.DMA((2,))],
        compiler_params=_params(("arbitrary",)),
    )(h2, w_up, w_up, dug, duv)


def loss_head(x1, f, target, n_post):
    def tile_loss(x1_t, f_t, g, tgt):
        err = x1_t + _rms(f_t, g) - tgt
        return 0.5 * jnp.sum(jnp.mean(err * err, axis=-1))

    def body(x_ref, f_ref, t_ref, g_ref, dx_ref, df_ref, dg_ref, loss_ref):
        val, (dx, df, dg) = jax.value_and_grad(tile_loss, argnums=(0, 1, 2))(
            x_ref[...], f_ref[...], g_ref[...], t_ref[...])
        dx_ref[...] = dx
        df_ref[...] = df.astype(BF16)

        @pl.when(pl.program_id(0) == 0)
        def _():
            dg_ref[...] = jnp.zeros_like(dg_ref)
            loss_ref[...] = jnp.zeros_like(loss_ref)

        dg_ref[...] += dg
        loss_ref[...] += jnp.full((1, LANES), val, F32)

    tile = pl.BlockSpec((TOK_TILE, D_MODEL), lambda i: (i, 0))
    vec = pl.BlockSpec((1, D_MODEL), lambda i: (0, 0))
    return pl.pallas_call(
        body, grid=(SEQ // TOK_TILE,), name="loss_head",
        in_specs=[tile, tile, tile, vec],
        out_specs=[tile, tile, vec, pl.BlockSpec((1, LANES), lambda i: (0, 0))],
        out_shape=[_sds((SEQ, D_MODEL)), _sds((SEQ, D_MODEL), BF16), _sds((1, D_MODEL)), _sds((1, LANES))],
        compiler_params=_params(("arbitrary",)),
    )(x1, f, target, n_post)


def _mesh_pos():
    return lax.axis_index("x"), lax.axis_index("y"), lax.axis_index("c")


def _flip(pos, rel):
    x, y, c = pos
    return (1 - x if rel & 4 else x, 1 - y if rel & 2 else y, 1 - c if rel & 1 else c)


def _slot(pos):
    x, y, c = pos
    return 4 * x + 2 * y + c


def cast_bf16(w, rows):
    def body(w_ref, o_ref):
        o_ref[...] = w_ref[...].astype(BF16)

    spec = pl.BlockSpec((rows, w.shape[1]), lambda i: (i, 0))
    return pl.pallas_call(body, grid=(w.shape[0] // rows,), name="cast_bf16_%dx%d" % w.shape,
                          in_specs=[spec], out_specs=spec, out_shape=_sds(w.shape, BF16),
                          compiler_params=_params(("arbitrary",)))(w)


class CommPlan:
    def __init__(self, ins, out_shape, scratch, stages):
        self.ins, self.out_shape, self.scratch, self.stages = ins, out_shape, scratch, stages


def run_comm(name, plan):
    n_in, n_out = len(plan.ins), len(plan.out_shape)

    def body(*refs):
        for stage in plan.stages(refs[:n_in], refs[n_in:n_in + n_out], refs[n_in + n_out:]):
            stage()

    any_spec = pl.BlockSpec(memory_space=pl.ANY)
    return pl.pallas_call(
        body, name=name, in_specs=[any_spec] * len(plan.ins), out_specs=[any_spec] * len(plan.out_shape),
        out_shape=plan.out_shape, scratch_shapes=plan.scratch)(*plan.ins)


def gather_plan(shards):
    n = len(shards)

    def stages(srcs, outs, sems):
        send_sems, recv_sems, local_sems = sems

        def places():
            me = _mesh_pos()
            return me, _flip(me, 1), [_flip(me, 2), _flip(me, 4), _flip(me, 6)]

        def copy(a, k, block, to, src=None):
            dst = outs[a].at[_slot(block)]
            return pltpu.make_async_remote_copy(
                src_ref=dst if src is None else src, dst_ref=dst,
                send_sem=send_sems.at[7 * a + k], recv_sem=recv_sems.at[7 * a + k],
                device_id=to, device_id_type=pl.DeviceIdType.MESH)

        def local(a, me):
            return pltpu.make_async_copy(srcs[a], outs[a].at[_slot(me)], local_sems.at[a])

        def own(a, me, sibling, chips):
            return [copy(a, 0, me, sibling, src=srcs[a])] + [
                copy(a, 1 + j, me, chip, src=srcs[a]) for j, chip in enumerate(chips)]

        def start():
            me, sibling, chips = places()
            for a in range(n):
                local(a, me).start()
                for cp in own(a, me, sibling, chips):
                    cp.start()

        def forward():
            me, sibling, chips = places()
            for j, chip in enumerate(chips):
                for a in range(n):
                    copy(a, 1 + j, chip, me).wait_recv()
                    copy(a, 4 + j, chip, sibling).start()

        def finish():
            me, sibling, chips = places()
            for a in range(n):
                copy(a, 0, sibling, me).wait_recv()
                for j, chip in enumerate(chips):
                    copy(a, 4 + j, _flip(chip, 1), me).wait_recv()
            for a in range(n):
                for cp in own(a, me, sibling, chips):
                    cp.wait_send()
                for j, chip in enumerate(chips):
                    copy(a, 4 + j, chip, sibling).wait_send()
                local(a, me).wait()

        return start, forward, finish

    return CommPlan(list(shards), [_sds((N_DEV,) + s.shape, s.dtype) for s in shards],
                    [pltpu.SemaphoreType.DMA((7 * n,)), pltpu.SemaphoreType.DMA((7 * n,)),
                     pltpu.SemaphoreType.DMA((n,))], stages)


def exchange_plan(parts, replicated, rels, members, index, member_axis, own_copy):
    n, nr = len(parts), len(rels)
    pick_index = (slice(None),) * member_axis + (0,)
    subs = [1 if (r or member_axis == 0) else p.shape[0] for p, r in zip(parts, replicated)]
    first = [sum(subs[:a]) for a in range(n)]
    total = sum(subs)

    def stages(srcs, outs, sems):
        send_sems, recv_sems, local_sems = sems

        def src(a, s, pos):
            if replicated[a]:
                return srcs[a]
            return srcs[a].at[index(pos)] if member_axis == 0 else srcs[a].at[s, index(pos)]

        def dst(a, s, pos):
            block = outs[a].at[index(pos)]
            return block if (replicated[a] or member_axis == 0) else block.at[s]

        def copy(a, s, j, me, src_pos, dst_pos):
            sem = nr * (first[a] + s) + j
            return pltpu.make_async_remote_copy(
                src_ref=src(a, s, src_pos), dst_ref=dst(a, s, dst_pos),
                send_sem=send_sems.at[sem], recv_sem=recv_sems.at[sem],
                device_id=_flip(me, rels[j]), device_id_type=pl.DeviceIdType.MESH)

        pieces = [(a, s) for a in range(n) for s in range(subs[a])]

        def local(a, s, me):
            return pltpu.make_async_copy(src(a, s, me), dst(a, s, me), local_sems.at[first[a] + s])

        def sends(me):
            return [copy(a, s, j, me, _flip(me, rels[j]), me) for j in range(nr) for a, s in pieces]

        own = pieces if own_copy else []

        def start():
            me = _mesh_pos()
            for cp in sends(me) + [local(a, s, me) for a, s in own]:
                cp.start()

        def middle():
            pass

        def finish():
            me = _mesh_pos()
            for j in range(nr):
                for a, s in pieces:
                    copy(a, s, j, me, me, _flip(me, rels[j])).wait_recv()
            for cp in sends(me):
                cp.wait_send()
            for a, s in own:
                local(a, s, me).wait()

        return start, middle, finish

    shapes = [p.shape if r else jax.eval_shape(lambda t: t[pick_index], p).shape for p, r in zip(parts, replicated)]
    return CommPlan(list(parts), [_sds((members,) + s, p.dtype) for s, p in zip(shapes, parts)],
                    [pltpu.SemaphoreType.DMA((nr * total,)), pltpu.SemaphoreType.DMA((nr * total,)),
                     pltpu.SemaphoreType.DMA((total,))], stages)


def pair_plan(parts, replicated):
    return exchange_plan(parts, replicated, [1], 2, lambda pos: pos[2], 1, False)


def chip_plan(parts, replicated):
    return exchange_plan(parts, replicated, [2, 4, 6], 4, lambda pos: 2 * pos[0] + pos[1], 0, True)


def add_pair(name, mine, swapped, out_dtype, rows):
    def body(m_ref, s_ref, o_ref):
        own = m_ref[0, 0] if mine.ndim == 4 else m_ref[0]
        o_ref[0] = (own.astype(F32) + s_ref[0, 0].astype(F32)).astype(o_ref.dtype)

    _, n, r, c = swapped.shape
    core = lambda: lax.axis_index("c")
    if mine.ndim == 4:
        mine_spec = pl.BlockSpec((1, 1, rows, c), lambda i, j: (i, core(), j, 0))
    else:
        mine_spec = pl.BlockSpec((1, rows, c), lambda i, j: (i, j, 0))
    return pl.pallas_call(
        body, grid=(n, r // rows), name=name,
        in_specs=[mine_spec, pl.BlockSpec((1, 1, rows, c), lambda i, j: (1 - core(), i, j, 0))],
        out_specs=pl.BlockSpec((1, rows, c), lambda i, j: (i, j, 0)),
        out_shape=_sds((n, r, c), out_dtype),
        compiler_params=_params(("arbitrary", "arbitrary")),
    )(mine, swapped)


def add_pair_small(mines, swappeds):
    n = len(mines)
    halves = [m.ndim == s.ndim for m, s in zip(mines, swappeds)]

    def body(*refs):
        c = lax.axis_index("c")
        for i in range(n):
            m_ref, s_ref, o_ref = refs[i], refs[n + i], refs[2 * n + i]
            o_ref[...] = (m_ref[:, c] if halves[i] else m_ref[...]) + s_ref[1 - c]

    vmem = pl.BlockSpec(memory_space=pltpu.VMEM)
    return pl.pallas_call(
        body, name="pair_add_small", in_specs=[vmem] * (2 * n), out_specs=[vmem] * n,
        out_shape=[_sds(s.shape[1:]) for s in swappeds], compiler_params=_params(),
    )(*mines, *swappeds)


def _adamw_math(w, g, m, v):
    nm = ADAM_B1 * m + (1.0 - ADAM_B1) * g
    nv = ADAM_B2 * v + (1.0 - ADAM_B2) * (g * g)
    m_hat = nm / (1.0 - ADAM_B1 ** ADAM_STEP)
    v_hat = nv / (1.0 - ADAM_B2 ** ADAM_STEP)
    return -ADAM_LR * (m_hat / (jnp.sqrt(v_hat) + ADAM_EPS) + ADAM_WD * w), nm, nv


def adamw_small(ws, parts, ms, vs):
    n = len(ws)

    def body(*refs):
        for i in range(n):
            w_ref, p_ref, m_ref, v_ref = (refs[k * n + i] for k in range(4))
            g = p_ref[0]
            for j in range(1, p_ref.shape[0]):
                g = g + p_ref[j]
            delta, nm, nv = _adamw_math(w_ref[...], g, m_ref[...], v_ref[...])
            for k, val in enumerate((g, delta, nm, nv)):
                refs[(4 + k) * n + i][...] = val

    vmem = pl.BlockSpec(memory_space=pltpu.VMEM)
    outs = pl.pallas_call(
        body, name="adamw_small", in_specs=[vmem] * (4 * n), out_specs=[vmem] * (4 * n),
        out_shape=[_sds(w.shape) for w in ws] * 4, compiler_params=_params(),
    )(*ws, *parts, *ms, *vs)
    return [outs[k * n:(k + 1) * n] for k in range(4)]

def adamw(name, w, parts, m, v, rows, plan=None):
    n_parts = parts.shape[0]

    def body(w_ref, p_ref, m_ref, v_ref, g_ref, d_ref, nm_ref, nv_ref):
        g = p_ref[0].astype(F32)
        for j in range(1, n_parts):
            g = g + p_ref[j].astype(F32)
        g_ref[...] = g
        d_ref[...], nm_ref[...], nv_ref[...] = _adamw_math(w_ref[...], g, m_ref[...], v_ref[...])

    cols = w.shape[1]
    spec = pl.BlockSpec((rows, cols), lambda i: (i, 0))
    grid = (w.shape[0] // rows,)
    in_specs = [spec, pl.BlockSpec((n_parts, rows, cols), lambda i: (0, i, 0)), spec, spec]
    if plan is not None:
        return call_with_comm(plan, 0, body, grid, name, in_specs, [spec] * 4, [_sds(w.shape)] * 4, [],
                              (w, parts, m, v))
    return pl.pallas_call(
        body, grid=grid, name=name, in_specs=in_specs, out_specs=[spec] * 4, out_shape=[_sds(w.shape)] * 4,
        compiler_params=_params(("arbitrary",)),
    )(w, parts, m, v)


def _to_slots(full, per):
    return full.reshape(full.shape[0], N_DEV, per).transpose(1, 0, 2)


def _from_slots(slots):
    return slots.transpose(1, 0, 2).reshape(slots.shape[1], -1)


def kernel(x, norm_mix_pre, norm_mix_post, norm_ffn_pre, norm_ffn_post, w_in, rel_bias, sinks, rwkv_shift_mix, w0, w_decay_up, a0, w_iclr_up, w_gate_up, k_k, k_a, r_k, ln_x_g, ln_x_b, w_out, w_ffn_up, conv_w, conv_b, w_ffn_down, loss_target, m_norm_mix_pre, m_norm_mix_post, m_norm_ffn_pre, m_norm_ffn_post, m_w_in, m_rel_bias, m_sinks, m_rwkv_shift_mix, m_w0, m_w_decay_up, m_a0, m_w_iclr_up, m_w_gate_up, m_k_k, m_k_a, m_r_k, m_ln_x_g, m_ln_x_b, m_w_out, m_w_ffn_up, m_conv_w, m_conv_b, m_w_ffn_down, v_norm_mix_pre, v_norm_mix_post, v_norm_ffn_pre, v_norm_ffn_post, v_w_in, v_rel_bias, v_sinks, v_rwkv_shift_mix, v_w0, v_w_decay_up, v_a0, v_w_iclr_up, v_w_gate_up, v_k_k, v_k_a, v_r_k, v_ln_x_g, v_ln_x_b, v_w_out, v_w_ffn_up, v_conv_w, v_conv_b, v_w_ffn_down):
    x2 = x[0]
    target = loss_target[0]

    g_in, g_out, g_decay, g_iclr, g_gate, g_conv = run_comm("all_gather_mixer", gather_plan([
        cast_bf16(w_in[0], 256), cast_bf16(w_out[0], 128), w_decay_up[0], w_iclr_up[0], w_gate_up[0], conv_w[0]]))
    up_gather = gather_plan([cast_bf16(w_ffn_up[0], 256)])
    down_gather = gather_plan([cast_bf16(w_ffn_down[0], 256)])
    w_in_b = _from_slots(g_in)
    w_out_b = g_out.reshape(D_MODEL, D_MODEL)
    lora = jnp.zeros((HEAD_DIM, D_RWKV), F32)
    wd_pad = jnp.concatenate([_from_slots(g_decay), lora], axis=0)
    wi_pad = jnp.concatenate([lora, _from_slots(g_iclr)], axis=0)
    wg_full = _from_slots(g_gate)
    mix_ext = jnp.concatenate([jnp.zeros((1, D_QKV), F32), rwkv_shift_mix], axis=1)
    r_k_row = r_k.reshape(1, D_RWKV)
    bucket = _bucket_table()

    (h1,) = tok_fwd("rms_mix_pre", rms_tile, [x2], [norm_mix_pre], [], [D_MODEL], [BF16])
    proj, ps = in_proj_fwd(h1, w_in_b, mix_ext)
    attn, (g_down,) = attn_fwd(proj, rel_bias, bucket, sinks, down_gather)
    pre_params = [w0, wd_pad, a0, wi_pad, wg_full, k_k, k_a]
    r_, lw_, k2_, v_, kk_, a_, gate_ = tok_fwd("rwkv_pre", rwkv_pre_tile, [ps], pre_params, [],
                                               [D_RWKV] * 7, [F32] * 7)
    (o_, states), (g_up,) = rwkv_scan_fwd(r_, lw_, k2_, v_, kk_, a_, up_gather)
    w_down_b = g_down.reshape(D_FF, D_MODEL)
    mix_tiles = [o_, r_, k2_, v_, gate_, attn, x2]
    mix_params = [w_out_b, norm_mix_post, ln_x_g, ln_x_b, r_k_row, norm_ffn_pre]
    x1, h2 = tok_fwd("mix_out", mix_out_tile, mix_tiles, mix_params, [(D_MODEL, D_MODEL)], [D_MODEL, D_MODEL],
                     [F32, BF16])
    f = ffn_fwd(h2, g_up, g_conv, conv_b, w_down_b)
    dy, df, d_n_ffn_post, loss_row = loss_head(x1, f, target, norm_ffn_post)

    d_ug, d_uv, d_cw_g, d_cw_v, d_cb_g, d_cb_v, d_down = ffn_bwd_mid(h2, g_up, g_conv, conv_b, w_down_b, df)
    dh2, d_up = ffn_bwd_up(h2, g_up, d_ug, d_uv)
    half = N_DEV // 2
    d_cw = jnp.concatenate([d_cw_g[:half], d_cw_v[half:]], axis=0)
    by_pair = lambda slots: slots.reshape((N_DEV // 2, 2) + slots.shape[1:])
    ffn_mine = [by_pair(d_up), by_pair(d_down.reshape(N_DEV, D_FF // N_DEV, D_MODEL))]
    ffn_swapped = run_comm("pair_exchange_ffn", pair_plan(ffn_mine, [False, False]))
    ffn_exchange = chip_plan([add_pair("pair_add_w_ffn_up", ffn_mine[0], ffn_swapped[0], BF16, 256),
                              add_pair("pair_add_w_ffn_down", ffn_mine[1], ffn_swapped[1], BF16, 256)],
                             [False, False])
    d_cb = jnp.concatenate([d_cb_g[:, :D_FF], d_cb_v[:, D_FF:]], axis=1)
    (d_o, d_r1, d_k1, d_v1, d_gate, d_attn, dx_res, d_n_mix_post, d_ln_g, d_ln_b, d_r_k, d_n_ffn_pre,
     d_w_out) = tok_bwd("mix_out_bwd", mix_out_tile, mix_tiles, mix_params, [(D_MODEL, D_MODEL)], [dy, dh2],
                        [1, 2, 3, 4, 5])
    (d_r2, d_lw, d_k2, d_v2, d_kk, d_a), (got_up, got_down) = rwkv_scan_bwd(
        r_, lw_, k2_, v_, kk_, a_, states, d_o, ffn_exchange)
    pre_cots = [(d_r1, d_r2), d_lw, (d_k1, d_k2), (d_v1, d_v2), d_kk, d_a, d_gate]
    (d_ps, d_w0, d_wd_pad, d_a0, d_wi_pad, d_wg, d_k_k, d_k_a) = tok_bwd(
        "rwkv_pre_bwd", rwkv_pre_tile, [ps], pre_params, [], pre_cots, [0, 1, 2, 3, 4, 5, 6])
    dq, dkc, dkp, dvc, dvp, d_rel_bias, d_sinks = attn_bwd(proj, rel_bias, bucket, sinks, d_attn)
    zero_blk = jnp.zeros((BLOCK, D_KV), F32)
    dk = dkc + jnp.concatenate([dkp[BLOCK:], zero_blk], axis=0)
    dv = dvc + jnp.concatenate([dvp[BLOCK:], zero_blk], axis=0)
    dpa = jnp.concatenate([dq, dk, dv, d_ps], axis=1)
    dh1, d_w_in, d_mix_ext = in_proj_bwd(h1, w_in_b, mix_ext, proj, dpa)
    grad_x2, d_n_mix_pre = tok_bwd("rms_mix_pre_bwd", rms_tile, [x2], [norm_mix_pre], [], [dh1], [0], {0: dx_res})
    grad_x = grad_x2[None]

    small_rep = [d_n_mix_pre, d_n_mix_post, d_n_ffn_pre, d_n_ffn_post, d_rel_bias, d_sinks,
                 d_mix_ext[:, D_QKV:], d_w0, d_a0, d_k_k, d_k_a, d_r_k.reshape(r_k.shape), d_ln_g, d_ln_b, d_cb]
    rep_w = [norm_mix_pre, norm_mix_post, norm_ffn_pre, norm_ffn_post, rel_bias, sinks, rwkv_shift_mix,
             w0, a0, k_k, k_a, r_k, ln_x_g, ln_x_b, conv_b]
    rep_m = [m_norm_mix_pre, m_norm_mix_post, m_norm_ffn_pre, m_norm_ffn_post, m_rel_bias, m_sinks,
             m_rwkv_shift_mix, m_w0, m_a0, m_k_k, m_k_a, m_r_k, m_ln_x_g, m_ln_x_b, m_conv_b]
    rep_v = [v_norm_mix_pre, v_norm_mix_post, v_norm_ffn_pre, v_norm_ffn_post, v_rel_bias, v_sinks,
             v_rwkv_shift_mix, v_w0, v_a0, v_k_k, v_k_a, v_r_k, v_ln_x_g, v_ln_x_b, v_conv_b]
    sh_w = [w_decay_up, w_iclr_up, w_gate_up, conv_w]
    sh_m = [m_w_decay_up, m_w_iclr_up, m_w_gate_up, m_conv_w]
    sh_v = [v_w_decay_up, v_w_iclr_up, v_w_gate_up, v_conv_w]
    sh_parts = [_to_slots(d_wd_pad[:HEAD_DIM], HEAD_DIM), _to_slots(d_wi_pad[HEAD_DIM:], HEAD_DIM),
                _to_slots(d_wg, HEAD_DIM), d_cw]
    n_rep, n_sh = len(small_rep), len(sh_parts)
    mine = [by_pair(_to_slots(d_w_in, D_IN // N_DEV)), by_pair(d_w_out.reshape(N_DEV, D_MODEL // N_DEV, D_MODEL)),
            *small_rep, *(by_pair(p) for p in sh_parts), loss_row]
    is_rep = [False, False] + [True] * n_rep + [False] * n_sh + [True]
    adam_down, swapped = adamw("adamw_w_ffn_down", w_ffn_down[0], got_down, m_w_ffn_down[0], v_w_ffn_down[0], 128,
                               pair_plan(mine, is_rep))
    chip_sums = [add_pair("pair_add_w_in", mine[0], swapped[0], BF16, 512),
                 add_pair("pair_add_w_out", mine[1], swapped[1], BF16, 128),
                 *add_pair_small(mine[2:], swapped[2:])]
    adam_up, got = adamw("adamw_w_ffn_up", w_ffn_up[0], got_up, m_w_ffn_up[0], v_w_ffn_up[0], 128,
                         chip_plan(chip_sums, is_rep))

    big = [adamw("adamw_w_in", w_in[0], got[0], m_w_in[0], v_w_in[0], 256),
           adamw("adamw_w_out", w_out[0], got[1], m_w_out[0], v_w_out[0], 128), adam_up, adam_down]
    loss = functools.reduce(jnp.add, [got[-1][q, 0, 0] for q in range(N_DEV // 2)])
    small_w, small_g = rep_w + sh_w, got[2:-1]
    as_grad = lambda arrays: [a.reshape(g.shape[1:]) for a, g in zip(arrays, small_g)]
    small = adamw_small(as_grad(small_w), small_g, as_grad(rep_m + sh_m), as_grad(rep_v + sh_v))
    small = [[a.reshape(w.shape) for a, w in zip(kind, small_w)] for kind in small]

    names = ["norm_mix_pre", "norm_mix_post", "norm_ffn_pre", "norm_ffn_post", "w_in", "rel_bias", "sinks",
             "rwkv_shift_mix", "w0", "w_decay_up", "a0", "w_iclr_up", "w_gate_up", "k_k", "k_a", "r_k",
             "ln_x_g", "ln_x_b", "w_out", "w_ffn_up", "conv_w", "conv_b", "w_ffn_down"]
    small_names = ["norm_mix_pre", "norm_mix_post", "norm_ffn_pre", "norm_ffn_post", "rel_bias", "sinks",
                   "rwkv_shift_mix", "w0", "a0", "k_k", "k_a", "r_k", "ln_x_g", "ln_x_b", "conv_b",
                   "w_decay_up", "w_iclr_up", "w_gate_up", "conv_w"]
    big_names = {"w_in": 0, "w_out": 1, "w_ffn_up": 2, "w_ffn_down": 3}
    outs = []
    for kind in range(4):
        for nm in names:
            if nm in big_names:
                outs.append(big[big_names[nm]][kind][None])
            else:
                outs.append(small[kind][small_names.index(nm)])
    return (loss, grad_x, *outs)
```

```python
import functools
import math

import jax
import jax.numpy as jnp
from jax import lax
from jax.experimental import pallas as pl
from jax.experimental.pallas import tpu as pltpu

F32 = jnp.float32
BF16 = jnp.bfloat16

N_DEV = 8
SEQ = 2048
D_MODEL = 1024
HEAD_DIM = 64
D_ATTN = 512
D_KV = 128
D_RWKV = 512
N_HEADS = 8
RWKV_COLS = 1792
D_QKV = D_ATTN + 2 * D_KV
D_IN = D_QKV + RWKV_COLS
D_FF = 4096
BLOCK = 128
N_BLOCKS = SEQ // BLOCK
N_BUCKETS = 32
MAX_DISTANCE = 128
NORM_EPS = 1e-6
GN_EPS = 64e-5
NEG_INF = -1e30
CHUNK = 64
N_CHUNKS = SEQ // CHUNK
SCAN_GROUPS = 4
SCAN_WIDTH = D_RWKV // SCAN_GROUPS
TOK_TILE = 256
FF_TILE = 256
FF_ROW_CHUNK = 256
FF_HALO = 8
COL_TILE = 256
LANES = 128
VMEM_LIMIT = 56 * 1024 * 1024

ADAM_LR = 0.001
ADAM_B1 = 0.9
ADAM_B2 = 0.999
ADAM_EPS = 1e-08
ADAM_WD = 0.01
ADAM_STEP = 10

NT = ((1,), (1,))
TN = ((0,), (0,))
NN = ((1,), (0,))


def _sds(shape, dtype=F32):
    return jax.ShapeDtypeStruct(shape, dtype)


def _params(sem=None):
    if sem is None:
        return pltpu.CompilerParams(vmem_limit_bytes=VMEM_LIMIT)
    return pltpu.CompilerParams(dimension_semantics=sem, vmem_limit_bytes=VMEM_LIMIT)


def _dot(a, b, dims):
    return lax.dot_general(a, b, (dims, ((), ())), preferred_element_type=F32)


def _split2(x):
    hi = x.astype(BF16)
    return hi, (x - hi.astype(F32)).astype(BF16)


def _dot3_raw(a, b, dims):
    ah, al = _split2(a)
    bh, bl = _split2(b)
    return _dot(ah, bh, dims) + (_dot(al, bh, dims) + _dot(ah, bl, dims))


@functools.partial(jax.custom_vjp, nondiff_argnums=(2,))
def dot3(a, b, dims):
    return _dot3_raw(a, b, dims)


def _dot3_fwd(a, b, dims):
    return _dot3_raw(a, b, dims), (a, b)


def _dot3_bwd(dims, res, g):
    a, b = res
    if dims == NN:
        return dot3(g, b, NT), dot3(a, g, TN)
    if dims == NT:
        return dot3(g, b, NN), dot3(g, a, TN)
    return dot3(b, g, NT), dot3(a, g, NN)


dot3.defvjp(_dot3_fwd, _dot3_bwd)


@functools.partial(jax.custom_vjp, nondiff_argnums=(2,))
def dot1(a, b, dims):
    return _dot(a.astype(BF16), b.astype(BF16), dims)


def _dot1_fwd(a, b, dims):
    return dot1(a, b, dims), (a, b)


def _dot1_bwd(dims, res, g):
    a, b = res
    if dims == NN:
        return dot1(g, b, NT), dot1(a, g, TN)
    if dims == NT:
        return dot1(g, b, NN), dot1(g, a, TN)
    return dot1(b, g, NT), dot1(a, g, NN)


dot1.defvjp(_dot1_fwd, _dot1_bwd)


@jax.custom_vjp
def mm(a, b):
    return _dot(a.astype(BF16), b.astype(BF16), NN)


def _mm_fwd(a, b):
    return mm(a, b), (a, b)


def _mm_bwd(res, g):
    a, b = res
    gb = g.astype(BF16)
    return _dot(gb, b.astype(BF16), NT).astype(a.dtype), _dot(a.astype(BF16), gb, TN).astype(b.dtype)


mm.defvjp(_mm_fwd, _mm_bwd)


@jax.custom_vjp
def mm_nt(a, b):
    return _dot(a.astype(BF16), b.astype(BF16), NT)


def _mm_nt_fwd(a, b):
    return mm_nt(a, b), (a, b)


def _mm_nt_bwd(res, g):
    a, b = res
    gb = g.astype(BF16)
    return _dot(gb, b.astype(BF16), NN).astype(a.dtype), _dot(gb, a.astype(BF16), TN).astype(b.dtype)


mm_nt.defvjp(_mm_nt_fwd, _mm_nt_bwd)


@jax.custom_vjp
def mmw(a, w, wz):
    return _dot(a.astype(BF16), w, NN)


def _mmw_fwd(a, w, wz):
    return mmw(a, w, wz), (a, w)


def _mmw_bwd(res, g):
    a, w = res
    gb = g.astype(BF16)
    return _dot(gb, w, NT).astype(a.dtype), jnp.zeros_like(w), _dot(a.astype(BF16), gb, TN)


mmw.defvjp(_mmw_fwd, _mmw_bwd)


def _shift_raw(x, n):
    rows = x.shape[0]
    rolled = pltpu.roll(x, n % rows, 0)
    idx = lax.broadcasted_iota(jnp.int32, x.shape, 0)
    keep = idx >= n if n > 0 else idx < rows + n
    return jnp.where(keep, rolled, 0.0)


@functools.partial(jax.custom_vjp, nondiff_argnums=(1,))
def shift_rows(x, n):
    return _shift_raw(x, n)


def _shift_fwd(x, n):
    return _shift_raw(x, n), None


def _shift_bwd(n, _, g):
    return (_shift_raw(g, -n),)


shift_rows.defvjp(_shift_fwd, _shift_bwd)


def _head_matrix(scale):
    a = lax.broadcasted_iota(jnp.int32, (D_RWKV, D_RWKV), 0) // HEAD_DIM
    b = lax.broadcasted_iota(jnp.int32, (D_RWKV, D_RWKV), 1) // HEAD_DIM
    return jnp.where(a == b, scale, 0.0).astype(F32)


def _rms(x, g):
    return x * lax.rsqrt(jnp.mean(x * x, axis=-1, keepdims=True) + NORM_EPS) * g


def _softplus(x):
    return jnp.maximum(x, 0.0) + jnp.log(1.0 + jnp.exp(-jnp.abs(x)))


def _tile_spec(arr, tm):
    return pl.BlockSpec((tm, arr.shape[1]), lambda i: (i, 0))


def _full_spec(arr):
    nd = arr.ndim
    return pl.BlockSpec(arr.shape, lambda i: (0,) * nd)


def tok_fwd(name, fn, tiles, params, zero_shapes, out_widths, out_dtypes, tm=TOK_TILE):
    n_t, n_p = len(tiles), len(params)

    def body(*refs):
        t_vals = [r[...] for r in refs[:n_t]]
        p_vals = [r[...] for r in refs[n_t:n_t + n_p]]
        z_vals = [jnp.zeros(s, F32) for s in zero_shapes]
        outs = fn(*t_vals, *p_vals, *z_vals)
        for r, o in zip(refs[n_t + n_p:], outs):
            r[...] = o.astype(r.dtype)

    rows = tiles[0].shape[0]
    return pl.pallas_call(
        body, grid=(rows // tm,), name=name,
        in_specs=[_tile_spec(t, tm) for t in tiles] + [_full_spec(p) for p in params],
        out_specs=[pl.BlockSpec((tm, w), lambda i: (i, 0)) for w in out_widths],
        out_shape=[_sds((rows, w), dt) for w, dt in zip(out_widths, out_dtypes)],
        compiler_params=_params(("arbitrary",)),
    )(*tiles, *params)


def tok_bwd(name, fn, tiles, params, zero_shapes, cots, diff_params, residuals=(), plan=None, tm=TOK_TILE):
    cot_parts = [c if isinstance(c, tuple) else (c,) for c in cots]
    flat_cots = [a for part in cot_parts for a in part]
    residuals = dict(residuals)
    extra = [residuals[i] for i in sorted(residuals)]
    n_t, n_p, n_c, n_r = len(tiles), len(params), len(flat_cots), len(extra)
    acc_shapes = [params[i].shape for i in diff_params] + list(zero_shapes)

    def body(*refs):
        t_vals = [r[...].astype(F32) for r in refs[:n_t]]
        p_vals = [r[...] for r in refs[n_t:n_t + n_p]]
        flat = iter(r[...] for r in refs[n_t + n_p:n_t + n_p + n_c])
        c_vals = [functools.reduce(jnp.add, [next(flat) for _ in part]) for part in cot_parts]
        r_vals = dict(zip(sorted(residuals), (r[...] for r in refs[n_t + n_p + n_c:n_t + n_p + n_c + n_r])))
        out_refs = refs[n_t + n_p + n_c + n_r:]
        z_vals = [jnp.zeros(s, F32) for s in zero_shapes]
        d_vals = [p_vals[i] for i in diff_params]

        def f(t_in, d_in, z_in):
            full = list(p_vals)
            for i, v in zip(diff_params, d_in):
                full[i] = v
            return tuple(fn(*t_in, *full, *z_in))

        _, vjp = jax.vjp(f, t_vals, d_vals, z_vals)
        g_t, g_d, g_z = vjp(tuple(c_vals))
        for i, (r, g) in enumerate(zip(out_refs[:n_t], g_t)):
            r[...] = (g + r_vals[i] if i in r_vals else g).astype(r.dtype)
        acc_refs = out_refs[n_t:]

        @pl.when(pl.program_id(0) == 0)
        def _():
            for r in acc_refs:
                r[...] = jnp.zeros_like(r)

        for r, g in zip(acc_refs, list(g_d) + list(g_z)):
            r[...] += g

    rows = tiles[0].shape[0]
    in_specs = ([_tile_spec(t, tm) for t in tiles] + [_full_spec(p) for p in params]
                + [_tile_spec(c, tm) for c in flat_cots + extra])
    out_specs = ([_tile_spec(t, tm) for t in tiles]
                 + [pl.BlockSpec(s, lambda i, nd=len(s): (0,) * nd) for s in acc_shapes])
    out_shape = [_sds(t.shape) for t in tiles] + [_sds(s) for s in acc_shapes]
    operands = (*tiles, *params, *flat_cots, *extra)
    if plan is not None:
        return call_with_comm(plan, 0, body, (rows // tm,), name, in_specs, out_specs, out_shape, [], operands)
    return pl.pallas_call(
        body, grid=(rows // tm,), name=name, in_specs=in_specs, out_specs=out_specs, out_shape=out_shape,
        compiler_params=_params(("arbitrary",)),
    )(*operands)


def rms_tile(x, g):
    return (_rms(x, g),)


def rwkv_pre_tile(ps, w0, wd_pad, a0, wi_pad, wg, k_k, k_a):
    r = ps[:, 0:D_RWKV]
    k = ps[:, D_RWKV:2 * D_RWKV]
    v = ps[:, 2 * D_RWKV:3 * D_RWKV]
    z2 = ps[:, 3 * D_RWKV:3 * D_RWKV + LANES]
    zg = ps[:, 3 * D_RWKV + LANES:RWKV_COLS]
    w_log = -_softplus(-(w0 + mm(jnp.tanh(z2), wd_pad))) - 0.5
    lw = -jnp.exp(w_log)
    a = jax.nn.sigmoid(a0 + mm(z2, wi_pad))
    g = mm(jax.nn.sigmoid(zg), wg)
    kk = k * k_k
    norm = jnp.sqrt(dot3(kk * kk, _head_matrix(1.0), NN))
    kk = kk / jnp.maximum(norm, 1e-12)
    k2 = k * (1.0 + (a - 1.0) * k_a)
    return r, lw, k2, v, kk, a, g


def mix_out_tile(o, r, k2, v, g, attn, x, w_out, n_post, ln_g, ln_b, r_k, n_ffn_pre, wz):
    hmean = _head_matrix(1.0 / HEAD_DIM)
    d = o - dot3(o, hmean, NN)
    var = dot3(d * d, hmean, NN)
    on = d * lax.rsqrt(var + GN_EPS) * ln_g + ln_b
    bonus = dot3(r * k2 * r_k, _head_matrix(1.0), NN) * v
    rw = (on + bonus) * g
    mix = mmw(jnp.concatenate([attn, rw], axis=1), w_out, wz)
    x1 = x + _rms(mix, n_post)
    return x1, _rms(x1, n_ffn_pre)


def in_proj_fwd(h, w_in, mix_ext, plan):
    def body(h_ref, w_ref, m_ref, proj_ref, ps_ref):
        p = _dot(h_ref[...], w_ref[...], NN)
        proj_ref[...] = p
        ps_ref[...] = p + (_shift_raw(p, 1) - p) * m_ref[...]

    n = D_IN // COL_TILE
    first = D_QKV // COL_TILE
    return call_with_comm(
        plan, n // 2, body, (n,), "in_proj_fwd",
        [pl.BlockSpec((SEQ, D_MODEL), lambda j: (0, 0)), pl.BlockSpec((D_MODEL, COL_TILE), lambda j: (0, j)),
         pl.BlockSpec((1, COL_TILE), lambda j: (0, j))],
        [pl.BlockSpec((SEQ, COL_TILE), lambda j: (0, j)),
         pl.BlockSpec((SEQ, COL_TILE), lambda j: (0, jnp.maximum(j - first, 0)))],
        [_sds((SEQ, D_IN)), _sds((SEQ, RWKV_COLS))], [], (h, w_in, mix_ext))


def in_proj_bwd(h, w_in, mix_ext, proj, dpa):
    def body(h_ref, w_ref, m_ref, p_ref, d_ref, dh_ref, dw_ref, dm_ref):
        d = d_ref[...]
        p = p_ref[...]
        dm_ref[...] = jnp.sum(d * (_shift_raw(p, 1) - p), axis=0, keepdims=True)
        dmix = d * m_ref[...]
        dp = (d - dmix + _shift_raw(dmix, -1)).astype(BF16)
        dw_ref[...] = _dot(h_ref[...], dp, TN)

        @pl.when(pl.program_id(0) == 0)
        def _():
            dh_ref[...] = jnp.zeros_like(dh_ref)

        dh_ref[...] += _dot(dp, w_ref[...], NT)

    n = D_IN // COL_TILE
    col = lambda rows: pl.BlockSpec((rows, COL_TILE), lambda j: (0, j))
    return pl.pallas_call(
        body, grid=(n,), name="in_proj_bwd",
        in_specs=[pl.BlockSpec((SEQ, D_MODEL), lambda j: (0, 0)), col(D_MODEL), col(1), col(SEQ), col(SEQ)],
        out_specs=[pl.BlockSpec((SEQ, D_MODEL), lambda j: (0, 0)), col(D_MODEL), col(1)],
        out_shape=[_sds((SEQ, D_MODEL)), _sds((D_MODEL, D_IN)), _sds((1, D_IN))],
        compiler_params=_params(("arbitrary",)),
    )(h, w_in, mix_ext, proj, dpa)


def _bucket_table():
    rel = (jnp.arange(BLOCK)[:, None] + BLOCK) - jnp.arange(2 * BLOCK)[None, :]
    n = jnp.maximum(rel, 0)
    max_exact = N_BUCKETS // 2
    large = max_exact + (jnp.log(jnp.maximum(n, 1).astype(F32) / max_exact)
                         / math.log(MAX_DISTANCE / max_exact) * (N_BUCKETS - max_exact)).astype(jnp.int32)
    large = jnp.minimum(large, N_BUCKETS - 1)
    return jnp.where(n < max_exact, n, large).astype(jnp.int32)


def _select_matrix(g, o):
    a = lax.broadcasted_iota(jnp.int32, (D_KV, D_KV), 0)
    b = lax.broadcasted_iota(jnp.int32, (D_KV, D_KV), 1)
    return ((a - HEAD_DIM * g == b - o) & (b >= o) & (b < o + HEAD_DIM)).astype(F32)


def _attn_block(q, kp, kc, vp, vc, bias, sinks, block_idx):
    kb = jnp.concatenate([kp, kc], axis=0)
    vb = jnp.concatenate([vp, vc], axis=0)
    row = lax.broadcasted_iota(jnp.int32, (BLOCK, 2 * BLOCK), 0)
    col = lax.broadcasted_iota(jnp.int32, (BLOCK, 2 * BLOCK), 1)
    rel = row + BLOCK - col
    mask = (rel >= 0) & (rel < BLOCK) & (col + (block_idx - 1) * BLOCK >= 0)
    lane8 = lax.broadcasted_iota(jnp.int32, (1, N_HEADS), 1)
    kt, vt = {}, {}
    for g in range(2):
        for o in (0, HEAD_DIM):
            sel = _select_matrix(g, o)
            kt[g, o] = mm(kb, sel)
            vt[g, o] = mm(vb, sel)
    outs = []
    for j in range(D_ATTN // LANES):
        qs = q[:, j * LANES:(j + 1) * LANES]
        acc = None
        for half in range(2):
            hq = 2 * j + half
            g, o = hq // 4, half * HEAD_DIM
            s = mm_nt(qs, kt[g, o]) * (HEAD_DIM ** -0.5) + bias[hq]
            s = jnp.where(mask, s, NEG_INF)
            sink = jnp.sum(jnp.where(lane8 == hq, sinks, 0.0), axis=1, keepdims=True)
            m = lax.stop_gradient(jnp.maximum(jnp.max(s, axis=-1, keepdims=True), sink))
            p = jnp.exp(s - m)
            probs = p / (jnp.sum(p, axis=-1, keepdims=True) + jnp.exp(sink - m))
            part = mm(probs, vt[g, o])
            acc = part if acc is None else acc + part
        outs.append(acc)
    return jnp.concatenate(outs, axis=1)


def _build_bias(rb_ref, bucket, bias_ref):
    for hq in range(N_HEADS):
        acc = jnp.zeros((BLOCK, 2 * BLOCK), F32)
        for b in range(N_BUCKETS):
            acc = jnp.where(bucket == b, rb_ref[b, hq], acc)
        bias_ref[hq] = acc


def _attn_in_specs():
    prev = lambda n: jnp.maximum(n - 1, 0)
    return [pl.BlockSpec((BLOCK, D_ATTN), lambda n: (n, 0)),
            pl.BlockSpec((BLOCK, D_KV), lambda n: (prev(n), D_ATTN // D_KV)),
            pl.BlockSpec((BLOCK, D_KV), lambda n: (n, D_ATTN // D_KV)),
            pl.BlockSpec((BLOCK, D_KV), lambda n: (prev(n), D_ATTN // D_KV + 1)),
            pl.BlockSpec((BLOCK, D_KV), lambda n: (n, D_ATTN // D_KV + 1)),
            pl.BlockSpec(memory_space=pltpu.SMEM),
            pl.BlockSpec((BLOCK, 2 * BLOCK), lambda n: (0, 0)),
            pl.BlockSpec((1, N_HEADS), lambda n: (0, 0))]


def attn_fwd(proj, rel_bias, bucket, sinks, plan):
    def body(q_ref, kp_ref, kc_ref, vp_ref, vc_ref, rb_ref, bk_ref, sk_ref, o_ref, bias_ref):
        n = pl.program_id(0)

        @pl.when(n == 0)
        def _():
            _build_bias(rb_ref, bk_ref[...], bias_ref)

        o_ref[...] = _attn_block(q_ref[...], kp_ref[...], kc_ref[...], vp_ref[...], vc_ref[...],
                                 tuple(bias_ref[h] for h in range(N_HEADS)), sk_ref[...], n)

    (attn,), gathered = call_with_comm(
        plan, 3 * N_BLOCKS // 4, body, (N_BLOCKS,), "attn_fwd", _attn_in_specs(),
        [pl.BlockSpec((BLOCK, D_ATTN), lambda n: (n, 0))], [_sds((SEQ, D_ATTN))],
        [pltpu.VMEM((N_HEADS, BLOCK, 2 * BLOCK), F32)], (proj, proj, proj, proj, proj, rel_bias, bucket, sinks))
    return attn, gathered


def attn_bwd(proj, rel_bias, bucket, sinks, d_attn):
    def body(q_ref, kp_ref, kc_ref, vp_ref, vc_ref, rb_ref, bk_ref, sk_ref, do_ref,
             dq_ref, dkc_ref, dkp_ref, dvc_ref, dvp_ref, drb_ref, dsk_ref, bias_ref, dbias_ref):
        n = pl.program_id(0)

        @pl.when(n == 0)
        def _():
            _build_bias(rb_ref, bk_ref[...], bias_ref)
            dbias_ref[...] = jnp.zeros_like(dbias_ref)
            dsk_ref[...] = jnp.zeros_like(dsk_ref)

        f = lambda q, kp, kc, vp, vc, bias, sk: _attn_block(q, kp, kc, vp, vc, bias, sk, n)
        _, vjp = jax.vjp(f, q_ref[...], kp_ref[...], kc_ref[...], vp_ref[...], vc_ref[...],
                         tuple(bias_ref[h] for h in range(N_HEADS)), sk_ref[...])
        dq, dkp, dkc, dvp, dvc, dbias, dsk = vjp(do_ref[...])
        dq_ref[...] = dq
        dkc_ref[...] = dkc
        dkp_ref[...] = dkp
        dvc_ref[...] = dvc
        dvp_ref[...] = dvp
        for h in range(N_HEADS):
            dbias_ref[h] += dbias[h]
        dsk_ref[...] += dsk

        @pl.when(n == N_BLOCKS - 1)
        def _():
            bucket_v = bk_ref[...]
            rowi = lax.broadcasted_iota(jnp.int32, (N_BUCKETS, 2 * BLOCK), 0)
            lane = lax.broadcasted_iota(jnp.int32, (N_BUCKETS, N_HEADS), 1)
            out = jnp.zeros((N_BUCKETS, N_HEADS), F32)
            for hq in range(N_HEADS):
                dbh = dbias_ref[hq]
                rows = jnp.zeros((N_BUCKETS, 2 * BLOCK), F32)
                for b in range(N_BUCKETS):
                    part = jnp.sum(jnp.where(bucket_v == b, dbh, 0.0), axis=0, keepdims=True)
                    rows = jnp.where(rowi == b, part, rows)
                tot = jnp.sum(rows, axis=1, keepdims=True)
                out = jnp.where(lane == hq, tot, out)
            drb_ref[...] = out

    blk = lambda w: pl.BlockSpec((BLOCK, w), lambda n: (n, 0))
    return pl.pallas_call(
        body, grid=(N_BLOCKS,), name="attn_bwd",
        in_specs=_attn_in_specs() + [blk(D_ATTN)],
        out_specs=[blk(D_ATTN), blk(D_KV), blk(D_KV), blk(D_KV), blk(D_KV),
                   pl.BlockSpec((N_BUCKETS, N_HEADS), lambda n: (0, 0)),
                   pl.BlockSpec((1, N_HEADS), lambda n: (0, 0))],
        out_shape=[_sds((SEQ, D_ATTN)), _sds((SEQ, D_KV)), _sds((SEQ, D_KV)), _sds((SEQ, D_KV)),
                   _sds((SEQ, D_KV)), _sds((N_BUCKETS, N_HEADS)), _sds((1, N_HEADS))],
        scratch_shapes=[pltpu.VMEM((N_HEADS, BLOCK, 2 * BLOCK), F32),
                        pltpu.VMEM((N_HEADS, BLOCK, 2 * BLOCK), F32)],
        compiler_params=_params(("arbitrary",)),
    )(proj, proj, proj, proj, proj, rel_bias, bucket, sinks, d_attn)


def _stack(x, size):
    groups = x.shape[1] // size
    lane = lax.broadcasted_iota(jnp.int32, x.shape, 1) // size
    return jnp.concatenate([jnp.where(lane == i, x, 0.0) for i in range(groups)], axis=0)


def _neumann(l):
    c = CHUNK
    t = lax.broadcasted_iota(jnp.int32, l.shape, 0)
    i = lax.broadcasted_iota(jnp.int32, l.shape, 1) % c
    inv = (i == t).astype(F32) + l
    pw = dot1(l, _stack(l, c), NN)
    for _ in range(int(math.log2(c)) - 2):
        both = dot1(jnp.concatenate([inv, pw], axis=0), _stack(pw, c), NN)
        inv = inv + both[:c]
        pw = both[c:]
    return inv + dot1(inv, _stack(pw, c), NN)


@jax.custom_vjp
def neumann_inv(l):
    return _neumann(l)


def _neumann_fwd(l):
    inv = _neumann(l)
    return inv, inv


def _neumann_bwd(inv, g):
    c = CHUNK
    bd_t = _stack(inv, c).T
    inv_t = bd_t[0:c]
    for h in range(1, inv.shape[1] // c):
        inv_t = inv_t + bd_t[h * c:(h + 1) * c]
    return (dot1(dot1(inv_t, _stack(g, c), NN), bd_t, NN),)


neumann_inv.defvjp(_neumann_fwd, _neumann_bwd)


def _cumsum_raw(x, dims):
    c = x.shape[0]
    tt = lax.broadcasted_iota(jnp.int32, (c, c), 0)
    ii = lax.broadcasted_iota(jnp.int32, (c, c), 1)
    tri = (ii <= tt).astype(BF16)
    hi = x.astype(BF16)
    rest = x - hi.astype(F32)
    mid = rest.astype(BF16)
    lo = (rest - mid.astype(F32)).astype(BF16)
    return _dot(tri, hi, dims) + (_dot(tri, mid, dims) + _dot(tri, lo, dims))


@jax.custom_vjp
def cumsum_rows(x):
    return _cumsum_raw(x, NN)


def _cumsum_fwd(x):
    return _cumsum_raw(x, NN), None


def _cumsum_bwd(_, g):
    return (_cumsum_raw(g, TN),)


cumsum_rows.defvjp(_cumsum_fwd, _cumsum_bwd)


def _rwkv_chunk(s0, r, lw, k, v, kk, a):
    heads = r.shape[1] // HEAD_DIM
    c, hc = CHUNK, heads * CHUNK
    t = lax.broadcasted_iota(jnp.int32, (c, hc), 0)
    i = lax.broadcasted_iota(jnp.int32, (c, hc), 1) % c
    strict, incl = i < t, i <= t
    stack = lambda x: _stack(x, HEAD_DIM)
    ba = lax.broadcasted_iota(jnp.int32, s0.shape, 0) // HEAD_DIM
    bb = lax.broadcasted_iota(jnp.int32, s0.shape, 1) // HEAD_DIM
    blocks = (ba == bb).astype(F32)

    cum = cumsum_rows(lw)
    cum_end = jnp.sum(lw, axis=0, keepdims=True)
    beta = kk * a
    al = -kk * jnp.exp(cum - lw)
    p_inv = jnp.exp(-cum)
    be, kb, rb = beta * p_inv, k * p_inv, r * jnp.exp(cum)
    ar = jnp.concatenate([al, rb], axis=0)
    sv = stack(v)
    l_all = dot1(ar, jnp.concatenate([stack(be), stack(kb)], axis=0), NT)
    l_ab = jnp.where(strict, l_all[:c, :hc], 0.0)
    l_ak = jnp.where(strict, l_all[:c, hc:], 0.0)
    l_rb = jnp.where(incl, l_all[c:, :hc], 0.0)
    l_rk = jnp.where(incl, l_all[c:, hc:], 0.0)
    inv = neumann_inv(l_ab)
    from_s0 = dot1(ar, s0, NT)
    from_v = dot1(jnp.concatenate([l_ak, l_rk], axis=0), sv, NN)
    u = dot1(inv, stack(from_s0[:c] + from_v[:c]), NN)
    o = from_s0[c:] + from_v[c:] + dot1(l_rb, stack(u), NN)
    to_end = jnp.exp(cum_end - cum)
    s1 = s0 * jnp.exp(cum_end) + blocks * dot1(
        jnp.concatenate([u, v], axis=0), jnp.concatenate([beta * to_end, k * to_end], axis=0), TN)
    return o, s1


def call_with_comm(plan, middle_step, body, grid, name, in_specs, out_specs, out_shape, scratch_shapes, operands):
    n_in, n_out, n_scr = len(in_specs), len(out_specs), len(scratch_shapes)
    p_in, p_out = len(plan.ins), len(plan.out_shape)

    def fused(*refs):
        refs = list(refs)
        ins, refs = refs[:n_in], refs[n_in:]
        p_ins, refs = refs[:p_in], refs[p_in:]
        outs, refs = refs[:n_out], refs[n_out:]
        p_outs, refs = refs[:p_out], refs[p_out:]
        scr, p_sems = refs[:n_scr], refs[n_scr:]
        start, middle, finish = plan.stages(p_ins, p_outs, p_sems)
        step = pl.program_id(0)
        pl.when(step == 0)(start)
        body(*ins, *outs, *scr)
        pl.when(step == middle_step)(middle)
        pl.when(step == grid[0] - 1)(finish)

    any_spec = pl.BlockSpec(memory_space=pl.ANY)
    res = pl.pallas_call(
        fused, grid=grid, name=name,
        in_specs=list(in_specs) + [any_spec] * p_in, out_specs=list(out_specs) + [any_spec] * p_out,
        out_shape=list(out_shape) + list(plan.out_shape), scratch_shapes=list(scratch_shapes) + list(plan.scratch),
        compiler_params=_params(("arbitrary",)),
    )(*operands, *plan.ins)
    return res[:n_out], res[n_out:]


def _by_group(ref):
    return jnp.stack([ref[:, g * SCAN_WIDTH:(g + 1) * SCAN_WIDTH] for g in range(SCAN_GROUPS)])


def _store_groups(ref, val):
    for g in range(SCAN_GROUPS):
        ref[:, g * SCAN_WIDTH:(g + 1) * SCAN_WIDTH] = val[g]


def rwkv_scan_fwd(r, lw, k, v, kk, a, plan):
    def body(r_ref, lw_ref, k_ref, v_ref, kk_ref, a_ref, o_ref, st_ref, s_ref):
        @pl.when(pl.program_id(0) == 0)
        def _():
            s_ref[...] = jnp.zeros_like(s_ref)

        s0 = s_ref[...]
        st_ref[0] = s0
        o, s1 = jax.vmap(_rwkv_chunk)(s0, *(_by_group(ref) for ref in (r_ref, lw_ref, k_ref, v_ref, kk_ref, a_ref)))
        _store_groups(o_ref, o)
        s_ref[...] = s1

    tb = pl.BlockSpec((CHUNK, D_RWKV), lambda c: (c, 0))
    state = (SCAN_GROUPS, SCAN_WIDTH, SCAN_WIDTH)
    return call_with_comm(
        plan, 3 * N_CHUNKS // 4, body, (N_CHUNKS,), "rwkv_scan_fwd",
        [tb] * 6, [tb, pl.BlockSpec((1,) + state, lambda c: (c, 0, 0, 0))],
        [_sds((SEQ, D_RWKV)), _sds((N_CHUNKS,) + state)], [pltpu.VMEM(state, F32)], (r, lw, k, v, kk, a))


def rwkv_scan_bwd(r, lw, k, v, kk, a, states, d_o, plan):
    def body(r_ref, lw_ref, k_ref, v_ref, kk_ref, a_ref, st_ref, do_ref,
             dr_ref, dlw_ref, dk_ref, dv_ref, dkk_ref, da_ref, ds_ref):
        @pl.when(pl.program_id(0) == 0)
        def _():
            ds_ref[...] = jnp.zeros_like(ds_ref)

        _, vjp = jax.vjp(jax.vmap(_rwkv_chunk), st_ref[0],
                         *(_by_group(ref) for ref in (r_ref, lw_ref, k_ref, v_ref, kk_ref, a_ref)))
        grads = vjp((_by_group(do_ref), ds_ref[...]))
        ds_ref[...] = grads[0]
        for ref, val in zip((dr_ref, dlw_ref, dk_ref, dv_ref, dkk_ref, da_ref), grads[1:]):
            _store_groups(ref, val)

    last = N_CHUNKS - 1
    tb = pl.BlockSpec((CHUNK, D_RWKV), lambda c: (last - c, 0))
    state = (SCAN_GROUPS, SCAN_WIDTH, SCAN_WIDTH)
    return call_with_comm(
        plan, N_CHUNKS // 4, body, (N_CHUNKS,), "rwkv_scan_bwd",
        [tb] * 6 + [pl.BlockSpec((1,) + state, lambda c: (last - c, 0, 0, 0)), tb], [tb] * 6,
        [_sds((SEQ, D_RWKV))] * 6, [pltpu.VMEM(state, F32)], (r, lw, k, v, kk, a, states, d_o))


def _ffn_mid(ug, uv, cg, cv, bg, bv):
    conv_g = bg + cg[0] * shift_rows(ug, 2) + cg[1] * shift_rows(ug, 1) + cg[2] * ug
    conv_v = bv + cv[0] * shift_rows(uv, 2) + cv[1] * shift_rows(uv, 1) + cv[2] * uv
    return jax.nn.gelu(conv_g, approximate=True) * conv_v


def _conv_rows(ref):
    return tuple(ref[0, j:j + 1, :] for j in range(3))


def _ffn_specs(tile):
    per = D_MODEL // tile
    half = N_DEV // 2
    w_g = pl.BlockSpec((1, D_MODEL, tile), lambda t: (t // per, 0, t % per))
    w_v = pl.BlockSpec((1, D_MODEL, tile), lambda t: (half + t // per, 0, t % per))
    c_g = pl.BlockSpec((1, 3, tile), lambda t: (t // per, 0, t % per))
    c_v = pl.BlockSpec((1, 3, tile), lambda t: (half + t // per, 0, t % per))
    b_g = pl.BlockSpec((1, tile), lambda t: (0, t))
    b_v = pl.BlockSpec((1, tile), lambda t: (0, D_FF // tile + t))
    w_d = pl.BlockSpec((tile, D_MODEL), lambda t: (t, 0))
    return w_g, w_v, c_g, c_v, b_g, b_v, w_d


def ffn_fwd(h2, w_up, conv_w, conv_b, w_down):
    def body(h_ref, wg_ref, wv_ref, cg_ref, cv_ref, bg_ref, bv_ref, wd_ref, f_ref):
        @pl.when(pl.program_id(0) == 0)
        def _():
            f_ref[...] = jnp.zeros_like(f_ref)

        h = h_ref[...]
        act = _ffn_mid(_dot(h, wg_ref[0], NN), _dot(h, wv_ref[0], NN), _conv_rows(cg_ref), _conv_rows(cv_ref),
                       bg_ref[...], bv_ref[...])
        f_ref[...] += _dot(act.astype(BF16), wd_ref[...], NN)

    full = pl.BlockSpec((SEQ, D_MODEL), lambda t: (0, 0))
    return pl.pallas_call(
        body, grid=(D_FF // FF_TILE,), name="ffn_fwd",
        in_specs=[full, *_ffn_specs(FF_TILE)],
        out_specs=full, out_shape=_sds((SEQ, D_MODEL)),
        compiler_params=_params(("arbitrary",)),
    )(h2, w_up, w_up, conv_w, conv_w, conv_b, conv_b, w_down)


def ffn_bwd_mid(h2, w_up, conv_w, conv_b, w_down, df):
    tile, rows, halo = FF_TILE, FF_ROW_CHUNK, FF_HALO
    ext = rows + 2 * halo

    def body(h_hbm, wg_ref, wv_ref, cg_ref, cv_ref, bg_ref, bv_ref, wd_ref, df_hbm,
             dug_ref, duv_ref, dcg_ref, dcv_ref, dbg_ref, dbv_ref, dwd_ref,
             h_ref, df_ref, ug_ref, uv_ref, da_ref, act_ref):
        @pl.when(pl.program_id(0) == 0)
        def _():
            pltpu.sync_copy(h_hbm, h_ref)
            pltpu.sync_copy(df_hbm, df_ref)
            for ref in (ug_ref, uv_ref, da_ref):
                ref[0:halo, :] = jnp.zeros((halo, tile), F32)
                ref[halo + SEQ:, :] = jnp.zeros((halo, tile), F32)

        h, df_b = h_ref[...], df_ref[...]
        ug_ref[halo:halo + SEQ, :] = _dot(h, wg_ref[0], NN)
        uv_ref[halo:halo + SEQ, :] = _dot(h, wv_ref[0], NN)
        da_ref[halo:halo + SEQ, :] = _dot(df_b, wd_ref[...], NT)
        cg, cv, bg, bv = _conv_rows(cg_ref), _conv_rows(cv_ref), bg_ref[...], bv_ref[...]
        down = lambda x, n: pltpu.roll(x, n, 0)
        up = lambda x, n: pltpu.roll(x, ext - n, 0)
        mid = slice(halo, halo + rows)

        def chunk(i, sums):
            r0 = pl.multiple_of(i * rows, rows)
            window = pl.ds(r0, ext)
            ug, uv, da = ug_ref[window, :], uv_ref[window, :], da_ref[window, :]
            ug1, ug2, uv1, uv2 = down(ug, 1), down(ug, 2), down(uv, 1), down(uv, 2)
            conv_g = bg + cg[0] * ug2 + cg[1] * ug1 + cg[2] * ug
            conv_v = bv + cv[0] * uv2 + cv[1] * uv1 + cv[2] * uv
            act, vjp = jax.vjp(lambda a, b: jax.nn.gelu(a, approximate=True) * b, conv_g, conv_v)
            dcg, dcv = vjp(da)
            dug = cg[2] * dcg + cg[1] * up(dcg, 1) + cg[0] * up(dcg, 2)
            duv = cv[2] * dcv + cv[1] * up(dcv, 1) + cv[0] * up(dcv, 2)
            out = pl.ds(r0, rows)
            act_ref[out, :] = act[mid].astype(BF16)
            dug_ref[out, :] = dug[mid].astype(BF16)
            duv_ref[out, :] = duv[mid].astype(BF16)
            col = lambda x: jnp.sum(x[mid], axis=0, keepdims=True)
            new = (col(dcg * ug2), col(dcg * ug1), col(dcg * ug), col(dcv * uv2), col(dcv * uv1), col(dcv * uv),
                   col(dcg), col(dcv))
            return tuple(s + n for s, n in zip(sums, new))

        zero = jnp.zeros((1, tile), F32)
        sums = lax.fori_loop(0, SEQ // rows, chunk, (zero,) * 8)
        for j in range(3):
            dcg_ref[0, j:j + 1, :] = sums[j]
            dcv_ref[0, j:j + 1, :] = sums[3 + j]
        dbg_ref[...] = sums[6]
        dbv_ref[...] = sums[7]
        dwd_ref[...] = _dot(act_ref[...], df_b, TN).astype(BF16)

    hbm = pl.BlockSpec(memory_space=pl.ANY)
    w_g, w_v, c_g, c_v, b_g, b_v, w_d = _ffn_specs(tile)
    col = pl.BlockSpec((SEQ, tile), lambda t: (0, t))
    padded = pltpu.VMEM((SEQ + 2 * halo, tile), F32)
    return pl.pallas_call(
        body, grid=(D_FF // tile,), name="ffn_bwd_mid",
        in_specs=[hbm, w_g, w_v, c_g, c_v, b_g, b_v, w_d, hbm],
        out_specs=[col, col, c_g, c_v, b_g, b_v, w_d],
        out_shape=[_sds((SEQ, D_FF), BF16), _sds((SEQ, D_FF), BF16), _sds((N_DEV, 3, D_MODEL)),
                   _sds((N_DEV, 3, D_MODEL)), _sds((1, 2 * D_FF)), _sds((1, 2 * D_FF)), _sds((D_FF, D_MODEL), BF16)],
        scratch_shapes=[pltpu.VMEM((SEQ, D_MODEL), BF16), pltpu.VMEM((SEQ, D_MODEL), BF16), padded, padded, padded,
                        pltpu.VMEM((SEQ, tile), BF16)],
        compiler_params=_params(("arbitrary",)),
    )(h2, w_up, w_up, conv_w, conv_w, conv_b, conv_b, w_down, df)


def ffn_bwd_up(h2, w_up, dug, duv):
    tile = FF_TILE
    per = D_MODEL // tile

    def body(h_hbm, wg_ref, wv_ref, dug_ref, duv_ref, dh_hbm, dup_hbm, h_ref, dh_ref, dwg_ref, dwv_ref, sem, up_sems):
        t = pl.program_id(0)

        @pl.when(t == 0)
        def _():
            pltpu.sync_copy(h_hbm, h_ref)
            dh_ref[...] = jnp.zeros_like(dh_ref)

        h, dug_b, duv_b = h_ref[...], dug_ref[...], duv_ref[...]
        cols = pl.ds(pl.multiple_of((t % per) * tile, tile), tile)
        to_gate = pltpu.make_async_copy(dwg_ref, dup_hbm.at[t // per, :, cols], up_sems.at[0])
        to_value = pltpu.make_async_copy(dwv_ref, dup_hbm.at[N_DEV // 2 + t // per, :, cols], up_sems.at[1])
        dwg_ref[...] = _dot(h, dug_b, TN).astype(BF16)
        to_gate.start()
        dwv_ref[...] = _dot(h, duv_b, TN).astype(BF16)
        to_value.start()
        dh_ref[...] += _dot(jnp.concatenate([dug_b, duv_b], axis=1),
                            jnp.concatenate([wg_ref[0], wv_ref[0]], axis=1), NT)
        to_gate.wait()
        to_value.wait()

        @pl.when(t == D_FF // tile - 1)
        def _():
            cp = pltpu.make_async_copy(dh_ref, dh_hbm, sem)
            cp.start()
            cp.wait()

    hbm = pl.BlockSpec(memory_space=pl.ANY)
    w_g, w_v = _ffn_specs(tile)[:2]
    col = pl.BlockSpec((SEQ, tile), lambda t: (0, t))
    return pl.pallas_call(
        body, grid=(D_FF // tile,), name="ffn_bwd_up",
        in_specs=[hbm, w_g, w_v, col, col], out_specs=[hbm, hbm],
        out_shape=[_sds((SEQ, D_MODEL)), _sds((N_DEV, D_MODEL, D_MODEL), BF16)],
        scratch_shapes=[pltpu.VMEM((SEQ, D_MODEL), BF16), pltpu.VMEM((SEQ, D_MODEL), F32),
                        pltpu.VMEM((D_MODEL, tile), BF16), pltpu.VMEM((D_MODEL, tile), BF16),
                        pltpu.SemaphoreType.DMA, pltpu.SemaphoreType.DMA((2,))],
        compiler_params=_params(("arbitrary",)),
    )(h2, w_up, w_up, dug, duv)


def loss_head(x1, f, target, n_post):
    def tile_loss(x1_t, f_t, g, tgt):
        err = x1_t + _rms(f_t, g) - tgt
        return 0.5 * jnp.sum(jnp.mean(err * err, axis=-1))

    def body(x_ref, f_ref, t_ref, g_ref, dx_ref, df_ref, dg_ref, loss_ref):
        val, (dx, df, dg) = jax.value_and_grad(tile_loss, argnums=(0, 1, 2))(
            x_ref[...], f_ref[...], g_ref[...], t_ref[...])
        dx_ref[...] = dx
        df_ref[...] = df.astype(BF16)

        @pl.when(pl.program_id(0) == 0)
        def _():
            dg_ref[...] = jnp.zeros_like(dg_ref)
            loss_ref[...] = jnp.zeros_like(loss_ref)

        dg_ref[...] += dg
        loss_ref[...] += jnp.full((1, LANES), val, F32)

    tile = pl.BlockSpec((TOK_TILE, D_MODEL), lambda i: (i, 0))
    vec = pl.BlockSpec((1, D_MODEL), lambda i: (0, 0))
    return pl.pallas_call(
        body, grid=(SEQ // TOK_TILE,), name="loss_head",
        in_specs=[tile, tile, tile, vec],
        out_specs=[tile, tile, vec, pl.BlockSpec((1, LANES), lambda i: (0, 0))],
        out_shape=[_sds((SEQ, D_MODEL)), _sds((SEQ, D_MODEL), BF16), _sds((1, D_MODEL)), _sds((1, LANES))],
        compiler_params=_params(("arbitrary",)),
    )(x1, f, target, n_post)


def _mesh_pos():
    return lax.axis_index("x"), lax.axis_index("y"), lax.axis_index("c")


def _flip(pos, rel):
    x, y, c = pos
    return (1 - x if rel & 4 else x, 1 - y if rel & 2 else y, 1 - c if rel & 1 else c)


def _slot(pos):
    x, y, c = pos
    return 4 * x + 2 * y + c


def cast_bf16(w, rows):
    def body(w_ref, o_ref):
        o_ref[...] = w_ref[...].astype(BF16)

    spec = pl.BlockSpec((rows, w.shape[1]), lambda i: (i, 0))
    return pl.pallas_call(body, grid=(w.shape[0] // rows,), name="cast_bf16_%dx%d" % w.shape,
                          in_specs=[spec], out_specs=spec, out_shape=_sds(w.shape, BF16),
                          compiler_params=_params(("arbitrary",)))(w)


class CommPlan:
    def __init__(self, ins, out_shape, scratch, stages):
        self.ins, self.out_shape, self.scratch, self.stages = ins, out_shape, scratch, stages


def run_comm(name, plan):
    n_in, n_out = len(plan.ins), len(plan.out_shape)

    def body(*refs):
        for stage in plan.stages(refs[:n_in], refs[n_in:n_in + n_out], refs[n_in + n_out:]):
            stage()

    any_spec = pl.BlockSpec(memory_space=pl.ANY)
    return pl.pallas_call(
        body, name=name, in_specs=[any_spec] * len(plan.ins), out_specs=[any_spec] * len(plan.out_shape),
        out_shape=plan.out_shape, scratch_shapes=plan.scratch)(*plan.ins)


def gather_plan(shards):
    n = len(shards)

    def stages(srcs, outs, sems):
        send_sems, recv_sems, local_sems = sems

        def places():
            me = _mesh_pos()
            return me, _flip(me, 1), [_flip(me, 2), _flip(me, 4), _flip(me, 6)]

        def copy(a, k, block, to, src=None):
            dst = outs[a].at[_slot(block)]
            return pltpu.make_async_remote_copy(
                src_ref=dst if src is None else src, dst_ref=dst,
                send_sem=send_sems.at[7 * a + k], recv_sem=recv_sems.at[7 * a + k],
                device_id=to, device_id_type=pl.DeviceIdType.MESH)

        def local(a, me):
            return pltpu.make_async_copy(srcs[a], outs[a].at[_slot(me)], local_sems.at[a])

        def own(a, me, sibling, chips):
            return [copy(a, 0, me, sibling, src=srcs[a])] + [
                copy(a, 1 + j, me, chip, src=srcs[a]) for j, chip in enumerate(chips)]

        def start():
            me, sibling, chips = places()
            for a in range(n):
                local(a, me).start()
                for cp in own(a, me, sibling, chips):
                    cp.start()

        def forward():
            me, sibling, chips = places()
            for j, chip in enumerate(chips):
                for a in range(n):
                    copy(a, 1 + j, chip, me).wait_recv()
                    copy(a, 4 + j, chip, sibling).start()

        def finish():
            me, sibling, chips = places()
            for a in range(n):
                copy(a, 0, sibling, me).wait_recv()
                for j, chip in enumerate(chips):
                    copy(a, 4 + j, _flip(chip, 1), me).wait_recv()
            for a in range(n):
                for cp in own(a, me, sibling, chips):
                    cp.wait_send()
                for j, chip in enumerate(chips):
                    copy(a, 4 + j, chip, sibling).wait_send()
                local(a, me).wait()

        return start, forward, finish

    return CommPlan(list(shards), [_sds((N_DEV,) + s.shape, s.dtype) for s in shards],
                    [pltpu.SemaphoreType.DMA((7 * n,)), pltpu.SemaphoreType.DMA((7 * n,)),
                     pltpu.SemaphoreType.DMA((n,))], stages)


def exchange_plan(parts, replicated, rels, members, index, member_axis, own_copy):
    n, nr = len(parts), len(rels)
    pick_index = (slice(None),) * member_axis + (0,)
    subs = [1 if (r or member_axis == 0) else p.shape[0] for p, r in zip(parts, replicated)]
    first = [sum(subs[:a]) for a in range(n)]
    total = sum(subs)

    def stages(srcs, outs, sems):
        send_sems, recv_sems, local_sems = sems

        def src(a, s, pos):
            if replicated[a]:
                return srcs[a]
            return srcs[a].at[index(pos)] if member_axis == 0 else srcs[a].at[s, index(pos)]

        def dst(a, s, pos):
            block = outs[a].at[index(pos)]
            return block if (replicated[a] or member_axis == 0) else block.at[s]

        def copy(a, s, j, me, src_pos, dst_pos):
            sem = nr * (first[a] + s) + j
            return pltpu.make_async_remote_copy(
                src_ref=src(a, s, src_pos), dst_ref=dst(a, s, dst_pos),
                send_sem=send_sems.at[sem], recv_sem=recv_sems.at[sem],
                device_id=_flip(me, rels[j]), device_id_type=pl.DeviceIdType.MESH)

        pieces = [(a, s) for a in range(n) for s in range(subs[a])]

        def local(a, s, me):
            return pltpu.make_async_copy(src(a, s, me), dst(a, s, me), local_sems.at[first[a] + s])

        def sends(me):
            return [copy(a, s, j, me, _flip(me, rels[j]), me) for j in range(nr) for a, s in pieces]

        own = pieces if own_copy else []

        def start():
            me = _mesh_pos()
            for cp in sends(me) + [local(a, s, me) for a, s in own]:
                cp.start()

        def middle():
            pass

        def finish():
            me = _mesh_pos()
            for j in range(nr):
                for a, s in pieces:
                    copy(a, s, j, me, me, _flip(me, rels[j])).wait_recv()
            for cp in sends(me):
                cp.wait_send()
            for a, s in own:
                local(a, s, me).wait()

        return start, middle, finish

    shapes = [p.shape if r else jax.eval_shape(lambda t: t[pick_index], p).shape for p, r in zip(parts, replicated)]
    return CommPlan(list(parts), [_sds((members,) + s, p.dtype) for s, p in zip(shapes, parts)],
                    [pltpu.SemaphoreType.DMA((nr * total,)), pltpu.SemaphoreType.DMA((nr * total,)),
                     pltpu.SemaphoreType.DMA((total,))], stages)


def pair_plan(parts, replicated):
    return exchange_plan(parts, replicated, [1], 2, lambda pos: pos[2], 1, False)


def chip_plan(parts, replicated):
    return exchange_plan(parts, replicated, [2, 4, 6], 4, lambda pos: 2 * pos[0] + pos[1], 0, True)


def add_pair(name, mine, swapped, out_dtype, rows):
    def body(m_ref, s_ref, o_ref):
        own = m_ref[0, 0] if mine.ndim == 4 else m_ref[0]
        o_ref[0] = (own.astype(F32) + s_ref[0, 0].astype(F32)).astype(o_ref.dtype)

    _, n, r, c = swapped.shape
    core = lambda: lax.axis_index("c")
    if mine.ndim == 4:
        mine_spec = pl.BlockSpec((1, 1, rows, c), lambda i, j: (i, core(), j, 0))
    else:
        mine_spec = pl.BlockSpec((1, rows, c), lambda i, j: (i, j, 0))
    return pl.pallas_call(
        body, grid=(n, r // rows), name=name,
        in_specs=[mine_spec, pl.BlockSpec((1, 1, rows, c), lambda i, j: (1 - core(), i, j, 0))],
        out_specs=pl.BlockSpec((1, rows, c), lambda i, j: (i, j, 0)),
        out_shape=_sds((n, r, c), out_dtype),
        compiler_params=_params(("arbitrary", "arbitrary")),
    )(mine, swapped)


def add_pair_small(mines, swappeds):
    n = len(mines)
    halves = [m.ndim == s.ndim for m, s in zip(mines, swappeds)]

    def body(*refs):
        c = lax.axis_index("c")
        for i in range(n):
            m_ref, s_ref, o_ref = refs[i], refs[n + i], refs[2 * n + i]
            o_ref[...] = (m_ref[:, c] if halves[i] else m_ref[...]) + s_ref[1 - c]

    vmem = pl.BlockSpec(memory_space=pltpu.VMEM)
    return pl.pallas_call(
        body, name="pair_add_small", in_specs=[vmem] * (2 * n), out_specs=[vmem] * n,
        out_shape=[_sds(s.shape[1:]) for s in swappeds], compiler_params=_params(),
    )(*mines, *swappeds)


def _adamw_math(w, g, m, v):
    nm = ADAM_B1 * m + (1.0 - ADAM_B1) * g
    nv = ADAM_B2 * v + (1.0 - ADAM_B2) * (g * g)
    m_hat = nm / (1.0 - ADAM_B1 ** ADAM_STEP)
    v_hat = nv / (1.0 - ADAM_B2 ** ADAM_STEP)
    return -ADAM_LR * (m_hat / (jnp.sqrt(v_hat) + ADAM_EPS) + ADAM_WD * w), nm, nv


def adamw_small(ws, parts, ms, vs):
    n = len(ws)

    def body(*refs):
        for i in range(n):
            w_ref, p_ref, m_ref, v_ref = (refs[k * n + i] for k in range(4))
            g = p_ref[0]
            for j in range(1, p_ref.shape[0]):
                g = g + p_ref[j]
            delta, nm, nv = _adamw_math(w_ref[...], g, m_ref[...], v_ref[...])
            for k, val in enumerate((g, delta, nm, nv)):
                refs[(4 + k) * n + i][...] = val

    vmem = pl.BlockSpec(memory_space=pltpu.VMEM)
    outs = pl.pallas_call(
        body, name="adamw_small", in_specs=[vmem] * (4 * n), out_specs=[vmem] * (4 * n),
        out_shape=[_sds(w.shape) for w in ws] * 4, compiler_params=_params(),
    )(*ws, *parts, *ms, *vs)
    return [outs[k * n:(k + 1) * n] for k in range(4)]

def adamw(name, w, parts, m, v, rows, plan=None):
    n_parts = parts.shape[0]

    def body(w_ref, p_ref, m_ref, v_ref, g_ref, d_ref, nm_ref, nv_ref):
        g = p_ref[0].astype(F32)
        for j in range(1, n_parts):
            g = g + p_ref[j].astype(F32)
        g_ref[...] = g
        d_ref[...], nm_ref[...], nv_ref[...] = _adamw_math(w_ref[...], g, m_ref[...], v_ref[...])

    cols = w.shape[1]
    spec = pl.BlockSpec((rows, cols), lambda i: (i, 0))
    grid = (w.shape[0] // rows,)
    in_specs = [spec, pl.BlockSpec((n_parts, rows, cols), lambda i: (0, i, 0)), spec, spec]
    if plan is not None:
        return call_with_comm(plan, 0, body, grid, name, in_specs, [spec] * 4, [_sds(w.shape)] * 4, [],
                              (w, parts, m, v))
    return pl.pallas_call(
        body, grid=grid, name=name, in_specs=in_specs, out_specs=[spec] * 4, out_shape=[_sds(w.shape)] * 4,
        compiler_params=_params(("arbitrary",)),
    )(w, parts, m, v)


def _to_slots(full, per):
    return full.reshape(full.shape[0], N_DEV, per).transpose(1, 0, 2)


def _from_slots(slots):
    return slots.transpose(1, 0, 2).reshape(slots.shape[1], -1)


def kernel(x, norm_mix_pre, norm_mix_post, norm_ffn_pre, norm_ffn_post, w_in, rel_bias, sinks, rwkv_shift_mix, w0, w_decay_up, a0, w_iclr_up, w_gate_up, k_k, k_a, r_k, ln_x_g, ln_x_b, w_out, w_ffn_up, conv_w, conv_b, w_ffn_down, loss_target, m_norm_mix_pre, m_norm_mix_post, m_norm_ffn_pre, m_norm_ffn_post, m_w_in, m_rel_bias, m_sinks, m_rwkv_shift_mix, m_w0, m_w_decay_up, m_a0, m_w_iclr_up, m_w_gate_up, m_k_k, m_k_a, m_r_k, m_ln_x_g, m_ln_x_b, m_w_out, m_w_ffn_up, m_conv_w, m_conv_b, m_w_ffn_down, v_norm_mix_pre, v_norm_mix_post, v_norm_ffn_pre, v_norm_ffn_post, v_w_in, v_rel_bias, v_sinks, v_rwkv_shift_mix, v_w0, v_w_decay_up, v_a0, v_w_iclr_up, v_w_gate_up, v_k_k, v_k_a, v_r_k, v_ln_x_g, v_ln_x_b, v_w_out, v_w_ffn_up, v_conv_w, v_conv_b, v_w_ffn_down):
    x2 = x[0]
    target = loss_target[0]

    (g_in,) = run_comm("all_gather_w_in", gather_plan([cast_bf16(w_in[0], 256)]))
    mixer_gather = gather_plan([cast_bf16(w_out[0], 128), w_decay_up[0], w_iclr_up[0], w_gate_up[0], conv_w[0]])
    up_gather = gather_plan([cast_bf16(w_ffn_up[0], 256)])
    down_gather = gather_plan([cast_bf16(w_ffn_down[0], 256)])
    w_in_b = _from_slots(g_in)
    mix_ext = jnp.concatenate([jnp.zeros((1, D_QKV), F32), rwkv_shift_mix], axis=1)
    r_k_row = r_k.reshape(1, D_RWKV)
    bucket = _bucket_table()

    (h1,) = tok_fwd("rms_mix_pre", rms_tile, [x2], [norm_mix_pre], [], [D_MODEL], [BF16])
    (proj, ps), (g_out, g_decay, g_iclr, g_gate, g_conv) = in_proj_fwd(h1, w_in_b, mix_ext, mixer_gather)
    w_out_b = g_out.reshape(D_MODEL, D_MODEL)
    lora = jnp.zeros((HEAD_DIM, D_RWKV), F32)
    wd_pad = jnp.concatenate([_from_slots(g_decay), lora], axis=0)
    wi_pad = jnp.concatenate([lora, _from_slots(g_iclr)], axis=0)
    wg_full = _from_slots(g_gate)
    attn, (g_down,) = attn_fwd(proj, rel_bias, bucket, sinks, down_gather)
    pre_params = [w0, wd_pad, a0, wi_pad, wg_full, k_k, k_a]
    r_, lw_, k2_, v_, kk_, a_, gate_ = tok_fwd("rwkv_pre", rwkv_pre_tile, [ps], pre_params, [],
                                               [D_RWKV] * 7, [F32] * 7)
    (o_, states), (g_up,) = rwkv_scan_fwd(r_, lw_, k2_, v_, kk_, a_, up_gather)
    w_down_b = g_down.reshape(D_FF, D_MODEL)
    mix_tiles = [o_, r_, k2_, v_, gate_, attn, x2]
    mix_params = [w_out_b, norm_mix_post, ln_x_g, ln_x_b, r_k_row, norm_ffn_pre]
    x1, h2 = tok_fwd("mix_out", mix_out_tile, mix_tiles, mix_params, [(D_MODEL, D_MODEL)], [D_MODEL, D_MODEL],
                     [F32, BF16])
    f = ffn_fwd(h2, g_up, g_conv, conv_b, w_down_b)
    dy, df, d_n_ffn_post, loss_row = loss_head(x1, f, target, norm_ffn_post)

    d_ug, d_uv, d_cw_g, d_cw_v, d_cb_g, d_cb_v, d_down = ffn_bwd_mid(h2, g_up, g_conv, conv_b, w_down_b, df)
    dh2, d_up = ffn_bwd_up(h2, g_up, d_ug, d_uv)
    half = N_DEV // 2
    d_cw = jnp.concatenate([d_cw_g[:half], d_cw_v[half:]], axis=0)
    by_pair = lambda slots: slots.reshape((N_DEV // 2, 2) + slots.shape[1:])
    ffn_mine = [by_pair(d_up), by_pair(d_down.reshape(N_DEV, D_FF // N_DEV, D_MODEL))]
    ffn_swapped = run_comm("pair_exchange_ffn", pair_plan(ffn_mine, [False, False]))
    up_exchange = chip_plan([add_pair("pair_add_w_ffn_up", ffn_mine[0], ffn_swapped[0], BF16, 256)], [False])
    down_exchange = chip_plan([add_pair("pair_add_w_ffn_down", ffn_mine[1], ffn_swapped[1], BF16, 256)], [False])
    d_cb = jnp.concatenate([d_cb_g[:, :D_FF], d_cb_v[:, D_FF:]], axis=1)
    ((d_o, d_r1, d_k1, d_v1, d_gate, d_attn, dx_res, d_n_mix_post, d_ln_g, d_ln_b, d_r_k, d_n_ffn_pre, d_w_out),
     (got_down,)) = tok_bwd("mix_out_bwd", mix_out_tile, mix_tiles, mix_params, [(D_MODEL, D_MODEL)], [dy, dh2],
                            [1, 2, 3, 4, 5], plan=down_exchange)
    (d_r2, d_lw, d_k2, d_v2, d_kk, d_a), (got_up,) = rwkv_scan_bwd(
        r_, lw_, k2_, v_, kk_, a_, states, d_o, up_exchange)
    pre_cots = [(d_r1, d_r2), d_lw, (d_k1, d_k2), (d_v1, d_v2), d_kk, d_a, d_gate]
    (d_ps, d_w0, d_wd_pad, d_a0, d_wi_pad, d_wg, d_k_k, d_k_a) = tok_bwd(
        "rwkv_pre_bwd", rwkv_pre_tile, [ps], pre_params, [], pre_cots, [0, 1, 2, 3, 4, 5, 6])
    dq, dkc, dkp, dvc, dvp, d_rel_bias, d_sinks = attn_bwd(proj, rel_bias, bucket, sinks, d_attn)
    zero_blk = jnp.zeros((BLOCK, D_KV), F32)
    dk = dkc + jnp.concatenate([dkp[BLOCK:], zero_blk], axis=0)
    dv = dvc + jnp.concatenate([dvp[BLOCK:], zero_blk], axis=0)
    dpa = jnp.concatenate([dq, dk, dv, d_ps], axis=1)
    dh1, d_w_in, d_mix_ext = in_proj_bwd(h1, w_in_b, mix_ext, proj, dpa)
    grad_x2, d_n_mix_pre = tok_bwd("rms_mix_pre_bwd", rms_tile, [x2], [norm_mix_pre], [], [dh1], [0], {0: dx_res})
    grad_x = grad_x2[None]

    small_rep = [d_n_mix_pre, d_n_mix_post, d_n_ffn_pre, d_n_ffn_post, d_rel_bias, d_sinks,
                 d_mix_ext[:, D_QKV:], d_w0, d_a0, d_k_k, d_k_a, d_r_k.reshape(r_k.shape), d_ln_g, d_ln_b, d_cb]
    rep_w = [norm_mix_pre, norm_mix_post, norm_ffn_pre, norm_ffn_post, rel_bias, sinks, rwkv_shift_mix,
             w0, a0, k_k, k_a, r_k, ln_x_g, ln_x_b, conv_b]
    rep_m = [m_norm_mix_pre, m_norm_mix_post, m_norm_ffn_pre, m_norm_ffn_post, m_rel_bias, m_sinks,
             m_rwkv_shift_mix, m_w0, m_a0, m_k_k, m_k_a, m_r_k, m_ln_x_g, m_ln_x_b, m_conv_b]
    rep_v = [v_norm_mix_pre, v_norm_mix_post, v_norm_ffn_pre, v_norm_ffn_post, v_rel_bias, v_sinks,
             v_rwkv_shift_mix, v_w0, v_a0, v_k_k, v_k_a, v_r_k, v_ln_x_g, v_ln_x_b, v_conv_b]
    sh_w = [w_decay_up, w_iclr_up, w_gate_up, conv_w]
    sh_m = [m_w_decay_up, m_w_iclr_up, m_w_gate_up, m_conv_w]
    sh_v = [v_w_decay_up, v_w_iclr_up, v_w_gate_up, v_conv_w]
    sh_parts = [_to_slots(d_wd_pad[:HEAD_DIM], HEAD_DIM), _to_slots(d_wi_pad[HEAD_DIM:], HEAD_DIM),
                _to_slots(d_wg, HEAD_DIM), d_cw]
    n_rep, n_sh = len(small_rep), len(sh_parts)
    mine = [by_pair(_to_slots(d_w_in, D_IN // N_DEV)), by_pair(d_w_out.reshape(N_DEV, D_MODEL // N_DEV, D_MODEL)),
            *small_rep, *(by_pair(p) for p in sh_parts), loss_row]
    is_rep = [False, False] + [True] * n_rep + [False] * n_sh + [True]
    adam_down, swapped = adamw("adamw_w_ffn_down", w_ffn_down[0], got_down, m_w_ffn_down[0], v_w_ffn_down[0], 128,
                               pair_plan(mine, is_rep))
    chip_sums = [add_pair("pair_add_w_in", mine[0], swapped[0], BF16, 512),
                 add_pair("pair_add_w_out", mine[1], swapped[1], BF16, 128),
                 *add_pair_small(mine[2:], swapped[2:])]
    adam_up, got = adamw("adamw_w_ffn_up", w_ffn_up[0], got_up, m_w_ffn_up[0], v_w_ffn_up[0], 128,
                         chip_plan(chip_sums, is_rep))

    big = [adamw("adamw_w_in", w_in[0], got[0], m_w_in[0], v_w_in[0], 256),
           adamw("adamw_w_out", w_out[0], got[1], m_w_out[0], v_w_out[0], 128), adam_up, adam_down]
    loss = functools.reduce(jnp.add, [got[-1][q, 0, 0] for q in range(N_DEV // 2)])
    small_w, small_g = rep_w + sh_w, got[2:-1]
    as_grad = lambda arrays: [a.reshape(g.shape[1:]) for a, g in zip(arrays, small_g)]
    small = adamw_small(as_grad(small_w), small_g, as_grad(rep_m + sh_m), as_grad(rep_v + sh_v))
    small = [[a.reshape(w.shape) for a, w in zip(kind, small_w)] for kind in small]

    names = ["norm_mix_pre", "norm_mix_post", "norm_ffn_pre", "norm_ffn_post", "w_in", "rel_bias", "sinks",
             "rwkv_shift_mix", "w0", "w_decay_up", "a0", "w_iclr_up", "w_gate_up", "k_k", "k_a", "r_k",
             "ln_x_g", "ln_x_b", "w_out", "w_ffn_up", "conv_w", "conv_b", "w_ffn_down"]
    small_names = ["norm_mix_pre", "norm_mix_post", "norm_ffn_pre", "norm_ffn_post", "rel_bias", "sinks",
                   "rwkv_shift_mix", "w0", "a0", "k_k", "k_a", "r_k", "ln_x_g", "ln_x_b", "conv_b",
                   "w_decay_up", "w_iclr_up", "w_gate_up", "conv_w"]
    big_names = {"w_in": 0, "w_out": 1, "w_ffn_up": 2, "w_ffn_down": 3}
    outs = []
    for kind in range(4):
        for nm in names:
            if nm in big_names:
                outs.append(big[big_names[nm]][kind][None])
            else:
                outs.append(small[kind][small_names.index(nm)])
    return (loss, grad_x, *outs)
```

```python
import functools
import math

import jax
import jax.numpy as jnp
from jax import lax
from jax.experimental import pallas as pl
from jax.experimental.pallas import tpu as pltpu

F32 = jnp.float32
BF16 = jnp.bfloat16

N_DEV = 8
SEQ = 2048
D_MODEL = 1024
HEAD_DIM = 64
D_ATTN = 512
D_KV = 128
D_RWKV = 512
N_HEADS = 8
RWKV_COLS = 1792
D_QKV = D_ATTN + 2 * D_KV
D_IN = D_QKV + RWKV_COLS
D_FF = 4096
BLOCK = 128
N_BLOCKS = SEQ // BLOCK
N_BUCKETS = 32
MAX_DISTANCE = 128
NORM_EPS = 1e-6
GN_EPS = 64e-5
NEG_INF = -1e30
CHUNK = 64
N_CHUNKS = SEQ // CHUNK
SCAN_GROUPS = 4
SCAN_WIDTH = D_RWKV // SCAN_GROUPS
TOK_TILE = 256
FF_TILE = 256
FF_ROW_CHUNK = 256
FF_HALO = 8
COL_TILE = 256
LANES = 128
VMEM_LIMIT = 56 * 1024 * 1024

ADAM_LR = 0.001
ADAM_B1 = 0.9
ADAM_B2 = 0.999
ADAM_EPS = 1e-08
ADAM_WD = 0.01
ADAM_STEP = 10

NT = ((1,), (1,))
TN = ((0,), (0,))
NN = ((1,), (0,))


def _sds(shape, dtype=F32):
    return jax.ShapeDtypeStruct(shape, dtype)


def _params(sem=None):
    if sem is None:
        return pltpu.CompilerParams(vmem_limit_bytes=VMEM_LIMIT)
    return pltpu.CompilerParams(dimension_semantics=sem, vmem_limit_bytes=VMEM_LIMIT)


def _dot(a, b, dims):
    return lax.dot_general(a, b, (dims, ((), ())), preferred_element_type=F32)


def _split2(x):
    hi = x.astype(BF16)
    return hi, (x - hi.astype(F32)).astype(BF16)


def _dot3_raw(a, b, dims):
    ah, al = _split2(a)
    bh, bl = _split2(b)
    return _dot(ah, bh, dims) + (_dot(al, bh, dims) + _dot(ah, bl, dims))


@functools.partial(jax.custom_vjp, nondiff_argnums=(2,))
def dot3(a, b, dims):
    return _dot3_raw(a, b, dims)


def _dot3_fwd(a, b, dims):
    return _dot3_raw(a, b, dims), (a, b)


def _dot3_bwd(dims, res, g):
    a, b = res
    if dims == NN:
        return dot3(g, b, NT), dot3(a, g, TN)
    if dims == NT:
        return dot3(g, b, NN), dot3(g, a, TN)
    return dot3(b, g, NT), dot3(a, g, NN)


dot3.defvjp(_dot3_fwd, _dot3_bwd)


@functools.partial(jax.custom_vjp, nondiff_argnums=(2,))
def dot1(a, b, dims):
    return _dot(a.astype(BF16), b.astype(BF16), dims)


def _dot1_fwd(a, b, dims):
    return dot1(a, b, dims), (a, b)


def _dot1_bwd(dims, res, g):
    a, b = res
    if dims == NN:
        return dot1(g, b, NT), dot1(a, g, TN)
    if dims == NT:
        return dot1(g, b, NN), dot1(g, a, TN)
    return dot1(b, g, NT), dot1(a, g, NN)


dot1.defvjp(_dot1_fwd, _dot1_bwd)


@jax.custom_vjp
def mm(a, b):
    return _dot(a.astype(BF16), b.astype(BF16), NN)


def _mm_fwd(a, b):
    return mm(a, b), (a, b)


def _mm_bwd(res, g):
    a, b = res
    gb = g.astype(BF16)
    return _dot(gb, b.astype(BF16), NT).astype(a.dtype), _dot(a.astype(BF16), gb, TN).astype(b.dtype)


mm.defvjp(_mm_fwd, _mm_bwd)


@jax.custom_vjp
def mm_nt(a, b):
    return _dot(a.astype(BF16), b.astype(BF16), NT)


def _mm_nt_fwd(a, b):
    return mm_nt(a, b), (a, b)


def _mm_nt_bwd(res, g):
    a, b = res
    gb = g.astype(BF16)
    return _dot(gb, b.astype(BF16), NN).astype(a.dtype), _dot(gb, a.astype(BF16), TN).astype(b.dtype)


mm_nt.defvjp(_mm_nt_fwd, _mm_nt_bwd)


@jax.custom_vjp
def mmw(a, w, wz):
    return _dot(a.astype(BF16), w, NN)


def _mmw_fwd(a, w, wz):
    return mmw(a, w, wz), (a, w)


def _mmw_bwd(res, g):
    a, w = res
    gb = g.astype(BF16)
    return _dot(gb, w, NT).astype(a.dtype), jnp.zeros_like(w), _dot(a.astype(BF16), gb, TN)


mmw.defvjp(_mmw_fwd, _mmw_bwd)


def _shift_raw(x, n):
    rows = x.shape[0]
    rolled = pltpu.roll(x, n % rows, 0)
    idx = lax.broadcasted_iota(jnp.int32, x.shape, 0)
    keep = idx >= n if n > 0 else idx < rows + n
    return jnp.where(keep, rolled, 0.0)


@functools.partial(jax.custom_vjp, nondiff_argnums=(1,))
def shift_rows(x, n):
    return _shift_raw(x, n)


def _shift_fwd(x, n):
    return _shift_raw(x, n), None


def _shift_bwd(n, _, g):
    return (_shift_raw(g, -n),)


shift_rows.defvjp(_shift_fwd, _shift_bwd)


def _head_sum(x, scale):
    a = lax.broadcasted_iota(jnp.int32, (LANES, LANES), 0) // HEAD_DIM
    b = lax.broadcasted_iota(jnp.int32, (LANES, LANES), 1) // HEAD_DIM
    pair = jnp.where(a == b, scale, 0.0).astype(F32)
    return jnp.concatenate([dot3(x[:, i:i + LANES], pair, NN) for i in range(0, x.shape[1], LANES)], axis=1)


def _rms(x, g):
    return x * lax.rsqrt(jnp.mean(x * x, axis=-1, keepdims=True) + NORM_EPS) * g


def _softplus(x):
    return jnp.maximum(x, 0.0) + jnp.log(1.0 + jnp.exp(-jnp.abs(x)))


def _tile_spec(arr, tm):
    return pl.BlockSpec((tm, arr.shape[1]), lambda i: (i, 0))


def _full_spec(arr):
    nd = arr.ndim
    return pl.BlockSpec(arr.shape, lambda i: (0,) * nd)


def tok_fwd(name, fn, tiles, params, zero_shapes, out_widths, out_dtypes, tm=TOK_TILE):
    n_t, n_p = len(tiles), len(params)

    def body(*refs):
        t_vals = [r[...] for r in refs[:n_t]]
        p_vals = [r[...] for r in refs[n_t:n_t + n_p]]
        z_vals = [jnp.zeros(s, F32) for s in zero_shapes]
        outs = fn(*t_vals, *p_vals, *z_vals)
        for r, o in zip(refs[n_t + n_p:], outs):
            r[...] = o.astype(r.dtype)

    rows = tiles[0].shape[0]
    return pl.pallas_call(
        body, grid=(rows // tm,), name=name,
        in_specs=[_tile_spec(t, tm) for t in tiles] + [_full_spec(p) for p in params],
        out_specs=[pl.BlockSpec((tm, w), lambda i: (i, 0)) for w in out_widths],
        out_shape=[_sds((rows, w), dt) for w, dt in zip(out_widths, out_dtypes)],
        compiler_params=_params(("arbitrary",)),
    )(*tiles, *params)


def tok_bwd(name, fn, tiles, params, zero_shapes, cots, diff_params, residuals=(), plan=None, tm=TOK_TILE):
    cot_parts = [c if isinstance(c, tuple) else (c,) for c in cots]
    flat_cots = [a for part in cot_parts for a in part]
    residuals = dict(residuals)
    extra = [residuals[i] for i in sorted(residuals)]
    n_t, n_p, n_c, n_r = len(tiles), len(params), len(flat_cots), len(extra)
    acc_shapes = [params[i].shape for i in diff_params] + list(zero_shapes)

    def body(*refs):
        t_vals = [r[...].astype(F32) for r in refs[:n_t]]
        p_vals = [r[...] for r in refs[n_t:n_t + n_p]]
        flat = iter(r[...] for r in refs[n_t + n_p:n_t + n_p + n_c])
        c_vals = [functools.reduce(jnp.add, [next(flat) for _ in part]) for part in cot_parts]
        r_vals = dict(zip(sorted(residuals), (r[...] for r in refs[n_t + n_p + n_c:n_t + n_p + n_c + n_r])))
        out_refs = refs[n_t + n_p + n_c + n_r:]
        z_vals = [jnp.zeros(s, F32) for s in zero_shapes]
        d_vals = [p_vals[i] for i in diff_params]

        def f(t_in, d_in, z_in):
            full = list(p_vals)
            for i, v in zip(diff_params, d_in):
                full[i] = v
            return tuple(fn(*t_in, *full, *z_in))

        _, vjp = jax.vjp(f, t_vals, d_vals, z_vals)
        g_t, g_d, g_z = vjp(tuple(c_vals))
        for i, (r, g) in enumerate(zip(out_refs[:n_t], g_t)):
            r[...] = (g + r_vals[i] if i in r_vals else g).astype(r.dtype)
        acc_refs = out_refs[n_t:]

        @pl.when(pl.program_id(0) == 0)
        def _():
            for r in acc_refs:
                r[...] = jnp.zeros_like(r)

        for r, g in zip(acc_refs, list(g_d) + list(g_z)):
            r[...] += g

    rows = tiles[0].shape[0]
    in_specs = ([_tile_spec(t, tm) for t in tiles] + [_full_spec(p) for p in params]
                + [_tile_spec(c, tm) for c in flat_cots + extra])
    out_specs = ([_tile_spec(t, tm) for t in tiles]
                 + [pl.BlockSpec(s, lambda i, nd=len(s): (0,) * nd) for s in acc_shapes])
    out_shape = [_sds(t.shape) for t in tiles] + [_sds(s) for s in acc_shapes]
    operands = (*tiles, *params, *flat_cots, *extra)
    if plan is not None:
        return call_with_comm(plan, 0, body, (rows // tm,), name, in_specs, out_specs, out_shape, [], operands)
    return pl.pallas_call(
        body, grid=(rows // tm,), name=name, in_specs=in_specs, out_specs=out_specs, out_shape=out_shape,
        compiler_params=_params(("arbitrary",)),
    )(*operands)


def rms_tile(x, g):
    return (_rms(x, g),)


def rwkv_pre_tile(ps, w0, wd_pad, a0, wi_pad, wg, k_k, k_a):
    r = ps[:, 0:D_RWKV]
    k = ps[:, D_RWKV:2 * D_RWKV]
    v = ps[:, 2 * D_RWKV:3 * D_RWKV]
    z2 = ps[:, 3 * D_RWKV:3 * D_RWKV + LANES]
    zg = ps[:, 3 * D_RWKV + LANES:RWKV_COLS]
    w_log = -_softplus(-(w0 + mm(jnp.tanh(z2), wd_pad))) - 0.5
    lw = -jnp.exp(w_log)
    a = jax.nn.sigmoid(a0 + mm(z2, wi_pad))
    g = mm(jax.nn.sigmoid(zg), wg)
    kk = k * k_k
    norm = jnp.sqrt(_head_sum(kk * kk, 1.0))
    kk = kk / jnp.maximum(norm, 1e-12)
    k2 = k * (1.0 + (a - 1.0) * k_a)
    return r, lw, k2, v, kk, a, g


def mix_out_tile(o, r, k2, v, g, attn, x, w_out, n_post, ln_g, ln_b, r_k, n_ffn_pre, wz):
    d = o - _head_sum(o, 1.0 / HEAD_DIM)
    var = _head_sum(d * d, 1.0 / HEAD_DIM)
    on = d * lax.rsqrt(var + GN_EPS) * ln_g + ln_b
    bonus = _head_sum(r * k2 * r_k, 1.0) * v
    rw = (on + bonus) * g
    mix = mmw(jnp.concatenate([attn, rw], axis=1), w_out, wz)
    x1 = x + _rms(mix, n_post)
    return x1, _rms(x1, n_ffn_pre)


def in_proj_fwd(h, w_in, mix_ext, plan):
    def body(h_ref, w_ref, m_ref, proj_ref, ps_ref):
        p = _dot(h_ref[...], w_ref[...], NN)
        proj_ref[...] = p
        ps_ref[...] = p + (_shift_raw(p, 1) - p) * m_ref[...]

    n = D_IN // COL_TILE
    first = D_QKV // COL_TILE
    return call_with_comm(
        plan, n // 2, body, (n,), "in_proj_fwd",
        [pl.BlockSpec((SEQ, D_MODEL), lambda j: (0, 0)), pl.BlockSpec((D_MODEL, COL_TILE), lambda j: (0, j)),
         pl.BlockSpec((1, COL_TILE), lambda j: (0, j))],
        [pl.BlockSpec((SEQ, COL_TILE), lambda j: (0, j)),
         pl.BlockSpec((SEQ, COL_TILE), lambda j: (0, jnp.maximum(j - first, 0)))],
        [_sds((SEQ, D_IN)), _sds((SEQ, RWKV_COLS))], [], (h, w_in, mix_ext))


def in_proj_bwd(h, w_in, mix_ext, proj, dpa):
    def body(h_ref, w_ref, m_ref, p_ref, d_ref, dh_ref, dw_ref, dm_ref):
        d = d_ref[...]
        p = p_ref[...]
        dm_ref[...] = jnp.sum(d * (_shift_raw(p, 1) - p), axis=0, keepdims=True)
        dmix = d * m_ref[...]
        dp = (d - dmix + _shift_raw(dmix, -1)).astype(BF16)
        dw_ref[...] = _dot(h_ref[...], dp, TN)

        @pl.when(pl.program_id(0) == 0)
        def _():
            dh_ref[...] = jnp.zeros_like(dh_ref)

        dh_ref[...] += _dot(dp, w_ref[...], NT)

    n = D_IN // COL_TILE
    col = lambda rows: pl.BlockSpec((rows, COL_TILE), lambda j: (0, j))
    return pl.pallas_call(
        body, grid=(n,), name="in_proj_bwd",
        in_specs=[pl.BlockSpec((SEQ, D_MODEL), lambda j: (0, 0)), col(D_MODEL), col(1), col(SEQ), col(SEQ)],
        out_specs=[pl.BlockSpec((SEQ, D_MODEL), lambda j: (0, 0)), col(D_MODEL), col(1)],
        out_shape=[_sds((SEQ, D_MODEL)), _sds((D_MODEL, D_IN)), _sds((1, D_IN))],
        compiler_params=_params(("arbitrary",)),
    )(h, w_in, mix_ext, proj, dpa)


def _bucket_table():
    rel = (jnp.arange(BLOCK)[:, None] + BLOCK) - jnp.arange(2 * BLOCK)[None, :]
    n = jnp.maximum(rel, 0)
    max_exact = N_BUCKETS // 2
    large = max_exact + (jnp.log(jnp.maximum(n, 1).astype(F32) / max_exact)
                         / math.log(MAX_DISTANCE / max_exact) * (N_BUCKETS - max_exact)).astype(jnp.int32)
    large = jnp.minimum(large, N_BUCKETS - 1)
    return jnp.where(n < max_exact, n, large).astype(jnp.int32)


def _select_matrix(g, o):
    a = lax.broadcasted_iota(jnp.int32, (D_KV, D_KV), 0)
    b = lax.broadcasted_iota(jnp.int32, (D_KV, D_KV), 1)
    return ((a - HEAD_DIM * g == b - o) & (b >= o) & (b < o + HEAD_DIM)).astype(F32)


def _attn_block(q, kp, kc, vp, vc, bias, sinks, block_idx):
    kb = jnp.concatenate([kp, kc], axis=0)
    vb = jnp.concatenate([vp, vc], axis=0)
    row = lax.broadcasted_iota(jnp.int32, (BLOCK, 2 * BLOCK), 0)
    col = lax.broadcasted_iota(jnp.int32, (BLOCK, 2 * BLOCK), 1)
    rel = row + BLOCK - col
    mask = (rel >= 0) & (rel < BLOCK) & (col + (block_idx - 1) * BLOCK >= 0)
    lane8 = lax.broadcasted_iota(jnp.int32, (1, N_HEADS), 1)
    kt, vt = {}, {}
    for g in range(2):
        for o in (0, HEAD_DIM):
            sel = _select_matrix(g, o)
            kt[g, o] = mm(kb, sel)
            vt[g, o] = mm(vb, sel)
    outs = []
    for j in range(D_ATTN // LANES):
        qs = q[:, j * LANES:(j + 1) * LANES]
        acc = None
        for half in range(2):
            hq = 2 * j + half
            g, o = hq // 4, half * HEAD_DIM
            s = mm_nt(qs, kt[g, o]) * (HEAD_DIM ** -0.5) + bias[hq]
            s = jnp.where(mask, s, NEG_INF)
            sink = jnp.sum(jnp.where(lane8 == hq, sinks, 0.0), axis=1, keepdims=True)
            m = lax.stop_gradient(jnp.maximum(jnp.max(s, axis=-1, keepdims=True), sink))
            p = jnp.exp(s - m)
            probs = p / (jnp.sum(p, axis=-1, keepdims=True) + jnp.exp(sink - m))
            part = mm(probs, vt[g, o])
            acc = part if acc is None else acc + part
        outs.append(acc)
    return jnp.concatenate(outs, axis=1)


def _build_bias(rb_ref, bucket, bias_ref):
    for hq in range(N_HEADS):
        acc = jnp.zeros((BLOCK, 2 * BLOCK), F32)
        for b in range(N_BUCKETS):
            acc = jnp.where(bucket == b, rb_ref[b, hq], acc)
        bias_ref[hq] = acc


def _attn_in_specs():
    prev = lambda n: jnp.maximum(n - 1, 0)
    return [pl.BlockSpec((BLOCK, D_ATTN), lambda n: (n, 0)),
            pl.BlockSpec((BLOCK, D_KV), lambda n: (prev(n), D_ATTN // D_KV)),
            pl.BlockSpec((BLOCK, D_KV), lambda n: (n, D_ATTN // D_KV)),
            pl.BlockSpec((BLOCK, D_KV), lambda n: (prev(n), D_ATTN // D_KV + 1)),
            pl.BlockSpec((BLOCK, D_KV), lambda n: (n, D_ATTN // D_KV + 1)),
            pl.BlockSpec(memory_space=pltpu.SMEM),
            pl.BlockSpec((BLOCK, 2 * BLOCK), lambda n: (0, 0)),
            pl.BlockSpec((1, N_HEADS), lambda n: (0, 0))]


def attn_fwd(proj, rel_bias, bucket, sinks, plan):
    def body(q_ref, kp_ref, kc_ref, vp_ref, vc_ref, rb_ref, bk_ref, sk_ref, o_ref, bias_ref):
        n = pl.program_id(0)

        @pl.when(n == 0)
        def _():
            _build_bias(rb_ref, bk_ref[...], bias_ref)

        o_ref[...] = _attn_block(q_ref[...], kp_ref[...], kc_ref[...], vp_ref[...], vc_ref[...],
                                 tuple(bias_ref[h] for h in range(N_HEADS)), sk_ref[...], n)

    (attn,), gathered = call_with_comm(
        plan, 3 * N_BLOCKS // 4, body, (N_BLOCKS,), "attn_fwd", _attn_in_specs(),
        [pl.BlockSpec((BLOCK, D_ATTN), lambda n: (n, 0))], [_sds((SEQ, D_ATTN))],
        [pltpu.VMEM((N_HEADS, BLOCK, 2 * BLOCK), F32)], (proj, proj, proj, proj, proj, rel_bias, bucket, sinks))
    return attn, gathered


def attn_bwd(proj, rel_bias, bucket, sinks, d_attn):
    def body(q_ref, kp_ref, kc_ref, vp_ref, vc_ref, rb_ref, bk_ref, sk_ref, do_ref,
             dq_ref, dkc_ref, dkp_ref, dvc_ref, dvp_ref, drb_ref, dsk_ref, bias_ref, dbias_ref):
        n = pl.program_id(0)

        @pl.when(n == 0)
        def _():
            _build_bias(rb_ref, bk_ref[...], bias_ref)
            dbias_ref[...] = jnp.zeros_like(dbias_ref)
            dsk_ref[...] = jnp.zeros_like(dsk_ref)

        f = lambda q, kp, kc, vp, vc, bias, sk: _attn_block(q, kp, kc, vp, vc, bias, sk, n)
        _, vjp = jax.vjp(f, q_ref[...], kp_ref[...], kc_ref[...], vp_ref[...], vc_ref[...],
                         tuple(bias_ref[h] for h in range(N_HEADS)), sk_ref[...])
        dq, dkp, dkc, dvp, dvc, dbias, dsk = vjp(do_ref[...])
        dq_ref[...] = dq
        dkc_ref[...] = dkc
        dkp_ref[...] = dkp
        dvc_ref[...] = dvc
        dvp_ref[...] = dvp
        for h in range(N_HEADS):
            dbias_ref[h] += dbias[h]
        dsk_ref[...] += dsk

        @pl.when(n == N_BLOCKS - 1)
        def _():
            bucket_v = bk_ref[...]
            rowi = lax.broadcasted_iota(jnp.int32, (N_BUCKETS, 2 * BLOCK), 0)
            lane = lax.broadcasted_iota(jnp.int32, (N_BUCKETS, N_HEADS), 1)
            out = jnp.zeros((N_BUCKETS, N_HEADS), F32)
            for hq in range(N_HEADS):
                dbh = dbias_ref[hq]
                rows = jnp.zeros((N_BUCKETS, 2 * BLOCK), F32)
                for b in range(N_BUCKETS):
                    part = jnp.sum(jnp.where(bucket_v == b, dbh, 0.0), axis=0, keepdims=True)
                    rows = jnp.where(rowi == b, part, rows)
                tot = jnp.sum(rows, axis=1, keepdims=True)
                out = jnp.where(lane == hq, tot, out)
            drb_ref[...] = out

    blk = lambda w: pl.BlockSpec((BLOCK, w), lambda n: (n, 0))
    return pl.pallas_call(
        body, grid=(N_BLOCKS,), name="attn_bwd",
        in_specs=_attn_in_specs() + [blk(D_ATTN)],
        out_specs=[blk(D_ATTN), blk(D_KV), blk(D_KV), blk(D_KV), blk(D_KV),
                   pl.BlockSpec((N_BUCKETS, N_HEADS), lambda n: (0, 0)),
                   pl.BlockSpec((1, N_HEADS), lambda n: (0, 0))],
        out_shape=[_sds((SEQ, D_ATTN)), _sds((SEQ, D_KV)), _sds((SEQ, D_KV)), _sds((SEQ, D_KV)),
                   _sds((SEQ, D_KV)), _sds((N_BUCKETS, N_HEADS)), _sds((1, N_HEADS))],
        scratch_shapes=[pltpu.VMEM((N_HEADS, BLOCK, 2 * BLOCK), F32),
                        pltpu.VMEM((N_HEADS, BLOCK, 2 * BLOCK), F32)],
        compiler_params=_params(("arbitrary",)),
    )(proj, proj, proj, proj, proj, rel_bias, bucket, sinks, d_attn)


def _stack(x, size):
    groups = x.shape[1] // size
    lane = lax.broadcasted_iota(jnp.int32, x.shape, 1) // size
    return jnp.concatenate([jnp.where(lane == i, x, 0.0) for i in range(groups)], axis=0)


def _neumann(l):
    c = CHUNK
    t = lax.broadcasted_iota(jnp.int32, l.shape, 0)
    i = lax.broadcasted_iota(jnp.int32, l.shape, 1) % c
    inv = (i == t).astype(F32) + l
    pw = dot1(l, _stack(l, c), NN)
    for _ in range(int(math.log2(c)) - 2):
        both = dot1(jnp.concatenate([inv, pw], axis=0), _stack(pw, c), NN)
        inv = inv + both[:c]
        pw = both[c:]
    return inv + dot1(inv, _stack(pw, c), NN)


@jax.custom_vjp
def neumann_inv(l):
    return _neumann(l)


def _neumann_fwd(l):
    inv = _neumann(l)
    return inv, inv


def _neumann_bwd(inv, g):
    c = CHUNK
    bd_t = _stack(inv, c).T
    inv_t = bd_t[0:c]
    for h in range(1, inv.shape[1] // c):
        inv_t = inv_t + bd_t[h * c:(h + 1) * c]
    return (dot1(dot1(inv_t, _stack(g, c), NN), bd_t, NN),)


neumann_inv.defvjp(_neumann_fwd, _neumann_bwd)


def _cumsum_raw(x, dims):
    c = x.shape[0]
    tt = lax.broadcasted_iota(jnp.int32, (c, c), 0)
    ii = lax.broadcasted_iota(jnp.int32, (c, c), 1)
    tri = (ii <= tt).astype(BF16)
    hi = x.astype(BF16)
    rest = x - hi.astype(F32)
    mid = rest.astype(BF16)
    lo = (rest - mid.astype(F32)).astype(BF16)
    return _dot(tri, hi, dims) + (_dot(tri, mid, dims) + _dot(tri, lo, dims))


@jax.custom_vjp
def cumsum_rows(x):
    return _cumsum_raw(x, NN)


def _cumsum_fwd(x):
    return _cumsum_raw(x, NN), None


def _cumsum_bwd(_, g):
    return (_cumsum_raw(g, TN),)


cumsum_rows.defvjp(_cumsum_fwd, _cumsum_bwd)


def _rwkv_chunk(s0, r, lw, k, v, kk, a):
    heads = r.shape[1] // HEAD_DIM
    c, hc = CHUNK, heads * CHUNK
    t = lax.broadcasted_iota(jnp.int32, (c, hc), 0)
    i = lax.broadcasted_iota(jnp.int32, (c, hc), 1) % c
    strict, incl = i < t, i <= t
    stack = lambda x: _stack(x, HEAD_DIM)
    ba = lax.broadcasted_iota(jnp.int32, s0.shape, 0) // HEAD_DIM
    bb = lax.broadcasted_iota(jnp.int32, s0.shape, 1) // HEAD_DIM
    blocks = (ba == bb).astype(F32)

    cum = cumsum_rows(lw)
    cum_end = jnp.sum(lw, axis=0, keepdims=True)
    beta = kk * a
    al = -kk * jnp.exp(cum - lw)
    p_inv = jnp.exp(-cum)
    be, kb, rb = beta * p_inv, k * p_inv, r * jnp.exp(cum)
    ar = jnp.concatenate([al, rb], axis=0)
    sv = stack(v)
    l_all = dot1(ar, jnp.concatenate([stack(be), stack(kb)], axis=0), NT)
    l_ab = jnp.where(strict, l_all[:c, :hc], 0.0)
    l_ak = jnp.where(strict, l_all[:c, hc:], 0.0)
    l_rb = jnp.where(incl, l_all[c:, :hc], 0.0)
    l_rk = jnp.where(incl, l_all[c:, hc:], 0.0)
    inv = neumann_inv(l_ab)
    from_s0 = dot1(ar, s0, NT)
    from_v = dot1(jnp.concatenate([l_ak, l_rk], axis=0), sv, NN)
    u = dot1(inv, stack(from_s0[:c] + from_v[:c]), NN)
    o = from_s0[c:] + from_v[c:] + dot1(l_rb, stack(u), NN)
    to_end = jnp.exp(cum_end - cum)
    s1 = s0 * jnp.exp(cum_end) + blocks * dot1(
        jnp.concatenate([u, v], axis=0), jnp.concatenate([beta * to_end, k * to_end], axis=0), TN)
    return o, s1


def call_with_comm(plan, middle_step, body, grid, name, in_specs, out_specs, out_shape, scratch_shapes, operands):
    n_in, n_out, n_scr = len(in_specs), len(out_specs), len(scratch_shapes)
    p_in, p_out = len(plan.ins), len(plan.out_shape)

    def fused(*refs):
        refs = list(refs)
        ins, refs = refs[:n_in], refs[n_in:]
        p_ins, refs = refs[:p_in], refs[p_in:]
        outs, refs = refs[:n_out], refs[n_out:]
        p_outs, refs = refs[:p_out], refs[p_out:]
        scr, p_sems = refs[:n_scr], refs[n_scr:]
        start, middle, finish = plan.stages(p_ins, p_outs, p_sems)
        step = pl.program_id(0)
        pl.when(step == 0)(start)
        body(*ins, *outs, *scr)
        pl.when(step == middle_step)(middle)
        pl.when(step == grid[0] - 1)(finish)

    any_spec = pl.BlockSpec(memory_space=pl.ANY)
    res = pl.pallas_call(
        fused, grid=grid, name=name,
        in_specs=list(in_specs) + [any_spec] * p_in, out_specs=list(out_specs) + [any_spec] * p_out,
        out_shape=list(out_shape) + list(plan.out_shape), scratch_shapes=list(scratch_shapes) + list(plan.scratch),
        compiler_params=_params(("arbitrary",)),
    )(*operands, *plan.ins)
    return res[:n_out], res[n_out:]


def _by_group(ref):
    return jnp.stack([ref[:, g * SCAN_WIDTH:(g + 1) * SCAN_WIDTH] for g in range(SCAN_GROUPS)])


def _store_groups(ref, val):
    for g in range(SCAN_GROUPS):
        ref[:, g * SCAN_WIDTH:(g + 1) * SCAN_WIDTH] = val[g]


def rwkv_scan_fwd(r, lw, k, v, kk, a, plan):
    def body(r_ref, lw_ref, k_ref, v_ref, kk_ref, a_ref, o_ref, st_ref, s_ref):
        @pl.when(pl.program_id(0) == 0)
        def _():
            s_ref[...] = jnp.zeros_like(s_ref)

        s0 = s_ref[...]
        st_ref[0] = s0
        o, s1 = jax.vmap(_rwkv_chunk)(s0, *(_by_group(ref) for ref in (r_ref, lw_ref, k_ref, v_ref, kk_ref, a_ref)))
        _store_groups(o_ref, o)
        s_ref[...] = s1

    tb = pl.BlockSpec((CHUNK, D_RWKV), lambda c: (c, 0))
    state = (SCAN_GROUPS, SCAN_WIDTH, SCAN_WIDTH)
    return call_with_comm(
        plan, 3 * N_CHUNKS // 4, body, (N_CHUNKS,), "rwkv_scan_fwd",
        [tb] * 6, [tb, pl.BlockSpec((1,) + state, lambda c: (c, 0, 0, 0))],
        [_sds((SEQ, D_RWKV)), _sds((N_CHUNKS,) + state)], [pltpu.VMEM(state, F32)], (r, lw, k, v, kk, a))


def rwkv_scan_bwd(r, lw, k, v, kk, a, states, d_o, plan):
    def body(r_ref, lw_ref, k_ref, v_ref, kk_ref, a_ref, st_ref, do_ref,
             dr_ref, dlw_ref, dk_ref, dv_ref, dkk_ref, da_ref, ds_ref):
        @pl.when(pl.program_id(0) == 0)
        def _():
            ds_ref[...] = jnp.zeros_like(ds_ref)

        _, vjp = jax.vjp(jax.vmap(_rwkv_chunk), st_ref[0],
                         *(_by_group(ref) for ref in (r_ref, lw_ref, k_ref, v_ref, kk_ref, a_ref)))
        grads = vjp((_by_group(do_ref), ds_ref[...]))
        ds_ref[...] = grads[0]
        for ref, val in zip((dr_ref, dlw_ref, dk_ref, dv_ref, dkk_ref, da_ref), grads[1:]):
            _store_groups(ref, val)

    last = N_CHUNKS - 1
    tb = pl.BlockSpec((CHUNK, D_RWKV), lambda c: (last - c, 0))
    state = (SCAN_GROUPS, SCAN_WIDTH, SCAN_WIDTH)
    return call_with_comm(
        plan, N_CHUNKS // 4, body, (N_CHUNKS,), "rwkv_scan_bwd",
        [tb] * 6 + [pl.BlockSpec((1,) + state, lambda c: (last - c, 0, 0, 0)), tb], [tb] * 6,
        [_sds((SEQ, D_RWKV))] * 6, [pltpu.VMEM(state, F32)], (r, lw, k, v, kk, a, states, d_o))


def _ffn_mid(ug, uv, cg, cv, bg, bv):
    conv_g = bg + cg[0] * shift_rows(ug, 2) + cg[1] * shift_rows(ug, 1) + cg[2] * ug
    conv_v = bv + cv[0] * shift_rows(uv, 2) + cv[1] * shift_rows(uv, 1) + cv[2] * uv
    return jax.nn.gelu(conv_g, approximate=True) * conv_v


def _conv_rows(ref):
    return tuple(ref[0, j:j + 1, :] for j in range(3))


def _ffn_specs(tile):
    per = D_MODEL // tile
    half = N_DEV // 2
    w_g = pl.BlockSpec((1, D_MODEL, tile), lambda t: (t // per, 0, t % per))
    w_v = pl.BlockSpec((1, D_MODEL, tile), lambda t: (half + t // per, 0, t % per))
    c_g = pl.BlockSpec((1, 3, tile), lambda t: (t // per, 0, t % per))
    c_v = pl.BlockSpec((1, 3, tile), lambda t: (half + t // per, 0, t % per))
    b_g = pl.BlockSpec((1, tile), lambda t: (0, t))
    b_v = pl.BlockSpec((1, tile), lambda t: (0, D_FF // tile + t))
    w_d = pl.BlockSpec((tile, D_MODEL), lambda t: (t, 0))
    return w_g, w_v, c_g, c_v, b_g, b_v, w_d


def ffn_fwd(h2, w_up, conv_w, conv_b, w_down):
    def body(h_ref, wg_ref, wv_ref, cg_ref, cv_ref, bg_ref, bv_ref, wd_ref, f_ref):
        @pl.when(pl.program_id(0) == 0)
        def _():
            f_ref[...] = jnp.zeros_like(f_ref)

        h = h_ref[...]
        act = _ffn_mid(_dot(h, wg_ref[0], NN), _dot(h, wv_ref[0], NN), _conv_rows(cg_ref), _conv_rows(cv_ref),
                       bg_ref[...], bv_ref[...])
        f_ref[...] += _dot(act.astype(BF16), wd_ref[...], NN)

    full = pl.BlockSpec((SEQ, D_MODEL), lambda t: (0, 0))
    return pl.pallas_call(
        body, grid=(D_FF // FF_TILE,), name="ffn_fwd",
        in_specs=[full, *_ffn_specs(FF_TILE)],
        out_specs=full, out_shape=_sds((SEQ, D_MODEL)),
        compiler_params=_params(("arbitrary",)),
    )(h2, w_up, w_up, conv_w, conv_w, conv_b, conv_b, w_down)


def ffn_bwd_mid(h2, w_up, conv_w, conv_b, w_down, df):
    tile, rows, halo = FF_TILE, FF_ROW_CHUNK, FF_HALO
    ext = rows + 2 * halo

    def body(h_hbm, wg_ref, wv_ref, cg_ref, cv_ref, bg_ref, bv_ref, wd_ref, df_hbm,
             dug_ref, duv_ref, dcg_ref, dcv_ref, dbg_ref, dbv_ref, dwd_ref,
             h_ref, df_ref, ug_ref, uv_ref, da_ref, act_ref):
        @pl.when(pl.program_id(0) == 0)
        def _():
            pltpu.sync_copy(h_hbm, h_ref)
            pltpu.sync_copy(df_hbm, df_ref)
            for ref in (ug_ref, uv_ref, da_ref):
                ref[0:halo, :] = jnp.zeros((halo, tile), F32)
                ref[halo + SEQ:, :] = jnp.zeros((halo, tile), F32)

        h, df_b = h_ref[...], df_ref[...]
        ug_ref[halo:halo + SEQ, :] = _dot(h, wg_ref[0], NN)
        uv_ref[halo:halo + SEQ, :] = _dot(h, wv_ref[0], NN)
        da_ref[halo:halo + SEQ, :] = _dot(df_b, wd_ref[...], NT)
        cg, cv, bg, bv = _conv_rows(cg_ref), _conv_rows(cv_ref), bg_ref[...], bv_ref[...]
        down = lambda x, n: pltpu.roll(x, n, 0)
        up = lambda x, n: pltpu.roll(x, ext - n, 0)
        mid = slice(halo, halo + rows)

        def chunk(i, sums):
            r0 = pl.multiple_of(i * rows, rows)
            window = pl.ds(r0, ext)
            ug, uv, da = ug_ref[window, :], uv_ref[window, :], da_ref[window, :]
            ug1, ug2, uv1, uv2 = down(ug, 1), down(ug, 2), down(uv, 1), down(uv, 2)
            conv_g = bg + cg[0] * ug2 + cg[1] * ug1 + cg[2] * ug
            conv_v = bv + cv[0] * uv2 + cv[1] * uv1 + cv[2] * uv
            act, vjp = jax.vjp(lambda a, b: jax.nn.gelu(a, approximate=True) * b, conv_g, conv_v)
            dcg, dcv = vjp(da)
            dug = cg[2] * dcg + cg[1] * up(dcg, 1) + cg[0] * up(dcg, 2)
            duv = cv[2] * dcv + cv[1] * up(dcv, 1) + cv[0] * up(dcv, 2)
            out = pl.ds(r0, rows)
            act_ref[out, :] = act[mid].astype(BF16)
            dug_ref[out, :] = dug[mid].astype(BF16)
            duv_ref[out, :] = duv[mid].astype(BF16)
            col = lambda x: jnp.sum(x[mid], axis=0, keepdims=True)
            new = (col(dcg * ug2), col(dcg * ug1), col(dcg * ug), col(dcv * uv2), col(dcv * uv1), col(dcv * uv),
                   col(dcg), col(dcv))
            return tuple(s + n for s, n in zip(sums, new))

        zero = jnp.zeros((1, tile), F32)
        sums = lax.fori_loop(0, SEQ // rows, chunk, (zero,) * 8)
        for j in range(3):
            dcg_ref[0, j:j + 1, :] = sums[j]
            dcv_ref[0, j:j + 1, :] = sums[3 + j]
        dbg_ref[...] = sums[6]
        dbv_ref[...] = sums[7]
        dwd_ref[...] = _dot(act_ref[...], df_b, TN).astype(BF16)

    hbm = pl.BlockSpec(memory_space=pl.ANY)
    w_g, w_v, c_g, c_v, b_g, b_v, w_d = _ffn_specs(tile)
    col = pl.BlockSpec((SEQ, tile), lambda t: (0, t))
    padded = pltpu.VMEM((SEQ + 2 * halo, tile), F32)
    return pl.pallas_call(
        body, grid=(D_FF // tile,), name="ffn_bwd_mid",
        in_specs=[hbm, w_g, w_v, c_g, c_v, b_g, b_v, w_d, hbm],
        out_specs=[col, col, c_g, c_v, b_g, b_v, w_d],
        out_shape=[_sds((SEQ, D_FF), BF16), _sds((SEQ, D_FF), BF16), _sds((N_DEV, 3, D_MODEL)),
                   _sds((N_DEV, 3, D_MODEL)), _sds((1, 2 * D_FF)), _sds((1, 2 * D_FF)), _sds((D_FF, D_MODEL), BF16)],
        scratch_shapes=[pltpu.VMEM((SEQ, D_MODEL), BF16), pltpu.VMEM((SEQ, D_MODEL), BF16), padded, padded, padded,
                        pltpu.VMEM((SEQ, tile), BF16)],
        compiler_params=_params(("arbitrary",)),
    )(h2, w_up, w_up, conv_w, conv_w, conv_b, conv_b, w_down, df)


def ffn_bwd_up(h2, w_up, dug, duv):
    tile = FF_TILE
    per = D_MODEL // tile

    def body(h_hbm, wg_ref, wv_ref, dug_ref, duv_ref, dh_hbm, dup_hbm, h_ref, dh_ref, dwg_ref, dwv_ref, sem, up_sems):
        t = pl.program_id(0)

        @pl.when(t == 0)
        def _():
            pltpu.sync_copy(h_hbm, h_ref)
            dh_ref[...] = jnp.zeros_like(dh_ref)

        h, dug_b, duv_b = h_ref[...], dug_ref[...], duv_ref[...]
        cols = pl.ds(pl.multiple_of((t % per) * tile, tile), tile)
        to_gate = pltpu.make_async_copy(dwg_ref, dup_hbm.at[t // per, :, cols], up_sems.at[0])
        to_value = pltpu.make_async_copy(dwv_ref, dup_hbm.at[N_DEV // 2 + t // per, :, cols], up_sems.at[1])
        dwg_ref[...] = _dot(h, dug_b, TN).astype(BF16)
        to_gate.start()
        dwv_ref[...] = _dot(h, duv_b, TN).astype(BF16)
        to_value.start()
        dh_ref[...] += _dot(jnp.concatenate([dug_b, duv_b], axis=1),
                            jnp.concatenate([wg_ref[0], wv_ref[0]], axis=1), NT)
        to_gate.wait()
        to_value.wait()

        @pl.when(t == D_FF // tile - 1)
        def _():
            cp = pltpu.make_async_copy(dh_ref, dh_hbm, sem)
            cp.start()
            cp.wait()

    hbm = pl.BlockSpec(memory_space=pl.ANY)
    w_g, w_v = _ffn_specs(tile)[:2]
    col = pl.BlockSpec((SEQ, tile), lambda t: (0, t))
    return pl.pallas_call(
        body, grid=(D_FF // tile,), name="ffn_bwd_up",
        in_specs=[hbm, w_g, w_v, col, col], out_specs=[hbm, hbm],
        out_shape=[_sds((SEQ, D_MODEL)), _sds((N_DEV, D_MODEL, D_MODEL), BF16)],
        scratch_shapes=[pltpu.VMEM((SEQ, D_MODEL), BF16), pltpu.VMEM((SEQ, D_MODEL), F32),
                        pltpu.VMEM((D_MODEL, tile), BF16), pltpu.VMEM((D_MODEL, tile), BF16),
                        pltpu.SemaphoreType.DMA, pltpu.SemaphoreType.DMA((2,))],
        compiler_params=_params(("arbitrary",)),
    )(h2, w_up, w_up, dug, duv)


def loss_head(x1, f, target, n_post):
    def tile_loss(x1_t, f_t, g, tgt):
        err = x1_t + _rms(f_t, g) - tgt
        return 0.5 * jnp.sum(jnp.mean(err * err, axis=-1))

    def body(x_ref, f_ref, t_ref, g_ref, dx_ref, df_ref, dg_ref, loss_ref):
        val, (dx, df, dg) = jax.value_and_grad(tile_loss, argnums=(0, 1, 2))(
            x_ref[...], f_ref[...], g_ref[...], t_ref[...])
        dx_ref[...] = dx
        df_ref[...] = df.astype(BF16)

        @pl.when(pl.program_id(0) == 0)
        def _():
            dg_ref[...] = jnp.zeros_like(dg_ref)
            loss_ref[...] = jnp.zeros_like(loss_ref)

        dg_ref[...] += dg
        loss_ref[...] += jnp.full((1, LANES), val, F32)

    tile = pl.BlockSpec((TOK_TILE, D_MODEL), lambda i: (i, 0))
    vec = pl.BlockSpec((1, D_MODEL), lambda i: (0, 0))
    return pl.pallas_call(
        body, grid=(SEQ // TOK_TILE,), name="loss_head",
        in_specs=[tile, tile, tile, vec],
        out_specs=[tile, tile, vec, pl.BlockSpec((1, LANES), lambda i: (0, 0))],
        out_shape=[_sds((SEQ, D_MODEL)), _sds((SEQ, D_MODEL), BF16), _sds((1, D_MODEL)), _sds((1, LANES))],
        compiler_params=_params(("arbitrary",)),
    )(x1, f, target, n_post)


def _mesh_pos():
    return lax.axis_index("x"), lax.axis_index("y"), lax.axis_index("c")


def _flip(pos, rel):
    x, y, c = pos
    return (1 - x if rel & 4 else x, 1 - y if rel & 2 else y, 1 - c if rel & 1 else c)


def _slot(pos):
    x, y, c = pos
    return 4 * x + 2 * y + c


def cast_bf16(w, rows):
    def body(w_ref, o_ref):
        o_ref[...] = w_ref[...].astype(BF16)

    spec = pl.BlockSpec((rows, w.shape[1]), lambda i: (i, 0))
    return pl.pallas_call(body, grid=(w.shape[0] // rows,), name="cast_bf16_%dx%d" % w.shape,
                          in_specs=[spec], out_specs=spec, out_shape=_sds(w.shape, BF16),
                          compiler_params=_params(("arbitrary",)))(w)


class CommPlan:
    def __init__(self, ins, out_shape, scratch, stages):
        self.ins, self.out_shape, self.scratch, self.stages = ins, out_shape, scratch, stages


def run_comm(name, plan):
    n_in, n_out = len(plan.ins), len(plan.out_shape)

    def body(*refs):
        for stage in plan.stages(refs[:n_in], refs[n_in:n_in + n_out], refs[n_in + n_out:]):
            stage()

    any_spec = pl.BlockSpec(memory_space=pl.ANY)
    return pl.pallas_call(
        body, name=name, in_specs=[any_spec] * len(plan.ins), out_specs=[any_spec] * len(plan.out_shape),
        out_shape=plan.out_shape, scratch_shapes=plan.scratch)(*plan.ins)


def gather_plan(shards):
    n = len(shards)

    def stages(srcs, outs, sems):
        send_sems, recv_sems, local_sems = sems

        def places():
            me = _mesh_pos()
            return me, _flip(me, 1), [_flip(me, 2), _flip(me, 4), _flip(me, 6)]

        def copy(a, k, block, to, src=None):
            dst = outs[a].at[_slot(block)]
            return pltpu.make_async_remote_copy(
                src_ref=dst if src is None else src, dst_ref=dst,
                send_sem=send_sems.at[7 * a + k], recv_sem=recv_sems.at[7 * a + k],
                device_id=to, device_id_type=pl.DeviceIdType.MESH)

        def local(a, me):
            return pltpu.make_async_copy(srcs[a], outs[a].at[_slot(me)], local_sems.at[a])

        def own(a, me, sibling, chips):
            return [copy(a, 0, me, sibling, src=srcs[a])] + [
                copy(a, 1 + j, me, chip, src=srcs[a]) for j, chip in enumerate(chips)]

        def start():
            me, sibling, chips = places()
            for a in range(n):
                local(a, me).start()
                for cp in own(a, me, sibling, chips):
                    cp.start()

        def forward():
            me, sibling, chips = places()
            for j, chip in enumerate(chips):
                for a in range(n):
                    copy(a, 1 + j, chip, me).wait_recv()
                    copy(a, 4 + j, chip, sibling).start()

        def finish():
            me, sibling, chips = places()
            for a in range(n):
                copy(a, 0, sibling, me).wait_recv()
                for j, chip in enumerate(chips):
                    copy(a, 4 + j, _flip(chip, 1), me).wait_recv()
            for a in range(n):
                for cp in own(a, me, sibling, chips):
                    cp.wait_send()
                for j, chip in enumerate(chips):
                    copy(a, 4 + j, chip, sibling).wait_send()
                local(a, me).wait()

        return start, forward, finish

    return CommPlan(list(shards), [_sds((N_DEV,) + s.shape, s.dtype) for s in shards],
                    [pltpu.SemaphoreType.DMA((7 * n,)), pltpu.SemaphoreType.DMA((7 * n,)),
                     pltpu.SemaphoreType.DMA((n,))], stages)


def exchange_plan(parts, replicated, rels, members, index, member_axis, own_copy):
    n, nr = len(parts), len(rels)
    pick_index = (slice(None),) * member_axis + (0,)
    subs = [1 if (r or member_axis == 0) else p.shape[0] for p, r in zip(parts, replicated)]
    first = [sum(subs[:a]) for a in range(n)]
    total = sum(subs)

    def stages(srcs, outs, sems):
        send_sems, recv_sems, local_sems = sems

        def src(a, s, pos):
            if replicated[a]:
                return srcs[a]
            return srcs[a].at[index(pos)] if member_axis == 0 else srcs[a].at[s, index(pos)]

        def dst(a, s, pos):
            block = outs[a].at[index(pos)]
            return block if (replicated[a] or member_axis == 0) else block.at[s]

        def copy(a, s, j, me, src_pos, dst_pos):
            sem = nr * (first[a] + s) + j
            return pltpu.make_async_remote_copy(
                src_ref=src(a, s, src_pos), dst_ref=dst(a, s, dst_pos),
                send_sem=send_sems.at[sem], recv_sem=recv_sems.at[sem],
                device_id=_flip(me, rels[j]), device_id_type=pl.DeviceIdType.MESH)

        pieces = [(a, s) for a in range(n) for s in range(subs[a])]

        def local(a, s, me):
            return pltpu.make_async_copy(src(a, s, me), dst(a, s, me), local_sems.at[first[a] + s])

        def sends(me):
            return [copy(a, s, j, me, _flip(me, rels[j]), me) for j in range(nr) for a, s in pieces]

        own = pieces if own_copy else []

        def start():
            me = _mesh_pos()
            for cp in sends(me) + [local(a, s, me) for a, s in own]:
                cp.start()

        def middle():
            pass

        def finish():
            me = _mesh_pos()
            for j in range(nr):
                for a, s in pieces:
                    copy(a, s, j, me, me, _flip(me, rels[j])).wait_recv()
            for cp in sends(me):
                cp.wait_send()
            for a, s in own:
                local(a, s, me).wait()

        return start, middle, finish

    shapes = [p.shape if r else jax.eval_shape(lambda t: t[pick_index], p).shape for p, r in zip(parts, replicated)]
    return CommPlan(list(parts), [_sds((members,) + s, p.dtype) for s, p in zip(shapes, parts)],
                    [pltpu.SemaphoreType.DMA((nr * total,)), pltpu.SemaphoreType.DMA((nr * total,)),
                     pltpu.SemaphoreType.DMA((total,))], stages)


def pair_plan(parts, replicated):
    return exchange_plan(parts, replicated, [1], 2, lambda pos: pos[2], 1, False)


def chip_plan(parts, replicated):
    return exchange_plan(parts, replicated, [2, 4, 6], 4, lambda pos: 2 * pos[0] + pos[1], 0, True)


def add_pair(name, mine, swapped, out_dtype, rows):
    def body(m_ref, s_ref, o_ref):
        own = m_ref[0, 0] if mine.ndim == 4 else m_ref[0]
        o_ref[0] = (own.astype(F32) + s_ref[0, 0].astype(F32)).astype(o_ref.dtype)

    _, n, r, c = swapped.shape
    core = lambda: lax.axis_index("c")
    if mine.ndim == 4:
        mine_spec = pl.BlockSpec((1, 1, rows, c), lambda i, j: (i, core(), j, 0))
    else:
        mine_spec = pl.BlockSpec((1, rows, c), lambda i, j: (i, j, 0))
    return pl.pallas_call(
        body, grid=(n, r // rows), name=name,
        in_specs=[mine_spec, pl.BlockSpec((1, 1, rows, c), lambda i, j: (1 - core(), i, j, 0))],
        out_specs=pl.BlockSpec((1, rows, c), lambda i, j: (i, j, 0)),
        out_shape=_sds((n, r, c), out_dtype),
        compiler_params=_params(("arbitrary", "arbitrary")),
    )(mine, swapped)


def add_pair_small(mines, swappeds):
    n = len(mines)
    halves = [m.ndim == s.ndim for m, s in zip(mines, swappeds)]

    def body(*refs):
        c = lax.axis_index("c")
        for i in range(n):
            m_ref, s_ref, o_ref = refs[i], refs[n + i], refs[2 * n + i]
            o_ref[...] = (m_ref[:, c] if halves[i] else m_ref[...]) + s_ref[1 - c]

    vmem = pl.BlockSpec(memory_space=pltpu.VMEM)
    return pl.pallas_call(
        body, name="pair_add_small", in_specs=[vmem] * (2 * n), out_specs=[vmem] * n,
        out_shape=[_sds(s.shape[1:]) for s in swappeds], compiler_params=_params(),
    )(*mines, *swappeds)


def _adamw_math(w, g, m, v):
    nm = ADAM_B1 * m + (1.0 - ADAM_B1) * g
    nv = ADAM_B2 * v + (1.0 - ADAM_B2) * (g * g)
    m_hat = nm / (1.0 - ADAM_B1 ** ADAM_STEP)
    v_hat = nv / (1.0 - ADAM_B2 ** ADAM_STEP)
    return -ADAM_LR * (m_hat / (jnp.sqrt(v_hat) + ADAM_EPS) + ADAM_WD * w), nm, nv


def adamw_small(ws, parts, ms, vs):
    n = len(ws)

    def body(*refs):
        for i in range(n):
            w_ref, p_ref, m_ref, v_ref = (refs[k * n + i] for k in range(4))
            g = p_ref[0]
            for j in range(1, p_ref.shape[0]):
                g = g + p_ref[j]
            delta, nm, nv = _adamw_math(w_ref[...], g, m_ref[...], v_ref[...])
            for k, val in enumerate((g, delta, nm, nv)):
                refs[(4 + k) * n + i][...] = val

    vmem = pl.BlockSpec(memory_space=pltpu.VMEM)
    outs = pl.pallas_call(
        body, name="adamw_small", in_specs=[vmem] * (4 * n), out_specs=[vmem] * (4 * n),
        out_shape=[_sds(w.shape) for w in ws] * 4, compiler_params=_params(),
    )(*ws, *parts, *ms, *vs)
    return [outs[k * n:(k + 1) * n] for k in range(4)]

def adamw(name, w, parts, m, v, rows, plan=None):
    n_parts = parts.shape[0]

    def body(w_ref, p_ref, m_ref, v_ref, g_ref, d_ref, nm_ref, nv_ref):
        g = p_ref[0].astype(F32)
        for j in range(1, n_parts):
            g = g + p_ref[j].astype(F32)
        g_ref[...] = g
        d_ref[...], nm_ref[...], nv_ref[...] = _adamw_math(w_ref[...], g, m_ref[...], v_ref[...])

    cols = w.shape[1]
    spec = pl.BlockSpec((rows, cols), lambda i: (i, 0))
    grid = (w.shape[0] // rows,)
    in_specs = [spec, pl.BlockSpec((n_parts, rows, cols), lambda i: (0, i, 0)), spec, spec]
    if plan is not None:
        return call_with_comm(plan, 0, body, grid, name, in_specs, [spec] * 4, [_sds(w.shape)] * 4, [],
                              (w, parts, m, v))
    return pl.pallas_call(
        body, grid=grid, name=name, in_specs=in_specs, out_specs=[spec] * 4, out_shape=[_sds(w.shape)] * 4,
        compiler_params=_params(("arbitrary",)),
    )(w, parts, m, v)


def _to_slots(full, per):
    return full.reshape(full.shape[0], N_DEV, per).transpose(1, 0, 2)


def _from_slots(slots):
    return slots.transpose(1, 0, 2).reshape(slots.shape[1], -1)


def kernel(x, norm_mix_pre, norm_mix_post, norm_ffn_pre, norm_ffn_post, w_in, rel_bias, sinks, rwkv_shift_mix, w0, w_decay_up, a0, w_iclr_up, w_gate_up, k_k, k_a, r_k, ln_x_g, ln_x_b, w_out, w_ffn_up, conv_w, conv_b, w_ffn_down, loss_target, m_norm_mix_pre, m_norm_mix_post, m_norm_ffn_pre, m_norm_ffn_post, m_w_in, m_rel_bias, m_sinks, m_rwkv_shift_mix, m_w0, m_w_decay_up, m_a0, m_w_iclr_up, m_w_gate_up, m_k_k, m_k_a, m_r_k, m_ln_x_g, m_ln_x_b, m_w_out, m_w_ffn_up, m_conv_w, m_conv_b, m_w_ffn_down, v_norm_mix_pre, v_norm_mix_post, v_norm_ffn_pre, v_norm_ffn_post, v_w_in, v_rel_bias, v_sinks, v_rwkv_shift_mix, v_w0, v_w_decay_up, v_a0, v_w_iclr_up, v_w_gate_up, v_k_k, v_k_a, v_r_k, v_ln_x_g, v_ln_x_b, v_w_out, v_w_ffn_up, v_conv_w, v_conv_b, v_w_ffn_down):
    x2 = x[0]
    target = loss_target[0]

    (g_in,) = run_comm("all_gather_w_in", gather_plan([cast_bf16(w_in[0], 256)]))
    mixer_gather = gather_plan([cast_bf16(w_out[0], 128), w_decay_up[0], w_iclr_up[0], w_gate_up[0], conv_w[0]])
    up_gather = gather_plan([cast_bf16(w_ffn_up[0], 256)])
    down_gather = gather_plan([cast_bf16(w_ffn_down[0], 256)])
    w_in_b = _from_slots(g_in)
    mix_ext = jnp.concatenate([jnp.zeros((1, D_QKV), F32), rwkv_shift_mix], axis=1)
    r_k_row = r_k.reshape(1, D_RWKV)
    bucket = _bucket_table()

    (h1,) = tok_fwd("rms_mix_pre", rms_tile, [x2], [norm_mix_pre], [], [D_MODEL], [BF16])
    (proj, ps), (g_out, g_decay, g_iclr, g_gate, g_conv) = in_proj_fwd(h1, w_in_b, mix_ext, mixer_gather)
    w_out_b = g_out.reshape(D_MODEL, D_MODEL)
    lora = jnp.zeros((HEAD_DIM, D_RWKV), F32)
    wd_pad = jnp.concatenate([_from_slots(g_decay), lora], axis=0)
    wi_pad = jnp.concatenate([lora, _from_slots(g_iclr)], axis=0)
    wg_full = _from_slots(g_gate)
    attn, (g_down,) = attn_fwd(proj, rel_bias, bucket, sinks, down_gather)
    pre_params = [w0, wd_pad, a0, wi_pad, wg_full, k_k, k_a]
    r_, lw_, k2_, v_, kk_, a_, gate_ = tok_fwd("rwkv_pre", rwkv_pre_tile, [ps], pre_params, [],
                                               [D_RWKV] * 7, [F32] * 7)
    (o_, states), (g_up,) = rwkv_scan_fwd(r_, lw_, k2_, v_, kk_, a_, up_gather)
    w_down_b = g_down.reshape(D_FF, D_MODEL)
    mix_tiles = [o_, r_, k2_, v_, gate_, attn, x2]
    mix_params = [w_out_b, norm_mix_post, ln_x_g, ln_x_b, r_k_row, norm_ffn_pre]
    x1, h2 = tok_fwd("mix_out", mix_out_tile, mix_tiles, mix_params, [(D_MODEL, D_MODEL)], [D_MODEL, D_MODEL],
                     [F32, BF16])
    f = ffn_fwd(h2, g_up, g_conv, conv_b, w_down_b)
    dy, df, d_n_ffn_post, loss_row = loss_head(x1, f, target, norm_ffn_post)

    d_ug, d_uv, d_cw_g, d_cw_v, d_cb_g, d_cb_v, d_down = ffn_bwd_mid(h2, g_up, g_conv, conv_b, w_down_b, df)
    dh2, d_up = ffn_bwd_up(h2, g_up, d_ug, d_uv)
    half = N_DEV // 2
    d_cw = jnp.concatenate([d_cw_g[:half], d_cw_v[half:]], axis=0)
    by_pair = lambda slots: slots.reshape((N_DEV // 2, 2) + slots.shape[1:])
    ffn_mine = [by_pair(d_up), by_pair(d_down.reshape(N_DEV, D_FF // N_DEV, D_MODEL))]
    ffn_swapped = run_comm("pair_exchange_ffn", pair_plan(ffn_mine, [False, False]))
    up_exchange = chip_plan([add_pair("pair_add_w_ffn_up", ffn_mine[0], ffn_swapped[0], BF16, 256)], [False])
    down_exchange = chip_plan([add_pair("pair_add_w_ffn_down", ffn_mine[1], ffn_swapped[1], BF16, 256)], [False])
    d_cb = jnp.concatenate([d_cb_g[:, :D_FF], d_cb_v[:, D_FF:]], axis=1)
    ((d_o, d_r1, d_k1, d_v1, d_gate, d_attn, dx_res, d_n_mix_post, d_ln_g, d_ln_b, d_r_k, d_n_ffn_pre, d_w_out),
     (got_down,)) = tok_bwd("mix_out_bwd", mix_out_tile, mix_tiles, mix_params, [(D_MODEL, D_MODEL)], [dy, dh2],
                            [1, 2, 3, 4, 5], plan=down_exchange)
    (d_r2, d_lw, d_k2, d_v2, d_kk, d_a), (got_up,) = rwkv_scan_bwd(
        r_, lw_, k2_, v_, kk_, a_, states, d_o, up_exchange)
    pre_cots = [(d_r1, d_r2), d_lw, (d_k1, d_k2), (d_v1, d_v2), d_kk, d_a, d_gate]
    (d_ps, d_w0, d_wd_pad, d_a0, d_wi_pad, d_wg, d_k_k, d_k_a) = tok_bwd(
        "rwkv_pre_bwd", rwkv_pre_tile, [ps], pre_params, [], pre_cots, [0, 1, 2, 3, 4, 5, 6])
    dq, dkc, dkp, dvc, dvp, d_rel_bias, d_sinks = attn_bwd(proj, rel_bias, bucket, sinks, d_attn)
    zero_blk = jnp.zeros((BLOCK, D_KV), F32)
    dk = dkc + jnp.concatenate([dkp[BLOCK:], zero_blk], axis=0)
    dv = dvc + jnp.concatenate([dvp[BLOCK:], zero_blk], axis=0)
    dpa = jnp.concatenate([dq, dk, dv, d_ps], axis=1)
    dh1, d_w_in, d_mix_ext = in_proj_bwd(h1, w_in_b, mix_ext, proj, dpa)
    grad_x2, d_n_mix_pre = tok_bwd("rms_mix_pre_bwd", rms_tile, [x2], [norm_mix_pre], [], [dh1], [0], {0: dx_res})
    grad_x = grad_x2[None]

    small_rep = [d_n_mix_pre, d_n_mix_post, d_n_ffn_pre, d_n_ffn_post, d_rel_bias, d_sinks,
                 d_mix_ext[:, D_QKV:], d_w0, d_a0, d_k_k, d_k_a, d_r_k.reshape(r_k.shape), d_ln_g, d_ln_b, d_cb]
    rep_w = [norm_mix_pre, norm_mix_post, norm_ffn_pre, norm_ffn_post, rel_bias, sinks, rwkv_shift_mix,
             w0, a0, k_k, k_a, r_k, ln_x_g, ln_x_b, conv_b]
    rep_m = [m_norm_mix_pre, m_norm_mix_post, m_norm_ffn_pre, m_norm_ffn_post, m_rel_bias, m_sinks,
             m_rwkv_shift_mix, m_w0, m_a0, m_k_k, m_k_a, m_r_k, m_ln_x_g, m_ln_x_b, m_conv_b]
    rep_v = [v_norm_mix_pre, v_norm_mix_post, v_norm_ffn_pre, v_norm_ffn_post, v_rel_bias, v_sinks,
             v_rwkv_shift_mix, v_w0, v_a0, v_k_k, v_k_a, v_r_k, v_ln_x_g, v_ln_x_b, v_conv_b]
    sh_w = [w_decay_up, w_iclr_up, w_gate_up, conv_w]
    sh_m = [m_w_decay_up, m_w_iclr_up, m_w_gate_up, m_conv_w]
    sh_v = [v_w_decay_up, v_w_iclr_up, v_w_gate_up, v_conv_w]
    sh_parts = [_to_slots(d_wd_pad[:HEAD_DIM], HEAD_DIM), _to_slots(d_wi_pad[HEAD_DIM:], HEAD_DIM),
                _to_slots(d_wg, HEAD_DIM), d_cw]
    n_rep, n_sh = len(small_rep), len(sh_parts)
    mine = [by_pair(_to_slots(d_w_in, D_IN // N_DEV)), by_pair(d_w_out.reshape(N_DEV, D_MODEL // N_DEV, D_MODEL)),
            *small_rep, *(by_pair(p) for p in sh_parts), loss_row]
    is_rep = [False, False] + [True] * n_rep + [False] * n_sh + [True]
    adam_down, swapped = adamw("adamw_w_ffn_down", w_ffn_down[0], got_down, m_w_ffn_down[0], v_w_ffn_down[0], 128,
                               pair_plan(mine, is_rep))
    chip_sums = [add_pair("pair_add_w_in", mine[0], swapped[0], BF16, 512),
                 add_pair("pair_add_w_out", mine[1], swapped[1], BF16, 128),
                 *add_pair_small(mine[2:], swapped[2:])]
    adam_up, got = adamw("adamw_w_ffn_up", w_ffn_up[0], got_up, m_w_ffn_up[0], v_w_ffn_up[0], 128,
                         chip_plan(chip_sums, is_rep))

    big = [adamw("adamw_w_in", w_in[0], got[0], m_w_in[0], v_w_in[0], 256),
           adamw("adamw_w_out", w_out[0], got[1], m_w_out[0], v_w_out[0], 128), adam_up, adam_down]
    loss = functools.reduce(jnp.add, [got[-1][q, 0, 0] for q in range(N_DEV // 2)])
    small_w, small_g = rep_w + sh_w, got[2:-1]
    as_grad = lambda arrays: [a.reshape(g.shape[1:]) for a, g in zip(arrays, small_g)]
    small = adamw_small(as_grad(small_w), small_g, as_grad(rep_m + sh_m), as_grad(rep_v + sh_v))
    small = [[a.reshape(w.shape) for a, w in zip(kind, small_w)] for kind in small]

    names = ["norm_mix_pre", "norm_mix_post", "norm_ffn_pre", "norm_ffn_post", "w_in", "rel_bias", "sinks",
             "rwkv_shift_mix", "w0", "w_decay_up", "a0", "w_iclr_up", "w_gate_up", "k_k", "k_a", "r_k",
             "ln_x_g", "ln_x_b", "w_out", "w_ffn_up", "conv_w", "conv_b", "w_ffn_down"]
    small_names = ["norm_mix_pre", "norm_mix_post", "norm_ffn_pre", "norm_ffn_post", "rel_bias", "sinks",
                   "rwkv_shift_mix", "w0", "a0", "k_k", "k_a", "r_k", "ln_x_g", "ln_x_b", "conv_b",
                   "w_decay_up", "w_iclr_up", "w_gate_up", "conv_w"]
    big_names = {"w_in": 0, "w_out": 1, "w_ffn_up": 2, "w_ffn_down": 3}
    outs = []
    for kind in range(4):
        for nm in names:
            if nm in big_names:
                outs.append(big[big_names[nm]][kind][None])
            else:
                outs.append(small[kind][small_names.index(nm)])
    return (loss, grad_x, *outs)
```

```python
import functools
import math

import jax
import jax.numpy as jnp
from jax import lax
from jax.experimental import pallas as pl
from jax.experimental.pallas import tpu as pltpu

F32 = jnp.float32
BF16 = jnp.bfloat16

N_DEV = 8
SEQ = 2048
D_MODEL = 1024
HEAD_DIM = 64
D_ATTN = 512
D_KV = 128
D_RWKV = 512
N_HEADS = 8
RWKV_COLS = 1792
D_QKV = D_ATTN + 2 * D_KV
D_IN = D_QKV + RWKV_COLS
D_FF = 4096
BLOCK = 128
N_BLOCKS = SEQ // BLOCK
N_BUCKETS = 32
MAX_DISTANCE = 128
NORM_EPS = 1e-6
GN_EPS = 64e-5
NEG_INF = -1e30
CHUNK = 64
N_CHUNKS = SEQ // CHUNK
SCAN_GROUPS = 4
SCAN_WIDTH = D_RWKV // SCAN_GROUPS
TOK_TILE = 256
FF_TILE = 256
FF_ROW_CHUNK = 256
FF_HALO = 8
COL_TILE = 256
LANES = 128
VMEM_LIMIT = 56 * 1024 * 1024

ADAM_LR = 0.001
ADAM_B1 = 0.9
ADAM_B2 = 0.999
ADAM_EPS = 1e-08
ADAM_WD = 0.01
ADAM_STEP = 10

NT = ((1,), (1,))
TN = ((0,), (0,))
NN = ((1,), (0,))


def _sds(shape, dtype=F32):
    return jax.ShapeDtypeStruct(shape, dtype)


def _params(sem=None):
    if sem is None:
        return pltpu.CompilerParams(vmem_limit_bytes=VMEM_LIMIT)
    return pltpu.CompilerParams(dimension_semantics=sem, vmem_limit_bytes=VMEM_LIMIT)


def _dot(a, b, dims):
    return lax.dot_general(a, b, (dims, ((), ())), preferred_element_type=F32)


def _split2(x):
    hi = x.astype(BF16)
    return hi, (x - hi.astype(F32)).astype(BF16)


def _dot3_raw(a, b, dims):
    ah, al = _split2(a)
    bh, bl = _split2(b)
    return _dot(ah, bh, dims) + (_dot(al, bh, dims) + _dot(ah, bl, dims))


@functools.partial(jax.custom_vjp, nondiff_argnums=(2,))
def dot3(a, b, dims):
    return _dot3_raw(a, b, dims)


def _dot3_fwd(a, b, dims):
    return _dot3_raw(a, b, dims), (a, b)


def _dot3_bwd(dims, res, g):
    a, b = res
    if dims == NN:
        return dot3(g, b, NT), dot3(a, g, TN)
    if dims == NT:
        return dot3(g, b, NN), dot3(g, a, TN)
    return dot3(b, g, NT), dot3(a, g, NN)


dot3.defvjp(_dot3_fwd, _dot3_bwd)


@functools.partial(jax.custom_vjp, nondiff_argnums=(2,))
def dot1(a, b, dims):
    return _dot(a.astype(BF16), b.astype(BF16), dims)


def _dot1_fwd(a, b, dims):
    return dot1(a, b, dims), (a, b)


def _dot1_bwd(dims, res, g):
    a, b = res
    if dims == NN:
        return dot1(g, b, NT), dot1(a, g, TN)
    if dims == NT:
        return dot1(g, b, NN), dot1(g, a, TN)
    return dot1(b, g, NT), dot1(a, g, NN)


dot1.defvjp(_dot1_fwd, _dot1_bwd)


@jax.custom_vjp
def mm(a, b):
    return _dot(a.astype(BF16), b.astype(BF16), NN)


def _mm_fwd(a, b):
    return mm(a, b), (a, b)


def _mm_bwd(res, g):
    a, b = res
    gb = g.astype(BF16)
    return _dot(gb, b.astype(BF16), NT).astype(a.dtype), _dot(a.astype(BF16), gb, TN).astype(b.dtype)


mm.defvjp(_mm_fwd, _mm_bwd)


@jax.custom_vjp
def mm_nt(a, b):
    return _dot(a.astype(BF16), b.astype(BF16), NT)


def _mm_nt_fwd(a, b):
    return mm_nt(a, b), (a, b)


def _mm_nt_bwd(res, g):
    a, b = res
    gb = g.astype(BF16)
    return _dot(gb, b.astype(BF16), NN).astype(a.dtype), _dot(gb, a.astype(BF16), TN).astype(b.dtype)


mm_nt.defvjp(_mm_nt_fwd, _mm_nt_bwd)


@jax.custom_vjp
def mmw(a, w, wz):
    return _dot(a.astype(BF16), w, NN)


def _mmw_fwd(a, w, wz):
    return mmw(a, w, wz), (a, w)


def _mmw_bwd(res, g):
    a, w = res
    gb = g.astype(BF16)
    return _dot(gb, w, NT).astype(a.dtype), jnp.zeros_like(w), _dot(a.astype(BF16), gb, TN)


mmw.defvjp(_mmw_fwd, _mmw_bwd)


def _shift_raw(x, n):
    rows = x.shape[0]
    rolled = pltpu.roll(x, n % rows, 0)
    idx = lax.broadcasted_iota(jnp.int32, x.shape, 0)
    keep = idx >= n if n > 0 else idx < rows + n
    return jnp.where(keep, rolled, 0.0)


@functools.partial(jax.custom_vjp, nondiff_argnums=(1,))
def shift_rows(x, n):
    return _shift_raw(x, n)


def _shift_fwd(x, n):
    return _shift_raw(x, n), None


def _shift_bwd(n, _, g):
    return (_shift_raw(g, -n),)


shift_rows.defvjp(_shift_fwd, _shift_bwd)


def _head_sum(x, scale):
    a = lax.broadcasted_iota(jnp.int32, (LANES, LANES), 0) // HEAD_DIM
    b = lax.broadcasted_iota(jnp.int32, (LANES, LANES), 1) // HEAD_DIM
    pair = jnp.where(a == b, scale, 0.0).astype(F32)
    return jnp.concatenate([dot3(x[:, i:i + LANES], pair, NN) for i in range(0, x.shape[1], LANES)], axis=1)


def _rms(x, g):
    return x * lax.rsqrt(jnp.mean(x * x, axis=-1, keepdims=True) + NORM_EPS) * g


def _softplus(x):
    return jnp.maximum(x, 0.0) + jnp.log(1.0 + jnp.exp(-jnp.abs(x)))


def _tile_spec(arr, tm):
    return pl.BlockSpec((tm, arr.shape[1]), lambda i: (i, 0))


def _full_spec(arr):
    nd = arr.ndim
    return pl.BlockSpec(arr.shape, lambda i: (0,) * nd)


def tok_fwd(name, fn, tiles, params, zero_shapes, out_widths, out_dtypes, plan=None, tm=TOK_TILE):
    n_t, n_p = len(tiles), len(params)

    def body(*refs):
        t_vals = [r[...] for r in refs[:n_t]]
        p_vals = [r[...] for r in refs[n_t:n_t + n_p]]
        z_vals = [jnp.zeros(s, F32) for s in zero_shapes]
        outs = fn(*t_vals, *p_vals, *z_vals)
        for r, o in zip(refs[n_t + n_p:], outs):
            r[...] = o.astype(r.dtype)

    rows = tiles[0].shape[0]
    steps = rows // tm
    in_specs = [_tile_spec(t, tm) for t in tiles] + [_full_spec(p) for p in params]
    out_specs = [pl.BlockSpec((tm, w), lambda i: (i, 0)) for w in out_widths]
    out_shape = [_sds((rows, w), dt) for w, dt in zip(out_widths, out_dtypes)]
    if plan is not None:
        return call_with_comm(plan, steps // 2, body, (steps,), name, in_specs, out_specs, out_shape, [],
                              (*tiles, *params))
    return pl.pallas_call(
        body, grid=(steps,), name=name, in_specs=in_specs, out_specs=out_specs, out_shape=out_shape,
        compiler_params=_params(("arbitrary",)),
    )(*tiles, *params)


def tok_bwd(name, fn, tiles, params, zero_shapes, cots, diff_params, residuals=(), plan=None, tm=TOK_TILE):
    cot_parts = [c if isinstance(c, tuple) else (c,) for c in cots]
    flat_cots = [a for part in cot_parts for a in part]
    residuals = dict(residuals)
    extra = [residuals[i] for i in sorted(residuals)]
    n_t, n_p, n_c, n_r = len(tiles), len(params), len(flat_cots), len(extra)
    acc_shapes = [params[i].shape for i in diff_params] + list(zero_shapes)

    def body(*refs):
        t_vals = [r[...].astype(F32) for r in refs[:n_t]]
        p_vals = [r[...] for r in refs[n_t:n_t + n_p]]
        flat = iter(r[...] for r in refs[n_t + n_p:n_t + n_p + n_c])
        c_vals = [functools.reduce(jnp.add, [next(flat) for _ in part]) for part in cot_parts]
        r_vals = dict(zip(sorted(residuals), (r[...] for r in refs[n_t + n_p + n_c:n_t + n_p + n_c + n_r])))
        out_refs = refs[n_t + n_p + n_c + n_r:]
        z_vals = [jnp.zeros(s, F32) for s in zero_shapes]
        d_vals = [p_vals[i] for i in diff_params]

        def f(t_in, d_in, z_in):
            full = list(p_vals)
            for i, v in zip(diff_params, d_in):
                full[i] = v
            return tuple(fn(*t_in, *full, *z_in))

        _, vjp = jax.vjp(f, t_vals, d_vals, z_vals)
        g_t, g_d, g_z = vjp(tuple(c_vals))
        for i, (r, g) in enumerate(zip(out_refs[:n_t], g_t)):
            r[...] = (g + r_vals[i] if i in r_vals else g).astype(r.dtype)
        acc_refs = out_refs[n_t:]

        @pl.when(pl.program_id(0) == 0)
        def _():
            for r in acc_refs:
                r[...] = jnp.zeros_like(r)

        for r, g in zip(acc_refs, list(g_d) + list(g_z)):
            r[...] += g

    rows = tiles[0].shape[0]
    in_specs = ([_tile_spec(t, tm) for t in tiles] + [_full_spec(p) for p in params]
                + [_tile_spec(c, tm) for c in flat_cots + extra])
    out_specs = ([_tile_spec(t, tm) for t in tiles]
                 + [pl.BlockSpec(s, lambda i, nd=len(s): (0,) * nd) for s in acc_shapes])
    out_shape = [_sds(t.shape) for t in tiles] + [_sds(s) for s in acc_shapes]
    operands = (*tiles, *params, *flat_cots, *extra)
    if plan is not None:
        return call_with_comm(plan, 0, body, (rows // tm,), name, in_specs, out_specs, out_shape, [], operands)
    return pl.pallas_call(
        body, grid=(rows // tm,), name=name, in_specs=in_specs, out_specs=out_specs, out_shape=out_shape,
        compiler_params=_params(("arbitrary",)),
    )(*operands)


def rms_tile(x, g):
    return (_rms(x, g),)


def rwkv_pre_tile(ps, w0, wd_pad, a0, wi_pad, wg, k_k, k_a):
    r = ps[:, 0:D_RWKV]
    k = ps[:, D_RWKV:2 * D_RWKV]
    v = ps[:, 2 * D_RWKV:3 * D_RWKV]
    z2 = ps[:, 3 * D_RWKV:3 * D_RWKV + LANES]
    zg = ps[:, 3 * D_RWKV + LANES:RWKV_COLS]
    w_log = -_softplus(-(w0 + mm(jnp.tanh(z2), wd_pad))) - 0.5
    lw = -jnp.exp(w_log)
    a = jax.nn.sigmoid(a0 + mm(z2, wi_pad))
    g = mm(jax.nn.sigmoid(zg), wg)
    kk = k * k_k
    norm = jnp.sqrt(_head_sum(kk * kk, 1.0))
    kk = kk / jnp.maximum(norm, 1e-12)
    k2 = k * (1.0 + (a - 1.0) * k_a)
    return r, lw, k2, v, kk, a, g


def mix_out_tile(o, r, k2, v, g, attn, x, w_out, n_post, ln_g, ln_b, r_k, n_ffn_pre, wz):
    d = o - _head_sum(o, 1.0 / HEAD_DIM)
    var = _head_sum(d * d, 1.0 / HEAD_DIM)
    on = d * lax.rsqrt(var + GN_EPS) * ln_g + ln_b
    bonus = _head_sum(r * k2 * r_k, 1.0) * v
    rw = (on + bonus) * g
    mix = mmw(jnp.concatenate([attn, rw], axis=1), w_out, wz)
    x1 = x + _rms(mix, n_post)
    return x1, _rms(x1, n_ffn_pre)


def in_proj_fwd(h, w_in, mix_ext, plan):
    def body(h_ref, w_ref, m_ref, proj_ref, ps_ref):
        p = _dot(h_ref[...], w_ref[...], NN)
        proj_ref[...] = p
        ps_ref[...] = p + (_shift_raw(p, 1) - p) * m_ref[...]

    n = D_IN // COL_TILE
    first = D_QKV // COL_TILE
    return call_with_comm(
        plan, n // 2, body, (n,), "in_proj_fwd",
        [pl.BlockSpec((SEQ, D_MODEL), lambda j: (0, 0)), pl.BlockSpec((D_MODEL, COL_TILE), lambda j: (0, j)),
         pl.BlockSpec((1, COL_TILE), lambda j: (0, j))],
        [pl.BlockSpec((SEQ, COL_TILE), lambda j: (0, j)),
         pl.BlockSpec((SEQ, COL_TILE), lambda j: (0, jnp.maximum(j - first, 0)))],
        [_sds((SEQ, D_IN)), _sds((SEQ, RWKV_COLS))], [], (h, w_in, mix_ext))


def in_proj_bwd(h, w_in, mix_ext, proj, dpa):
    def body(h_ref, w_ref, m_ref, p_ref, d_ref, dh_ref, dw_ref, dm_ref):
        d = d_ref[...]
        p = p_ref[...]
        dm_ref[...] = jnp.sum(d * (_shift_raw(p, 1) - p), axis=0, keepdims=True)
        dmix = d * m_ref[...]
        dp = (d - dmix + _shift_raw(dmix, -1)).astype(BF16)
        dw_ref[...] = _dot(h_ref[...], dp, TN)

        @pl.when(pl.program_id(0) == 0)
        def _():
            dh_ref[...] = jnp.zeros_like(dh_ref)

        dh_ref[...] += _dot(dp, w_ref[...], NT)

    n = D_IN // COL_TILE
    col = lambda rows: pl.BlockSpec((rows, COL_TILE), lambda j: (0, j))
    return pl.pallas_call(
        body, grid=(n,), name="in_proj_bwd",
        in_specs=[pl.BlockSpec((SEQ, D_MODEL), lambda j: (0, 0)), col(D_MODEL), col(1), col(SEQ), col(SEQ)],
        out_specs=[pl.BlockSpec((SEQ, D_MODEL), lambda j: (0, 0)), col(D_MODEL), col(1)],
        out_shape=[_sds((SEQ, D_MODEL)), _sds((D_MODEL, D_IN)), _sds((1, D_IN))],
        compiler_params=_params(("arbitrary",)),
    )(h, w_in, mix_ext, proj, dpa)


def _bucket_table():
    rel = (jnp.arange(BLOCK)[:, None] + BLOCK) - jnp.arange(2 * BLOCK)[None, :]
    n = jnp.maximum(rel, 0)
    max_exact = N_BUCKETS // 2
    large = max_exact + (jnp.log(jnp.maximum(n, 1).astype(F32) / max_exact)
                         / math.log(MAX_DISTANCE / max_exact) * (N_BUCKETS - max_exact)).astype(jnp.int32)
    large = jnp.minimum(large, N_BUCKETS - 1)
    return jnp.where(n < max_exact, n, large).astype(jnp.int32)


def _select_matrix(g, o):
    a = lax.broadcasted_iota(jnp.int32, (D_KV, D_KV), 0)
    b = lax.broadcasted_iota(jnp.int32, (D_KV, D_KV), 1)
    return ((a - HEAD_DIM * g == b - o) & (b >= o) & (b < o + HEAD_DIM)).astype(F32)


def _attn_block(q, kp, kc, vp, vc, bias, sinks, block_idx):
    kb = jnp.concatenate([kp, kc], axis=0)
    vb = jnp.concatenate([vp, vc], axis=0)
    row = lax.broadcasted_iota(jnp.int32, (BLOCK, 2 * BLOCK), 0)
    col = lax.broadcasted_iota(jnp.int32, (BLOCK, 2 * BLOCK), 1)
    rel = row + BLOCK - col
    mask = (rel >= 0) & (rel < BLOCK) & (col + (block_idx - 1) * BLOCK >= 0)
    lane8 = lax.broadcasted_iota(jnp.int32, (1, N_HEADS), 1)
    kt, vt = {}, {}
    for g in range(2):
        for o in (0, HEAD_DIM):
            sel = _select_matrix(g, o)
            kt[g, o] = mm(kb, sel)
            vt[g, o] = mm(vb, sel)
    outs = []
    for j in range(D_ATTN // LANES):
        qs = q[:, j * LANES:(j + 1) * LANES]
        acc = None
        for half in range(2):
            hq = 2 * j + half
            g, o = hq // 4, half * HEAD_DIM
            s = mm_nt(qs, kt[g, o]) * (HEAD_DIM ** -0.5) + bias[hq]
            s = jnp.where(mask, s, NEG_INF)
            sink = jnp.sum(jnp.where(lane8 == hq, sinks, 0.0), axis=1, keepdims=True)
            m = lax.stop_gradient(jnp.maximum(jnp.max(s, axis=-1, keepdims=True), sink))
            p = jnp.exp(s - m)
            probs = p / (jnp.sum(p, axis=-1, keepdims=True) + jnp.exp(sink - m))
            part = mm(probs, vt[g, o])
            acc = part if acc is None else acc + part
        outs.append(acc)
    return jnp.concatenate(outs, axis=1)


def _build_bias(rb_ref, bucket, bias_ref):
    for hq in range(N_HEADS):
        acc = jnp.zeros((BLOCK, 2 * BLOCK), F32)
        for b in range(N_BUCKETS):
            acc = jnp.where(bucket == b, rb_ref[b, hq], acc)
        bias_ref[hq] = acc


def _attn_in_specs():
    prev = lambda n: jnp.maximum(n - 1, 0)
    return [pl.BlockSpec((BLOCK, D_ATTN), lambda n: (n, 0)),
            pl.BlockSpec((BLOCK, D_KV), lambda n: (prev(n), D_ATTN // D_KV)),
            pl.BlockSpec((BLOCK, D_KV), lambda n: (n, D_ATTN // D_KV)),
            pl.BlockSpec((BLOCK, D_KV), lambda n: (prev(n), D_ATTN // D_KV + 1)),
            pl.BlockSpec((BLOCK, D_KV), lambda n: (n, D_ATTN // D_KV + 1)),
            pl.BlockSpec(memory_space=pltpu.SMEM),
            pl.BlockSpec((BLOCK, 2 * BLOCK), lambda n: (0, 0)),
            pl.BlockSpec((1, N_HEADS), lambda n: (0, 0))]


def attn_fwd(proj, rel_bias, bucket, sinks, plan):
    def body(q_ref, kp_ref, kc_ref, vp_ref, vc_ref, rb_ref, bk_ref, sk_ref, o_ref, bias_ref):
        n = pl.program_id(0)

        @pl.when(n == 0)
        def _():
            _build_bias(rb_ref, bk_ref[...], bias_ref)

        o_ref[...] = _attn_block(q_ref[...], kp_ref[...], kc_ref[...], vp_ref[...], vc_ref[...],
                                 tuple(bias_ref[h] for h in range(N_HEADS)), sk_ref[...], n)

    (attn,), gathered = call_with_comm(
        plan, 3 * N_BLOCKS // 4, body, (N_BLOCKS,), "attn_fwd", _attn_in_specs(),
        [pl.BlockSpec((BLOCK, D_ATTN), lambda n: (n, 0))], [_sds((SEQ, D_ATTN))],
        [pltpu.VMEM((N_HEADS, BLOCK, 2 * BLOCK), F32)], (proj, proj, proj, proj, proj, rel_bias, bucket, sinks))
    return attn, gathered


def attn_bwd(proj, rel_bias, bucket, sinks, d_attn):
    def body(q_ref, kp_ref, kc_ref, vp_ref, vc_ref, rb_ref, bk_ref, sk_ref, do_ref,
             dq_ref, dkc_ref, dkp_ref, dvc_ref, dvp_ref, drb_ref, dsk_ref, bias_ref, dbias_ref):
        n = pl.program_id(0)

        @pl.when(n == 0)
        def _():
            _build_bias(rb_ref, bk_ref[...], bias_ref)
            dbias_ref[...] = jnp.zeros_like(dbias_ref)
            dsk_ref[...] = jnp.zeros_like(dsk_ref)

        f = lambda q, kp, kc, vp, vc, bias, sk: _attn_block(q, kp, kc, vp, vc, bias, sk, n)
        _, vjp = jax.vjp(f, q_ref[...], kp_ref[...], kc_ref[...], vp_ref[...], vc_ref[...],
                         tuple(bias_ref[h] for h in range(N_HEADS)), sk_ref[...])
        dq, dkp, dkc, dvp, dvc, dbias, dsk = vjp(do_ref[...])
        dq_ref[...] = dq
        dkc_ref[...] = dkc
        dkp_ref[...] = dkp
        dvc_ref[...] = dvc
        dvp_ref[...] = dvp
        for h in range(N_HEADS):
            dbias_ref[h] += dbias[h]
        dsk_ref[...] += dsk

        @pl.when(n == N_BLOCKS - 1)
        def _():
            bucket_v = bk_ref[...]
            rowi = lax.broadcasted_iota(jnp.int32, (N_BUCKETS, 2 * BLOCK), 0)
            lane = lax.broadcasted_iota(jnp.int32, (N_BUCKETS, N_HEADS), 1)
            out = jnp.zeros((N_BUCKETS, N_HEADS), F32)
            for hq in range(N_HEADS):
                dbh = dbias_ref[hq]
                rows = jnp.zeros((N_BUCKETS, 2 * BLOCK), F32)
                for b in range(N_BUCKETS):
                    part = jnp.sum(jnp.where(bucket_v == b, dbh, 0.0), axis=0, keepdims=True)
                    rows = jnp.where(rowi == b, part, rows)
                tot = jnp.sum(rows, axis=1, keepdims=True)
                out = jnp.where(lane == hq, tot, out)
            drb_ref[...] = out

    blk = lambda w: pl.BlockSpec((BLOCK, w), lambda n: (n, 0))
    return pl.pallas_call(
        body, grid=(N_BLOCKS,), name="attn_bwd",
        in_specs=_attn_in_specs() + [blk(D_ATTN)],
        out_specs=[blk(D_ATTN), blk(D_KV), blk(D_KV), blk(D_KV), blk(D_KV),
                   pl.BlockSpec((N_BUCKETS, N_HEADS), lambda n: (0, 0)),
                   pl.BlockSpec((1, N_HEADS), lambda n: (0, 0))],
        out_shape=[_sds((SEQ, D_ATTN)), _sds((SEQ, D_KV)), _sds((SEQ, D_KV)), _sds((SEQ, D_KV)),
                   _sds((SEQ, D_KV)), _sds((N_BUCKETS, N_HEADS)), _sds((1, N_HEADS))],
        scratch_shapes=[pltpu.VMEM((N_HEADS, BLOCK, 2 * BLOCK), F32),
                        pltpu.VMEM((N_HEADS, BLOCK, 2 * BLOCK), F32)],
        compiler_params=_params(("arbitrary",)),
    )(proj, proj, proj, proj, proj, rel_bias, bucket, sinks, d_attn)


def _stack(x, size):
    groups = x.shape[1] // size
    lane = lax.broadcasted_iota(jnp.int32, x.shape, 1) // size
    return jnp.concatenate([jnp.where(lane == i, x, 0.0) for i in range(groups)], axis=0)


def _neumann(l):
    c = CHUNK
    t = lax.broadcasted_iota(jnp.int32, l.shape, 0)
    i = lax.broadcasted_iota(jnp.int32, l.shape, 1) % c
    inv = (i == t).astype(F32) + l
    pw = dot1(l, _stack(l, c), NN)
    for _ in range(int(math.log2(c)) - 2):
        both = dot1(jnp.concatenate([inv, pw], axis=0), _stack(pw, c), NN)
        inv = inv + both[:c]
        pw = both[c:]
    return inv + dot1(inv, _stack(pw, c), NN)


@jax.custom_vjp
def neumann_inv(l):
    return _neumann(l)


def _neumann_fwd(l):
    inv = _neumann(l)
    return inv, inv


def _neumann_bwd(inv, g):
    c = CHUNK
    bd_t = _stack(inv, c).T
    inv_t = bd_t[0:c]
    for h in range(1, inv.shape[1] // c):
        inv_t = inv_t + bd_t[h * c:(h + 1) * c]
    return (dot1(dot1(inv_t, _stack(g, c), NN), bd_t, NN),)


neumann_inv.defvjp(_neumann_fwd, _neumann_bwd)


def _cumsum_raw(x, dims):
    c = x.shape[0]
    tt = lax.broadcasted_iota(jnp.int32, (c, c), 0)
    ii = lax.broadcasted_iota(jnp.int32, (c, c), 1)
    tri = (ii <= tt).astype(BF16)
    hi = x.astype(BF16)
    rest = x - hi.astype(F32)
    mid = rest.astype(BF16)
    lo = (rest - mid.astype(F32)).astype(BF16)
    return _dot(tri, hi, dims) + (_dot(tri, mid, dims) + _dot(tri, lo, dims))


@jax.custom_vjp
def cumsum_rows(x):
    return _cumsum_raw(x, NN)


def _cumsum_fwd(x):
    return _cumsum_raw(x, NN), None


def _cumsum_bwd(_, g):
    return (_cumsum_raw(g, TN),)


cumsum_rows.defvjp(_cumsum_fwd, _cumsum_bwd)


def _rwkv_chunk(s0, r, lw, k, v, kk, a):
    heads = r.shape[1] // HEAD_DIM
    c, hc = CHUNK, heads * CHUNK
    t = lax.broadcasted_iota(jnp.int32, (c, hc), 0)
    i = lax.broadcasted_iota(jnp.int32, (c, hc), 1) % c
    strict, incl = i < t, i <= t
    stack = lambda x: _stack(x, HEAD_DIM)
    ba = lax.broadcasted_iota(jnp.int32, s0.shape, 0) // HEAD_DIM
    bb = lax.broadcasted_iota(jnp.int32, s0.shape, 1) // HEAD_DIM
    blocks = (ba == bb).astype(F32)

    cum = cumsum_rows(lw)
    cum_end = jnp.sum(lw, axis=0, keepdims=True)
    beta = kk * a
    al = -kk * jnp.exp(cum - lw)
    p_inv = jnp.exp(-cum)
    be, kb, rb = beta * p_inv, k * p_inv, r * jnp.exp(cum)
    ar = jnp.concatenate([al, rb], axis=0)
    sv = stack(v)
    l_all = dot1(ar, jnp.concatenate([stack(be), stack(kb)], axis=0), NT)
    l_ab = jnp.where(strict, l_all[:c, :hc], 0.0)
    l_ak = jnp.where(strict, l_all[:c, hc:], 0.0)
    l_rb = jnp.where(incl, l_all[c:, :hc], 0.0)
    l_rk = jnp.where(incl, l_all[c:, hc:], 0.0)
    inv = neumann_inv(l_ab)
    from_s0 = dot1(ar, s0, NT)
    from_v = dot1(jnp.concatenate([l_ak, l_rk], axis=0), sv, NN)
    u = dot1(inv, stack(from_s0[:c] + from_v[:c]), NN)
    o = from_s0[c:] + from_v[c:] + dot1(l_rb, stack(u), NN)
    to_end = jnp.exp(cum_end - cum)
    s1 = s0 * jnp.exp(cum_end) + blocks * dot1(
        jnp.concatenate([u, v], axis=0), jnp.concatenate([beta * to_end, k * to_end], axis=0), TN)
    return o, s1


def call_with_comm(plan, middle_step, body, grid, name, in_specs, out_specs, out_shape, scratch_shapes, operands):
    n_in, n_out, n_scr = len(in_specs), len(out_specs), len(scratch_shapes)
    p_in, p_out = len(plan.ins), len(plan.out_shape)

    def fused(*refs):
        refs = list(refs)
        ins, refs = refs[:n_in], refs[n_in:]
        p_ins, refs = refs[:p_in], refs[p_in:]
        outs, refs = refs[:n_out], refs[n_out:]
        p_outs, refs = refs[:p_out], refs[p_out:]
        scr, p_sems = refs[:n_scr], refs[n_scr:]
        start, middle, finish = plan.stages(p_ins, p_outs, p_sems)
        step = pl.program_id(0)
        pl.when(step == 0)(start)
        body(*ins, *outs, *scr)
        pl.when(step == middle_step)(middle)
        pl.when(step == grid[0] - 1)(finish)

    any_spec = pl.BlockSpec(memory_space=pl.ANY)
    res = pl.pallas_call(
        fused, grid=grid, name=name,
        in_specs=list(in_specs) + [any_spec] * p_in, out_specs=list(out_specs) + [any_spec] * p_out,
        out_shape=list(out_shape) + list(plan.out_shape), scratch_shapes=list(scratch_shapes) + list(plan.scratch),
        compiler_params=_params(("arbitrary",)),
    )(*operands, *plan.ins)
    return res[:n_out], res[n_out:]


def _by_group(ref):
    return jnp.stack([ref[:, g * SCAN_WIDTH:(g + 1) * SCAN_WIDTH] for g in range(SCAN_GROUPS)])


def _store_groups(ref, val):
    for g in range(SCAN_GROUPS):
        ref[:, g * SCAN_WIDTH:(g + 1) * SCAN_WIDTH] = val[g]


def rwkv_scan_fwd(r, lw, k, v, kk, a, plan):
    def body(r_ref, lw_ref, k_ref, v_ref, kk_ref, a_ref, o_ref, st_ref, s_ref):
        @pl.when(pl.program_id(0) == 0)
        def _():
            s_ref[...] = jnp.zeros_like(s_ref)

        s0 = s_ref[...]
        st_ref[0] = s0
        o, s1 = jax.vmap(_rwkv_chunk)(s0, *(_by_group(ref) for ref in (r_ref, lw_ref, k_ref, v_ref, kk_ref, a_ref)))
        _store_groups(o_ref, o)
        s_ref[...] = s1

    tb = pl.BlockSpec((CHUNK, D_RWKV), lambda c: (c, 0))
    state = (SCAN_GROUPS, SCAN_WIDTH, SCAN_WIDTH)
    return call_with_comm(
        plan, 3 * N_CHUNKS // 4, body, (N_CHUNKS,), "rwkv_scan_fwd",
        [tb] * 6, [tb, pl.BlockSpec((1,) + state, lambda c: (c, 0, 0, 0))],
        [_sds((SEQ, D_RWKV)), _sds((N_CHUNKS,) + state)], [pltpu.VMEM(state, F32)], (r, lw, k, v, kk, a))


def rwkv_scan_bwd(r, lw, k, v, kk, a, states, d_o, plan):
    def body(r_ref, lw_ref, k_ref, v_ref, kk_ref, a_ref, st_ref, do_ref,
             dr_ref, dlw_ref, dk_ref, dv_ref, dkk_ref, da_ref, ds_ref):
        @pl.when(pl.program_id(0) == 0)
        def _():
            ds_ref[...] = jnp.zeros_like(ds_ref)

        _, vjp = jax.vjp(jax.vmap(_rwkv_chunk), st_ref[0],
                         *(_by_group(ref) for ref in (r_ref, lw_ref, k_ref, v_ref, kk_ref, a_ref)))
        grads = vjp((_by_group(do_ref), ds_ref[...]))
        ds_ref[...] = grads[0]
        for ref, val in zip((dr_ref, dlw_ref, dk_ref, dv_ref, dkk_ref, da_ref), grads[1:]):
            _store_groups(ref, val)

    last = N_CHUNKS - 1
    tb = pl.BlockSpec((CHUNK, D_RWKV), lambda c: (last - c, 0))
    state = (SCAN_GROUPS, SCAN_WIDTH, SCAN_WIDTH)
    return call_with_comm(
        plan, N_CHUNKS // 4, body, (N_CHUNKS,), "rwkv_scan_bwd",
        [tb] * 6 + [pl.BlockSpec((1,) + state, lambda c: (last - c, 0, 0, 0)), tb], [tb] * 6,
        [_sds((SEQ, D_RWKV))] * 6, [pltpu.VMEM(state, F32)], (r, lw, k, v, kk, a, states, d_o))


def _ffn_mid(ug, uv, cg, cv, bg, bv):
    conv_g = bg + cg[0] * shift_rows(ug, 2) + cg[1] * shift_rows(ug, 1) + cg[2] * ug
    conv_v = bv + cv[0] * shift_rows(uv, 2) + cv[1] * shift_rows(uv, 1) + cv[2] * uv
    return jax.nn.gelu(conv_g, approximate=True) * conv_v


def _conv_rows(ref):
    return tuple(ref[0, j:j + 1, :] for j in range(3))


def _ffn_specs(tile):
    per = D_MODEL // tile
    half = N_DEV // 2
    w_g = pl.BlockSpec((1, D_MODEL, tile), lambda t: (t // per, 0, t % per))
    w_v = pl.BlockSpec((1, D_MODEL, tile), lambda t: (half + t // per, 0, t % per))
    c_g = pl.BlockSpec((1, 3, tile), lambda t: (t // per, 0, t % per))
    c_v = pl.BlockSpec((1, 3, tile), lambda t: (half + t // per, 0, t % per))
    b_g = pl.BlockSpec((1, tile), lambda t: (0, t))
    b_v = pl.BlockSpec((1, tile), lambda t: (0, D_FF // tile + t))
    w_d = pl.BlockSpec((tile, D_MODEL), lambda t: (t, 0))
    return w_g, w_v, c_g, c_v, b_g, b_v, w_d


def ffn_fwd(h2, w_up, conv_w, conv_b, w_down):
    def body(h_ref, wg_ref, wv_ref, cg_ref, cv_ref, bg_ref, bv_ref, wd_ref, f_ref):
        @pl.when(pl.program_id(0) == 0)
        def _():
            f_ref[...] = jnp.zeros_like(f_ref)

        h = h_ref[...]
        act = _ffn_mid(_dot(h, wg_ref[0], NN), _dot(h, wv_ref[0], NN), _conv_rows(cg_ref), _conv_rows(cv_ref),
                       bg_ref[...], bv_ref[...])
        f_ref[...] += _dot(act.astype(BF16), wd_ref[...], NN)

    full = pl.BlockSpec((SEQ, D_MODEL), lambda t: (0, 0))
    return pl.pallas_call(
        body, grid=(D_FF // FF_TILE,), name="ffn_fwd",
        in_specs=[full, *_ffn_specs(FF_TILE)],
        out_specs=full, out_shape=_sds((SEQ, D_MODEL)),
        compiler_params=_params(("arbitrary",)),
    )(h2, w_up, w_up, conv_w, conv_w, conv_b, conv_b, w_down)


def ffn_bwd_mid(h2, w_up, conv_w, conv_b, w_down, df):
    tile, rows, halo = FF_TILE, FF_ROW_CHUNK, FF_HALO
    ext = rows + 2 * halo

    def body(h_hbm, wg_ref, wv_ref, cg_ref, cv_ref, bg_ref, bv_ref, wd_ref, df_hbm,
             dug_ref, duv_ref, dcg_ref, dcv_ref, dbg_ref, dbv_ref, dwd_ref,
             h_ref, df_ref, ug_ref, uv_ref, da_ref, act_ref):
        @pl.when(pl.program_id(0) == 0)
        def _():
            pltpu.sync_copy(h_hbm, h_ref)
            pltpu.sync_copy(df_hbm, df_ref)
            for ref in (ug_ref, uv_ref, da_ref):
                ref[0:halo, :] = jnp.zeros((halo, tile), F32)
                ref[halo + SEQ:, :] = jnp.zeros((halo, tile), F32)

        h, df_b = h_ref[...], df_ref[...]
        ug_ref[halo:halo + SEQ, :] = _dot(h, wg_ref[0], NN)
        uv_ref[halo:halo + SEQ, :] = _dot(h, wv_ref[0], NN)
        da_ref[halo:halo + SEQ, :] = _dot(df_b, wd_ref[...], NT)
        cg, cv, bg, bv = _conv_rows(cg_ref), _conv_rows(cv_ref), bg_ref[...], bv_ref[...]
        down = lambda x, n: pltpu.roll(x, n, 0)
        up = lambda x, n: pltpu.roll(x, ext - n, 0)
        mid = slice(halo, halo + rows)

        def chunk(i, sums):
            r0 = pl.multiple_of(i * rows, rows)
            window = pl.ds(r0, ext)
            ug, uv, da = ug_ref[window, :], uv_ref[window, :], da_ref[window, :]
            ug1, ug2, uv1, uv2 = down(ug, 1), down(ug, 2), down(uv, 1), down(uv, 2)
            conv_g = bg + cg[0] * ug2 + cg[1] * ug1 + cg[2] * ug
            conv_v = bv + cv[0] * uv2 + cv[1] * uv1 + cv[2] * uv
            act, vjp = jax.vjp(lambda a, b: jax.nn.gelu(a, approximate=True) * b, conv_g, conv_v)
            dcg, dcv = vjp(da)
            dug = cg[2] * dcg + cg[1] * up(dcg, 1) + cg[0] * up(dcg, 2)
            duv = cv[2] * dcv + cv[1] * up(dcv, 1) + cv[0] * up(dcv, 2)
            out = pl.ds(r0, rows)
            act_ref[out, :] = act[mid].astype(BF16)
            dug_ref[out, :] = dug[mid].astype(BF16)
            duv_ref[out, :] = duv[mid].astype(BF16)
            col = lambda x: jnp.sum(x[mid], axis=0, keepdims=True)
            new = (col(dcg * ug2), col(dcg * ug1), col(dcg * ug), col(dcv * uv2), col(dcv * uv1), col(dcv * uv),
                   col(dcg), col(dcv))
            return tuple(s + n for s, n in zip(sums, new))

        zero = jnp.zeros((1, tile), F32)
        sums = lax.fori_loop(0, SEQ // rows, chunk, (zero,) * 8)
        for j in range(3):
            dcg_ref[0, j:j + 1, :] = sums[j]
            dcv_ref[0, j:j + 1, :] = sums[3 + j]
        dbg_ref[...] = sums[6]
        dbv_ref[...] = sums[7]
        dwd_ref[...] = _dot(act_ref[...], df_b, TN).astype(BF16)

    hbm = pl.BlockSpec(memory_space=pl.ANY)
    w_g, w_v, c_g, c_v, b_g, b_v, w_d = _ffn_specs(tile)
    col = pl.BlockSpec((SEQ, tile), lambda t: (0, t))
    padded = pltpu.VMEM((SEQ + 2 * halo, tile), F32)
    return pl.pallas_call(
        body, grid=(D_FF // tile,), name="ffn_bwd_mid",
        in_specs=[hbm, w_g, w_v, c_g, c_v, b_g, b_v, w_d, hbm],
        out_specs=[col, col, c_g, c_v, b_g, b_v, w_d],
        out_shape=[_sds((SEQ, D_FF), BF16), _sds((SEQ, D_FF), BF16), _sds((N_DEV, 3, D_MODEL)),
                   _sds((N_DEV, 3, D_MODEL)), _sds((1, 2 * D_FF)), _sds((1, 2 * D_FF)), _sds((D_FF, D_MODEL), BF16)],
        scratch_shapes=[pltpu.VMEM((SEQ, D_MODEL), BF16), pltpu.VMEM((SEQ, D_MODEL), BF16), padded, padded, padded,
                        pltpu.VMEM((SEQ, tile), BF16)],
        compiler_params=_params(("arbitrary",)),
    )(h2, w_up, w_up, conv_w, conv_w, conv_b, conv_b, w_down, df)


def ffn_bwd_up(h2, w_up, dug, duv):
    tile = FF_TILE
    per = D_MODEL // tile

    def body(h_hbm, wg_ref, wv_ref, dug_ref, duv_ref, dh_hbm, dup_hbm, h_ref, dh_ref, dwg_ref, dwv_ref, sem, up_sems):
        t = pl.program_id(0)

        @pl.when(t == 0)
        def _():
            pltpu.sync_copy(h_hbm, h_ref)
            dh_ref[...] = jnp.zeros_like(dh_ref)

        h, dug_b, duv_b = h_ref[...], dug_ref[...], duv_ref[...]
        cols = pl.ds(pl.multiple_of((t % per) * tile, tile), tile)
        to_gate = pltpu.make_async_copy(dwg_ref, dup_hbm.at[t // per, :, cols], up_sems.at[0])
        to_value = pltpu.make_async_copy(dwv_ref, dup_hbm.at[N_DEV // 2 + t // per, :, cols], up_sems.at[1])
        dwg_ref[...] = _dot(h, dug_b, TN).astype(BF16)
        to_gate.start()
        dwv_ref[...] = _dot(h, duv_b, TN).astype(BF16)
        to_value.start()
        dh_ref[...] += _dot(jnp.concatenate([dug_b, duv_b], axis=1),
                            jnp.concatenate([wg_ref[0], wv_ref[0]], axis=1), NT)
        to_gate.wait()
        to_value.wait()

        @pl.when(t == D_FF // tile - 1)
        def _():
            cp = pltpu.make_async_copy(dh_ref, dh_hbm, sem)
            cp.start()
            cp.wait()

    hbm = pl.BlockSpec(memory_space=pl.ANY)
    w_g, w_v = _ffn_specs(tile)[:2]
    col = pl.BlockSpec((SEQ, tile), lambda t: (0, t))
    return pl.pallas_call(
        body, grid=(D_FF // tile,), name="ffn_bwd_up",
        in_specs=[hbm, w_g, w_v, col, col], out_specs=[hbm, hbm],
        out_shape=[_sds((SEQ, D_MODEL)), _sds((N_DEV, D_MODEL, D_MODEL), BF16)],
        scratch_shapes=[pltpu.VMEM((SEQ, D_MODEL), BF16), pltpu.VMEM((SEQ, D_MODEL), F32),
                        pltpu.VMEM((D_MODEL, tile), BF16), pltpu.VMEM((D_MODEL, tile), BF16),
                        pltpu.SemaphoreType.DMA, pltpu.SemaphoreType.DMA((2,))],
        compiler_params=_params(("arbitrary",)),
    )(h2, w_up, w_up, dug, duv)


def loss_head(x1, f, target, n_post):
    def tile_loss(x1_t, f_t, g, tgt):
        err = x1_t + _rms(f_t, g) - tgt
        return 0.5 * jnp.sum(jnp.mean(err * err, axis=-1))

    def body(x_ref, f_ref, t_ref, g_ref, dx_ref, df_ref, dg_ref, loss_ref):
        val, (dx, df, dg) = jax.value_and_grad(tile_loss, argnums=(0, 1, 2))(
            x_ref[...], f_ref[...], g_ref[...], t_ref[...])
        dx_ref[...] = dx
        df_ref[...] = df.astype(BF16)

        @pl.when(pl.program_id(0) == 0)
        def _():
            dg_ref[...] = jnp.zeros_like(dg_ref)
            loss_ref[...] = jnp.zeros_like(loss_ref)

        dg_ref[...] += dg
        loss_ref[...] += jnp.full((1, LANES), val, F32)

    tile = pl.BlockSpec((TOK_TILE, D_MODEL), lambda i: (i, 0))
    vec = pl.BlockSpec((1, D_MODEL), lambda i: (0, 0))
    return pl.pallas_call(
        body, grid=(SEQ // TOK_TILE,), name="loss_head",
        in_specs=[tile, tile, tile, vec],
        out_specs=[tile, tile, vec, pl.BlockSpec((1, LANES), lambda i: (0, 0))],
        out_shape=[_sds((SEQ, D_MODEL)), _sds((SEQ, D_MODEL), BF16), _sds((1, D_MODEL)), _sds((1, LANES))],
        compiler_params=_params(("arbitrary",)),
    )(x1, f, target, n_post)


def _mesh_pos():
    return lax.axis_index("x"), lax.axis_index("y"), lax.axis_index("c")


def _flip(pos, rel):
    x, y, c = pos
    return (1 - x if rel & 4 else x, 1 - y if rel & 2 else y, 1 - c if rel & 1 else c)


def _slot(pos):
    x, y, c = pos
    return 4 * x + 2 * y + c


def cast_bf16(w, rows):
    def body(w_ref, o_ref):
        o_ref[...] = w_ref[...].astype(BF16)

    spec = pl.BlockSpec((rows, w.shape[1]), lambda i: (i, 0))
    return pl.pallas_call(body, grid=(w.shape[0] // rows,), name="cast_bf16_%dx%d" % w.shape,
                          in_specs=[spec], out_specs=spec, out_shape=_sds(w.shape, BF16),
                          compiler_params=_params(("arbitrary",)))(w)


class CommPlan:
    def __init__(self, ins, out_shape, scratch, stages):
        self.ins, self.out_shape, self.scratch, self.stages = ins, out_shape, scratch, stages


def run_comm(name, plan):
    n_in, n_out = len(plan.ins), len(plan.out_shape)

    def body(*refs):
        for stage in plan.stages(refs[:n_in], refs[n_in:n_in + n_out], refs[n_in + n_out:]):
            stage()

    any_spec = pl.BlockSpec(memory_space=pl.ANY)
    return pl.pallas_call(
        body, name=name, in_specs=[any_spec] * len(plan.ins), out_specs=[any_spec] * len(plan.out_shape),
        out_shape=plan.out_shape, scratch_shapes=plan.scratch)(*plan.ins)


def gather_plan(shards):
    n = len(shards)

    def stages(srcs, outs, sems):
        send_sems, recv_sems, local_sems = sems

        def places():
            me = _mesh_pos()
            return me, _flip(me, 1), [_flip(me, 2), _flip(me, 4), _flip(me, 6)]

        def copy(a, k, block, to, src=None):
            dst = outs[a].at[_slot(block)]
            return pltpu.make_async_remote_copy(
                src_ref=dst if src is None else src, dst_ref=dst,
                send_sem=send_sems.at[7 * a + k], recv_sem=recv_sems.at[7 * a + k],
                device_id=to, device_id_type=pl.DeviceIdType.MESH)

        def local(a, me):
            return pltpu.make_async_copy(srcs[a], outs[a].at[_slot(me)], local_sems.at[a])

        def own(a, me, sibling, chips):
            return [copy(a, 0, me, sibling, src=srcs[a])] + [
                copy(a, 1 + j, me, chip, src=srcs[a]) for j, chip in enumerate(chips)]

        def start():
            me, sibling, chips = places()
            for a in range(n):
                local(a, me).start()
                for cp in own(a, me, sibling, chips):
                    cp.start()

        def forward():
            me, sibling, chips = places()
            for j, chip in enumerate(chips):
                for a in range(n):
                    copy(a, 1 + j, chip, me).wait_recv()
                    copy(a, 4 + j, chip, sibling).start()

        def finish():
            me, sibling, chips = places()
            for a in range(n):
                copy(a, 0, sibling, me).wait_recv()
                for j, chip in enumerate(chips):
                    copy(a, 4 + j, _flip(chip, 1), me).wait_recv()
            for a in range(n):
                for cp in own(a, me, sibling, chips):
                    cp.wait_send()
                for j, chip in enumerate(chips):
                    copy(a, 4 + j, chip, sibling).wait_send()
                local(a, me).wait()

        return start, forward, finish

    return CommPlan(list(shards), [_sds((N_DEV,) + s.shape, s.dtype) for s in shards],
                    [pltpu.SemaphoreType.DMA((7 * n,)), pltpu.SemaphoreType.DMA((7 * n,)),
                     pltpu.SemaphoreType.DMA((n,))], stages)


def exchange_plan(parts, replicated, rels, members, index, member_axis, own_copy):
    n, nr = len(parts), len(rels)
    pick_index = (slice(None),) * member_axis + (0,)
    subs = [1 if (r or member_axis == 0) else p.shape[0] for p, r in zip(parts, replicated)]
    first = [sum(subs[:a]) for a in range(n)]
    total = sum(subs)

    def stages(srcs, outs, sems):
        send_sems, recv_sems, local_sems = sems

        def src(a, s, pos):
            if replicated[a]:
                return srcs[a]
            return srcs[a].at[index(pos)] if member_axis == 0 else srcs[a].at[s, index(pos)]

        def dst(a, s, pos):
            block = outs[a].at[index(pos)]
            return block if (replicated[a] or member_axis == 0) else block.at[s]

        def copy(a, s, j, me, src_pos, dst_pos):
            sem = nr * (first[a] + s) + j
            return pltpu.make_async_remote_copy(
                src_ref=src(a, s, src_pos), dst_ref=dst(a, s, dst_pos),
                send_sem=send_sems.at[sem], recv_sem=recv_sems.at[sem],
                device_id=_flip(me, rels[j]), device_id_type=pl.DeviceIdType.MESH)

        pieces = [(a, s) for a in range(n) for s in range(subs[a])]

        def local(a, s, me):
            return pltpu.make_async_copy(src(a, s, me), dst(a, s, me), local_sems.at[first[a] + s])

        def sends(me):
            return [copy(a, s, j, me, _flip(me, rels[j]), me) for j in range(nr) for a, s in pieces]

        own = pieces if own_copy else []

        def start():
            me = _mesh_pos()
            for cp in sends(me) + [local(a, s, me) for a, s in own]:
                cp.start()

        def middle():
            pass

        def finish():
            me = _mesh_pos()
            for j in range(nr):
                for a, s in pieces:
                    copy(a, s, j, me, me, _flip(me, rels[j])).wait_recv()
            for cp in sends(me):
                cp.wait_send()
            for a, s in own:
                local(a, s, me).wait()

        return start, middle, finish

    shapes = [p.shape if r else jax.eval_shape(lambda t: t[pick_index], p).shape for p, r in zip(parts, replicated)]
    return CommPlan(list(parts), [_sds((members,) + s, p.dtype) for s, p in zip(shapes, parts)],
                    [pltpu.SemaphoreType.DMA((nr * total,)), pltpu.SemaphoreType.DMA((nr * total,)),
                     pltpu.SemaphoreType.DMA((total,))], stages)


def pair_plan(parts, replicated):
    return exchange_plan(parts, replicated, [1], 2, lambda pos: pos[2], 1, False)


def chip_plan(parts, replicated):
    return exchange_plan(parts, replicated, [2, 4, 6], 4, lambda pos: 2 * pos[0] + pos[1], 0, True)


def add_pair(name, mine, swapped, out_dtype, rows):
    def body(m_ref, s_ref, o_ref):
        own = m_ref[0, 0] if mine.ndim == 4 else m_ref[0]
        o_ref[0] = (own.astype(F32) + s_ref[0, 0].astype(F32)).astype(o_ref.dtype)

    _, n, r, c = swapped.shape
    core = lambda: lax.axis_index("c")
    if mine.ndim == 4:
        mine_spec = pl.BlockSpec((1, 1, rows, c), lambda i, j: (i, core(), j, 0))
    else:
        mine_spec = pl.BlockSpec((1, rows, c), lambda i, j: (i, j, 0))
    return pl.pallas_call(
        body, grid=(n, r // rows), name=name,
        in_specs=[mine_spec, pl.BlockSpec((1, 1, rows, c), lambda i, j: (1 - core(), i, j, 0))],
        out_specs=pl.BlockSpec((1, rows, c), lambda i, j: (i, j, 0)),
        out_shape=_sds((n, r, c), out_dtype),
        compiler_params=_params(("arbitrary", "arbitrary")),
    )(mine, swapped)


def add_pair_small(mines, swappeds):
    n = len(mines)
    halves = [m.ndim == s.ndim for m, s in zip(mines, swappeds)]

    def body(*refs):
        c = lax.axis_index("c")
        for i in range(n):
            m_ref, s_ref, o_ref = refs[i], refs[n + i], refs[2 * n + i]
            o_ref[...] = (m_ref[:, c] if halves[i] else m_ref[...]) + s_ref[1 - c]

    vmem = pl.BlockSpec(memory_space=pltpu.VMEM)
    return pl.pallas_call(
        body, name="pair_add_small", in_specs=[vmem] * (2 * n), out_specs=[vmem] * n,
        out_shape=[_sds(s.shape[1:]) for s in swappeds], compiler_params=_params(),
    )(*mines, *swappeds)


def _adamw_math(w, g, m, v):
    nm = ADAM_B1 * m + (1.0 - ADAM_B1) * g
    nv = ADAM_B2 * v + (1.0 - ADAM_B2) * (g * g)
    m_hat = nm / (1.0 - ADAM_B1 ** ADAM_STEP)
    v_hat = nv / (1.0 - ADAM_B2 ** ADAM_STEP)
    return -ADAM_LR * (m_hat / (jnp.sqrt(v_hat) + ADAM_EPS) + ADAM_WD * w), nm, nv


def adamw_small(ws, parts, ms, vs):
    n = len(ws)

    def body(*refs):
        for i in range(n):
            w_ref, p_ref, m_ref, v_ref = (refs[k * n + i] for k in range(4))
            g = p_ref[0]
            for j in range(1, p_ref.shape[0]):
                g = g + p_ref[j]
            delta, nm, nv = _adamw_math(w_ref[...], g, m_ref[...], v_ref[...])
            for k, val in enumerate((g, delta, nm, nv)):
                refs[(4 + k) * n + i][...] = val

    vmem = pl.BlockSpec(memory_space=pltpu.VMEM)
    outs = pl.pallas_call(
        body, name="adamw_small", in_specs=[vmem] * (4 * n), out_specs=[vmem] * (4 * n),
        out_shape=[_sds(w.shape) for w in ws] * 4, compiler_params=_params(),
    )(*ws, *parts, *ms, *vs)
    return [outs[k * n:(k + 1) * n] for k in range(4)]

def adamw(name, w, parts, m, v, rows, plan=None):
    n_parts = parts.shape[0]

    def body(w_ref, p_ref, m_ref, v_ref, g_ref, d_ref, nm_ref, nv_ref):
        g = p_ref[0].astype(F32)
        for j in range(1, n_parts):
            g = g + p_ref[j].astype(F32)
        g_ref[...] = g
        d_ref[...], nm_ref[...], nv_ref[...] = _adamw_math(w_ref[...], g, m_ref[...], v_ref[...])

    cols = w.shape[1]
    spec = pl.BlockSpec((rows, cols), lambda i: (i, 0))
    grid = (w.shape[0] // rows,)
    in_specs = [spec, pl.BlockSpec((n_parts, rows, cols), lambda i: (0, i, 0)), spec, spec]
    if plan is not None:
        return call_with_comm(plan, 0, body, grid, name, in_specs, [spec] * 4, [_sds(w.shape)] * 4, [],
                              (w, parts, m, v))
    return pl.pallas_call(
        body, grid=grid, name=name, in_specs=in_specs, out_specs=[spec] * 4, out_shape=[_sds(w.shape)] * 4,
        compiler_params=_params(("arbitrary",)),
    )(w, parts, m, v)


def _to_slots(full, per):
    return full.reshape(full.shape[0], N_DEV, per).transpose(1, 0, 2)


def _from_slots(slots):
    return slots.transpose(1, 0, 2).reshape(slots.shape[1], -1)


def kernel(x, norm_mix_pre, norm_mix_post, norm_ffn_pre, norm_ffn_post, w_in, rel_bias, sinks, rwkv_shift_mix, w0, w_decay_up, a0, w_iclr_up, w_gate_up, k_k, k_a, r_k, ln_x_g, ln_x_b, w_out, w_ffn_up, conv_w, conv_b, w_ffn_down, loss_target, m_norm_mix_pre, m_norm_mix_post, m_norm_ffn_pre, m_norm_ffn_post, m_w_in, m_rel_bias, m_sinks, m_rwkv_shift_mix, m_w0, m_w_decay_up, m_a0, m_w_iclr_up, m_w_gate_up, m_k_k, m_k_a, m_r_k, m_ln_x_g, m_ln_x_b, m_w_out, m_w_ffn_up, m_conv_w, m_conv_b, m_w_ffn_down, v_norm_mix_pre, v_norm_mix_post, v_norm_ffn_pre, v_norm_ffn_post, v_w_in, v_rel_bias, v_sinks, v_rwkv_shift_mix, v_w0, v_w_decay_up, v_a0, v_w_iclr_up, v_w_gate_up, v_k_k, v_k_a, v_r_k, v_ln_x_g, v_ln_x_b, v_w_out, v_w_ffn_up, v_conv_w, v_conv_b, v_w_ffn_down):
    x2 = x[0]
    target = loss_target[0]

    in_gather = gather_plan([cast_bf16(w_in[0], 256)])
    mixer_gather = gather_plan([cast_bf16(w_out[0], 128), w_decay_up[0], w_iclr_up[0], w_gate_up[0], conv_w[0]])
    up_gather = gather_plan([cast_bf16(w_ffn_up[0], 256)])
    down_gather = gather_plan([cast_bf16(w_ffn_down[0], 256)])
    mix_ext = jnp.concatenate([jnp.zeros((1, D_QKV), F32), rwkv_shift_mix], axis=1)
    r_k_row = r_k.reshape(1, D_RWKV)
    bucket = _bucket_table()

    (h1,), (g_in,) = tok_fwd("rms_mix_pre", rms_tile, [x2], [norm_mix_pre], [], [D_MODEL], [BF16], plan=in_gather)
    w_in_b = _from_slots(g_in)
    (proj, ps), (g_out, g_decay, g_iclr, g_gate, g_conv) = in_proj_fwd(h1, w_in_b, mix_ext, mixer_gather)
    w_out_b = g_out.reshape(D_MODEL, D_MODEL)
    lora = jnp.zeros((HEAD_DIM, D_RWKV), F32)
    wd_pad = jnp.concatenate([_from_slots(g_decay), lora], axis=0)
    wi_pad = jnp.concatenate([lora, _from_slots(g_iclr)], axis=0)
    wg_full = _from_slots(g_gate)
    attn, (g_down,) = attn_fwd(proj, rel_bias, bucket, sinks, down_gather)
    pre_params = [w0, wd_pad, a0, wi_pad, wg_full, k_k, k_a]
    r_, lw_, k2_, v_, kk_, a_, gate_ = tok_fwd("rwkv_pre", rwkv_pre_tile, [ps], pre_params, [],
                                               [D_RWKV] * 7, [F32] * 7)
    (o_, states), (g_up,) = rwkv_scan_fwd(r_, lw_, k2_, v_, kk_, a_, up_gather)
    w_down_b = g_down.reshape(D_FF, D_MODEL)
    mix_tiles = [o_, r_, k2_, v_, gate_, attn, x2]
    mix_params = [w_out_b, norm_mix_post, ln_x_g, ln_x_b, r_k_row, norm_ffn_pre]
    x1, h2 = tok_fwd("mix_out", mix_out_tile, mix_tiles, mix_params, [(D_MODEL, D_MODEL)], [D_MODEL, D_MODEL],
                     [F32, BF16])
    f = ffn_fwd(h2, g_up, g_conv, conv_b, w_down_b)
    dy, df, d_n_ffn_post, loss_row = loss_head(x1, f, target, norm_ffn_post)

    d_ug, d_uv, d_cw_g, d_cw_v, d_cb_g, d_cb_v, d_down = ffn_bwd_mid(h2, g_up, g_conv, conv_b, w_down_b, df)
    dh2, d_up = ffn_bwd_up(h2, g_up, d_ug, d_uv)
    half = N_DEV // 2
    d_cw = jnp.concatenate([d_cw_g[:half], d_cw_v[half:]], axis=0)
    by_pair = lambda slots: slots.reshape((N_DEV // 2, 2) + slots.shape[1:])
    ffn_mine = [by_pair(d_up), by_pair(d_down.reshape(N_DEV, D_FF // N_DEV, D_MODEL))]
    ffn_swapped = run_comm("pair_exchange_ffn", pair_plan(ffn_mine, [False, False]))
    up_exchange = chip_plan([add_pair("pair_add_w_ffn_up", ffn_mine[0], ffn_swapped[0], BF16, 256)], [False])
    down_exchange = chip_plan([add_pair("pair_add_w_ffn_down", ffn_mine[1], ffn_swapped[1], BF16, 256)], [False])
    d_cb = jnp.concatenate([d_cb_g[:, :D_FF], d_cb_v[:, D_FF:]], axis=1)
    ((d_o, d_r1, d_k1, d_v1, d_gate, d_attn, dx_res, d_n_mix_post, d_ln_g, d_ln_b, d_r_k, d_n_ffn_pre, d_w_out),
     (got_down,)) = tok_bwd("mix_out_bwd", mix_out_tile, mix_tiles, mix_params, [(D_MODEL, D_MODEL)], [dy, dh2],
                            [1, 2, 3, 4, 5], plan=down_exchange)
    (d_r2, d_lw, d_k2, d_v2, d_kk, d_a), (got_up,) = rwkv_scan_bwd(
        r_, lw_, k2_, v_, kk_, a_, states, d_o, up_exchange)
    pre_cots = [(d_r1, d_r2), d_lw, (d_k1, d_k2), (d_v1, d_v2), d_kk, d_a, d_gate]
    (d_ps, d_w0, d_wd_pad, d_a0, d_wi_pad, d_wg, d_k_k, d_k_a) = tok_bwd(
        "rwkv_pre_bwd", rwkv_pre_tile, [ps], pre_params, [], pre_cots, [0, 1, 2, 3, 4, 5, 6])
    dq, dkc, dkp, dvc, dvp, d_rel_bias, d_sinks = attn_bwd(proj, rel_bias, bucket, sinks, d_attn)
    zero_blk = jnp.zeros((BLOCK, D_KV), F32)
    dk = dkc + jnp.concatenate([dkp[BLOCK:], zero_blk], axis=0)
    dv = dvc + jnp.concatenate([dvp[BLOCK:], zero_blk], axis=0)
    dpa = jnp.concatenate([dq, dk, dv, d_ps], axis=1)
    dh1, d_w_in, d_mix_ext = in_proj_bwd(h1, w_in_b, mix_ext, proj, dpa)
    grad_x2, d_n_mix_pre = tok_bwd("rms_mix_pre_bwd", rms_tile, [x2], [norm_mix_pre], [], [dh1], [0], {0: dx_res})
    grad_x = grad_x2[None]

    small_rep = [d_n_mix_pre, d_n_mix_post, d_n_ffn_pre, d_n_ffn_post, d_rel_bias, d_sinks,
                 d_mix_ext[:, D_QKV:], d_w0, d_a0, d_k_k, d_k_a, d_r_k.reshape(r_k.shape), d_ln_g, d_ln_b, d_cb]
    rep_w = [norm_mix_pre, norm_mix_post, norm_ffn_pre, norm_ffn_post, rel_bias, sinks, rwkv_shift_mix,
             w0, a0, k_k, k_a, r_k, ln_x_g, ln_x_b, conv_b]
    rep_m = [m_norm_mix_pre, m_norm_mix_post, m_norm_ffn_pre, m_norm_ffn_post, m_rel_bias, m_sinks,
             m_rwkv_shift_mix, m_w0, m_a0, m_k_k, m_k_a, m_r_k, m_ln_x_g, m_ln_x_b, m_conv_b]
    rep_v = [v_norm_mix_pre, v_norm_mix_post, v_norm_ffn_pre, v_norm_ffn_post, v_rel_bias, v_sinks,
             v_rwkv_shift_mix, v_w0, v_a0, v_k_k, v_k_a, v_r_k, v_ln_x_g, v_ln_x_b, v_conv_b]
    sh_w = [w_decay_up, w_iclr_up, w_gate_up, conv_w]
    sh_m = [m_w_decay_up, m_w_iclr_up, m_w_gate_up, m_conv_w]
    sh_v = [v_w_decay_up, v_w_iclr_up, v_w_gate_up, v_conv_w]
    sh_parts = [_to_slots(d_wd_pad[:HEAD_DIM], HEAD_DIM), _to_slots(d_wi_pad[HEAD_DIM:], HEAD_DIM),
                _to_slots(d_wg, HEAD_DIM), d_cw]
    n_rep, n_sh = len(small_rep), len(sh_parts)
    mine = [by_pair(_to_slots(d_w_in, D_IN // N_DEV)), by_pair(d_w_out.reshape(N_DEV, D_MODEL // N_DEV, D_MODEL)),
            *small_rep, *(by_pair(p) for p in sh_parts), loss_row]
    is_rep = [False, False] + [True] * n_rep + [False] * n_sh + [True]
    adam_down, swapped = adamw("adamw_w_ffn_down", w_ffn_down[0], got_down, m_w_ffn_down[0], v_w_ffn_down[0], 128,
                               pair_plan(mine, is_rep))
    chip_sums = [add_pair("pair_add_w_in", mine[0], swapped[0], BF16, 512),
                 add_pair("pair_add_w_out", mine[1], swapped[1], BF16, 128),
                 *add_pair_small(mine[2:], swapped[2:])]
    adam_up, got = adamw("adamw_w_ffn_up", w_ffn_up[0], got_up, m_w_ffn_up[0], v_w_ffn_up[0], 128,
                         chip_plan(chip_sums, is_rep))

    big = [adamw("adamw_w_in", w_in[0], got[0], m_w_in[0], v_w_in[0], 256),
           adamw("adamw_w_out", w_out[0], got[1], m_w_out[0], v_w_out[0], 128), adam_up, adam_down]
    loss = functools.reduce(jnp.add, [got[-1][q, 0, 0] for q in range(N_DEV // 2)])
    small_w, small_g = rep_w + sh_w, got[2:-1]
    as_grad = lambda arrays: [a.reshape(g.shape[1:]) for a, g in zip(arrays, small_g)]
    small = adamw_small(as_grad(small_w), small_g, as_grad(rep_m + sh_m), as_grad(rep_v + sh_v))
    small = [[a.reshape(w.shape) for a, w in zip(kind, small_w)] for kind in small]

    names = ["norm_mix_pre", "norm_mix_post", "norm_ffn_pre", "norm_ffn_post", "w_in", "rel_bias", "sinks",
             "rwkv_shift_mix", "w0", "w_decay_up", "a0", "w_iclr_up", "w_gate_up", "k_k", "k_a", "r_k",
             "ln_x_g", "ln_x_b", "w_out", "w_ffn_up", "conv_w", "conv_b", "w_ffn_down"]
    small_names = ["norm_mix_pre", "norm_mix_post", "norm_ffn_pre", "norm_ffn_post", "rel_bias", "sinks",
                   "rwkv_shift_mix", "w0", "a0", "k_k", "k_a", "r_k", "ln_x_g", "ln_x_b", "conv_b",
                   "w_decay_up", "w_iclr_up", "w_gate_up", "conv_w"]
    big_names = {"w_in": 0, "w_out": 1, "w_ffn_up": 2, "w_ffn_down": 3}
    outs = []
    for kind in range(4):
        for nm in names:
            if nm in big_names:
                outs.append(big[big_names[nm]][kind][None])
            else:
                outs.append(small[kind][small_names.index(nm)])
    return (loss, grad_x, *outs)
```

```python
import functools
import math

import jax
import jax.numpy as jnp
from jax import lax
from jax.experimental import pallas as pl
from jax.experimental.pallas import tpu as pltpu

F32 = jnp.float32
BF16 = jnp.bfloat16

N_DEV = 8
SEQ = 2048
D_MODEL = 1024
HEAD_DIM = 64
D_ATTN = 512
D_KV = 128
D_RWKV = 512
N_HEADS = 8
RWKV_COLS = 1792
D_QKV = D_ATTN + 2 * D_KV
D_IN = D_QKV + RWKV_COLS
D_FF = 4096
BLOCK = 128
N_BLOCKS = SEQ // BLOCK
N_BUCKETS = 32
MAX_DISTANCE = 128
NORM_EPS = 1e-6
GN_EPS = 64e-5
NEG_INF = -1e30
CHUNK = 64
N_CHUNKS = SEQ // CHUNK
SCAN_GROUPS = 4
SCAN_WIDTH = D_RWKV // SCAN_GROUPS
TOK_TILE = 256
FF_TILE = 256
FF_ROW_CHUNK = 256
FF_HALO = 8
COL_TILE = 256
LANES = 128
VMEM_LIMIT = 56 * 1024 * 1024

ADAM_LR = 0.001
ADAM_B1 = 0.9
ADAM_B2 = 0.999
ADAM_EPS = 1e-08
ADAM_WD = 0.01
ADAM_STEP = 10

NT = ((1,), (1,))
TN = ((0,), (0,))
NN = ((1,), (0,))


def _sds(shape, dtype=F32):
    return jax.ShapeDtypeStruct(shape, dtype)


def _params(sem=None):
    if sem is None:
        return pltpu.CompilerParams(vmem_limit_bytes=VMEM_LIMIT)
    return pltpu.CompilerParams(dimension_semantics=sem, vmem_limit_bytes=VMEM_LIMIT)


def _dot(a, b, dims):
    return lax.dot_general(a, b, (dims, ((), ())), preferred_element_type=F32)


def _split2(x):
    hi = x.astype(BF16)
    return hi, (x - hi.astype(F32)).astype(BF16)


def _dot3_raw(a, b, dims):
    ah, al = _split2(a)
    bh, bl = _split2(b)
    return _dot(ah, bh, dims) + (_dot(al, bh, dims) + _dot(ah, bl, dims))


@functools.partial(jax.custom_vjp, nondiff_argnums=(2,))
def dot3(a, b, dims):
    return _dot3_raw(a, b, dims)


def _dot3_fwd(a, b, dims):
    return _dot3_raw(a, b, dims), (a, b)


def _dot3_bwd(dims, res, g):
    a, b = res
    if dims == NN:
        return dot3(g, b, NT), dot3(a, g, TN)
    if dims == NT:
        return dot3(g, b, NN), dot3(g, a, TN)
    return dot3(b, g, NT), dot3(a, g, NN)


dot3.defvjp(_dot3_fwd, _dot3_bwd)


@functools.partial(jax.custom_vjp, nondiff_argnums=(2,))
def dot1(a, b, dims):
    return _dot(a.astype(BF16), b.astype(BF16), dims)


def _dot1_fwd(a, b, dims):
    return dot1(a, b, dims), (a, b)


def _dot1_bwd(dims, res, g):
    a, b = res
    if dims == NN:
        return dot1(g, b, NT), dot1(a, g, TN)
    if dims == NT:
        return dot1(g, b, NN), dot1(g, a, TN)
    return dot1(b, g, NT), dot1(a, g, NN)


dot1.defvjp(_dot1_fwd, _dot1_bwd)


@jax.custom_vjp
def mm(a, b):
    return _dot(a.astype(BF16), b.astype(BF16), NN)


def _mm_fwd(a, b):
    return mm(a, b), (a, b)


def _mm_bwd(res, g):
    a, b = res
    gb = g.astype(BF16)
    return _dot(gb, b.astype(BF16), NT).astype(a.dtype), _dot(a.astype(BF16), gb, TN).astype(b.dtype)


mm.defvjp(_mm_fwd, _mm_bwd)


@jax.custom_vjp
def mm_nt(a, b):
    return _dot(a.astype(BF16), b.astype(BF16), NT)


def _mm_nt_fwd(a, b):
    return mm_nt(a, b), (a, b)


def _mm_nt_bwd(res, g):
    a, b = res
    gb = g.astype(BF16)
    return _dot(gb, b.astype(BF16), NN).astype(a.dtype), _dot(gb, a.astype(BF16), TN).astype(b.dtype)


mm_nt.defvjp(_mm_nt_fwd, _mm_nt_bwd)


@jax.custom_vjp
def mmw(a, w, wz):
    return _dot(a.astype(BF16), w, NN)


def _mmw_fwd(a, w, wz):
    return mmw(a, w, wz), (a, w)


def _mmw_bwd(res, g):
    a, w = res
    gb = g.astype(BF16)
    return _dot(gb, w, NT).astype(a.dtype), jnp.zeros_like(w), _dot(a.astype(BF16), gb, TN)


mmw.defvjp(_mmw_fwd, _mmw_bwd)


def _shift_raw(x, n):
    rows = x.shape[0]
    rolled = pltpu.roll(x, n % rows, 0)
    idx = lax.broadcasted_iota(jnp.int32, x.shape, 0)
    keep = idx >= n if n > 0 else idx < rows + n
    return jnp.where(keep, rolled, 0.0)


@functools.partial(jax.custom_vjp, nondiff_argnums=(1,))
def shift_rows(x, n):
    return _shift_raw(x, n)


def _shift_fwd(x, n):
    return _shift_raw(x, n), None


def _shift_bwd(n, _, g):
    return (_shift_raw(g, -n),)


shift_rows.defvjp(_shift_fwd, _shift_bwd)


def _head_sum(x, scale):
    a = lax.broadcasted_iota(jnp.int32, (LANES, LANES), 0) // HEAD_DIM
    b = lax.broadcasted_iota(jnp.int32, (LANES, LANES), 1) // HEAD_DIM
    pair = jnp.where(a == b, scale, 0.0).astype(F32)
    return jnp.concatenate([dot3(x[:, i:i + LANES], pair, NN) for i in range(0, x.shape[1], LANES)], axis=1)


def _rms(x, g):
    return x * lax.rsqrt(jnp.mean(x * x, axis=-1, keepdims=True) + NORM_EPS) * g


def _softplus(x):
    return jnp.maximum(x, 0.0) + jnp.log(1.0 + jnp.exp(-jnp.abs(x)))


def _tile_spec(arr, tm):
    return pl.BlockSpec((tm, arr.shape[1]), lambda i: (i, 0))


def _full_spec(arr):
    nd = arr.ndim
    return pl.BlockSpec(arr.shape, lambda i: (0,) * nd)


def tok_fwd(name, fn, tiles, params, zero_shapes, out_widths, out_dtypes, plan=None, tm=TOK_TILE):
    n_t, n_p = len(tiles), len(params)

    def body(*refs):
        t_vals = [r[...] for r in refs[:n_t]]
        p_vals = [r[...] for r in refs[n_t:n_t + n_p]]
        z_vals = [jnp.zeros(s, F32) for s in zero_shapes]
        outs = fn(*t_vals, *p_vals, *z_vals)
        for r, o in zip(refs[n_t + n_p:], outs):
            r[...] = o.astype(r.dtype)

    rows = tiles[0].shape[0]
    steps = rows // tm
    in_specs = [_tile_spec(t, tm) for t in tiles] + [_full_spec(p) for p in params]
    out_specs = [pl.BlockSpec((tm, w), lambda i: (i, 0)) for w in out_widths]
    out_shape = [_sds((rows, w), dt) for w, dt in zip(out_widths, out_dtypes)]
    if plan is not None:
        return call_with_comm(plan, steps // 2, body, (steps,), name, in_specs, out_specs, out_shape, [],
                              (*tiles, *params))
    return pl.pallas_call(
        body, grid=(steps,), name=name, in_specs=in_specs, out_specs=out_specs, out_shape=out_shape,
        compiler_params=_params(("arbitrary",)),
    )(*tiles, *params)


def tok_bwd(name, fn, tiles, params, zero_shapes, cots, diff_params, residuals=(), plan=None, tm=TOK_TILE):
    cot_parts = [c if isinstance(c, tuple) else (c,) for c in cots]
    flat_cots = [a for part in cot_parts for a in part]
    residuals = dict(residuals)
    extra = [residuals[i] for i in sorted(residuals)]
    n_t, n_p, n_c, n_r = len(tiles), len(params), len(flat_cots), len(extra)
    acc_shapes = [params[i].shape for i in diff_params] + list(zero_shapes)

    def body(*refs):
        t_vals = [r[...].astype(F32) for r in refs[:n_t]]
        p_vals = [r[...] for r in refs[n_t:n_t + n_p]]
        flat = iter(r[...] for r in refs[n_t + n_p:n_t + n_p + n_c])
        c_vals = [functools.reduce(jnp.add, [next(flat) for _ in part]) for part in cot_parts]
        r_vals = dict(zip(sorted(residuals), (r[...] for r in refs[n_t + n_p + n_c:n_t + n_p + n_c + n_r])))
        out_refs = refs[n_t + n_p + n_c + n_r:]
        z_vals = [jnp.zeros(s, F32) for s in zero_shapes]
        d_vals = [p_vals[i] for i in diff_params]

        def f(t_in, d_in, z_in):
            full = list(p_vals)
            for i, v in zip(diff_params, d_in):
                full[i] = v
            return tuple(fn(*t_in, *full, *z_in))

        _, vjp = jax.vjp(f, t_vals, d_vals, z_vals)
        g_t, g_d, g_z = vjp(tuple(c_vals))
        for i, (r, g) in enumerate(zip(out_refs[:n_t], g_t)):
            r[...] = (g + r_vals[i] if i in r_vals else g).astype(r.dtype)
        acc_refs = out_refs[n_t:]

        @pl.when(pl.program_id(0) == 0)
        def _():
            for r in acc_refs:
                r[...] = jnp.zeros_like(r)

        for r, g in zip(acc_refs, list(g_d) + list(g_z)):
            r[...] += g

    rows = tiles[0].shape[0]
    in_specs = ([_tile_spec(t, tm) for t in tiles] + [_full_spec(p) for p in params]
                + [_tile_spec(c, tm) for c in flat_cots + extra])
    out_specs = ([_tile_spec(t, tm) for t in tiles]
                 + [pl.BlockSpec(s, lambda i, nd=len(s): (0,) * nd) for s in acc_shapes])
    out_shape = [_sds(t.shape) for t in tiles] + [_sds(s) for s in acc_shapes]
    operands = (*tiles, *params, *flat_cots, *extra)
    if plan is not None:
        return call_with_comm(plan, 0, body, (rows // tm,), name, in_specs, out_specs, out_shape, [], operands)
    return pl.pallas_call(
        body, grid=(rows // tm,), name=name, in_specs=in_specs, out_specs=out_specs, out_shape=out_shape,
        compiler_params=_params(("arbitrary",)),
    )(*operands)


def rms_tile(x, g):
    return (_rms(x, g),)


def rwkv_pre_tile(ps, w0, wd_pad, a0, wi_pad, wg, k_k, k_a):
    r = ps[:, 0:D_RWKV]
    k = ps[:, D_RWKV:2 * D_RWKV]
    v = ps[:, 2 * D_RWKV:3 * D_RWKV]
    z2 = ps[:, 3 * D_RWKV:3 * D_RWKV + LANES]
    zg = ps[:, 3 * D_RWKV + LANES:RWKV_COLS]
    w_log = -_softplus(-(w0 + mm(jnp.tanh(z2), wd_pad))) - 0.5
    lw = -jnp.exp(w_log)
    a = jax.nn.sigmoid(a0 + mm(z2, wi_pad))
    g = mm(jax.nn.sigmoid(zg), wg)
    kk = k * k_k
    norm = jnp.sqrt(_head_sum(kk * kk, 1.0))
    kk = kk / jnp.maximum(norm, 1e-12)
    k2 = k * (1.0 + (a - 1.0) * k_a)
    return r, lw, k2, v, kk, a, g


def mix_out_tile(o, r, k2, v, g, attn, x, w_out, n_post, ln_g, ln_b, r_k, n_ffn_pre, wz):
    d = o - _head_sum(o, 1.0 / HEAD_DIM)
    var = _head_sum(d * d, 1.0 / HEAD_DIM)
    on = d * lax.rsqrt(var + GN_EPS) * ln_g + ln_b
    bonus = _head_sum(r * k2 * r_k, 1.0) * v
    rw = (on + bonus) * g
    mix = mmw(jnp.concatenate([attn, rw], axis=1), w_out, wz)
    x1 = x + _rms(mix, n_post)
    return x1, _rms(x1, n_ffn_pre)


def in_proj_fwd(h, w_in, mix_ext, plan):
    def body(h_ref, w_ref, m_ref, proj_ref, ps_ref):
        p = _dot(h_ref[...], w_ref[...], NN)
        proj_ref[...] = p
        ps_ref[...] = p + (_shift_raw(p, 1) - p) * m_ref[...]

    n = D_IN // COL_TILE
    first = D_QKV // COL_TILE
    return call_with_comm(
        plan, n // 2, body, (n,), "in_proj_fwd",
        [pl.BlockSpec((SEQ, D_MODEL), lambda j: (0, 0)), pl.BlockSpec((D_MODEL, COL_TILE), lambda j: (0, j)),
         pl.BlockSpec((1, COL_TILE), lambda j: (0, j))],
        [pl.BlockSpec((SEQ, COL_TILE), lambda j: (0, j)),
         pl.BlockSpec((SEQ, COL_TILE), lambda j: (0, jnp.maximum(j - first, 0)))],
        [_sds((SEQ, D_IN)), _sds((SEQ, RWKV_COLS))], [], (h, w_in, mix_ext))


def in_proj_bwd(h, w_in, mix_ext, proj, dpa):
    def body(h_ref, w_ref, m_ref, p_ref, d_ref, dh_ref, dw_ref, dm_ref):
        d = d_ref[...]
        p = p_ref[...]
        dm_ref[...] = jnp.sum(d * (_shift_raw(p, 1) - p), axis=0, keepdims=True)
        dmix = d * m_ref[...]
        dp = (d - dmix + _shift_raw(dmix, -1)).astype(BF16)
        dw_ref[...] = _dot(h_ref[...], dp, TN)

        @pl.when(pl.program_id(0) == 0)
        def _():
            dh_ref[...] = jnp.zeros_like(dh_ref)

        dh_ref[...] += _dot(dp, w_ref[...], NT)

    n = D_IN // COL_TILE
    col = lambda rows: pl.BlockSpec((rows, COL_TILE), lambda j: (0, j))
    return pl.pallas_call(
        body, grid=(n,), name="in_proj_bwd",
        in_specs=[pl.BlockSpec((SEQ, D_MODEL), lambda j: (0, 0)), col(D_MODEL), col(1), col(SEQ), col(SEQ)],
        out_specs=[pl.BlockSpec((SEQ, D_MODEL), lambda j: (0, 0)), col(D_MODEL), col(1)],
        out_shape=[_sds((SEQ, D_MODEL)), _sds((D_MODEL, D_IN)), _sds((1, D_IN))],
        compiler_params=_params(("arbitrary",)),
    )(h, w_in, mix_ext, proj, dpa)


def _bucket_table():
    rel = (jnp.arange(BLOCK)[:, None] + BLOCK) - jnp.arange(2 * BLOCK)[None, :]
    n = jnp.maximum(rel, 0)
    max_exact = N_BUCKETS // 2
    large = max_exact + (jnp.log(jnp.maximum(n, 1).astype(F32) / max_exact)
                         / math.log(MAX_DISTANCE / max_exact) * (N_BUCKETS - max_exact)).astype(jnp.int32)
    large = jnp.minimum(large, N_BUCKETS - 1)
    return jnp.where(n < max_exact, n, large).astype(jnp.int32)


def _select_matrix(g, o):
    a = lax.broadcasted_iota(jnp.int32, (D_KV, D_KV), 0)
    b = lax.broadcasted_iota(jnp.int32, (D_KV, D_KV), 1)
    return ((a - HEAD_DIM * g == b - o) & (b >= o) & (b < o + HEAD_DIM)).astype(F32)


def _attn_block(q, kp, kc, vp, vc, bias, sinks, block_idx):
    kb = jnp.concatenate([kp, kc], axis=0)
    vb = jnp.concatenate([vp, vc], axis=0)
    row = lax.broadcasted_iota(jnp.int32, (BLOCK, 2 * BLOCK), 0)
    col = lax.broadcasted_iota(jnp.int32, (BLOCK, 2 * BLOCK), 1)
    rel = row + BLOCK - col
    mask = (rel >= 0) & (rel < BLOCK) & (col + (block_idx - 1) * BLOCK >= 0)
    lane8 = lax.broadcasted_iota(jnp.int32, (1, N_HEADS), 1)
    kt, vt = {}, {}
    for g in range(2):
        for o in (0, HEAD_DIM):
            sel = _select_matrix(g, o)
            kt[g, o] = mm(kb, sel)
            vt[g, o] = mm(vb, sel)
    outs = []
    for j in range(D_ATTN // LANES):
        qs = q[:, j * LANES:(j + 1) * LANES]
        acc = None
        for half in range(2):
            hq = 2 * j + half
            g, o = hq // 4, half * HEAD_DIM
            s = mm_nt(qs, kt[g, o]) * (HEAD_DIM ** -0.5) + bias[hq]
            s = jnp.where(mask, s, NEG_INF)
            sink = jnp.sum(jnp.where(lane8 == hq, sinks, 0.0), axis=1, keepdims=True)
            m = lax.stop_gradient(jnp.maximum(jnp.max(s, axis=-1, keepdims=True), sink))
            p = jnp.exp(s - m)
            probs = p / (jnp.sum(p, axis=-1, keepdims=True) + jnp.exp(sink - m))
            part = mm(probs, vt[g, o])
            acc = part if acc is None else acc + part
        outs.append(acc)
    return jnp.concatenate(outs, axis=1)


def _build_bias(rb_ref, bucket, bias_ref):
    for hq in range(N_HEADS):
        acc = jnp.zeros((BLOCK, 2 * BLOCK), F32)
        for b in range(N_BUCKETS):
            acc = jnp.where(bucket == b, rb_ref[b, hq], acc)
        bias_ref[hq] = acc


def _attn_in_specs(block=lambda n: n):
    prev = lambda n: jnp.maximum(block(n) - 1, 0)
    return [pl.BlockSpec((BLOCK, D_ATTN), lambda n: (block(n), 0)),
            pl.BlockSpec((BLOCK, D_KV), lambda n: (prev(n), D_ATTN // D_KV)),
            pl.BlockSpec((BLOCK, D_KV), lambda n: (block(n), D_ATTN // D_KV)),
            pl.BlockSpec((BLOCK, D_KV), lambda n: (prev(n), D_ATTN // D_KV + 1)),
            pl.BlockSpec((BLOCK, D_KV), lambda n: (block(n), D_ATTN // D_KV + 1)),
            pl.BlockSpec(memory_space=pltpu.SMEM),
            pl.BlockSpec((BLOCK, 2 * BLOCK), lambda n: (0, 0)),
            pl.BlockSpec((1, N_HEADS), lambda n: (0, 0))]


def attn_bwd(proj, rel_bias, bucket, sinks, d_attn):
    def body(q_ref, kp_ref, kc_ref, vp_ref, vc_ref, rb_ref, bk_ref, sk_ref, do_ref,
             dq_ref, dkc_ref, dkp_ref, dvc_ref, dvp_ref, drb_ref, dsk_ref, bias_ref, dbias_ref):
        n = pl.program_id(0)

        @pl.when(n == 0)
        def _():
            _build_bias(rb_ref, bk_ref[...], bias_ref)
            dbias_ref[...] = jnp.zeros_like(dbias_ref)
            dsk_ref[...] = jnp.zeros_like(dsk_ref)

        f = lambda q, kp, kc, vp, vc, bias, sk: _attn_block(q, kp, kc, vp, vc, bias, sk, n)
        _, vjp = jax.vjp(f, q_ref[...], kp_ref[...], kc_ref[...], vp_ref[...], vc_ref[...],
                         tuple(bias_ref[h] for h in range(N_HEADS)), sk_ref[...])
        dq, dkp, dkc, dvp, dvc, dbias, dsk = vjp(do_ref[...])
        dq_ref[...] = dq
        dkc_ref[...] = dkc
        dkp_ref[...] = dkp
        dvc_ref[...] = dvc
        dvp_ref[...] = dvp
        for h in range(N_HEADS):
            dbias_ref[h] += dbias[h]
        dsk_ref[...] += dsk

        @pl.when(n == N_BLOCKS - 1)
        def _():
            bucket_v = bk_ref[...]
            rowi = lax.broadcasted_iota(jnp.int32, (N_BUCKETS, 2 * BLOCK), 0)
            lane = lax.broadcasted_iota(jnp.int32, (N_BUCKETS, N_HEADS), 1)
            out = jnp.zeros((N_BUCKETS, N_HEADS), F32)
            for hq in range(N_HEADS):
                dbh = dbias_ref[hq]
                rows = jnp.zeros((N_BUCKETS, 2 * BLOCK), F32)
                for b in range(N_BUCKETS):
                    part = jnp.sum(jnp.where(bucket_v == b, dbh, 0.0), axis=0, keepdims=True)
                    rows = jnp.where(rowi == b, part, rows)
                tot = jnp.sum(rows, axis=1, keepdims=True)
                out = jnp.where(lane == hq, tot, out)
            drb_ref[...] = out

    blk = lambda w: pl.BlockSpec((BLOCK, w), lambda n: (n, 0))
    return pl.pallas_call(
        body, grid=(N_BLOCKS,), name="attn_bwd",
        in_specs=_attn_in_specs() + [blk(D_ATTN)],
        out_specs=[blk(D_ATTN), blk(D_KV), blk(D_KV), blk(D_KV), blk(D_KV),
                   pl.BlockSpec((N_BUCKETS, N_HEADS), lambda n: (0, 0)),
                   pl.BlockSpec((1, N_HEADS), lambda n: (0, 0))],
        out_shape=[_sds((SEQ, D_ATTN)), _sds((SEQ, D_KV)), _sds((SEQ, D_KV)), _sds((SEQ, D_KV)),
                   _sds((SEQ, D_KV)), _sds((N_BUCKETS, N_HEADS)), _sds((1, N_HEADS))],
        scratch_shapes=[pltpu.VMEM((N_HEADS, BLOCK, 2 * BLOCK), F32),
                        pltpu.VMEM((N_HEADS, BLOCK, 2 * BLOCK), F32)],
        compiler_params=_params(("arbitrary",)),
    )(proj, proj, proj, proj, proj, rel_bias, bucket, sinks, d_attn)


def _stack(x, size):
    groups = x.shape[1] // size
    lane = lax.broadcasted_iota(jnp.int32, x.shape, 1) // size
    return jnp.concatenate([jnp.where(lane == i, x, 0.0) for i in range(groups)], axis=0)


def _neumann(l):
    c = CHUNK
    t = lax.broadcasted_iota(jnp.int32, l.shape, 0)
    i = lax.broadcasted_iota(jnp.int32, l.shape, 1) % c
    inv = (i == t).astype(F32) + l
    pw = dot1(l, _stack(l, c), NN)
    for _ in range(int(math.log2(c)) - 2):
        both = dot1(jnp.concatenate([inv, pw], axis=0), _stack(pw, c), NN)
        inv = inv + both[:c]
        pw = both[c:]
    return inv + dot1(inv, _stack(pw, c), NN)


@jax.custom_vjp
def neumann_inv(l):
    return _neumann(l)


def _neumann_fwd(l):
    inv = _neumann(l)
    return inv, inv


def _neumann_bwd(inv, g):
    c = CHUNK
    bd_t = _stack(inv, c).T
    inv_t = bd_t[0:c]
    for h in range(1, inv.shape[1] // c):
        inv_t = inv_t + bd_t[h * c:(h + 1) * c]
    return (dot1(dot1(inv_t, _stack(g, c), NN), bd_t, NN),)


neumann_inv.defvjp(_neumann_fwd, _neumann_bwd)


def _cumsum_raw(x, dims):
    c = x.shape[0]
    tt = lax.broadcasted_iota(jnp.int32, (c, c), 0)
    ii = lax.broadcasted_iota(jnp.int32, (c, c), 1)
    tri = (ii <= tt).astype(BF16)
    hi = x.astype(BF16)
    rest = x - hi.astype(F32)
    mid = rest.astype(BF16)
    lo = (rest - mid.astype(F32)).astype(BF16)
    return _dot(tri, hi, dims) + (_dot(tri, mid, dims) + _dot(tri, lo, dims))


@jax.custom_vjp
def cumsum_rows(x):
    return _cumsum_raw(x, NN)


def _cumsum_fwd(x):
    return _cumsum_raw(x, NN), None


def _cumsum_bwd(_, g):
    return (_cumsum_raw(g, TN),)


cumsum_rows.defvjp(_cumsum_fwd, _cumsum_bwd)


def _rwkv_chunk(s0, r, lw, k, v, kk, a):
    heads = r.shape[1] // HEAD_DIM
    c, hc = CHUNK, heads * CHUNK
    t = lax.broadcasted_iota(jnp.int32, (c, hc), 0)
    i = lax.broadcasted_iota(jnp.int32, (c, hc), 1) % c
    strict, incl = i < t, i <= t
    stack = lambda x: _stack(x, HEAD_DIM)
    ba = lax.broadcasted_iota(jnp.int32, s0.shape, 0) // HEAD_DIM
    bb = lax.broadcasted_iota(jnp.int32, s0.shape, 1) // HEAD_DIM
    blocks = (ba == bb).astype(F32)

    cum = cumsum_rows(lw)
    cum_end = jnp.sum(lw, axis=0, keepdims=True)
    beta = kk * a
    al = -kk * jnp.exp(cum - lw)
    p_inv = jnp.exp(-cum)
    be, kb, rb = beta * p_inv, k * p_inv, r * jnp.exp(cum)
    ar = jnp.concatenate([al, rb], axis=0)
    sv = stack(v)
    l_all = dot1(ar, jnp.concatenate([stack(be), stack(kb)], axis=0), NT)
    l_ab = jnp.where(strict, l_all[:c, :hc], 0.0)
    l_ak = jnp.where(strict, l_all[:c, hc:], 0.0)
    l_rb = jnp.where(incl, l_all[c:, :hc], 0.0)
    l_rk = jnp.where(incl, l_all[c:, hc:], 0.0)
    inv = neumann_inv(l_ab)
    from_s0 = dot1(ar, s0, NT)
    from_v = dot1(jnp.concatenate([l_ak, l_rk], axis=0), sv, NN)
    u = dot1(inv, stack(from_s0[:c] + from_v[:c]), NN)
    o = from_s0[c:] + from_v[c:] + dot1(l_rb, stack(u), NN)
    to_end = jnp.exp(cum_end - cum)
    s1 = s0 * jnp.exp(cum_end) + blocks * dot1(
        jnp.concatenate([u, v], axis=0), jnp.concatenate([beta * to_end, k * to_end], axis=0), TN)
    return o, s1


def call_with_comm(plan, middle_step, body, grid, name, in_specs, out_specs, out_shape, scratch_shapes, operands):
    n_in, n_out, n_scr = len(in_specs), len(out_specs), len(scratch_shapes)
    p_in, p_out = len(plan.ins), len(plan.out_shape)

    def fused(*refs):
        refs = list(refs)
        ins, refs = refs[:n_in], refs[n_in:]
        p_ins, refs = refs[:p_in], refs[p_in:]
        outs, refs = refs[:n_out], refs[n_out:]
        p_outs, refs = refs[:p_out], refs[p_out:]
        scr, p_sems = refs[:n_scr], refs[n_scr:]
        start, middle, finish = plan.stages(p_ins, p_outs, p_sems)
        step = pl.program_id(0)
        pl.when(step == 0)(start)
        body(*ins, *outs, *scr)
        pl.when(step == middle_step)(middle)
        pl.when(step == grid[0] - 1)(finish)

    any_spec = pl.BlockSpec(memory_space=pl.ANY)
    res = pl.pallas_call(
        fused, grid=grid, name=name,
        in_specs=list(in_specs) + [any_spec] * p_in, out_specs=list(out_specs) + [any_spec] * p_out,
        out_shape=list(out_shape) + list(plan.out_shape), scratch_shapes=list(scratch_shapes) + list(plan.scratch),
        compiler_params=_params(("arbitrary",)),
    )(*operands, *plan.ins)
    return res[:n_out], res[n_out:]


def _by_group(ref):
    return jnp.stack([ref[:, g * SCAN_WIDTH:(g + 1) * SCAN_WIDTH] for g in range(SCAN_GROUPS)])


def _store_groups(ref, val):
    for g in range(SCAN_GROUPS):
        ref[:, g * SCAN_WIDTH:(g + 1) * SCAN_WIDTH] = val[g]


def mixer_fwd(proj, rel_bias, bucket, sinks, r, lw, k, v, kk, a, plan):
    block = lambda c: jnp.minimum(c, N_BLOCKS - 1)

    def body(q_ref, kp_ref, kc_ref, vp_ref, vc_ref, rb_ref, bk_ref, sk_ref, r_ref, lw_ref, k_ref, v_ref, kk_ref,
             a_ref, attn_ref, o_ref, st_ref, bias_ref, s_ref):
        c = pl.program_id(0)

        @pl.when(c == 0)
        def _():
            _build_bias(rb_ref, bk_ref[...], bias_ref)
            s_ref[...] = jnp.zeros_like(s_ref)

        @pl.when(c < N_BLOCKS)
        def _():
            attn_ref[...] = _attn_block(q_ref[...], kp_ref[...], kc_ref[...], vp_ref[...], vc_ref[...],
                                        tuple(bias_ref[h] for h in range(N_HEADS)), sk_ref[...], c)

        s0 = s_ref[...]
        st_ref[0] = s0
        o, s1 = jax.vmap(_rwkv_chunk)(s0, *(_by_group(ref) for ref in (r_ref, lw_ref, k_ref, v_ref, kk_ref, a_ref)))
        _store_groups(o_ref, o)
        s_ref[...] = s1

    tb = pl.BlockSpec((CHUNK, D_RWKV), lambda c: (c, 0))
    state = (SCAN_GROUPS, SCAN_WIDTH, SCAN_WIDTH)
    return call_with_comm(
        plan, 3 * N_CHUNKS // 4, body, (N_CHUNKS,), "mixer_fwd", _attn_in_specs(block) + [tb] * 6,
        [pl.BlockSpec((BLOCK, D_ATTN), lambda c: (block(c), 0)), tb,
         pl.BlockSpec((1,) + state, lambda c: (c, 0, 0, 0))],
        [_sds((SEQ, D_ATTN)), _sds((SEQ, D_RWKV)), _sds((N_CHUNKS,) + state)],
        [pltpu.VMEM((N_HEADS, BLOCK, 2 * BLOCK), F32), pltpu.VMEM(state, F32)],
        (proj, proj, proj, proj, proj, rel_bias, bucket, sinks, r, lw, k, v, kk, a))


def rwkv_scan_bwd(r, lw, k, v, kk, a, states, d_o, plan):
    def body(r_ref, lw_ref, k_ref, v_ref, kk_ref, a_ref, st_ref, do_ref,
             dr_ref, dlw_ref, dk_ref, dv_ref, dkk_ref, da_ref, ds_ref):
        @pl.when(pl.program_id(0) == 0)
        def _():
            ds_ref[...] = jnp.zeros_like(ds_ref)

        _, vjp = jax.vjp(jax.vmap(_rwkv_chunk), st_ref[0],
                         *(_by_group(ref) for ref in (r_ref, lw_ref, k_ref, v_ref, kk_ref, a_ref)))
        grads = vjp((_by_group(do_ref), ds_ref[...]))
        ds_ref[...] = grads[0]
        for ref, val in zip((dr_ref, dlw_ref, dk_ref, dv_ref, dkk_ref, da_ref), grads[1:]):
            _store_groups(ref, val)

    last = N_CHUNKS - 1
    tb = pl.BlockSpec((CHUNK, D_RWKV), lambda c: (last - c, 0))
    state = (SCAN_GROUPS, SCAN_WIDTH, SCAN_WIDTH)
    return call_with_comm(
        plan, N_CHUNKS // 4, body, (N_CHUNKS,), "rwkv_scan_bwd",
        [tb] * 6 + [pl.BlockSpec((1,) + state, lambda c: (last - c, 0, 0, 0)), tb], [tb] * 6,
        [_sds((SEQ, D_RWKV))] * 6, [pltpu.VMEM(state, F32)], (r, lw, k, v, kk, a, states, d_o))


def _ffn_mid(ug, uv, cg, cv, bg, bv):
    conv_g = bg + cg[0] * shift_rows(ug, 2) + cg[1] * shift_rows(ug, 1) + cg[2] * ug
    conv_v = bv + cv[0] * shift_rows(uv, 2) + cv[1] * shift_rows(uv, 1) + cv[2] * uv
    return jax.nn.gelu(conv_g, approximate=True) * conv_v


def _conv_rows(ref):
    return tuple(ref[0, j:j + 1, :] for j in range(3))


def _ffn_specs(tile):
    per = D_MODEL // tile
    half = N_DEV // 2
    w_g = pl.BlockSpec((1, D_MODEL, tile), lambda t: (t // per, 0, t % per))
    w_v = pl.BlockSpec((1, D_MODEL, tile), lambda t: (half + t // per, 0, t % per))
    c_g = pl.BlockSpec((1, 3, tile), lambda t: (t // per, 0, t % per))
    c_v = pl.BlockSpec((1, 3, tile), lambda t: (half + t // per, 0, t % per))
    b_g = pl.BlockSpec((1, tile), lambda t: (0, t))
    b_v = pl.BlockSpec((1, tile), lambda t: (0, D_FF // tile + t))
    w_d = pl.BlockSpec((tile, D_MODEL), lambda t: (t, 0))
    return w_g, w_v, c_g, c_v, b_g, b_v, w_d


def ffn_fwd(h2, w_up, conv_w, conv_b, w_down):
    def body(h_ref, wg_ref, wv_ref, cg_ref, cv_ref, bg_ref, bv_ref, wd_ref, f_ref):
        @pl.when(pl.program_id(0) == 0)
        def _():
            f_ref[...] = jnp.zeros_like(f_ref)

        h = h_ref[...]
        act = _ffn_mid(_dot(h, wg_ref[0], NN), _dot(h, wv_ref[0], NN), _conv_rows(cg_ref), _conv_rows(cv_ref),
                       bg_ref[...], bv_ref[...])
        f_ref[...] += _dot(act.astype(BF16), wd_ref[...], NN)

    full = pl.BlockSpec((SEQ, D_MODEL), lambda t: (0, 0))
    return pl.pallas_call(
        body, grid=(D_FF // FF_TILE,), name="ffn_fwd",
        in_specs=[full, *_ffn_specs(FF_TILE)],
        out_specs=full, out_shape=_sds((SEQ, D_MODEL)),
        compiler_params=_params(("arbitrary",)),
    )(h2, w_up, w_up, conv_w, conv_w, conv_b, conv_b, w_down)


def ffn_bwd_mid(h2, w_up, conv_w, conv_b, w_down, df):
    tile, rows, halo = FF_TILE, FF_ROW_CHUNK, FF_HALO
    ext = rows + 2 * halo

    def body(h_hbm, wg_ref, wv_ref, cg_ref, cv_ref, bg_ref, bv_ref, wd_ref, df_hbm,
             dug_ref, duv_ref, dcg_ref, dcv_ref, dbg_ref, dbv_ref, dwd_ref,
             h_ref, df_ref, ug_ref, uv_ref, da_ref, act_ref):
        @pl.when(pl.program_id(0) == 0)
        def _():
            pltpu.sync_copy(h_hbm, h_ref)
            pltpu.sync_copy(df_hbm, df_ref)
            for ref in (ug_ref, uv_ref, da_ref):
                ref[0:halo, :] = jnp.zeros((halo, tile), F32)
                ref[halo + SEQ:, :] = jnp.zeros((halo, tile), F32)

        h, df_b = h_ref[...], df_ref[...]
        ug_ref[halo:halo + SEQ, :] = _dot(h, wg_ref[0], NN)
        uv_ref[halo:halo + SEQ, :] = _dot(h, wv_ref[0], NN)
        da_ref[halo:halo + SEQ, :] = _dot(df_b, wd_ref[...], NT)
        cg, cv, bg, bv = _conv_rows(cg_ref), _conv_rows(cv_ref), bg_ref[...], bv_ref[...]
        down = lambda x, n: pltpu.roll(x, n, 0)
        up = lambda x, n: pltpu.roll(x, ext - n, 0)
        mid = slice(halo, halo + rows)

        def chunk(i, sums):
            r0 = pl.multiple_of(i * rows, rows)
            window = pl.ds(r0, ext)
            ug, uv, da = ug_ref[window, :], uv_ref[window, :], da_ref[window, :]
            ug1, ug2, uv1, uv2 = down(ug, 1), down(ug, 2), down(uv, 1), down(uv, 2)
            conv_g = bg + cg[0] * ug2 + cg[1] * ug1 + cg[2] * ug
            conv_v = bv + cv[0] * uv2 + cv[1] * uv1 + cv[2] * uv
            act, vjp = jax.vjp(lambda a, b: jax.nn.gelu(a, approximate=True) * b, conv_g, conv_v)
            dcg, dcv = vjp(da)
            dug = cg[2] * dcg + cg[1] * up(dcg, 1) + cg[0] * up(dcg, 2)
            duv = cv[2] * dcv + cv[1] * up(dcv, 1) + cv[0] * up(dcv, 2)
            out = pl.ds(r0, rows)
            act_ref[out, :] = act[mid].astype(BF16)
            dug_ref[out, :] = dug[mid].astype(BF16)
            duv_ref[out, :] = duv[mid].astype(BF16)
            col = lambda x: jnp.sum(x[mid], axis=0, keepdims=True)
            new = (col(dcg * ug2), col(dcg * ug1), col(dcg * ug), col(dcv * uv2), col(dcv * uv1), col(dcv * uv),
                   col(dcg), col(dcv))
            return tuple(s + n for s, n in zip(sums, new))

        zero = jnp.zeros((1, tile), F32)
        sums = lax.fori_loop(0, SEQ // rows, chunk, (zero,) * 8)
        for j in range(3):
            dcg_ref[0, j:j + 1, :] = sums[j]
            dcv_ref[0, j:j + 1, :] = sums[3 + j]
        dbg_ref[...] = sums[6]
        dbv_ref[...] = sums[7]
        dwd_ref[...] = _dot(act_ref[...], df_b, TN).astype(BF16)

    hbm = pl.BlockSpec(memory_space=pl.ANY)
    w_g, w_v, c_g, c_v, b_g, b_v, w_d = _ffn_specs(tile)
    col = pl.BlockSpec((SEQ, tile), lambda t: (0, t))
    padded = pltpu.VMEM((SEQ + 2 * halo, tile), F32)
    return pl.pallas_call(
        body, grid=(D_FF // tile,), name="ffn_bwd_mid",
        in_specs=[hbm, w_g, w_v, c_g, c_v, b_g, b_v, w_d, hbm],
        out_specs=[col, col, c_g, c_v, b_g, b_v, w_d],
        out_shape=[_sds((SEQ, D_FF), BF16), _sds((SEQ, D_FF), BF16), _sds((N_DEV, 3, D_MODEL)),
                   _sds((N_DEV, 3, D_MODEL)), _sds((1, 2 * D_FF)), _sds((1, 2 * D_FF)), _sds((D_FF, D_MODEL), BF16)],
        scratch_shapes=[pltpu.VMEM((SEQ, D_MODEL), BF16), pltpu.VMEM((SEQ, D_MODEL), BF16), padded, padded, padded,
                        pltpu.VMEM((SEQ, tile), BF16)],
        compiler_params=_params(("arbitrary",)),
    )(h2, w_up, w_up, conv_w, conv_w, conv_b, conv_b, w_down, df)


def ffn_bwd_up(h2, w_up, dug, duv):
    tile = FF_TILE
    per = D_MODEL // tile

    def body(h_hbm, wg_ref, wv_ref, dug_ref, duv_ref, dh_hbm, dup_hbm, h_ref, dh_ref, dwg_ref, dwv_ref, sem, up_sems):
        t = pl.program_id(0)

        @pl.when(t == 0)
        def _():
            pltpu.sync_copy(h_hbm, h_ref)
            dh_ref[...] = jnp.zeros_like(dh_ref)

        h, dug_b, duv_b = h_ref[...], dug_ref[...], duv_ref[...]
        cols = pl.ds(pl.multiple_of((t % per) * tile, tile), tile)
        to_gate = pltpu.make_async_copy(dwg_ref, dup_hbm.at[t // per, :, cols], up_sems.at[0])
        to_value = pltpu.make_async_copy(dwv_ref, dup_hbm.at[N_DEV // 2 + t // per, :, cols], up_sems.at[1])
        dwg_ref[...] = _dot(h, dug_b, TN).astype(BF16)
        to_gate.start()
        dwv_ref[...] = _dot(h, duv_b, TN).astype(BF16)
        to_value.start()
        dh_ref[...] += _dot(jnp.concatenate([dug_b, duv_b], axis=1),
                            jnp.concatenate([wg_ref[0], wv_ref[0]], axis=1), NT)
        to_gate.wait()
        to_value.wait()

        @pl.when(t == D_FF // tile - 1)
        def _():
            cp = pltpu.make_async_copy(dh_ref, dh_hbm, sem)
            cp.start()
            cp.wait()

    hbm = pl.BlockSpec(memory_space=pl.ANY)
    w_g, w_v = _ffn_specs(tile)[:2]
    col = pl.BlockSpec((SEQ, tile), lambda t: (0, t))
    return pl.pallas_call(
        body, grid=(D_FF // tile,), name="ffn_bwd_up",
        in_specs=[hbm, w_g, w_v, col, col], out_specs=[hbm, hbm],
        out_shape=[_sds((SEQ, D_MODEL)), _sds((N_DEV, D_MODEL, D_MODEL), BF16)],
        scratch_shapes=[pltpu.VMEM((SEQ, D_MODEL), BF16), pltpu.VMEM((SEQ, D_MODEL), F32),
                        pltpu.VMEM((D_MODEL, tile), BF16), pltpu.VMEM((D_MODEL, tile), BF16),
                        pltpu.SemaphoreType.DMA, pltpu.SemaphoreType.DMA((2,))],
        compiler_params=_params(("arbitrary",)),
    )(h2, w_up, w_up, dug, duv)


def loss_head(x1, f, target, n_post):
    def tile_loss(x1_t, f_t, g, tgt):
        err = x1_t + _rms(f_t, g) - tgt
        return 0.5 * jnp.sum(jnp.mean(err * err, axis=-1))

    def body(x_ref, f_ref, t_ref, g_ref, dx_ref, df_ref, dg_ref, loss_ref):
        val, (dx, df, dg) = jax.value_and_grad(tile_loss, argnums=(0, 1, 2))(
            x_ref[...], f_ref[...], g_ref[...], t_ref[...])
        dx_ref[...] = dx
        df_ref[...] = df.astype(BF16)

        @pl.when(pl.program_id(0) == 0)
        def _():
            dg_ref[...] = jnp.zeros_like(dg_ref)
            loss_ref[...] = jnp.zeros_like(loss_ref)

        dg_ref[...] += dg
        loss_ref[...] += jnp.full((1, LANES), val, F32)

    tile = pl.BlockSpec((TOK_TILE, D_MODEL), lambda i: (i, 0))
    vec = pl.BlockSpec((1, D_MODEL), lambda i: (0, 0))
    return pl.pallas_call(
        body, grid=(SEQ // TOK_TILE,), name="loss_head",
        in_specs=[tile, tile, tile, vec],
        out_specs=[tile, tile, vec, pl.BlockSpec((1, LANES), lambda i: (0, 0))],
        out_shape=[_sds((SEQ, D_MODEL)), _sds((SEQ, D_MODEL), BF16), _sds((1, D_MODEL)), _sds((1, LANES))],
        compiler_params=_params(("arbitrary",)),
    )(x1, f, target, n_post)


def _mesh_pos():
    return lax.axis_index("x"), lax.axis_index("y"), lax.axis_index("c")


def _flip(pos, rel):
    x, y, c = pos
    return (1 - x if rel & 4 else x, 1 - y if rel & 2 else y, 1 - c if rel & 1 else c)


def _slot(pos):
    x, y, c = pos
    return 4 * x + 2 * y + c


def cast_bf16(w, rows):
    def body(w_ref, o_ref):
        o_ref[...] = w_ref[...].astype(BF16)

    spec = pl.BlockSpec((rows, w.shape[1]), lambda i: (i, 0))
    return pl.pallas_call(body, grid=(w.shape[0] // rows,), name="cast_bf16_%dx%d" % w.shape,
                          in_specs=[spec], out_specs=spec, out_shape=_sds(w.shape, BF16),
                          compiler_params=_params(("arbitrary",)))(w)


class CommPlan:
    def __init__(self, ins, out_shape, scratch, stages):
        self.ins, self.out_shape, self.scratch, self.stages = ins, out_shape, scratch, stages


def run_comm(name, plan):
    n_in, n_out = len(plan.ins), len(plan.out_shape)

    def body(*refs):
        for stage in plan.stages(refs[:n_in], refs[n_in:n_in + n_out], refs[n_in + n_out:]):
            stage()

    any_spec = pl.BlockSpec(memory_space=pl.ANY)
    return pl.pallas_call(
        body, name=name, in_specs=[any_spec] * len(plan.ins), out_specs=[any_spec] * len(plan.out_shape),
        out_shape=plan.out_shape, scratch_shapes=plan.scratch)(*plan.ins)


def gather_plan(shards):
    n = len(shards)

    def stages(srcs, outs, sems):
        send_sems, recv_sems, local_sems = sems

        def places():
            me = _mesh_pos()
            return me, _flip(me, 1), [_flip(me, 2), _flip(me, 4), _flip(me, 6)]

        def copy(a, k, block, to, src=None):
            dst = outs[a].at[_slot(block)]
            return pltpu.make_async_remote_copy(
                src_ref=dst if src is None else src, dst_ref=dst,
                send_sem=send_sems.at[7 * a + k], recv_sem=recv_sems.at[7 * a + k],
                device_id=to, device_id_type=pl.DeviceIdType.MESH)

        def local(a, me):
            return pltpu.make_async_copy(srcs[a], outs[a].at[_slot(me)], local_sems.at[a])

        def own(a, me, sibling, chips):
            return [copy(a, 0, me, sibling, src=srcs[a])] + [
                copy(a, 1 + j, me, chip, src=srcs[a]) for j, chip in enumerate(chips)]

        def start():
            me, sibling, chips = places()
            for a in range(n):
                local(a, me).start()
                for cp in own(a, me, sibling, chips):
                    cp.start()

        def forward():
            me, sibling, chips = places()
            for j, chip in enumerate(chips):
                for a in range(n):
                    copy(a, 1 + j, chip, me).wait_recv()
                    copy(a, 4 + j, chip, sibling).start()

        def finish():
            me, sibling, chips = places()
            for a in range(n):
                copy(a, 0, sibling, me).wait_recv()
                for j, chip in enumerate(chips):
                    copy(a, 4 + j, _flip(chip, 1), me).wait_recv()
            for a in range(n):
                for cp in own(a, me, sibling, chips):
                    cp.wait_send()
                for j, chip in enumerate(chips):
                    copy(a, 4 + j, chip, sibling).wait_send()
                local(a, me).wait()

        return start, forward, finish

    return CommPlan(list(shards), [_sds((N_DEV,) + s.shape, s.dtype) for s in shards],
                    [pltpu.SemaphoreType.DMA((7 * n,)), pltpu.SemaphoreType.DMA((7 * n,)),
                     pltpu.SemaphoreType.DMA((n,))], stages)


def exchange_plan(parts, replicated, rels, members, index, member_axis, own_copy):
    n, nr = len(parts), len(rels)
    pick_index = (slice(None),) * member_axis + (0,)
    subs = [1 if (r or member_axis == 0) else p.shape[0] for p, r in zip(parts, replicated)]
    first = [sum(subs[:a]) for a in range(n)]
    total = sum(subs)

    def stages(srcs, outs, sems):
        send_sems, recv_sems, local_sems = sems

        def src(a, s, pos):
            if replicated[a]:
                return srcs[a]
            return srcs[a].at[index(pos)] if member_axis == 0 else srcs[a].at[s, index(pos)]

        def dst(a, s, pos):
            block = outs[a].at[index(pos)]
            return block if (replicated[a] or member_axis == 0) else block.at[s]

        def copy(a, s, j, me, src_pos, dst_pos):
            sem = nr * (first[a] + s) + j
            return pltpu.make_async_remote_copy(
                src_ref=src(a, s, src_pos), dst_ref=dst(a, s, dst_pos),
                send_sem=send_sems.at[sem], recv_sem=recv_sems.at[sem],
                device_id=_flip(me, rels[j]), device_id_type=pl.DeviceIdType.MESH)

        pieces = [(a, s) for a in range(n) for s in range(subs[a])]

        def local(a, s, me):
            return pltpu.make_async_copy(src(a, s, me), dst(a, s, me), local_sems.at[first[a] + s])

        def sends(me):
            return [copy(a, s, j, me, _flip(me, rels[j]), me) for j in range(nr) for a, s in pieces]

        own = pieces if own_copy else []

        def start():
            me = _mesh_pos()
            for cp in sends(me) + [local(a, s, me) for a, s in own]:
                cp.start()

        def middle():
            pass

        def finish():
            me = _mesh_pos()
            for j in range(nr):
                for a, s in pieces:
                    copy(a, s, j, me, me, _flip(me, rels[j])).wait_recv()
            for cp in sends(me):
                cp.wait_send()
            for a, s in own:
                local(a, s, me).wait()

        return start, middle, finish

    shapes = [p.shape if r else jax.eval_shape(lambda t: t[pick_index], p).shape for p, r in zip(parts, replicated)]
    return CommPlan(list(parts), [_sds((members,) + s, p.dtype) for s, p in zip(shapes, parts)],
                    [pltpu.SemaphoreType.DMA((nr * total,)), pltpu.SemaphoreType.DMA((nr * total,)),
                     pltpu.SemaphoreType.DMA((total,))], stages)


def pair_plan(parts, replicated):
    return exchange_plan(parts, replicated, [1], 2, lambda pos: pos[2], 1, False)


def chip_plan(parts, replicated):
    return exchange_plan(parts, replicated, [2, 4, 6], 4, lambda pos: 2 * pos[0] + pos[1], 0, True)


def add_pair(name, mine, swapped, out_dtype, rows):
    def body(m_ref, s_ref, o_ref):
        own = m_ref[0, 0] if mine.ndim == 4 else m_ref[0]
        o_ref[0] = (own.astype(F32) + s_ref[0, 0].astype(F32)).astype(o_ref.dtype)

    _, n, r, c = swapped.shape
    core = lambda: lax.axis_index("c")
    if mine.ndim == 4:
        mine_spec = pl.BlockSpec((1, 1, rows, c), lambda i, j: (i, core(), j, 0))
    else:
        mine_spec = pl.BlockSpec((1, rows, c), lambda i, j: (i, j, 0))
    return pl.pallas_call(
        body, grid=(n, r // rows), name=name,
        in_specs=[mine_spec, pl.BlockSpec((1, 1, rows, c), lambda i, j: (1 - core(), i, j, 0))],
        out_specs=pl.BlockSpec((1, rows, c), lambda i, j: (i, j, 0)),
        out_shape=_sds((n, r, c), out_dtype),
        compiler_params=_params(("arbitrary", "arbitrary")),
    )(mine, swapped)


def add_pair_small(mines, swappeds):
    n = len(mines)
    halves = [m.ndim == s.ndim for m, s in zip(mines, swappeds)]

    def body(*refs):
        c = lax.axis_index("c")
        for i in range(n):
            m_ref, s_ref, o_ref = refs[i], refs[n + i], refs[2 * n + i]
            o_ref[...] = (m_ref[:, c] if halves[i] else m_ref[...]) + s_ref[1 - c]

    vmem = pl.BlockSpec(memory_space=pltpu.VMEM)
    return pl.pallas_call(
        body, name="pair_add_small", in_specs=[vmem] * (2 * n), out_specs=[vmem] * n,
        out_shape=[_sds(s.shape[1:]) for s in swappeds], compiler_params=_params(),
    )(*mines, *swappeds)


def _adamw_math(w, g, m, v):
    nm = ADAM_B1 * m + (1.0 - ADAM_B1) * g
    nv = ADAM_B2 * v + (1.0 - ADAM_B2) * (g * g)
    m_hat = nm / (1.0 - ADAM_B1 ** ADAM_STEP)
    v_hat = nv / (1.0 - ADAM_B2 ** ADAM_STEP)
    return -ADAM_LR * (m_hat / (jnp.sqrt(v_hat) + ADAM_EPS) + ADAM_WD * w), nm, nv


def adamw_small(ws, parts, ms, vs):
    n = len(ws)

    def body(*refs):
        for i in range(n):
            w_ref, p_ref, m_ref, v_ref = (refs[k * n + i] for k in range(4))
            g = p_ref[0]
            for j in range(1, p_ref.shape[0]):
                g = g + p_ref[j]
            delta, nm, nv = _adamw_math(w_ref[...], g, m_ref[...], v_ref[...])
            for k, val in enumerate((g, delta, nm, nv)):
                refs[(4 + k) * n + i][...] = val

    vmem = pl.BlockSpec(memory_space=pltpu.VMEM)
    outs = pl.pallas_call(
        body, name="adamw_small", in_specs=[vmem] * (4 * n), out_specs=[vmem] * (4 * n),
        out_shape=[_sds(w.shape) for w in ws] * 4, compiler_params=_params(),
    )(*ws, *parts, *ms, *vs)
    return [outs[k * n:(k + 1) * n] for k in range(4)]

def adamw(name, w, parts, m, v, rows, plan=None):
    n_parts = parts.shape[0]

    def body(w_ref, p_ref, m_ref, v_ref, g_ref, d_ref, nm_ref, nv_ref):
        g = p_ref[0].astype(F32)
        for j in range(1, n_parts):
            g = g + p_ref[j].astype(F32)
        g_ref[...] = g
        d_ref[...], nm_ref[...], nv_ref[...] = _adamw_math(w_ref[...], g, m_ref[...], v_ref[...])

    cols = w.shape[1]
    spec = pl.BlockSpec((rows, cols), lambda i: (i, 0))
    grid = (w.shape[0] // rows,)
    in_specs = [spec, pl.BlockSpec((n_parts, rows, cols), lambda i: (0, i, 0)), spec, spec]
    if plan is not None:
        return call_with_comm(plan, 0, body, grid, name, in_specs, [spec] * 4, [_sds(w.shape)] * 4, [],
                              (w, parts, m, v))
    return pl.pallas_call(
        body, grid=grid, name=name, in_specs=in_specs, out_specs=[spec] * 4, out_shape=[_sds(w.shape)] * 4,
        compiler_params=_params(("arbitrary",)),
    )(w, parts, m, v)


def _to_slots(full, per):
    return full.reshape(full.shape[0], N_DEV, per).transpose(1, 0, 2)


def _from_slots(slots):
    return slots.transpose(1, 0, 2).reshape(slots.shape[1], -1)


def kernel(x, norm_mix_pre, norm_mix_post, norm_ffn_pre, norm_ffn_post, w_in, rel_bias, sinks, rwkv_shift_mix, w0, w_decay_up, a0, w_iclr_up, w_gate_up, k_k, k_a, r_k, ln_x_g, ln_x_b, w_out, w_ffn_up, conv_w, conv_b, w_ffn_down, loss_target, m_norm_mix_pre, m_norm_mix_post, m_norm_ffn_pre, m_norm_ffn_post, m_w_in, m_rel_bias, m_sinks, m_rwkv_shift_mix, m_w0, m_w_decay_up, m_a0, m_w_iclr_up, m_w_gate_up, m_k_k, m_k_a, m_r_k, m_ln_x_g, m_ln_x_b, m_w_out, m_w_ffn_up, m_conv_w, m_conv_b, m_w_ffn_down, v_norm_mix_pre, v_norm_mix_post, v_norm_ffn_pre, v_norm_ffn_post, v_w_in, v_rel_bias, v_sinks, v_rwkv_shift_mix, v_w0, v_w_decay_up, v_a0, v_w_iclr_up, v_w_gate_up, v_k_k, v_k_a, v_r_k, v_ln_x_g, v_ln_x_b, v_w_out, v_w_ffn_up, v_conv_w, v_conv_b, v_w_ffn_down):
    x2 = x[0]
    target = loss_target[0]

    in_gather = gather_plan([cast_bf16(w_in[0], 256)])
    mixer_gather = gather_plan([cast_bf16(w_out[0], 128), w_decay_up[0], w_iclr_up[0], w_gate_up[0], conv_w[0]])
    ffn_gather = gather_plan([cast_bf16(w_ffn_up[0], 256), cast_bf16(w_ffn_down[0], 256)])
    mix_ext = jnp.concatenate([jnp.zeros((1, D_QKV), F32), rwkv_shift_mix], axis=1)
    r_k_row = r_k.reshape(1, D_RWKV)
    bucket = _bucket_table()

    (h1,), (g_in,) = tok_fwd("rms_mix_pre", rms_tile, [x2], [norm_mix_pre], [], [D_MODEL], [BF16], plan=in_gather)
    w_in_b = _from_slots(g_in)
    (proj, ps), (g_out, g_decay, g_iclr, g_gate, g_conv) = in_proj_fwd(h1, w_in_b, mix_ext, mixer_gather)
    w_out_b = g_out.reshape(D_MODEL, D_MODEL)
    lora = jnp.zeros((HEAD_DIM, D_RWKV), F32)
    wd_pad = jnp.concatenate([_from_slots(g_decay), lora], axis=0)
    wi_pad = jnp.concatenate([lora, _from_slots(g_iclr)], axis=0)
    wg_full = _from_slots(g_gate)
    pre_params = [w0, wd_pad, a0, wi_pad, wg_full, k_k, k_a]
    r_, lw_, k2_, v_, kk_, a_, gate_ = tok_fwd("rwkv_pre", rwkv_pre_tile, [ps], pre_params, [],
                                               [D_RWKV] * 7, [F32] * 7)
    (attn, o_, states), (g_up, g_down) = mixer_fwd(proj, rel_bias, bucket, sinks, r_, lw_, k2_, v_, kk_, a_,
                                                   ffn_gather)
    w_down_b = g_down.reshape(D_FF, D_MODEL)
    mix_tiles = [o_, r_, k2_, v_, gate_, attn, x2]
    mix_params = [w_out_b, norm_mix_post, ln_x_g, ln_x_b, r_k_row, norm_ffn_pre]
    x1, h2 = tok_fwd("mix_out", mix_out_tile, mix_tiles, mix_params, [(D_MODEL, D_MODEL)], [D_MODEL, D_MODEL],
                     [F32, BF16])
    f = ffn_fwd(h2, g_up, g_conv, conv_b, w_down_b)
    dy, df, d_n_ffn_post, loss_row = loss_head(x1, f, target, norm_ffn_post)

    d_ug, d_uv, d_cw_g, d_cw_v, d_cb_g, d_cb_v, d_down = ffn_bwd_mid(h2, g_up, g_conv, conv_b, w_down_b, df)
    dh2, d_up = ffn_bwd_up(h2, g_up, d_ug, d_uv)
    half = N_DEV // 2
    d_cw = jnp.concatenate([d_cw_g[:half], d_cw_v[half:]], axis=0)
    by_pair = lambda slots: slots.reshape((N_DEV // 2, 2) + slots.shape[1:])
    ffn_mine = [by_pair(d_up), by_pair(d_down.reshape(N_DEV, D_FF // N_DEV, D_MODEL))]
    ffn_swapped = run_comm("pair_exchange_ffn", pair_plan(ffn_mine, [False, False]))
    up_exchange = chip_plan([add_pair("pair_add_w_ffn_up", ffn_mine[0], ffn_swapped[0], BF16, 256)], [False])
    down_exchange = chip_plan([add_pair("pair_add_w_ffn_down", ffn_mine[1], ffn_swapped[1], BF16, 256)], [False])
    d_cb = jnp.concatenate([d_cb_g[:, :D_FF], d_cb_v[:, D_FF:]], axis=1)
    ((d_o, d_r1, d_k1, d_v1, d_gate, d_attn, dx_res, d_n_mix_post, d_ln_g, d_ln_b, d_r_k, d_n_ffn_pre, d_w_out),
     (got_down,)) = tok_bwd("mix_out_bwd", mix_out_tile, mix_tiles, mix_params, [(D_MODEL, D_MODEL)], [dy, dh2],
                            [1, 2, 3, 4, 5], plan=down_exchange)
    (d_r2, d_lw, d_k2, d_v2, d_kk, d_a), (got_up,) = rwkv_scan_bwd(
        r_, lw_, k2_, v_, kk_, a_, states, d_o, up_exchange)
    pre_cots = [(d_r1, d_r2), d_lw, (d_k1, d_k2), (d_v1, d_v2), d_kk, d_a, d_gate]
    (d_ps, d_w0, d_wd_pad, d_a0, d_wi_pad, d_wg, d_k_k, d_k_a) = tok_bwd(
        "rwkv_pre_bwd", rwkv_pre_tile, [ps], pre_params, [], pre_cots, [0, 1, 2, 3, 4, 5, 6])
    dq, dkc, dkp, dvc, dvp, d_rel_bias, d_sinks = attn_bwd(proj, rel_bias, bucket, sinks, d_attn)
    zero_blk = jnp.zeros((BLOCK, D_KV), F32)
    dk = dkc + jnp.concatenate([dkp[BLOCK:], zero_blk], axis=0)
    dv = dvc + jnp.concatenate([dvp[BLOCK:], zero_blk], axis=0)
    dpa = jnp.concatenate([dq, dk, dv, d_ps], axis=1)
    dh1, d_w_in, d_mix_ext = in_proj_bwd(h1, w_in_b, mix_ext, proj, dpa)
    grad_x2, d_n_mix_pre = tok_bwd("rms_mix_pre_bwd", rms_tile, [x2], [norm_mix_pre], [], [dh1], [0], {0: dx_res})
    grad_x = grad_x2[None]

    small_rep = [d_n_mix_pre, d_n_mix_post, d_n_ffn_pre, d_n_ffn_post, d_rel_bias, d_sinks,
                 d_mix_ext[:, D_QKV:], d_w0, d_a0, d_k_k, d_k_a, d_r_k.reshape(r_k.shape), d_ln_g, d_ln_b, d_cb]
    rep_w = [norm_mix_pre, norm_mix_post, norm_ffn_pre, norm_ffn_post, rel_bias, sinks, rwkv_shift_mix,
             w0, a0, k_k, k_a, r_k, ln_x_g, ln_x_b, conv_b]
    rep_m = [m_norm_mix_pre, m_norm_mix_post, m_norm_ffn_pre, m_norm_ffn_post, m_rel_bias, m_sinks,
             m_rwkv_shift_mix, m_w0, m_a0, m_k_k, m_k_a, m_r_k, m_ln_x_g, m_ln_x_b, m_conv_b]
    rep_v = [v_norm_mix_pre, v_norm_mix_post, v_norm_ffn_pre, v_norm_ffn_post, v_rel_bias, v_sinks,
             v_rwkv_shift_mix, v_w0, v_a0, v_k_k, v_k_a, v_r_k, v_ln_x_g, v_ln_x_b, v_conv_b]
    sh_w = [w_decay_up, w_iclr_up, w_gate_up, conv_w]
    sh_m = [m_w_decay_up, m_w_iclr_up, m_w_gate_up, m_conv_w]
    sh_v = [v_w_decay_up, v_w_iclr_up, v_w_gate_up, v_conv_w]
    sh_parts = [_to_slots(d_wd_pad[:HEAD_DIM], HEAD_DIM), _to_slots(d_wi_pad[HEAD_DIM:], HEAD_DIM),
                _to_slots(d_wg, HEAD_DIM), d_cw]
    n_rep, n_sh = len(small_rep), len(sh_parts)
    mine = [by_pair(_to_slots(d_w_in, D_IN // N_DEV)), by_pair(d_w_out.reshape(N_DEV, D_MODEL // N_DEV, D_MODEL)),
            *small_rep, *(by_pair(p) for p in sh_parts), loss_row]
    is_rep = [False, False] + [True] * n_rep + [False] * n_sh + [True]
    adam_down, swapped = adamw("adamw_w_ffn_down", w_ffn_down[0], got_down, m_w_ffn_down[0], v_w_ffn_down[0], 128,
                               pair_plan(mine, is_rep))
    chip_sums = [add_pair("pair_add_w_in", mine[0], swapped[0], BF16, 512),
                 add_pair("pair_add_w_out", mine[1], swapped[1], BF16, 128),
                 *add_pair_small(mine[2:], swapped[2:])]
    adam_up, got = adamw("adamw_w_ffn_up", w_ffn_up[0], got_up, m_w_ffn_up[0], v_w_ffn_up[0], 128,
                         chip_plan(chip_sums, is_rep))

    big = [adamw("adamw_w_in", w_in[0], got[0], m_w_in[0], v_w_in[0], 256),
           adamw("adamw_w_out", w_out[0], got[1], m_w_out[0], v_w_out[0], 128), adam_up, adam_down]
    loss = functools.reduce(jnp.add, [got[-1][q, 0, 0] for q in range(N_DEV // 2)])
    small_w, small_g = rep_w + sh_w, got[2:-1]
    as_grad = lambda arrays: [a.reshape(g.shape[1:]) for a, g in zip(arrays, small_g)]
    small = adamw_small(as_grad(small_w), small_g, as_grad(rep_m + sh_m), as_grad(rep_v + sh_v))
    small = [[a.reshape(w.shape) for a, w in zip(kind, small_w)] for kind in small]

    names = ["norm_mix_pre", "norm_mix_post", "norm_ffn_pre", "norm_ffn_post", "w_in", "rel_bias", "sinks",
             "rwkv_shift_mix", "w0", "w_decay_up", "a0", "w_iclr_up", "w_gate_up", "k_k", "k_a", "r_k",
             "ln_x_g", "ln_x_b", "w_out", "w_ffn_up", "conv_w", "conv_b", "w_ffn_down"]
    small_names = ["norm_mix_pre", "norm_mix_post", "norm_ffn_pre", "norm_ffn_post", "rel_bias", "sinks",
                   "rwkv_shift_mix", "w0", "a0", "k_k", "k_a", "r_k", "ln_x_g", "ln_x_b", "conv_b",
                   "w_decay_up", "w_iclr_up", "w_gate_up", "conv_w"]
    big_names = {"w_in": 0, "w_out": 1, "w_ffn_up": 2, "w_ffn_down": 3}
    outs = []
    for kind in range(4):
        for nm in names:
            if nm in big_names:
                outs.append(big[big_names[nm]][kind][None])
            else:
                outs.append(small[kind][small_names.index(nm)])
    return (loss, grad_x, *outs)
```

```python
import functools
import math

import jax
import jax.numpy as jnp
from jax import lax
from jax.experimental import pallas as pl
from jax.experimental.pallas import tpu as pltpu

F32 = jnp.float32
BF16 = jnp.bfloat16

N_DEV = 8
SEQ = 2048
D_MODEL = 1024
HEAD_DIM = 64
D_ATTN = 512
D_KV = 128
D_RWKV = 512
N_HEADS = 8
RWKV_COLS = 1792
D_QKV = D_ATTN + 2 * D_KV
D_IN = D_QKV + RWKV_COLS
D_FF = 4096
BLOCK = 128
N_BLOCKS = SEQ // BLOCK
N_BUCKETS = 32
MAX_DISTANCE = 128
NORM_EPS = 1e-6
GN_EPS = 64e-5
NEG_INF = -1e30
CHUNK = 64
N_CHUNKS = SEQ // CHUNK
SCAN_GROUPS = 4
SCAN_WIDTH = D_RWKV // SCAN_GROUPS
TOK_TILE = 256
FF_TILE = 256
FF_ROW_CHUNK = 256
FF_HALO = 8
COL_TILE = 256
LANES = 128
VMEM_LIMIT = 56 * 1024 * 1024

ADAM_LR = 0.001
ADAM_B1 = 0.9
ADAM_B2 = 0.999
ADAM_EPS = 1e-08
ADAM_WD = 0.01
ADAM_STEP = 10

NT = ((1,), (1,))
TN = ((0,), (0,))
NN = ((1,), (0,))


def _sds(shape, dtype=F32):
    return jax.ShapeDtypeStruct(shape, dtype)


def _params(sem=None):
    if sem is None:
        return pltpu.CompilerParams(vmem_limit_bytes=VMEM_LIMIT)
    return pltpu.CompilerParams(dimension_semantics=sem, vmem_limit_bytes=VMEM_LIMIT)


def _dot(a, b, dims):
    return lax.dot_general(a, b, (dims, ((), ())), preferred_element_type=F32)


def _split2(x):
    hi = x.astype(BF16)
    return hi, (x - hi.astype(F32)).astype(BF16)


def _dot3_raw(a, b, dims):
    ah, al = _split2(a)
    bh, bl = _split2(b)
    return _dot(ah, bh, dims) + (_dot(al, bh, dims) + _dot(ah, bl, dims))


@functools.partial(jax.custom_vjp, nondiff_argnums=(2,))
def dot3(a, b, dims):
    return _dot3_raw(a, b, dims)


def _dot3_fwd(a, b, dims):
    return _dot3_raw(a, b, dims), (a, b)


def _dot3_bwd(dims, res, g):
    a, b = res
    if dims == NN:
        return dot3(g, b, NT), dot3(a, g, TN)
    if dims == NT:
        return dot3(g, b, NN), dot3(g, a, TN)
    return dot3(b, g, NT), dot3(a, g, NN)


dot3.defvjp(_dot3_fwd, _dot3_bwd)


@functools.partial(jax.custom_vjp, nondiff_argnums=(2,))
def dot1(a, b, dims):
    return _dot(a.astype(BF16), b.astype(BF16), dims)


def _dot1_fwd(a, b, dims):
    return dot1(a, b, dims), (a, b)


def _dot1_bwd(dims, res, g):
    a, b = res
    if dims == NN:
        return dot1(g, b, NT), dot1(a, g, TN)
    if dims == NT:
        return dot1(g, b, NN), dot1(g, a, TN)
    return dot1(b, g, NT), dot1(a, g, NN)


dot1.defvjp(_dot1_fwd, _dot1_bwd)


@jax.custom_vjp
def mm(a, b):
    return _dot(a.astype(BF16), b.astype(BF16), NN)


def _mm_fwd(a, b):
    return mm(a, b), (a, b)


def _mm_bwd(res, g):
    a, b = res
    gb = g.astype(BF16)
    return _dot(gb, b.astype(BF16), NT).astype(a.dtype), _dot(a.astype(BF16), gb, TN).astype(b.dtype)


mm.defvjp(_mm_fwd, _mm_bwd)


@jax.custom_vjp
def mm_nt(a, b):
    return _dot(a.astype(BF16), b.astype(BF16), NT)


def _mm_nt_fwd(a, b):
    return mm_nt(a, b), (a, b)


def _mm_nt_bwd(res, g):
    a, b = res
    gb = g.astype(BF16)
    return _dot(gb, b.astype(BF16), NN).astype(a.dtype), _dot(gb, a.astype(BF16), TN).astype(b.dtype)


mm_nt.defvjp(_mm_nt_fwd, _mm_nt_bwd)


@jax.custom_vjp
def mmw(a, w, wz):
    return _dot(a.astype(BF16), w, NN)


def _mmw_fwd(a, w, wz):
    return mmw(a, w, wz), (a, w)


def _mmw_bwd(res, g):
    a, w = res
    gb = g.astype(BF16)
    return _dot(gb, w, NT).astype(a.dtype), jnp.zeros_like(w), _dot(a.astype(BF16), gb, TN)


mmw.defvjp(_mmw_fwd, _mmw_bwd)


def _shift_raw(x, n):
    rows = x.shape[0]
    rolled = pltpu.roll(x, n % rows, 0)
    idx = lax.broadcasted_iota(jnp.int32, x.shape, 0)
    keep = idx >= n if n > 0 else idx < rows + n
    return jnp.where(keep, rolled, 0.0)


@functools.partial(jax.custom_vjp, nondiff_argnums=(1,))
def shift_rows(x, n):
    return _shift_raw(x, n)


def _shift_fwd(x, n):
    return _shift_raw(x, n), None


def _shift_bwd(n, _, g):
    return (_shift_raw(g, -n),)


shift_rows.defvjp(_shift_fwd, _shift_bwd)


def _head_sum(x, scale):
    a = lax.broadcasted_iota(jnp.int32, (LANES, LANES), 0) // HEAD_DIM
    b = lax.broadcasted_iota(jnp.int32, (LANES, LANES), 1) // HEAD_DIM
    pair = jnp.where(a == b, scale, 0.0).astype(F32)
    return jnp.concatenate([dot3(x[:, i:i + LANES], pair, NN) for i in range(0, x.shape[1], LANES)], axis=1)


def _rms(x, g):
    return x * lax.rsqrt(jnp.mean(x * x, axis=-1, keepdims=True) + NORM_EPS) * g


def _softplus(x):
    return jnp.maximum(x, 0.0) + jnp.log(1.0 + jnp.exp(-jnp.abs(x)))


def _tile_spec(arr, tm):
    return pl.BlockSpec((tm, arr.shape[1]), lambda i: (i, 0))


def _full_spec(arr):
    nd = arr.ndim
    return pl.BlockSpec(arr.shape, lambda i: (0,) * nd)


def tok_fwd(name, fn, tiles, params, zero_shapes, out_widths, out_dtypes, plan=None, tm=TOK_TILE):
    n_t, n_p = len(tiles), len(params)

    def body(*refs):
        t_vals = [r[...] for r in refs[:n_t]]
        p_vals = [r[...] for r in refs[n_t:n_t + n_p]]
        z_vals = [jnp.zeros(s, F32) for s in zero_shapes]
        outs = fn(*t_vals, *p_vals, *z_vals)
        for r, o in zip(refs[n_t + n_p:], outs):
            r[...] = o.astype(r.dtype)

    rows = tiles[0].shape[0]
    steps = rows // tm
    in_specs = [_tile_spec(t, tm) for t in tiles] + [_full_spec(p) for p in params]
    out_specs = [pl.BlockSpec((tm, w), lambda i: (i, 0)) for w in out_widths]
    out_shape = [_sds((rows, w), dt) for w, dt in zip(out_widths, out_dtypes)]
    if plan is not None:
        return call_with_comm(plan, steps // 2, body, (steps,), name, in_specs, out_specs, out_shape, [],
                              (*tiles, *params))
    return pl.pallas_call(
        body, grid=(steps,), name=name, in_specs=in_specs, out_specs=out_specs, out_shape=out_shape,
        compiler_params=_params(("arbitrary",)),
    )(*tiles, *params)


def tok_bwd(name, fn, tiles, params, zero_shapes, cots, diff_params, residuals=(), plan=None, tm=TOK_TILE):
    cot_parts = [c if isinstance(c, tuple) else (c,) for c in cots]
    flat_cots = [a for part in cot_parts for a in part]
    residuals = dict(residuals)
    extra = [residuals[i] for i in sorted(residuals)]
    n_t, n_p, n_c, n_r = len(tiles), len(params), len(flat_cots), len(extra)
    acc_shapes = [params[i].shape for i in diff_params] + list(zero_shapes)

    def body(*refs):
        t_vals = [r[...].astype(F32) for r in refs[:n_t]]
        p_vals = [r[...] for r in refs[n_t:n_t + n_p]]
        flat = iter(r[...] for r in refs[n_t + n_p:n_t + n_p + n_c])
        c_vals = [functools.reduce(jnp.add, [next(flat) for _ in part]) for part in cot_parts]
        r_vals = dict(zip(sorted(residuals), (r[...] for r in refs[n_t + n_p + n_c:n_t + n_p + n_c + n_r])))
        out_refs = refs[n_t + n_p + n_c + n_r:]
        z_vals = [jnp.zeros(s, F32) for s in zero_shapes]
        d_vals = [p_vals[i] for i in diff_params]

        def f(t_in, d_in, z_in):
            full = list(p_vals)
            for i, v in zip(diff_params, d_in):
                full[i] = v
            return tuple(fn(*t_in, *full, *z_in))

        _, vjp = jax.vjp(f, t_vals, d_vals, z_vals)
        g_t, g_d, g_z = vjp(tuple(c_vals))
        for i, (r, g) in enumerate(zip(out_refs[:n_t], g_t)):
            r[...] = (g + r_vals[i] if i in r_vals else g).astype(r.dtype)
        acc_refs = out_refs[n_t:]

        @pl.when(pl.program_id(0) == 0)
        def _():
            for r in acc_refs:
                r[...] = jnp.zeros_like(r)

        for r, g in zip(acc_refs, list(g_d) + list(g_z)):
            r[...] += g

    rows = tiles[0].shape[0]
    in_specs = ([_tile_spec(t, tm) for t in tiles] + [_full_spec(p) for p in params]
                + [_tile_spec(c, tm) for c in flat_cots + extra])
    out_specs = ([_tile_spec(t, tm) for t in tiles]
                 + [pl.BlockSpec(s, lambda i, nd=len(s): (0,) * nd) for s in acc_shapes])
    out_shape = [_sds(t.shape) for t in tiles] + [_sds(s) for s in acc_shapes]
    operands = (*tiles, *params, *flat_cots, *extra)
    if plan is not None:
        return call_with_comm(plan, 0, body, (rows // tm,), name, in_specs, out_specs, out_shape, [], operands)
    return pl.pallas_call(
        body, grid=(rows // tm,), name=name, in_specs=in_specs, out_specs=out_specs, out_shape=out_shape,
        compiler_params=_params(("arbitrary",)),
    )(*operands)


def rms_tile(x, g):
    return (_rms(x, g),)


def rwkv_pre_tile(ps, w0, wd_pad, a0, wi_pad, wg, k_k, k_a):
    r = ps[:, 0:D_RWKV]
    k = ps[:, D_RWKV:2 * D_RWKV]
    v = ps[:, 2 * D_RWKV:3 * D_RWKV]
    z2 = ps[:, 3 * D_RWKV:3 * D_RWKV + LANES]
    zg = ps[:, 3 * D_RWKV + LANES:RWKV_COLS]
    w_log = -_softplus(-(w0 + mm(jnp.tanh(z2), wd_pad))) - 0.5
    lw = -jnp.exp(w_log)
    a = jax.nn.sigmoid(a0 + mm(z2, wi_pad))
    g = mm(jax.nn.sigmoid(zg), wg)
    kk = k * k_k
    norm = jnp.sqrt(_head_sum(kk * kk, 1.0))
    kk = kk / jnp.maximum(norm, 1e-12)
    k2 = k * (1.0 + (a - 1.0) * k_a)
    return r, lw, k2, v, kk, a, g


def mix_out_tile(o, r, k2, v, g, attn, x, w_out, n_post, ln_g, ln_b, r_k, n_ffn_pre, wz):
    d = o - _head_sum(o, 1.0 / HEAD_DIM)
    var = _head_sum(d * d, 1.0 / HEAD_DIM)
    on = d * lax.rsqrt(var + GN_EPS) * ln_g + ln_b
    bonus = _head_sum(r * k2 * r_k, 1.0) * v
    rw = (on + bonus) * g
    mix = mmw(jnp.concatenate([attn, rw], axis=1), w_out, wz)
    x1 = x + _rms(mix, n_post)
    return x1, _rms(x1, n_ffn_pre)


def in_proj_fwd(h, w_in, mix_ext, plan):
    def body(h_ref, w_ref, m_ref, proj_ref, ps_ref):
        p = _dot(h_ref[...], w_ref[...], NN)
        proj_ref[...] = p
        ps_ref[...] = p + (_shift_raw(p, 1) - p) * m_ref[...]

    n = D_IN // COL_TILE
    first = D_QKV // COL_TILE
    return call_with_comm(
        plan, n // 2, body, (n,), "in_proj_fwd",
        [pl.BlockSpec((SEQ, D_MODEL), lambda j: (0, 0)), pl.BlockSpec((D_MODEL, COL_TILE), lambda j: (0, j)),
         pl.BlockSpec((1, COL_TILE), lambda j: (0, j))],
        [pl.BlockSpec((SEQ, COL_TILE), lambda j: (0, j)),
         pl.BlockSpec((SEQ, COL_TILE), lambda j: (0, jnp.maximum(j - first, 0)))],
        [_sds((SEQ, D_IN)), _sds((SEQ, RWKV_COLS))], [], (h, w_in, mix_ext))


def in_proj_bwd(h, w_in, mix_ext, proj, dpa):
    def body(h_ref, w_ref, m_ref, p_ref, d_ref, dh_ref, dw_ref, dm_ref):
        d = d_ref[...]
        p = p_ref[...]
        dm_ref[...] = jnp.sum(d * (_shift_raw(p, 1) - p), axis=0, keepdims=True)
        dmix = d * m_ref[...]
        dp = (d - dmix + _shift_raw(dmix, -1)).astype(BF16)
        dw_ref[...] = _dot(h_ref[...], dp, TN)

        @pl.when(pl.program_id(0) == 0)
        def _():
            dh_ref[...] = jnp.zeros_like(dh_ref)

        dh_ref[...] += _dot(dp, w_ref[...], NT)

    n = D_IN // COL_TILE
    col = lambda rows: pl.BlockSpec((rows, COL_TILE), lambda j: (0, j))
    return pl.pallas_call(
        body, grid=(n,), name="in_proj_bwd",
        in_specs=[pl.BlockSpec((SEQ, D_MODEL), lambda j: (0, 0)), col(D_MODEL), col(1), col(SEQ), col(SEQ)],
        out_specs=[pl.BlockSpec((SEQ, D_MODEL), lambda j: (0, 0)), col(D_MODEL), col(1)],
        out_shape=[_sds((SEQ, D_MODEL)), _sds((D_MODEL, D_IN)), _sds((1, D_IN))],
        compiler_params=_params(("arbitrary",)),
    )(h, w_in, mix_ext, proj, dpa)


def _bucket_table():
    rel = (jnp.arange(BLOCK)[:, None] + BLOCK) - jnp.arange(2 * BLOCK)[None, :]
    n = jnp.maximum(rel, 0)
    max_exact = N_BUCKETS // 2
    large = max_exact + (jnp.log(jnp.maximum(n, 1).astype(F32) / max_exact)
                         / math.log(MAX_DISTANCE / max_exact) * (N_BUCKETS - max_exact)).astype(jnp.int32)
    large = jnp.minimum(large, N_BUCKETS - 1)
    return jnp.where(n < max_exact, n, large).astype(jnp.int32)


def _select_matrix(g, o):
    a = lax.broadcasted_iota(jnp.int32, (D_KV, D_KV), 0)
    b = lax.broadcasted_iota(jnp.int32, (D_KV, D_KV), 1)
    return ((a - HEAD_DIM * g == b - o) & (b >= o) & (b < o + HEAD_DIM)).astype(F32)


def _attn_block(q, kp, kc, vp, vc, bias, sinks, block_idx):
    kb = jnp.concatenate([kp, kc], axis=0)
    vb = jnp.concatenate([vp, vc], axis=0)
    row = lax.broadcasted_iota(jnp.int32, (BLOCK, 2 * BLOCK), 0)
    col = lax.broadcasted_iota(jnp.int32, (BLOCK, 2 * BLOCK), 1)
    rel = row + BLOCK - col
    mask = (rel >= 0) & (rel < BLOCK) & (col + (block_idx - 1) * BLOCK >= 0)
    lane8 = lax.broadcasted_iota(jnp.int32, (1, N_HEADS), 1)
    kt, vt = {}, {}
    for g in range(2):
        for o in (0, HEAD_DIM):
            sel = _select_matrix(g, o)
            kt[g, o] = mm(kb, sel)
            vt[g, o] = mm(vb, sel)
    outs = []
    for j in range(D_ATTN // LANES):
        qs = q[:, j * LANES:(j + 1) * LANES]
        acc = None
        for half in range(2):
            hq = 2 * j + half
            g, o = hq // 4, half * HEAD_DIM
            s = mm_nt(qs, kt[g, o]) * (HEAD_DIM ** -0.5) + bias[hq]
            s = jnp.where(mask, s, NEG_INF)
            sink = jnp.sum(jnp.where(lane8 == hq, sinks, 0.0), axis=1, keepdims=True)
            m = lax.stop_gradient(jnp.maximum(jnp.max(s, axis=-1, keepdims=True), sink))
            p = jnp.exp(s - m)
            probs = p / (jnp.sum(p, axis=-1, keepdims=True) + jnp.exp(sink - m))
            part = mm(probs, vt[g, o])
            acc = part if acc is None else acc + part
        outs.append(acc)
    return jnp.concatenate(outs, axis=1)


def _build_bias(rb_ref, bucket, bias_ref):
    for hq in range(N_HEADS):
        acc = jnp.zeros((BLOCK, 2 * BLOCK), F32)
        for b in range(N_BUCKETS):
            acc = jnp.where(bucket == b, rb_ref[b, hq], acc)
        bias_ref[hq] = acc


def _attn_in_specs(block=lambda n: n):
    prev = lambda n: jnp.maximum(block(n) - 1, 0)
    return [pl.BlockSpec((BLOCK, D_ATTN), lambda n: (block(n), 0)),
            pl.BlockSpec((BLOCK, D_KV), lambda n: (prev(n), D_ATTN // D_KV)),
            pl.BlockSpec((BLOCK, D_KV), lambda n: (block(n), D_ATTN // D_KV)),
            pl.BlockSpec((BLOCK, D_KV), lambda n: (prev(n), D_ATTN // D_KV + 1)),
            pl.BlockSpec((BLOCK, D_KV), lambda n: (block(n), D_ATTN // D_KV + 1)),
            pl.BlockSpec(memory_space=pltpu.SMEM),
            pl.BlockSpec((BLOCK, 2 * BLOCK), lambda n: (0, 0)),
            pl.BlockSpec((1, N_HEADS), lambda n: (0, 0))]


def attn_bwd(proj, rel_bias, bucket, sinks, d_attn):
    last = N_BLOCKS - 1

    def body(q_ref, kp_ref, kc_ref, vp_ref, vc_ref, rb_ref, bk_ref, sk_ref, do_ref,
             dq_ref, dk_ref, dv_ref, drb_ref, dsk_ref, bias_ref, dbias_ref, dk_next, dv_next):
        n = pl.program_id(0)

        @pl.when(n == 0)
        def _():
            _build_bias(rb_ref, bk_ref[...], bias_ref)
            dbias_ref[...] = jnp.zeros_like(dbias_ref)
            dsk_ref[...] = jnp.zeros_like(dsk_ref)
            dk_next[...] = jnp.zeros_like(dk_next)
            dv_next[...] = jnp.zeros_like(dv_next)

        f = lambda q, kp, kc, vp, vc, bias, sk: _attn_block(q, kp, kc, vp, vc, bias, sk, last - n)
        _, vjp = jax.vjp(f, q_ref[...], kp_ref[...], kc_ref[...], vp_ref[...], vc_ref[...],
                         tuple(bias_ref[h] for h in range(N_HEADS)), sk_ref[...])
        dq, dkp, dkc, dvp, dvc, dbias, dsk = vjp(do_ref[...])
        dq_ref[...] = dq
        dk_ref[...] = dkc + dk_next[...]
        dv_ref[...] = dvc + dv_next[...]
        dk_next[...] = dkp
        dv_next[...] = dvp
        for h in range(N_HEADS):
            dbias_ref[h] += dbias[h]
        dsk_ref[...] += dsk

        @pl.when(n == N_BLOCKS - 1)
        def _():
            bucket_v = bk_ref[...]
            rowi = lax.broadcasted_iota(jnp.int32, (N_BUCKETS, 2 * BLOCK), 0)
            lane = lax.broadcasted_iota(jnp.int32, (N_BUCKETS, N_HEADS), 1)
            out = jnp.zeros((N_BUCKETS, N_HEADS), F32)
            for hq in range(N_HEADS):
                dbh = dbias_ref[hq]
                rows = jnp.zeros((N_BUCKETS, 2 * BLOCK), F32)
                for b in range(N_BUCKETS):
                    part = jnp.sum(jnp.where(bucket_v == b, dbh, 0.0), axis=0, keepdims=True)
                    rows = jnp.where(rowi == b, part, rows)
                tot = jnp.sum(rows, axis=1, keepdims=True)
                out = jnp.where(lane == hq, tot, out)
            drb_ref[...] = out

    blk = lambda w: pl.BlockSpec((BLOCK, w), lambda n: (last - n, 0))
    return pl.pallas_call(
        body, grid=(N_BLOCKS,), name="attn_bwd",
        in_specs=_attn_in_specs(lambda n: last - n) + [blk(D_ATTN)],
        out_specs=[blk(D_ATTN), blk(D_KV), blk(D_KV),
                   pl.BlockSpec((N_BUCKETS, N_HEADS), lambda n: (0, 0)),
                   pl.BlockSpec((1, N_HEADS), lambda n: (0, 0))],
        out_shape=[_sds((SEQ, D_ATTN)), _sds((SEQ, D_KV)), _sds((SEQ, D_KV)),
                   _sds((N_BUCKETS, N_HEADS)), _sds((1, N_HEADS))],
        scratch_shapes=[pltpu.VMEM((N_HEADS, BLOCK, 2 * BLOCK), F32),
                        pltpu.VMEM((N_HEADS, BLOCK, 2 * BLOCK), F32),
                        pltpu.VMEM((BLOCK, D_KV), F32), pltpu.VMEM((BLOCK, D_KV), F32)],
        compiler_params=_params(("arbitrary",)),
    )(proj, proj, proj, proj, proj, rel_bias, bucket, sinks, d_attn)


def _stack(x, size):
    groups = x.shape[1] // size
    lane = lax.broadcasted_iota(jnp.int32, x.shape, 1) // size
    return jnp.concatenate([jnp.where(lane == i, x, 0.0) for i in range(groups)], axis=0)


def _neumann(l):
    c = CHUNK
    t = lax.broadcasted_iota(jnp.int32, l.shape, 0)
    i = lax.broadcasted_iota(jnp.int32, l.shape, 1) % c
    inv = (i == t).astype(F32) + l
    pw = dot1(l, _stack(l, c), NN)
    for _ in range(int(math.log2(c)) - 2):
        both = dot1(jnp.concatenate([inv, pw], axis=0), _stack(pw, c), NN)
        inv = inv + both[:c]
        pw = both[c:]
    return inv + dot1(inv, _stack(pw, c), NN)


@jax.custom_vjp
def neumann_inv(l):
    return _neumann(l)


def _neumann_fwd(l):
    inv = _neumann(l)
    return inv, inv


def _neumann_bwd(inv, g):
    c = CHUNK
    bd_t = _stack(inv, c).T
    inv_t = bd_t[0:c]
    for h in range(1, inv.shape[1] // c):
        inv_t = inv_t + bd_t[h * c:(h + 1) * c]
    return (dot1(dot1(inv_t, _stack(g, c), NN), bd_t, NN),)


neumann_inv.defvjp(_neumann_fwd, _neumann_bwd)


def _cumsum_raw(x, dims):
    c = x.shape[0]
    tt = lax.broadcasted_iota(jnp.int32, (c, c), 0)
    ii = lax.broadcasted_iota(jnp.int32, (c, c), 1)
    tri = (ii <= tt).astype(BF16)
    hi = x.astype(BF16)
    rest = x - hi.astype(F32)
    mid = rest.astype(BF16)
    lo = (rest - mid.astype(F32)).astype(BF16)
    return _dot(tri, hi, dims) + (_dot(tri, mid, dims) + _dot(tri, lo, dims))


@jax.custom_vjp
def cumsum_rows(x):
    return _cumsum_raw(x, NN)


def _cumsum_fwd(x):
    return _cumsum_raw(x, NN), None


def _cumsum_bwd(_, g):
    return (_cumsum_raw(g, TN),)


cumsum_rows.defvjp(_cumsum_fwd, _cumsum_bwd)


def _rwkv_chunk(s0, r, lw, k, v, kk, a):
    heads = r.shape[1] // HEAD_DIM
    c, hc = CHUNK, heads * CHUNK
    t = lax.broadcasted_iota(jnp.int32, (c, hc), 0)
    i = lax.broadcasted_iota(jnp.int32, (c, hc), 1) % c
    strict, incl = i < t, i <= t
    stack = lambda x: _stack(x, HEAD_DIM)
    ba = lax.broadcasted_iota(jnp.int32, s0.shape, 0) // HEAD_DIM
    bb = lax.broadcasted_iota(jnp.int32, s0.shape, 1) // HEAD_DIM
    blocks = (ba == bb).astype(F32)

    cum = cumsum_rows(lw)
    cum_end = jnp.sum(lw, axis=0, keepdims=True)
    beta = kk * a
    al = -kk * jnp.exp(cum - lw)
    p_inv = jnp.exp(-cum)
    be, kb, rb = beta * p_inv, k * p_inv, r * jnp.exp(cum)
    ar = jnp.concatenate([al, rb], axis=0)
    sv = stack(v)
    l_all = dot1(ar, jnp.concatenate([stack(be), stack(kb)], axis=0), NT)
    l_ab = jnp.where(strict, l_all[:c, :hc], 0.0)
    l_ak = jnp.where(strict, l_all[:c, hc:], 0.0)
    l_rb = jnp.where(incl, l_all[c:, :hc], 0.0)
    l_rk = jnp.where(incl, l_all[c:, hc:], 0.0)
    inv = neumann_inv(l_ab)
    from_s0 = dot1(ar, s0, NT)
    from_v = dot1(jnp.concatenate([l_ak, l_rk], axis=0), sv, NN)
    u = dot1(inv, stack(from_s0[:c] + from_v[:c]), NN)
    o = from_s0[c:] + from_v[c:] + dot1(l_rb, stack(u), NN)
    to_end = jnp.exp(cum_end - cum)
    s1 = s0 * jnp.exp(cum_end) + blocks * dot1(
        jnp.concatenate([u, v], axis=0), jnp.concatenate([beta * to_end, k * to_end], axis=0), TN)
    return o, s1


def call_with_comm(plan, middle_step, body, grid, name, in_specs, out_specs, out_shape, scratch_shapes, operands):
    n_in, n_out, n_scr = len(in_specs), len(out_specs), len(scratch_shapes)
    p_in, p_out = len(plan.ins), len(plan.out_shape)

    def fused(*refs):
        refs = list(refs)
        ins, refs = refs[:n_in], refs[n_in:]
        p_ins, refs = refs[:p_in], refs[p_in:]
        outs, refs = refs[:n_out], refs[n_out:]
        p_outs, refs = refs[:p_out], refs[p_out:]
        scr, p_sems = refs[:n_scr], refs[n_scr:]
        start, middle, finish = plan.stages(p_ins, p_outs, p_sems)
        step = pl.program_id(0)
        pl.when(step == 0)(start)
        body(*ins, *outs, *scr)
        pl.when(step == middle_step)(middle)
        pl.when(step == grid[0] - 1)(finish)

    any_spec = pl.BlockSpec(memory_space=pl.ANY)
    res = pl.pallas_call(
        fused, grid=grid, name=name,
        in_specs=list(in_specs) + [any_spec] * p_in, out_specs=list(out_specs) + [any_spec] * p_out,
        out_shape=list(out_shape) + list(plan.out_shape), scratch_shapes=list(scratch_shapes) + list(plan.scratch),
        compiler_params=_params(("arbitrary",)),
    )(*operands, *plan.ins)
    return res[:n_out], res[n_out:]


def _by_group(ref):
    return jnp.stack([ref[:, g * SCAN_WIDTH:(g + 1) * SCAN_WIDTH] for g in range(SCAN_GROUPS)])


def _store_groups(ref, val):
    for g in range(SCAN_GROUPS):
        ref[:, g * SCAN_WIDTH:(g + 1) * SCAN_WIDTH] = val[g]


def mixer_fwd(proj, rel_bias, bucket, sinks, r, lw, k, v, kk, a, plan):
    block = lambda c: jnp.minimum(c, N_BLOCKS - 1)

    def body(q_ref, kp_ref, kc_ref, vp_ref, vc_ref, rb_ref, bk_ref, sk_ref, r_ref, lw_ref, k_ref, v_ref, kk_ref,
             a_ref, attn_ref, o_ref, st_ref, bias_ref, s_ref):
        c = pl.program_id(0)

        @pl.when(c == 0)
        def _():
            _build_bias(rb_ref, bk_ref[...], bias_ref)
            s_ref[...] = jnp.zeros_like(s_ref)

        @pl.when(c < N_BLOCKS)
        def _():
            attn_ref[...] = _attn_block(q_ref[...], kp_ref[...], kc_ref[...], vp_ref[...], vc_ref[...],
                                        tuple(bias_ref[h] for h in range(N_HEADS)), sk_ref[...], c)

        s0 = s_ref[...]
        st_ref[0] = s0
        o, s1 = jax.vmap(_rwkv_chunk)(s0, *(_by_group(ref) for ref in (r_ref, lw_ref, k_ref, v_ref, kk_ref, a_ref)))
        _store_groups(o_ref, o)
        s_ref[...] = s1

    tb = pl.BlockSpec((CHUNK, D_RWKV), lambda c: (c, 0))
    state = (SCAN_GROUPS, SCAN_WIDTH, SCAN_WIDTH)
    return call_with_comm(
        plan, 3 * N_CHUNKS // 4, body, (N_CHUNKS,), "mixer_fwd", _attn_in_specs(block) + [tb] * 6,
        [pl.BlockSpec((BLOCK, D_ATTN), lambda c: (block(c), 0)), tb,
         pl.BlockSpec((1,) + state, lambda c: (c, 0, 0, 0))],
        [_sds((SEQ, D_ATTN)), _sds((SEQ, D_RWKV)), _sds((N_CHUNKS,) + state)],
        [pltpu.VMEM((N_HEADS, BLOCK, 2 * BLOCK), F32), pltpu.VMEM(state, F32)],
        (proj, proj, proj, proj, proj, rel_bias, bucket, sinks, r, lw, k, v, kk, a))


def rwkv_scan_bwd(r, lw, k, v, kk, a, states, d_o, plan):
    def body(r_ref, lw_ref, k_ref, v_ref, kk_ref, a_ref, st_ref, do_ref,
             dr_ref, dlw_ref, dk_ref, dv_ref, dkk_ref, da_ref, ds_ref):
        @pl.when(pl.program_id(0) == 0)
        def _():
            ds_ref[...] = jnp.zeros_like(ds_ref)

        _, vjp = jax.vjp(jax.vmap(_rwkv_chunk), st_ref[0],
                         *(_by_group(ref) for ref in (r_ref, lw_ref, k_ref, v_ref, kk_ref, a_ref)))
        grads = vjp((_by_group(do_ref), ds_ref[...]))
        ds_ref[...] = grads[0]
        for ref, val in zip((dr_ref, dlw_ref, dk_ref, dv_ref, dkk_ref, da_ref), grads[1:]):
            _store_groups(ref, val)

    last = N_CHUNKS - 1
    tb = pl.BlockSpec((CHUNK, D_RWKV), lambda c: (last - c, 0))
    state = (SCAN_GROUPS, SCAN_WIDTH, SCAN_WIDTH)
    return call_with_comm(
        plan, N_CHUNKS // 4, body, (N_CHUNKS,), "rwkv_scan_bwd",
        [tb] * 6 + [pl.BlockSpec((1,) + state, lambda c: (last - c, 0, 0, 0)), tb], [tb] * 6,
        [_sds((SEQ, D_RWKV))] * 6, [pltpu.VMEM(state, F32)], (r, lw, k, v, kk, a, states, d_o))


def _ffn_mid(ug, uv, cg, cv, bg, bv):
    conv_g = bg + cg[0] * shift_rows(ug, 2) + cg[1] * shift_rows(ug, 1) + cg[2] * ug
    conv_v = bv + cv[0] * shift_rows(uv, 2) + cv[1] * shift_rows(uv, 1) + cv[2] * uv
    return jax.nn.gelu(conv_g, approximate=True) * conv_v


def _conv_rows(ref):
    return tuple(ref[0, j:j + 1, :] for j in range(3))


def _ffn_specs(tile):
    per = D_MODEL // tile
    half = N_DEV // 2
    w_g = pl.BlockSpec((1, D_MODEL, tile), lambda t: (t // per, 0, t % per))
    w_v = pl.BlockSpec((1, D_MODEL, tile), lambda t: (half + t // per, 0, t % per))
    c_g = pl.BlockSpec((1, 3, tile), lambda t: (t // per, 0, t % per))
    c_v = pl.BlockSpec((1, 3, tile), lambda t: (half + t // per, 0, t % per))
    b_g = pl.BlockSpec((1, tile), lambda t: (0, t))
    b_v = pl.BlockSpec((1, tile), lambda t: (0, D_FF // tile + t))
    w_d = pl.BlockSpec((tile, D_MODEL), lambda t: (t, 0))
    return w_g, w_v, c_g, c_v, b_g, b_v, w_d


def ffn_fwd(h2, w_up, conv_w, conv_b, w_down):
    def body(h_ref, wg_ref, wv_ref, cg_ref, cv_ref, bg_ref, bv_ref, wd_ref, f_ref):
        @pl.when(pl.program_id(0) == 0)
        def _():
            f_ref[...] = jnp.zeros_like(f_ref)

        h = h_ref[...]
        act = _ffn_mid(_dot(h, wg_ref[0], NN), _dot(h, wv_ref[0], NN), _conv_rows(cg_ref), _conv_rows(cv_ref),
                       bg_ref[...], bv_ref[...])
        f_ref[...] += _dot(act.astype(BF16), wd_ref[...], NN)

    full = pl.BlockSpec((SEQ, D_MODEL), lambda t: (0, 0))
    return pl.pallas_call(
        body, grid=(D_FF // FF_TILE,), name="ffn_fwd",
        in_specs=[full, *_ffn_specs(FF_TILE)],
        out_specs=full, out_shape=_sds((SEQ, D_MODEL)),
        compiler_params=_params(("arbitrary",)),
    )(h2, w_up, w_up, conv_w, conv_w, conv_b, conv_b, w_down)


def ffn_bwd_mid(h2, w_up, conv_w, conv_b, w_down, df):
    tile, rows, halo = FF_TILE, FF_ROW_CHUNK, FF_HALO
    ext = rows + 2 * halo

    def body(h_hbm, wg_ref, wv_ref, cg_ref, cv_ref, bg_ref, bv_ref, wd_ref, df_hbm,
             dug_ref, duv_ref, dcg_ref, dcv_ref, dbg_ref, dbv_ref, dwd_ref,
             h_ref, df_ref, ug_ref, uv_ref, da_ref, act_ref):
        @pl.when(pl.program_id(0) == 0)
        def _():
            pltpu.sync_copy(h_hbm, h_ref)
            pltpu.sync_copy(df_hbm, df_ref)
            for ref in (ug_ref, uv_ref, da_ref):
                ref[0:halo, :] = jnp.zeros((halo, tile), F32)
                ref[halo + SEQ:, :] = jnp.zeros((halo, tile), F32)

        h, df_b = h_ref[...], df_ref[...]
        ug_ref[halo:halo + SEQ, :] = _dot(h, wg_ref[0], NN)
        uv_ref[halo:halo + SEQ, :] = _dot(h, wv_ref[0], NN)
        da_ref[halo:halo + SEQ, :] = _dot(df_b, wd_ref[...], NT)
        cg, cv, bg, bv = _conv_rows(cg_ref), _conv_rows(cv_ref), bg_ref[...], bv_ref[...]
        down = lambda x, n: pltpu.roll(x, n, 0)
        up = lambda x, n: pltpu.roll(x, ext - n, 0)
        mid = slice(halo, halo + rows)

        def chunk(i, sums):
            r0 = pl.multiple_of(i * rows, rows)
            window = pl.ds(r0, ext)
            ug, uv, da = ug_ref[window, :], uv_ref[window, :], da_ref[window, :]
            ug1, ug2, uv1, uv2 = down(ug, 1), down(ug, 2), down(uv, 1), down(uv, 2)
            conv_g = bg + cg[0] * ug2 + cg[1] * ug1 + cg[2] * ug
            conv_v = bv + cv[0] * uv2 + cv[1] * uv1 + cv[2] * uv
            act, vjp = jax.vjp(lambda a, b: jax.nn.gelu(a, approximate=True) * b, conv_g, conv_v)
            dcg, dcv = vjp(da)
            dug = cg[2] * dcg + cg[1] * up(dcg, 1) + cg[0] * up(dcg, 2)
            duv = cv[2] * dcv + cv[1] * up(dcv, 1) + cv[0] * up(dcv, 2)
            out = pl.ds(r0, rows)
            act_ref[out, :] = act[mid].astype(BF16)
            dug_ref[out, :] = dug[mid].astype(BF16)
            duv_ref[out, :] = duv[mid].astype(BF16)
            col = lambda x: jnp.sum(x[mid], axis=0, keepdims=True)
            new = (col(dcg * ug2), col(dcg * ug1), col(dcg * ug), col(dcv * uv2), col(dcv * uv1), col(dcv * uv),
                   col(dcg), col(dcv))
            return tuple(s + n for s, n in zip(sums, new))

        zero = jnp.zeros((1, tile), F32)
        sums = lax.fori_loop(0, SEQ // rows, chunk, (zero,) * 8)
        for j in range(3):
            dcg_ref[0, j:j + 1, :] = sums[j]
            dcv_ref[0, j:j + 1, :] = sums[3 + j]
        dbg_ref[...] = sums[6]
        dbv_ref[...] = sums[7]
        dwd_ref[...] = _dot(act_ref[...], df_b, TN).astype(BF16)

    hbm = pl.BlockSpec(memory_space=pl.ANY)
    w_g, w_v, c_g, c_v, b_g, b_v, w_d = _ffn_specs(tile)
    col = pl.BlockSpec((SEQ, tile), lambda t: (0, t))
    padded = pltpu.VMEM((SEQ + 2 * halo, tile), F32)
    return pl.pallas_call(
        body, grid=(D_FF // tile,), name="ffn_bwd_mid",
        in_specs=[hbm, w_g, w_v, c_g, c_v, b_g, b_v, w_d, hbm],
        out_specs=[col, col, c_g, c_v, b_g, b_v, w_d],
        out_shape=[_sds((SEQ, D_FF), BF16), _sds((SEQ, D_FF), BF16), _sds((N_DEV, 3, D_MODEL)),
                   _sds((N_DEV, 3, D_MODEL)), _sds((1, 2 * D_FF)), _sds((1, 2 * D_FF)), _sds((D_FF, D_MODEL), BF16)],
        scratch_shapes=[pltpu.VMEM((SEQ, D_MODEL), BF16), pltpu.VMEM((SEQ, D_MODEL), BF16), padded, padded, padded,
                        pltpu.VMEM((SEQ, tile), BF16)],
        compiler_params=_params(("arbitrary",)),
    )(h2, w_up, w_up, conv_w, conv_w, conv_b, conv_b, w_down, df)


def ffn_bwd_up(h2, w_up, dug, duv):
    tile = FF_TILE
    per = D_MODEL // tile

    def body(h_hbm, wg_ref, wv_ref, dug_ref, duv_ref, dh_hbm, dup_hbm, h_ref, dh_ref, dwg_ref, dwv_ref, sem, up_sems):
        t = pl.program_id(0)

        @pl.when(t == 0)
        def _():
            pltpu.sync_copy(h_hbm, h_ref)
            dh_ref[...] = jnp.zeros_like(dh_ref)

        h, dug_b, duv_b = h_ref[...], dug_ref[...], duv_ref[...]
        cols = pl.ds(pl.multiple_of((t % per) * tile, tile), tile)
        to_gate = pltpu.make_async_copy(dwg_ref, dup_hbm.at[t // per, :, cols], up_sems.at[0])
        to_value = pltpu.make_async_copy(dwv_ref, dup_hbm.at[N_DEV // 2 + t // per, :, cols], up_sems.at[1])
        dwg_ref[...] = _dot(h, dug_b, TN).astype(BF16)
        to_gate.start()
        dwv_ref[...] = _dot(h, duv_b, TN).astype(BF16)
        to_value.start()
        dh_ref[...] += _dot(jnp.concatenate([dug_b, duv_b], axis=1),
                            jnp.concatenate([wg_ref[0], wv_ref[0]], axis=1), NT)
        to_gate.wait()
        to_value.wait()

        @pl.when(t == D_FF // tile - 1)
        def _():
            cp = pltpu.make_async_copy(dh_ref, dh_hbm, sem)
            cp.start()
            cp.wait()

    hbm = pl.BlockSpec(memory_space=pl.ANY)
    w_g, w_v = _ffn_specs(tile)[:2]
    col = pl.BlockSpec((SEQ, tile), lambda t: (0, t))
    return pl.pallas_call(
        body, grid=(D_FF // tile,), name="ffn_bwd_up",
        in_specs=[hbm, w_g, w_v, col, col], out_specs=[hbm, hbm],
        out_shape=[_sds((SEQ, D_MODEL)), _sds((N_DEV, D_MODEL, D_MODEL), BF16)],
        scratch_shapes=[pltpu.VMEM((SEQ, D_MODEL), BF16), pltpu.VMEM((SEQ, D_MODEL), F32),
                        pltpu.VMEM((D_MODEL, tile), BF16), pltpu.VMEM((D_MODEL, tile), BF16),
                        pltpu.SemaphoreType.DMA, pltpu.SemaphoreType.DMA((2,))],
        compiler_params=_params(("arbitrary",)),
    )(h2, w_up, w_up, dug, duv)


def loss_head(x1, f, target, n_post):
    def tile_loss(x1_t, f_t, g, tgt):
        err = x1_t + _rms(f_t, g) - tgt
        return 0.5 * jnp.sum(jnp.mean(err * err, axis=-1))

    def body(x_ref, f_ref, t_ref, g_ref, dx_ref, df_ref, dg_ref, loss_ref):
        val, (dx, df, dg) = jax.value_and_grad(tile_loss, argnums=(0, 1, 2))(
            x_ref[...], f_ref[...], g_ref[...], t_ref[...])
        dx_ref[...] = dx
        df_ref[...] = df.astype(BF16)

        @pl.when(pl.program_id(0) == 0)
        def _():
            dg_ref[...] = jnp.zeros_like(dg_ref)
            loss_ref[...] = jnp.zeros_like(loss_ref)

        dg_ref[...] += dg
        loss_ref[...] += jnp.full((1, LANES), val, F32)

    tile = pl.BlockSpec((TOK_TILE, D_MODEL), lambda i: (i, 0))
    vec = pl.BlockSpec((1, D_MODEL), lambda i: (0, 0))
    return pl.pallas_call(
        body, grid=(SEQ // TOK_TILE,), name="loss_head",
        in_specs=[tile, tile, tile, vec],
        out_specs=[tile, tile, vec, pl.BlockSpec((1, LANES), lambda i: (0, 0))],
        out_shape=[_sds((SEQ, D_MODEL)), _sds((SEQ, D_MODEL), BF16), _sds((1, D_MODEL)), _sds((1, LANES))],
        compiler_params=_params(("arbitrary",)),
    )(x1, f, target, n_post)


def _mesh_pos():
    return lax.axis_index("x"), lax.axis_index("y"), lax.axis_index("c")


def _flip(pos, rel):
    x, y, c = pos
    return (1 - x if rel & 4 else x, 1 - y if rel & 2 else y, 1 - c if rel & 1 else c)


def _slot(pos):
    x, y, c = pos
    return 4 * x + 2 * y + c


def cast_bf16(w, rows):
    def body(w_ref, o_ref):
        o_ref[...] = w_ref[...].astype(BF16)

    spec = pl.BlockSpec((rows, w.shape[1]), lambda i: (i, 0))
    return pl.pallas_call(body, grid=(w.shape[0] // rows,), name="cast_bf16_%dx%d" % w.shape,
                          in_specs=[spec], out_specs=spec, out_shape=_sds(w.shape, BF16),
                          compiler_params=_params(("arbitrary",)))(w)


class CommPlan:
    def __init__(self, ins, out_shape, scratch, stages):
        self.ins, self.out_shape, self.scratch, self.stages = ins, out_shape, scratch, stages


def run_comm(name, plan):
    n_in, n_out = len(plan.ins), len(plan.out_shape)

    def body(*refs):
        for stage in plan.stages(refs[:n_in], refs[n_in:n_in + n_out], refs[n_in + n_out:]):
            stage()

    any_spec = pl.BlockSpec(memory_space=pl.ANY)
    return pl.pallas_call(
        body, name=name, in_specs=[any_spec] * len(plan.ins), out_specs=[any_spec] * len(plan.out_shape),
        out_shape=plan.out_shape, scratch_shapes=plan.scratch)(*plan.ins)


def gather_plan(shards):
    n = len(shards)

    def stages(srcs, outs, sems):
        send_sems, recv_sems, local_sems = sems

        def places():
            me = _mesh_pos()
            return me, _flip(me, 1), [_flip(me, 2), _flip(me, 4), _flip(me, 6)]

        def copy(a, k, block, to, src=None):
            dst = outs[a].at[_slot(block)]
            return pltpu.make_async_remote_copy(
                src_ref=dst if src is None else src, dst_ref=dst,
                send_sem=send_sems.at[7 * a + k], recv_sem=recv_sems.at[7 * a + k],
                device_id=to, device_id_type=pl.DeviceIdType.MESH)

        def local(a, me):
            return pltpu.make_async_copy(srcs[a], outs[a].at[_slot(me)], local_sems.at[a])

        def own(a, me, sibling, chips):
            return [copy(a, 0, me, sibling, src=srcs[a])] + [
                copy(a, 1 + j, me, chip, src=srcs[a]) for j, chip in enumerate(chips)]

        def start():
            me, sibling, chips = places()
            for a in range(n):
                local(a, me).start()
                for cp in own(a, me, sibling, chips):
                    cp.start()

        def forward():
            me, sibling, chips = places()
            for j, chip in enumerate(chips):
                for a in range(n):
                    copy(a, 1 + j, chip, me).wait_recv()
                    copy(a, 4 + j, chip, sibling).start()

        def finish():
            me, sibling, chips = places()
            for a in range(n):
                copy(a, 0, sibling, me).wait_recv()
                for j, chip in enumerate(chips):
                    copy(a, 4 + j, _flip(chip, 1), me).wait_recv()
            for a in range(n):
                for cp in own(a, me, sibling, chips):
                    cp.wait_send()
                for j, chip in enumerate(chips):
                    copy(a, 4 + j, chip, sibling).wait_send()
                local(a, me).wait()

        return start, forward, finish

    return CommPlan(list(shards), [_sds((N_DEV,) + s.shape, s.dtype) for s in shards],
                    [pltpu.SemaphoreType.DMA((7 * n,)), pltpu.SemaphoreType.DMA((7 * n,)),
                     pltpu.SemaphoreType.DMA((n,))], stages)


def exchange_plan(parts, replicated, rels, members, index, member_axis, own_copy):
    n, nr = len(parts), len(rels)
    pick_index = (slice(None),) * member_axis + (0,)
    subs = [1 if (r or member_axis == 0) else p.shape[0] for p, r in zip(parts, replicated)]
    first = [sum(subs[:a]) for a in range(n)]
    total = sum(subs)

    def stages(srcs, outs, sems):
        send_sems, recv_sems, local_sems = sems

        def src(a, s, pos):
            if replicated[a]:
                return srcs[a]
            return srcs[a].at[index(pos)] if member_axis == 0 else srcs[a].at[s, index(pos)]

        def dst(a, s, pos):
            block = outs[a].at[index(pos)]
            return block if (replicated[a] or member_axis == 0) else block.at[s]

        def copy(a, s, j, me, src_pos, dst_pos):
            sem = nr * (first[a] + s) + j
            return pltpu.make_async_remote_copy(
                src_ref=src(a, s, src_pos), dst_ref=dst(a, s, dst_pos),
                send_sem=send_sems.at[sem], recv_sem=recv_sems.at[sem],
                device_id=_flip(me, rels[j]), device_id_type=pl.DeviceIdType.MESH)

        pieces = [(a, s) for a in range(n) for s in range(subs[a])]

        def local(a, s, me):
            return pltpu.make_async_copy(src(a, s, me), dst(a, s, me), local_sems.at[first[a] + s])

        def sends(me):
            return [copy(a, s, j, me, _flip(me, rels[j]), me) for j in range(nr) for a, s in pieces]

        own = pieces if own_copy else []

        def start():
            me = _mesh_pos()
            for cp in sends(me) + [local(a, s, me) for a, s in own]:
                cp.start()

        def middle():
            pass

        def finish():
            me = _mesh_pos()
            for j in range(nr):
                for a, s in pieces:
                    copy(a, s, j, me, me, _flip(me, rels[j])).wait_recv()
            for cp in sends(me):
                cp.wait_send()
            for a, s in own:
                local(a, s, me).wait()

        return start, middle, finish

    shapes = [p.shape if r else jax.eval_shape(lambda t: t[pick_index], p).shape for p, r in zip(parts, replicated)]
    return CommPlan(list(parts), [_sds((members,) + s, p.dtype) for s, p in zip(shapes, parts)],
                    [pltpu.SemaphoreType.DMA((nr * total,)), pltpu.SemaphoreType.DMA((nr * total,)),
                     pltpu.SemaphoreType.DMA((total,))], stages)


def pair_plan(parts, replicated):
    return exchange_plan(parts, replicated, [1], 2, lambda pos: pos[2], 1, False)


def chip_plan(parts, replicated):
    return exchange_plan(parts, replicated, [2, 4, 6], 4, lambda pos: 2 * pos[0] + pos[1], 0, True)


def add_pair(name, mine, swapped, out_dtype, rows):
    def body(m_ref, s_ref, o_ref):
        own = m_ref[0, 0] if mine.ndim == 4 else m_ref[0]
        o_ref[0] = (own.astype(F32) + s_ref[0, 0].astype(F32)).astype(o_ref.dtype)

    _, n, r, c = swapped.shape
    core = lambda: lax.axis_index("c")
    if mine.ndim == 4:
        mine_spec = pl.BlockSpec((1, 1, rows, c), lambda i, j: (i, core(), j, 0))
    else:
        mine_spec = pl.BlockSpec((1, rows, c), lambda i, j: (i, j, 0))
    return pl.pallas_call(
        body, grid=(n, r // rows), name=name,
        in_specs=[mine_spec, pl.BlockSpec((1, 1, rows, c), lambda i, j: (1 - core(), i, j, 0))],
        out_specs=pl.BlockSpec((1, rows, c), lambda i, j: (i, j, 0)),
        out_shape=_sds((n, r, c), out_dtype),
        compiler_params=_params(("arbitrary", "arbitrary")),
    )(mine, swapped)


def add_pair_small(mines, swappeds):
    n = len(mines)
    halves = [m.ndim == s.ndim for m, s in zip(mines, swappeds)]

    def body(*refs):
        c = lax.axis_index("c")
        for i in range(n):
            m_ref, s_ref, o_ref = refs[i], refs[n + i], refs[2 * n + i]
            o_ref[...] = (m_ref[:, c] if halves[i] else m_ref[...]) + s_ref[1 - c]

    vmem = pl.BlockSpec(memory_space=pltpu.VMEM)
    return pl.pallas_call(
        body, name="pair_add_small", in_specs=[vmem] * (2 * n), out_specs=[vmem] * n,
        out_shape=[_sds(s.shape[1:]) for s in swappeds], compiler_params=_params(),
    )(*mines, *swappeds)


def _adamw_math(w, g, m, v):
    nm = ADAM_B1 * m + (1.0 - ADAM_B1) * g
    nv = ADAM_B2 * v + (1.0 - ADAM_B2) * (g * g)
    m_hat = nm / (1.0 - ADAM_B1 ** ADAM_STEP)
    v_hat = nv / (1.0 - ADAM_B2 ** ADAM_STEP)
    return -ADAM_LR * (m_hat / (jnp.sqrt(v_hat) + ADAM_EPS) + ADAM_WD * w), nm, nv


def adamw_small(ws, parts, ms, vs):
    n = len(ws)

    def body(*refs):
        for i in range(n):
            w_ref, p_ref, m_ref, v_ref = (refs[k * n + i] for k in range(4))
            g = p_ref[0]
            for j in range(1, p_ref.shape[0]):
                g = g + p_ref[j]
            delta, nm, nv = _adamw_math(w_ref[...], g, m_ref[...], v_ref[...])
            for k, val in enumerate((g, delta, nm, nv)):
                refs[(4 + k) * n + i][...] = val

    vmem = pl.BlockSpec(memory_space=pltpu.VMEM)
    outs = pl.pallas_call(
        body, name="adamw_small", in_specs=[vmem] * (4 * n), out_specs=[vmem] * (4 * n),
        out_shape=[_sds(w.shape) for w in ws] * 4, compiler_params=_params(),
    )(*ws, *parts, *ms, *vs)
    return [outs[k * n:(k + 1) * n] for k in range(4)]

def adamw(name, w, parts, m, v, rows, plan=None):
    n_parts = parts.shape[0]

    def body(w_ref, p_ref, m_ref, v_ref, g_ref, d_ref, nm_ref, nv_ref):
        g = p_ref[0].astype(F32)
        for j in range(1, n_parts):
            g = g + p_ref[j].astype(F32)
        g_ref[...] = g
        d_ref[...], nm_ref[...], nv_ref[...] = _adamw_math(w_ref[...], g, m_ref[...], v_ref[...])

    cols = w.shape[1]
    spec = pl.BlockSpec((rows, cols), lambda i: (i, 0))
    grid = (w.shape[0] // rows,)
    in_specs = [spec, pl.BlockSpec((n_parts, rows, cols), lambda i: (0, i, 0)), spec, spec]
    if plan is not None:
        return call_with_comm(plan, 0, body, grid, name, in_specs, [spec] * 4, [_sds(w.shape)] * 4, [],
                              (w, parts, m, v))
    return pl.pallas_call(
        body, grid=grid, name=name, in_specs=in_specs, out_specs=[spec] * 4, out_shape=[_sds(w.shape)] * 4,
        compiler_params=_params(("arbitrary",)),
    )(w, parts, m, v)


def _to_slots(full, per):
    return full.reshape(full.shape[0], N_DEV, per).transpose(1, 0, 2)


def _from_slots(slots):
    return slots.transpose(1, 0, 2).reshape(slots.shape[1], -1)


def kernel(x, norm_mix_pre, norm_mix_post, norm_ffn_pre, norm_ffn_post, w_in, rel_bias, sinks, rwkv_shift_mix, w0, w_decay_up, a0, w_iclr_up, w_gate_up, k_k, k_a, r_k, ln_x_g, ln_x_b, w_out, w_ffn_up, conv_w, conv_b, w_ffn_down, loss_target, m_norm_mix_pre, m_norm_mix_post, m_norm_ffn_pre, m_norm_ffn_post, m_w_in, m_rel_bias, m_sinks, m_rwkv_shift_mix, m_w0, m_w_decay_up, m_a0, m_w_iclr_up, m_w_gate_up, m_k_k, m_k_a, m_r_k, m_ln_x_g, m_ln_x_b, m_w_out, m_w_ffn_up, m_conv_w, m_conv_b, m_w_ffn_down, v_norm_mix_pre, v_norm_mix_post, v_norm_ffn_pre, v_norm_ffn_post, v_w_in, v_rel_bias, v_sinks, v_rwkv_shift_mix, v_w0, v_w_decay_up, v_a0, v_w_iclr_up, v_w_gate_up, v_k_k, v_k_a, v_r_k, v_ln_x_g, v_ln_x_b, v_w_out, v_w_ffn_up, v_conv_w, v_conv_b, v_w_ffn_down):
    x2 = x[0]
    target = loss_target[0]

    in_gather = gather_plan([cast_bf16(w_in[0], 256)])
    mixer_gather = gather_plan([cast_bf16(w_out[0], 128), w_decay_up[0], w_iclr_up[0], w_gate_up[0], conv_w[0]])
    ffn_gather = gather_plan([cast_bf16(w_ffn_up[0], 256), cast_bf16(w_ffn_down[0], 256)])
    mix_ext = jnp.concatenate([jnp.zeros((1, D_QKV), F32), rwkv_shift_mix], axis=1)
    r_k_row = r_k.reshape(1, D_RWKV)
    bucket = _bucket_table()

    (h1,), (g_in,) = tok_fwd("rms_mix_pre", rms_tile, [x2], [norm_mix_pre], [], [D_MODEL], [BF16], plan=in_gather)
    w_in_b = _from_slots(g_in)
    (proj, ps), (g_out, g_decay, g_iclr, g_gate, g_conv) = in_proj_fwd(h1, w_in_b, mix_ext, mixer_gather)
    w_out_b = g_out.reshape(D_MODEL, D_MODEL)
    lora = jnp.zeros((HEAD_DIM, D_RWKV), F32)
    wd_pad = jnp.concatenate([_from_slots(g_decay), lora], axis=0)
    wi_pad = jnp.concatenate([lora, _from_slots(g_iclr)], axis=0)
    wg_full = _from_slots(g_gate)
    pre_params = [w0, wd_pad, a0, wi_pad, wg_full, k_k, k_a]
    r_, lw_, k2_, v_, kk_, a_, gate_ = tok_fwd("rwkv_pre", rwkv_pre_tile, [ps], pre_params, [],
                                               [D_RWKV] * 7, [F32] * 7)
    (attn, o_, states), (g_up, g_down) = mixer_fwd(proj, rel_bias, bucket, sinks, r_, lw_, k2_, v_, kk_, a_,
                                                   ffn_gather)
    w_down_b = g_down.reshape(D_FF, D_MODEL)
    mix_tiles = [o_, r_, k2_, v_, gate_, attn, x2]
    mix_params = [w_out_b, norm_mix_post, ln_x_g, ln_x_b, r_k_row, norm_ffn_pre]
    x1, h2 = tok_fwd("mix_out", mix_out_tile, mix_tiles, mix_params, [(D_MODEL, D_MODEL)], [D_MODEL, D_MODEL],
                     [F32, BF16])
    f = ffn_fwd(h2, g_up, g_conv, conv_b, w_down_b)
    dy, df, d_n_ffn_post, loss_row = loss_head(x1, f, target, norm_ffn_post)

    d_ug, d_uv, d_cw_g, d_cw_v, d_cb_g, d_cb_v, d_down = ffn_bwd_mid(h2, g_up, g_conv, conv_b, w_down_b, df)
    dh2, d_up = ffn_bwd_up(h2, g_up, d_ug, d_uv)
    half = N_DEV // 2
    d_cw = jnp.concatenate([d_cw_g[:half], d_cw_v[half:]], axis=0)
    by_pair = lambda slots: slots.reshape((N_DEV // 2, 2) + slots.shape[1:])
    ffn_mine = [by_pair(d_up), by_pair(d_down.reshape(N_DEV, D_FF // N_DEV, D_MODEL))]
    ffn_swapped = run_comm("pair_exchange_ffn", pair_plan(ffn_mine, [False, False]))
    up_exchange = chip_plan([add_pair("pair_add_w_ffn_up", ffn_mine[0], ffn_swapped[0], BF16, 256)], [False])
    down_exchange = chip_plan([add_pair("pair_add_w_ffn_down", ffn_mine[1], ffn_swapped[1], BF16, 256)], [False])
    d_cb = jnp.concatenate([d_cb_g[:, :D_FF], d_cb_v[:, D_FF:]], axis=1)
    ((d_o, d_r1, d_k1, d_v1, d_gate, d_attn, dx_res, d_n_mix_post, d_ln_g, d_ln_b, d_r_k, d_n_ffn_pre, d_w_out),
     (got_down,)) = tok_bwd("mix_out_bwd", mix_out_tile, mix_tiles, mix_params, [(D_MODEL, D_MODEL)], [dy, dh2],
                            [1, 2, 3, 4, 5], plan=down_exchange)
    (d_r2, d_lw, d_k2, d_v2, d_kk, d_a), (got_up,) = rwkv_scan_bwd(
        r_, lw_, k2_, v_, kk_, a_, states, d_o, up_exchange)
    pre_cots = [(d_r1, d_r2), d_lw, (d_k1, d_k2), (d_v1, d_v2), d_kk, d_a, d_gate]
    (d_ps, d_w0, d_wd_pad, d_a0, d_wi_pad, d_wg, d_k_k, d_k_a) = tok_bwd(
        "rwkv_pre_bwd", rwkv_pre_tile, [ps], pre_params, [], pre_cots, [0, 1, 2, 3, 4, 5, 6])
    dq, dk, dv, d_rel_bias, d_sinks = attn_bwd(proj, rel_bias, bucket, sinks, d_attn)
    dpa = jnp.concatenate([dq, dk, dv, d_ps], axis=1)
    dh1, d_w_in, d_mix_ext = in_proj_bwd(h1, w_in_b, mix_ext, proj, dpa)
    grad_x2, d_n_mix_pre = tok_bwd("rms_mix_pre_bwd", rms_tile, [x2], [norm_mix_pre], [], [dh1], [0], {0: dx_res})
    grad_x = grad_x2[None]

    small_rep = [d_n_mix_pre, d_n_mix_post, d_n_ffn_pre, d_n_ffn_post, d_rel_bias, d_sinks,
                 d_mix_ext[:, D_QKV:], d_w0, d_a0, d_k_k, d_k_a, d_r_k.reshape(r_k.shape), d_ln_g, d_ln_b, d_cb]
    rep_w = [norm_mix_pre, norm_mix_post, norm_ffn_pre, norm_ffn_post, rel_bias, sinks, rwkv_shift_mix,
             w0, a0, k_k, k_a, r_k, ln_x_g, ln_x_b, conv_b]
    rep_m = [m_norm_mix_pre, m_norm_mix_post, m_norm_ffn_pre, m_norm_ffn_post, m_rel_bias, m_sinks,
             m_rwkv_shift_mix, m_w0, m_a0, m_k_k, m_k_a, m_r_k, m_ln_x_g, m_ln_x_b, m_conv_b]
    rep_v = [v_norm_mix_pre, v_norm_mix_post, v_norm_ffn_pre, v_norm_ffn_post, v_rel_bias, v_sinks,
             v_rwkv_shift_mix, v_w0, v_a0, v_k_k, v_k_a, v_r_k, v_ln_x_g, v_ln_x_b, v_conv_b]
    sh_w = [w_decay_up, w_iclr_up, w_gate_up, conv_w]
    sh_m = [m_w_decay_up, m_w_iclr_up, m_w_gate_up, m_conv_w]
    sh_v = [v_w_decay_up, v_w_iclr_up, v_w_gate_up, v_conv_w]
    sh_parts = [_to_slots(d_wd_pad[:HEAD_DIM], HEAD_DIM), _to_slots(d_wi_pad[HEAD_DIM:], HEAD_DIM),
                _to_slots(d_wg, HEAD_DIM), d_cw]
    n_rep, n_sh = len(small_rep), len(sh_parts)
    mine = [by_pair(_to_slots(d_w_in, D_IN // N_DEV)), by_pair(d_w_out.reshape(N_DEV, D_MODEL // N_DEV, D_MODEL)),
            *small_rep, *(by_pair(p) for p in sh_parts), loss_row]
    is_rep = [False, False] + [True] * n_rep + [False] * n_sh + [True]
    adam_down, swapped = adamw("adamw_w_ffn_down", w_ffn_down[0], got_down, m_w_ffn_down[0], v_w_ffn_down[0], 128,
                               pair_plan(mine, is_rep))
    chip_sums = [add_pair("pair_add_w_in", mine[0], swapped[0], BF16, 512),
                 add_pair("pair_add_w_out", mine[1], swapped[1], BF16, 128),
                 *add_pair_small(mine[2:], swapped[2:])]
    adam_up, got = adamw("adamw_w_ffn_up", w_ffn_up[0], got_up, m_w_ffn_up[0], v_w_ffn_up[0], 128,
                         chip_plan(chip_sums, is_rep))

    big = [adamw("adamw_w_in", w_in[0], got[0], m_w_in[0], v_w_in[0], 256),
           adamw("adamw_w_out", w_out[0], got[1], m_w_out[0], v_w_out[0], 128), adam_up, adam_down]
    loss = functools.reduce(jnp.add, [got[-1][q, 0, 0] for q in range(N_DEV // 2)])
    small_w, small_g = rep_w + sh_w, got[2:-1]
    as_grad = lambda arrays: [a.reshape(g.shape[1:]) for a, g in zip(arrays, small_g)]
    small = adamw_small(as_grad(small_w), small_g, as_grad(rep_m + sh_m), as_grad(rep_v + sh_v))
    small = [[a.reshape(w.shape) for a, w in zip(kind, small_w)] for kind in small]

    names = ["norm_mix_pre", "norm_mix_post", "norm_ffn_pre", "norm_ffn_post", "w_in", "rel_bias", "sinks",
             "rwkv_shift_mix", "w0", "w_decay_up", "a0", "w_iclr_up", "w_gate_up", "k_k", "k_a", "r_k",
             "ln_x_g", "ln_x_b", "w_out", "w_ffn_up", "conv_w", "conv_b", "w_ffn_down"]
    small_names = ["norm_mix_pre", "norm_mix_post", "norm_ffn_pre", "norm_ffn_post", "rel_bias", "sinks",
                   "rwkv_shift_mix", "w0", "a0", "k_k", "k_a", "r_k", "ln_x_g", "ln_x_b", "conv_b",
                   "w_decay_up", "w_iclr_up", "w_gate_up", "conv_w"]
    big_names = {"w_in": 0, "w_out": 1, "w_ffn_up": 2, "w_ffn_down": 3}
    outs = []
    for kind in range(4):
        for nm in names:
            if nm in big_names:
                outs.append(big[big_names[nm]][kind][None])
            else:
                outs.append(small[kind][small_names.index(nm)])
    return (loss, grad_x, *outs)
```

```python
import functools
import math

import jax
import jax.numpy as jnp
from jax import lax
from jax.experimental import pallas as pl
from jax.experimental.pallas import tpu as pltpu

F32 = jnp.float32
BF16 = jnp.bfloat16

N_DEV = 8
SEQ = 2048
D_MODEL = 1024
HEAD_DIM = 64
D_ATTN = 512
D_KV = 128
D_RWKV = 512
N_HEADS = 8
RWKV_COLS = 1792
D_QKV = D_ATTN + 2 * D_KV
D_IN = D_QKV + RWKV_COLS
D_FF = 4096
BLOCK = 128
N_BLOCKS = SEQ // BLOCK
N_BUCKETS = 32
MAX_DISTANCE = 128
NORM_EPS = 1e-6
GN_EPS = 64e-5
NEG_INF = -1e30
CHUNK = 64
N_CHUNKS = SEQ // CHUNK
SCAN_GROUPS = 4
SCAN_WIDTH = D_RWKV // SCAN_GROUPS
TOK_TILE = 256
FF_TILE = 256
FF_ROW_CHUNK = 256
FF_HALO = 8
COL_TILE = 256
LANES = 128
VMEM_LIMIT = 56 * 1024 * 1024

ADAM_LR = 0.001
ADAM_B1 = 0.9
ADAM_B2 = 0.999
ADAM_EPS = 1e-08
ADAM_WD = 0.01
ADAM_STEP = 10

NT = ((1,), (1,))
TN = ((0,), (0,))
NN = ((1,), (0,))


def _sds(shape, dtype=F32):
    return jax.ShapeDtypeStruct(shape, dtype)


def _params(sem=None):
    if sem is None:
        return pltpu.CompilerParams(vmem_limit_bytes=VMEM_LIMIT)
    return pltpu.CompilerParams(dimension_semantics=sem, vmem_limit_bytes=VMEM_LIMIT)


def _dot(a, b, dims):
    return lax.dot_general(a, b, (dims, ((), ())), preferred_element_type=F32)


def _split2(x):
    hi = x.astype(BF16)
    return hi, (x - hi.astype(F32)).astype(BF16)


def _dot3_raw(a, b, dims):
    ah, al = _split2(a)
    bh, bl = _split2(b)
    return _dot(ah, bh, dims) + (_dot(al, bh, dims) + _dot(ah, bl, dims))


@functools.partial(jax.custom_vjp, nondiff_argnums=(2,))
def dot3(a, b, dims):
    return _dot3_raw(a, b, dims)


def _dot3_fwd(a, b, dims):
    return _dot3_raw(a, b, dims), (a, b)


def _dot3_bwd(dims, res, g):
    a, b = res
    if dims == NN:
        return dot3(g, b, NT), dot3(a, g, TN)
    if dims == NT:
        return dot3(g, b, NN), dot3(g, a, TN)
    return dot3(b, g, NT), dot3(a, g, NN)


dot3.defvjp(_dot3_fwd, _dot3_bwd)


@functools.partial(jax.custom_vjp, nondiff_argnums=(2,))
def dot1(a, b, dims):
    return _dot(a.astype(BF16), b.astype(BF16), dims)


def _dot1_fwd(a, b, dims):
    return dot1(a, b, dims), (a, b)


def _dot1_bwd(dims, res, g):
    a, b = res
    if dims == NN:
        return dot1(g, b, NT), dot1(a, g, TN)
    if dims == NT:
        return dot1(g, b, NN), dot1(g, a, TN)
    return dot1(b, g, NT), dot1(a, g, NN)


dot1.defvjp(_dot1_fwd, _dot1_bwd)


@jax.custom_vjp
def mm(a, b):
    return _dot(a.astype(BF16), b.astype(BF16), NN)


def _mm_fwd(a, b):
    return mm(a, b), (a, b)


def _mm_bwd(res, g):
    a, b = res
    gb = g.astype(BF16)
    return _dot(gb, b.astype(BF16), NT).astype(a.dtype), _dot(a.astype(BF16), gb, TN).astype(b.dtype)


mm.defvjp(_mm_fwd, _mm_bwd)


@jax.custom_vjp
def mm_nt(a, b):
    return _dot(a.astype(BF16), b.astype(BF16), NT)


def _mm_nt_fwd(a, b):
    return mm_nt(a, b), (a, b)


def _mm_nt_bwd(res, g):
    a, b = res
    gb = g.astype(BF16)
    return _dot(gb, b.astype(BF16), NN).astype(a.dtype), _dot(gb, a.astype(BF16), TN).astype(b.dtype)


mm_nt.defvjp(_mm_nt_fwd, _mm_nt_bwd)


@jax.custom_vjp
def mmw(a, w, wz):
    return _dot(a.astype(BF16), w, NN)


def _mmw_fwd(a, w, wz):
    return mmw(a, w, wz), (a, w)


def _mmw_bwd(res, g):
    a, w = res
    gb = g.astype(BF16)
    return _dot(gb, w, NT).astype(a.dtype), jnp.zeros_like(w), _dot(a.astype(BF16), gb, TN)


mmw.defvjp(_mmw_fwd, _mmw_bwd)


def _shift_raw(x, n):
    rows = x.shape[0]
    rolled = pltpu.roll(x, n % rows, 0)
    idx = lax.broadcasted_iota(jnp.int32, x.shape, 0)
    keep = idx >= n if n > 0 else idx < rows + n
    return jnp.where(keep, rolled, 0.0)


@functools.partial(jax.custom_vjp, nondiff_argnums=(1,))
def shift_rows(x, n):
    return _shift_raw(x, n)


def _shift_fwd(x, n):
    return _shift_raw(x, n), None


def _shift_bwd(n, _, g):
    return (_shift_raw(g, -n),)


shift_rows.defvjp(_shift_fwd, _shift_bwd)


def _head_sum(x, scale):
    a = lax.broadcasted_iota(jnp.int32, (LANES, LANES), 0) // HEAD_DIM
    b = lax.broadcasted_iota(jnp.int32, (LANES, LANES), 1) // HEAD_DIM
    pair = jnp.where(a == b, scale, 0.0).astype(F32)
    return jnp.concatenate([dot3(x[:, i:i + LANES], pair, NN) for i in range(0, x.shape[1], LANES)], axis=1)


def _rms(x, g):
    return x * lax.rsqrt(jnp.mean(x * x, axis=-1, keepdims=True) + NORM_EPS) * g


def _softplus(x):
    return jnp.maximum(x, 0.0) + jnp.log(1.0 + jnp.exp(-jnp.abs(x)))


def _tile_spec(arr, tm):
    return pl.BlockSpec((tm, arr.shape[1]), lambda i: (i, 0))


def _full_spec(arr):
    nd = arr.ndim
    return pl.BlockSpec(arr.shape, lambda i: (0,) * nd)


def tok_fwd(name, fn, tiles, params, zero_shapes, out_widths, out_dtypes, plan=None, tm=TOK_TILE):
    n_t, n_p = len(tiles), len(params)

    def body(*refs):
        t_vals = [r[...] for r in refs[:n_t]]
        p_vals = [r[...] for r in refs[n_t:n_t + n_p]]
        z_vals = [jnp.zeros(s, F32) for s in zero_shapes]
        outs = fn(*t_vals, *p_vals, *z_vals)
        for r, o in zip(refs[n_t + n_p:], outs):
            r[...] = o.astype(r.dtype)

    rows = tiles[0].shape[0]
    steps = rows // tm
    in_specs = [_tile_spec(t, tm) for t in tiles] + [_full_spec(p) for p in params]
    out_specs = [pl.BlockSpec((tm, w), lambda i: (i, 0)) for w in out_widths]
    out_shape = [_sds((rows, w), dt) for w, dt in zip(out_widths, out_dtypes)]
    if plan is not None:
        return call_with_comm(plan, steps // 2, body, (steps,), name, in_specs, out_specs, out_shape, [],
                              (*tiles, *params))
    return pl.pallas_call(
        body, grid=(steps,), name=name, in_specs=in_specs, out_specs=out_specs, out_shape=out_shape,
        compiler_params=_params(("arbitrary",)),
    )(*tiles, *params)


def tok_bwd(name, fn, tiles, params, zero_shapes, cots, diff_params, residuals=(), plan=None, tm=TOK_TILE):
    cot_parts = [c if isinstance(c, tuple) else (c,) for c in cots]
    flat_cots = [a for part in cot_parts for a in part]
    residuals = dict(residuals)
    extra = [residuals[i] for i in sorted(residuals)]
    n_t, n_p, n_c, n_r = len(tiles), len(params), len(flat_cots), len(extra)
    acc_shapes = [params[i].shape for i in diff_params] + list(zero_shapes)

    def body(*refs):
        t_vals = [r[...].astype(F32) for r in refs[:n_t]]
        p_vals = [r[...] for r in refs[n_t:n_t + n_p]]
        flat = iter(r[...] for r in refs[n_t + n_p:n_t + n_p + n_c])
        c_vals = [functools.reduce(jnp.add, [next(flat) for _ in part]) for part in cot_parts]
        r_vals = dict(zip(sorted(residuals), (r[...] for r in refs[n_t + n_p + n_c:n_t + n_p + n_c + n_r])))
        out_refs = refs[n_t + n_p + n_c + n_r:]
        z_vals = [jnp.zeros(s, F32) for s in zero_shapes]
        d_vals = [p_vals[i] for i in diff_params]

        def f(t_in, d_in, z_in):
            full = list(p_vals)
            for i, v in zip(diff_params, d_in):
                full[i] = v
            return tuple(fn(*t_in, *full, *z_in))

        _, vjp = jax.vjp(f, t_vals, d_vals, z_vals)
        g_t, g_d, g_z = vjp(tuple(c_vals))
        for i, (r, g) in enumerate(zip(out_refs[:n_t], g_t)):
            r[...] = (g + r_vals[i] if i in r_vals else g).astype(r.dtype)
        acc_refs = out_refs[n_t:]

        @pl.when(pl.program_id(0) == 0)
        def _():
            for r in acc_refs:
                r[...] = jnp.zeros_like(r)

        for r, g in zip(acc_refs, list(g_d) + list(g_z)):
            r[...] += g

    rows = tiles[0].shape[0]
    in_specs = ([_tile_spec(t, tm) for t in tiles] + [_full_spec(p) for p in params]
                + [_tile_spec(c, tm) for c in flat_cots + extra])
    out_specs = ([_tile_spec(t, tm) for t in tiles]
                 + [pl.BlockSpec(s, lambda i, nd=len(s): (0,) * nd) for s in acc_shapes])
    out_shape = [_sds(t.shape) for t in tiles] + [_sds(s) for s in acc_shapes]
    operands = (*tiles, *params, *flat_cots, *extra)
    if plan is not None:
        return call_with_comm(plan, 0, body, (rows // tm,), name, in_specs, out_specs, out_shape, [], operands)
    return pl.pallas_call(
        body, grid=(rows // tm,), name=name, in_specs=in_specs, out_specs=out_specs, out_shape=out_shape,
        compiler_params=_params(("arbitrary",)),
    )(*operands)


def rms_tile(x, g):
    return (_rms(x, g),)


def rwkv_pre_tile(ps, w0, wd_pad, a0, wi_pad, wg, k_k, k_a):
    r = ps[:, 0:D_RWKV]
    k = ps[:, D_RWKV:2 * D_RWKV]
    v = ps[:, 2 * D_RWKV:3 * D_RWKV]
    z2 = ps[:, 3 * D_RWKV:3 * D_RWKV + LANES]
    zg = ps[:, 3 * D_RWKV + LANES:RWKV_COLS]
    w_log = -_softplus(-(w0 + mm(jnp.tanh(z2), wd_pad))) - 0.5
    lw = -jnp.exp(w_log)
    a = jax.nn.sigmoid(a0 + mm(z2, wi_pad))
    g = mm(jax.nn.sigmoid(zg), wg)
    kk = k * k_k
    norm = jnp.sqrt(_head_sum(kk * kk, 1.0))
    kk = kk / jnp.maximum(norm, 1e-12)
    k2 = k * (1.0 + (a - 1.0) * k_a)
    return r, lw, k2, v, kk, a, g


def mix_out_tile(o, r, k2, v, g, attn, x, w_out, n_post, ln_g, ln_b, r_k, n_ffn_pre, wz):
    d = o - _head_sum(o, 1.0 / HEAD_DIM)
    var = _head_sum(d * d, 1.0 / HEAD_DIM)
    on = d * lax.rsqrt(var + GN_EPS) * ln_g + ln_b
    bonus = _head_sum(r * k2 * r_k, 1.0) * v
    rw = (on + bonus) * g
    mix = mmw(jnp.concatenate([attn, rw], axis=1), w_out, wz)
    x1 = x + _rms(mix, n_post)
    return x1, _rms(x1, n_ffn_pre)


def in_proj_fwd(h, w_in, mix_ext, plan):
    def body(h_ref, w_ref, m_ref, proj_ref, ps_ref):
        p = _dot(h_ref[...], w_ref[...], NN)
        proj_ref[...] = p
        ps_ref[...] = p + (_shift_raw(p, 1) - p) * m_ref[...]

    n = D_IN // COL_TILE
    first = D_QKV // COL_TILE
    return call_with_comm(
        plan, n // 2, body, (n,), "in_proj_fwd",
        [pl.BlockSpec((SEQ, D_MODEL), lambda j: (0, 0)), pl.BlockSpec((D_MODEL, COL_TILE), lambda j: (0, j)),
         pl.BlockSpec((1, COL_TILE), lambda j: (0, j))],
        [pl.BlockSpec((SEQ, COL_TILE), lambda j: (0, j)),
         pl.BlockSpec((SEQ, COL_TILE), lambda j: (0, jnp.maximum(j - first, 0)))],
        [_sds((SEQ, D_IN)), _sds((SEQ, RWKV_COLS))], [], (h, w_in, mix_ext))


def in_proj_bwd(h, w_in, mix_ext, proj, dqkv, d_ps):
    first = D_QKV // COL_TILE

    def body(h_ref, w_ref, m_ref, p_ref, a_ref, r_ref, dh_ref, dw_ref, dm_ref):
        d = jnp.where(pl.program_id(0) < first, a_ref[...], r_ref[...])
        p = p_ref[...]
        dm_ref[...] = jnp.sum(d * (_shift_raw(p, 1) - p), axis=0, keepdims=True)
        dmix = d * m_ref[...]
        dp = (d - dmix + _shift_raw(dmix, -1)).astype(BF16)
        dw_ref[...] = _dot(h_ref[...], dp, TN)

        @pl.when(pl.program_id(0) == 0)
        def _():
            dh_ref[...] = jnp.zeros_like(dh_ref)

        dh_ref[...] += _dot(dp, w_ref[...], NT)

    n = D_IN // COL_TILE
    col = lambda rows: pl.BlockSpec((rows, COL_TILE), lambda j: (0, j))
    return pl.pallas_call(
        body, grid=(n,), name="in_proj_bwd",
        in_specs=[pl.BlockSpec((SEQ, D_MODEL), lambda j: (0, 0)), col(D_MODEL), col(1), col(SEQ),
                  pl.BlockSpec((SEQ, COL_TILE), lambda j: (0, jnp.minimum(j, first - 1))),
                  pl.BlockSpec((SEQ, COL_TILE), lambda j: (0, jnp.maximum(j - first, 0)))],
        out_specs=[pl.BlockSpec((SEQ, D_MODEL), lambda j: (0, 0)), col(D_MODEL), col(1)],
        out_shape=[_sds((SEQ, D_MODEL)), _sds((D_MODEL, D_IN)), _sds((1, D_IN))],
        compiler_params=_params(("arbitrary",)),
    )(h, w_in, mix_ext, proj, dqkv, d_ps)


def _bucket_table():
    rel = (jnp.arange(BLOCK)[:, None] + BLOCK) - jnp.arange(2 * BLOCK)[None, :]
    n = jnp.maximum(rel, 0)
    max_exact = N_BUCKETS // 2
    large = max_exact + (jnp.log(jnp.maximum(n, 1).astype(F32) / max_exact)
                         / math.log(MAX_DISTANCE / max_exact) * (N_BUCKETS - max_exact)).astype(jnp.int32)
    large = jnp.minimum(large, N_BUCKETS - 1)
    return jnp.where(n < max_exact, n, large).astype(jnp.int32)


def _select_matrix(g, o):
    a = lax.broadcasted_iota(jnp.int32, (D_KV, D_KV), 0)
    b = lax.broadcasted_iota(jnp.int32, (D_KV, D_KV), 1)
    return ((a - HEAD_DIM * g == b - o) & (b >= o) & (b < o + HEAD_DIM)).astype(F32)


def _attn_block(q, kp, kc, vp, vc, bias, sinks, block_idx):
    kb = jnp.concatenate([kp, kc], axis=0)
    vb = jnp.concatenate([vp, vc], axis=0)
    row = lax.broadcasted_iota(jnp.int32, (BLOCK, 2 * BLOCK), 0)
    col = lax.broadcasted_iota(jnp.int32, (BLOCK, 2 * BLOCK), 1)
    rel = row + BLOCK - col
    mask = (rel >= 0) & (rel < BLOCK) & (col + (block_idx - 1) * BLOCK >= 0)
    lane8 = lax.broadcasted_iota(jnp.int32, (1, N_HEADS), 1)
    kt, vt = {}, {}
    for g in range(2):
        for o in (0, HEAD_DIM):
            sel = _select_matrix(g, o)
            kt[g, o] = mm(kb, sel)
            vt[g, o] = mm(vb, sel)
    outs = []
    for j in range(D_ATTN // LANES):
        qs = q[:, j * LANES:(j + 1) * LANES]
        acc = None
        for half in range(2):
            hq = 2 * j + half
            g, o = hq // 4, half * HEAD_DIM
            s = mm_nt(qs, kt[g, o]) * (HEAD_DIM ** -0.5) + bias[hq]
            s = jnp.where(mask, s, NEG_INF)
            sink = jnp.sum(jnp.where(lane8 == hq, sinks, 0.0), axis=1, keepdims=True)
            m = lax.stop_gradient(jnp.maximum(jnp.max(s, axis=-1, keepdims=True), sink))
            p = jnp.exp(s - m)
            probs = p / (jnp.sum(p, axis=-1, keepdims=True) + jnp.exp(sink - m))
            part = mm(probs, vt[g, o])
            acc = part if acc is None else acc + part
        outs.append(acc)
    return jnp.concatenate(outs, axis=1)


def _build_bias(rb_ref, bucket, bias_ref):
    for hq in range(N_HEADS):
        acc = jnp.zeros((BLOCK, 2 * BLOCK), F32)
        for b in range(N_BUCKETS):
            acc = jnp.where(bucket == b, rb_ref[b, hq], acc)
        bias_ref[hq] = acc


def _attn_in_specs(block=lambda n: n):
    prev = lambda n: jnp.maximum(block(n) - 1, 0)
    return [pl.BlockSpec((BLOCK, D_ATTN), lambda n: (block(n), 0)),
            pl.BlockSpec((BLOCK, D_KV), lambda n: (prev(n), D_ATTN // D_KV)),
            pl.BlockSpec((BLOCK, D_KV), lambda n: (block(n), D_ATTN // D_KV)),
            pl.BlockSpec((BLOCK, D_KV), lambda n: (prev(n), D_ATTN // D_KV + 1)),
            pl.BlockSpec((BLOCK, D_KV), lambda n: (block(n), D_ATTN // D_KV + 1)),
            pl.BlockSpec(memory_space=pltpu.SMEM),
            pl.BlockSpec((BLOCK, 2 * BLOCK), lambda n: (0, 0)),
            pl.BlockSpec((1, N_HEADS), lambda n: (0, 0))]


def attn_bwd(proj, rel_bias, bucket, sinks, d_attn):
    last = N_BLOCKS - 1

    def body(q_ref, kp_ref, kc_ref, vp_ref, vc_ref, rb_ref, bk_ref, sk_ref, do_ref,
             dqkv_ref, drb_ref, dsk_ref, bias_ref, dbias_ref, dk_next, dv_next):
        n = pl.program_id(0)

        @pl.when(n == 0)
        def _():
            _build_bias(rb_ref, bk_ref[...], bias_ref)
            dbias_ref[...] = jnp.zeros_like(dbias_ref)
            dsk_ref[...] = jnp.zeros_like(dsk_ref)
            dk_next[...] = jnp.zeros_like(dk_next)
            dv_next[...] = jnp.zeros_like(dv_next)

        f = lambda q, kp, kc, vp, vc, bias, sk: _attn_block(q, kp, kc, vp, vc, bias, sk, last - n)
        _, vjp = jax.vjp(f, q_ref[...], kp_ref[...], kc_ref[...], vp_ref[...], vc_ref[...],
                         tuple(bias_ref[h] for h in range(N_HEADS)), sk_ref[...])
        dq, dkp, dkc, dvp, dvc, dbias, dsk = vjp(do_ref[...])
        dqkv_ref[:, 0:D_ATTN] = dq
        dqkv_ref[:, D_ATTN:D_ATTN + D_KV] = dkc + dk_next[...]
        dqkv_ref[:, D_ATTN + D_KV:D_QKV] = dvc + dv_next[...]
        dk_next[...] = dkp
        dv_next[...] = dvp
        for h in range(N_HEADS):
            dbias_ref[h] += dbias[h]
        dsk_ref[...] += dsk

        @pl.when(n == N_BLOCKS - 1)
        def _():
            bucket_v = bk_ref[...]
            rowi = lax.broadcasted_iota(jnp.int32, (N_BUCKETS, 2 * BLOCK), 0)
            lane = lax.broadcasted_iota(jnp.int32, (N_BUCKETS, N_HEADS), 1)
            out = jnp.zeros((N_BUCKETS, N_HEADS), F32)
            for hq in range(N_HEADS):
                dbh = dbias_ref[hq]
                rows = jnp.zeros((N_BUCKETS, 2 * BLOCK), F32)
                for b in range(N_BUCKETS):
                    part = jnp.sum(jnp.where(bucket_v == b, dbh, 0.0), axis=0, keepdims=True)
                    rows = jnp.where(rowi == b, part, rows)
                tot = jnp.sum(rows, axis=1, keepdims=True)
                out = jnp.where(lane == hq, tot, out)
            drb_ref[...] = out

    blk = lambda w: pl.BlockSpec((BLOCK, w), lambda n: (last - n, 0))
    return pl.pallas_call(
        body, grid=(N_BLOCKS,), name="attn_bwd",
        in_specs=_attn_in_specs(lambda n: last - n) + [blk(D_ATTN)],
        out_specs=[blk(D_QKV), pl.BlockSpec((N_BUCKETS, N_HEADS), lambda n: (0, 0)),
                   pl.BlockSpec((1, N_HEADS), lambda n: (0, 0))],
        out_shape=[_sds((SEQ, D_QKV)), _sds((N_BUCKETS, N_HEADS)), _sds((1, N_HEADS))],
        scratch_shapes=[pltpu.VMEM((N_HEADS, BLOCK, 2 * BLOCK), F32),
                        pltpu.VMEM((N_HEADS, BLOCK, 2 * BLOCK), F32),
                        pltpu.VMEM((BLOCK, D_KV), F32), pltpu.VMEM((BLOCK, D_KV), F32)],
        compiler_params=_params(("arbitrary",)),
    )(proj, proj, proj, proj, proj, rel_bias, bucket, sinks, d_attn)


def _stack(x, size):
    groups = x.shape[1] // size
    lane = lax.broadcasted_iota(jnp.int32, x.shape, 1) // size
    return jnp.concatenate([jnp.where(lane == i, x, 0.0) for i in range(groups)], axis=0)


def _neumann(l):
    c = CHUNK
    t = lax.broadcasted_iota(jnp.int32, l.shape, 0)
    i = lax.broadcasted_iota(jnp.int32, l.shape, 1) % c
    inv = (i == t).astype(F32) + l
    pw = dot1(l, _stack(l, c), NN)
    for _ in range(int(math.log2(c)) - 2):
        both = dot1(jnp.concatenate([inv, pw], axis=0), _stack(pw, c), NN)
        inv = inv + both[:c]
        pw = both[c:]
    return inv + dot1(inv, _stack(pw, c), NN)


@jax.custom_vjp
def neumann_inv(l):
    return _neumann(l)


def _neumann_fwd(l):
    inv = _neumann(l)
    return inv, inv


def _neumann_bwd(inv, g):
    c = CHUNK
    bd_t = _stack(inv, c).T
    inv_t = bd_t[0:c]
    for h in range(1, inv.shape[1] // c):
        inv_t = inv_t + bd_t[h * c:(h + 1) * c]
    return (dot1(dot1(inv_t, _stack(g, c), NN), bd_t, NN),)


neumann_inv.defvjp(_neumann_fwd, _neumann_bwd)


def _cumsum_raw(x, dims):
    c = x.shape[0]
    tt = lax.broadcasted_iota(jnp.int32, (c, c), 0)
    ii = lax.broadcasted_iota(jnp.int32, (c, c), 1)
    tri = (ii <= tt).astype(BF16)
    hi = x.astype(BF16)
    rest = x - hi.astype(F32)
    mid = rest.astype(BF16)
    lo = (rest - mid.astype(F32)).astype(BF16)
    return _dot(tri, hi, dims) + (_dot(tri, mid, dims) + _dot(tri, lo, dims))


@jax.custom_vjp
def cumsum_rows(x):
    return _cumsum_raw(x, NN)


def _cumsum_fwd(x):
    return _cumsum_raw(x, NN), None


def _cumsum_bwd(_, g):
    return (_cumsum_raw(g, TN),)


cumsum_rows.defvjp(_cumsum_fwd, _cumsum_bwd)


def _rwkv_chunk(s0, r, lw, k, v, kk, a):
    heads = r.shape[1] // HEAD_DIM
    c, hc = CHUNK, heads * CHUNK
    t = lax.broadcasted_iota(jnp.int32, (c, hc), 0)
    i = lax.broadcasted_iota(jnp.int32, (c, hc), 1) % c
    strict, incl = i < t, i <= t
    stack = lambda x: _stack(x, HEAD_DIM)
    ba = lax.broadcasted_iota(jnp.int32, s0.shape, 0) // HEAD_DIM
    bb = lax.broadcasted_iota(jnp.int32, s0.shape, 1) // HEAD_DIM
    blocks = (ba == bb).astype(F32)

    cum = cumsum_rows(lw)
    cum_end = jnp.sum(lw, axis=0, keepdims=True)
    beta = kk * a
    al = -kk * jnp.exp(cum - lw)
    p_inv = jnp.exp(-cum)
    be, kb, rb = beta * p_inv, k * p_inv, r * jnp.exp(cum)
    ar = jnp.concatenate([al, rb], axis=0)
    sv = stack(v)
    l_all = dot1(ar, jnp.concatenate([stack(be), stack(kb)], axis=0), NT)
    l_ab = jnp.where(strict, l_all[:c, :hc], 0.0)
    l_ak = jnp.where(strict, l_all[:c, hc:], 0.0)
    l_rb = jnp.where(incl, l_all[c:, :hc], 0.0)
    l_rk = jnp.where(incl, l_all[c:, hc:], 0.0)
    inv = neumann_inv(l_ab)
    from_s0 = dot1(ar, s0, NT)
    from_v = dot1(jnp.concatenate([l_ak, l_rk], axis=0), sv, NN)
    u = dot1(inv, stack(from_s0[:c] + from_v[:c]), NN)
    o = from_s0[c:] + from_v[c:] + dot1(l_rb, stack(u), NN)
    to_end = jnp.exp(cum_end - cum)
    s1 = s0 * jnp.exp(cum_end) + blocks * dot1(
        jnp.concatenate([u, v], axis=0), jnp.concatenate([beta * to_end, k * to_end], axis=0), TN)
    return o, s1


def call_with_comm(plan, middle_step, body, grid, name, in_specs, out_specs, out_shape, scratch_shapes, operands):
    n_in, n_out, n_scr = len(in_specs), len(out_specs), len(scratch_shapes)
    p_in, p_out = len(plan.ins), len(plan.out_shape)

    def fused(*refs):
        refs = list(refs)
        ins, refs = refs[:n_in], refs[n_in:]
        p_ins, refs = refs[:p_in], refs[p_in:]
        outs, refs = refs[:n_out], refs[n_out:]
        p_outs, refs = refs[:p_out], refs[p_out:]
        scr, p_sems = refs[:n_scr], refs[n_scr:]
        start, middle, finish = plan.stages(p_ins, p_outs, p_sems)
        step = pl.program_id(0)
        pl.when(step == 0)(start)
        body(*ins, *outs, *scr)
        pl.when(step == middle_step)(middle)
        pl.when(step == grid[0] - 1)(finish)

    any_spec = pl.BlockSpec(memory_space=pl.ANY)
    res = pl.pallas_call(
        fused, grid=grid, name=name,
        in_specs=list(in_specs) + [any_spec] * p_in, out_specs=list(out_specs) + [any_spec] * p_out,
        out_shape=list(out_shape) + list(plan.out_shape), scratch_shapes=list(scratch_shapes) + list(plan.scratch),
        compiler_params=_params(("arbitrary",)),
    )(*operands, *plan.ins)
    return res[:n_out], res[n_out:]


def _by_group(ref):
    return jnp.stack([ref[:, g * SCAN_WIDTH:(g + 1) * SCAN_WIDTH] for g in range(SCAN_GROUPS)])


def _store_groups(ref, val):
    for g in range(SCAN_GROUPS):
        ref[:, g * SCAN_WIDTH:(g + 1) * SCAN_WIDTH] = val[g]


def mixer_fwd(proj, rel_bias, bucket, sinks, r, lw, k, v, kk, a, plan):
    block = lambda c: jnp.minimum(c, N_BLOCKS - 1)

    def body(q_ref, kp_ref, kc_ref, vp_ref, vc_ref, rb_ref, bk_ref, sk_ref, r_ref, lw_ref, k_ref, v_ref, kk_ref,
             a_ref, attn_ref, o_ref, st_ref, bias_ref, s_ref):
        c = pl.program_id(0)

        @pl.when(c == 0)
        def _():
            _build_bias(rb_ref, bk_ref[...], bias_ref)
            s_ref[...] = jnp.zeros_like(s_ref)

        @pl.when(c < N_BLOCKS)
        def _():
            attn_ref[...] = _attn_block(q_ref[...], kp_ref[...], kc_ref[...], vp_ref[...], vc_ref[...],
                                        tuple(bias_ref[h] for h in range(N_HEADS)), sk_ref[...], c)

        s0 = s_ref[...]
        st_ref[0] = s0
        o, s1 = jax.vmap(_rwkv_chunk)(s0, *(_by_group(ref) for ref in (r_ref, lw_ref, k_ref, v_ref, kk_ref, a_ref)))
        _store_groups(o_ref, o)
        s_ref[...] = s1

    tb = pl.BlockSpec((CHUNK, D_RWKV), lambda c: (c, 0))
    state = (SCAN_GROUPS, SCAN_WIDTH, SCAN_WIDTH)
    return call_with_comm(
        plan, 3 * N_CHUNKS // 4, body, (N_CHUNKS,), "mixer_fwd", _attn_in_specs(block) + [tb] * 6,
        [pl.BlockSpec((BLOCK, D_ATTN), lambda c: (block(c), 0)), tb,
         pl.BlockSpec((1,) + state, lambda c: (c, 0, 0, 0))],
        [_sds((SEQ, D_ATTN)), _sds((SEQ, D_RWKV)), _sds((N_CHUNKS,) + state)],
        [pltpu.VMEM((N_HEADS, BLOCK, 2 * BLOCK), F32), pltpu.VMEM(state, F32)],
        (proj, proj, proj, proj, proj, rel_bias, bucket, sinks, r, lw, k, v, kk, a))


def rwkv_scan_bwd(r, lw, k, v, kk, a, states, d_o, plan):
    def body(r_ref, lw_ref, k_ref, v_ref, kk_ref, a_ref, st_ref, do_ref,
             dr_ref, dlw_ref, dk_ref, dv_ref, dkk_ref, da_ref, ds_ref):
        @pl.when(pl.program_id(0) == 0)
        def _():
            ds_ref[...] = jnp.zeros_like(ds_ref)

        _, vjp = jax.vjp(jax.vmap(_rwkv_chunk), st_ref[0],
                         *(_by_group(ref) for ref in (r_ref, lw_ref, k_ref, v_ref, kk_ref, a_ref)))
        grads = vjp((_by_group(do_ref), ds_ref[...]))
        ds_ref[...] = grads[0]
        for ref, val in zip((dr_ref, dlw_ref, dk_ref, dv_ref, dkk_ref, da_ref), grads[1:]):
            _store_groups(ref, val)

    last = N_CHUNKS - 1
    tb = pl.BlockSpec((CHUNK, D_RWKV), lambda c: (last - c, 0))
    state = (SCAN_GROUPS, SCAN_WIDTH, SCAN_WIDTH)
    return call_with_comm(
        plan, N_CHUNKS // 4, body, (N_CHUNKS,), "rwkv_scan_bwd",
        [tb] * 6 + [pl.BlockSpec((1,) + state, lambda c: (last - c, 0, 0, 0)), tb], [tb] * 6,
        [_sds((SEQ, D_RWKV))] * 6, [pltpu.VMEM(state, F32)], (r, lw, k, v, kk, a, states, d_o))


def _ffn_mid(ug, uv, cg, cv, bg, bv):
    conv_g = bg + cg[0] * shift_rows(ug, 2) + cg[1] * shift_rows(ug, 1) + cg[2] * ug
    conv_v = bv + cv[0] * shift_rows(uv, 2) + cv[1] * shift_rows(uv, 1) + cv[2] * uv
    return jax.nn.gelu(conv_g, approximate=True) * conv_v


def _conv_rows(ref):
    return tuple(ref[0, j:j + 1, :] for j in range(3))


def _ffn_specs(tile):
    per = D_MODEL // tile
    half = N_DEV // 2
    w_g = pl.BlockSpec((1, D_MODEL, tile), lambda t: (t // per, 0, t % per))
    w_v = pl.BlockSpec((1, D_MODEL, tile), lambda t: (half + t // per, 0, t % per))
    c_g = pl.BlockSpec((1, 3, tile), lambda t: (t // per, 0, t % per))
    c_v = pl.BlockSpec((1, 3, tile), lambda t: (half + t // per, 0, t % per))
    b_g = pl.BlockSpec((1, tile), lambda t: (0, t))
    b_v = pl.BlockSpec((1, tile), lambda t: (0, D_FF // tile + t))
    w_d = pl.BlockSpec((tile, D_MODEL), lambda t: (t, 0))
    return w_g, w_v, c_g, c_v, b_g, b_v, w_d


def ffn_fwd(h2, w_up, conv_w, conv_b, w_down):
    def body(h_ref, wg_ref, wv_ref, cg_ref, cv_ref, bg_ref, bv_ref, wd_ref, f_ref):
        @pl.when(pl.program_id(0) == 0)
        def _():
            f_ref[...] = jnp.zeros_like(f_ref)

        h = h_ref[...]
        act = _ffn_mid(_dot(h, wg_ref[0], NN), _dot(h, wv_ref[0], NN), _conv_rows(cg_ref), _conv_rows(cv_ref),
                       bg_ref[...], bv_ref[...])
        f_ref[...] += _dot(act.astype(BF16), wd_ref[...], NN)

    full = pl.BlockSpec((SEQ, D_MODEL), lambda t: (0, 0))
    return pl.pallas_call(
        body, grid=(D_FF // FF_TILE,), name="ffn_fwd",
        in_specs=[full, *_ffn_specs(FF_TILE)],
        out_specs=full, out_shape=_sds((SEQ, D_MODEL)),
        compiler_params=_params(("arbitrary",)),
    )(h2, w_up, w_up, conv_w, conv_w, conv_b, conv_b, w_down)


def ffn_bwd_mid(h2, w_up, conv_w, conv_b, w_down, df):
    tile, rows, halo = FF_TILE, FF_ROW_CHUNK, FF_HALO
    ext = rows + 2 * halo

    def body(h_hbm, wg_ref, wv_ref, cg_ref, cv_ref, bg_ref, bv_ref, wd_ref, df_hbm,
             dug_ref, duv_ref, dcg_ref, dcv_ref, dbg_ref, dbv_ref, dwd_ref,
             h_ref, df_ref, ug_ref, uv_ref, da_ref, act_ref):
        @pl.when(pl.program_id(0) == 0)
        def _():
            pltpu.sync_copy(h_hbm, h_ref)
            pltpu.sync_copy(df_hbm, df_ref)
            for ref in (ug_ref, uv_ref, da_ref):
                ref[0:halo, :] = jnp.zeros((halo, tile), F32)
                ref[halo + SEQ:, :] = jnp.zeros((halo, tile), F32)

        h, df_b = h_ref[...], df_ref[...]
        ug_ref[halo:halo + SEQ, :] = _dot(h, wg_ref[0], NN)
        uv_ref[halo:halo + SEQ, :] = _dot(h, wv_ref[0], NN)
        da_ref[halo:halo + SEQ, :] = _dot(df_b, wd_ref[...], NT)
        cg, cv, bg, bv = _conv_rows(cg_ref), _conv_rows(cv_ref), bg_ref[...], bv_ref[...]
        down = lambda x, n: pltpu.roll(x, n, 0)
        up = lambda x, n: pltpu.roll(x, ext - n, 0)
        mid = slice(halo, halo + rows)

        def chunk(i, sums):
            r0 = pl.multiple_of(i * rows, rows)
            window = pl.ds(r0, ext)
            ug, uv, da = ug_ref[window, :], uv_ref[window, :], da_ref[window, :]
            ug1, ug2, uv1, uv2 = down(ug, 1), down(ug, 2), down(uv, 1), down(uv, 2)
            conv_g = bg + cg[0] * ug2 + cg[1] * ug1 + cg[2] * ug
            conv_v = bv + cv[0] * uv2 + cv[1] * uv1 + cv[2] * uv
            act, vjp = jax.vjp(lambda a, b: jax.nn.gelu(a, approximate=True) * b, conv_g, conv_v)
            dcg, dcv = vjp(da)
            dug = cg[2] * dcg + cg[1] * up(dcg, 1) + cg[0] * up(dcg, 2)
            duv = cv[2] * dcv + cv[1] * up(dcv, 1) + cv[0] * up(dcv, 2)
            out = pl.ds(r0, rows)
            act_ref[out, :] = act[mid].astype(BF16)
            dug_ref[out, :] = dug[mid].astype(BF16)
            duv_ref[out, :] = duv[mid].astype(BF16)
            col = lambda x: jnp.sum(x[mid], axis=0, keepdims=True)
            new = (col(dcg * ug2), col(dcg * ug1), col(dcg * ug), col(dcv * uv2), col(dcv * uv1), col(dcv * uv),
                   col(dcg), col(dcv))
            return tuple(s + n for s, n in zip(sums, new))

        zero = jnp.zeros((1, tile), F32)
        sums = lax.fori_loop(0, SEQ // rows, chunk, (zero,) * 8)
        for j in range(3):
            dcg_ref[0, j:j + 1, :] = sums[j]
            dcv_ref[0, j:j + 1, :] = sums[3 + j]
        dbg_ref[...] = sums[6]
        dbv_ref[...] = sums[7]
        dwd_ref[...] = _dot(act_ref[...], df_b, TN).astype(BF16)

    hbm = pl.BlockSpec(memory_space=pl.ANY)
    w_g, w_v, c_g, c_v, b_g, b_v, w_d = _ffn_specs(tile)
    col = pl.BlockSpec((SEQ, tile), lambda t: (0, t))
    padded = pltpu.VMEM((SEQ + 2 * halo, tile), F32)
    return pl.pallas_call(
        body, grid=(D_FF // tile,), name="ffn_bwd_mid",
        in_specs=[hbm, w_g, w_v, c_g, c_v, b_g, b_v, w_d, hbm],
        out_specs=[col, col, c_g, c_v, b_g, b_v, w_d],
        out_shape=[_sds((SEQ, D_FF), BF16), _sds((SEQ, D_FF), BF16), _sds((N_DEV, 3, D_MODEL)),
                   _sds((N_DEV, 3, D_MODEL)), _sds((1, 2 * D_FF)), _sds((1, 2 * D_FF)), _sds((D_FF, D_MODEL), BF16)],
        scratch_shapes=[pltpu.VMEM((SEQ, D_MODEL), BF16), pltpu.VMEM((SEQ, D_MODEL), BF16), padded, padded, padded,
                        pltpu.VMEM((SEQ, tile), BF16)],
        compiler_params=_params(("arbitrary",)),
    )(h2, w_up, w_up, conv_w, conv_w, conv_b, conv_b, w_down, df)


def ffn_bwd_up(h2, w_up, dug, duv):
    tile = FF_TILE
    per = D_MODEL // tile

    def body(h_hbm, wg_ref, wv_ref, dug_ref, duv_ref, dh_hbm, dup_hbm, h_ref, dh_ref, dwg_ref, dwv_ref, sem, up_sems):
        t = pl.program_id(0)

        @pl.when(t == 0)
        def _():
            pltpu.sync_copy(h_hbm, h_ref)
            dh_ref[...] = jnp.zeros_like(dh_ref)

        h, dug_b, duv_b = h_ref[...], dug_ref[...], duv_ref[...]
        cols = pl.ds(pl.multiple_of((t % per) * tile, tile), tile)
        to_gate = pltpu.make_async_copy(dwg_ref, dup_hbm.at[t // per, :, cols], up_sems.at[0])
        to_value = pltpu.make_async_copy(dwv_ref, dup_hbm.at[N_DEV // 2 + t // per, :, cols], up_sems.at[1])
        dwg_ref[...] = _dot(h, dug_b, TN).astype(BF16)
        to_gate.start()
        dwv_ref[...] = _dot(h, duv_b, TN).astype(BF16)
        to_value.start()
        dh_ref[...] += _dot(jnp.concatenate([dug_b, duv_b], axis=1),
                            jnp.concatenate([wg_ref[0], wv_ref[0]], axis=1), NT)
        to_gate.wait()
        to_value.wait()

        @pl.when(t == D_FF // tile - 1)
        def _():
            cp = pltpu.make_async_copy(dh_ref, dh_hbm, sem)
            cp.start()
            cp.wait()

    hbm = pl.BlockSpec(memory_space=pl.ANY)
    w_g, w_v = _ffn_specs(tile)[:2]
    col = pl.BlockSpec((SEQ, tile), lambda t: (0, t))
    return pl.pallas_call(
        body, grid=(D_FF // tile,), name="ffn_bwd_up",
        in_specs=[hbm, w_g, w_v, col, col], out_specs=[hbm, hbm],
        out_shape=[_sds((SEQ, D_MODEL)), _sds((N_DEV, D_MODEL, D_MODEL), BF16)],
        scratch_shapes=[pltpu.VMEM((SEQ, D_MODEL), BF16), pltpu.VMEM((SEQ, D_MODEL), F32),
                        pltpu.VMEM((D_MODEL, tile), BF16), pltpu.VMEM((D_MODEL, tile), BF16),
                        pltpu.SemaphoreType.DMA, pltpu.SemaphoreType.DMA((2,))],
        compiler_params=_params(("arbitrary",)),
    )(h2, w_up, w_up, dug, duv)


def loss_head(x1, f, target, n_post):
    def tile_loss(x1_t, f_t, g, tgt):
        err = x1_t + _rms(f_t, g) - tgt
        return 0.5 * jnp.sum(jnp.mean(err * err, axis=-1))

    def body(x_ref, f_ref, t_ref, g_ref, dx_ref, df_ref, dg_ref, loss_ref):
        val, (dx, df, dg) = jax.value_and_grad(tile_loss, argnums=(0, 1, 2))(
            x_ref[...], f_ref[...], g_ref[...], t_ref[...])
        dx_ref[...] = dx
        df_ref[...] = df.astype(BF16)

        @pl.when(pl.program_id(0) == 0)
        def _():
            dg_ref[...] = jnp.zeros_like(dg_ref)
            loss_ref[...] = jnp.zeros_like(loss_ref)

        dg_ref[...] += dg
        loss_ref[...] += jnp.full((1, LANES), val, F32)

    tile = pl.BlockSpec((TOK_TILE, D_MODEL), lambda i: (i, 0))
    vec = pl.BlockSpec((1, D_MODEL), lambda i: (0, 0))
    return pl.pallas_call(
        body, grid=(SEQ // TOK_TILE,), name="loss_head",
        in_specs=[tile, tile, tile, vec],
        out_specs=[tile, tile, vec, pl.BlockSpec((1, LANES), lambda i: (0, 0))],
        out_shape=[_sds((SEQ, D_MODEL)), _sds((SEQ, D_MODEL), BF16), _sds((1, D_MODEL)), _sds((1, LANES))],
        compiler_params=_params(("arbitrary",)),
    )(x1, f, target, n_post)


def _mesh_pos():
    return lax.axis_index("x"), lax.axis_index("y"), lax.axis_index("c")


def _flip(pos, rel):
    x, y, c = pos
    return (1 - x if rel & 4 else x, 1 - y if rel & 2 else y, 1 - c if rel & 1 else c)


def _slot(pos):
    x, y, c = pos
    return 4 * x + 2 * y + c


def cast_bf16(w, rows):
    def body(w_ref, o_ref):
        o_ref[...] = w_ref[...].astype(BF16)

    spec = pl.BlockSpec((rows, w.shape[1]), lambda i: (i, 0))
    return pl.pallas_call(body, grid=(w.shape[0] // rows,), name="cast_bf16_%dx%d" % w.shape,
                          in_specs=[spec], out_specs=spec, out_shape=_sds(w.shape, BF16),
                          compiler_params=_params(("arbitrary",)))(w)


class CommPlan:
    def __init__(self, ins, out_shape, scratch, stages):
        self.ins, self.out_shape, self.scratch, self.stages = ins, out_shape, scratch, stages


def run_comm(name, plan):
    n_in, n_out = len(plan.ins), len(plan.out_shape)

    def body(*refs):
        for stage in plan.stages(refs[:n_in], refs[n_in:n_in + n_out], refs[n_in + n_out:]):
            stage()

    any_spec = pl.BlockSpec(memory_space=pl.ANY)
    return pl.pallas_call(
        body, name=name, in_specs=[any_spec] * len(plan.ins), out_specs=[any_spec] * len(plan.out_shape),
        out_shape=plan.out_shape, scratch_shapes=plan.scratch)(*plan.ins)


def gather_plan(shards):
    n = len(shards)

    def stages(srcs, outs, sems):
        send_sems, recv_sems, local_sems = sems

        def places():
            me = _mesh_pos()
            return me, _flip(me, 1), [_flip(me, 2), _flip(me, 4), _flip(me, 6)]

        def copy(a, k, block, to, src=None):
            dst = outs[a].at[_slot(block)]
            return pltpu.make_async_remote_copy(
                src_ref=dst if src is None else src, dst_ref=dst,
                send_sem=send_sems.at[7 * a + k], recv_sem=recv_sems.at[7 * a + k],
                device_id=to, device_id_type=pl.DeviceIdType.MESH)

        def local(a, me):
            return pltpu.make_async_copy(srcs[a], outs[a].at[_slot(me)], local_sems.at[a])

        def own(a, me, sibling, chips):
            return [copy(a, 0, me, sibling, src=srcs[a])] + [
                copy(a, 1 + j, me, chip, src=srcs[a]) for j, chip in enumerate(chips)]

        def start():
            me, sibling, chips = places()
            for a in range(n):
                local(a, me).start()
                for cp in own(a, me, sibling, chips):
                    cp.start()

        def forward():
            me, sibling, chips = places()
            for j, chip in enumerate(chips):
                for a in range(n):
                    copy(a, 1 + j, chip, me).wait_recv()
                    copy(a, 4 + j, chip, sibling).start()

        def finish():
            me, sibling, chips = places()
            for a in range(n):
                copy(a, 0, sibling, me).wait_recv()
                for j, chip in enumerate(chips):
                    copy(a, 4 + j, _flip(chip, 1), me).wait_recv()
            for a in range(n):
                for cp in own(a, me, sibling, chips):
                    cp.wait_send()
                for j, chip in enumerate(chips):
                    copy(a, 4 + j, chip, sibling).wait_send()
                local(a, me).wait()

        return start, forward, finish

    return CommPlan(list(shards), [_sds((N_DEV,) + s.shape, s.dtype) for s in shards],
                    [pltpu.SemaphoreType.DMA((7 * n,)), pltpu.SemaphoreType.DMA((7 * n,)),
                     pltpu.SemaphoreType.DMA((n,))], stages)


def exchange_plan(parts, replicated, rels, members, index, member_axis, own_copy):
    n, nr = len(parts), len(rels)
    pick_index = (slice(None),) * member_axis + (0,)
    subs = [1 if (r or member_axis == 0) else p.shape[0] for p, r in zip(parts, replicated)]
    first = [sum(subs[:a]) for a in range(n)]
    total = sum(subs)

    def stages(srcs, outs, sems):
        send_sems, recv_sems, local_sems = sems

        def src(a, s, pos):
            if replicated[a]:
                return srcs[a]
            return srcs[a].at[index(pos)] if member_axis == 0 else srcs[a].at[s, index(pos)]

        def dst(a, s, pos):
            block = outs[a].at[index(pos)]
            return block if (replicated[a] or member_axis == 0) else block.at[s]

        def copy(a, s, j, me, src_pos, dst_pos):
            sem = nr * (first[a] + s) + j
            return pltpu.make_async_remote_copy(
                src_ref=src(a, s, src_pos), dst_ref=dst(a, s, dst_pos),
                send_sem=send_sems.at[sem], recv_sem=recv_sems.at[sem],
                device_id=_flip(me, rels[j]), device_id_type=pl.DeviceIdType.MESH)

        pieces = [(a, s) for a in range(n) for s in range(subs[a])]

        def local(a, s, me):
            return pltpu.make_async_copy(src(a, s, me), dst(a, s, me), local_sems.at[first[a] + s])

        def sends(me):
            return [copy(a, s, j, me, _flip(me, rels[j]), me) for j in range(nr) for a, s in pieces]

        own = pieces if own_copy else []

        def start():
            me = _mesh_pos()
            for cp in sends(me) + [local(a, s, me) for a, s in own]:
                cp.start()

        def middle():
            pass

        def finish():
            me = _mesh_pos()
            for j in range(nr):
                for a, s in pieces:
                    copy(a, s, j, me, me, _flip(me, rels[j])).wait_recv()
            for cp in sends(me):
                cp.wait_send()
            for a, s in own:
                local(a, s, me).wait()

        return start, middle, finish

    shapes = [p.shape if r else jax.eval_shape(lambda t: t[pick_index], p).shape for p, r in zip(parts, replicated)]
    return CommPlan(list(parts), [_sds((members,) + s, p.dtype) for s, p in zip(shapes, parts)],
                    [pltpu.SemaphoreType.DMA((nr * total,)), pltpu.SemaphoreType.DMA((nr * total,)),
                     pltpu.SemaphoreType.DMA((total,))], stages)


def pair_plan(parts, replicated):
    return exchange_plan(parts, replicated, [1], 2, lambda pos: pos[2], 1, False)


def chip_plan(parts, replicated):
    return exchange_plan(parts, replicated, [2, 4, 6], 4, lambda pos: 2 * pos[0] + pos[1], 0, True)


def add_pair(name, mine, swapped, out_dtype, rows):
    def body(m_ref, s_ref, o_ref):
        own = m_ref[0, 0] if mine.ndim == 4 else m_ref[0]
        o_ref[0] = (own.astype(F32) + s_ref[0, 0].astype(F32)).astype(o_ref.dtype)

    _, n, r, c = swapped.shape
    core = lambda: lax.axis_index("c")
    if mine.ndim == 4:
        mine_spec = pl.BlockSpec((1, 1, rows, c), lambda i, j: (i, core(), j, 0))
    else:
        mine_spec = pl.BlockSpec((1, rows, c), lambda i, j: (i, j, 0))
    return pl.pallas_call(
        body, grid=(n, r // rows), name=name,
        in_specs=[mine_spec, pl.BlockSpec((1, 1, rows, c), lambda i, j: (1 - core(), i, j, 0))],
        out_specs=pl.BlockSpec((1, rows, c), lambda i, j: (i, j, 0)),
        out_shape=_sds((n, r, c), out_dtype),
        compiler_params=_params(("arbitrary", "arbitrary")),
    )(mine, swapped)


def add_pair_small(mines, swappeds):
    n = len(mines)
    halves = [m.ndim == s.ndim for m, s in zip(mines, swappeds)]

    def body(*refs):
        c = lax.axis_index("c")
        for i in range(n):
            m_ref, s_ref, o_ref = refs[i], refs[n + i], refs[2 * n + i]
            o_ref[...] = (m_ref[:, c] if halves[i] else m_ref[...]) + s_ref[1 - c]

    vmem = pl.BlockSpec(memory_space=pltpu.VMEM)
    return pl.pallas_call(
        body, name="pair_add_small", in_specs=[vmem] * (2 * n), out_specs=[vmem] * n,
        out_shape=[_sds(s.shape[1:]) for s in swappeds], compiler_params=_params(),
    )(*mines, *swappeds)


def _adamw_math(w, g, m, v):
    nm = ADAM_B1 * m + (1.0 - ADAM_B1) * g
    nv = ADAM_B2 * v + (1.0 - ADAM_B2) * (g * g)
    m_hat = nm / (1.0 - ADAM_B1 ** ADAM_STEP)
    v_hat = nv / (1.0 - ADAM_B2 ** ADAM_STEP)
    return -ADAM_LR * (m_hat / (jnp.sqrt(v_hat) + ADAM_EPS) + ADAM_WD * w), nm, nv


def adamw_small(ws, parts, ms, vs):
    n = len(ws)

    def body(*refs):
        for i in range(n):
            w_ref, p_ref, m_ref, v_ref = (refs[k * n + i] for k in range(4))
            g = p_ref[0]
            for j in range(1, p_ref.shape[0]):
                g = g + p_ref[j]
            delta, nm, nv = _adamw_math(w_ref[...], g, m_ref[...], v_ref[...])
            for k, val in enumerate((g, delta, nm, nv)):
                refs[(4 + k) * n + i][...] = val

    vmem = pl.BlockSpec(memory_space=pltpu.VMEM)
    outs = pl.pallas_call(
        body, name="adamw_small", in_specs=[vmem] * (4 * n), out_specs=[vmem] * (4 * n),
        out_shape=[_sds(w.shape) for w in ws] * 4, compiler_params=_params(),
    )(*ws, *parts, *ms, *vs)
    return [outs[k * n:(k + 1) * n] for k in range(4)]

def adamw(name, w, parts, m, v, rows, plan=None):
    n_parts = parts.shape[0]

    def body(w_ref, p_ref, m_ref, v_ref, g_ref, d_ref, nm_ref, nv_ref):
        g = p_ref[0].astype(F32)
        for j in range(1, n_parts):
            g = g + p_ref[j].astype(F32)
        g_ref[...] = g
        d_ref[...], nm_ref[...], nv_ref[...] = _adamw_math(w_ref[...], g, m_ref[...], v_ref[...])

    cols = w.shape[1]
    spec = pl.BlockSpec((rows, cols), lambda i: (i, 0))
    grid = (w.shape[0] // rows,)
    in_specs = [spec, pl.BlockSpec((n_parts, rows, cols), lambda i: (0, i, 0)), spec, spec]
    if plan is not None:
        return call_with_comm(plan, 0, body, grid, name, in_specs, [spec] * 4, [_sds(w.shape)] * 4, [],
                              (w, parts, m, v))
    return pl.pallas_call(
        body, grid=grid, name=name, in_specs=in_specs, out_specs=[spec] * 4, out_shape=[_sds(w.shape)] * 4,
        compiler_params=_params(("arbitrary",)),
    )(w, parts, m, v)


def _to_slots(full, per):
    return full.reshape(full.shape[0], N_DEV, per).transpose(1, 0, 2)


def _from_slots(slots):
    return slots.transpose(1, 0, 2).reshape(slots.shape[1], -1)


def kernel(x, norm_mix_pre, norm_mix_post, norm_ffn_pre, norm_ffn_post, w_in, rel_bias, sinks, rwkv_shift_mix, w0, w_decay_up, a0, w_iclr_up, w_gate_up, k_k, k_a, r_k, ln_x_g, ln_x_b, w_out, w_ffn_up, conv_w, conv_b, w_ffn_down, loss_target, m_norm_mix_pre, m_norm_mix_post, m_norm_ffn_pre, m_norm_ffn_post, m_w_in, m_rel_bias, m_sinks, m_rwkv_shift_mix, m_w0, m_w_decay_up, m_a0, m_w_iclr_up, m_w_gate_up, m_k_k, m_k_a, m_r_k, m_ln_x_g, m_ln_x_b, m_w_out, m_w_ffn_up, m_conv_w, m_conv_b, m_w_ffn_down, v_norm_mix_pre, v_norm_mix_post, v_norm_ffn_pre, v_norm_ffn_post, v_w_in, v_rel_bias, v_sinks, v_rwkv_shift_mix, v_w0, v_w_decay_up, v_a0, v_w_iclr_up, v_w_gate_up, v_k_k, v_k_a, v_r_k, v_ln_x_g, v_ln_x_b, v_w_out, v_w_ffn_up, v_conv_w, v_conv_b, v_w_ffn_down):
    x2 = x[0]
    target = loss_target[0]

    in_gather = gather_plan([cast_bf16(w_in[0], 256)])
    mixer_gather = gather_plan([cast_bf16(w_out[0], 128), w_decay_up[0], w_iclr_up[0], w_gate_up[0], conv_w[0]])
    ffn_gather = gather_plan([cast_bf16(w_ffn_up[0], 256), cast_bf16(w_ffn_down[0], 256)])
    mix_ext = jnp.concatenate([jnp.zeros((1, D_QKV), F32), rwkv_shift_mix], axis=1)
    r_k_row = r_k.reshape(1, D_RWKV)
    bucket = _bucket_table()

    (h1,), (g_in,) = tok_fwd("rms_mix_pre", rms_tile, [x2], [norm_mix_pre], [], [D_MODEL], [BF16], plan=in_gather)
    w_in_b = _from_slots(g_in)
    (proj, ps), (g_out, g_decay, g_iclr, g_gate, g_conv) = in_proj_fwd(h1, w_in_b, mix_ext, mixer_gather)
    w_out_b = g_out.reshape(D_MODEL, D_MODEL)
    lora = jnp.zeros((HEAD_DIM, D_RWKV), F32)
    wd_pad = jnp.concatenate([_from_slots(g_decay), lora], axis=0)
    wi_pad = jnp.concatenate([lora, _from_slots(g_iclr)], axis=0)
    wg_full = _from_slots(g_gate)
    pre_params = [w0, wd_pad, a0, wi_pad, wg_full, k_k, k_a]
    r_, lw_, k2_, v_, kk_, a_, gate_ = tok_fwd("rwkv_pre", rwkv_pre_tile, [ps], pre_params, [],
                                               [D_RWKV] * 7, [F32] * 7)
    (attn, o_, states), (g_up, g_down) = mixer_fwd(proj, rel_bias, bucket, sinks, r_, lw_, k2_, v_, kk_, a_,
                                                   ffn_gather)
    w_down_b = g_down.reshape(D_FF, D_MODEL)
    mix_tiles = [o_, r_, k2_, v_, gate_, attn, x2]
    mix_params = [w_out_b, norm_mix_post, ln_x_g, ln_x_b, r_k_row, norm_ffn_pre]
    x1, h2 = tok_fwd("mix_out", mix_out_tile, mix_tiles, mix_params, [(D_MODEL, D_MODEL)], [D_MODEL, D_MODEL],
                     [F32, BF16])
    f = ffn_fwd(h2, g_up, g_conv, conv_b, w_down_b)
    dy, df, d_n_ffn_post, loss_row = loss_head(x1, f, target, norm_ffn_post)

    d_ug, d_uv, d_cw_g, d_cw_v, d_cb_g, d_cb_v, d_down = ffn_bwd_mid(h2, g_up, g_conv, conv_b, w_down_b, df)
    dh2, d_up = ffn_bwd_up(h2, g_up, d_ug, d_uv)
    half = N_DEV // 2
    d_cw = jnp.concatenate([d_cw_g[:half], d_cw_v[half:]], axis=0)
    by_pair = lambda slots: slots.reshape((N_DEV // 2, 2) + slots.shape[1:])
    ffn_mine = [by_pair(d_up), by_pair(d_down.reshape(N_DEV, D_FF // N_DEV, D_MODEL))]
    ffn_swapped = run_comm("pair_exchange_ffn", pair_plan(ffn_mine, [False, False]))
    up_exchange = chip_plan([add_pair("pair_add_w_ffn_up", ffn_mine[0], ffn_swapped[0], BF16, 256)], [False])
    down_exchange = chip_plan([add_pair("pair_add_w_ffn_down", ffn_mine[1], ffn_swapped[1], BF16, 256)], [False])
    d_cb = jnp.concatenate([d_cb_g[:, :D_FF], d_cb_v[:, D_FF:]], axis=1)
    ((d_o, d_r1, d_k1, d_v1, d_gate, d_attn, dx_res, d_n_mix_post, d_ln_g, d_ln_b, d_r_k, d_n_ffn_pre, d_w_out),
     (got_down,)) = tok_bwd("mix_out_bwd", mix_out_tile, mix_tiles, mix_params, [(D_MODEL, D_MODEL)], [dy, dh2],
                            [1, 2, 3, 4, 5], plan=down_exchange)
    (d_r2, d_lw, d_k2, d_v2, d_kk, d_a), (got_up,) = rwkv_scan_bwd(
        r_, lw_, k2_, v_, kk_, a_, states, d_o, up_exchange)
    pre_cots = [(d_r1, d_r2), d_lw, (d_k1, d_k2), (d_v1, d_v2), d_kk, d_a, d_gate]
    (d_ps, d_w0, d_wd_pad, d_a0, d_wi_pad, d_wg, d_k_k, d_k_a) = tok_bwd(
        "rwkv_pre_bwd", rwkv_pre_tile, [ps], pre_params, [], pre_cots, [0, 1, 2, 3, 4, 5, 6])
    dqkv, d_rel_bias, d_sinks = attn_bwd(proj, rel_bias, bucket, sinks, d_attn)
    dh1, d_w_in, d_mix_ext = in_proj_bwd(h1, w_in_b, mix_ext, proj, dqkv, d_ps)
    grad_x2, d_n_mix_pre = tok_bwd("rms_mix_pre_bwd", rms_tile, [x2], [norm_mix_pre], [], [dh1], [0], {0: dx_res})
    grad_x = grad_x2[None]

    small_rep = [d_n_mix_pre, d_n_mix_post, d_n_ffn_pre, d_n_ffn_post, d_rel_bias, d_sinks,
                 d_mix_ext[:, D_QKV:], d_w0, d_a0, d_k_k, d_k_a, d_r_k.reshape(r_k.shape), d_ln_g, d_ln_b, d_cb]
    rep_w = [norm_mix_pre, norm_mix_post, norm_ffn_pre, norm_ffn_post, rel_bias, sinks, rwkv_shift_mix,
             w0, a0, k_k, k_a, r_k, ln_x_g, ln_x_b, conv_b]
    rep_m = [m_norm_mix_pre, m_norm_mix_post, m_norm_ffn_pre, m_norm_ffn_post, m_rel_bias, m_sinks,
             m_rwkv_shift_mix, m_w0, m_a0, m_k_k, m_k_a, m_r_k, m_ln_x_g, m_ln_x_b, m_conv_b]
    rep_v = [v_norm_mix_pre, v_norm_mix_post, v_norm_ffn_pre, v_norm_ffn_post, v_rel_bias, v_sinks,
             v_rwkv_shift_mix, v_w0, v_a0, v_k_k, v_k_a, v_r_k, v_ln_x_g, v_ln_x_b, v_conv_b]
    sh_w = [w_decay_up, w_iclr_up, w_gate_up, conv_w]
    sh_m = [m_w_decay_up, m_w_iclr_up, m_w_gate_up, m_conv_w]
    sh_v = [v_w_decay_up, v_w_iclr_up, v_w_gate_up, v_conv_w]
    sh_parts = [_to_slots(d_wd_pad[:HEAD_DIM], HEAD_DIM), _to_slots(d_wi_pad[HEAD_DIM:], HEAD_DIM),
                _to_slots(d_wg, HEAD_DIM), d_cw]
    n_rep, n_sh = len(small_rep), len(sh_parts)
    mine = [by_pair(_to_slots(d_w_in, D_IN // N_DEV)), by_pair(d_w_out.reshape(N_DEV, D_MODEL // N_DEV, D_MODEL)),
            *small_rep, *(by_pair(p) for p in sh_parts), loss_row]
    is_rep = [False, False] + [True] * n_rep + [False] * n_sh + [True]
    adam_down, swapped = adamw("adamw_w_ffn_down", w_ffn_down[0], got_down, m_w_ffn_down[0], v_w_ffn_down[0], 128,
                               pair_plan(mine, is_rep))
    chip_sums = [add_pair("pair_add_w_in", mine[0], swapped[0], BF16, 512),
                 add_pair("pair_add_w_out", mine[1], swapped[1], BF16, 128),
                 *add_pair_small(mine[2:], swapped[2:])]
    adam_up, got = adamw("adamw_w_ffn_up", w_ffn_up[0], got_up, m_w_ffn_up[0], v_w_ffn_up[0], 128,
                         chip_plan(chip_sums, is_rep))

    big = [adamw("adamw_w_in", w_in[0], got[0], m_w_in[0], v_w_in[0], 256),
           adamw("adamw_w_out", w_out[0], got[1], m_w_out[0], v_w_out[0], 128), adam_up, adam_down]
    loss = functools.reduce(jnp.add, [got[-1][q, 0, 0] for q in range(N_DEV // 2)])
    small_w, small_g = rep_w + sh_w, got[2:-1]
    as_grad = lambda arrays: [a.reshape(g.shape[1:]) for a, g in zip(arrays, small_g)]
    small = adamw_small(as_grad(small_w), small_g, as_grad(rep_m + sh_m), as_grad(rep_v + sh_v))
    small = [[a.reshape(w.shape) for a, w in zip(kind, small_w)] for kind in small]

    names = ["norm_mix_pre", "norm_mix_post", "norm_ffn_pre", "norm_ffn_post", "w_in", "rel_bias", "sinks",
             "rwkv_shift_mix", "w0", "w_decay_up", "a0", "w_iclr_up", "w_gate_up", "k_k", "k_a", "r_k",
             "ln_x_g", "ln_x_b", "w_out", "w_ffn_up", "conv_w", "conv_b", "w_ffn_down"]
    small_names = ["norm_mix_pre", "norm_mix_post", "norm_ffn_pre", "norm_ffn_post", "rel_bias", "sinks",
                   "rwkv_shift_mix", "w0", "a0", "k_k", "k_a", "r_k", "ln_x_g", "ln_x_b", "conv_b",
                   "w_decay_up", "w_iclr_up", "w_gate_up", "conv_w"]
    big_names = {"w_in": 0, "w_out": 1, "w_ffn_up": 2, "w_ffn_down": 3}
    outs = []
    for kind in range(4):
        for nm in names:
            if nm in big_names:
                outs.append(big[big_names[nm]][kind][None])
            else:
                outs.append(small[kind][small_names.index(nm)])
    return (loss, grad_x, *outs)
```

```python
import functools
import math

import jax
import jax.numpy as jnp
from jax import lax
from jax.experimental import pallas as pl
from jax.experimental.pallas import tpu as pltpu

F32 = jnp.float32
BF16 = jnp.bfloat16

N_DEV = 8
SEQ = 2048
D_MODEL = 1024
HEAD_DIM = 64
D_ATTN = 512
D_KV = 128
D_RWKV = 512
N_HEADS = 8
RWKV_COLS = 1792
D_QKV = D_ATTN + 2 * D_KV
D_IN = D_QKV + RWKV_COLS
D_FF = 4096
BLOCK = 128
N_BLOCKS = SEQ // BLOCK
N_BUCKETS = 32
MAX_DISTANCE = 128
NORM_EPS = 1e-6
GN_EPS = 64e-5
NEG_INF = -1e30
CHUNK = 64
N_CHUNKS = SEQ // CHUNK
SCAN_GROUPS = 4
SCAN_WIDTH = D_RWKV // SCAN_GROUPS
TOK_TILE = 256
FF_TILE = 256
FF_ROW_CHUNK = 256
FF_HALO = 8
COL_TILE = 256
LANES = 128
VMEM_LIMIT = 56 * 1024 * 1024

ADAM_LR = 0.001
ADAM_B1 = 0.9
ADAM_B2 = 0.999
ADAM_EPS = 1e-08
ADAM_WD = 0.01
ADAM_STEP = 10

NT = ((1,), (1,))
TN = ((0,), (0,))
NN = ((1,), (0,))


def _sds(shape, dtype=F32):
    return jax.ShapeDtypeStruct(shape, dtype)


def _params(sem=None):
    if sem is None:
        return pltpu.CompilerParams(vmem_limit_bytes=VMEM_LIMIT)
    return pltpu.CompilerParams(dimension_semantics=sem, vmem_limit_bytes=VMEM_LIMIT)


def _dot(a, b, dims):
    return lax.dot_general(a, b, (dims, ((), ())), preferred_element_type=F32)


def _split2(x):
    hi = x.astype(BF16)
    return hi, (x - hi.astype(F32)).astype(BF16)


def _dot3_raw(a, b, dims):
    ah, al = _split2(a)
    bh, bl = _split2(b)
    return _dot(ah, bh, dims) + (_dot(al, bh, dims) + _dot(ah, bl, dims))


@functools.partial(jax.custom_vjp, nondiff_argnums=(2,))
def dot3(a, b, dims):
    return _dot3_raw(a, b, dims)


def _dot3_fwd(a, b, dims):
    return _dot3_raw(a, b, dims), (a, b)


def _dot3_bwd(dims, res, g):
    a, b = res
    if dims == NN:
        return dot3(g, b, NT), dot3(a, g, TN)
    if dims == NT:
        return dot3(g, b, NN), dot3(g, a, TN)
    return dot3(b, g, NT), dot3(a, g, NN)


dot3.defvjp(_dot3_fwd, _dot3_bwd)


@functools.partial(jax.custom_vjp, nondiff_argnums=(2,))
def dot1(a, b, dims):
    return _dot(a.astype(BF16), b.astype(BF16), dims)


def _dot1_fwd(a, b, dims):
    return dot1(a, b, dims), (a, b)


def _dot1_bwd(dims, res, g):
    a, b = res
    if dims == NN:
        return dot1(g, b, NT), dot1(a, g, TN)
    if dims == NT:
        return dot1(g, b, NN), dot1(g, a, TN)
    return dot1(b, g, NT), dot1(a, g, NN)


dot1.defvjp(_dot1_fwd, _dot1_bwd)


@jax.custom_vjp
def mm(a, b):
    return _dot(a.astype(BF16), b.astype(BF16), NN)


def _mm_fwd(a, b):
    return mm(a, b), (a, b)


def _mm_bwd(res, g):
    a, b = res
    gb = g.astype(BF16)
    return _dot(gb, b.astype(BF16), NT).astype(a.dtype), _dot(a.astype(BF16), gb, TN).astype(b.dtype)


mm.defvjp(_mm_fwd, _mm_bwd)


@jax.custom_vjp
def mm_nt(a, b):
    return _dot(a.astype(BF16), b.astype(BF16), NT)


def _mm_nt_fwd(a, b):
    return mm_nt(a, b), (a, b)


def _mm_nt_bwd(res, g):
    a, b = res
    gb = g.astype(BF16)
    return _dot(gb, b.astype(BF16), NN).astype(a.dtype), _dot(gb, a.astype(BF16), TN).astype(b.dtype)


mm_nt.defvjp(_mm_nt_fwd, _mm_nt_bwd)


@jax.custom_vjp
def mmw(a, w, wz):
    return _dot(a.astype(BF16), w, NN)


def _mmw_fwd(a, w, wz):
    return mmw(a, w, wz), (a, w)


def _mmw_bwd(res, g):
    a, w = res
    gb = g.astype(BF16)
    return _dot(gb, w, NT).astype(a.dtype), jnp.zeros_like(w), _dot(a.astype(BF16), gb, TN)


mmw.defvjp(_mmw_fwd, _mmw_bwd)


def _shift_raw(x, n):
    rows = x.shape[0]
    rolled = pltpu.roll(x, n % rows, 0)
    idx = lax.broadcasted_iota(jnp.int32, x.shape, 0)
    keep = idx >= n if n > 0 else idx < rows + n
    return jnp.where(keep, rolled, 0.0)


@functools.partial(jax.custom_vjp, nondiff_argnums=(1,))
def shift_rows(x, n):
    return _shift_raw(x, n)


def _shift_fwd(x, n):
    return _shift_raw(x, n), None


def _shift_bwd(n, _, g):
    return (_shift_raw(g, -n),)


shift_rows.defvjp(_shift_fwd, _shift_bwd)


def _head_sum(x, scale):
    a = lax.broadcasted_iota(jnp.int32, (LANES, LANES), 0) // HEAD_DIM
    b = lax.broadcasted_iota(jnp.int32, (LANES, LANES), 1) // HEAD_DIM
    pair = jnp.where(a == b, scale, 0.0).astype(F32)
    return jnp.concatenate([dot3(x[:, i:i + LANES], pair, NN) for i in range(0, x.shape[1], LANES)], axis=1)


def _rms(x, g):
    return x * lax.rsqrt(jnp.mean(x * x, axis=-1, keepdims=True) + NORM_EPS) * g


def _softplus(x):
    return jnp.maximum(x, 0.0) + jnp.log(1.0 + jnp.exp(-jnp.abs(x)))


def _tile_spec(arr, tm):
    return pl.BlockSpec((tm, arr.shape[1]), lambda i: (i, 0))


def _full_spec(arr):
    nd = arr.ndim
    return pl.BlockSpec(arr.shape, lambda i: (0,) * nd)


def tok_fwd(name, fn, tiles, params, zero_shapes, out_widths, out_dtypes, plan=None, tm=TOK_TILE):
    n_t, n_p = len(tiles), len(params)

    def body(*refs):
        t_vals = [r[...] for r in refs[:n_t]]
        p_vals = [r[...] for r in refs[n_t:n_t + n_p]]
        z_vals = [jnp.zeros(s, F32) for s in zero_shapes]
        outs = fn(*t_vals, *p_vals, *z_vals)
        for r, o in zip(refs[n_t + n_p:], outs):
            r[...] = o.astype(r.dtype)

    rows = tiles[0].shape[0]
    steps = rows // tm
    in_specs = [_tile_spec(t, tm) for t in tiles] + [_full_spec(p) for p in params]
    out_specs = [pl.BlockSpec((tm, w), lambda i: (i, 0)) for w in out_widths]
    out_shape = [_sds((rows, w), dt) for w, dt in zip(out_widths, out_dtypes)]
    if plan is not None:
        return call_with_comm(plan, steps // 2, body, (steps,), name, in_specs, out_specs, out_shape, [],
                              (*tiles, *params))
    return pl.pallas_call(
        body, grid=(steps,), name=name, in_specs=in_specs, out_specs=out_specs, out_shape=out_shape,
        compiler_params=_params(("arbitrary",)),
    )(*tiles, *params)


def tok_bwd(name, fn, tiles, params, zero_shapes, cots, diff_params, residuals=(), plan=None, tm=TOK_TILE):
    cot_parts = [c if isinstance(c, tuple) else (c,) for c in cots]
    flat_cots = [a for part in cot_parts for a in part]
    residuals = dict(residuals)
    extra = [residuals[i] for i in sorted(residuals)]
    n_t, n_p, n_c, n_r = len(tiles), len(params), len(flat_cots), len(extra)
    acc_shapes = [params[i].shape for i in diff_params] + list(zero_shapes)

    def body(*refs):
        t_vals = [r[...].astype(F32) for r in refs[:n_t]]
        p_vals = [r[...] for r in refs[n_t:n_t + n_p]]
        flat = iter(r[...] for r in refs[n_t + n_p:n_t + n_p + n_c])
        c_vals = [functools.reduce(jnp.add, [next(flat) for _ in part]) for part in cot_parts]
        r_vals = dict(zip(sorted(residuals), (r[...] for r in refs[n_t + n_p + n_c:n_t + n_p + n_c + n_r])))
        out_refs = refs[n_t + n_p + n_c + n_r:]
        z_vals = [jnp.zeros(s, F32) for s in zero_shapes]
        d_vals = [p_vals[i] for i in diff_params]

        def f(t_in, d_in, z_in):
            full = list(p_vals)
            for i, v in zip(diff_params, d_in):
                full[i] = v
            return tuple(fn(*t_in, *full, *z_in))

        _, vjp = jax.vjp(f, t_vals, d_vals, z_vals)
        g_t, g_d, g_z = vjp(tuple(c_vals))
        for i, (r, g) in enumerate(zip(out_refs[:n_t], g_t)):
            r[...] = (g + r_vals[i] if i in r_vals else g).astype(r.dtype)
        acc_refs = out_refs[n_t:]

        @pl.when(pl.program_id(0) == 0)
        def _():
            for r in acc_refs:
                r[...] = jnp.zeros_like(r)

        for r, g in zip(acc_refs, list(g_d) + list(g_z)):
            r[...] += g

    rows = tiles[0].shape[0]
    in_specs = ([_tile_spec(t, tm) for t in tiles] + [_full_spec(p) for p in params]
                + [_tile_spec(c, tm) for c in flat_cots + extra])
    out_specs = ([_tile_spec(t, tm) for t in tiles]
                 + [pl.BlockSpec(s, lambda i, nd=len(s): (0,) * nd) for s in acc_shapes])
    out_shape = [_sds(t.shape) for t in tiles] + [_sds(s) for s in acc_shapes]
    operands = (*tiles, *params, *flat_cots, *extra)
    if plan is not None:
        return call_with_comm(plan, 0, body, (rows // tm,), name, in_specs, out_specs, out_shape, [], operands)
    return pl.pallas_call(
        body, grid=(rows // tm,), name=name, in_specs=in_specs, out_specs=out_specs, out_shape=out_shape,
        compiler_params=_params(("arbitrary",)),
    )(*operands)


def rms_tile(x, g):
    return (_rms(x, g),)


def rwkv_pre_tile(ps, w0, wd_pad, a0, wi_pad, wg, k_k, k_a):
    r = ps[:, 0:D_RWKV]
    k = ps[:, D_RWKV:2 * D_RWKV]
    v = ps[:, 2 * D_RWKV:3 * D_RWKV]
    z2 = ps[:, 3 * D_RWKV:3 * D_RWKV + LANES]
    zg = ps[:, 3 * D_RWKV + LANES:RWKV_COLS]
    w_log = -_softplus(-(w0 + mm(jnp.tanh(z2), wd_pad))) - 0.5
    lw = -jnp.exp(w_log)
    a = jax.nn.sigmoid(a0 + mm(z2, wi_pad))
    g = mm(jax.nn.sigmoid(zg), wg)
    kk = k * k_k
    norm = jnp.sqrt(_head_sum(kk * kk, 1.0))
    kk = kk / jnp.maximum(norm, 1e-12)
    k2 = k * (1.0 + (a - 1.0) * k_a)
    return r, lw, k2, v, kk, a, g


def mix_out_tile(o, r, k2, v, g, attn, x, w_out, n_post, ln_g, ln_b, r_k, n_ffn_pre, wz):
    d = o - _head_sum(o, 1.0 / HEAD_DIM)
    var = _head_sum(d * d, 1.0 / HEAD_DIM)
    on = d * lax.rsqrt(var + GN_EPS) * ln_g + ln_b
    bonus = _head_sum(r * k2 * r_k, 1.0) * v
    rw = (on + bonus) * g
    mix = mmw(jnp.concatenate([attn, rw], axis=1), w_out, wz)
    x1 = x + _rms(mix, n_post)
    return x1, _rms(x1, n_ffn_pre)


def in_proj_fwd(h, w_in, mix_ext, plan):
    def body(h_ref, w_ref, m_ref, proj_ref, ps_ref):
        p = _dot(h_ref[...], w_ref[...], NN)
        proj_ref[...] = p
        ps_ref[...] = p + (_shift_raw(p, 1) - p) * m_ref[...]

    n = D_IN // COL_TILE
    first = D_QKV // COL_TILE
    return call_with_comm(
        plan, n // 2, body, (n,), "in_proj_fwd",
        [pl.BlockSpec((SEQ, D_MODEL), lambda j: (0, 0)), pl.BlockSpec((D_MODEL, COL_TILE), lambda j: (0, j)),
         pl.BlockSpec((1, COL_TILE), lambda j: (0, j))],
        [pl.BlockSpec((SEQ, COL_TILE), lambda j: (0, j)),
         pl.BlockSpec((SEQ, COL_TILE), lambda j: (0, jnp.maximum(j - first, 0)))],
        [_sds((SEQ, D_IN)), _sds((SEQ, RWKV_COLS))], [], (h, w_in, mix_ext))


def in_proj_bwd(h, w_in, mix_ext, proj, dqkv, d_ps):
    first = D_QKV // COL_TILE

    def body(h_ref, w_ref, m_ref, p_ref, a_ref, r_ref, dh_ref, dw_ref, dm_ref):
        d = jnp.where(pl.program_id(0) < first, a_ref[...], r_ref[...])
        p = p_ref[...]
        dm_ref[...] = jnp.sum(d * (_shift_raw(p, 1) - p), axis=0, keepdims=True)
        dmix = d * m_ref[...]
        dp = (d - dmix + _shift_raw(dmix, -1)).astype(BF16)
        dw_ref[...] = _dot(h_ref[...], dp, TN)

        @pl.when(pl.program_id(0) == 0)
        def _():
            dh_ref[...] = jnp.zeros_like(dh_ref)

        dh_ref[...] += _dot(dp, w_ref[...], NT)

    n = D_IN // COL_TILE
    col = lambda rows: pl.BlockSpec((rows, COL_TILE), lambda j: (0, j))
    return pl.pallas_call(
        body, grid=(n,), name="in_proj_bwd",
        in_specs=[pl.BlockSpec((SEQ, D_MODEL), lambda j: (0, 0)), col(D_MODEL), col(1), col(SEQ),
                  pl.BlockSpec((SEQ, COL_TILE), lambda j: (0, jnp.minimum(j, first - 1))),
                  pl.BlockSpec((SEQ, COL_TILE), lambda j: (0, jnp.maximum(j - first, 0)))],
        out_specs=[pl.BlockSpec((SEQ, D_MODEL), lambda j: (0, 0)), col(D_MODEL), col(1)],
        out_shape=[_sds((SEQ, D_MODEL)), _sds((D_MODEL, D_IN)), _sds((1, D_IN))],
        compiler_params=_params(("arbitrary",)),
    )(h, w_in, mix_ext, proj, dqkv, d_ps)


def _bucket_table():
    rel = (jnp.arange(BLOCK)[:, None] + BLOCK) - jnp.arange(2 * BLOCK)[None, :]
    n = jnp.maximum(rel, 0)
    max_exact = N_BUCKETS // 2
    large = max_exact + (jnp.log(jnp.maximum(n, 1).astype(F32) / max_exact)
                         / math.log(MAX_DISTANCE / max_exact) * (N_BUCKETS - max_exact)).astype(jnp.int32)
    large = jnp.minimum(large, N_BUCKETS - 1)
    return jnp.where(n < max_exact, n, large).astype(jnp.int32)


def _select_matrix(g, o):
    a = lax.broadcasted_iota(jnp.int32, (D_KV, D_KV), 0)
    b = lax.broadcasted_iota(jnp.int32, (D_KV, D_KV), 1)
    return ((a - HEAD_DIM * g == b - o) & (b >= o) & (b < o + HEAD_DIM)).astype(F32)


def _attn_block(q, kp, kc, vp, vc, bias, sinks, block_idx):
    kb = jnp.concatenate([kp, kc], axis=0)
    vb = jnp.concatenate([vp, vc], axis=0)
    row = lax.broadcasted_iota(jnp.int32, (BLOCK, 2 * BLOCK), 0)
    col = lax.broadcasted_iota(jnp.int32, (BLOCK, 2 * BLOCK), 1)
    rel = row + BLOCK - col
    mask = (rel >= 0) & (rel < BLOCK) & (col + (block_idx - 1) * BLOCK >= 0)
    lane8 = lax.broadcasted_iota(jnp.int32, (1, N_HEADS), 1)
    kt, vt = {}, {}
    for g in range(2):
        for o in (0, HEAD_DIM):
            sel = _select_matrix(g, o)
            kt[g, o] = mm(kb, sel)
            vt[g, o] = mm(vb, sel)
    outs = []
    for j in range(D_ATTN // LANES):
        qs = q[:, j * LANES:(j + 1) * LANES]
        acc = None
        for half in range(2):
            hq = 2 * j + half
            g, o = hq // 4, half * HEAD_DIM
            s = mm_nt(qs, kt[g, o]) * (HEAD_DIM ** -0.5) + bias[hq]
            s = jnp.where(mask, s, NEG_INF)
            sink = jnp.sum(jnp.where(lane8 == hq, sinks, 0.0), axis=1, keepdims=True)
            m = lax.stop_gradient(jnp.maximum(jnp.max(s, axis=-1, keepdims=True), sink))
            p = jnp.exp(s - m)
            probs = p / (jnp.sum(p, axis=-1, keepdims=True) + jnp.exp(sink - m))
            part = mm(probs, vt[g, o])
            acc = part if acc is None else acc + part
        outs.append(acc)
    return jnp.concatenate(outs, axis=1)


def _build_bias(rb_ref, bucket, bias_ref):
    for hq in range(N_HEADS):
        acc = jnp.zeros((BLOCK, 2 * BLOCK), F32)
        for b in range(N_BUCKETS):
            acc = jnp.where(bucket == b, rb_ref[b, hq], acc)
        bias_ref[hq] = acc


def _attn_in_specs(block=lambda n: n):
    prev = lambda n: jnp.maximum(block(n) - 1, 0)
    return [pl.BlockSpec((BLOCK, D_ATTN), lambda n: (block(n), 0)),
            pl.BlockSpec((BLOCK, D_KV), lambda n: (prev(n), D_ATTN // D_KV)),
            pl.BlockSpec((BLOCK, D_KV), lambda n: (block(n), D_ATTN // D_KV)),
            pl.BlockSpec((BLOCK, D_KV), lambda n: (prev(n), D_ATTN // D_KV + 1)),
            pl.BlockSpec((BLOCK, D_KV), lambda n: (block(n), D_ATTN // D_KV + 1)),
            pl.BlockSpec(memory_space=pltpu.SMEM),
            pl.BlockSpec((BLOCK, 2 * BLOCK), lambda n: (0, 0)),
            pl.BlockSpec((1, N_HEADS), lambda n: (0, 0))]


def attn_bwd(proj, rel_bias, bucket, sinks, d_attn):
    last = N_BLOCKS - 1

    def body(q_ref, kp_ref, kc_ref, vp_ref, vc_ref, rb_ref, bk_ref, sk_ref, do_ref,
             dqkv_ref, drb_ref, dsk_ref, bias_ref, dbias_ref, dk_next, dv_next):
        n = pl.program_id(0)

        @pl.when(n == 0)
        def _():
            _build_bias(rb_ref, bk_ref[...], bias_ref)
            dbias_ref[...] = jnp.zeros_like(dbias_ref)
            dsk_ref[...] = jnp.zeros_like(dsk_ref)
            dk_next[...] = jnp.zeros_like(dk_next)
            dv_next[...] = jnp.zeros_like(dv_next)

        f = lambda q, kp, kc, vp, vc, bias, sk: _attn_block(q, kp, kc, vp, vc, bias, sk, last - n)
        _, vjp = jax.vjp(f, q_ref[...], kp_ref[...], kc_ref[...], vp_ref[...], vc_ref[...],
                         tuple(bias_ref[h] for h in range(N_HEADS)), sk_ref[...])
        dq, dkp, dkc, dvp, dvc, dbias, dsk = vjp(do_ref[...])
        dqkv_ref[:, 0:D_ATTN] = dq
        dqkv_ref[:, D_ATTN:D_ATTN + D_KV] = dkc + dk_next[...]
        dqkv_ref[:, D_ATTN + D_KV:D_QKV] = dvc + dv_next[...]
        dk_next[...] = dkp
        dv_next[...] = dvp
        for h in range(N_HEADS):
            dbias_ref[h] += dbias[h]
        dsk_ref[...] += dsk

        @pl.when(n == N_BLOCKS - 1)
        def _():
            bucket_v = bk_ref[...]
            rowi = lax.broadcasted_iota(jnp.int32, (N_BUCKETS, 2 * BLOCK), 0)
            lane = lax.broadcasted_iota(jnp.int32, (N_BUCKETS, N_HEADS), 1)
            out = jnp.zeros((N_BUCKETS, N_HEADS), F32)
            for hq in range(N_HEADS):
                dbh = dbias_ref[hq]
                rows = jnp.zeros((N_BUCKETS, 2 * BLOCK), F32)
                for b in range(N_BUCKETS):
                    part = jnp.sum(jnp.where(bucket_v == b, dbh, 0.0), axis=0, keepdims=True)
                    rows = jnp.where(rowi == b, part, rows)
                tot = jnp.sum(rows, axis=1, keepdims=True)
                out = jnp.where(lane == hq, tot, out)
            drb_ref[...] = out

    blk = lambda w: pl.BlockSpec((BLOCK, w), lambda n: (last - n, 0))
    return pl.pallas_call(
        body, grid=(N_BLOCKS,), name="attn_bwd",
        in_specs=_attn_in_specs(lambda n: last - n) + [blk(D_ATTN)],
        out_specs=[blk(D_QKV), pl.BlockSpec((N_BUCKETS, N_HEADS), lambda n: (0, 0)),
                   pl.BlockSpec((1, N_HEADS), lambda n: (0, 0))],
        out_shape=[_sds((SEQ, D_QKV)), _sds((N_BUCKETS, N_HEADS)), _sds((1, N_HEADS))],
        scratch_shapes=[pltpu.VMEM((N_HEADS, BLOCK, 2 * BLOCK), F32),
                        pltpu.VMEM((N_HEADS, BLOCK, 2 * BLOCK), F32),
                        pltpu.VMEM((BLOCK, D_KV), F32), pltpu.VMEM((BLOCK, D_KV), F32)],
        compiler_params=_params(("arbitrary",)),
    )(proj, proj, proj, proj, proj, rel_bias, bucket, sinks, d_attn)


def _stack(x, size):
    groups = x.shape[1] // size
    lane = lax.broadcasted_iota(jnp.int32, x.shape, 1) // size
    return jnp.concatenate([jnp.where(lane == i, x, 0.0) for i in range(groups)], axis=0)


def _neumann(l):
    c = CHUNK
    t = lax.broadcasted_iota(jnp.int32, l.shape, 0)
    i = lax.broadcasted_iota(jnp.int32, l.shape, 1) % c
    inv = (i == t).astype(F32) + l
    pw = dot1(l, _stack(l, c), NN)
    for _ in range(int(math.log2(c)) - 2):
        both = dot1(jnp.concatenate([inv, pw], axis=0), _stack(pw, c), NN)
        inv = inv + both[:c]
        pw = both[c:]
    return inv + dot1(inv, _stack(pw, c), NN)


@jax.custom_vjp
def neumann_inv(l):
    return _neumann(l)


def _neumann_fwd(l):
    inv = _neumann(l)
    return inv, inv


def _neumann_bwd(inv, g):
    c = CHUNK
    bd_t = _stack(inv, c).T
    inv_t = bd_t[0:c]
    for h in range(1, inv.shape[1] // c):
        inv_t = inv_t + bd_t[h * c:(h + 1) * c]
    return (dot1(dot1(inv_t, _stack(g, c), NN), bd_t, NN),)


neumann_inv.defvjp(_neumann_fwd, _neumann_bwd)


def _cumsum_raw(x, dims):
    c = x.shape[0]
    tt = lax.broadcasted_iota(jnp.int32, (c, c), 0)
    ii = lax.broadcasted_iota(jnp.int32, (c, c), 1)
    tri = (ii <= tt).astype(BF16)
    hi = x.astype(BF16)
    rest = x - hi.astype(F32)
    mid = rest.astype(BF16)
    lo = (rest - mid.astype(F32)).astype(BF16)
    return _dot(tri, hi, dims) + (_dot(tri, mid, dims) + _dot(tri, lo, dims))


@jax.custom_vjp
def cumsum_rows(x):
    return _cumsum_raw(x, NN)


def _cumsum_fwd(x):
    return _cumsum_raw(x, NN), None


def _cumsum_bwd(_, g):
    return (_cumsum_raw(g, TN),)


cumsum_rows.defvjp(_cumsum_fwd, _cumsum_bwd)


def _rwkv_chunk(s0, r, lw, k, v, kk, a):
    heads = r.shape[1] // HEAD_DIM
    c, hc = CHUNK, heads * CHUNK
    t = lax.broadcasted_iota(jnp.int32, (c, hc), 0)
    i = lax.broadcasted_iota(jnp.int32, (c, hc), 1) % c
    strict, incl = i < t, i <= t
    stack = lambda x: _stack(x, HEAD_DIM)
    ba = lax.broadcasted_iota(jnp.int32, s0.shape, 0) // HEAD_DIM
    bb = lax.broadcasted_iota(jnp.int32, s0.shape, 1) // HEAD_DIM
    blocks = (ba == bb).astype(F32)

    cum = cumsum_rows(lw)
    cum_end = jnp.sum(lw, axis=0, keepdims=True)
    beta = kk * a
    al = -kk * jnp.exp(cum - lw)
    p_inv = jnp.exp(-cum)
    be, kb, rb = beta * p_inv, k * p_inv, r * jnp.exp(cum)
    ar = jnp.concatenate([al, rb], axis=0)
    sv = stack(v)
    l_all = dot1(ar, jnp.concatenate([stack(be), stack(kb)], axis=0), NT)
    l_ab = jnp.where(strict, l_all[:c, :hc], 0.0)
    l_ak = jnp.where(strict, l_all[:c, hc:], 0.0)
    l_rb = jnp.where(incl, l_all[c:, :hc], 0.0)
    l_rk = jnp.where(incl, l_all[c:, hc:], 0.0)
    inv = neumann_inv(l_ab)
    from_s0 = dot1(ar, s0, NT)
    from_v = dot1(jnp.concatenate([l_ak, l_rk], axis=0), sv, NN)
    u = dot1(inv, stack(from_s0[:c] + from_v[:c]), NN)
    o = from_s0[c:] + from_v[c:] + dot1(l_rb, stack(u), NN)
    to_end = jnp.exp(cum_end - cum)
    s1 = s0 * jnp.exp(cum_end) + blocks * dot1(
        jnp.concatenate([u, v], axis=0), jnp.concatenate([beta * to_end, k * to_end], axis=0), TN)
    return o, s1


def call_with_comm(plan, middle_step, body, grid, name, in_specs, out_specs, out_shape, scratch_shapes, operands):
    n_in, n_out, n_scr = len(in_specs), len(out_specs), len(scratch_shapes)
    p_in, p_out = len(plan.ins), len(plan.out_shape)

    def fused(*refs):
        refs = list(refs)
        ins, refs = refs[:n_in], refs[n_in:]
        p_ins, refs = refs[:p_in], refs[p_in:]
        outs, refs = refs[:n_out], refs[n_out:]
        p_outs, refs = refs[:p_out], refs[p_out:]
        scr, p_sems = refs[:n_scr], refs[n_scr:]
        start, middle, finish = plan.stages(p_ins, p_outs, p_sems)
        step = pl.program_id(0)
        pl.when(step == 0)(start)
        body(*ins, *outs, *scr)
        pl.when(step == middle_step)(middle)
        pl.when(step == grid[0] - 1)(finish)

    any_spec = pl.BlockSpec(memory_space=pl.ANY)
    res = pl.pallas_call(
        fused, grid=grid, name=name,
        in_specs=list(in_specs) + [any_spec] * p_in, out_specs=list(out_specs) + [any_spec] * p_out,
        out_shape=list(out_shape) + list(plan.out_shape), scratch_shapes=list(scratch_shapes) + list(plan.scratch),
        compiler_params=_params(("arbitrary",)),
    )(*operands, *plan.ins)
    return res[:n_out], res[n_out:]


def _by_group(ref):
    return jnp.stack([ref[:, g * SCAN_WIDTH:(g + 1) * SCAN_WIDTH] for g in range(SCAN_GROUPS)])


def _store_groups(ref, val):
    for g in range(SCAN_GROUPS):
        ref[:, g * SCAN_WIDTH:(g + 1) * SCAN_WIDTH] = val[g]


def mixer_fwd(proj, rel_bias, bucket, sinks, r, lw, k, v, kk, a, plan):
    block = lambda c: jnp.minimum(c, N_BLOCKS - 1)

    def body(q_ref, kp_ref, kc_ref, vp_ref, vc_ref, rb_ref, bk_ref, sk_ref, r_ref, lw_ref, k_ref, v_ref, kk_ref,
             a_ref, attn_ref, o_ref, st_ref, bias_ref, s_ref):
        c = pl.program_id(0)

        @pl.when(c == 0)
        def _():
            _build_bias(rb_ref, bk_ref[...], bias_ref)
            s_ref[...] = jnp.zeros_like(s_ref)

        @pl.when(c < N_BLOCKS)
        def _():
            attn_ref[...] = _attn_block(q_ref[...], kp_ref[...], kc_ref[...], vp_ref[...], vc_ref[...],
                                        tuple(bias_ref[h] for h in range(N_HEADS)), sk_ref[...], c)

        s0 = s_ref[...]
        st_ref[0] = s0
        o, s1 = jax.vmap(_rwkv_chunk)(s0, *(_by_group(ref) for ref in (r_ref, lw_ref, k_ref, v_ref, kk_ref, a_ref)))
        _store_groups(o_ref, o)
        s_ref[...] = s1

    tb = pl.BlockSpec((CHUNK, D_RWKV), lambda c: (c, 0))
    state = (SCAN_GROUPS, SCAN_WIDTH, SCAN_WIDTH)
    return call_with_comm(
        plan, N_CHUNKS - 1, body, (N_CHUNKS,), "mixer_fwd", _attn_in_specs(block) + [tb] * 6,
        [pl.BlockSpec((BLOCK, D_ATTN), lambda c: (block(c), 0)), tb,
         pl.BlockSpec((1,) + state, lambda c: (c, 0, 0, 0))],
        [_sds((SEQ, D_ATTN)), _sds((SEQ, D_RWKV)), _sds((N_CHUNKS,) + state)],
        [pltpu.VMEM((N_HEADS, BLOCK, 2 * BLOCK), F32), pltpu.VMEM(state, F32)],
        (proj, proj, proj, proj, proj, rel_bias, bucket, sinks, r, lw, k, v, kk, a))


def rwkv_scan_bwd(r, lw, k, v, kk, a, states, d_o, plan):
    def body(r_ref, lw_ref, k_ref, v_ref, kk_ref, a_ref, st_ref, do_ref,
             dr_ref, dlw_ref, dk_ref, dv_ref, dkk_ref, da_ref, ds_ref):
        @pl.when(pl.program_id(0) == 0)
        def _():
            ds_ref[...] = jnp.zeros_like(ds_ref)

        _, vjp = jax.vjp(jax.vmap(_rwkv_chunk), st_ref[0],
                         *(_by_group(ref) for ref in (r_ref, lw_ref, k_ref, v_ref, kk_ref, a_ref)))
        grads = vjp((_by_group(do_ref), ds_ref[...]))
        ds_ref[...] = grads[0]
        for ref, val in zip((dr_ref, dlw_ref, dk_ref, dv_ref, dkk_ref, da_ref), grads[1:]):
            _store_groups(ref, val)

    last = N_CHUNKS - 1
    tb = pl.BlockSpec((CHUNK, D_RWKV), lambda c: (last - c, 0))
    state = (SCAN_GROUPS, SCAN_WIDTH, SCAN_WIDTH)
    return call_with_comm(
        plan, N_CHUNKS // 4, body, (N_CHUNKS,), "rwkv_scan_bwd",
        [tb] * 6 + [pl.BlockSpec((1,) + state, lambda c: (last - c, 0, 0, 0)), tb], [tb] * 6,
        [_sds((SEQ, D_RWKV))] * 6, [pltpu.VMEM(state, F32)], (r, lw, k, v, kk, a, states, d_o))


def _ffn_mid(ug, uv, cg, cv, bg, bv):
    conv_g = bg + cg[0] * shift_rows(ug, 2) + cg[1] * shift_rows(ug, 1) + cg[2] * ug
    conv_v = bv + cv[0] * shift_rows(uv, 2) + cv[1] * shift_rows(uv, 1) + cv[2] * uv
    return jax.nn.gelu(conv_g, approximate=True) * conv_v


def _conv_rows(ref):
    return tuple(ref[0, j:j + 1, :] for j in range(3))


def _ffn_specs(tile):
    per = D_MODEL // tile
    half = N_DEV // 2
    w_g = pl.BlockSpec((1, D_MODEL, tile), lambda t: (t // per, 0, t % per))
    w_v = pl.BlockSpec((1, D_MODEL, tile), lambda t: (half + t // per, 0, t % per))
    c_g = pl.BlockSpec((1, 3, tile), lambda t: (t // per, 0, t % per))
    c_v = pl.BlockSpec((1, 3, tile), lambda t: (half + t // per, 0, t % per))
    b_g = pl.BlockSpec((1, tile), lambda t: (0, t))
    b_v = pl.BlockSpec((1, tile), lambda t: (0, D_FF // tile + t))
    w_d = pl.BlockSpec((tile, D_MODEL), lambda t: (t, 0))
    return w_g, w_v, c_g, c_v, b_g, b_v, w_d


def ffn_fwd(h2, w_up, conv_w, conv_b, w_down):
    def body(h_ref, wg_ref, wv_ref, cg_ref, cv_ref, bg_ref, bv_ref, wd_ref, f_ref):
        @pl.when(pl.program_id(0) == 0)
        def _():
            f_ref[...] = jnp.zeros_like(f_ref)

        h = h_ref[...]
        act = _ffn_mid(_dot(h, wg_ref[0], NN), _dot(h, wv_ref[0], NN), _conv_rows(cg_ref), _conv_rows(cv_ref),
                       bg_ref[...], bv_ref[...])
        f_ref[...] += _dot(act.astype(BF16), wd_ref[...], NN)

    full = pl.BlockSpec((SEQ, D_MODEL), lambda t: (0, 0))
    return pl.pallas_call(
        body, grid=(D_FF // FF_TILE,), name="ffn_fwd",
        in_specs=[full, *_ffn_specs(FF_TILE)],
        out_specs=full, out_shape=_sds((SEQ, D_MODEL)),
        compiler_params=_params(("arbitrary",)),
    )(h2, w_up, w_up, conv_w, conv_w, conv_b, conv_b, w_down)


def ffn_bwd_mid(h2, w_up, conv_w, conv_b, w_down, df):
    tile, rows, halo = FF_TILE, FF_ROW_CHUNK, FF_HALO
    ext = rows + 2 * halo

    def body(h_hbm, wg_ref, wv_ref, cg_ref, cv_ref, bg_ref, bv_ref, wd_ref, df_hbm,
             dug_ref, duv_ref, dcg_ref, dcv_ref, dbg_ref, dbv_ref, dwd_ref,
             h_ref, df_ref, ug_ref, uv_ref, da_ref, act_ref):
        @pl.when(pl.program_id(0) == 0)
        def _():
            pltpu.sync_copy(h_hbm, h_ref)
            pltpu.sync_copy(df_hbm, df_ref)
            for ref in (ug_ref, uv_ref, da_ref):
                ref[0:halo, :] = jnp.zeros((halo, tile), F32)
                ref[halo + SEQ:, :] = jnp.zeros((halo, tile), F32)

        h, df_b = h_ref[...], df_ref[...]
        ug_ref[halo:halo + SEQ, :] = _dot(h, wg_ref[0], NN)
        uv_ref[halo:halo + SEQ, :] = _dot(h, wv_ref[0], NN)
        da_ref[halo:halo + SEQ, :] = _dot(df_b, wd_ref[...], NT)
        cg, cv, bg, bv = _conv_rows(cg_ref), _conv_rows(cv_ref), bg_ref[...], bv_ref[...]
        down = lambda x, n: pltpu.roll(x, n, 0)
        up = lambda x, n: pltpu.roll(x, ext - n, 0)
        mid = slice(halo, halo + rows)

        def chunk(i, sums):
            r0 = pl.multiple_of(i * rows, rows)
            window = pl.ds(r0, ext)
            ug, uv, da = ug_ref[window, :], uv_ref[window, :], da_ref[window, :]
            ug1, ug2, uv1, uv2 = down(ug, 1), down(ug, 2), down(uv, 1), down(uv, 2)
            conv_g = bg + cg[0] * ug2 + cg[1] * ug1 + cg[2] * ug
            conv_v = bv + cv[0] * uv2 + cv[1] * uv1 + cv[2] * uv
            act, vjp = jax.vjp(lambda a, b: jax.nn.gelu(a, approximate=True) * b, conv_g, conv_v)
            dcg, dcv = vjp(da)
            dug = cg[2] * dcg + cg[1] * up(dcg, 1) + cg[0] * up(dcg, 2)
            duv = cv[2] * dcv + cv[1] * up(dcv, 1) + cv[0] * up(dcv, 2)
            out = pl.ds(r0, rows)
            act_ref[out, :] = act[mid].astype(BF16)
            dug_ref[out, :] = dug[mid].astype(BF16)
            duv_ref[out, :] = duv[mid].astype(BF16)
            col = lambda x: jnp.sum(x[mid], axis=0, keepdims=True)
            new = (col(dcg * ug2), col(dcg * ug1), col(dcg * ug), col(dcv * uv2), col(dcv * uv1), col(dcv * uv),
                   col(dcg), col(dcv))
            return tuple(s + n for s, n in zip(sums, new))

        zero = jnp.zeros((1, tile), F32)
        sums = lax.fori_loop(0, SEQ // rows, chunk, (zero,) * 8)
        for j in range(3):
            dcg_ref[0, j:j + 1, :] = sums[j]
            dcv_ref[0, j:j + 1, :] = sums[3 + j]
        dbg_ref[...] = sums[6]
        dbv_ref[...] = sums[7]
        dwd_ref[...] = _dot(act_ref[...], df_b, TN).astype(BF16)

    hbm = pl.BlockSpec(memory_space=pl.ANY)
    w_g, w_v, c_g, c_v, b_g, b_v, w_d = _ffn_specs(tile)
    col = pl.BlockSpec((SEQ, tile), lambda t: (0, t))
    padded = pltpu.VMEM((SEQ + 2 * halo, tile), F32)
    return pl.pallas_call(
        body, grid=(D_FF // tile,), name="ffn_bwd_mid",
        in_specs=[hbm, w_g, w_v, c_g, c_v, b_g, b_v, w_d, hbm],
        out_specs=[col, col, c_g, c_v, b_g, b_v, w_d],
        out_shape=[_sds((SEQ, D_FF), BF16), _sds((SEQ, D_FF), BF16), _sds((N_DEV, 3, D_MODEL)),
                   _sds((N_DEV, 3, D_MODEL)), _sds((1, 2 * D_FF)), _sds((1, 2 * D_FF)), _sds((D_FF, D_MODEL), BF16)],
        scratch_shapes=[pltpu.VMEM((SEQ, D_MODEL), BF16), pltpu.VMEM((SEQ, D_MODEL), BF16), padded, padded, padded,
                        pltpu.VMEM((SEQ, tile), BF16)],
        compiler_params=_params(("arbitrary",)),
    )(h2, w_up, w_up, conv_w, conv_w, conv_b, conv_b, w_down, df)


def ffn_bwd_up(h2, w_up, dug, duv):
    tile = FF_TILE
    per = D_MODEL // tile

    def body(h_hbm, wg_ref, wv_ref, dug_ref, duv_ref, dh_hbm, dup_hbm, h_ref, dh_ref, dwg_ref, dwv_ref, sem, up_sems):
        t = pl.program_id(0)

        @pl.when(t == 0)
        def _():
            pltpu.sync_copy(h_hbm, h_ref)
            dh_ref[...] = jnp.zeros_like(dh_ref)

        h, dug_b, duv_b = h_ref[...], dug_ref[...], duv_ref[...]
        cols = pl.ds(pl.multiple_of((t % per) * tile, tile), tile)
        to_gate = pltpu.make_async_copy(dwg_ref, dup_hbm.at[t // per, :, cols], up_sems.at[0])
        to_value = pltpu.make_async_copy(dwv_ref, dup_hbm.at[N_DEV // 2 + t // per, :, cols], up_sems.at[1])
        dwg_ref[...] = _dot(h, dug_b, TN).astype(BF16)
        to_gate.start()
        dwv_ref[...] = _dot(h, duv_b, TN).astype(BF16)
        to_value.start()
        dh_ref[...] += _dot(jnp.concatenate([dug_b, duv_b], axis=1),
                            jnp.concatenate([wg_ref[0], wv_ref[0]], axis=1), NT)
        to_gate.wait()
        to_value.wait()

        @pl.when(t == D_FF // tile - 1)
        def _():
            cp = pltpu.make_async_copy(dh_ref, dh_hbm, sem)
            cp.start()
            cp.wait()

    hbm = pl.BlockSpec(memory_space=pl.ANY)
    w_g, w_v = _ffn_specs(tile)[:2]
    col = pl.BlockSpec((SEQ, tile), lambda t: (0, t))
    return pl.pallas_call(
        body, grid=(D_FF // tile,), name="ffn_bwd_up",
        in_specs=[hbm, w_g, w_v, col, col], out_specs=[hbm, hbm],
        out_shape=[_sds((SEQ, D_MODEL)), _sds((N_DEV, D_MODEL, D_MODEL), BF16)],
        scratch_shapes=[pltpu.VMEM((SEQ, D_MODEL), BF16), pltpu.VMEM((SEQ, D_MODEL), F32),
                        pltpu.VMEM((D_MODEL, tile), BF16), pltpu.VMEM((D_MODEL, tile), BF16),
                        pltpu.SemaphoreType.DMA, pltpu.SemaphoreType.DMA((2,))],
        compiler_params=_params(("arbitrary",)),
    )(h2, w_up, w_up, dug, duv)


def loss_head(x1, f, target, n_post):
    def tile_loss(x1_t, f_t, g, tgt):
        err = x1_t + _rms(f_t, g) - tgt
        return 0.5 * jnp.sum(jnp.mean(err * err, axis=-1))

    def body(x_ref, f_ref, t_ref, g_ref, dx_ref, df_ref, dg_ref, loss_ref):
        val, (dx, df, dg) = jax.value_and_grad(tile_loss, argnums=(0, 1, 2))(
            x_ref[...], f_ref[...], g_ref[...], t_ref[...])
        dx_ref[...] = dx
        df_ref[...] = df.astype(BF16)

        @pl.when(pl.program_id(0) == 0)
        def _():
            dg_ref[...] = jnp.zeros_like(dg_ref)
            loss_ref[...] = jnp.zeros_like(loss_ref)

        dg_ref[...] += dg
        loss_ref[...] += jnp.full((1, LANES), val, F32)

    tile = pl.BlockSpec((TOK_TILE, D_MODEL), lambda i: (i, 0))
    vec = pl.BlockSpec((1, D_MODEL), lambda i: (0, 0))
    return pl.pallas_call(
        body, grid=(SEQ // TOK_TILE,), name="loss_head",
        in_specs=[tile, tile, tile, vec],
        out_specs=[tile, tile, vec, pl.BlockSpec((1, LANES), lambda i: (0, 0))],
        out_shape=[_sds((SEQ, D_MODEL)), _sds((SEQ, D_MODEL), BF16), _sds((1, D_MODEL)), _sds((1, LANES))],
        compiler_params=_params(("arbitrary",)),
    )(x1, f, target, n_post)


def _mesh_pos():
    return lax.axis_index("x"), lax.axis_index("y"), lax.axis_index("c")


def _flip(pos, rel):
    x, y, c = pos
    return (1 - x if rel & 4 else x, 1 - y if rel & 2 else y, 1 - c if rel & 1 else c)


def _slot(pos):
    x, y, c = pos
    return 4 * x + 2 * y + c


def cast_bf16(w, rows):
    def body(w_ref, o_ref):
        o_ref[...] = w_ref[...].astype(BF16)

    spec = pl.BlockSpec((rows, w.shape[1]), lambda i: (i, 0))
    return pl.pallas_call(body, grid=(w.shape[0] // rows,), name="cast_bf16_%dx%d" % w.shape,
                          in_specs=[spec], out_specs=spec, out_shape=_sds(w.shape, BF16),
                          compiler_params=_params(("arbitrary",)))(w)


class CommPlan:
    def __init__(self, ins, out_shape, scratch, stages):
        self.ins, self.out_shape, self.scratch, self.stages = ins, out_shape, scratch, stages


def run_comm(name, plan):
    n_in, n_out = len(plan.ins), len(plan.out_shape)

    def body(*refs):
        for stage in plan.stages(refs[:n_in], refs[n_in:n_in + n_out], refs[n_in + n_out:]):
            stage()

    any_spec = pl.BlockSpec(memory_space=pl.ANY)
    return pl.pallas_call(
        body, name=name, in_specs=[any_spec] * len(plan.ins), out_specs=[any_spec] * len(plan.out_shape),
        out_shape=plan.out_shape, scratch_shapes=plan.scratch)(*plan.ins)


def gather_plan(shards):
    n = len(shards)

    def stages(srcs, outs, sems):
        send_sems, recv_sems, local_sems = sems

        def places():
            me = _mesh_pos()
            return me, _flip(me, 1), [_flip(me, 2), _flip(me, 4), _flip(me, 6)]

        def copy(a, k, block, to, src=None):
            dst = outs[a].at[_slot(block)]
            return pltpu.make_async_remote_copy(
                src_ref=dst if src is None else src, dst_ref=dst,
                send_sem=send_sems.at[7 * a + k], recv_sem=recv_sems.at[7 * a + k],
                device_id=to, device_id_type=pl.DeviceIdType.MESH)

        def local(a, me):
            return pltpu.make_async_copy(srcs[a], outs[a].at[_slot(me)], local_sems.at[a])

        def own(a, me, sibling, chips):
            return [copy(a, 0, me, sibling, src=srcs[a])] + [
                copy(a, 1 + j, me, chip, src=srcs[a]) for j, chip in enumerate(chips)]

        def start():
            me, sibling, chips = places()
            for a in range(n):
                local(a, me).start()
                for cp in own(a, me, sibling, chips):
                    cp.start()

        def forward():
            me, sibling, chips = places()
            for j, chip in enumerate(chips):
                for a in range(n):
                    copy(a, 1 + j, chip, me).wait_recv()
                    copy(a, 4 + j, chip, sibling).start()

        def finish():
            me, sibling, chips = places()
            for a in range(n):
                copy(a, 0, sibling, me).wait_recv()
                for j, chip in enumerate(chips):
                    copy(a, 4 + j, _flip(chip, 1), me).wait_recv()
            for a in range(n):
                for cp in own(a, me, sibling, chips):
                    cp.wait_send()
                for j, chip in enumerate(chips):
                    copy(a, 4 + j, chip, sibling).wait_send()
                local(a, me).wait()

        return start, forward, finish

    return CommPlan(list(shards), [_sds((N_DEV,) + s.shape, s.dtype) for s in shards],
                    [pltpu.SemaphoreType.DMA((7 * n,)), pltpu.SemaphoreType.DMA((7 * n,)),
                     pltpu.SemaphoreType.DMA((n,))], stages)


def exchange_plan(parts, replicated, rels, members, index, member_axis, own_copy):
    n, nr = len(parts), len(rels)
    pick_index = (slice(None),) * member_axis + (0,)
    subs = [1 if (r or member_axis == 0) else p.shape[0] for p, r in zip(parts, replicated)]
    first = [sum(subs[:a]) for a in range(n)]
    total = sum(subs)

    def stages(srcs, outs, sems):
        send_sems, recv_sems, local_sems = sems

        def src(a, s, pos):
            if replicated[a]:
                return srcs[a]
            return srcs[a].at[index(pos)] if member_axis == 0 else srcs[a].at[s, index(pos)]

        def dst(a, s, pos):
            block = outs[a].at[index(pos)]
            return block if (replicated[a] or member_axis == 0) else block.at[s]

        def copy(a, s, j, me, src_pos, dst_pos):
            sem = nr * (first[a] + s) + j
            return pltpu.make_async_remote_copy(
                src_ref=src(a, s, src_pos), dst_ref=dst(a, s, dst_pos),
                send_sem=send_sems.at[sem], recv_sem=recv_sems.at[sem],
                device_id=_flip(me, rels[j]), device_id_type=pl.DeviceIdType.MESH)

        pieces = [(a, s) for a in range(n) for s in range(subs[a])]

        def local(a, s, me):
            return pltpu.make_async_copy(src(a, s, me), dst(a, s, me), local_sems.at[first[a] + s])

        def sends(me):
            return [copy(a, s, j, me, _flip(me, rels[j]), me) for j in range(nr) for a, s in pieces]

        own = pieces if own_copy else []

        def start():
            me = _mesh_pos()
            for cp in sends(me) + [local(a, s, me) for a, s in own]:
                cp.start()

        def middle():
            pass

        def finish():
            me = _mesh_pos()
            for j in range(nr):
                for a, s in pieces:
                    copy(a, s, j, me, me, _flip(me, rels[j])).wait_recv()
            for cp in sends(me):
                cp.wait_send()
            for a, s in own:
                local(a, s, me).wait()

        return start, middle, finish

    shapes = [p.shape if r else jax.eval_shape(lambda t: t[pick_index], p).shape for p, r in zip(parts, replicated)]
    return CommPlan(list(parts), [_sds((members,) + s, p.dtype) for s, p in zip(shapes, parts)],
                    [pltpu.SemaphoreType.DMA((nr * total,)), pltpu.SemaphoreType.DMA((nr * total,)),
                     pltpu.SemaphoreType.DMA((total,))], stages)


def pair_plan(parts, replicated):
    return exchange_plan(parts, replicated, [1], 2, lambda pos: pos[2], 1, False)


def chip_plan(parts, replicated):
    return exchange_plan(parts, replicated, [2, 4, 6], 4, lambda pos: 2 * pos[0] + pos[1], 0, True)


def add_pair(name, mine, swapped, out_dtype, rows):
    def body(m_ref, s_ref, o_ref):
        own = m_ref[0, 0] if mine.ndim == 4 else m_ref[0]
        o_ref[0] = (own.astype(F32) + s_ref[0, 0].astype(F32)).astype(o_ref.dtype)

    _, n, r, c = swapped.shape
    core = lambda: lax.axis_index("c")
    if mine.ndim == 4:
        mine_spec = pl.BlockSpec((1, 1, rows, c), lambda i, j: (i, core(), j, 0))
    else:
        mine_spec = pl.BlockSpec((1, rows, c), lambda i, j: (i, j, 0))
    return pl.pallas_call(
        body, grid=(n, r // rows), name=name,
        in_specs=[mine_spec, pl.BlockSpec((1, 1, rows, c), lambda i, j: (1 - core(), i, j, 0))],
        out_specs=pl.BlockSpec((1, rows, c), lambda i, j: (i, j, 0)),
        out_shape=_sds((n, r, c), out_dtype),
        compiler_params=_params(("arbitrary", "arbitrary")),
    )(mine, swapped)


def add_pair_small(mines, swappeds):
    n = len(mines)
    halves = [m.ndim == s.ndim for m, s in zip(mines, swappeds)]

    def body(*refs):
        c = lax.axis_index("c")
        for i in range(n):
            m_ref, s_ref, o_ref = refs[i], refs[n + i], refs[2 * n + i]
            o_ref[...] = (m_ref[:, c] if halves[i] else m_ref[...]) + s_ref[1 - c]

    vmem = pl.BlockSpec(memory_space=pltpu.VMEM)
    return pl.pallas_call(
        body, name="pair_add_small", in_specs=[vmem] * (2 * n), out_specs=[vmem] * n,
        out_shape=[_sds(s.shape[1:]) for s in swappeds], compiler_params=_params(),
    )(*mines, *swappeds)


def _adamw_math(w, g, m, v):
    nm = ADAM_B1 * m + (1.0 - ADAM_B1) * g
    nv = ADAM_B2 * v + (1.0 - ADAM_B2) * (g * g)
    m_hat = nm / (1.0 - ADAM_B1 ** ADAM_STEP)
    v_hat = nv / (1.0 - ADAM_B2 ** ADAM_STEP)
    return -ADAM_LR * (m_hat / (jnp.sqrt(v_hat) + ADAM_EPS) + ADAM_WD * w), nm, nv


def adamw_small(ws, parts, ms, vs):
    n = len(ws)

    def body(*refs):
        for i in range(n):
            w_ref, p_ref, m_ref, v_ref = (refs[k * n + i] for k in range(4))
            g = p_ref[0]
            for j in range(1, p_ref.shape[0]):
                g = g + p_ref[j]
            delta, nm, nv = _adamw_math(w_ref[...], g, m_ref[...], v_ref[...])
            for k, val in enumerate((g, delta, nm, nv)):
                refs[(4 + k) * n + i][...] = val

    vmem = pl.BlockSpec(memory_space=pltpu.VMEM)
    outs = pl.pallas_call(
        body, name="adamw_small", in_specs=[vmem] * (4 * n), out_specs=[vmem] * (4 * n),
        out_shape=[_sds(w.shape) for w in ws] * 4, compiler_params=_params(),
    )(*ws, *parts, *ms, *vs)
    return [outs[k * n:(k + 1) * n] for k in range(4)]

def adamw(name, w, parts, m, v, rows, plan=None):
    n_parts = parts.shape[0]

    def body(w_ref, p_ref, m_ref, v_ref, g_ref, d_ref, nm_ref, nv_ref):
        g = p_ref[0].astype(F32)
        for j in range(1, n_parts):
            g = g + p_ref[j].astype(F32)
        g_ref[...] = g
        d_ref[...], nm_ref[...], nv_ref[...] = _adamw_math(w_ref[...], g, m_ref[...], v_ref[...])

    cols = w.shape[1]
    spec = pl.BlockSpec((rows, cols), lambda i: (i, 0))
    grid = (w.shape[0] // rows,)
    in_specs = [spec, pl.BlockSpec((n_parts, rows, cols), lambda i: (0, i, 0)), spec, spec]
    if plan is not None:
        return call_with_comm(plan, 0, body, grid, name, in_specs, [spec] * 4, [_sds(w.shape)] * 4, [],
                              (w, parts, m, v))
    return pl.pallas_call(
        body, grid=grid, name=name, in_specs=in_specs, out_specs=[spec] * 4, out_shape=[_sds(w.shape)] * 4,
        compiler_params=_params(("arbitrary",)),
    )(w, parts, m, v)


def _to_slots(full, per):
    return full.reshape(full.shape[0], N_DEV, per).transpose(1, 0, 2)


def _from_slots(slots):
    return slots.transpose(1, 0, 2).reshape(slots.shape[1], -1)


def kernel(x, norm_mix_pre, norm_mix_post, norm_ffn_pre, norm_ffn_post, w_in, rel_bias, sinks, rwkv_shift_mix, w0, w_decay_up, a0, w_iclr_up, w_gate_up, k_k, k_a, r_k, ln_x_g, ln_x_b, w_out, w_ffn_up, conv_w, conv_b, w_ffn_down, loss_target, m_norm_mix_pre, m_norm_mix_post, m_norm_ffn_pre, m_norm_ffn_post, m_w_in, m_rel_bias, m_sinks, m_rwkv_shift_mix, m_w0, m_w_decay_up, m_a0, m_w_iclr_up, m_w_gate_up, m_k_k, m_k_a, m_r_k, m_ln_x_g, m_ln_x_b, m_w_out, m_w_ffn_up, m_conv_w, m_conv_b, m_w_ffn_down, v_norm_mix_pre, v_norm_mix_post, v_norm_ffn_pre, v_norm_ffn_post, v_w_in, v_rel_bias, v_sinks, v_rwkv_shift_mix, v_w0, v_w_decay_up, v_a0, v_w_iclr_up, v_w_gate_up, v_k_k, v_k_a, v_r_k, v_ln_x_g, v_ln_x_b, v_w_out, v_w_ffn_up, v_conv_w, v_conv_b, v_w_ffn_down):
    x2 = x[0]
    target = loss_target[0]

    in_gather = gather_plan([cast_bf16(w_in[0], 256)])
    mixer_gather = gather_plan([cast_bf16(w_out[0], 128), w_decay_up[0], w_iclr_up[0], w_gate_up[0], conv_w[0]])
    ffn_gather = gather_plan([cast_bf16(w_ffn_up[0], 256), cast_bf16(w_ffn_down[0], 256)])
    mix_ext = jnp.concatenate([jnp.zeros((1, D_QKV), F32), rwkv_shift_mix], axis=1)
    r_k_row = r_k.reshape(1, D_RWKV)
    bucket = _bucket_table()

    (h1,), (g_in,) = tok_fwd("rms_mix_pre", rms_tile, [x2], [norm_mix_pre], [], [D_MODEL], [BF16], plan=in_gather)
    w_in_b = _from_slots(g_in)
    (proj, ps), (g_out, g_decay, g_iclr, g_gate, g_conv) = in_proj_fwd(h1, w_in_b, mix_ext, mixer_gather)
    w_out_b = g_out.reshape(D_MODEL, D_MODEL)
    lora = jnp.zeros((HEAD_DIM, D_RWKV), F32)
    wd_pad = jnp.concatenate([_from_slots(g_decay), lora], axis=0)
    wi_pad = jnp.concatenate([lora, _from_slots(g_iclr)], axis=0)
    wg_full = _from_slots(g_gate)
    pre_params = [w0, wd_pad, a0, wi_pad, wg_full, k_k, k_a]
    r_, lw_, k2_, v_, kk_, a_, gate_ = tok_fwd("rwkv_pre", rwkv_pre_tile, [ps], pre_params, [],
                                               [D_RWKV] * 7, [F32] * 7)
    (attn, o_, states), (g_up, g_down) = mixer_fwd(proj, rel_bias, bucket, sinks, r_, lw_, k2_, v_, kk_, a_,
                                                   ffn_gather)
    w_down_b = g_down.reshape(D_FF, D_MODEL)
    mix_tiles = [o_, r_, k2_, v_, gate_, attn, x2]
    mix_params = [w_out_b, norm_mix_post, ln_x_g, ln_x_b, r_k_row, norm_ffn_pre]
    x1, h2 = tok_fwd("mix_out", mix_out_tile, mix_tiles, mix_params, [(D_MODEL, D_MODEL)], [D_MODEL, D_MODEL],
                     [F32, BF16])
    f = ffn_fwd(h2, g_up, g_conv, conv_b, w_down_b)
    dy, df, d_n_ffn_post, loss_row = loss_head(x1, f, target, norm_ffn_post)

    d_ug, d_uv, d_cw_g, d_cw_v, d_cb_g, d_cb_v, d_down = ffn_bwd_mid(h2, g_up, g_conv, conv_b, w_down_b, df)
    dh2, d_up = ffn_bwd_up(h2, g_up, d_ug, d_uv)
    half = N_DEV // 2
    d_cw = jnp.concatenate([d_cw_g[:half], d_cw_v[half:]], axis=0)
    by_pair = lambda slots: slots.reshape((N_DEV // 2, 2) + slots.shape[1:])
    ffn_mine = [by_pair(d_up), by_pair(d_down.reshape(N_DEV, D_FF // N_DEV, D_MODEL))]
    ffn_swapped = run_comm("pair_exchange_ffn", pair_plan(ffn_mine, [False, False]))
    up_exchange = chip_plan([add_pair("pair_add_w_ffn_up", ffn_mine[0], ffn_swapped[0], BF16, 256)], [False])
    down_exchange = chip_plan([add_pair("pair_add_w_ffn_down", ffn_mine[1], ffn_swapped[1], BF16, 256)], [False])
    d_cb = jnp.concatenate([d_cb_g[:, :D_FF], d_cb_v[:, D_FF:]], axis=1)
    ((d_o, d_r1, d_k1, d_v1, d_gate, d_attn, dx_res, d_n_mix_post, d_ln_g, d_ln_b, d_r_k, d_n_ffn_pre, d_w_out),
     (got_down,)) = tok_bwd("mix_out_bwd", mix_out_tile, mix_tiles, mix_params, [(D_MODEL, D_MODEL)], [dy, dh2],
                            [1, 2, 3, 4, 5], plan=down_exchange)
    (d_r2, d_lw, d_k2, d_v2, d_kk, d_a), (got_up,) = rwkv_scan_bwd(
        r_, lw_, k2_, v_, kk_, a_, states, d_o, up_exchange)
    pre_cots = [(d_r1, d_r2), d_lw, (d_k1, d_k2), (d_v1, d_v2), d_kk, d_a, d_gate]
    (d_ps, d_w0, d_wd_pad, d_a0, d_wi_pad, d_wg, d_k_k, d_k_a) = tok_bwd(
        "rwkv_pre_bwd", rwkv_pre_tile, [ps], pre_params, [], pre_cots, [0, 1, 2, 3, 4, 5, 6])
    dqkv, d_rel_bias, d_sinks = attn_bwd(proj, rel_bias, bucket, sinks, d_attn)
    dh1, d_w_in, d_mix_ext = in_proj_bwd(h1, w_in_b, mix_ext, proj, dqkv, d_ps)
    grad_x2, d_n_mix_pre = tok_bwd("rms_mix_pre_bwd", rms_tile, [x2], [norm_mix_pre], [], [dh1], [0], {0: dx_res})
    grad_x = grad_x2[None]

    small_rep = [d_n_mix_pre, d_n_mix_post, d_n_ffn_pre, d_n_ffn_post, d_rel_bias, d_sinks,
                 d_mix_ext[:, D_QKV:], d_w0, d_a0, d_k_k, d_k_a, d_r_k.reshape(r_k.shape), d_ln_g, d_ln_b, d_cb]
    rep_w = [norm_mix_pre, norm_mix_post, norm_ffn_pre, norm_ffn_post, rel_bias, sinks, rwkv_shift_mix,
             w0, a0, k_k, k_a, r_k, ln_x_g, ln_x_b, conv_b]
    rep_m = [m_norm_mix_pre, m_norm_mix_post, m_norm_ffn_pre, m_norm_ffn_post, m_rel_bias, m_sinks,
             m_rwkv_shift_mix, m_w0, m_a0, m_k_k, m_k_a, m_r_k, m_ln_x_g, m_ln_x_b, m_conv_b]
    rep_v = [v_norm_mix_pre, v_norm_mix_post, v_norm_ffn_pre, v_norm_ffn_post, v_rel_bias, v_sinks,
             v_rwkv_shift_mix, v_w0, v_a0, v_k_k, v_k_a, v_r_k, v_ln_x_g, v_ln_x_b, v_conv_b]
    sh_w = [w_decay_up, w_iclr_up, w_gate_up, conv_w]
    sh_m = [m_w_decay_up, m_w_iclr_up, m_w_gate_up, m_conv_w]
    sh_v = [v_w_decay_up, v_w_iclr_up, v_w_gate_up, v_conv_w]
    sh_parts = [_to_slots(d_wd_pad[:HEAD_DIM], HEAD_DIM), _to_slots(d_wi_pad[HEAD_DIM:], HEAD_DIM),
                _to_slots(d_wg, HEAD_DIM), d_cw]
    n_rep, n_sh = len(small_rep), len(sh_parts)
    mine = [by_pair(_to_slots(d_w_in, D_IN // N_DEV)), by_pair(d_w_out.reshape(N_DEV, D_MODEL // N_DEV, D_MODEL)),
            *small_rep, *(by_pair(p) for p in sh_parts), loss_row]
    is_rep = [False, False] + [True] * n_rep + [False] * n_sh + [True]
    adam_down, swapped = adamw("adamw_w_ffn_down", w_ffn_down[0], got_down, m_w_ffn_down[0], v_w_ffn_down[0], 128,
                               pair_plan(mine, is_rep))
    chip_sums = [add_pair("pair_add_w_in", mine[0], swapped[0], BF16, 512),
                 add_pair("pair_add_w_out", mine[1], swapped[1], BF16, 128),
                 *add_pair_small(mine[2:], swapped[2:])]
    adam_up, got = adamw("adamw_w_ffn_up", w_ffn_up[0], got_up, m_w_ffn_up[0], v_w_ffn_up[0], 128,
                         chip_plan(chip_sums, is_rep))

    big = [adamw("adamw_w_in", w_in[0], got[0], m_w_in[0], v_w_in[0], 256),
           adamw("adamw_w_out", w_out[0], got[1], m_w_out[0], v_w_out[0], 128), adam_up, adam_down]
    loss = functools.reduce(jnp.add, [got[-1][q, 0, 0] for q in range(N_DEV // 2)])
    small_w, small_g = rep_w + sh_w, got[2:-1]
    as_grad = lambda arrays: [a.reshape(g.shape[1:]) for a, g in zip(arrays, small_g)]
    small = adamw_small(as_grad(small_w), small_g, as_grad(rep_m + sh_m), as_grad(rep_v + sh_v))
    small = [[a.reshape(w.shape) for a, w in zip(kind, small_w)] for kind in small]

    names = ["norm_mix_pre", "norm_mix_post", "norm_ffn_pre", "norm_ffn_post", "w_in", "rel_bias", "sinks",
             "rwkv_shift_mix", "w0", "w_decay_up", "a0", "w_iclr_up", "w_gate_up", "k_k", "k_a", "r_k",
             "ln_x_g", "ln_x_b", "w_out", "w_ffn_up", "conv_w", "conv_b", "w_ffn_down"]
    small_names = ["norm_mix_pre", "norm_mix_post", "norm_ffn_pre", "norm_ffn_post", "rel_bias", "sinks",
                   "rwkv_shift_mix", "w0", "a0", "k_k", "k_a", "r_k", "ln_x_g", "ln_x_b", "conv_b",
                   "w_decay_up", "w_iclr_up", "w_gate_up", "conv_w"]
    big_names = {"w_in": 0, "w_out": 1, "w_ffn_up": 2, "w_ffn_down": 3}
    outs = []
    for kind in range(4):
        for nm in names:
            if nm in big_names:
                outs.append(big[big_names[nm]][kind][None])
            else:
                outs.append(small[kind][small_names.index(nm)])
    return (loss, grad_x, *outs)
```

```python
import functools
import math

import jax
import jax.numpy as jnp
from jax import lax
from jax.experimental import pallas as pl
from jax.experimental.pallas import tpu as pltpu

F32 = jnp.float32
BF16 = jnp.bfloat16

N_DEV = 8
SEQ = 2048
D_MODEL = 1024
HEAD_DIM = 64
D_ATTN = 512
D_KV = 128
D_RWKV = 512
N_HEADS = 8
RWKV_COLS = 1792
D_QKV = D_ATTN + 2 * D_KV
D_IN = D_QKV + RWKV_COLS
D_FF = 4096
BLOCK = 128
N_BLOCKS = SEQ // BLOCK
N_BUCKETS = 32
MAX_DISTANCE = 128
NORM_EPS = 1e-6
GN_EPS = 64e-5
NEG_INF = -1e30
CHUNK = 64
N_CHUNKS = SEQ // CHUNK
SCAN_GROUPS = 4
SCAN_WIDTH = D_RWKV // SCAN_GROUPS
TOK_TILE = 256
FF_TILE = 256
FF_ROW_CHUNK = 256
FF_HALO = 8
COL_TILE = 256
LANES = 128
VMEM_LIMIT = 56 * 1024 * 1024

ADAM_LR = 0.001
ADAM_B1 = 0.9
ADAM_B2 = 0.999
ADAM_EPS = 1e-08
ADAM_WD = 0.01
ADAM_STEP = 10

NT = ((1,), (1,))
TN = ((0,), (0,))
NN = ((1,), (0,))


def _sds(shape, dtype=F32):
    return jax.ShapeDtypeStruct(shape, dtype)


def _params(sem=None):
    if sem is None:
        return pltpu.CompilerParams(vmem_limit_bytes=VMEM_LIMIT)
    return pltpu.CompilerParams(dimension_semantics=sem, vmem_limit_bytes=VMEM_LIMIT)


def _dot(a, b, dims):
    return lax.dot_general(a, b, (dims, ((), ())), preferred_element_type=F32)


def _split2(x):
    hi = x.astype(BF16)
    return hi, (x - hi.astype(F32)).astype(BF16)


def _dot3_raw(a, b, dims):
    ah, al = _split2(a)
    bh, bl = _split2(b)
    return _dot(ah, bh, dims) + (_dot(al, bh, dims) + _dot(ah, bl, dims))


@functools.partial(jax.custom_vjp, nondiff_argnums=(2,))
def dot3(a, b, dims):
    return _dot3_raw(a, b, dims)


def _dot3_fwd(a, b, dims):
    return _dot3_raw(a, b, dims), (a, b)


def _dot3_bwd(dims, res, g):
    a, b = res
    if dims == NN:
        return dot3(g, b, NT), dot3(a, g, TN)
    if dims == NT:
        return dot3(g, b, NN), dot3(g, a, TN)
    return dot3(b, g, NT), dot3(a, g, NN)


dot3.defvjp(_dot3_fwd, _dot3_bwd)


@functools.partial(jax.custom_vjp, nondiff_argnums=(2,))
def dot1(a, b, dims):
    return _dot(a.astype(BF16), b.astype(BF16), dims)


def _dot1_fwd(a, b, dims):
    return dot1(a, b, dims), (a, b)


def _dot1_bwd(dims, res, g):
    a, b = res
    if dims == NN:
        return dot1(g, b, NT), dot1(a, g, TN)
    if dims == NT:
        return dot1(g, b, NN), dot1(g, a, TN)
    return dot1(b, g, NT), dot1(a, g, NN)


dot1.defvjp(_dot1_fwd, _dot1_bwd)


@jax.custom_vjp
def mm(a, b):
    return _dot(a.astype(BF16), b.astype(BF16), NN)


def _mm_fwd(a, b):
    return mm(a, b), (a, b)


def _mm_bwd(res, g):
    a, b = res
    gb = g.astype(BF16)
    return _dot(gb, b.astype(BF16), NT).astype(a.dtype), _dot(a.astype(BF16), gb, TN).astype(b.dtype)


mm.defvjp(_mm_fwd, _mm_bwd)


@jax.custom_vjp
def mm_nt(a, b):
    return _dot(a.astype(BF16), b.astype(BF16), NT)


def _mm_nt_fwd(a, b):
    return mm_nt(a, b), (a, b)


def _mm_nt_bwd(res, g):
    a, b = res
    gb = g.astype(BF16)
    return _dot(gb, b.astype(BF16), NN).astype(a.dtype), _dot(gb, a.astype(BF16), TN).astype(b.dtype)


mm_nt.defvjp(_mm_nt_fwd, _mm_nt_bwd)


@jax.custom_vjp
def mmw(a, w, wz):
    return _dot(a.astype(BF16), w, NN)


def _mmw_fwd(a, w, wz):
    return mmw(a, w, wz), (a, w)


def _mmw_bwd(res, g):
    a, w = res
    gb = g.astype(BF16)
    return _dot(gb, w, NT).astype(a.dtype), jnp.zeros_like(w), _dot(a.astype(BF16), gb, TN)


mmw.defvjp(_mmw_fwd, _mmw_bwd)


def _shift_raw(x, n):
    rows = x.shape[0]
    rolled = pltpu.roll(x, n % rows, 0)
    idx = lax.broadcasted_iota(jnp.int32, x.shape, 0)
    keep = idx >= n if n > 0 else idx < rows + n
    return jnp.where(keep, rolled, 0.0)


@functools.partial(jax.custom_vjp, nondiff_argnums=(1,))
def shift_rows(x, n):
    return _shift_raw(x, n)


def _shift_fwd(x, n):
    return _shift_raw(x, n), None


def _shift_bwd(n, _, g):
    return (_shift_raw(g, -n),)


shift_rows.defvjp(_shift_fwd, _shift_bwd)


def _head_sum(x, scale):
    a = lax.broadcasted_iota(jnp.int32, (LANES, LANES), 0) // HEAD_DIM
    b = lax.broadcasted_iota(jnp.int32, (LANES, LANES), 1) // HEAD_DIM
    pair = jnp.where(a == b, scale, 0.0).astype(F32)
    return jnp.concatenate([dot3(x[:, i:i + LANES], pair, NN) for i in range(0, x.shape[1], LANES)], axis=1)


def _rms(x, g):
    return x * lax.rsqrt(jnp.mean(x * x, axis=-1, keepdims=True) + NORM_EPS) * g


def _softplus(x):
    return jnp.maximum(x, 0.0) + jnp.log(1.0 + jnp.exp(-jnp.abs(x)))


def _tile_spec(arr, tm):
    return pl.BlockSpec((tm, arr.shape[1]), lambda i: (i, 0))


def _full_spec(arr):
    nd = arr.ndim
    return pl.BlockSpec(arr.shape, lambda i: (0,) * nd)


def tok_fwd(name, fn, tiles, params, zero_shapes, out_widths, out_dtypes, plan=None, tm=TOK_TILE):
    n_t, n_p = len(tiles), len(params)

    def body(*refs):
        t_vals = [r[...] for r in refs[:n_t]]
        p_vals = [r[...] for r in refs[n_t:n_t + n_p]]
        z_vals = [jnp.zeros(s, F32) for s in zero_shapes]
        outs = fn(*t_vals, *p_vals, *z_vals)
        for r, o in zip(refs[n_t + n_p:], outs):
            r[...] = o.astype(r.dtype)

    rows = tiles[0].shape[0]
    steps = rows // tm
    in_specs = [_tile_spec(t, tm) for t in tiles] + [_full_spec(p) for p in params]
    out_specs = [pl.BlockSpec((tm, w), lambda i: (i, 0)) for w in out_widths]
    out_shape = [_sds((rows, w), dt) for w, dt in zip(out_widths, out_dtypes)]
    if plan is not None:
        return call_with_comm(plan, steps - 1, body, (steps,), name, in_specs, out_specs, out_shape, [],
                              (*tiles, *params))
    return pl.pallas_call(
        body, grid=(steps,), name=name, in_specs=in_specs, out_specs=out_specs, out_shape=out_shape,
        compiler_params=_params(("arbitrary",)),
    )(*tiles, *params)


def tok_bwd(name, fn, tiles, params, zero_shapes, cots, diff_params, residuals=(), plan=None, tm=TOK_TILE):
    cot_parts = [c if isinstance(c, tuple) else (c,) for c in cots]
    flat_cots = [a for part in cot_parts for a in part]
    residuals = dict(residuals)
    extra = [residuals[i] for i in sorted(residuals)]
    n_t, n_p, n_c, n_r = len(tiles), len(params), len(flat_cots), len(extra)
    acc_shapes = [params[i].shape for i in diff_params] + list(zero_shapes)

    def body(*refs):
        t_vals = [r[...].astype(F32) for r in refs[:n_t]]
        p_vals = [r[...] for r in refs[n_t:n_t + n_p]]
        flat = iter(r[...] for r in refs[n_t + n_p:n_t + n_p + n_c])
        c_vals = [functools.reduce(jnp.add, [next(flat) for _ in part]) for part in cot_parts]
        r_vals = dict(zip(sorted(residuals), (r[...] for r in refs[n_t + n_p + n_c:n_t + n_p + n_c + n_r])))
        out_refs = refs[n_t + n_p + n_c + n_r:]
        z_vals = [jnp.zeros(s, F32) for s in zero_shapes]
        d_vals = [p_vals[i] for i in diff_params]

        def f(t_in, d_in, z_in):
            full = list(p_vals)
            for i, v in zip(diff_params, d_in):
                full[i] = v
            return tuple(fn(*t_in, *full, *z_in))

        _, vjp = jax.vjp(f, t_vals, d_vals, z_vals)
        g_t, g_d, g_z = vjp(tuple(c_vals))
        for i, (r, g) in enumerate(zip(out_refs[:n_t], g_t)):
            r[...] = (g + r_vals[i] if i in r_vals else g).astype(r.dtype)
        acc_refs = out_refs[n_t:]

        @pl.when(pl.program_id(0) == 0)
        def _():
            for r in acc_refs:
                r[...] = jnp.zeros_like(r)

        for r, g in zip(acc_refs, list(g_d) + list(g_z)):
            r[...] += g

    rows = tiles[0].shape[0]
    in_specs = ([_tile_spec(t, tm) for t in tiles] + [_full_spec(p) for p in params]
                + [_tile_spec(c, tm) for c in flat_cots + extra])
    out_specs = ([_tile_spec(t, tm) for t in tiles]
                 + [pl.BlockSpec(s, lambda i, nd=len(s): (0,) * nd) for s in acc_shapes])
    out_shape = [_sds(t.shape) for t in tiles] + [_sds(s) for s in acc_shapes]
    operands = (*tiles, *params, *flat_cots, *extra)
    if plan is not None:
        return call_with_comm(plan, 0, body, (rows // tm,), name, in_specs, out_specs, out_shape, [], operands)
    return pl.pallas_call(
        body, grid=(rows // tm,), name=name, in_specs=in_specs, out_specs=out_specs, out_shape=out_shape,
        compiler_params=_params(("arbitrary",)),
    )(*operands)


def rms_tile(x, g):
    return (_rms(x, g),)


def rwkv_pre_tile(ps, w0, wd_pad, a0, wi_pad, wg, k_k, k_a):
    r = ps[:, 0:D_RWKV]
    k = ps[:, D_RWKV:2 * D_RWKV]
    v = ps[:, 2 * D_RWKV:3 * D_RWKV]
    z2 = ps[:, 3 * D_RWKV:3 * D_RWKV + LANES]
    zg = ps[:, 3 * D_RWKV + LANES:RWKV_COLS]
    w_log = -_softplus(-(w0 + mm(jnp.tanh(z2), wd_pad))) - 0.5
    lw = -jnp.exp(w_log)
    a = jax.nn.sigmoid(a0 + mm(z2, wi_pad))
    g = mm(jax.nn.sigmoid(zg), wg)
    kk = k * k_k
    norm = jnp.sqrt(_head_sum(kk * kk, 1.0))
    kk = kk / jnp.maximum(norm, 1e-12)
    k2 = k * (1.0 + (a - 1.0) * k_a)
    return r, lw, k2, v, kk, a, g


def mix_out_tile(o, r, k2, v, g, attn, x, w_out, n_post, ln_g, ln_b, r_k, n_ffn_pre, wz):
    d = o - _head_sum(o, 1.0 / HEAD_DIM)
    var = _head_sum(d * d, 1.0 / HEAD_DIM)
    on = d * lax.rsqrt(var + GN_EPS) * ln_g + ln_b
    bonus = _head_sum(r * k2 * r_k, 1.0) * v
    rw = (on + bonus) * g
    mix = mmw(jnp.concatenate([attn, rw], axis=1), w_out, wz)
    x1 = x + _rms(mix, n_post)
    return x1, _rms(x1, n_ffn_pre)


def in_proj_fwd(h, w_in, mix_ext, plan):
    def body(h_ref, w_ref, m_ref, proj_ref, ps_ref):
        p = _dot(h_ref[...], w_ref[...], NN)
        proj_ref[...] = p
        ps_ref[...] = p + (_shift_raw(p, 1) - p) * m_ref[...]

    n = D_IN // COL_TILE
    first = D_QKV // COL_TILE
    return call_with_comm(
        plan, n - 1, body, (n,), "in_proj_fwd",
        [pl.BlockSpec((SEQ, D_MODEL), lambda j: (0, 0)), pl.BlockSpec((D_MODEL, COL_TILE), lambda j: (0, j)),
         pl.BlockSpec((1, COL_TILE), lambda j: (0, j))],
        [pl.BlockSpec((SEQ, COL_TILE), lambda j: (0, j)),
         pl.BlockSpec((SEQ, COL_TILE), lambda j: (0, jnp.maximum(j - first, 0)))],
        [_sds((SEQ, D_IN)), _sds((SEQ, RWKV_COLS))], [], (h, w_in, mix_ext))


def in_proj_bwd(h, w_in, mix_ext, proj, dqkv, d_ps):
    first = D_QKV // COL_TILE

    def body(h_ref, w_ref, m_ref, p_ref, a_ref, r_ref, dh_ref, dw_ref, dm_ref):
        d = jnp.where(pl.program_id(0) < first, a_ref[...], r_ref[...])
        p = p_ref[...]
        dm_ref[...] = jnp.sum(d * (_shift_raw(p, 1) - p), axis=0, keepdims=True)
        dmix = d * m_ref[...]
        dp = (d - dmix + _shift_raw(dmix, -1)).astype(BF16)
        dw_ref[...] = _dot(h_ref[...], dp, TN)

        @pl.when(pl.program_id(0) == 0)
        def _():
            dh_ref[...] = jnp.zeros_like(dh_ref)

        dh_ref[...] += _dot(dp, w_ref[...], NT)

    n = D_IN // COL_TILE
    col = lambda rows: pl.BlockSpec((rows, COL_TILE), lambda j: (0, j))
    return pl.pallas_call(
        body, grid=(n,), name="in_proj_bwd",
        in_specs=[pl.BlockSpec((SEQ, D_MODEL), lambda j: (0, 0)), col(D_MODEL), col(1), col(SEQ),
                  pl.BlockSpec((SEQ, COL_TILE), lambda j: (0, jnp.minimum(j, first - 1))),
                  pl.BlockSpec((SEQ, COL_TILE), lambda j: (0, jnp.maximum(j - first, 0)))],
        out_specs=[pl.BlockSpec((SEQ, D_MODEL), lambda j: (0, 0)), col(D_MODEL), col(1)],
        out_shape=[_sds((SEQ, D_MODEL)), _sds((D_MODEL, D_IN)), _sds((1, D_IN))],
        compiler_params=_params(("arbitrary",)),
    )(h, w_in, mix_ext, proj, dqkv, d_ps)


def _bucket_table():
    rel = (jnp.arange(BLOCK)[:, None] + BLOCK) - jnp.arange(2 * BLOCK)[None, :]
    n = jnp.maximum(rel, 0)
    max_exact = N_BUCKETS // 2
    large = max_exact + (jnp.log(jnp.maximum(n, 1).astype(F32) / max_exact)
                         / math.log(MAX_DISTANCE / max_exact) * (N_BUCKETS - max_exact)).astype(jnp.int32)
    large = jnp.minimum(large, N_BUCKETS - 1)
    return jnp.where(n < max_exact, n, large).astype(jnp.int32)


def _select_matrix(g, o):
    a = lax.broadcasted_iota(jnp.int32, (D_KV, D_KV), 0)
    b = lax.broadcasted_iota(jnp.int32, (D_KV, D_KV), 1)
    return ((a - HEAD_DIM * g == b - o) & (b >= o) & (b < o + HEAD_DIM)).astype(F32)


def _attn_block(q, kp, kc, vp, vc, bias, sinks, block_idx):
    kb = jnp.concatenate([kp, kc], axis=0)
    vb = jnp.concatenate([vp, vc], axis=0)
    row = lax.broadcasted_iota(jnp.int32, (BLOCK, 2 * BLOCK), 0)
    col = lax.broadcasted_iota(jnp.int32, (BLOCK, 2 * BLOCK), 1)
    rel = row + BLOCK - col
    mask = (rel >= 0) & (rel < BLOCK) & (col + (block_idx - 1) * BLOCK >= 0)
    lane8 = lax.broadcasted_iota(jnp.int32, (1, N_HEADS), 1)
    kt, vt = {}, {}
    for g in range(2):
        for o in (0, HEAD_DIM):
            sel = _select_matrix(g, o)
            kt[g, o] = mm(kb, sel)
            vt[g, o] = mm(vb, sel)
    outs = []
    for j in range(D_ATTN // LANES):
        qs = q[:, j * LANES:(j + 1) * LANES]
        acc = None
        for half in range(2):
            hq = 2 * j + half
            g, o = hq // 4, half * HEAD_DIM
            s = mm_nt(qs, kt[g, o]) * (HEAD_DIM ** -0.5) + bias[hq]
            s = jnp.where(mask, s, NEG_INF)
            sink = jnp.sum(jnp.where(lane8 == hq, sinks, 0.0), axis=1, keepdims=True)
            m = lax.stop_gradient(jnp.maximum(jnp.max(s, axis=-1, keepdims=True), sink))
            p = jnp.exp(s - m)
            probs = p / (jnp.sum(p, axis=-1, keepdims=True) + jnp.exp(sink - m))
            part = mm(probs, vt[g, o])
            acc = part if acc is None else acc + part
        outs.append(acc)
    return jnp.concatenate(outs, axis=1)


def _build_bias(rb_ref, bucket, bias_ref):
    for hq in range(N_HEADS):
        acc = jnp.zeros((BLOCK, 2 * BLOCK), F32)
        for b in range(N_BUCKETS):
            acc = jnp.where(bucket == b, rb_ref[b, hq], acc)
        bias_ref[hq] = acc


def _attn_in_specs(block=lambda n: n):
    prev = lambda n: jnp.maximum(block(n) - 1, 0)
    return [pl.BlockSpec((BLOCK, D_ATTN), lambda n: (block(n), 0)),
            pl.BlockSpec((BLOCK, D_KV), lambda n: (prev(n), D_ATTN // D_KV)),
            pl.BlockSpec((BLOCK, D_KV), lambda n: (block(n), D_ATTN // D_KV)),
            pl.BlockSpec((BLOCK, D_KV), lambda n: (prev(n), D_ATTN // D_KV + 1)),
            pl.BlockSpec((BLOCK, D_KV), lambda n: (block(n), D_ATTN // D_KV + 1)),
            pl.BlockSpec(memory_space=pltpu.SMEM),
            pl.BlockSpec((BLOCK, 2 * BLOCK), lambda n: (0, 0)),
            pl.BlockSpec((1, N_HEADS), lambda n: (0, 0))]


def attn_bwd(proj, rel_bias, bucket, sinks, d_attn):
    last = N_BLOCKS - 1

    def body(q_ref, kp_ref, kc_ref, vp_ref, vc_ref, rb_ref, bk_ref, sk_ref, do_ref,
             dqkv_ref, drb_ref, dsk_ref, bias_ref, dbias_ref, dk_next, dv_next):
        n = pl.program_id(0)

        @pl.when(n == 0)
        def _():
            _build_bias(rb_ref, bk_ref[...], bias_ref)
            dbias_ref[...] = jnp.zeros_like(dbias_ref)
            dsk_ref[...] = jnp.zeros_like(dsk_ref)
            dk_next[...] = jnp.zeros_like(dk_next)
            dv_next[...] = jnp.zeros_like(dv_next)

        f = lambda q, kp, kc, vp, vc, bias, sk: _attn_block(q, kp, kc, vp, vc, bias, sk, last - n)
        _, vjp = jax.vjp(f, q_ref[...], kp_ref[...], kc_ref[...], vp_ref[...], vc_ref[...],
                         tuple(bias_ref[h] for h in range(N_HEADS)), sk_ref[...])
        dq, dkp, dkc, dvp, dvc, dbias, dsk = vjp(do_ref[...])
        dqkv_ref[:, 0:D_ATTN] = dq
        dqkv_ref[:, D_ATTN:D_ATTN + D_KV] = dkc + dk_next[...]
        dqkv_ref[:, D_ATTN + D_KV:D_QKV] = dvc + dv_next[...]
        dk_next[...] = dkp
        dv_next[...] = dvp
        for h in range(N_HEADS):
            dbias_ref[h] += dbias[h]
        dsk_ref[...] += dsk

        @pl.when(n == N_BLOCKS - 1)
        def _():
            bucket_v = bk_ref[...]
            rowi = lax.broadcasted_iota(jnp.int32, (N_BUCKETS, 2 * BLOCK), 0)
            lane = lax.broadcasted_iota(jnp.int32, (N_BUCKETS, N_HEADS), 1)
            out = jnp.zeros((N_BUCKETS, N_HEADS), F32)
            for hq in range(N_HEADS):
                dbh = dbias_ref[hq]
                rows = jnp.zeros((N_BUCKETS, 2 * BLOCK), F32)
                for b in range(N_BUCKETS):
                    part = jnp.sum(jnp.where(bucket_v == b, dbh, 0.0), axis=0, keepdims=True)
                    rows = jnp.where(rowi == b, part, rows)
                tot = jnp.sum(rows, axis=1, keepdims=True)
                out = jnp.where(lane == hq, tot, out)
            drb_ref[...] = out

    blk = lambda w: pl.BlockSpec((BLOCK, w), lambda n: (last - n, 0))
    return pl.pallas_call(
        body, grid=(N_BLOCKS,), name="attn_bwd",
        in_specs=_attn_in_specs(lambda n: last - n) + [blk(D_ATTN)],
        out_specs=[blk(D_QKV), pl.BlockSpec((N_BUCKETS, N_HEADS), lambda n: (0, 0)),
                   pl.BlockSpec((1, N_HEADS), lambda n: (0, 0))],
        out_shape=[_sds((SEQ, D_QKV)), _sds((N_BUCKETS, N_HEADS)), _sds((1, N_HEADS))],
        scratch_shapes=[pltpu.VMEM((N_HEADS, BLOCK, 2 * BLOCK), F32),
                        pltpu.VMEM((N_HEADS, BLOCK, 2 * BLOCK), F32),
                        pltpu.VMEM((BLOCK, D_KV), F32), pltpu.VMEM((BLOCK, D_KV), F32)],
        compiler_params=_params(("arbitrary",)),
    )(proj, proj, proj, proj, proj, rel_bias, bucket, sinks, d_attn)


def _stack(x, size):
    groups = x.shape[1] // size
    lane = lax.broadcasted_iota(jnp.int32, x.shape, 1) // size
    return jnp.concatenate([jnp.where(lane == i, x, 0.0) for i in range(groups)], axis=0)


def _neumann(l):
    c = CHUNK
    t = lax.broadcasted_iota(jnp.int32, l.shape, 0)
    i = lax.broadcasted_iota(jnp.int32, l.shape, 1) % c
    inv = (i == t).astype(F32) + l
    pw = dot1(l, _stack(l, c), NN)
    for _ in range(int(math.log2(c)) - 2):
        both = dot1(jnp.concatenate([inv, pw], axis=0), _stack(pw, c), NN)
        inv = inv + both[:c]
        pw = both[c:]
    return inv + dot1(inv, _stack(pw, c), NN)


@jax.custom_vjp
def neumann_inv(l):
    return _neumann(l)


def _neumann_fwd(l):
    inv = _neumann(l)
    return inv, inv


def _neumann_bwd(inv, g):
    c = CHUNK
    bd_t = _stack(inv, c).T
    inv_t = bd_t[0:c]
    for h in range(1, inv.shape[1] // c):
        inv_t = inv_t + bd_t[h * c:(h + 1) * c]
    return (dot1(dot1(inv_t, _stack(g, c), NN), bd_t, NN),)


neumann_inv.defvjp(_neumann_fwd, _neumann_bwd)


def _cumsum_raw(x, dims):
    c = x.shape[0]
    tt = lax.broadcasted_iota(jnp.int32, (c, c), 0)
    ii = lax.broadcasted_iota(jnp.int32, (c, c), 1)
    tri = (ii <= tt).astype(BF16)
    hi = x.astype(BF16)
    rest = x - hi.astype(F32)
    mid = rest.astype(BF16)
    lo = (rest - mid.astype(F32)).astype(BF16)
    return _dot(tri, hi, dims) + (_dot(tri, mid, dims) + _dot(tri, lo, dims))


@jax.custom_vjp
def cumsum_rows(x):
    return _cumsum_raw(x, NN)


def _cumsum_fwd(x):
    return _cumsum_raw(x, NN), None


def _cumsum_bwd(_, g):
    return (_cumsum_raw(g, TN),)


cumsum_rows.defvjp(_cumsum_fwd, _cumsum_bwd)


def _rwkv_chunk(s0, r, lw, k, v, kk, a):
    heads = r.shape[1] // HEAD_DIM
    c, hc = CHUNK, heads * CHUNK
    t = lax.broadcasted_iota(jnp.int32, (c, hc), 0)
    i = lax.broadcasted_iota(jnp.int32, (c, hc), 1) % c
    strict, incl = i < t, i <= t
    stack = lambda x: _stack(x, HEAD_DIM)
    ba = lax.broadcasted_iota(jnp.int32, s0.shape, 0) // HEAD_DIM
    bb = lax.broadcasted_iota(jnp.int32, s0.shape, 1) // HEAD_DIM
    blocks = (ba == bb).astype(F32)

    cum = cumsum_rows(lw)
    cum_end = jnp.sum(lw, axis=0, keepdims=True)
    beta = kk * a
    al = -kk * jnp.exp(cum - lw)
    p_inv = jnp.exp(-cum)
    be, kb, rb = beta * p_inv, k * p_inv, r * jnp.exp(cum)
    ar = jnp.concatenate([al, rb], axis=0)
    sv = stack(v)
    l_all = dot1(ar, jnp.concatenate([stack(be), stack(kb)], axis=0), NT)
    l_ab = jnp.where(strict, l_all[:c, :hc], 0.0)
    l_ak = jnp.where(strict, l_all[:c, hc:], 0.0)
    l_rb = jnp.where(incl, l_all[c:, :hc], 0.0)
    l_rk = jnp.where(incl, l_all[c:, hc:], 0.0)
    inv = neumann_inv(l_ab)
    from_s0 = dot1(ar, s0, NT)
    from_v = dot1(jnp.concatenate([l_ak, l_rk], axis=0), sv, NN)
    u = dot1(inv, stack(from_s0[:c] + from_v[:c]), NN)
    o = from_s0[c:] + from_v[c:] + dot1(l_rb, stack(u), NN)
    to_end = jnp.exp(cum_end - cum)
    s1 = s0 * jnp.exp(cum_end) + blocks * dot1(
        jnp.concatenate([u, v], axis=0), jnp.concatenate([beta * to_end, k * to_end], axis=0), TN)
    return o, s1


def call_with_comm(plan, middle_step, body, grid, name, in_specs, out_specs, out_shape, scratch_shapes, operands):
    n_in, n_out, n_scr = len(in_specs), len(out_specs), len(scratch_shapes)
    p_in, p_out = len(plan.ins), len(plan.out_shape)

    def fused(*refs):
        refs = list(refs)
        ins, refs = refs[:n_in], refs[n_in:]
        p_ins, refs = refs[:p_in], refs[p_in:]
        outs, refs = refs[:n_out], refs[n_out:]
        p_outs, refs = refs[:p_out], refs[p_out:]
        scr, p_sems = refs[:n_scr], refs[n_scr:]
        start, middle, finish = plan.stages(p_ins, p_outs, p_sems)
        step = pl.program_id(0)
        pl.when(step == 0)(start)
        body(*ins, *outs, *scr)
        pl.when(step == middle_step)(middle)
        pl.when(step == grid[0] - 1)(finish)

    any_spec = pl.BlockSpec(memory_space=pl.ANY)
    res = pl.pallas_call(
        fused, grid=grid, name=name,
        in_specs=list(in_specs) + [any_spec] * p_in, out_specs=list(out_specs) + [any_spec] * p_out,
        out_shape=list(out_shape) + list(plan.out_shape), scratch_shapes=list(scratch_shapes) + list(plan.scratch),
        compiler_params=_params(("arbitrary",)),
    )(*operands, *plan.ins)
    return res[:n_out], res[n_out:]


def _by_group(ref):
    return jnp.stack([ref[:, g * SCAN_WIDTH:(g + 1) * SCAN_WIDTH] for g in range(SCAN_GROUPS)])


def _store_groups(ref, val):
    for g in range(SCAN_GROUPS):
        ref[:, g * SCAN_WIDTH:(g + 1) * SCAN_WIDTH] = val[g]


def mixer_fwd(proj, rel_bias, bucket, sinks, r, lw, k, v, kk, a, plan):
    block = lambda c: jnp.minimum(c, N_BLOCKS - 1)

    def body(q_ref, kp_ref, kc_ref, vp_ref, vc_ref, rb_ref, bk_ref, sk_ref, r_ref, lw_ref, k_ref, v_ref, kk_ref,
             a_ref, attn_ref, o_ref, st_ref, bias_ref, s_ref):
        c = pl.program_id(0)

        @pl.when(c == 0)
        def _():
            _build_bias(rb_ref, bk_ref[...], bias_ref)
            s_ref[...] = jnp.zeros_like(s_ref)

        @pl.when(c < N_BLOCKS)
        def _():
            attn_ref[...] = _attn_block(q_ref[...], kp_ref[...], kc_ref[...], vp_ref[...], vc_ref[...],
                                        tuple(bias_ref[h] for h in range(N_HEADS)), sk_ref[...], c)

        s0 = s_ref[...]
        st_ref[0] = s0
        o, s1 = jax.vmap(_rwkv_chunk)(s0, *(_by_group(ref) for ref in (r_ref, lw_ref, k_ref, v_ref, kk_ref, a_ref)))
        _store_groups(o_ref, o)
        s_ref[...] = s1

    tb = pl.BlockSpec((CHUNK, D_RWKV), lambda c: (c, 0))
    state = (SCAN_GROUPS, SCAN_WIDTH, SCAN_WIDTH)
    return call_with_comm(
        plan, N_CHUNKS - 1, body, (N_CHUNKS,), "mixer_fwd", _attn_in_specs(block) + [tb] * 6,
        [pl.BlockSpec((BLOCK, D_ATTN), lambda c: (block(c), 0)), tb,
         pl.BlockSpec((1,) + state, lambda c: (c, 0, 0, 0))],
        [_sds((SEQ, D_ATTN)), _sds((SEQ, D_RWKV)), _sds((N_CHUNKS,) + state)],
        [pltpu.VMEM((N_HEADS, BLOCK, 2 * BLOCK), F32), pltpu.VMEM(state, F32)],
        (proj, proj, proj, proj, proj, rel_bias, bucket, sinks, r, lw, k, v, kk, a))


def rwkv_scan_bwd(r, lw, k, v, kk, a, states, d_o, plan):
    def body(r_ref, lw_ref, k_ref, v_ref, kk_ref, a_ref, st_ref, do_ref,
             dr_ref, dlw_ref, dk_ref, dv_ref, dkk_ref, da_ref, ds_ref):
        @pl.when(pl.program_id(0) == 0)
        def _():
            ds_ref[...] = jnp.zeros_like(ds_ref)

        _, vjp = jax.vjp(jax.vmap(_rwkv_chunk), st_ref[0],
                         *(_by_group(ref) for ref in (r_ref, lw_ref, k_ref, v_ref, kk_ref, a_ref)))
        grads = vjp((_by_group(do_ref), ds_ref[...]))
        ds_ref[...] = grads[0]
        for ref, val in zip((dr_ref, dlw_ref, dk_ref, dv_ref, dkk_ref, da_ref), grads[1:]):
            _store_groups(ref, val)

    last = N_CHUNKS - 1
    tb = pl.BlockSpec((CHUNK, D_RWKV), lambda c: (last - c, 0))
    state = (SCAN_GROUPS, SCAN_WIDTH, SCAN_WIDTH)
    return call_with_comm(
        plan, N_CHUNKS // 4, body, (N_CHUNKS,), "rwkv_scan_bwd",
        [tb] * 6 + [pl.BlockSpec((1,) + state, lambda c: (last - c, 0, 0, 0)), tb], [tb] * 6,
        [_sds((SEQ, D_RWKV))] * 6, [pltpu.VMEM(state, F32)], (r, lw, k, v, kk, a, states, d_o))


def _ffn_mid(ug, uv, cg, cv, bg, bv):
    conv_g = bg + cg[0] * shift_rows(ug, 2) + cg[1] * shift_rows(ug, 1) + cg[2] * ug
    conv_v = bv + cv[0] * shift_rows(uv, 2) + cv[1] * shift_rows(uv, 1) + cv[2] * uv
    return jax.nn.gelu(conv_g, approximate=True) * conv_v


def _conv_rows(ref):
    return tuple(ref[0, j:j + 1, :] for j in range(3))


def _ffn_specs(tile):
    per = D_MODEL // tile
    half = N_DEV // 2
    w_g = pl.BlockSpec((1, D_MODEL, tile), lambda t: (t // per, 0, t % per))
    w_v = pl.BlockSpec((1, D_MODEL, tile), lambda t: (half + t // per, 0, t % per))
    c_g = pl.BlockSpec((1, 3, tile), lambda t: (t // per, 0, t % per))
    c_v = pl.BlockSpec((1, 3, tile), lambda t: (half + t // per, 0, t % per))
    b_g = pl.BlockSpec((1, tile), lambda t: (0, t))
    b_v = pl.BlockSpec((1, tile), lambda t: (0, D_FF // tile + t))
    w_d = pl.BlockSpec((tile, D_MODEL), lambda t: (t, 0))
    return w_g, w_v, c_g, c_v, b_g, b_v, w_d


def ffn_fwd(h2, w_up, conv_w, conv_b, w_down):
    def body(h_ref, wg_ref, wv_ref, cg_ref, cv_ref, bg_ref, bv_ref, wd_ref, f_ref):
        @pl.when(pl.program_id(0) == 0)
        def _():
            f_ref[...] = jnp.zeros_like(f_ref)

        h = h_ref[...]
        act = _ffn_mid(_dot(h, wg_ref[0], NN), _dot(h, wv_ref[0], NN), _conv_rows(cg_ref), _conv_rows(cv_ref),
                       bg_ref[...], bv_ref[...])
        f_ref[...] += _dot(act.astype(BF16), wd_ref[...], NN)

    full = pl.BlockSpec((SEQ, D_MODEL), lambda t: (0, 0))
    return pl.pallas_call(
        body, grid=(D_FF // FF_TILE,), name="ffn_fwd",
        in_specs=[full, *_ffn_specs(FF_TILE)],
        out_specs=full, out_shape=_sds((SEQ, D_MODEL)),
        compiler_params=_params(("arbitrary",)),
    )(h2, w_up, w_up, conv_w, conv_w, conv_b, conv_b, w_down)


def ffn_bwd_mid(h2, w_up, conv_w, conv_b, w_down, df):
    tile, rows, halo = FF_TILE, FF_ROW_CHUNK, FF_HALO
    ext = rows + 2 * halo

    def body(h_hbm, wg_ref, wv_ref, cg_ref, cv_ref, bg_ref, bv_ref, wd_ref, df_hbm,
             dug_ref, duv_ref, dcg_ref, dcv_ref, dbg_ref, dbv_ref, dwd_ref,
             h_ref, df_ref, ug_ref, uv_ref, da_ref, act_ref):
        @pl.when(pl.program_id(0) == 0)
        def _():
            pltpu.sync_copy(h_hbm, h_ref)
            pltpu.sync_copy(df_hbm, df_ref)
            for ref in (ug_ref, uv_ref, da_ref):
                ref[0:halo, :] = jnp.zeros((halo, tile), F32)
                ref[halo + SEQ:, :] = jnp.zeros((halo, tile), F32)

        h, df_b = h_ref[...], df_ref[...]
        ug_ref[halo:halo + SEQ, :] = _dot(h, wg_ref[0], NN)
        uv_ref[halo:halo + SEQ, :] = _dot(h, wv_ref[0], NN)
        da_ref[halo:halo + SEQ, :] = _dot(df_b, wd_ref[...], NT)
        cg, cv, bg, bv = _conv_rows(cg_ref), _conv_rows(cv_ref), bg_ref[...], bv_ref[...]
        down = lambda x, n: pltpu.roll(x, n, 0)
        up = lambda x, n: pltpu.roll(x, ext - n, 0)
        mid = slice(halo, halo + rows)

        def chunk(i, sums):
            r0 = pl.multiple_of(i * rows, rows)
            window = pl.ds(r0, ext)
            ug, uv, da = ug_ref[window, :], uv_ref[window, :], da_ref[window, :]
            ug1, ug2, uv1, uv2 = down(ug, 1), down(ug, 2), down(uv, 1), down(uv, 2)
            conv_g = bg + cg[0] * ug2 + cg[1] * ug1 + cg[2] * ug
            conv_v = bv + cv[0] * uv2 + cv[1] * uv1 + cv[2] * uv
            act, vjp = jax.vjp(lambda a, b: jax.nn.gelu(a, approximate=True) * b, conv_g, conv_v)
            dcg, dcv = vjp(da)
            dug = cg[2] * dcg + cg[1] * up(dcg, 1) + cg[0] * up(dcg, 2)
            duv = cv[2] * dcv + cv[1] * up(dcv, 1) + cv[0] * up(dcv, 2)
            out = pl.ds(r0, rows)
            act_ref[out, :] = act[mid].astype(BF16)
            dug_ref[out, :] = dug[mid].astype(BF16)
            duv_ref[out, :] = duv[mid].astype(BF16)
            col = lambda x: jnp.sum(x[mid], axis=0, keepdims=True)
            new = (col(dcg * ug2), col(dcg * ug1), col(dcg * ug), col(dcv * uv2), col(dcv * uv1), col(dcv * uv),
                   col(dcg), col(dcv))
            return tuple(s + n for s, n in zip(sums, new))

        zero = jnp.zeros((1, tile), F32)
        sums = lax.fori_loop(0, SEQ // rows, chunk, (zero,) * 8)
        for j in range(3):
            dcg_ref[0, j:j + 1, :] = sums[j]
            dcv_ref[0, j:j + 1, :] = sums[3 + j]
        dbg_ref[...] = sums[6]
        dbv_ref[...] = sums[7]
        dwd_ref[...] = _dot(act_ref[...], df_b, TN).astype(BF16)

    hbm = pl.BlockSpec(memory_space=pl.ANY)
    w_g, w_v, c_g, c_v, b_g, b_v, w_d = _ffn_specs(tile)
    col = pl.BlockSpec((SEQ, tile), lambda t: (0, t))
    padded = pltpu.VMEM((SEQ + 2 * halo, tile), F32)
    return pl.pallas_call(
        body, grid=(D_FF // tile,), name="ffn_bwd_mid",
        in_specs=[hbm, w_g, w_v, c_g, c_v, b_g, b_v, w_d, hbm],
        out_specs=[col, col, c_g, c_v, b_g, b_v, w_d],
        out_shape=[_sds((SEQ, D_FF), BF16), _sds((SEQ, D_FF), BF16), _sds((N_DEV, 3, D_MODEL)),
                   _sds((N_DEV, 3, D_MODEL)), _sds((1, 2 * D_FF)), _sds((1, 2 * D_FF)), _sds((D_FF, D_MODEL), BF16)],
        scratch_shapes=[pltpu.VMEM((SEQ, D_MODEL), BF16), pltpu.VMEM((SEQ, D_MODEL), BF16), padded, padded, padded,
                        pltpu.VMEM((SEQ, tile), BF16)],
        compiler_params=_params(("arbitrary",)),
    )(h2, w_up, w_up, conv_w, conv_w, conv_b, conv_b, w_down, df)


def ffn_bwd_up(h2, w_up, dug, duv):
    tile = FF_TILE
    per = D_MODEL // tile

    def body(h_hbm, wg_ref, wv_ref, dug_ref, duv_ref, dh_hbm, dup_hbm, h_ref, dh_ref, dwg_ref, dwv_ref, sem, up_sems):
        t = pl.program_id(0)

        @pl.when(t == 0)
        def _():
            pltpu.sync_copy(h_hbm, h_ref)
            dh_ref[...] = jnp.zeros_like(dh_ref)

        h, dug_b, duv_b = h_ref[...], dug_ref[...], duv_ref[...]
        cols = pl.ds(pl.multiple_of((t % per) * tile, tile), tile)
        to_gate = pltpu.make_async_copy(dwg_ref, dup_hbm.at[t // per, :, cols], up_sems.at[0])
        to_value = pltpu.make_async_copy(dwv_ref, dup_hbm.at[N_DEV // 2 + t // per, :, cols], up_sems.at[1])
        dwg_ref[...] = _dot(h, dug_b, TN).astype(BF16)
        to_gate.start()
        dwv_ref[...] = _dot(h, duv_b, TN).astype(BF16)
        to_value.start()
        dh_ref[...] += _dot(jnp.concatenate([dug_b, duv_b], axis=1),
                            jnp.concatenate([wg_ref[0], wv_ref[0]], axis=1), NT)
        to_gate.wait()
        to_value.wait()

        @pl.when(t == D_FF // tile - 1)
        def _():
            cp = pltpu.make_async_copy(dh_ref, dh_hbm, sem)
            cp.start()
            cp.wait()

    hbm = pl.BlockSpec(memory_space=pl.ANY)
    w_g, w_v = _ffn_specs(tile)[:2]
    col = pl.BlockSpec((SEQ, tile), lambda t: (0, t))
    return pl.pallas_call(
        body, grid=(D_FF // tile,), name="ffn_bwd_up",
        in_specs=[hbm, w_g, w_v, col, col], out_specs=[hbm, hbm],
        out_shape=[_sds((SEQ, D_MODEL)), _sds((N_DEV, D_MODEL, D_MODEL), BF16)],
        scratch_shapes=[pltpu.VMEM((SEQ, D_MODEL), BF16), pltpu.VMEM((SEQ, D_MODEL), F32),
                        pltpu.VMEM((D_MODEL, tile), BF16), pltpu.VMEM((D_MODEL, tile), BF16),
                        pltpu.SemaphoreType.DMA, pltpu.SemaphoreType.DMA((2,))],
        compiler_params=_params(("arbitrary",)),
    )(h2, w_up, w_up, dug, duv)


def loss_head(x1, f, target, n_post):
    def tile_loss(x1_t, f_t, g, tgt):
        err = x1_t + _rms(f_t, g) - tgt
        return 0.5 * jnp.sum(jnp.mean(err * err, axis=-1))

    def body(x_ref, f_ref, t_ref, g_ref, dx_ref, df_ref, dg_ref, loss_ref):
        val, (dx, df, dg) = jax.value_and_grad(tile_loss, argnums=(0, 1, 2))(
            x_ref[...], f_ref[...], g_ref[...], t_ref[...])
        dx_ref[...] = dx
        df_ref[...] = df.astype(BF16)

        @pl.when(pl.program_id(0) == 0)
        def _():
            dg_ref[...] = jnp.zeros_like(dg_ref)
            loss_ref[...] = jnp.zeros_like(loss_ref)

        dg_ref[...] += dg
        loss_ref[...] += jnp.full((1, LANES), val, F32)

    tile = pl.BlockSpec((TOK_TILE, D_MODEL), lambda i: (i, 0))
    vec = pl.BlockSpec((1, D_MODEL), lambda i: (0, 0))
    return pl.pallas_call(
        body, grid=(SEQ // TOK_TILE,), name="loss_head",
        in_specs=[tile, tile, tile, vec],
        out_specs=[tile, tile, vec, pl.BlockSpec((1, LANES), lambda i: (0, 0))],
        out_shape=[_sds((SEQ, D_MODEL)), _sds((SEQ, D_MODEL), BF16), _sds((1, D_MODEL)), _sds((1, LANES))],
        compiler_params=_params(("arbitrary",)),
    )(x1, f, target, n_post)


def _mesh_pos():
    return lax.axis_index("x"), lax.axis_index("y"), lax.axis_index("c")


def _flip(pos, rel):
    x, y, c = pos
    return (1 - x if rel & 4 else x, 1 - y if rel & 2 else y, 1 - c if rel & 1 else c)


def _slot(pos):
    x, y, c = pos
    return 4 * x + 2 * y + c


def cast_bf16(w, rows):
    def body(w_ref, o_ref):
        o_ref[...] = w_ref[...].astype(BF16)

    spec = pl.BlockSpec((rows, w.shape[1]), lambda i: (i, 0))
    return pl.pallas_call(body, grid=(w.shape[0] // rows,), name="cast_bf16_%dx%d" % w.shape,
                          in_specs=[spec], out_specs=spec, out_shape=_sds(w.shape, BF16),
                          compiler_params=_params(("arbitrary",)))(w)


class CommPlan:
    def __init__(self, ins, out_shape, scratch, stages):
        self.ins, self.out_shape, self.scratch, self.stages = ins, out_shape, scratch, stages


def run_comm(name, plan):
    n_in, n_out = len(plan.ins), len(plan.out_shape)

    def body(*refs):
        for stage in plan.stages(refs[:n_in], refs[n_in:n_in + n_out], refs[n_in + n_out:]):
            stage()

    any_spec = pl.BlockSpec(memory_space=pl.ANY)
    return pl.pallas_call(
        body, name=name, in_specs=[any_spec] * len(plan.ins), out_specs=[any_spec] * len(plan.out_shape),
        out_shape=plan.out_shape, scratch_shapes=plan.scratch)(*plan.ins)


def gather_plan(shards):
    n = len(shards)

    def stages(srcs, outs, sems):
        send_sems, recv_sems, local_sems = sems

        def places():
            me = _mesh_pos()
            return me, _flip(me, 1), [_flip(me, 2), _flip(me, 4), _flip(me, 6)]

        def copy(a, k, block, to, src=None):
            dst = outs[a].at[_slot(block)]
            return pltpu.make_async_remote_copy(
                src_ref=dst if src is None else src, dst_ref=dst,
                send_sem=send_sems.at[7 * a + k], recv_sem=recv_sems.at[7 * a + k],
                device_id=to, device_id_type=pl.DeviceIdType.MESH)

        def local(a, me):
            return pltpu.make_async_copy(srcs[a], outs[a].at[_slot(me)], local_sems.at[a])

        def own(a, me, sibling, chips):
            return [copy(a, 0, me, sibling, src=srcs[a])] + [
                copy(a, 1 + j, me, chip, src=srcs[a]) for j, chip in enumerate(chips)]

        def start():
            me, sibling, chips = places()
            for a in range(n):
                local(a, me).start()
                for cp in own(a, me, sibling, chips):
                    cp.start()

        def forward():
            me, sibling, chips = places()
            for j, chip in enumerate(chips):
                for a in range(n):
                    copy(a, 1 + j, chip, me).wait_recv()
                    copy(a, 4 + j, chip, sibling).start()

        def finish():
            me, sibling, chips = places()
            for a in range(n):
                copy(a, 0, sibling, me).wait_recv()
                for j, chip in enumerate(chips):
                    copy(a, 4 + j, _flip(chip, 1), me).wait_recv()
            for a in range(n):
                for cp in own(a, me, sibling, chips):
                    cp.wait_send()
                for j, chip in enumerate(chips):
                    copy(a, 4 + j, chip, sibling).wait_send()
                local(a, me).wait()

        return start, forward, finish

    return CommPlan(list(shards), [_sds((N_DEV,) + s.shape, s.dtype) for s in shards],
                    [pltpu.SemaphoreType.DMA((7 * n,)), pltpu.SemaphoreType.DMA((7 * n,)),
                     pltpu.SemaphoreType.DMA((n,))], stages)


def exchange_plan(parts, replicated, rels, members, index, member_axis, own_copy):
    n, nr = len(parts), len(rels)
    pick_index = (slice(None),) * member_axis + (0,)
    subs = [1 if (r or member_axis == 0) else p.shape[0] for p, r in zip(parts, replicated)]
    first = [sum(subs[:a]) for a in range(n)]
    total = sum(subs)

    def stages(srcs, outs, sems):
        send_sems, recv_sems, local_sems = sems

        def src(a, s, pos):
            if replicated[a]:
                return srcs[a]
            return srcs[a].at[index(pos)] if member_axis == 0 else srcs[a].at[s, index(pos)]

        def dst(a, s, pos):
            block = outs[a].at[index(pos)]
            return block if (replicated[a] or member_axis == 0) else block.at[s]

        def copy(a, s, j, me, src_pos, dst_pos):
            sem = nr * (first[a] + s) + j
            return pltpu.make_async_remote_copy(
                src_ref=src(a, s, src_pos), dst_ref=dst(a, s, dst_pos),
                send_sem=send_sems.at[sem], recv_sem=recv_sems.at[sem],
                device_id=_flip(me, rels[j]), device_id_type=pl.DeviceIdType.MESH)

        pieces = [(a, s) for a in range(n) for s in range(subs[a])]

        def local(a, s, me):
            return pltpu.make_async_copy(src(a, s, me), dst(a, s, me), local_sems.at[first[a] + s])

        def sends(me):
            return [copy(a, s, j, me, _flip(me, rels[j]), me) for j in range(nr) for a, s in pieces]

        own = pieces if own_copy else []

        def start():
            me = _mesh_pos()
            for cp in sends(me) + [local(a, s, me) for a, s in own]:
                cp.start()

        def middle():
            pass

        def finish():
            me = _mesh_pos()
            for j in range(nr):
                for a, s in pieces:
                    copy(a, s, j, me, me, _flip(me, rels[j])).wait_recv()
            for cp in sends(me):
                cp.wait_send()
            for a, s in own:
                local(a, s, me).wait()

        return start, middle, finish

    shapes = [p.shape if r else jax.eval_shape(lambda t: t[pick_index], p).shape for p, r in zip(parts, replicated)]
    return CommPlan(list(parts), [_sds((members,) + s, p.dtype) for s, p in zip(shapes, parts)],
                    [pltpu.SemaphoreType.DMA((nr * total,)), pltpu.SemaphoreType.DMA((nr * total,)),
                     pltpu.SemaphoreType.DMA((total,))], stages)


def pair_plan(parts, replicated):
    return exchange_plan(parts, replicated, [1], 2, lambda pos: pos[2], 1, False)


def chip_plan(parts, replicated):
    return exchange_plan(parts, replicated, [2, 4, 6], 4, lambda pos: 2 * pos[0] + pos[1], 0, True)


def add_pair(name, mine, swapped, out_dtype, rows):
    def body(m_ref, s_ref, o_ref):
        own = m_ref[0, 0] if mine.ndim == 4 else m_ref[0]
        o_ref[0] = (own.astype(F32) + s_ref[0, 0].astype(F32)).astype(o_ref.dtype)

    _, n, r, c = swapped.shape
    core = lambda: lax.axis_index("c")
    if mine.ndim == 4:
        mine_spec = pl.BlockSpec((1, 1, rows, c), lambda i, j: (i, core(), j, 0))
    else:
        mine_spec = pl.BlockSpec((1, rows, c), lambda i, j: (i, j, 0))
    return pl.pallas_call(
        body, grid=(n, r // rows), name=name,
        in_specs=[mine_spec, pl.BlockSpec((1, 1, rows, c), lambda i, j: (1 - core(), i, j, 0))],
        out_specs=pl.BlockSpec((1, rows, c), lambda i, j: (i, j, 0)),
        out_shape=_sds((n, r, c), out_dtype),
        compiler_params=_params(("arbitrary", "arbitrary")),
    )(mine, swapped)


def add_pair_small(mines, swappeds):
    n = len(mines)
    halves = [m.ndim == s.ndim for m, s in zip(mines, swappeds)]

    def body(*refs):
        c = lax.axis_index("c")
        for i in range(n):
            m_ref, s_ref, o_ref = refs[i], refs[n + i], refs[2 * n + i]
            o_ref[...] = (m_ref[:, c] if halves[i] else m_ref[...]) + s_ref[1 - c]

    vmem = pl.BlockSpec(memory_space=pltpu.VMEM)
    return pl.pallas_call(
        body, name="pair_add_small", in_specs=[vmem] * (2 * n), out_specs=[vmem] * n,
        out_shape=[_sds(s.shape[1:]) for s in swappeds], compiler_params=_params(),
    )(*mines, *swappeds)


def _adamw_math(w, g, m, v):
    nm = ADAM_B1 * m + (1.0 - ADAM_B1) * g
    nv = ADAM_B2 * v + (1.0 - ADAM_B2) * (g * g)
    m_hat = nm / (1.0 - ADAM_B1 ** ADAM_STEP)
    v_hat = nv / (1.0 - ADAM_B2 ** ADAM_STEP)
    return -ADAM_LR * (m_hat / (jnp.sqrt(v_hat) + ADAM_EPS) + ADAM_WD * w), nm, nv


def adamw_small(ws, parts, ms, vs):
    n = len(ws)

    def body(*refs):
        for i in range(n):
            w_ref, p_ref, m_ref, v_ref = (refs[k * n + i] for k in range(4))
            g = p_ref[0]
            for j in range(1, p_ref.shape[0]):
                g = g + p_ref[j]
            delta, nm, nv = _adamw_math(w_ref[...], g, m_ref[...], v_ref[...])
            for k, val in enumerate((g, delta, nm, nv)):
                refs[(4 + k) * n + i][...] = val

    vmem = pl.BlockSpec(memory_space=pltpu.VMEM)
    outs = pl.pallas_call(
        body, name="adamw_small", in_specs=[vmem] * (4 * n), out_specs=[vmem] * (4 * n),
        out_shape=[_sds(w.shape) for w in ws] * 4, compiler_params=_params(),
    )(*ws, *parts, *ms, *vs)
    return [outs[k * n:(k + 1) * n] for k in range(4)]

def adamw(name, w, parts, m, v, rows, plan=None):
    n_parts = parts.shape[0]

    def body(w_ref, p_ref, m_ref, v_ref, g_ref, d_ref, nm_ref, nv_ref):
        g = p_ref[0].astype(F32)
        for j in range(1, n_parts):
            g = g + p_ref[j].astype(F32)
        g_ref[...] = g
        d_ref[...], nm_ref[...], nv_ref[...] = _adamw_math(w_ref[...], g, m_ref[...], v_ref[...])

    cols = w.shape[1]
    spec = pl.BlockSpec((rows, cols), lambda i: (i, 0))
    grid = (w.shape[0] // rows,)
    in_specs = [spec, pl.BlockSpec((n_parts, rows, cols), lambda i: (0, i, 0)), spec, spec]
    if plan is not None:
        return call_with_comm(plan, 0, body, grid, name, in_specs, [spec] * 4, [_sds(w.shape)] * 4, [],
                              (w, parts, m, v))
    return pl.pallas_call(
        body, grid=grid, name=name, in_specs=in_specs, out_specs=[spec] * 4, out_shape=[_sds(w.shape)] * 4,
        compiler_params=_params(("arbitrary",)),
    )(w, parts, m, v)


def _to_slots(full, per):
    return full.reshape(full.shape[0], N_DEV, per).transpose(1, 0, 2)


def _from_slots(slots):
    return slots.transpose(1, 0, 2).reshape(slots.shape[1], -1)


def kernel(x, norm_mix_pre, norm_mix_post, norm_ffn_pre, norm_ffn_post, w_in, rel_bias, sinks, rwkv_shift_mix, w0, w_decay_up, a0, w_iclr_up, w_gate_up, k_k, k_a, r_k, ln_x_g, ln_x_b, w_out, w_ffn_up, conv_w, conv_b, w_ffn_down, loss_target, m_norm_mix_pre, m_norm_mix_post, m_norm_ffn_pre, m_norm_ffn_post, m_w_in, m_rel_bias, m_sinks, m_rwkv_shift_mix, m_w0, m_w_decay_up, m_a0, m_w_iclr_up, m_w_gate_up, m_k_k, m_k_a, m_r_k, m_ln_x_g, m_ln_x_b, m_w_out, m_w_ffn_up, m_conv_w, m_conv_b, m_w_ffn_down, v_norm_mix_pre, v_norm_mix_post, v_norm_ffn_pre, v_norm_ffn_post, v_w_in, v_rel_bias, v_sinks, v_rwkv_shift_mix, v_w0, v_w_decay_up, v_a0, v_w_iclr_up, v_w_gate_up, v_k_k, v_k_a, v_r_k, v_ln_x_g, v_ln_x_b, v_w_out, v_w_ffn_up, v_conv_w, v_conv_b, v_w_ffn_down):
    x2 = x[0]
    target = loss_target[0]

    in_gather = gather_plan([cast_bf16(w_in[0], 256)])
    mixer_gather = gather_plan([cast_bf16(w_out[0], 128), w_decay_up[0], w_iclr_up[0], w_gate_up[0], conv_w[0]])
    ffn_gather = gather_plan([cast_bf16(w_ffn_up[0], 256), cast_bf16(w_ffn_down[0], 256)])
    mix_ext = jnp.concatenate([jnp.zeros((1, D_QKV), F32), rwkv_shift_mix], axis=1)
    r_k_row = r_k.reshape(1, D_RWKV)
    bucket = _bucket_table()

    (h1,), (g_in,) = tok_fwd("rms_mix_pre", rms_tile, [x2], [norm_mix_pre], [], [D_MODEL], [BF16], plan=in_gather)
    w_in_b = _from_slots(g_in)
    (proj, ps), (g_out, g_decay, g_iclr, g_gate, g_conv) = in_proj_fwd(h1, w_in_b, mix_ext, mixer_gather)
    w_out_b = g_out.reshape(D_MODEL, D_MODEL)
    lora = jnp.zeros((HEAD_DIM, D_RWKV), F32)
    wd_pad = jnp.concatenate([_from_slots(g_decay), lora], axis=0)
    wi_pad = jnp.concatenate([lora, _from_slots(g_iclr)], axis=0)
    wg_full = _from_slots(g_gate)
    pre_params = [w0, wd_pad, a0, wi_pad, wg_full, k_k, k_a]
    r_, lw_, k2_, v_, kk_, a_, gate_ = tok_fwd("rwkv_pre", rwkv_pre_tile, [ps], pre_params, [],
                                               [D_RWKV] * 7, [F32] * 7)
    (attn, o_, states), (g_up, g_down) = mixer_fwd(proj, rel_bias, bucket, sinks, r_, lw_, k2_, v_, kk_, a_,
                                                   ffn_gather)
    w_down_b = g_down.reshape(D_FF, D_MODEL)
    mix_tiles = [o_, r_, k2_, v_, gate_, attn, x2]
    mix_params = [w_out_b, norm_mix_post, ln_x_g, ln_x_b, r_k_row, norm_ffn_pre]
    x1, h2 = tok_fwd("mix_out", mix_out_tile, mix_tiles, mix_params, [(D_MODEL, D_MODEL)], [D_MODEL, D_MODEL],
                     [F32, BF16])
    f = ffn_fwd(h2, g_up, g_conv, conv_b, w_down_b)
    dy, df, d_n_ffn_post, loss_row = loss_head(x1, f, target, norm_ffn_post)

    d_ug, d_uv, d_cw_g, d_cw_v, d_cb_g, d_cb_v, d_down = ffn_bwd_mid(h2, g_up, g_conv, conv_b, w_down_b, df)
    dh2, d_up = ffn_bwd_up(h2, g_up, d_ug, d_uv)
    half = N_DEV // 2
    d_cw = jnp.concatenate([d_cw_g[:half], d_cw_v[half:]], axis=0)
    by_pair = lambda slots: slots.reshape((N_DEV // 2, 2) + slots.shape[1:])
    ffn_mine = [by_pair(d_up), by_pair(d_down.reshape(N_DEV, D_FF // N_DEV, D_MODEL))]
    ffn_swapped = run_comm("pair_exchange_ffn", pair_plan(ffn_mine, [False, False]))
    up_exchange = chip_plan([add_pair("pair_add_w_ffn_up", ffn_mine[0], ffn_swapped[0], BF16, 256)], [False])
    down_exchange = chip_plan([add_pair("pair_add_w_ffn_down", ffn_mine[1], ffn_swapped[1], BF16, 256)], [False])
    d_cb = jnp.concatenate([d_cb_g[:, :D_FF], d_cb_v[:, D_FF:]], axis=1)
    ((d_o, d_r1, d_k1, d_v1, d_gate, d_attn, dx_res, d_n_mix_post, d_ln_g, d_ln_b, d_r_k, d_n_ffn_pre, d_w_out),
     (got_down,)) = tok_bwd("mix_out_bwd", mix_out_tile, mix_tiles, mix_params, [(D_MODEL, D_MODEL)], [dy, dh2],
                            [1, 2, 3, 4, 5], plan=down_exchange)
    (d_r2, d_lw, d_k2, d_v2, d_kk, d_a), (got_up,) = rwkv_scan_bwd(
        r_, lw_, k2_, v_, kk_, a_, states, d_o, up_exchange)
    pre_cots = [(d_r1, d_r2), d_lw, (d_k1, d_k2), (d_v1, d_v2), d_kk, d_a, d_gate]
    (d_ps, d_w0, d_wd_pad, d_a0, d_wi_pad, d_wg, d_k_k, d_k_a) = tok_bwd(
        "rwkv_pre_bwd", rwkv_pre_tile, [ps], pre_params, [], pre_cots, [0, 1, 2, 3, 4, 5, 6])
    dqkv, d_rel_bias, d_sinks = attn_bwd(proj, rel_bias, bucket, sinks, d_attn)
    dh1, d_w_in, d_mix_ext = in_proj_bwd(h1, w_in_b, mix_ext, proj, dqkv, d_ps)
    grad_x2, d_n_mix_pre = tok_bwd("rms_mix_pre_bwd", rms_tile, [x2], [norm_mix_pre], [], [dh1], [0], {0: dx_res})
    grad_x = grad_x2[None]

    small_rep = [d_n_mix_pre, d_n_mix_post, d_n_ffn_pre, d_n_ffn_post, d_rel_bias, d_sinks,
                 d_mix_ext[:, D_QKV:], d_w0, d_a0, d_k_k, d_k_a, d_r_k.reshape(r_k.shape), d_ln_g, d_ln_b, d_cb]
    rep_w = [norm_mix_pre, norm_mix_post, norm_ffn_pre, norm_ffn_post, rel_bias, sinks, rwkv_shift_mix,
             w0, a0, k_k, k_a, r_k, ln_x_g, ln_x_b, conv_b]
    rep_m = [m_norm_mix_pre, m_norm_mix_post, m_norm_ffn_pre, m_norm_ffn_post, m_rel_bias, m_sinks,
             m_rwkv_shift_mix, m_w0, m_a0, m_k_k, m_k_a, m_r_k, m_ln_x_g, m_ln_x_b, m_conv_b]
    rep_v = [v_norm_mix_pre, v_norm_mix_post, v_norm_ffn_pre, v_norm_ffn_post, v_rel_bias, v_sinks,
             v_rwkv_shift_mix, v_w0, v_a0, v_k_k, v_k_a, v_r_k, v_ln_x_g, v_ln_x_b, v_conv_b]
    sh_w = [w_decay_up, w_iclr_up, w_gate_up, conv_w]
    sh_m = [m_w_decay_up, m_w_iclr_up, m_w_gate_up, m_conv_w]
    sh_v = [v_w_decay_up, v_w_iclr_up, v_w_gate_up, v_conv_w]
    sh_parts = [_to_slots(d_wd_pad[:HEAD_DIM], HEAD_DIM), _to_slots(d_wi_pad[HEAD_DIM:], HEAD_DIM),
                _to_slots(d_wg, HEAD_DIM), d_cw]
    n_rep, n_sh = len(small_rep), len(sh_parts)
    mine = [by_pair(_to_slots(d_w_in, D_IN // N_DEV)), by_pair(d_w_out.reshape(N_DEV, D_MODEL // N_DEV, D_MODEL)),
            *small_rep, *(by_pair(p) for p in sh_parts), loss_row]
    is_rep = [False, False] + [True] * n_rep + [False] * n_sh + [True]
    adam_down, swapped = adamw("adamw_w_ffn_down", w_ffn_down[0], got_down, m_w_ffn_down[0], v_w_ffn_down[0], 128,
                               pair_plan(mine, is_rep))
    chip_sums = [add_pair("pair_add_w_in", mine[0], swapped[0], BF16, 512),
                 add_pair("pair_add_w_out", mine[1], swapped[1], BF16, 128),
                 *add_pair_small(mine[2:], swapped[2:])]
    adam_up, got = adamw("adamw_w_ffn_up", w_ffn_up[0], got_up, m_w_ffn_up[0], v_w_ffn_up[0], 128,
                         chip_plan(chip_sums, is_rep))

    big = [adamw("adamw_w_in", w_in[0], got[0], m_w_in[0], v_w_in[0], 256),
           adamw("adamw_w_out", w_out[0], got[1], m_w_out[0], v_w_out[0], 128), adam_up, adam_down]
    loss = functools.reduce(jnp.add, [got[-1][q, 0, 0] for q in range(N_DEV // 2)])
    small_w, small_g = rep_w + sh_w, got[2:-1]
    as_grad = lambda arrays: [a.reshape(g.shape[1:]) for a, g in zip(arrays, small_g)]
    small = adamw_small(as_grad(small_w), small_g, as_grad(rep_m + sh_m), as_grad(rep_v + sh_v))
    small = [[a.reshape(w.shape) for a, w in zip(kind, small_w)] for kind in small]

    names = ["norm_mix_pre", "norm_mix_post", "norm_ffn_pre", "norm_ffn_post", "w_in", "rel_bias", "sinks",
             "rwkv_shift_mix", "w0", "w_decay_up", "a0", "w_iclr_up", "w_gate_up", "k_k", "k_a", "r_k",
             "ln_x_g", "ln_x_b", "w_out", "w_ffn_up", "conv_w", "conv_b", "w_ffn_down"]
    small_names = ["norm_mix_pre", "norm_mix_post", "norm_ffn_pre", "norm_ffn_post", "rel_bias", "sinks",
                   "rwkv_shift_mix", "w0", "a0", "k_k", "k_a", "r_k", "ln_x_g", "ln_x_b", "conv_b",
                   "w_decay_up", "w_iclr_up", "w_gate_up", "conv_w"]
    big_names = {"w_in": 0, "w_out": 1, "w_ffn_up": 2, "w_ffn_down": 3}
    outs = []
    for kind in range(4):
        for nm in names:
            if nm in big_names:
                outs.append(big[big_names[nm]][kind][None])
            else:
                outs.append(small[kind][small_names.index(nm)])
    return (loss, grad_x, *outs)
```

```python
import functools
import math

import jax
import jax.numpy as jnp
from jax import lax
from jax.experimental import pallas as pl
from jax.experimental.pallas import tpu as pltpu

F32 = jnp.float32
BF16 = jnp.bfloat16

N_DEV = 8
SEQ = 2048
D_MODEL = 1024
HEAD_DIM = 64
D_ATTN = 512
D_KV = 128
D_RWKV = 512
N_HEADS = 8
RWKV_COLS = 1792
D_QKV = D_ATTN + 2 * D_KV
D_IN = D_QKV + RWKV_COLS
D_FF = 4096
BLOCK = 128
N_BLOCKS = SEQ // BLOCK
N_BUCKETS = 32
MAX_DISTANCE = 128
NORM_EPS = 1e-6
GN_EPS = 64e-5
NEG_INF = -1e30
CHUNK = 64
N_CHUNKS = SEQ // CHUNK
SCAN_GROUPS = 4
SCAN_WIDTH = D_RWKV // SCAN_GROUPS
TOK_TILE = 256
FF_TILE = 256
FF_ROW_CHUNK = 256
FF_HALO = 8
COL_TILE = 256
LANES = 128
VMEM_LIMIT = 56 * 1024 * 1024

ADAM_LR = 0.001
ADAM_B1 = 0.9
ADAM_B2 = 0.999
ADAM_EPS = 1e-08
ADAM_WD = 0.01
ADAM_STEP = 10

NT = ((1,), (1,))
TN = ((0,), (0,))
NN = ((1,), (0,))


def _sds(shape, dtype=F32):
    return jax.ShapeDtypeStruct(shape, dtype)


def _params(sem=None):
    if sem is None:
        return pltpu.CompilerParams(vmem_limit_bytes=VMEM_LIMIT)
    return pltpu.CompilerParams(dimension_semantics=sem, vmem_limit_bytes=VMEM_LIMIT)


def _dot(a, b, dims):
    return lax.dot_general(a, b, (dims, ((), ())), preferred_element_type=F32)


def _split2(x):
    hi = x.astype(BF16)
    return hi, (x - hi.astype(F32)).astype(BF16)


def _dot3_raw(a, b, dims):
    ah, al = _split2(a)
    bh, bl = _split2(b)
    return _dot(ah, bh, dims) + (_dot(al, bh, dims) + _dot(ah, bl, dims))


@functools.partial(jax.custom_vjp, nondiff_argnums=(2,))
def dot3(a, b, dims):
    return _dot3_raw(a, b, dims)


def _dot3_fwd(a, b, dims):
    return _dot3_raw(a, b, dims), (a, b)


def _dot3_bwd(dims, res, g):
    a, b = res
    if dims == NN:
        return dot3(g, b, NT), dot3(a, g, TN)
    if dims == NT:
        return dot3(g, b, NN), dot3(g, a, TN)
    return dot3(b, g, NT), dot3(a, g, NN)


dot3.defvjp(_dot3_fwd, _dot3_bwd)


@functools.partial(jax.custom_vjp, nondiff_argnums=(2,))
def dot1(a, b, dims):
    return _dot(a.astype(BF16), b.astype(BF16), dims)


def _dot1_fwd(a, b, dims):
    return dot1(a, b, dims), (a, b)


def _dot1_bwd(dims, res, g):
    a, b = res
    if dims == NN:
        return dot1(g, b, NT), dot1(a, g, TN)
    if dims == NT:
        return dot1(g, b, NN), dot1(g, a, TN)
    return dot1(b, g, NT), dot1(a, g, NN)


dot1.defvjp(_dot1_fwd, _dot1_bwd)


@jax.custom_vjp
def mm(a, b):
    return _dot(a.astype(BF16), b.astype(BF16), NN)


def _mm_fwd(a, b):
    return mm(a, b), (a, b)


def _mm_bwd(res, g):
    a, b = res
    gb = g.astype(BF16)
    return _dot(gb, b.astype(BF16), NT).astype(a.dtype), _dot(a.astype(BF16), gb, TN).astype(b.dtype)


mm.defvjp(_mm_fwd, _mm_bwd)


@jax.custom_vjp
def mm_nt(a, b):
    return _dot(a.astype(BF16), b.astype(BF16), NT)


def _mm_nt_fwd(a, b):
    return mm_nt(a, b), (a, b)


def _mm_nt_bwd(res, g):
    a, b = res
    gb = g.astype(BF16)
    return _dot(gb, b.astype(BF16), NN).astype(a.dtype), _dot(gb, a.astype(BF16), TN).astype(b.dtype)


mm_nt.defvjp(_mm_nt_fwd, _mm_nt_bwd)


@jax.custom_vjp
def mmw(a, w, wz):
    return _dot(a.astype(BF16), w, NN)


def _mmw_fwd(a, w, wz):
    return mmw(a, w, wz), (a, w)


def _mmw_bwd(res, g):
    a, w = res
    gb = g.astype(BF16)
    return _dot(gb, w, NT).astype(a.dtype), jnp.zeros_like(w), _dot(a.astype(BF16), gb, TN)


mmw.defvjp(_mmw_fwd, _mmw_bwd)


def _shift_raw(x, n):
    rows = x.shape[0]
    rolled = pltpu.roll(x, n % rows, 0)
    idx = lax.broadcasted_iota(jnp.int32, x.shape, 0)
    keep = idx >= n if n > 0 else idx < rows + n
    return jnp.where(keep, rolled, 0.0)


@functools.partial(jax.custom_vjp, nondiff_argnums=(1,))
def shift_rows(x, n):
    return _shift_raw(x, n)


def _shift_fwd(x, n):
    return _shift_raw(x, n), None


def _shift_bwd(n, _, g):
    return (_shift_raw(g, -n),)


shift_rows.defvjp(_shift_fwd, _shift_bwd)


def _head_sum(x, scale):
    a = lax.broadcasted_iota(jnp.int32, (LANES, LANES), 0) // HEAD_DIM
    b = lax.broadcasted_iota(jnp.int32, (LANES, LANES), 1) // HEAD_DIM
    pair = jnp.where(a == b, scale, 0.0).astype(F32)
    return jnp.concatenate([dot3(x[:, i:i + LANES], pair, NN) for i in range(0, x.shape[1], LANES)], axis=1)


def _rms(x, g):
    return x * lax.rsqrt(jnp.mean(x * x, axis=-1, keepdims=True) + NORM_EPS) * g


def _softplus(x):
    return jnp.maximum(x, 0.0) + jnp.log(1.0 + jnp.exp(-jnp.abs(x)))


def _tile_spec(arr, tm):
    return pl.BlockSpec((tm, arr.shape[1]), lambda i: (i, 0))


def _full_spec(arr):
    nd = arr.ndim
    return pl.BlockSpec(arr.shape, lambda i: (0,) * nd)


def tok_fwd(name, fn, tiles, params, zero_shapes, out_widths, out_dtypes, plan=None, tm=TOK_TILE):
    n_t, n_p = len(tiles), len(params)

    def body(*refs):
        t_vals = [r[...] for r in refs[:n_t]]
        p_vals = [r[...] for r in refs[n_t:n_t + n_p]]
        z_vals = [jnp.zeros(s, F32) for s in zero_shapes]
        outs = fn(*t_vals, *p_vals, *z_vals)
        for r, o in zip(refs[n_t + n_p:], outs):
            r[...] = o.astype(r.dtype)

    rows = tiles[0].shape[0]
    steps = rows // tm
    in_specs = [_tile_spec(t, tm) for t in tiles] + [_full_spec(p) for p in params]
    out_specs = [pl.BlockSpec((tm, w), lambda i: (i, 0)) for w in out_widths]
    out_shape = [_sds((rows, w), dt) for w, dt in zip(out_widths, out_dtypes)]
    if plan is not None:
        return call_with_comm(plan, steps - 1, body, (steps,), name, in_specs, out_specs, out_shape, [],
                              (*tiles, *params))
    return pl.pallas_call(
        body, grid=(steps,), name=name, in_specs=in_specs, out_specs=out_specs, out_shape=out_shape,
        compiler_params=_params(("arbitrary",)),
    )(*tiles, *params)


def tok_bwd(name, fn, tiles, params, zero_shapes, cots, diff_params, residuals=(), plan=None, tm=TOK_TILE):
    cot_parts = [c if isinstance(c, tuple) else (c,) for c in cots]
    flat_cots = [a for part in cot_parts for a in part]
    residuals = dict(residuals)
    extra = [residuals[i] for i in sorted(residuals)]
    n_t, n_p, n_c, n_r = len(tiles), len(params), len(flat_cots), len(extra)
    acc_shapes = [params[i].shape for i in diff_params] + list(zero_shapes)

    def body(*refs):
        t_vals = [r[...].astype(F32) for r in refs[:n_t]]
        p_vals = [r[...] for r in refs[n_t:n_t + n_p]]
        flat = iter(r[...] for r in refs[n_t + n_p:n_t + n_p + n_c])
        c_vals = [functools.reduce(jnp.add, [next(flat) for _ in part]) for part in cot_parts]
        r_vals = dict(zip(sorted(residuals), (r[...] for r in refs[n_t + n_p + n_c:n_t + n_p + n_c + n_r])))
        out_refs = refs[n_t + n_p + n_c + n_r:]
        z_vals = [jnp.zeros(s, F32) for s in zero_shapes]
        d_vals = [p_vals[i] for i in diff_params]

        def f(t_in, d_in, z_in):
            full = list(p_vals)
            for i, v in zip(diff_params, d_in):
                full[i] = v
            return tuple(fn(*t_in, *full, *z_in))

        _, vjp = jax.vjp(f, t_vals, d_vals, z_vals)
        g_t, g_d, g_z = vjp(tuple(c_vals))
        for i, (r, g) in enumerate(zip(out_refs[:n_t], g_t)):
            r[...] = (g + r_vals[i] if i in r_vals else g).astype(r.dtype)
        acc_refs = out_refs[n_t:]

        @pl.when(pl.program_id(0) == 0)
        def _():
            for r in acc_refs:
                r[...] = jnp.zeros_like(r)

        for r, g in zip(acc_refs, list(g_d) + list(g_z)):
            r[...] += g

    rows = tiles[0].shape[0]
    in_specs = ([_tile_spec(t, tm) for t in tiles] + [_full_spec(p) for p in params]
                + [_tile_spec(c, tm) for c in flat_cots + extra])
    out_specs = ([_tile_spec(t, tm) for t in tiles]
                 + [pl.BlockSpec(s, lambda i, nd=len(s): (0,) * nd) for s in acc_shapes])
    out_shape = [_sds(t.shape) for t in tiles] + [_sds(s) for s in acc_shapes]
    operands = (*tiles, *params, *flat_cots, *extra)
    if plan is not None:
        return call_with_comm(plan, 0, body, (rows // tm,), name, in_specs, out_specs, out_shape, [], operands)
    return pl.pallas_call(
        body, grid=(rows // tm,), name=name, in_specs=in_specs, out_specs=out_specs, out_shape=out_shape,
        compiler_params=_params(("arbitrary",)),
    )(*operands)


def rms_tile(x, g):
    return (_rms(x, g),)


def rwkv_pre_tile(ps, w0, wd_pad, a0, wi_pad, wg, k_k, k_a):
    r = ps[:, 0:D_RWKV]
    k = ps[:, D_RWKV:2 * D_RWKV]
    v = ps[:, 2 * D_RWKV:3 * D_RWKV]
    z2 = ps[:, 3 * D_RWKV:3 * D_RWKV + LANES]
    zg = ps[:, 3 * D_RWKV + LANES:RWKV_COLS]
    w_log = -_softplus(-(w0 + mm(jnp.tanh(z2), wd_pad))) - 0.5
    lw = -jnp.exp(w_log)
    a = jax.nn.sigmoid(a0 + mm(z2, wi_pad))
    g = mm(jax.nn.sigmoid(zg), wg)
    kk = k * k_k
    norm = jnp.sqrt(_head_sum(kk * kk, 1.0))
    kk = kk / jnp.maximum(norm, 1e-12)
    k2 = k * (1.0 + (a - 1.0) * k_a)
    return r, lw, k2, v, kk, a, g


def mix_out_tile(o, r, k2, v, g, attn, x, w_out, n_post, ln_g, ln_b, r_k, n_ffn_pre, wz):
    d = o - _head_sum(o, 1.0 / HEAD_DIM)
    var = _head_sum(d * d, 1.0 / HEAD_DIM)
    on = d * lax.rsqrt(var + GN_EPS) * ln_g + ln_b
    bonus = _head_sum(r * k2 * r_k, 1.0) * v
    rw = (on + bonus) * g
    mix = mmw(jnp.concatenate([attn, rw], axis=1), w_out, wz)
    x1 = x + _rms(mix, n_post)
    return x1, _rms(x1, n_ffn_pre)


def in_proj_fwd(h, w_in, mix_ext, plan):
    def body(h_ref, w_ref, m_ref, proj_ref, ps_ref):
        p = _dot(h_ref[...], w_ref[...], NN)
        proj_ref[...] = p
        ps_ref[...] = p + (_shift_raw(p, 1) - p) * m_ref[...]

    n = D_IN // COL_TILE
    first = D_QKV // COL_TILE
    return call_with_comm(
        plan, n - 1, body, (n,), "in_proj_fwd",
        [pl.BlockSpec((SEQ, D_MODEL), lambda j: (0, 0)), pl.BlockSpec((D_MODEL, COL_TILE), lambda j: (0, j)),
         pl.BlockSpec((1, COL_TILE), lambda j: (0, j))],
        [pl.BlockSpec((SEQ, COL_TILE), lambda j: (0, j)),
         pl.BlockSpec((SEQ, COL_TILE), lambda j: (0, jnp.maximum(j - first, 0)))],
        [_sds((SEQ, D_IN)), _sds((SEQ, RWKV_COLS))], [], (h, w_in, mix_ext))


def in_proj_bwd(h, w_in, mix_ext, proj, dqkv, d_ps):
    first = D_QKV // COL_TILE

    def body(h_ref, w_ref, m_ref, p_ref, a_ref, r_ref, dh_ref, dw_ref, dm_ref):
        d = jnp.where(pl.program_id(0) < first, a_ref[...], r_ref[...])
        p = p_ref[...]
        dm_ref[...] = jnp.sum(d * (_shift_raw(p, 1) - p), axis=0, keepdims=True)
        dmix = d * m_ref[...]
        dp = (d - dmix + _shift_raw(dmix, -1)).astype(BF16)
        dw_ref[...] = _dot(h_ref[...], dp, TN)

        @pl.when(pl.program_id(0) == 0)
        def _():
            dh_ref[...] = jnp.zeros_like(dh_ref)

        dh_ref[...] += _dot(dp, w_ref[...], NT)

    n = D_IN // COL_TILE
    col = lambda rows: pl.BlockSpec((rows, COL_TILE), lambda j: (0, j))
    return pl.pallas_call(
        body, grid=(n,), name="in_proj_bwd",
        in_specs=[pl.BlockSpec((SEQ, D_MODEL), lambda j: (0, 0)), col(D_MODEL), col(1), col(SEQ),
                  pl.BlockSpec((SEQ, COL_TILE), lambda j: (0, jnp.minimum(j, first - 1))),
                  pl.BlockSpec((SEQ, COL_TILE), lambda j: (0, jnp.maximum(j - first, 0)))],
        out_specs=[pl.BlockSpec((SEQ, D_MODEL), lambda j: (0, 0)), col(D_MODEL), col(1)],
        out_shape=[_sds((SEQ, D_MODEL)), _sds((D_MODEL, D_IN)), _sds((1, D_IN))],
        compiler_params=_params(("arbitrary",)),
    )(h, w_in, mix_ext, proj, dqkv, d_ps)


def _bucket_table():
    rel = (jnp.arange(BLOCK)[:, None] + BLOCK) - jnp.arange(2 * BLOCK)[None, :]
    n = jnp.maximum(rel, 0)
    max_exact = N_BUCKETS // 2
    large = max_exact + (jnp.log(jnp.maximum(n, 1).astype(F32) / max_exact)
                         / math.log(MAX_DISTANCE / max_exact) * (N_BUCKETS - max_exact)).astype(jnp.int32)
    large = jnp.minimum(large, N_BUCKETS - 1)
    return jnp.where(n < max_exact, n, large).astype(jnp.int32)


def _select_matrix(g, o):
    a = lax.broadcasted_iota(jnp.int32, (D_KV, D_KV), 0)
    b = lax.broadcasted_iota(jnp.int32, (D_KV, D_KV), 1)
    return ((a - HEAD_DIM * g == b - o) & (b >= o) & (b < o + HEAD_DIM)).astype(F32)


def _attn_block(q, kp, kc, vp, vc, bias, sinks, block_idx):
    kb = jnp.concatenate([kp, kc], axis=0)
    vb = jnp.concatenate([vp, vc], axis=0)
    row = lax.broadcasted_iota(jnp.int32, (BLOCK, 2 * BLOCK), 0)
    col = lax.broadcasted_iota(jnp.int32, (BLOCK, 2 * BLOCK), 1)
    rel = row + BLOCK - col
    mask = (rel >= 0) & (rel < BLOCK) & (col + (block_idx - 1) * BLOCK >= 0)
    lane8 = lax.broadcasted_iota(jnp.int32, (1, N_HEADS), 1)
    kt, vt = {}, {}
    for g in range(2):
        for o in (0, HEAD_DIM):
            sel = _select_matrix(g, o)
            kt[g, o] = mm(kb, sel)
            vt[g, o] = mm(vb, sel)
    outs = []
    for j in range(D_ATTN // LANES):
        qs = q[:, j * LANES:(j + 1) * LANES]
        acc = None
        for half in range(2):
            hq = 2 * j + half
            g, o = hq // 4, half * HEAD_DIM
            s = mm_nt(qs, kt[g, o]) * (HEAD_DIM ** -0.5) + bias[hq]
            s = jnp.where(mask, s, NEG_INF)
            sink = jnp.sum(jnp.where(lane8 == hq, sinks, 0.0), axis=1, keepdims=True)
            m = lax.stop_gradient(jnp.maximum(jnp.max(s, axis=-1, keepdims=True), sink))
            p = jnp.exp(s - m)
            probs = p / (jnp.sum(p, axis=-1, keepdims=True) + jnp.exp(sink - m))
            part = mm(probs, vt[g, o])
            acc = part if acc is None else acc + part
        outs.append(acc)
    return jnp.concatenate(outs, axis=1)


def _build_bias(rb_ref, bucket, bias_ref):
    for hq in range(N_HEADS):
        acc = jnp.zeros((BLOCK, 2 * BLOCK), F32)
        for b in range(N_BUCKETS):
            acc = jnp.where(bucket == b, rb_ref[b, hq], acc)
        bias_ref[hq] = acc


def _attn_in_specs(block=lambda n: n):
    prev = lambda n: jnp.maximum(block(n) - 1, 0)
    return [pl.BlockSpec((BLOCK, D_ATTN), lambda n: (block(n), 0)),
            pl.BlockSpec((BLOCK, D_KV), lambda n: (prev(n), D_ATTN // D_KV)),
            pl.BlockSpec((BLOCK, D_KV), lambda n: (block(n), D_ATTN // D_KV)),
            pl.BlockSpec((BLOCK, D_KV), lambda n: (prev(n), D_ATTN // D_KV + 1)),
            pl.BlockSpec((BLOCK, D_KV), lambda n: (block(n), D_ATTN // D_KV + 1)),
            pl.BlockSpec(memory_space=pltpu.SMEM),
            pl.BlockSpec((BLOCK, 2 * BLOCK), lambda n: (0, 0)),
            pl.BlockSpec((1, N_HEADS), lambda n: (0, 0))]


def attn_bwd(proj, rel_bias, bucket, sinks, d_attn):
    last = N_BLOCKS - 1

    def body(q_ref, kp_ref, kc_ref, vp_ref, vc_ref, rb_ref, bk_ref, sk_ref, do_ref,
             dqkv_ref, drb_ref, dsk_ref, bias_ref, dbias_ref, dk_next, dv_next):
        n = pl.program_id(0)

        @pl.when(n == 0)
        def _():
            _build_bias(rb_ref, bk_ref[...], bias_ref)
            dbias_ref[...] = jnp.zeros_like(dbias_ref)
            dsk_ref[...] = jnp.zeros_like(dsk_ref)
            dk_next[...] = jnp.zeros_like(dk_next)
            dv_next[...] = jnp.zeros_like(dv_next)

        f = lambda q, kp, kc, vp, vc, bias, sk: _attn_block(q, kp, kc, vp, vc, bias, sk, last - n)
        _, vjp = jax.vjp(f, q_ref[...], kp_ref[...], kc_ref[...], vp_ref[...], vc_ref[...],
                         tuple(bias_ref[h] for h in range(N_HEADS)), sk_ref[...])
        dq, dkp, dkc, dvp, dvc, dbias, dsk = vjp(do_ref[...])
        dqkv_ref[:, 0:D_ATTN] = dq
        dqkv_ref[:, D_ATTN:D_ATTN + D_KV] = dkc + dk_next[...]
        dqkv_ref[:, D_ATTN + D_KV:D_QKV] = dvc + dv_next[...]
        dk_next[...] = dkp
        dv_next[...] = dvp
        for h in range(N_HEADS):
            dbias_ref[h] += dbias[h]
        dsk_ref[...] += dsk

        @pl.when(n == N_BLOCKS - 1)
        def _():
            bucket_v = bk_ref[...]
            rowi = lax.broadcasted_iota(jnp.int32, (N_BUCKETS, 2 * BLOCK), 0)
            lane = lax.broadcasted_iota(jnp.int32, (N_BUCKETS, N_HEADS), 1)
            out = jnp.zeros((N_BUCKETS, N_HEADS), F32)
            for hq in range(N_HEADS):
                dbh = dbias_ref[hq]
                rows = jnp.zeros((N_BUCKETS, 2 * BLOCK), F32)
                for b in range(N_BUCKETS):
                    part = jnp.sum(jnp.where(bucket_v == b, dbh, 0.0), axis=0, keepdims=True)
                    rows = jnp.where(rowi == b, part, rows)
                tot = jnp.sum(rows, axis=1, keepdims=True)
                out = jnp.where(lane == hq, tot, out)
            drb_ref[...] = out

    blk = lambda w: pl.BlockSpec((BLOCK, w), lambda n: (last - n, 0))
    return pl.pallas_call(
        body, grid=(N_BLOCKS,), name="attn_bwd",
        in_specs=_attn_in_specs(lambda n: last - n) + [blk(D_ATTN)],
        out_specs=[blk(D_QKV), pl.BlockSpec((N_BUCKETS, N_HEADS), lambda n: (0, 0)),
                   pl.BlockSpec((1, N_HEADS), lambda n: (0, 0))],
        out_shape=[_sds((SEQ, D_QKV)), _sds((N_BUCKETS, N_HEADS)), _sds((1, N_HEADS))],
        scratch_shapes=[pltpu.VMEM((N_HEADS, BLOCK, 2 * BLOCK), F32),
                        pltpu.VMEM((N_HEADS, BLOCK, 2 * BLOCK), F32),
                        pltpu.VMEM((BLOCK, D_KV), F32), pltpu.VMEM((BLOCK, D_KV), F32)],
        compiler_params=_params(("arbitrary",)),
    )(proj, proj, proj, proj, proj, rel_bias, bucket, sinks, d_attn)


def _stack(x, size):
    groups = x.shape[1] // size
    lane = lax.broadcasted_iota(jnp.int32, x.shape, 1) // size
    return jnp.concatenate([jnp.where(lane == i, x, 0.0) for i in range(groups)], axis=0)


def _neumann(l):
    c = CHUNK
    t = lax.broadcasted_iota(jnp.int32, l.shape, 0)
    i = lax.broadcasted_iota(jnp.int32, l.shape, 1) % c
    inv = (i == t).astype(F32) + l
    pw = dot1(l, _stack(l, c), NN)
    for _ in range(int(math.log2(c)) - 2):
        both = dot1(jnp.concatenate([inv, pw], axis=0), _stack(pw, c), NN)
        inv = inv + both[:c]
        pw = both[c:]
    return inv + dot1(inv, _stack(pw, c), NN)


@jax.custom_vjp
def neumann_inv(l):
    return _neumann(l)


def _neumann_fwd(l):
    inv = _neumann(l)
    return inv, inv


def _neumann_bwd(inv, g):
    c = CHUNK
    bd_t = _stack(inv, c).T
    inv_t = bd_t[0:c]
    for h in range(1, inv.shape[1] // c):
        inv_t = inv_t + bd_t[h * c:(h + 1) * c]
    return (dot1(dot1(inv_t, _stack(g, c), NN), bd_t, NN),)


neumann_inv.defvjp(_neumann_fwd, _neumann_bwd)


def _cumsum_raw(x, dims):
    c = x.shape[0]
    tt = lax.broadcasted_iota(jnp.int32, (c, c), 0)
    ii = lax.broadcasted_iota(jnp.int32, (c, c), 1)
    tri = (ii <= tt).astype(BF16)
    hi = x.astype(BF16)
    rest = x - hi.astype(F32)
    mid = rest.astype(BF16)
    lo = (rest - mid.astype(F32)).astype(BF16)
    return _dot(tri, hi, dims) + (_dot(tri, mid, dims) + _dot(tri, lo, dims))


@jax.custom_vjp
def cumsum_rows(x):
    return _cumsum_raw(x, NN)


def _cumsum_fwd(x):
    return _cumsum_raw(x, NN), None


def _cumsum_bwd(_, g):
    return (_cumsum_raw(g, TN),)


cumsum_rows.defvjp(_cumsum_fwd, _cumsum_bwd)


def _rwkv_chunk(s0, r, lw, k, v, kk, a):
    heads = r.shape[1] // HEAD_DIM
    c, hc = CHUNK, heads * CHUNK
    t = lax.broadcasted_iota(jnp.int32, (c, hc), 0)
    i = lax.broadcasted_iota(jnp.int32, (c, hc), 1) % c
    strict, incl = i < t, i <= t
    stack = lambda x: _stack(x, HEAD_DIM)
    ba = lax.broadcasted_iota(jnp.int32, s0.shape, 0) // HEAD_DIM
    bb = lax.broadcasted_iota(jnp.int32, s0.shape, 1) // HEAD_DIM
    blocks = (ba == bb).astype(F32)

    cum = cumsum_rows(lw)
    cum_end = jnp.sum(lw, axis=0, keepdims=True)
    beta = kk * a
    al = -kk * jnp.exp(cum - lw)
    p_inv = jnp.exp(-cum)
    be, kb, rb = beta * p_inv, k * p_inv, r * jnp.exp(cum)
    ar = jnp.concatenate([al, rb], axis=0)
    sv = stack(v)
    l_all = dot1(ar, jnp.concatenate([stack(be), stack(kb)], axis=0), NT)
    l_ab = jnp.where(strict, l_all[:c, :hc], 0.0)
    l_ak = jnp.where(strict, l_all[:c, hc:], 0.0)
    l_rb = jnp.where(incl, l_all[c:, :hc], 0.0)
    l_rk = jnp.where(incl, l_all[c:, hc:], 0.0)
    inv = neumann_inv(l_ab)
    from_s0 = dot1(ar, s0, NT)
    from_v = dot1(jnp.concatenate([l_ak, l_rk], axis=0), sv, NN)
    u = dot1(inv, stack(from_s0[:c] + from_v[:c]), NN)
    o = from_s0[c:] + from_v[c:] + dot1(l_rb, stack(u), NN)
    to_end = jnp.exp(cum_end - cum)
    s1 = s0 * jnp.exp(cum_end) + blocks * dot1(
        jnp.concatenate([u, v], axis=0), jnp.concatenate([beta * to_end, k * to_end], axis=0), TN)
    return o, s1


def call_with_comm(plan, middle_step, body, grid, name, in_specs, out_specs, out_shape, scratch_shapes, operands):
    n_in, n_out, n_scr = len(in_specs), len(out_specs), len(scratch_shapes)
    p_in, p_out = len(plan.ins), len(plan.out_shape)

    def fused(*refs):
        refs = list(refs)
        ins, refs = refs[:n_in], refs[n_in:]
        p_ins, refs = refs[:p_in], refs[p_in:]
        outs, refs = refs[:n_out], refs[n_out:]
        p_outs, refs = refs[:p_out], refs[p_out:]
        scr, p_sems = refs[:n_scr], refs[n_scr:]
        start, middle, finish = plan.stages(p_ins, p_outs, p_sems)
        step = pl.program_id(0)
        pl.when(step == 0)(start)
        body(*ins, *outs, *scr)
        pl.when(step == middle_step)(middle)
        pl.when(step == grid[0] - 1)(finish)

    any_spec = pl.BlockSpec(memory_space=pl.ANY)
    res = pl.pallas_call(
        fused, grid=grid, name=name,
        in_specs=list(in_specs) + [any_spec] * p_in, out_specs=list(out_specs) + [any_spec] * p_out,
        out_shape=list(out_shape) + list(plan.out_shape), scratch_shapes=list(scratch_shapes) + list(plan.scratch),
        compiler_params=_params(("arbitrary",)),
    )(*operands, *plan.ins)
    return res[:n_out], res[n_out:]


def _by_group(ref):
    return jnp.stack([ref[:, g * SCAN_WIDTH:(g + 1) * SCAN_WIDTH] for g in range(SCAN_GROUPS)])


def _store_groups(ref, val):
    for g in range(SCAN_GROUPS):
        ref[:, g * SCAN_WIDTH:(g + 1) * SCAN_WIDTH] = val[g]


def mixer_fwd(proj, rel_bias, bucket, sinks, r, lw, k, v, kk, a, plan):
    block = lambda c: jnp.minimum(c, N_BLOCKS - 1)

    def body(q_ref, kp_ref, kc_ref, vp_ref, vc_ref, rb_ref, bk_ref, sk_ref, r_ref, lw_ref, k_ref, v_ref, kk_ref,
             a_ref, attn_ref, o_ref, st_ref, bias_ref, s_ref):
        c = pl.program_id(0)

        @pl.when(c == 0)
        def _():
            _build_bias(rb_ref, bk_ref[...], bias_ref)
            s_ref[...] = jnp.zeros_like(s_ref)

        @pl.when(c < N_BLOCKS)
        def _():
            attn_ref[...] = _attn_block(q_ref[...], kp_ref[...], kc_ref[...], vp_ref[...], vc_ref[...],
                                        tuple(bias_ref[h] for h in range(N_HEADS)), sk_ref[...], c)

        s0 = s_ref[...]
        st_ref[0] = s0
        o, s1 = jax.vmap(_rwkv_chunk)(s0, *(_by_group(ref) for ref in (r_ref, lw_ref, k_ref, v_ref, kk_ref, a_ref)))
        _store_groups(o_ref, o)
        s_ref[...] = s1

    tb = pl.BlockSpec((CHUNK, D_RWKV), lambda c: (c, 0))
    state = (SCAN_GROUPS, SCAN_WIDTH, SCAN_WIDTH)
    return call_with_comm(
        plan, N_CHUNKS - 1, body, (N_CHUNKS,), "mixer_fwd", _attn_in_specs(block) + [tb] * 6,
        [pl.BlockSpec((BLOCK, D_ATTN), lambda c: (block(c), 0)), tb,
         pl.BlockSpec((1,) + state, lambda c: (c, 0, 0, 0))],
        [_sds((SEQ, D_ATTN)), _sds((SEQ, D_RWKV)), _sds((N_CHUNKS,) + state)],
        [pltpu.VMEM((N_HEADS, BLOCK, 2 * BLOCK), F32), pltpu.VMEM(state, F32)],
        (proj, proj, proj, proj, proj, rel_bias, bucket, sinks, r, lw, k, v, kk, a))


def rwkv_scan_bwd(r, lw, k, v, kk, a, states, d_o, plan):
    def body(r_ref, lw_ref, k_ref, v_ref, kk_ref, a_ref, st_ref, do_ref,
             dr_ref, dlw_ref, dk_ref, dv_ref, dkk_ref, da_ref, ds_ref):
        @pl.when(pl.program_id(0) == 0)
        def _():
            ds_ref[...] = jnp.zeros_like(ds_ref)

        _, vjp = jax.vjp(jax.vmap(_rwkv_chunk), st_ref[0],
                         *(_by_group(ref) for ref in (r_ref, lw_ref, k_ref, v_ref, kk_ref, a_ref)))
        grads = vjp((_by_group(do_ref), ds_ref[...]))
        ds_ref[...] = grads[0]
        for ref, val in zip((dr_ref, dlw_ref, dk_ref, dv_ref, dkk_ref, da_ref), grads[1:]):
            _store_groups(ref, val)

    last = N_CHUNKS - 1
    tb = pl.BlockSpec((CHUNK, D_RWKV), lambda c: (last - c, 0))
    state = (SCAN_GROUPS, SCAN_WIDTH, SCAN_WIDTH)
    return call_with_comm(
        plan, N_CHUNKS // 4, body, (N_CHUNKS,), "rwkv_scan_bwd",
        [tb] * 6 + [pl.BlockSpec((1,) + state, lambda c: (last - c, 0, 0, 0)), tb], [tb] * 6,
        [_sds((SEQ, D_RWKV))] * 6, [pltpu.VMEM(state, F32)], (r, lw, k, v, kk, a, states, d_o))


def _ffn_mid(ug, uv, cg, cv, bg, bv):
    conv_g = bg + cg[0] * shift_rows(ug, 2) + cg[1] * shift_rows(ug, 1) + cg[2] * ug
    conv_v = bv + cv[0] * shift_rows(uv, 2) + cv[1] * shift_rows(uv, 1) + cv[2] * uv
    return jax.nn.gelu(conv_g, approximate=True) * conv_v


def _conv_rows(ref):
    return tuple(ref[0, j:j + 1, :] for j in range(3))


def _ffn_specs(tile):
    per = D_MODEL // tile
    half = N_DEV // 2
    w_g = pl.BlockSpec((1, D_MODEL, tile), lambda t: (t // per, 0, t % per))
    w_v = pl.BlockSpec((1, D_MODEL, tile), lambda t: (half + t // per, 0, t % per))
    c_g = pl.BlockSpec((1, 3, tile), lambda t: (t // per, 0, t % per))
    c_v = pl.BlockSpec((1, 3, tile), lambda t: (half + t // per, 0, t % per))
    b_g = pl.BlockSpec((1, tile), lambda t: (0, t))
    b_v = pl.BlockSpec((1, tile), lambda t: (0, D_FF // tile + t))
    w_d = pl.BlockSpec((tile, D_MODEL), lambda t: (t, 0))
    return w_g, w_v, c_g, c_v, b_g, b_v, w_d


def ffn_fwd(h2, w_up, conv_w, conv_b, w_down):
    def body(h_ref, wg_ref, wv_ref, cg_ref, cv_ref, bg_ref, bv_ref, wd_ref, f_ref):
        @pl.when(pl.program_id(0) == 0)
        def _():
            f_ref[...] = jnp.zeros_like(f_ref)

        h = h_ref[...]
        act = _ffn_mid(_dot(h, wg_ref[0], NN), _dot(h, wv_ref[0], NN), _conv_rows(cg_ref), _conv_rows(cv_ref),
                       bg_ref[...], bv_ref[...])
        f_ref[...] += _dot(act.astype(BF16), wd_ref[...], NN)

    full = pl.BlockSpec((SEQ, D_MODEL), lambda t: (0, 0))
    return pl.pallas_call(
        body, grid=(D_FF // FF_TILE,), name="ffn_fwd",
        in_specs=[full, *_ffn_specs(FF_TILE)],
        out_specs=full, out_shape=_sds((SEQ, D_MODEL)),
        compiler_params=_params(("arbitrary",)),
    )(h2, w_up, w_up, conv_w, conv_w, conv_b, conv_b, w_down)


def ffn_bwd_mid(h2, w_up, conv_w, conv_b, w_down, df):
    tile, rows, halo = FF_TILE, FF_ROW_CHUNK, FF_HALO
    ext = rows + 2 * halo

    def body(h_hbm, wg_ref, wv_ref, cg_ref, cv_ref, bg_ref, bv_ref, wd_ref, df_hbm,
             dug_ref, duv_ref, dcg_ref, dcv_ref, dbg_ref, dbv_ref, dwd_ref,
             h_ref, df_ref, ug_ref, uv_ref, da_ref, act_ref):
        @pl.when(pl.program_id(0) == 0)
        def _():
            pltpu.sync_copy(h_hbm, h_ref)
            pltpu.sync_copy(df_hbm, df_ref)
            for ref in (ug_ref, uv_ref, da_ref):
                ref[0:halo, :] = jnp.zeros((halo, tile), F32)
                ref[halo + SEQ:, :] = jnp.zeros((halo, tile), F32)

        h, df_b = h_ref[...], df_ref[...]
        ug_ref[halo:halo + SEQ, :] = _dot(h, wg_ref[0], NN)
        uv_ref[halo:halo + SEQ, :] = _dot(h, wv_ref[0], NN)
        da_ref[halo:halo + SEQ, :] = _dot(df_b, wd_ref[...], NT)
        cg, cv, bg, bv = _conv_rows(cg_ref), _conv_rows(cv_ref), bg_ref[...], bv_ref[...]
        down = lambda x, n: pltpu.roll(x, n, 0)
        up = lambda x, n: pltpu.roll(x, ext - n, 0)
        mid = slice(halo, halo + rows)

        def chunk(i, sums):
            r0 = pl.multiple_of(i * rows, rows)
            window = pl.ds(r0, ext)
            ug, uv, da = ug_ref[window, :], uv_ref[window, :], da_ref[window, :]
            ug1, ug2, uv1, uv2 = down(ug, 1), down(ug, 2), down(uv, 1), down(uv, 2)
            conv_g = bg + cg[0] * ug2 + cg[1] * ug1 + cg[2] * ug
            conv_v = bv + cv[0] * uv2 + cv[1] * uv1 + cv[2] * uv
            act, vjp = jax.vjp(lambda a, b: jax.nn.gelu(a, approximate=True) * b, conv_g, conv_v)
            dcg, dcv = vjp(da)
            dug = cg[2] * dcg + cg[1] * up(dcg, 1) + cg[0] * up(dcg, 2)
            duv = cv[2] * dcv + cv[1] * up(dcv, 1) + cv[0] * up(dcv, 2)
            out = pl.ds(r0, rows)
            act_ref[out, :] = act[mid].astype(BF16)
            dug_ref[out, :] = dug[mid].astype(BF16)
            duv_ref[out, :] = duv[mid].astype(BF16)
            col = lambda x: jnp.sum(x[mid], axis=0, keepdims=True)
            new = (col(dcg * ug2), col(dcg * ug1), col(dcg * ug), col(dcv * uv2), col(dcv * uv1), col(dcv * uv),
                   col(dcg), col(dcv))
            return tuple(s + n for s, n in zip(sums, new))

        zero = jnp.zeros((1, tile), F32)
        sums = lax.fori_loop(0, SEQ // rows, chunk, (zero,) * 8)
        for j in range(3):
            dcg_ref[0, j:j + 1, :] = sums[j]
            dcv_ref[0, j:j + 1, :] = sums[3 + j]
        dbg_ref[...] = sums[6]
        dbv_ref[...] = sums[7]
        dwd_ref[...] = _dot(act_ref[...], df_b, TN).astype(BF16)

    hbm = pl.BlockSpec(memory_space=pl.ANY)
    w_g, w_v, c_g, c_v, b_g, b_v, w_d = _ffn_specs(tile)
    col = pl.BlockSpec((SEQ, tile), lambda t: (0, t))
    padded = pltpu.VMEM((SEQ + 2 * halo, tile), F32)
    return pl.pallas_call(
        body, grid=(D_FF // tile,), name="ffn_bwd_mid",
        in_specs=[hbm, w_g, w_v, c_g, c_v, b_g, b_v, w_d, hbm],
        out_specs=[col, col, c_g, c_v, b_g, b_v, w_d],
        out_shape=[_sds((SEQ, D_FF), BF16), _sds((SEQ, D_FF), BF16), _sds((N_DEV, 3, D_MODEL)),
                   _sds((N_DEV, 3, D_MODEL)), _sds((1, 2 * D_FF)), _sds((1, 2 * D_FF)), _sds((D_FF, D_MODEL), BF16)],
        scratch_shapes=[pltpu.VMEM((SEQ, D_MODEL), BF16), pltpu.VMEM((SEQ, D_MODEL), BF16), padded, padded, padded,
                        pltpu.VMEM((SEQ, tile), BF16)],
        compiler_params=_params(("arbitrary",)),
    )(h2, w_up, w_up, conv_w, conv_w, conv_b, conv_b, w_down, df)


def ffn_bwd_up(h2, w_up, dug, duv):
    tile = FF_TILE
    per = D_MODEL // tile

    def body(h_hbm, wg_ref, wv_ref, dug_ref, duv_ref, dh_hbm, dup_hbm, h_ref, dh_ref, dwg_ref, dwv_ref, sem, up_sems):
        t = pl.program_id(0)

        @pl.when(t == 0)
        def _():
            pltpu.sync_copy(h_hbm, h_ref)
            dh_ref[...] = jnp.zeros_like(dh_ref)

        h, dug_b, duv_b = h_ref[...], dug_ref[...], duv_ref[...]
        cols = pl.ds(pl.multiple_of((t % per) * tile, tile), tile)
        to_gate = pltpu.make_async_copy(dwg_ref, dup_hbm.at[t // per, :, cols], up_sems.at[0])
        to_value = pltpu.make_async_copy(dwv_ref, dup_hbm.at[N_DEV // 2 + t // per, :, cols], up_sems.at[1])
        dwg_ref[...] = _dot(h, dug_b, TN).astype(BF16)
        to_gate.start()
        dwv_ref[...] = _dot(h, duv_b, TN).astype(BF16)
        to_value.start()
        dh_ref[...] += _dot(jnp.concatenate([dug_b, duv_b], axis=1),
                            jnp.concatenate([wg_ref[0], wv_ref[0]], axis=1), NT)
        to_gate.wait()
        to_value.wait()

        @pl.when(t == D_FF // tile - 1)
        def _():
            cp = pltpu.make_async_copy(dh_ref, dh_hbm, sem)
            cp.start()
            cp.wait()

    hbm = pl.BlockSpec(memory_space=pl.ANY)
    w_g, w_v = _ffn_specs(tile)[:2]
    col = pl.BlockSpec((SEQ, tile), lambda t: (0, t))
    return pl.pallas_call(
        body, grid=(D_FF // tile,), name="ffn_bwd_up",
        in_specs=[hbm, w_g, w_v, col, col], out_specs=[hbm, hbm],
        out_shape=[_sds((SEQ, D_MODEL)), _sds((N_DEV, D_MODEL, D_MODEL), BF16)],
        scratch_shapes=[pltpu.VMEM((SEQ, D_MODEL), BF16), pltpu.VMEM((SEQ, D_MODEL), F32),
                        pltpu.VMEM((D_MODEL, tile), BF16), pltpu.VMEM((D_MODEL, tile), BF16),
                        pltpu.SemaphoreType.DMA, pltpu.SemaphoreType.DMA((2,))],
        compiler_params=_params(("arbitrary",)),
    )(h2, w_up, w_up, dug, duv)


def loss_head(x1, f, target, n_post):
    def tile_loss(x1_t, f_t, g, tgt):
        err = x1_t + _rms(f_t, g) - tgt
        return 0.5 * jnp.sum(jnp.mean(err * err, axis=-1))

    def body(x_ref, f_ref, t_ref, g_ref, dx_ref, df_ref, dg_ref, loss_ref):
        val, (dx, df, dg) = jax.value_and_grad(tile_loss, argnums=(0, 1, 2))(
            x_ref[...], f_ref[...], g_ref[...], t_ref[...])
        dx_ref[...] = dx
        df_ref[...] = df.astype(BF16)

        @pl.when(pl.program_id(0) == 0)
        def _():
            dg_ref[...] = jnp.zeros_like(dg_ref)
            loss_ref[...] = jnp.zeros_like(loss_ref)

        dg_ref[...] += dg
        loss_ref[...] += jnp.full((1, LANES), val, F32)

    tile = pl.BlockSpec((TOK_TILE, D_MODEL), lambda i: (i, 0))
    vec = pl.BlockSpec((1, D_MODEL), lambda i: (0, 0))
    return pl.pallas_call(
        body, grid=(SEQ // TOK_TILE,), name="loss_head",
        in_specs=[tile, tile, tile, vec],
        out_specs=[tile, tile, vec, pl.BlockSpec((1, LANES), lambda i: (0, 0))],
        out_shape=[_sds((SEQ, D_MODEL)), _sds((SEQ, D_MODEL), BF16), _sds((1, D_MODEL)), _sds((1, LANES))],
        compiler_params=_params(("arbitrary",)),
    )(x1, f, target, n_post)


def _mesh_pos():
    return lax.axis_index("x"), lax.axis_index("y"), lax.axis_index("c")


def _flip(pos, rel):
    x, y, c = pos
    return (1 - x if rel & 4 else x, 1 - y if rel & 2 else y, 1 - c if rel & 1 else c)


def _slot(pos):
    x, y, c = pos
    return 4 * x + 2 * y + c


def cast_bf16(w, rows):
    def body(w_ref, o_ref):
        o_ref[...] = w_ref[...].astype(BF16)

    spec = pl.BlockSpec((rows, w.shape[1]), lambda i: (i, 0))
    return pl.pallas_call(body, grid=(w.shape[0] // rows,), name="cast_bf16_%dx%d" % w.shape,
                          in_specs=[spec], out_specs=spec, out_shape=_sds(w.shape, BF16),
                          compiler_params=_params(("arbitrary",)))(w)


class CommPlan:
    def __init__(self, ins, out_shape, scratch, stages):
        self.ins, self.out_shape, self.scratch, self.stages = ins, out_shape, scratch, stages


def run_comm(name, plan):
    n_in, n_out = len(plan.ins), len(plan.out_shape)

    def body(*refs):
        for stage in plan.stages(refs[:n_in], refs[n_in:n_in + n_out], refs[n_in + n_out:]):
            stage()

    any_spec = pl.BlockSpec(memory_space=pl.ANY)
    return pl.pallas_call(
        body, name=name, in_specs=[any_spec] * len(plan.ins), out_specs=[any_spec] * len(plan.out_shape),
        out_shape=plan.out_shape, scratch_shapes=plan.scratch)(*plan.ins)


def gather_plan(shards):
    n = len(shards)

    def stages(srcs, outs, sems):
        send_sems, recv_sems, local_sems = sems

        def places():
            me = _mesh_pos()
            return me, _flip(me, 1), [_flip(me, 2), _flip(me, 4), _flip(me, 6)]

        def copy(a, k, block, to, src=None):
            dst = outs[a].at[_slot(block)]
            return pltpu.make_async_remote_copy(
                src_ref=dst if src is None else src, dst_ref=dst,
                send_sem=send_sems.at[7 * a + k], recv_sem=recv_sems.at[7 * a + k],
                device_id=to, device_id_type=pl.DeviceIdType.MESH)

        def local(a, me):
            return pltpu.make_async_copy(srcs[a], outs[a].at[_slot(me)], local_sems.at[a])

        def own(a, me, sibling, chips):
            return [copy(a, 0, me, sibling, src=srcs[a])] + [
                copy(a, 1 + j, me, chip, src=srcs[a]) for j, chip in enumerate(chips)]

        def start():
            me, sibling, chips = places()
            for a in range(n):
                local(a, me).start()
                for cp in own(a, me, sibling, chips):
                    cp.start()

        def forward():
            me, sibling, chips = places()
            for j, chip in enumerate(chips):
                for a in range(n):
                    copy(a, 1 + j, chip, me).wait_recv()
                    copy(a, 4 + j, chip, sibling).start()

        def finish():
            me, sibling, chips = places()
            for a in range(n):
                copy(a, 0, sibling, me).wait_recv()
                for j, chip in enumerate(chips):
                    copy(a, 4 + j, _flip(chip, 1), me).wait_recv()
            for a in range(n):
                for cp in own(a, me, sibling, chips):
                    cp.wait_send()
                for j, chip in enumerate(chips):
                    copy(a, 4 + j, chip, sibling).wait_send()
                local(a, me).wait()

        return start, forward, finish

    return CommPlan(list(shards), [_sds((N_DEV,) + s.shape, s.dtype) for s in shards],
                    [pltpu.SemaphoreType.DMA((7 * n,)), pltpu.SemaphoreType.DMA((7 * n,)),
                     pltpu.SemaphoreType.DMA((n,))], stages)


def exchange_plan(parts, replicated, rels, members, index, member_axis, own_copy):
    n, nr = len(parts), len(rels)
    pick_index = (slice(None),) * member_axis + (0,)
    subs = [1 if (r or member_axis == 0) else p.shape[0] for p, r in zip(parts, replicated)]
    first = [sum(subs[:a]) for a in range(n)]
    total = sum(subs)

    def stages(srcs, outs, sems):
        send_sems, recv_sems, local_sems = sems

        def src(a, s, pos):
            if replicated[a]:
                return srcs[a]
            return srcs[a].at[index(pos)] if member_axis == 0 else srcs[a].at[s, index(pos)]

        def dst(a, s, pos):
            block = outs[a].at[index(pos)]
            return block if (replicated[a] or member_axis == 0) else block.at[s]

        def copy(a, s, j, me, src_pos, dst_pos):
            sem = nr * (first[a] + s) + j
            return pltpu.make_async_remote_copy(
                src_ref=src(a, s, src_pos), dst_ref=dst(a, s, dst_pos),
                send_sem=send_sems.at[sem], recv_sem=recv_sems.at[sem],
                device_id=_flip(me, rels[j]), device_id_type=pl.DeviceIdType.MESH)

        pieces = [(a, s) for a in range(n) for s in range(subs[a])]

        def local(a, s, me):
            return pltpu.make_async_copy(src(a, s, me), dst(a, s, me), local_sems.at[first[a] + s])

        def sends(me):
            return [copy(a, s, j, me, _flip(me, rels[j]), me) for j in range(nr) for a, s in pieces]

        own = pieces if own_copy else []

        def start():
            me = _mesh_pos()
            for cp in sends(me) + [local(a, s, me) for a, s in own]:
                cp.start()

        def middle():
            pass

        def finish():
            me = _mesh_pos()
            for j in range(nr):
                for a, s in pieces:
                    copy(a, s, j, me, me, _flip(me, rels[j])).wait_recv()
            for cp in sends(me):
                cp.wait_send()
            for a, s in own:
                local(a, s, me).wait()

        return start, middle, finish

    shapes = [p.shape if r else jax.eval_shape(lambda t: t[pick_index], p).shape for p, r in zip(parts, replicated)]
    return CommPlan(list(parts), [_sds((members,) + s, p.dtype) for s, p in zip(shapes, parts)],
                    [pltpu.SemaphoreType.DMA((nr * total,)), pltpu.SemaphoreType.DMA((nr * total,)),
                     pltpu.SemaphoreType.DMA((total,))], stages)


def pair_plan(parts, replicated):
    return exchange_plan(parts, replicated, [1], 2, lambda pos: pos[2], 1, False)


def chip_plan(parts, replicated):
    return exchange_plan(parts, replicated, [2, 4, 6], 4, lambda pos: 2 * pos[0] + pos[1], 0, True)


def add_pair(name, mine, swapped, out_dtype, rows):
    def body(m_ref, s_ref, o_ref):
        own = m_ref[0, 0] if mine.ndim == 4 else m_ref[0]
        o_ref[0] = (own.astype(F32) + s_ref[0, 0].astype(F32)).astype(o_ref.dtype)

    _, n, r, c = swapped.shape
    core = lambda: lax.axis_index("c")
    if mine.ndim == 4:
        mine_spec = pl.BlockSpec((1, 1, rows, c), lambda i, j: (i, core(), j, 0))
    else:
        mine_spec = pl.BlockSpec((1, rows, c), lambda i, j: (i, j, 0))
    return pl.pallas_call(
        body, grid=(n, r // rows), name=name,
        in_specs=[mine_spec, pl.BlockSpec((1, 1, rows, c), lambda i, j: (1 - core(), i, j, 0))],
        out_specs=pl.BlockSpec((1, rows, c), lambda i, j: (i, j, 0)),
        out_shape=_sds((n, r, c), out_dtype),
        compiler_params=_params(("arbitrary", "arbitrary")),
    )(mine, swapped)


def add_pair_small(mines, swappeds):
    n = len(mines)
    halves = [m.ndim == s.ndim for m, s in zip(mines, swappeds)]

    def body(*refs):
        c = lax.axis_index("c")
        for i in range(n):
            m_ref, s_ref, o_ref = refs[i], refs[n + i], refs[2 * n + i]
            o_ref[...] = (m_ref[:, c] if halves[i] else m_ref[...]) + s_ref[1 - c]

    vmem = pl.BlockSpec(memory_space=pltpu.VMEM)
    return pl.pallas_call(
        body, name="pair_add_small", in_specs=[vmem] * (2 * n), out_specs=[vmem] * n,
        out_shape=[_sds(s.shape[1:]) for s in swappeds], compiler_params=_params(),
    )(*mines, *swappeds)


def _adamw_math(w, g, m, v):
    nm = ADAM_B1 * m + (1.0 - ADAM_B1) * g
    nv = ADAM_B2 * v + (1.0 - ADAM_B2) * (g * g)
    m_hat = nm / (1.0 - ADAM_B1 ** ADAM_STEP)
    v_hat = nv / (1.0 - ADAM_B2 ** ADAM_STEP)
    return -ADAM_LR * (m_hat / (jnp.sqrt(v_hat) + ADAM_EPS) + ADAM_WD * w), nm, nv


def adamw_small(ws, parts, ms, vs):
    n = len(ws)

    def body(*refs):
        for i in range(n):
            w_ref, p_ref, m_ref, v_ref = (refs[k * n + i] for k in range(4))
            g = p_ref[0]
            for j in range(1, p_ref.shape[0]):
                g = g + p_ref[j]
            delta, nm, nv = _adamw_math(w_ref[...], g, m_ref[...], v_ref[...])
            for k, val in enumerate((g, delta, nm, nv)):
                refs[(4 + k) * n + i][...] = val

    vmem = pl.BlockSpec(memory_space=pltpu.VMEM)
    outs = pl.pallas_call(
        body, name="adamw_small", in_specs=[vmem] * (4 * n), out_specs=[vmem] * (4 * n),
        out_shape=[_sds(w.shape) for w in ws] * 4, compiler_params=_params(),
    )(*ws, *parts, *ms, *vs)
    return [outs[k * n:(k + 1) * n] for k in range(4)]

def adamw(name, w, parts, m, v, rows, plan=None):
    n_parts = parts.shape[0]

    def body(w_ref, p_ref, m_ref, v_ref, g_ref, d_ref, nm_ref, nv_ref):
        g = p_ref[0].astype(F32)
        for j in range(1, n_parts):
            g = g + p_ref[j].astype(F32)
        g_ref[...] = g
        d_ref[...], nm_ref[...], nv_ref[...] = _adamw_math(w_ref[...], g, m_ref[...], v_ref[...])

    cols = w.shape[1]
    spec = pl.BlockSpec((rows, cols), lambda i: (i, 0))
    grid = (w.shape[0] // rows,)
    in_specs = [spec, pl.BlockSpec((n_parts, rows, cols), lambda i: (0, i, 0)), spec, spec]
    if plan is not None:
        return call_with_comm(plan, 0, body, grid, name, in_specs, [spec] * 4, [_sds(w.shape)] * 4, [],
                              (w, parts, m, v))
    return pl.pallas_call(
        body, grid=grid, name=name, in_specs=in_specs, out_specs=[spec] * 4, out_shape=[_sds(w.shape)] * 4,
        compiler_params=_params(("arbitrary",)),
    )(w, parts, m, v)


def _to_slots(full, per):
    return full.reshape(full.shape[0], N_DEV, per).transpose(1, 0, 2)


def _from_slots(slots):
    return slots.transpose(1, 0, 2).reshape(slots.shape[1], -1)


def kernel(x, norm_mix_pre, norm_mix_post, norm_ffn_pre, norm_ffn_post, w_in, rel_bias, sinks, rwkv_shift_mix, w0, w_decay_up, a0, w_iclr_up, w_gate_up, k_k, k_a, r_k, ln_x_g, ln_x_b, w_out, w_ffn_up, conv_w, conv_b, w_ffn_down, loss_target, m_norm_mix_pre, m_norm_mix_post, m_norm_ffn_pre, m_norm_ffn_post, m_w_in, m_rel_bias, m_sinks, m_rwkv_shift_mix, m_w0, m_w_decay_up, m_a0, m_w_iclr_up, m_w_gate_up, m_k_k, m_k_a, m_r_k, m_ln_x_g, m_ln_x_b, m_w_out, m_w_ffn_up, m_conv_w, m_conv_b, m_w_ffn_down, v_norm_mix_pre, v_norm_mix_post, v_norm_ffn_pre, v_norm_ffn_post, v_w_in, v_rel_bias, v_sinks, v_rwkv_shift_mix, v_w0, v_w_decay_up, v_a0, v_w_iclr_up, v_w_gate_up, v_k_k, v_k_a, v_r_k, v_ln_x_g, v_ln_x_b, v_w_out, v_w_ffn_up, v_conv_w, v_conv_b, v_w_ffn_down):
    x2 = x[0]
    target = loss_target[0]

    in_gather = gather_plan([cast_bf16(w_in[0], 256)])
    mixer_gather = gather_plan([cast_bf16(w_out[0], 128), w_decay_up[0], w_iclr_up[0], w_gate_up[0], conv_w[0]])
    ffn_gather = gather_plan([cast_bf16(w_ffn_up[0], 256), cast_bf16(w_ffn_down[0], 256)])
    mix_ext = jnp.concatenate([jnp.zeros((1, D_QKV), F32), rwkv_shift_mix], axis=1)
    r_k_row = r_k.reshape(1, D_RWKV)
    bucket = _bucket_table()

    (h1,), (g_in,) = tok_fwd("rms_mix_pre", rms_tile, [x2], [norm_mix_pre], [], [D_MODEL], [BF16], plan=in_gather)
    w_in_b = _from_slots(g_in)
    (proj, ps), (g_out, g_decay, g_iclr, g_gate, g_conv) = in_proj_fwd(h1, w_in_b, mix_ext, mixer_gather)
    w_out_b = g_out.reshape(D_MODEL, D_MODEL)
    lora = jnp.zeros((HEAD_DIM, D_RWKV), F32)
    wd_pad = jnp.concatenate([_from_slots(g_decay), lora], axis=0)
    wi_pad = jnp.concatenate([lora, _from_slots(g_iclr)], axis=0)
    wg_full = _from_slots(g_gate)
    pre_params = [w0, wd_pad, a0, wi_pad, wg_full, k_k, k_a]
    r_, lw_, k2_, v_, kk_, a_, gate_ = tok_fwd("rwkv_pre", rwkv_pre_tile, [ps], pre_params, [],
                                               [D_RWKV] * 7, [F32] * 7)
    (attn, o_, states), (g_up, g_down) = mixer_fwd(proj, rel_bias, bucket, sinks, r_, lw_, k2_, v_, kk_, a_,
                                                   ffn_gather)
    w_down_b = g_down.reshape(D_FF, D_MODEL)
    mix_tiles = [o_, r_, k2_, v_, gate_, attn, x2]
    mix_params = [w_out_b, norm_mix_post, ln_x_g, ln_x_b, r_k_row, norm_ffn_pre]
    x1, h2 = tok_fwd("mix_out", mix_out_tile, mix_tiles, mix_params, [(D_MODEL, D_MODEL)], [D_MODEL, D_MODEL],
                     [F32, BF16])
    f = ffn_fwd(h2, g_up, g_conv, conv_b, w_down_b)
    dy, df, d_n_ffn_post, loss_row = loss_head(x1, f, target, norm_ffn_post)

    d_ug, d_uv, d_cw_g, d_cw_v, d_cb_g, d_cb_v, d_down = ffn_bwd_mid(h2, g_up, g_conv, conv_b, w_down_b, df)
    dh2, d_up = ffn_bwd_up(h2, g_up, d_ug, d_uv)
    half = N_DEV // 2
    d_cw = jnp.concatenate([d_cw_g[:half], d_cw_v[half:]], axis=0)
    by_pair = lambda slots: slots.reshape((N_DEV // 2, 2) + slots.shape[1:])
    ffn_mine = [by_pair(d_up), by_pair(d_down.reshape(N_DEV, D_FF // N_DEV, D_MODEL))]
    d_cb = jnp.concatenate([d_cb_g[:, :D_FF], d_cb_v[:, D_FF:]], axis=1)
    ((d_o, d_r1, d_k1, d_v1, d_gate, d_attn, dx_res, d_n_mix_post, d_ln_g, d_ln_b, d_r_k, d_n_ffn_pre, d_w_out),
     ffn_swapped) = tok_bwd("mix_out_bwd", mix_out_tile, mix_tiles, mix_params, [(D_MODEL, D_MODEL)], [dy, dh2],
                            [1, 2, 3, 4, 5], plan=pair_plan(ffn_mine, [False, False]))
    ffn_exchange = chip_plan([add_pair("pair_add_w_ffn_up", ffn_mine[0], ffn_swapped[0], BF16, 256),
                              add_pair("pair_add_w_ffn_down", ffn_mine[1], ffn_swapped[1], BF16, 256)],
                             [False, False])
    (d_r2, d_lw, d_k2, d_v2, d_kk, d_a), (got_up, got_down) = rwkv_scan_bwd(
        r_, lw_, k2_, v_, kk_, a_, states, d_o, ffn_exchange)
    pre_cots = [(d_r1, d_r2), d_lw, (d_k1, d_k2), (d_v1, d_v2), d_kk, d_a, d_gate]
    (d_ps, d_w0, d_wd_pad, d_a0, d_wi_pad, d_wg, d_k_k, d_k_a) = tok_bwd(
        "rwkv_pre_bwd", rwkv_pre_tile, [ps], pre_params, [], pre_cots, [0, 1, 2, 3, 4, 5, 6])
    dqkv, d_rel_bias, d_sinks = attn_bwd(proj, rel_bias, bucket, sinks, d_attn)
    dh1, d_w_in, d_mix_ext = in_proj_bwd(h1, w_in_b, mix_ext, proj, dqkv, d_ps)
    grad_x2, d_n_mix_pre = tok_bwd("rms_mix_pre_bwd", rms_tile, [x2], [norm_mix_pre], [], [dh1], [0], {0: dx_res})
    grad_x = grad_x2[None]

    small_rep = [d_n_mix_pre, d_n_mix_post, d_n_ffn_pre, d_n_ffn_post, d_rel_bias, d_sinks,
                 d_mix_ext[:, D_QKV:], d_w0, d_a0, d_k_k, d_k_a, d_r_k.reshape(r_k.shape), d_ln_g, d_ln_b, d_cb]
    rep_w = [norm_mix_pre, norm_mix_post, norm_ffn_pre, norm_ffn_post, rel_bias, sinks, rwkv_shift_mix,
             w0, a0, k_k, k_a, r_k, ln_x_g, ln_x_b, conv_b]
    rep_m = [m_norm_mix_pre, m_norm_mix_post, m_norm_ffn_pre, m_norm_ffn_post, m_rel_bias, m_sinks,
             m_rwkv_shift_mix, m_w0, m_a0, m_k_k, m_k_a, m_r_k, m_ln_x_g, m_ln_x_b, m_conv_b]
    rep_v = [v_norm_mix_pre, v_norm_mix_post, v_norm_ffn_pre, v_norm_ffn_post, v_rel_bias, v_sinks,
             v_rwkv_shift_mix, v_w0, v_a0, v_k_k, v_k_a, v_r_k, v_ln_x_g, v_ln_x_b, v_conv_b]
    sh_w = [w_decay_up, w_iclr_up, w_gate_up, conv_w]
    sh_m = [m_w_decay_up, m_w_iclr_up, m_w_gate_up, m_conv_w]
    sh_v = [v_w_decay_up, v_w_iclr_up, v_w_gate_up, v_conv_w]
    sh_parts = [_to_slots(d_wd_pad[:HEAD_DIM], HEAD_DIM), _to_slots(d_wi_pad[HEAD_DIM:], HEAD_DIM),
                _to_slots(d_wg, HEAD_DIM), d_cw]
    n_rep, n_sh = len(small_rep), len(sh_parts)
    mine = [by_pair(_to_slots(d_w_in, D_IN // N_DEV)), by_pair(d_w_out.reshape(N_DEV, D_MODEL // N_DEV, D_MODEL)),
            *small_rep, *(by_pair(p) for p in sh_parts), loss_row]
    is_rep = [False, False] + [True] * n_rep + [False] * n_sh + [True]
    adam_down, swapped = adamw("adamw_w_ffn_down", w_ffn_down[0], got_down, m_w_ffn_down[0], v_w_ffn_down[0], 128,
                               pair_plan(mine, is_rep))
    chip_sums = [add_pair("pair_add_w_in", mine[0], swapped[0], BF16, 512),
                 add_pair("pair_add_w_out", mine[1], swapped[1], BF16, 128),
                 *add_pair_small(mine[2:], swapped[2:])]
    adam_up, got = adamw("adamw_w_ffn_up", w_ffn_up[0], got_up, m_w_ffn_up[0], v_w_ffn_up[0], 128,
                         chip_plan(chip_sums, is_rep))

    big = [adamw("adamw_w_in", w_in[0], got[0], m_w_in[0], v_w_in[0], 256),
           adamw("adamw_w_out", w_out[0], got[1], m_w_out[0], v_w_out[0], 128), adam_up, adam_down]
    loss = functools.reduce(jnp.add, [got[-1][q, 0, 0] for q in range(N_DEV // 2)])
    small_w, small_g = rep_w + sh_w, got[2:-1]
    as_grad = lambda arrays: [a.reshape(g.shape[1:]) for a, g in zip(arrays, small_g)]
    small = adamw_small(as_grad(small_w), small_g, as_grad(rep_m + sh_m), as_grad(rep_v + sh_v))
    small = [[a.reshape(w.shape) for a, w in zip(kind, small_w)] for kind in small]

    names = ["norm_mix_pre", "norm_mix_post", "norm_ffn_pre", "norm_ffn_post", "w_in", "rel_bias", "sinks",
             "rwkv_shift_mix", "w0", "w_decay_up", "a0", "w_iclr_up", "w_gate_up", "k_k", "k_a", "r_k",
             "ln_x_g", "ln_x_b", "w_out", "w_ffn_up", "conv_w", "conv_b", "w_ffn_down"]
    small_names = ["norm_mix_pre", "norm_mix_post", "norm_ffn_pre", "norm_ffn_post", "rel_bias", "sinks",
                   "rwkv_shift_mix", "w0", "a0", "k_k", "k_a", "r_k", "ln_x_g", "ln_x_b", "conv_b",
                   "w_decay_up", "w_iclr_up", "w_gate_up", "conv_w"]
    big_names = {"w_in": 0, "w_out": 1, "w_ffn_up": 2, "w_ffn_down": 3}
    outs = []
    for kind in range(4):
        for nm in names:
            if nm in big_names:
                outs.append(big[big_names[nm]][kind][None])
            else:
                outs.append(small[kind][small_names.index(nm)])
    return (loss, grad_x, *outs)
```
